```python
import jax, jax.numpy as jnp
from jax import lax
import numpy as np

D_MODEL = 1024
BATCH = 16
SEQ = 256
DEPTH = 2
DEC_BATCH = 8
DEC_SEQ = 2048
PAST_LEN = 512

GRID_W = 64
N_EVEN = (DEPTH + 1) // 2
N_ODD = DEPTH // 2
D_FF = 4 * D_MODEL
NORM_EPS = 1e-6
N_MOD = 6

D_POOL = D_MODEL // 2
POOL_WINDOWS = (2, 4, 8, 16)
POOL_GROUPS = len(POOL_WINDOWS)
POOL_GW = D_POOL // POOL_GROUPS
D_RG = D_MODEL // 2
RG_BLOCKS = 8
RG_BW = D_RG // RG_BLOCKS
RG_C = 8.0
CONV_W = 4
CONV_LEFT = 2
EV_COLS = D_POOL + 2 * D_RG

D_RW = D_MODEL // 2
RW_HEAD = 64
RW_HEADS = D_RW // RW_HEAD
RW_W_LORA = 64
RW_A_LORA = 64
RW_G_LORA = 128
RW_COLS = 3 * D_RW + RW_G_LORA + RW_W_LORA + RW_A_LORA
RW_LN_EPS = 64e-5
D_ML = D_MODEL // 2
ML_HEADS = 4
ML_DK = D_ML // ML_HEADS
ML_CHUNK = 64
ML_COLS = 4 * D_ML + 4 * ML_HEADS
OD_COLS = RW_COLS + ML_COLS

kernel_name = 'hybrid_pool_rglru_rwkv7_mlstm_diffusion_step'

F32 = jnp.float32


def rms_norm(x, w):
    xf = x.astype(F32)
    y = xf * lax.rsqrt(jnp.mean(xf * xf, axis=-1, keepdims=True) + NORM_EPS)
    return (y * w.astype(F32)).astype(x.dtype)


def head_layer_norm(x, w, b, n_heads, eps):
    B, T, C = x.shape
    xf = x.astype(F32).reshape(B, T, n_heads, C // n_heads)
    mu = jnp.mean(xf, axis=-1, keepdims=True)
    var = jnp.mean(jnp.square(xf - mu), axis=-1, keepdims=True)
    y = ((xf - mu) * lax.rsqrt(var + eps)).reshape(B, T, C) * w
    if b is not None:
        y = y + b
    return y


def _dir(t, d):
    return t if d == 0 else jnp.flip(t, axis=1)


def centred_mean(x, w, axis):
    L = x.shape[axis]
    lo_off = w // 2
    hi_off = w - 1 - lo_off
    t = jnp.arange(L)
    lo = jnp.maximum(t - lo_off, 0)
    hi = jnp.minimum(t + hi_off, L - 1)
    zshape = list(x.shape)
    zshape[axis] = 1
    cs = jnp.concatenate([jnp.zeros(zshape, x.dtype), jnp.cumsum(x, axis=axis)], axis=axis)
    s = jnp.take(cs, hi + 1, axis=axis) - jnp.take(cs, lo, axis=axis)
    cshape = [1] * x.ndim
    cshape[axis] = L
    return s / (hi - lo + 1).astype(x.dtype).reshape(cshape)


def pool_mixer(u, pool_w, pool_scale, grid):
    B, T, _ = u.shape
    ug = u.astype(F32).reshape(B, T, POOL_GROUPS, POOL_GW)
    outs = []
    for g, w in enumerate(POOL_WINDOWS):
        xg = ug[:, :, g]
        if grid:
            rows = T // GRID_W
            x2 = xg.reshape(B, rows, GRID_W, POOL_GW)
            m = centred_mean(centred_mean(x2, w, 1), w, 2).reshape(B, T, POOL_GW)
        else:
            m = centred_mean(xg, w, 1)
        outs.append(m - xg)
    d = jnp.stack(outs, axis=2).astype(u.dtype)
    y = jnp.einsum('btgc,gcd->btgd', d, pool_w).reshape(B, T, D_POOL)
    return y * pool_scale


def dw_conv(x, w, b):
    T = x.shape[1]
    xp = jnp.pad(x, ((0, 0), (CONV_LEFT, CONV_W - 1 - CONV_LEFT), (0, 0)))
    y = b
    for j in range(CONV_W):
        y = y + xp[:, j:j + T] * w[j]
    return y


def linear_scan(a, b, h0):
    b = b.at[:, 0].add(a[:, 0] * h0)

    def combine(left, right):
        a_l, b_l = left
        a_r, b_r = right
        return a_l * a_r, a_r * b_l + b_r

    _, h = lax.associative_scan(combine, (a, b), axis=1)
    return h


def rglru_dir(x, wa, ba, wx, bx, lam, h0):
    B, T, _ = x.shape
    xb = x.reshape(B, T, RG_BLOCKS, RG_BW)
    r = jax.nn.sigmoid(jnp.einsum('btnc,ncd->btnd', xb, wa.astype(F32)).reshape(B, T, D_RG) + ba)
    i = jax.nn.sigmoid(jnp.einsum('btnc,ncd->btnd', xb, wx.astype(F32)).reshape(B, T, D_RG) + bx)
    log_a = -RG_C * r * jax.nn.softplus(-lam.astype(F32))
    a = jnp.exp(log_a)
    b = jnp.sqrt(-jnp.expm1(2.0 * log_a)) * (i * x)
    return linear_scan(a, b, h0)


def even_mixer(h, P, j, grid, rg_state):
    z = h @ P['ev_w_in'][j]
    u_pool, u_rg, u_gate = jnp.split(z, [D_POOL, D_POOL + D_RG], axis=-1)
    y_pool = pool_mixer(u_pool, P['pool_w'][j], P['pool_scale'][j], grid)
    xc = dw_conv(u_rg, P['rg_conv_w'][j], P['rg_conv_b'][j]).astype(F32)
    h_sum = 0.0
    finals = []
    for d in range(2):
        hd = rglru_dir(_dir(xc, d), P['rg_wa'][j, d], P['rg_ba'][j, d], P['rg_wx'][j, d],
                       P['rg_bx'][j, d], P['rg_lam'][j, d], rg_state[:, d].astype(F32))
        finals.append(hd[:, -1])
        h_sum = h_sum + _dir(hd, d)
    y_rg = h_sum * jax.nn.gelu(u_gate.astype(F32))
    y = jnp.concatenate([y_pool.astype(h.dtype), y_rg.astype(h.dtype)], axis=-1) @ P['ev_w_out'][j]
    return y, jnp.stack(finals, axis=1)


def token_shift(z, mu):
    prev = jnp.pad(z[:, :-1], ((0, 0), (1, 0), (0, 0)))
    nxt = jnp.pad(z[:, 1:], ((0, 0), (0, 1), (0, 0)))
    return z + mu[0] * (prev - z) + mu[1] * (nxt - z)


def rwkv7_scan(r, w, k, v, kk, a, S0):
    def step(S, inp):
        r_t, w_t, k_t, v_t, kk_t, a_t = inp
        sa = jnp.einsum('bhvk,bhk->bhv', S, -kk_t)
        S = (S * w_t[:, :, None, :] + sa[..., None] * (kk_t * a_t)[:, :, None, :]
             + v_t[..., None] * k_t[:, :, None, :])
        y = jnp.einsum('bhvk,bhk->bhv', S, r_t)
        return S, y

    xs = tuple(jnp.moveaxis(t, 1, 0) for t in (r, w, k, v, kk, a))
    S, ys = lax.scan(step, S0, xs)
    return jnp.moveaxis(ys, 0, 1), S


def rwkv7_mixer(z, P, j, S_init):
    B, T, _ = z.shape
    H, N = RW_HEADS, RW_HEAD
    zs = token_shift(z, P['rw_mu'][j]).astype(F32)
    cuts = [D_RW, 2 * D_RW, 3 * D_RW, 3 * D_RW + RW_G_LORA, 3 * D_RW + RW_G_LORA + RW_W_LORA]
    r, k, v, gd, wd, ad = jnp.split(zs, cuts, axis=-1)

    def heads(t):
        return t.reshape(B, T, H, N)

    kk = heads(k * P['rw_kk'][j])
    kk = kk / jnp.maximum(jnp.sqrt(jnp.sum(kk * kk, axis=-1, keepdims=True)), 1e-12)
    g = jax.nn.sigmoid(gd) @ P['rw_g2'][j]
    rh, vh = heads(r), heads(v)
    y_sum = 0.0
    bonus = 0.0
    finals = []
    for d in range(2):
        w_log = -jax.nn.softplus(-(P['rw_w0'][j, d] + jnp.tanh(wd) @ P['rw_w2'][j, d])) - 0.5
        decay = heads(jnp.exp(-jnp.exp(w_log)))
        a = jax.nn.sigmoid(P['rw_a0'][j, d] + ad @ P['rw_a2'][j, d])
        kd = heads(k * (1.0 + (a - 1.0) * P['rw_ka'][j]))
        yd, Sd = rwkv7_scan(_dir(rh, d), _dir(decay, d), _dir(kd, d), _dir(vh, d), _dir(kk, d),
                            _dir(heads(a), d), S_init[:, d].astype(F32))
        y_sum = y_sum + _dir(yd, d)
        bonus = bonus + jnp.sum(rh * kd * P['rw_rk'][j], axis=-1, keepdims=True) * vh
        finals.append(Sd)
    y = head_layer_norm(y_sum.reshape(B, T, D_RW), P['rw_ln_w'][j], P['rw_ln_b'][j], H, RW_LN_EPS)
    y = y + bonus.reshape(B, T, D_RW)
    return y * g, jnp.stack(finals, axis=1)


def mlstm_chunkwise(q, k, v, li, lf, C, n, m):
    B, T, H, DK = q.shape
    L = ML_CHUNK
    NC = T // L

    def chunks(t):
        return jnp.moveaxis(t.reshape((B, NC, L) + t.shape[2:]), 1, 0)

    causal = jnp.tril(jnp.ones((L, L), dtype=bool))[None, :, :, None]

    def step(carry, inp):
        C, n, m = carry
        qc, kc, vc, lic, lfc = inp
        b = jnp.cumsum(lfc, axis=1)
        log_d = jnp.where(causal, b[:, :, None, :] - b[:, None, :, :] + lic[:, None, :, :], -jnp.inf)
        log_inter = b + m[:, None, :]
        m_t = jnp.maximum(log_inter, jnp.max(log_d, axis=2))
        s = jnp.einsum('bthd,bshd->btsh', qc, kc) * jnp.exp(log_d - m_t[:, :, None, :])
        w_inter = jnp.exp(log_inter - m_t)
        num = (jnp.einsum('btsh,bshd->bthd', s, vc)
               + w_inter[..., None] * jnp.einsum('bthk,bhkv->bthv', qc, C))
        den = jnp.sum(s, axis=2) + w_inter * jnp.einsum('bthk,bhk->bth', qc, n)
        h = num / jnp.maximum(jnp.abs(den), jnp.exp(-m_t))[..., None]
        m_new = m_t[:, -1]
        w_s = jnp.exp(b[:, -1:] - b + lic - m_new[:, None])
        dec = jnp.exp(b[:, -1] + m - m_new)
        C = dec[..., None, None] * C + jnp.einsum('bsh,bshk,bshv->bhkv', w_s, kc, vc)
        n = dec[..., None] * n + jnp.einsum('bsh,bshk->bhk', w_s, kc)
        return (C, n, m_new), h

    (C, n, m), hs = lax.scan(step, (C, n, m), tuple(chunks(t) for t in (q, k, v, li, lf)))
    h = jnp.moveaxis(hs, 0, 1).reshape(B, T, H, DK)
    return h, (C, n, m)


def mlstm_mixer(z, P, j, C0, n0, m0):
    B, T, _ = z.shape
    qk, v, o, gates = jnp.split(z, [2 * D_ML, 3 * D_ML, 4 * D_ML], axis=-1)
    qk = jax.nn.silu(dw_conv(qk, P['ml_conv_w'][j], P['ml_conv_b'][j])).astype(F32)
    q, k = jnp.split(qk, 2, axis=-1)

    def heads(t):
        return t.reshape(B, T, ML_HEADS, ML_DK)

    q = heads(q) * (ML_DK ** -0.5)
    k = heads(k)
    vh = heads(v.astype(F32))
    gates = gates.astype(F32).reshape(B, T, 2, 2, ML_HEADS)
    h_sum = 0.0
    Cs, ns, ms = [], [], []
    for d in range(2):
        li = gates[:, :, d, 0] + P['ml_bi'][j, d]
        lf = jax.nn.log_sigmoid(gates[:, :, d, 1] + P['ml_bf'][j, d])
        hd, (Cd, nd, md) = mlstm_chunkwise(_dir(q, d), _dir(k, d), _dir(vh, d), _dir(li, d), _dir(lf, d),
                                           C0[:, d].astype(F32), n0[:, d].astype(F32), m0[:, d].astype(F32))
        h_sum = h_sum + _dir(hd, d)
        Cs.append(Cd)
        ns.append(nd)
        ms.append(md)
    h = head_layer_norm(h_sum.reshape(B, T, D_ML), P['ml_norm_w'][j], None, ML_HEADS, NORM_EPS)
    y = h * jax.nn.sigmoid(o.astype(F32))
    return y, (jnp.stack(Cs, axis=1), jnp.stack(ns, axis=1), jnp.stack(ms, axis=1))


def odd_mixer(h, P, j, S0, C0, n0, m0):
    z = h @ P['od_w_in'][j]
    zr, zm = jnp.split(z, [RW_COLS], axis=-1)
    y_r, S = rwkv7_mixer(zr, P, j, S0)
    y_m, (C, n, m) = mlstm_mixer(zm, P, j, C0, n0, m0)
    y = jnp.concatenate([y_r, y_m], axis=-1).astype(h.dtype) @ P['od_w_out'][j]
    return y, (S, C, n, m)


def trunk(x, cond, grid, rg0, rw0, C0, n0, m0, P):
    rg_f, rw_f, C_f, n_f, m_f = [], [], [], [], []
    for l in range(DEPTH):
        j = l // 2
        mod = (jax.nn.silu(cond) @ P['w_mod'][l] + P['b_mod'][l])[:, None, :]
        sh1, sc1, g1, sh2, sc2, g2 = jnp.split(mod, N_MOD, axis=-1)
        h = rms_norm(x, P['norm1_w'][l]) * (1.0 + sc1) + sh1
        if l % 2 == 0:
            y, rg = even_mixer(h, P, j, grid, rg0[:, j])
            rg_f.append(rg)
        else:
            y, (S, C, n, m) = odd_mixer(h, P, j, rw0[:, j], C0[:, j], n0[:, j], m0[:, j])
            rw_f.append(S)
            C_f.append(C)
            n_f.append(n)
            m_f.append(m)
        x = x + g1 * y
        h = rms_norm(x, P['norm2_w'][l]) * (1.0 + sc2) + sh2
        x = x + g2 * (jnp.square(jax.nn.relu(h @ P['mlp_w1'][l])) @ P['mlp_w2'][l])
    y = rms_norm(x, P['final_norm_w'])
    return y, (jnp.stack(rg_f, axis=1), jnp.stack(rw_f, axis=1), jnp.stack(C_f, axis=1),
               jnp.stack(n_f, axis=1), jnp.stack(m_f, axis=1))


def setup_inputs(seed: int = 0) -> dict:
    key = jax.random.key(seed)
    ks = list(jax.random.split(key, 64))

    def normal(shape, scale):
        return jax.random.normal(ks.pop(), shape, F32) * scale

    def unif(shape, lo, hi):
        return jax.random.uniform(ks.pop(), shape, F32, lo, hi)

    D = D_MODEL
    lam_a = unif((N_EVEN, 2, D_RG), 0.9, 0.999)
    return {
        'x_prompt': normal((BATCH, SEQ, D), 1.0),
        'x_sample': normal((DEC_BATCH, DEC_SEQ, D), 1.0),
        'state_rglru': normal((DEC_BATCH, N_EVEN, 2, D_RG), 0.5),
        'state_rwkv': normal((DEC_BATCH, N_ODD, 2, RW_HEADS, RW_HEAD, RW_HEAD), 0.2),
        'state_mlstm_C': normal((DEC_BATCH, N_ODD, 2, ML_HEADS, ML_DK, ML_DK), 0.5),
        'state_mlstm_n': normal((DEC_BATCH, N_ODD, 2, ML_HEADS, ML_DK), 0.5),
        'state_mlstm_m': unif((DEC_BATCH, N_ODD, 2, ML_HEADS), 0.0, 3.0),
        'c': normal((DEC_BATCH, D), 1.0),
        'c_ctx': normal((D,), 1.0),
        'norm1_w': 1.0 + normal((DEPTH, D), 0.05),
        'norm2_w': 1.0 + normal((DEPTH, D), 0.05),
        'w_mod': normal((DEPTH, D, N_MOD * D), 0.5 * D ** -0.5),
        'b_mod': normal((DEPTH, N_MOD * D), 0.02),
        'mlp_w1': normal((DEPTH, D, D_FF), D ** -0.5),
        'mlp_w2': normal((DEPTH, D_FF, D), D_FF ** -0.5),
        'final_norm_w': 1.0 + normal((D,), 0.05),
        'ev_w_in': normal((N_EVEN, D, EV_COLS), D ** -0.5),
        'pool_w': normal((N_EVEN, POOL_GROUPS, POOL_GW, POOL_GW), POOL_GW ** -0.5),
        'pool_scale': 1.0 + normal((N_EVEN, D_POOL), 0.1),
        'rg_conv_w': normal((N_EVEN, CONV_W, D_RG), CONV_W ** -0.5),
        'rg_conv_b': normal((N_EVEN, D_RG), 0.02),
        'rg_wa': normal((N_EVEN, 2, RG_BLOCKS, RG_BW, RG_BW), RG_BW ** -0.5),
        'rg_ba': normal((N_EVEN, 2, D_RG), 0.1),
        'rg_wx': normal((N_EVEN, 2, RG_BLOCKS, RG_BW, RG_BW), RG_BW ** -0.5),
        'rg_bx': normal((N_EVEN, 2, D_RG), 0.1),
        'rg_lam': jnp.log(lam_a) - jnp.log1p(-lam_a),
        'ev_w_out': normal((N_EVEN, D_POOL + D_RG, D), (D_POOL + D_RG) ** -0.5),
        'od_w_in': normal((N_ODD, D, OD_COLS), D ** -0.5),
        'rw_mu': unif((N_ODD, 2, RW_COLS), 0.0, 0.5),
        'rw_w0': jnp.linspace(-6.5, -1.0, D_RW, dtype=F32) + normal((N_ODD, 2, D_RW), 0.1),
        'rw_w2': normal((N_ODD, 2, RW_W_LORA, D_RW), 0.5 * RW_W_LORA ** -0.5),
        'rw_a0': normal((N_ODD, 2, D_RW), 0.1),
        'rw_a2': normal((N_ODD, 2, RW_A_LORA, D_RW), 0.5 * RW_A_LORA ** -0.5),
        'rw_kk': 0.85 + normal((N_ODD, D_RW), 0.05),
        'rw_ka': 1.0 + normal((N_ODD, D_RW), 0.05),
        'rw_rk': normal((N_ODD, RW_HEADS, RW_HEAD), 0.1),
        'rw_g2': normal((N_ODD, RW_G_LORA, D_RW), RW_G_LORA ** -0.5),
        'rw_ln_w': 1.0 + normal((N_ODD, D_RW), 0.05),
        'rw_ln_b': normal((N_ODD, D_RW), 0.02),
        'ml_conv_w': normal((N_ODD, CONV_W, 2 * D_ML), CONV_W ** -0.5),
        'ml_conv_b': normal((N_ODD, 2 * D_ML), 0.02),
        'ml_bi': normal((N_ODD, 2, ML_HEADS), 0.1),
        'ml_bf': jnp.linspace(3.0, 6.0, ML_HEADS, dtype=F32) + normal((N_ODD, 2, ML_HEADS), 0.1),
        'ml_norm_w': 1.0 + normal((N_ODD, D_ML), 0.05),
        'od_w_out': normal((N_ODD, D_RW + D_ML, D), (D_RW + D_ML) ** -0.5),
    }


def reference(x_prompt, x_sample, state_rglru, state_rwkv, state_mlstm_C, state_mlstm_n, state_mlstm_m,
              c, c_ctx, norm1_w, norm2_w, w_mod, b_mod, mlp_w1, mlp_w2, final_norm_w,
              ev_w_in, pool_w, pool_scale, rg_conv_w, rg_conv_b, rg_wa, rg_ba, rg_wx, rg_bx, rg_lam, ev_w_out,
              od_w_in, rw_mu, rw_w0, rw_w2, rw_a0, rw_a2, rw_kk, rw_ka, rw_rk, rw_g2, rw_ln_w, rw_ln_b,
              ml_conv_w, ml_conv_b, ml_bi, ml_bf, ml_norm_w, od_w_out):
    P = dict(norm1_w=norm1_w, norm2_w=norm2_w, w_mod=w_mod, b_mod=b_mod, mlp_w1=mlp_w1, mlp_w2=mlp_w2,
             final_norm_w=final_norm_w, ev_w_in=ev_w_in, pool_w=pool_w, pool_scale=pool_scale,
             rg_conv_w=rg_conv_w, rg_conv_b=rg_conv_b, rg_wa=rg_wa, rg_ba=rg_ba, rg_wx=rg_wx, rg_bx=rg_bx,
             rg_lam=rg_lam, ev_w_out=ev_w_out, od_w_in=od_w_in, rw_mu=rw_mu, rw_w0=rw_w0, rw_w2=rw_w2,
             rw_a0=rw_a0, rw_a2=rw_a2, rw_kk=rw_kk, rw_ka=rw_ka, rw_rk=rw_rk, rw_g2=rw_g2,
             rw_ln_w=rw_ln_w, rw_ln_b=rw_ln_b, ml_conv_w=ml_conv_w, ml_conv_b=ml_conv_b, ml_bi=ml_bi,
             ml_bf=ml_bf, ml_norm_w=ml_norm_w, od_w_out=od_w_out)
    Bp = x_prompt.shape[0]
    rg0 = jnp.zeros((Bp, N_EVEN, 2, D_RG), F32)
    rw0 = jnp.zeros((Bp, N_ODD, 2, RW_HEADS, RW_HEAD, RW_HEAD), F32)
    C0 = jnp.zeros((Bp, N_ODD, 2, ML_HEADS, ML_DK, ML_DK), F32)
    n0 = jnp.zeros((Bp, N_ODD, 2, ML_HEADS, ML_DK), F32)
    m0 = jnp.zeros((Bp, N_ODD, 2, ML_HEADS), F32)
    cond_ctx = jnp.broadcast_to(c_ctx[None, :], (Bp, D_MODEL))
    y_prompt, (s_rg, s_rw, s_C, s_n, s_m) = trunk(x_prompt, cond_ctx, False, rg0, rw0, C0, n0, m0, P)
    y_sample, _ = trunk(x_sample, c, True, state_rglru, state_rwkv, state_mlstm_C, state_mlstm_n,
                        state_mlstm_m, P)
    return (y_prompt, y_sample, s_rg, s_rw, s_C, s_n, s_m)
```

```python
import functools
import math

import jax
import jax.numpy as jnp
from jax import lax
from jax.experimental import pallas as pl
from jax.experimental.pallas import tpu as pltpu

F32 = jnp.float32
BF16 = jnp.bfloat16

D_MODEL = 1024
DEPTH = 2
GRID_W = 64
D_FF = 4 * D_MODEL
NORM_EPS = 1e-6
N_MOD = 6

D_POOL = D_MODEL // 2
POOL_WINDOWS = (2, 4, 8, 16)
POOL_GW = D_POOL // len(POOL_WINDOWS)
D_RG = D_MODEL // 2
RG_BLOCKS = 8
RG_C = 8.0
CONV_W = 4
CONV_LEFT = 2
EV_COLS = D_POOL + 2 * D_RG

D_RW = D_MODEL // 2
RW_HEAD = 64
RW_HEADS = D_RW // RW_HEAD
RW_PAIRS = RW_HEADS // 2
RW_W_LORA = 64
RW_A_LORA = 64
RW_G_LORA = 128
RW_LN_EPS = 64e-5
RW_ZCOLS = 3 * D_RW + RW_G_LORA + 128 + 128
D_ML = D_MODEL // 2
ML_HEADS = 4
ML_DK = D_ML // ML_HEADS
ML_ZCOLS = 4 * D_ML + 128

LANES = 128
HALO = 16
Z_DTYPE = BF16
Y_DTYPE = BF16
ROW_BLOCK = 512
SCAN_CHUNK = 64
RG_CHUNK = 256
VMEM_LIMIT = 56 * 1024 * 1024

_ARB1 = pltpu.CompilerParams(dimension_semantics=("arbitrary",), vmem_limit_bytes=VMEM_LIMIT)
_ARB2 = pltpu.CompilerParams(dimension_semantics=("arbitrary", "arbitrary"), vmem_limit_bytes=VMEM_LIMIT)


def _dot(a, b):
    return jnp.dot(a, b, preferred_element_type=F32)


def _dot_nt(a, b):
    return lax.dot_general(a, b, (((1,), (1,)), ((), ())), preferred_element_type=F32)


def _dot_tn(a, b):
    return lax.dot_general(a, b, (((0,), (0,)), ((), ())), preferred_element_type=F32)


def _split2(x):
    hi = x.astype(BF16)
    lo = (x - hi.astype(F32)).astype(BF16)
    return hi, lo


def _split3(x):
    hi = x.astype(BF16)
    r = x - hi.astype(F32)
    mid = r.astype(BF16)
    lo = (r - mid.astype(F32)).astype(BF16)
    return hi, mid, lo


def _dot_f32(a, b):
    ah, al = _split2(a)
    bh, bl = _split2(b)
    return _dot(ah, bh) + (_dot(ah, bl) + _dot(al, bh))


def _dot_exact_lhs(m_bf16, x):
    hi, mid, lo = _split3(x)
    return _dot(m_bf16, hi) + (_dot(m_bf16, mid) + _dot(m_bf16, lo))


def _dot_exact_rhs(x, m_bf16):
    hi, mid, lo = _split3(x)
    return _dot(hi, m_bf16) + (_dot(mid, m_bf16) + _dot(lo, m_bf16))


def _softplus(x):
    return jnp.maximum(x, 0.0) + jnp.log1p(jnp.exp(-jnp.abs(x)))


def _sigmoid(x):
    return jax.nn.sigmoid(x)


def _silu(x):
    return x * jax.nn.sigmoid(x)


def _gelu_tanh(x):
    c = math.sqrt(2.0 / math.pi)
    return x * (0.5 * (1.0 + jnp.tanh(c * (x + 0.044715 * (x * x * x)))))


def _rms_norm(x, w):
    ms = jnp.mean(x * x, axis=-1, keepdims=True)
    return x * lax.rsqrt(ms + NORM_EPS) * w


def _load_ext(z_ref, c0, n, t_len, lo, hi):
    cur = z_ref[pl.ds(c0, n), lo:hi].astype(F32)
    pb = pl.multiple_of(jnp.maximum(c0 - HALO, 0), HALO)
    pa = pl.multiple_of(jnp.minimum(c0 + n, t_len - HALO), HALO)
    before = z_ref[pl.ds(pb, HALO), lo:hi].astype(F32)
    after = z_ref[pl.ds(pa, HALO), lo:hi].astype(F32)
    before = jnp.where(c0 > 0, before, 0.0)
    after = jnp.where(c0 + n < t_len, after, 0.0)
    return jnp.concatenate([before, cur, after], axis=0)


def _head_sum(x, ones_bd):
    n = x.shape[0]
    groups = x.shape[1] // LANES
    xs = jnp.concatenate([x[:, g * LANES:(g + 1) * LANES] for g in range(groups)], axis=0)
    s = _dot_exact_rhs(xs, ones_bd)
    return jnp.concatenate([s[g * n:(g + 1) * n] for g in range(groups)], axis=1)


def _mod_kernel(c_ref, w_ref, b_ref, o_ref):
    s = _silu(c_ref[...])
    o_ref[...] = _dot_f32(s, w_ref[...]) + b_ref[...]


def _modulation(cond, w_mod, b_mod):
    rows = cond.shape[0]
    tn = 1536
    out = pl.pallas_call(
        _mod_kernel,
        grid=(DEPTH, N_MOD * D_MODEL // tn),
        in_specs=[
            pl.BlockSpec((rows, D_MODEL), lambda l, j: (0, 0)),
            pl.BlockSpec((None, D_MODEL, tn), lambda l, j: (l, 0, j)),
            pl.BlockSpec((None, 1, tn), lambda l, j: (l, 0, j)),
        ],
        out_specs=pl.BlockSpec((None, rows, tn), lambda l, j: (l, 0, j)),
        out_shape=jax.ShapeDtypeStruct((DEPTH, rows, N_MOD * D_MODEL), F32),
        compiler_params=_ARB2,
    )(cond, w_mod, b_mod.reshape(DEPTH, 1, N_MOD * D_MODEL))
    return out.reshape(DEPTH, rows, N_MOD, D_MODEL)


def _inproj_kernel(*refs, n_out, col_chunk):
    x_ref, mod_ref, nw_ref = refs[:3]
    w_refs = refs[3:3 + n_out]
    z_refs = refs[3 + n_out:]
    h = _rms_norm(x_ref[...], nw_ref[...])
    h = h * (1.0 + mod_ref[1:2, :]) + mod_ref[0:1, :]
    hb = h.astype(BF16)
    for w_ref, z_ref in zip(w_refs, z_refs):
        n = w_ref.shape[1]
        for c in range(0, n, col_chunk):
            e = min(c + col_chunk, n)
            z_ref[:, c:e] = _dot(hb, w_ref[:, c:e]).astype(z_ref.dtype)


def _in_proj(x, mod, rows_per_mod, mod_base, norm_w, weights):
    m = x.shape[0]
    tm = ROW_BLOCK
    n_out = len(weights)
    const = lambda i: (0, 0)
    in_specs = [
        pl.BlockSpec((tm, D_MODEL), lambda i: (i, 0)),
        pl.BlockSpec((None, N_MOD, D_MODEL), lambda i: (mod_base + (i * tm) // rows_per_mod, 0, 0)),
        pl.BlockSpec((1, D_MODEL), const),
    ] + [pl.BlockSpec(w.shape, const, pipeline_mode=pl.Buffered(1)) for w in weights]
    out_specs = [pl.BlockSpec((tm, w.shape[1]), lambda i: (i, 0)) for w in weights]
    out_shape = [jax.ShapeDtypeStruct((m, w.shape[1]), Z_DTYPE) for w in weights]
    return pl.pallas_call(
        functools.partial(_inproj_kernel, n_out=n_out, col_chunk=512),
        grid=(m // tm,),
        in_specs=in_specs,
        out_specs=out_specs,
        out_shape=out_shape,
        compiler_params=_ARB1,
    )(x, mod, norm_w.reshape(1, D_MODEL), *weights)


def _outmlp_kernel(*refs, n_y, final, ff_chunk):
    y_refs = refs[:n_y]
    x_ref, mod_ref, nw_ref, wo_ref, w1_ref, w2_ref, fw_ref, o_ref = refs[n_y:]
    y = None
    row = 0
    for y_ref in y_refs:
        k = y_ref.shape[1]
        part = _dot(y_ref[...], wo_ref[row:row + k, :])
        y = part if y is None else y + part
        row += k
    x = x_ref[...] + mod_ref[2:3, :] * y
    h = _rms_norm(x, nw_ref[...]) * (1.0 + mod_ref[4:5, :]) + mod_ref[3:4, :]
    hb = h.astype(BF16)
    acc = None
    for c in range(0, D_FF, ff_chunk):
        u = jnp.maximum(_dot(hb, w1_ref[:, c:c + ff_chunk]), 0.0)
        part = _dot((u * u).astype(BF16), w2_ref[c:c + ff_chunk, :])
        acc = part if acc is None else acc + part
    x = x + mod_ref[5:6, :] * acc
    if final:
        x = _rms_norm(x, fw_ref[...])
    o_ref[...] = x


def _out_mlp(ys, x, mod, rows_per_mod, mod_base, norm_w, w_out, w1, w2, final_w, final):
    m = x.shape[0]
    tm = ROW_BLOCK
    const = lambda i: (0, 0)
    single = dict(pipeline_mode=pl.Buffered(1))
    in_specs = [pl.BlockSpec((tm, y.shape[1]), lambda i: (i, 0)) for y in ys] + [
        pl.BlockSpec((tm, D_MODEL), lambda i: (i, 0)),
        pl.BlockSpec((None, N_MOD, D_MODEL), lambda i: (mod_base + (i * tm) // rows_per_mod, 0, 0)),
        pl.BlockSpec((1, D_MODEL), const),
        pl.BlockSpec(w_out.shape, const, **single),
        pl.BlockSpec(w1.shape, const, **single),
        pl.BlockSpec(w2.shape, const, **single),
        pl.BlockSpec((1, D_MODEL), const),
    ]
    return pl.pallas_call(
        functools.partial(_outmlp_kernel, n_y=len(ys), final=final, ff_chunk=1024),
        grid=(m // tm,),
        in_specs=in_specs,
        out_specs=pl.BlockSpec((tm, D_MODEL), lambda i: (i, 0)),
        out_shape=jax.ShapeDtypeStruct((m, D_MODEL), F32),
        compiler_params=_ARB1,
    )(*ys, x, mod, norm_w.reshape(1, D_MODEL), w_out, w1, w2, final_w.reshape(1, D_MODEL))


def _even_kernel(z_ref, h0_ref, pw_ref, ps_ref, cw_ref, cb_ref, wg_ref, bg_ref, lam_ref,
                 y_ref, hfin_ref, rowbuf, cbuf, a_s, b_s, h_s, hf_s, *, t_len, grid):
    seg = GRID_W if grid else t_len
    n_seg = t_len // seg
    n_groups = len(POOL_WINDOWS)
    pad = 8

    rowbuf[...] = jnp.zeros(rowbuf.shape, F32)
    pos = lax.broadcasted_iota(jnp.int32, (seg, POOL_GW), 0)

    def pool_segment(r):
        base = r * seg if isinstance(r, int) else pl.multiple_of(r * seg, seg)
        for g, w in enumerate(POOL_WINDOWS):
            lo_off = w // 2
            hi_off = w - 1 - lo_off
            cols = slice(g * POOL_GW, (g + 1) * POOL_GW)
            xg = z_ref[pl.ds(base, seg), cols].astype(F32)
            if grid:
                s1 = None
                for o in range(-lo_off, hi_off + 1):
                    rr = r + o
                    valid = jnp.logical_and(rr >= 0, rr < n_seg)
                    src = pl.multiple_of(jnp.clip(rr, 0, n_seg - 1) * seg, seg)
                    term = jnp.where(valid, z_ref[pl.ds(src, seg), cols].astype(F32), 0.0)
                    s1 = term if s1 is None else s1 + term
                cnt_r = jnp.minimum(r + hi_off, n_seg - 1) - jnp.maximum(r - lo_off, 0) + 1
                m1 = s1 / cnt_r.astype(F32)
            else:
                m1 = xg
            rowbuf[g, pad:pad + seg, :] = m1
            s2 = None
            for o in range(-lo_off, hi_off + 1):
                term = rowbuf[g, pad + o:pad + o + seg, :]
                s2 = term if s2 is None else s2 + term
            cnt_c = jnp.minimum(pos + hi_off, seg - 1) - jnp.maximum(pos - lo_off, 0) + 1
            d = s2 / cnt_c.astype(F32) - xg
            yg = _dot(d.astype(BF16), pw_ref[g]) * ps_ref[:, cols]
            y_ref[pl.ds(base, seg), cols] = yg.astype(y_ref.dtype)

    if n_seg == 1:
        pool_segment(0)
    else:
        def seg_body(r, carry):
            pool_segment(r)
            return carry
        lax.fori_loop(0, n_seg, seg_body, 0)

    ch = RG_CHUNK
    n_ch = t_len // ch
    rg_lo, rg_hi = D_POOL, D_POOL + D_RG

    def gates(c0, d):
        cbuf[...] = _load_ext(z_ref, c0, ch, t_len, rg_lo, rg_hi)
        xc = cb_ref[...]
        for j in range(CONV_W):
            off = HALO - CONV_LEFT + j
            xc = xc + cbuf[off:off + ch, :] * cw_ref[j:j + 1, :]
        g = _dot(xc.astype(BF16), wg_ref[:, d * 2 * D_RG:(d + 1) * 2 * D_RG]) \
            + bg_ref[:, d * 2 * D_RG:(d + 1) * 2 * D_RG]
        r = _sigmoid(g[:, :D_RG])
        i = _sigmoid(g[:, D_RG:])
        log_a = (-RG_C) * r * _softplus(-lam_ref[d:d + 1, :])
        a = jnp.exp(log_a)
        a_s[...] = a
        b_s[...] = jnp.sqrt(jnp.tanh(-log_a) * (a * a + 1.0)) * (i * xc)

    def scan_rows(h, reverse):
        def row(t, h):
            tt = (ch - 1 - t) if reverse else t
            h = a_s[pl.ds(tt, 1), :] * h + b_s[pl.ds(tt, 1), :]
            h_s[pl.ds(tt, 1), :] = h
            return h
        return lax.fori_loop(0, ch, row, h, unroll=8)

    def fwd_body(i, h):
        c0 = pl.multiple_of(i * ch, ch)
        gates(c0, 0)
        h = scan_rows(h, False)
        hf_s[pl.ds(c0, ch), :] = h_s[...]
        return h

    h_fwd = lax.fori_loop(0, n_ch, fwd_body, h0_ref[0:1, :])

    def bwd_body(i, h):
        c0 = pl.multiple_of((n_ch - 1 - i) * ch, ch)
        gates(c0, 1)
        h = scan_rows(h, True)
        u_gate = z_ref[pl.ds(c0, ch), rg_hi:rg_hi + D_RG].astype(F32)
        y = (hf_s[pl.ds(c0, ch), :] + h_s[...]) * _gelu_tanh(u_gate)
        y_ref[pl.ds(c0, ch), D_POOL:D_POOL + D_RG] = y.astype(y_ref.dtype)
        return h

    h_bwd = lax.fori_loop(0, n_ch, bwd_body, h0_ref[1:2, :])
    hfin_ref[0:1, :] = h_fwd
    hfin_ref[1:2, :] = h_bwd


def _even_mixer(z, h0, p, n_batch, t_len, grid):
    seg = GRID_W if grid else t_len
    const2 = lambda b: (0, 0)
    const3 = lambda b: (0, 0, 0)
    return pl.pallas_call(
        functools.partial(_even_kernel, t_len=t_len, grid=grid),
        grid=(n_batch,),
        in_specs=[
            pl.BlockSpec((t_len, EV_COLS), lambda b: (b, 0)),
            pl.BlockSpec((None, 2, D_RG), lambda b: (b, 0, 0)),
            pl.BlockSpec(p['pool_w'].shape, const3),
            pl.BlockSpec((1, D_POOL), const2),
            pl.BlockSpec((CONV_W, D_RG), const2),
            pl.BlockSpec((1, D_RG), const2),
            pl.BlockSpec((D_RG, 4 * D_RG), const2),
            pl.BlockSpec((1, 4 * D_RG), const2),
            pl.BlockSpec((2, D_RG), const2),
        ],
        out_specs=[
            pl.BlockSpec((t_len, D_POOL + D_RG), lambda b: (b, 0)),
            pl.BlockSpec((None, 2, D_RG), lambda b: (b, 0, 0)),
        ],
        out_shape=[
            jax.ShapeDtypeStruct((n_batch * t_len, D_POOL + D_RG), Y_DTYPE),
            jax.ShapeDtypeStruct((n_batch, 2, D_RG), F32),
        ],
        scratch_shapes=[
            pltpu.VMEM((len(POOL_WINDOWS), seg + 16, POOL_GW), F32),
            pltpu.VMEM((RG_CHUNK + 2 * HALO, D_RG), F32),
            pltpu.VMEM((RG_CHUNK, D_RG), F32),
            pltpu.VMEM((RG_CHUNK, D_RG), F32),
            pltpu.VMEM((RG_CHUNK, D_RG), F32),
            pltpu.VMEM((t_len, D_RG), F32),
        ],
        compiler_params=_ARB1,
    )(z, h0, p['pool_w'], p['pool_scale'], p['conv_w'], p['conv_b'], p['wg'], p['bg'], p['lam'])


def _rwkv_kernel(z_ref, s0_ref, mu_ref, kkw_ref, ka_ref, rk_ref, g2_ref, w0_ref, w2_ref, a0_ref, a2_ref,
                 lnw_ref, lnb_ref, tri_ref, bd_ref,
                 y_ref, sfin_ref, ebuf, s_s, yd_s, bon_s, g_s, *, t_len):
    L = SCAN_CHUNK
    n_ch = t_len // L
    L2 = 2 * L
    s_s[...] = s0_ref[...]


    def stack(x):
        first_head = lax.broadcasted_iota(jnp.int32, (L, LANES), 1) < RW_HEAD
        return jnp.concatenate([jnp.where(first_head, x, 0.0), jnp.where(first_head, 0.0, x)],
                               axis=0).astype(BF16)

    def chunk_dir(c0, d):
        ones_bd = bd_ref[...]
        ri = lax.broadcasted_iota(jnp.int32, (L2, L2), 0)
        ci = lax.broadcasted_iota(jnp.int32, (L2, L2), 1)
        eye = jnp.where(ri == ci, 1.0, 0.0)
        rl = jnp.where(ri < L, ri, ri - L)
        cl = jnp.where(ci < L, ci, ci - L)
        strict = (cl < rl) if d == 0 else (cl > rl)
        incl = (cl <= rl) if d == 0 else (cl >= rl)
        level = ri ^ ci
        ebuf[d] = _load_ext(z_ref, c0, L, t_len, 0, RW_ZCOLS)
        cur = ebuf[d, HALO:HALO + L, :]
        prev = ebuf[d, HALO - 1:HALO - 1 + L, :]
        nxt = ebuf[d, HALO + 1:HALO + 1 + L, :]
        zs = cur + mu_ref[0:1, :] * (prev - cur) + mu_ref[1:2, :] * (nxt - cur)
        r = zs[:, 0:D_RW]
        k = zs[:, D_RW:2 * D_RW]
        v = zs[:, 2 * D_RW:3 * D_RW]
        o = 3 * D_RW
        gd = zs[:, o:o + RW_G_LORA]
        wd = zs[:, o + RW_G_LORA:o + RW_G_LORA + LANES]
        ad = zs[:, o + RW_G_LORA + LANES:o + RW_G_LORA + 2 * LANES]

        kkv = k * kkw_ref[...]
        nrm = jnp.sqrt(_head_sum(kkv * kkv, ones_bd))
        kkn = kkv / jnp.maximum(nrm, 1e-12)

        w_log = -_softplus(-(w0_ref[d:d + 1, :] + _dot_f32(jnp.tanh(wd), w2_ref[d]))) - 0.5
        lw = -jnp.exp(w_log)
        a = _sigmoid(a0_ref[d:d + 1, :] + _dot_f32(ad, a2_ref[d]))
        kd = k * (1.0 + (a - 1.0) * ka_ref[...])
        bon_s[d, pl.ds(c0, L), :] = _head_sum(r * kd * rk_ref[...], ones_bd) * v
        if d == 0:
            g_s[pl.ds(c0, L), :] = _dot(_sigmoid(gd).astype(BF16), g2_ref[...])

        cum = _dot_exact_lhs(tri_ref[d], lw)
        e_pos = jnp.exp(cum)
        e_neg = jnp.exp(-cum)
        rh = r * e_pos
        kh = kd * e_neg
        bh = (kkn * a) * e_neg
        ah = -kkn * jnp.exp(cum - lw)
        last = L - 1 if d == 0 else 0
        gam = e_pos[last:last + 1, :]
        bhg = bh * gam
        khg = kh * gam

        outs = []
        for p in range(RW_PAIRS):
            cols = slice(p * LANES, (p + 1) * LANES)
            a_st, r_st, b_st, k_st = stack(ah[:, cols]), stack(rh[:, cols]), stack(bh[:, cols]), stack(kh[:, cols])
            v_st = stack(v[:, cols])
            lhs = jnp.concatenate([a_st, r_st], axis=0)
            rhs = jnp.concatenate([b_st, k_st], axis=0)
            sc = _dot_nt(lhs, rhs)
            n_ab = jnp.where(strict, sc[0:L2, 0:L2], 0.0)
            m_ak = jnp.where(strict, sc[0:L2, L2:2 * L2], 0.0)
            t_rb = jnp.where(incl, sc[L2:2 * L2, 0:L2], 0.0)
            t_rk = jnp.where(incl, sc[L2:2 * L2, L2:2 * L2], 0.0)

            t_inv = eye + jnp.where(level == 1, n_ab, 0.0)
            for lv in range(1, int(math.log2(L))):
                n_lv = jnp.where(lax.shift_right_logical(level, lv) == 1, n_ab, 0.0)
                t_b = t_inv.astype(BF16)
                t_inv = t_inv + _dot(t_b, _dot(n_lv.astype(BF16), t_b).astype(BF16))

            s_prev = s_s[d, p]
            ar_s = _dot_nt(lhs, s_prev.astype(BF16))
            u_rhs = ar_s[0:L2] + _dot(m_ak.astype(BF16), v_st)
            u_st = _dot(t_inv.astype(BF16), u_rhs.astype(BF16)).astype(BF16)
            uv = jnp.concatenate([u_st, v_st], axis=0)
            y_st = ar_s[L2:2 * L2] + _dot(jnp.concatenate([t_rb, t_rk], axis=1).astype(BF16), uv)
            outs.append(y_st[0:L] + y_st[L:L2])
            bk = jnp.concatenate([stack(bhg[:, cols]), stack(khg[:, cols])], axis=0)
            s_s[d, p] = s_prev * gam[:, cols] + _dot_tn(uv, bk)
        yd_s[d, pl.ds(c0, L), :] = jnp.concatenate(outs, axis=1)

    def body(i, carry):
        chunk_dir(pl.multiple_of(i * L, L), 0)
        chunk_dir(pl.multiple_of((n_ch - 1 - i) * L, L), 1)
        return carry

    lax.fori_loop(0, n_ch, body, 0)

    def epilogue(i, carry):
        c0 = pl.multiple_of(i * L, L)
        ones_bd = bd_ref[...]
        ys = yd_s[0, pl.ds(c0, L), :] + yd_s[1, pl.ds(c0, L), :]
        mu = _head_sum(ys, ones_bd) * (1.0 / RW_HEAD)
        yc = ys - mu
        var = _head_sum(yc * yc, ones_bd) * (1.0 / RW_HEAD)
        yn = yc * lax.rsqrt(var + RW_LN_EPS) * lnw_ref[...] + lnb_ref[...]
        out = (yn + (bon_s[0, pl.ds(c0, L), :] + bon_s[1, pl.ds(c0, L), :])) * g_s[pl.ds(c0, L), :]
        y_ref[pl.ds(c0, L), :] = out.astype(y_ref.dtype)
        return carry

    lax.fori_loop(0, n_ch, epilogue, 0)
    sfin_ref[...] = s_s[...]


def _rwkv_mixer(z, s0, p, n_batch, t_len):
    L = SCAN_CHUNK
    c2 = lambda b: (0, 0)
    c3 = lambda b: (0, 0, 0)
    st_spec = pl.BlockSpec((None, 2, RW_PAIRS, LANES, LANES), lambda b: (b, 0, 0, 0, 0))
    return pl.pallas_call(
        functools.partial(_rwkv_kernel, t_len=t_len),
        grid=(n_batch,),
        in_specs=[
            pl.BlockSpec((t_len, RW_ZCOLS), lambda b: (b, 0)),
            st_spec,
            pl.BlockSpec((2, RW_ZCOLS), c2),
            pl.BlockSpec((1, D_RW), c2),
            pl.BlockSpec((1, D_RW), c2),
            pl.BlockSpec((1, D_RW), c2),
            pl.BlockSpec((RW_G_LORA, D_RW), c2),
            pl.BlockSpec((2, D_RW), c2),
            pl.BlockSpec((2, LANES, D_RW), c3),
            pl.BlockSpec((2, D_RW), c2),
            pl.BlockSpec((2, LANES, D_RW), c3),
            pl.BlockSpec((1, D_RW), c2),
            pl.BlockSpec((1, D_RW), c2),
            pl.BlockSpec((2, L, L), c3),
            pl.BlockSpec((LANES, LANES), c2),
        ],
        out_specs=[pl.BlockSpec((t_len, D_RW), lambda b: (b, 0)), st_spec],
        out_shape=[
            jax.ShapeDtypeStruct((n_batch * t_len, D_RW), Y_DTYPE),
            jax.ShapeDtypeStruct((n_batch, 2, RW_PAIRS, LANES, LANES), F32),
        ],
        scratch_shapes=[
            pltpu.VMEM((2, L + 2 * HALO, RW_ZCOLS), F32),
            pltpu.VMEM((2, RW_PAIRS, LANES, LANES), F32),
            pltpu.VMEM((2, t_len, D_RW), F32),
            pltpu.VMEM((2, t_len, D_RW), F32),
            pltpu.VMEM((t_len, D_RW), F32),
        ],
        compiler_params=_ARB1,
    )(z, s0, p['mu'], p['kk'], p['ka'], p['rk'], p['g2'], p['w0'], p['w2'], p['a0'], p['a2'],
      p['ln_w'], p['ln_b'], p['tri'], p['ones_bd'])


def _mlstm_kernel(z_ref, c0_ref, n0_ref, m0_ref, cw_ref, cb_ref, gb_ref, nw_ref, tri_ref,
                  y_ref, cfin_ref, nfin_ref, mfin_ref, cbuf, c_s, n_s, m_s, hd_s, *, t_len):
    L = SCAN_CHUNK
    n_ch = t_len // L
    c_s[...] = c0_ref[...]
    n_s[...] = n0_ref[...]
    m_s[...] = m0_ref[...]
    q_scale = ML_DK ** -0.5
    g_lo = 4 * D_ML

    def chunk_dir(c0, d):
        ri = lax.broadcasted_iota(jnp.int32, (L, L), 0)
        ci = lax.broadcasted_iota(jnp.int32, (L, L), 1)
        causal = (ci <= ri) if d == 0 else (ci >= ri)
        cbuf[d] = _load_ext(z_ref, c0, L, t_len, 0, 2 * D_ML)
        qk = cb_ref[...]
        for j in range(CONV_W):
            off = HALO - CONV_LEFT + j
            qk = qk + cbuf[d, off:off + L, :] * cw_ref[j:j + 1, :]
        qk = _silu(qk)
        q = qk[:, :D_ML] * q_scale
        k = qk[:, D_ML:]
        v = z_ref[pl.ds(c0, L), 2 * D_ML:3 * D_ML].astype(F32)
        gts = z_ref[pl.ds(c0, L), g_lo:g_lo + LANES].astype(F32) + gb_ref[...]
        lf_all = -_softplus(-gts)
        bcum = _dot_exact_lhs(tri_ref[d], lf_all)
        both_t = jnp.concatenate([gts, bcum], axis=0).T
        last = L - 1 if d == 0 else 0
        outs = []
        for h in range(ML_HEADS):
            col_i = d * 2 * ML_HEADS + h
            col_f = col_i + ML_HEADS
            cols = slice(h * ML_DK, (h + 1) * ML_DK)
            b_col = bcum[:, col_f:col_f + 1]
            li_col = gts[:, col_i:col_i + 1]
            b_row = both_t[col_f:col_f + 1, L:2 * L]
            li_row = both_t[col_i:col_i + 1, 0:L]
            m_prev = m_s[d, h:h + 1, 0:1]
            log_d = jnp.where(causal, b_col - b_row + li_row, -jnp.inf)
            log_inter = b_col + m_prev
            m_t = jnp.maximum(log_inter, jnp.max(log_d, axis=1, keepdims=True))
            qb = q[:, cols].astype(BF16)
            kb = k[:, cols].astype(BF16)
            vh = v[:, cols]
            s = _dot_nt(qb, kb) * jnp.exp(log_d - m_t)
            w_inter = jnp.exp(log_inter - m_t)
            c_prev = c_s[d, h]
            n_prev = n_s[d, h:h + 1, :]
            num = _dot(s.astype(BF16), vh.astype(BF16)) + w_inter * _dot(qb, c_prev.astype(BF16))
            den = jnp.sum(s, axis=1, keepdims=True) \
                + w_inter * jnp.sum(q[:, cols] * n_prev, axis=1, keepdims=True)
            outs.append(num / jnp.maximum(jnp.abs(den), jnp.exp(-m_t)))
            m_new = m_t[last:last + 1, :]
            b_last = b_col[last:last + 1, :]
            w_s = jnp.exp(b_last - b_col + li_col - m_new)
            dec = jnp.exp(b_last + m_prev - m_new)
            c_s[d, h] = dec * c_prev + _dot_tn(kb, (w_s * vh).astype(BF16))
            n_s[d, h:h + 1, :] = dec * n_prev + jnp.sum(w_s * k[:, cols], axis=0, keepdims=True)
            m_s[d, h:h + 1, :] = jnp.broadcast_to(m_new, (1, LANES))
        hd_s[d, pl.ds(c0, L), :] = jnp.concatenate(outs, axis=1)

    def body(i, carry):
        chunk_dir(pl.multiple_of(i * L, L), 0)
        chunk_dir(pl.multiple_of((n_ch - 1 - i) * L, L), 1)
        return carry

    lax.fori_loop(0, n_ch, body, 0)

    def epilogue(i, carry):
        c0 = pl.multiple_of(i * L, L)
        hs = hd_s[0, pl.ds(c0, L), :] + hd_s[1, pl.ds(c0, L), :]
        o = z_ref[pl.ds(c0, L), 3 * D_ML:4 * D_ML].astype(F32)
        outs = []
        for h in range(ML_HEADS):
            x = hs[:, h * ML_DK:(h + 1) * ML_DK]
            mu = jnp.mean(x, axis=1, keepdims=True)
            xc = x - mu
            var = jnp.mean(xc * xc, axis=1, keepdims=True)
            outs.append(xc * lax.rsqrt(var + NORM_EPS))
        y = jnp.concatenate(outs, axis=1) * nw_ref[...] * _sigmoid(o)
        y_ref[pl.ds(c0, L), :] = y.astype(y_ref.dtype)
        return carry

    lax.fori_loop(0, n_ch, epilogue, 0)
    cfin_ref[...] = c_s[...]
    nfin_ref[...] = n_s[...]
    mfin_ref[...] = m_s[...]


def _mlstm_mixer(z, c0, n0, m0, p, n_batch, t_len):
    L = SCAN_CHUNK
    c2 = lambda b: (0, 0)
    c3 = lambda b: (0, 0, 0)
    c_spec = pl.BlockSpec((None, 2, ML_HEADS, ML_DK, ML_DK), lambda b: (b, 0, 0, 0, 0))
    n_spec = pl.BlockSpec((None, 2, ML_HEADS, ML_DK), lambda b: (b, 0, 0, 0))
    return pl.pallas_call(
        functools.partial(_mlstm_kernel, t_len=t_len),
        grid=(n_batch,),
        in_specs=[
            pl.BlockSpec((t_len, ML_ZCOLS), lambda b: (b, 0)),
            c_spec, n_spec, n_spec,
            pl.BlockSpec((CONV_W, 2 * D_ML), c2),
            pl.BlockSpec((1, 2 * D_ML), c2),
            pl.BlockSpec((1, LANES), c2),
            pl.BlockSpec((1, D_ML), c2),
            pl.BlockSpec((2, L, L), c3),
        ],
        out_specs=[pl.BlockSpec((t_len, D_ML), lambda b: (b, 0)), c_spec, n_spec, n_spec],
        out_shape=[
            jax.ShapeDtypeStruct((n_batch * t_len, D_ML), Y_DTYPE),
            jax.ShapeDtypeStruct((n_batch, 2, ML_HEADS, ML_DK, ML_DK), F32),
            jax.ShapeDtypeStruct((n_batch, 2, ML_HEADS, ML_DK), F32),
            jax.ShapeDtypeStruct((n_batch, 2, ML_HEADS, ML_DK), F32),
        ],
        scratch_shapes=[
            pltpu.VMEM((2, L + 2 * HALO, 2 * D_ML), F32),
            pltpu.VMEM((2, ML_HEADS, ML_DK, ML_DK), F32),
            pltpu.VMEM((2, ML_HEADS, ML_DK), F32),
            pltpu.VMEM((2, ML_HEADS, ML_DK), F32),
            pltpu.VMEM((2, t_len, D_ML), F32),
        ],
        compiler_params=_ARB1,
    )(z, c0, n0, m0, p['conv_w'], p['conv_b'], p['gate_b'], p['norm_w'], p['tri'])


def _block_diag(w):
    n, c, d = w.shape
    return jnp.einsum('ncd,nm->ncmd', w, jnp.eye(n, dtype=w.dtype)).reshape(n * c, n * d)


def _scan_tri():
    L = SCAN_CHUNK
    lower = jnp.tril(jnp.ones((L, L), F32))
    return jnp.stack([lower, lower.T]).astype(BF16)


def _prep_even(ev_w_in, pool_w, pool_scale, rg_conv_w, rg_conv_b, rg_wa, rg_ba, rg_wx, rg_bx, rg_lam, ev_w_out, j):
    wg = jnp.concatenate([_block_diag(rg_wa[j, 0]), _block_diag(rg_wx[j, 0]),
                          _block_diag(rg_wa[j, 1]), _block_diag(rg_wx[j, 1])], axis=1)
    bg = jnp.concatenate([rg_ba[j, 0], rg_bx[j, 0], rg_ba[j, 1], rg_bx[j, 1]]).reshape(1, 4 * D_RG)
    return dict(
        w_in=[ev_w_in[j].astype(BF16)],
        w_out=ev_w_out[j].astype(BF16),
        pool_w=pool_w[j].astype(BF16),
        pool_scale=pool_scale[j].reshape(1, D_POOL),
        conv_w=rg_conv_w[j],
        conv_b=rg_conv_b[j].reshape(1, D_RG),
        wg=wg.astype(BF16),
        bg=bg,
        lam=rg_lam[j],
    )


def _pad_cols(w, cuts, widths):
    pieces = jnp.split(w, cuts, axis=-1)
    out = []
    for piece, width in zip(pieces, widths):
        extra = width - piece.shape[-1]
        if extra:
            piece = jnp.pad(piece, [(0, 0)] * (piece.ndim - 1) + [(0, extra)])
        out.append(piece)
    return jnp.concatenate(out, axis=-1)


def _prep_odd(od_w_in, rw_mu, rw_w0, rw_w2, rw_a0, rw_a2, rw_kk, rw_ka, rw_rk, rw_g2, rw_ln_w, rw_ln_b,
              ml_conv_w, ml_conv_b, ml_bi, ml_bf, ml_norm_w, od_w_out, j):
    rw_cols = 3 * D_RW + RW_G_LORA + RW_W_LORA + RW_A_LORA
    base = 3 * D_RW + RW_G_LORA
    cuts = [base, base + RW_W_LORA]
    widths = [base, LANES, LANES]
    w_rw = _pad_cols(od_w_in[j][:, :rw_cols], cuts, widths)
    w_ml = _pad_cols(od_w_in[j][:, rw_cols:], [4 * D_ML], [4 * D_ML, LANES])
    pad_rows = lambda w: jnp.pad(w, ((0, 0), (0, LANES - w.shape[1]), (0, 0)))
    gate_b = jnp.concatenate([ml_bi[j, 0], ml_bf[j, 0], ml_bi[j, 1], ml_bf[j, 1]])
    gate_b = jnp.pad(gate_b, (0, LANES - gate_b.shape[0])).reshape(1, LANES)
    hm = jnp.arange(LANES) // RW_HEAD
    rw = dict(
        mu=_pad_cols(rw_mu[j], cuts, widths),
        kk=rw_kk[j].reshape(1, D_RW), ka=rw_ka[j].reshape(1, D_RW), rk=rw_rk[j].reshape(1, D_RW),
        g2=rw_g2[j].astype(BF16),
        w0=rw_w0[j], w2=pad_rows(rw_w2[j]), a0=rw_a0[j], a2=pad_rows(rw_a2[j]),
        ln_w=rw_ln_w[j].reshape(1, D_RW), ln_b=rw_ln_b[j].reshape(1, D_RW),
        tri=_scan_tri(),
        ones_bd=(hm[:, None] == hm[None, :]).astype(BF16),
    )
    ml = dict(
        conv_w=ml_conv_w[j], conv_b=ml_conv_b[j].reshape(1, 2 * D_ML),
        gate_b=gate_b, norm_w=ml_norm_w[j].reshape(1, D_ML), tri=_scan_tri(),
    )
    return dict(w_in=[w_rw.astype(BF16), w_ml.astype(BF16)], w_out=od_w_out[j].astype(BF16), rw=rw, ml=ml)


def _rwkv_state_in(s):
    b = s.shape[0]
    s = s.reshape(b, 2, RW_PAIRS, 2, RW_HEAD, RW_HEAD)
    bd = jnp.einsum('bdpivk,ij->bdpivjk', s, jnp.eye(2, dtype=s.dtype))
    return bd.reshape(b, 2, RW_PAIRS, LANES, LANES)


def _rwkv_state_out(bd):
    b = bd.shape[0]
    x = bd.reshape(b, 2, RW_PAIRS, 2, RW_HEAD, 2, RW_HEAD)
    s = jnp.stack([x[:, :, :, 0, :, 0, :], x[:, :, :, 1, :, 1, :]], axis=3)
    return s.reshape(b, 2, RW_HEADS, RW_HEAD, RW_HEAD)


def _trunk(x, mod, rows_per_mod, mod_base, grid, states, params, n_batch, t_len):
    ev, od, dense = params
    rg0, rw0, c0, n0, m0 = states
    x = x.reshape(n_batch * t_len, D_MODEL)
    spec = (rows_per_mod, mod_base)

    (z,) = _in_proj(x, mod[0], *spec, dense['norm1_w'][0], ev['w_in'])
    y_ev, rg_f = _even_mixer(z, rg0, ev, n_batch, t_len, grid)
    x = _out_mlp([y_ev], x, mod[0], *spec, dense['norm2_w'][0], ev['w_out'],
                 dense['mlp_w1'][0], dense['mlp_w2'][0], dense['final_norm_w'], False)

    z_rw, z_ml = _in_proj(x, mod[1], *spec, dense['norm1_w'][1], od['w_in'])
    y_rw, s_f = _rwkv_mixer(z_rw, rw0, od['rw'], n_batch, t_len)
    y_ml, c_f, n_f, m_f = _mlstm_mixer(z_ml, c0, n0, m0, od['ml'], n_batch, t_len)
    y = _out_mlp([y_rw, y_ml], x, mod[1], *spec, dense['norm2_w'][1], od['w_out'],
                 dense['mlp_w1'][1], dense['mlp_w2'][1], dense['final_norm_w'], True)
    return y.reshape(n_batch, t_len, D_MODEL), (rg_f, s_f, c_f, n_f, m_f)


def kernel(x_prompt, x_sample, state_rglru, state_rwkv, state_mlstm_C, state_mlstm_n, state_mlstm_m,
           c, c_ctx, norm1_w, norm2_w, w_mod, b_mod, mlp_w1, mlp_w2, final_norm_w,
           ev_w_in, pool_w, pool_scale, rg_conv_w, rg_conv_b, rg_wa, rg_ba, rg_wx, rg_bx, rg_lam, ev_w_out,
           od_w_in, rw_mu, rw_w0, rw_w2, rw_a0, rw_a2, rw_kk, rw_ka, rw_rk, rw_g2, rw_ln_w, rw_ln_b,
           ml_conv_w, ml_conv_b, ml_bi, ml_bf, ml_norm_w, od_w_out):
    bp, tp, _ = x_prompt.shape
    bs, ts, _ = x_sample.shape

    n_cond = 16
    cond = jnp.concatenate([c_ctx[None, :], c, jnp.zeros((n_cond - 1 - bs, D_MODEL), F32)], axis=0)
    mod = _modulation(cond, w_mod, b_mod)

    ev = _prep_even(ev_w_in, pool_w, pool_scale, rg_conv_w, rg_conv_b, rg_wa, rg_ba, rg_wx, rg_bx, rg_lam,
                    ev_w_out, 0)
    od = _prep_odd(od_w_in, rw_mu, rw_w0, rw_w2, rw_a0, rw_a2, rw_kk, rw_ka, rw_rk, rw_g2, rw_ln_w, rw_ln_b,
                   ml_conv_w, ml_conv_b, ml_bi, ml_bf, ml_norm_w, od_w_out, 0)
    dense = dict(norm1_w=norm1_w, norm2_w=norm2_w, mlp_w1=mlp_w1.astype(BF16), mlp_w2=mlp_w2.astype(BF16),
                 final_norm_w=final_norm_w)
    params = (ev, od, dense)

    zero_states = (
        jnp.zeros((bp, 2, D_RG), F32),
        jnp.zeros((bp, 2, RW_PAIRS, LANES, LANES), F32),
        jnp.zeros((bp, 2, ML_HEADS, ML_DK, ML_DK), F32),
        jnp.zeros((bp, 2, ML_HEADS, ML_DK), F32),
        jnp.zeros((bp, 2, ML_HEADS, ML_DK), F32),
    )
    y_prompt, (rg_f, s_f, c_f, n_f, m_f) = _trunk(
        x_prompt, mod, bp * tp, 0, False, zero_states, params, bp, tp)

    sample_states = (
        state_rglru[:, 0],
        _rwkv_state_in(state_rwkv[:, 0]),
        state_mlstm_C[:, 0],
        state_mlstm_n[:, 0],
        jnp.broadcast_to(state_mlstm_m[:, 0][..., None], (bs, 2, ML_HEADS, ML_DK)),
    )
    y_sample, _ = _trunk(x_sample, mod, ts, 1, True, sample_states, params, bs, ts)

    return (y_prompt, y_sample,
            rg_f[:, None],
            _rwkv_state_out(s_f)[:, None],
            c_f[:, None],
            n_f[:, None],
            m_f[:, None, :, :, 0])
```

```python
import functools
import math

import jax
import jax.numpy as jnp
from jax import lax
from jax.experimental import pallas as pl
from jax.experimental.pallas import tpu as pltpu

F32 = jnp.float32
BF16 = jnp.bfloat16

D_MODEL = 1024
DEPTH = 2
GRID_W = 64
D_FF = 4 * D_MODEL
NORM_EPS = 1e-6
N_MOD = 6

D_POOL = D_MODEL // 2
POOL_WINDOWS = (2, 4, 8, 16)
POOL_GW = D_POOL // len(POOL_WINDOWS)
D_RG = D_MODEL // 2
RG_BLOCKS = 8
RG_C = 8.0
CONV_W = 4
CONV_LEFT = 2
EV_COLS = D_POOL + 2 * D_RG

D_RW = D_MODEL // 2
RW_HEAD = 64
RW_HEADS = D_RW // RW_HEAD
RW_PAIRS = RW_HEADS // 2
RW_W_LORA = 64
RW_A_LORA = 64
RW_G_LORA = 128
RW_LN_EPS = 64e-5
RW_ZCOLS = 3 * D_RW + RW_G_LORA + 128 + 128
D_ML = D_MODEL // 2
ML_HEADS = 4
ML_DK = D_ML // ML_HEADS
ML_ZCOLS = 4 * D_ML + 128

LANES = 128
HALO = 16
Z_DTYPE = BF16
Y_DTYPE = BF16
ROW_BLOCK = 512
SCAN_CHUNK = 64
RG_CHUNK = 256
VMEM_LIMIT = 56 * 1024 * 1024

_ARB1 = pltpu.CompilerParams(dimension_semantics=("arbitrary",), vmem_limit_bytes=VMEM_LIMIT)
_ARB2 = pltpu.CompilerParams(dimension_semantics=("arbitrary", "arbitrary"), vmem_limit_bytes=VMEM_LIMIT)


def _dot(a, b):
    return jnp.dot(a, b, preferred_element_type=F32)


def _dot_nt(a, b):
    return lax.dot_general(a, b, (((1,), (1,)), ((), ())), preferred_element_type=F32)


def _dot_tn(a, b):
    return lax.dot_general(a, b, (((0,), (0,)), ((), ())), preferred_element_type=F32)


def _split2(x):
    hi = x.astype(BF16)
    lo = (x - hi.astype(F32)).astype(BF16)
    return hi, lo


def _split3(x):
    hi = x.astype(BF16)
    r = x - hi.astype(F32)
    mid = r.astype(BF16)
    lo = (r - mid.astype(F32)).astype(BF16)
    return hi, mid, lo


def _dot_f32(a, b):
    ah, al = _split2(a)
    bh, bl = _split2(b)
    return _dot(ah, bh) + (_dot(ah, bl) + _dot(al, bh))


def _dot_exact_lhs(m_bf16, x):
    hi, mid, lo = _split3(x)
    return _dot(m_bf16, hi) + (_dot(m_bf16, mid) + _dot(m_bf16, lo))


def _dot_exact_rhs(x, m_bf16):
    hi, mid, lo = _split3(x)
    return _dot(hi, m_bf16) + (_dot(mid, m_bf16) + _dot(lo, m_bf16))


def _softplus(x):
    return jnp.maximum(x, 0.0) + jnp.log1p(jnp.exp(-jnp.abs(x)))


def _sigmoid(x):
    return jax.nn.sigmoid(x)


def _silu(x):
    return x * jax.nn.sigmoid(x)


def _gelu_tanh(x):
    c = math.sqrt(2.0 / math.pi)
    return x * (0.5 * (1.0 + jnp.tanh(c * (x + 0.044715 * (x * x * x)))))


def _rms_norm(x, w):
    ms = jnp.mean(x * x, axis=-1, keepdims=True)
    return x * lax.rsqrt(ms + NORM_EPS) * w


def _load_ext(z_ref, c0, n, t_len, lo, hi):
    cur = z_ref[pl.ds(c0, n), lo:hi].astype(F32)
    pb = pl.multiple_of(jnp.maximum(c0 - HALO, 0), HALO)
    pa = pl.multiple_of(jnp.minimum(c0 + n, t_len - HALO), HALO)
    before = z_ref[pl.ds(pb, HALO), lo:hi].astype(F32)
    after = z_ref[pl.ds(pa, HALO), lo:hi].astype(F32)
    before = jnp.where(c0 > 0, before, 0.0)
    after = jnp.where(c0 + n < t_len, after, 0.0)
    return jnp.concatenate([before, cur, after], axis=0)


def _head_sum(x, ones_bd):
    n = x.shape[0]
    groups = x.shape[1] // LANES
    xs = jnp.concatenate([x[:, g * LANES:(g + 1) * LANES] for g in range(groups)], axis=0)
    s = _dot_exact_rhs(xs, ones_bd)
    return jnp.concatenate([s[g * n:(g + 1) * n] for g in range(groups)], axis=1)


def _mod_kernel(c_ref, w_ref, b_ref, o_ref):
    s = _silu(c_ref[...])
    o_ref[...] = _dot_f32(s, w_ref[...]) + b_ref[...]


def _modulation(cond, w_mod, b_mod):
    rows = cond.shape[0]
    tn = 1536
    out = pl.pallas_call(
        _mod_kernel,
        grid=(DEPTH, N_MOD * D_MODEL // tn),
        in_specs=[
            pl.BlockSpec((rows, D_MODEL), lambda l, j: (0, 0)),
            pl.BlockSpec((None, D_MODEL, tn), lambda l, j: (l, 0, j)),
            pl.BlockSpec((None, 1, tn), lambda l, j: (l, 0, j)),
        ],
        out_specs=pl.BlockSpec((None, rows, tn), lambda l, j: (l, 0, j)),
        out_shape=jax.ShapeDtypeStruct((DEPTH, rows, N_MOD * D_MODEL), F32),
        compiler_params=_ARB2,
    )(cond, w_mod, b_mod.reshape(DEPTH, 1, N_MOD * D_MODEL))
    return out.reshape(DEPTH, rows, N_MOD, D_MODEL)


def _inproj_kernel(*refs, n_out, col_chunk):
    x_ref, mod_ref, nw_ref = refs[:3]
    w_refs = refs[3:3 + n_out]
    z_refs = refs[3 + n_out:]
    h = _rms_norm(x_ref[...], nw_ref[...])
    h = h * (1.0 + mod_ref[1:2, :]) + mod_ref[0:1, :]
    hb = h.astype(BF16)
    for w_ref, z_ref in zip(w_refs, z_refs):
        n = w_ref.shape[1]
        for c in range(0, n, col_chunk):
            e = min(c + col_chunk, n)
            z_ref[:, c:e] = _dot(hb, w_ref[:, c:e]).astype(z_ref.dtype)


def _in_proj(x, mod, rows_per_mod, mod_base, norm_w, weights):
    m = x.shape[0]
    tm = ROW_BLOCK
    n_out = len(weights)
    const = lambda i: (0, 0)
    in_specs = [
        pl.BlockSpec((tm, D_MODEL), lambda i: (i, 0)),
        pl.BlockSpec((None, N_MOD, D_MODEL), lambda i: (mod_base + (i * tm) // rows_per_mod, 0, 0)),
        pl.BlockSpec((1, D_MODEL), const),
    ] + [pl.BlockSpec(w.shape, const, pipeline_mode=pl.Buffered(1)) for w in weights]
    out_specs = [pl.BlockSpec((tm, w.shape[1]), lambda i: (i, 0)) for w in weights]
    out_shape = [jax.ShapeDtypeStruct((m, w.shape[1]), Z_DTYPE) for w in weights]
    return pl.pallas_call(
        functools.partial(_inproj_kernel, n_out=n_out, col_chunk=512),
        grid=(m // tm,),
        in_specs=in_specs,
        out_specs=out_specs,
        out_shape=out_shape,
        compiler_params=_ARB1,
    )(x, mod, norm_w.reshape(1, D_MODEL), *weights)


def _outmlp_kernel(*refs, n_y, final, ff_chunk):
    y_refs = refs[:n_y]
    x_ref, mod_ref, nw_ref, wo_ref, w1_ref, w2_ref, fw_ref, o_ref = refs[n_y:]
    y = None
    row = 0
    for y_ref in y_refs:
        k = y_ref.shape[1]
        part = _dot(y_ref[...], wo_ref[row:row + k, :])
        y = part if y is None else y + part
        row += k
    x = x_ref[...] + mod_ref[2:3, :] * y
    h = _rms_norm(x, nw_ref[...]) * (1.0 + mod_ref[4:5, :]) + mod_ref[3:4, :]
    hb = h.astype(BF16)
    acc = None
    for c in range(0, D_FF, ff_chunk):
        u = jnp.maximum(_dot(hb, w1_ref[:, c:c + ff_chunk]), 0.0)
        part = _dot((u * u).astype(BF16), w2_ref[c:c + ff_chunk, :])
        acc = part if acc is None else acc + part
    x = x + mod_ref[5:6, :] * acc
    if final:
        x = _rms_norm(x, fw_ref[...])
    o_ref[...] = x


def _out_mlp(ys, x, mod, rows_per_mod, mod_base, norm_w, w_out, w1, w2, final_w, final):
    m = x.shape[0]
    tm = ROW_BLOCK
    const = lambda i: (0, 0)
    single = dict(pipeline_mode=pl.Buffered(1))
    in_specs = [pl.BlockSpec((tm, y.shape[1]), lambda i: (i, 0)) for y in ys] + [
        pl.BlockSpec((tm, D_MODEL), lambda i: (i, 0)),
        pl.BlockSpec((None, N_MOD, D_MODEL), lambda i: (mod_base + (i * tm) // rows_per_mod, 0, 0)),
        pl.BlockSpec((1, D_MODEL), const),
        pl.BlockSpec(w_out.shape, const, **single),
        pl.BlockSpec(w1.shape, const, **single),
        pl.BlockSpec(w2.shape, const, **single),
        pl.BlockSpec((1, D_MODEL), const),
    ]
    return pl.pallas_call(
        functools.partial(_outmlp_kernel, n_y=len(ys), final=final, ff_chunk=1024),
        grid=(m // tm,),
        in_specs=in_specs,
        out_specs=pl.BlockSpec((tm, D_MODEL), lambda i: (i, 0)),
        out_shape=jax.ShapeDtypeStruct((m, D_MODEL), F32),
        compiler_params=_ARB1,
    )(*ys, x, mod, norm_w.reshape(1, D_MODEL), w_out, w1, w2, final_w.reshape(1, D_MODEL))


def _even_kernel(z_ref, h0_ref, pw_ref, ps_ref, cw_ref, cb_ref, wg_ref, bg_ref, lam_ref,
                 y_ref, hfin_ref, rowbuf, cbuf, a_s, b_s, h_s, hf_s, *, t_len, grid):
    seg = GRID_W if grid else t_len
    n_seg = t_len // seg
    n_groups = len(POOL_WINDOWS)
    pad = 8

    rowbuf[...] = jnp.zeros(rowbuf.shape, F32)
    pos = lax.broadcasted_iota(jnp.int32, (seg, POOL_GW), 0)

    def pool_segment(r):
        base = r * seg if isinstance(r, int) else pl.multiple_of(r * seg, seg)
        for g, w in enumerate(POOL_WINDOWS):
            lo_off = w // 2
            hi_off = w - 1 - lo_off
            cols = slice(g * POOL_GW, (g + 1) * POOL_GW)
            xg = z_ref[pl.ds(base, seg), cols].astype(F32)
            if grid:
                s1 = None
                for o in range(-lo_off, hi_off + 1):
                    rr = r + o
                    valid = jnp.logical_and(rr >= 0, rr < n_seg)
                    src = pl.multiple_of(jnp.clip(rr, 0, n_seg - 1) * seg, seg)
                    term = jnp.where(valid, z_ref[pl.ds(src, seg), cols].astype(F32), 0.0)
                    s1 = term if s1 is None else s1 + term
                cnt_r = jnp.minimum(r + hi_off, n_seg - 1) - jnp.maximum(r - lo_off, 0) + 1
                m1 = s1 / cnt_r.astype(F32)
            else:
                m1 = xg
            rowbuf[g, pad:pad + seg, :] = m1
            s2 = None
            for o in range(-lo_off, hi_off + 1):
                term = rowbuf[g, pad + o:pad + o + seg, :]
                s2 = term if s2 is None else s2 + term
            cnt_c = jnp.minimum(pos + hi_off, seg - 1) - jnp.maximum(pos - lo_off, 0) + 1
            d = s2 / cnt_c.astype(F32) - xg
            yg = _dot(d.astype(BF16), pw_ref[g]) * ps_ref[:, cols]
            y_ref[pl.ds(base, seg), cols] = yg.astype(y_ref.dtype)

    if n_seg == 1:
        pool_segment(0)
    else:
        def seg_body(r, carry):
            pool_segment(r)
            return carry
        lax.fori_loop(0, n_seg, seg_body, 0)

    ch = RG_CHUNK
    n_ch = t_len // ch
    rg_lo, rg_hi = D_POOL, D_POOL + D_RG

    def gates(c0, d):
        cbuf[...] = _load_ext(z_ref, c0, ch, t_len, rg_lo, rg_hi)
        xc = cb_ref[...]
        for j in range(CONV_W):
            off = HALO - CONV_LEFT + j
            xc = xc + cbuf[off:off + ch, :] * cw_ref[j:j + 1, :]
        g = _dot(xc.astype(BF16), wg_ref[:, d * 2 * D_RG:(d + 1) * 2 * D_RG]) \
            + bg_ref[:, d * 2 * D_RG:(d + 1) * 2 * D_RG]
        r = _sigmoid(g[:, :D_RG])
        i = _sigmoid(g[:, D_RG:])
        log_a = (-RG_C) * r * _softplus(-lam_ref[d:d + 1, :])
        a = jnp.exp(log_a)
        a_s[...] = a
        b_s[...] = jnp.sqrt(jnp.tanh(-log_a) * (a * a + 1.0)) * (i * xc)

    def scan_rows(h, reverse):
        def row(t, h):
            tt = (ch - 1 - t) if reverse else t
            h = a_s[pl.ds(tt, 1), :] * h + b_s[pl.ds(tt, 1), :]
            h_s[pl.ds(tt, 1), :] = h
            return h
        return lax.fori_loop(0, ch, row, h, unroll=8)

    def fwd_body(i, h):
        c0 = pl.multiple_of(i * ch, ch)
        gates(c0, 0)
        h = scan_rows(h, False)
        hf_s[pl.ds(c0, ch), :] = h_s[...]
        return h

    h_fwd = lax.fori_loop(0, n_ch, fwd_body, h0_ref[0:1, :])

    def bwd_body(i, h):
        c0 = pl.multiple_of((n_ch - 1 - i) * ch, ch)
        gates(c0, 1)
        h = scan_rows(h, True)
        u_gate = z_ref[pl.ds(c0, ch), rg_hi:rg_hi + D_RG].astype(F32)
        y = (hf_s[pl.ds(c0, ch), :] + h_s[...]) * _gelu_tanh(u_gate)
        y_ref[pl.ds(c0, ch), D_POOL:D_POOL + D_RG] = y.astype(y_ref.dtype)
        return h

    h_bwd = lax.fori_loop(0, n_ch, bwd_body, h0_ref[1:2, :])
    hfin_ref[0:1, :] = h_fwd
    hfin_ref[1:2, :] = h_bwd


def _even_mixer(z, h0, p, n_batch, t_len, grid):
    seg = GRID_W if grid else t_len
    const2 = lambda b: (0, 0)
    const3 = lambda b: (0, 0, 0)
    return pl.pallas_call(
        functools.partial(_even_kernel, t_len=t_len, grid=grid),
        grid=(n_batch,),
        in_specs=[
            pl.BlockSpec((t_len, EV_COLS), lambda b: (b, 0)),
            pl.BlockSpec((None, 2, D_RG), lambda b: (b, 0, 0)),
            pl.BlockSpec(p['pool_w'].shape, const3),
            pl.BlockSpec((1, D_POOL), const2),
            pl.BlockSpec((CONV_W, D_RG), const2),
            pl.BlockSpec((1, D_RG), const2),
            pl.BlockSpec((D_RG, 4 * D_RG), const2),
            pl.BlockSpec((1, 4 * D_RG), const2),
            pl.BlockSpec((2, D_RG), const2),
        ],
        out_specs=[
            pl.BlockSpec((t_len, D_POOL + D_RG), lambda b: (b, 0)),
            pl.BlockSpec((None, 2, D_RG), lambda b: (b, 0, 0)),
        ],
        out_shape=[
            jax.ShapeDtypeStruct((n_batch * t_len, D_POOL + D_RG), Y_DTYPE),
            jax.ShapeDtypeStruct((n_batch, 2, D_RG), F32),
        ],
        scratch_shapes=[
            pltpu.VMEM((len(POOL_WINDOWS), seg + 16, POOL_GW), F32),
            pltpu.VMEM((RG_CHUNK + 2 * HALO, D_RG), F32),
            pltpu.VMEM((RG_CHUNK, D_RG), F32),
            pltpu.VMEM((RG_CHUNK, D_RG), F32),
            pltpu.VMEM((RG_CHUNK, D_RG), F32),
            pltpu.VMEM((t_len, D_RG), F32),
        ],
        compiler_params=_ARB1,
    )(z, h0, p['pool_w'], p['pool_scale'], p['conv_w'], p['conv_b'], p['wg'], p['bg'], p['lam'])


def _rwkv_kernel(z_ref, s0_ref, mu_ref, kkw_ref, ka_ref, rk_ref, g2_ref, w0_ref, w2_ref, a0_ref, a2_ref,
                 lnw_ref, lnb_ref, tri_ref, bd_ref,
                 y_ref, sfin_ref, ebuf, s_s, yd_s, bon_s, g_s, *, t_len):
    L = SCAN_CHUNK
    n_ch = t_len // L
    L2 = 2 * L
    s_s[...] = s0_ref[...]


    def stack(x):
        first_head = lax.broadcasted_iota(jnp.int32, (L, LANES), 1) < RW_HEAD
        return jnp.concatenate([jnp.where(first_head, x, 0.0), jnp.where(first_head, 0.0, x)],
                               axis=0).astype(BF16)

    def prep(c0, d):
        ones_bd = bd_ref[...]
        ebuf[d] = _load_ext(z_ref, c0, L, t_len, 0, RW_ZCOLS)
        cur = ebuf[d, HALO:HALO + L, :]
        prev = ebuf[d, HALO - 1:HALO - 1 + L, :]
        nxt = ebuf[d, HALO + 1:HALO + 1 + L, :]
        zs = cur + mu_ref[0:1, :] * (prev - cur) + mu_ref[1:2, :] * (nxt - cur)
        r = zs[:, 0:D_RW]
        k = zs[:, D_RW:2 * D_RW]
        v = zs[:, 2 * D_RW:3 * D_RW]
        o = 3 * D_RW
        gd = zs[:, o:o + RW_G_LORA]
        wd = zs[:, o + RW_G_LORA:o + RW_G_LORA + LANES]
        ad = zs[:, o + RW_G_LORA + LANES:o + RW_G_LORA + 2 * LANES]

        kkv = k * kkw_ref[...]
        nrm = jnp.sqrt(_head_sum(kkv * kkv, ones_bd))
        kkn = kkv / jnp.maximum(nrm, 1e-12)

        w_log = -_softplus(-(w0_ref[d:d + 1, :] + _dot_f32(jnp.tanh(wd), w2_ref[d]))) - 0.5
        lw = -jnp.exp(w_log)
        a = _sigmoid(a0_ref[d:d + 1, :] + _dot_f32(ad, a2_ref[d]))
        kd = k * (1.0 + (a - 1.0) * ka_ref[...])
        bon_s[d, pl.ds(c0, L), :] = _head_sum(r * kd * rk_ref[...], ones_bd) * v
        if d == 0:
            g_s[pl.ds(c0, L), :] = _dot(_sigmoid(gd).astype(BF16), g2_ref[...])

        cum = _dot_exact_lhs(tri_ref[d], lw)
        e_pos = jnp.exp(cum)
        e_neg = jnp.exp(-cum)
        rh = r * e_pos
        kh = kd * e_neg
        bh = (kkn * a) * e_neg
        ah = -kkn * jnp.exp(cum - lw)
        last = L - 1 if d == 0 else 0
        gam = e_pos[last:last + 1, :]
        bhg = bh * gam
        khg = kh * gam

        chains = []
        for p in range(RW_PAIRS):
            cols = slice(p * LANES, (p + 1) * LANES)
            chains.append(dict(
                d=d, p=p, gam=gam[:, cols], v_st=stack(v[:, cols]),
                lhs=jnp.concatenate([stack(ah[:, cols]), stack(rh[:, cols])], axis=0),
                rhs=jnp.concatenate([stack(bh[:, cols]), stack(kh[:, cols])], axis=0),
                bk=jnp.concatenate([stack(bhg[:, cols]), stack(khg[:, cols])], axis=0)))
        return chains

    n_levels = int(math.log2(L))

    def body(i, carry):
        c0s = (pl.multiple_of(i * L, L), pl.multiple_of((n_ch - 1 - i) * L, L))
        chains = prep(c0s[0], 0) + prep(c0s[1], 1)
        ri = lax.broadcasted_iota(jnp.int32, (L2, L2), 0)
        ci = lax.broadcasted_iota(jnp.int32, (L2, L2), 1)
        eye = jnp.where(ri == ci, 1.0, 0.0)
        rl = jnp.where(ri < L, ri, ri - L)
        cl = jnp.where(ci < L, ci, ci - L)
        strict = (cl < rl, cl > rl)
        incl = (cl <= rl, cl >= rl)
        level = ri ^ ci

        for ch in chains:
            ch['sc'] = _dot_nt(ch['lhs'], ch['rhs'])
        for ch in chains:
            sc, d = ch['sc'], ch['d']
            n_ab = jnp.where(strict[d], sc[0:L2, 0:L2], 0.0)
            m_ak = jnp.where(strict[d], sc[0:L2, L2:2 * L2], 0.0).astype(BF16)
            t_rb = jnp.where(incl[d], sc[L2:2 * L2, 0:L2], 0.0)
            t_rk = jnp.where(incl[d], sc[L2:2 * L2, L2:2 * L2], 0.0)
            ch['t_rbk'] = jnp.concatenate([t_rb, t_rk], axis=1).astype(BF16)
            ch['mv'] = _dot(m_ak, ch['v_st'])
            ch['t'] = eye + jnp.where(level == 1, n_ab, 0.0)
            ch['n_lv'] = [jnp.where(lax.shift_right_logical(level, lv) == 1, n_ab, 0.0).astype(BF16)
                          for lv in range(1, n_levels)]
            del ch['sc']
        for lv in range(n_levels - 1):
            for ch in chains:
                ch['tb'] = ch['t'].astype(BF16)
                ch['x'] = _dot(ch['n_lv'][lv], ch['tb']).astype(BF16)
            for ch in chains:
                ch['t'] = ch['t'] + _dot(ch['tb'], ch['x'])

        for ch in chains:
            ch['s_prev'] = s_s[ch['d'], ch['p']]
            ch['ar_s'] = _dot_nt(ch['lhs'], ch['s_prev'].astype(BF16))
        for ch in chains:
            u_rhs = ch['ar_s'][0:L2] + ch['mv']
            u_st = _dot(ch['t'].astype(BF16), u_rhs.astype(BF16)).astype(BF16)
            ch['uv'] = jnp.concatenate([u_st, ch['v_st']], axis=0)
        for ch in chains:
            y_st = ch['ar_s'][L2:2 * L2] + _dot(ch['t_rbk'], ch['uv'])
            ch['y'] = y_st[0:L] + y_st[L:L2]
            s_s[ch['d'], ch['p']] = ch['s_prev'] * ch['gam'] + _dot_tn(ch['uv'], ch['bk'])
        for d in range(2):
            yd_s[d, pl.ds(c0s[d], L), :] = jnp.concatenate([ch['y'] for ch in chains if ch['d'] == d], axis=1)
        return carry

    lax.fori_loop(0, n_ch, body, 0)

    def epilogue(i, carry):
        c0 = pl.multiple_of(i * L, L)
        ones_bd = bd_ref[...]
        ys = yd_s[0, pl.ds(c0, L), :] + yd_s[1, pl.ds(c0, L), :]
        mu = _head_sum(ys, ones_bd) * (1.0 / RW_HEAD)
        yc = ys - mu
        var = _head_sum(yc * yc, ones_bd) * (1.0 / RW_HEAD)
        yn = yc * lax.rsqrt(var + RW_LN_EPS) * lnw_ref[...] + lnb_ref[...]
        out = (yn + (bon_s[0, pl.ds(c0, L), :] + bon_s[1, pl.ds(c0, L), :])) * g_s[pl.ds(c0, L), :]
        y_ref[pl.ds(c0, L), :] = out.astype(y_ref.dtype)
        return carry

    lax.fori_loop(0, n_ch, epilogue, 0)
    sfin_ref[...] = s_s[...]


def _rwkv_mixer(z, s0, p, n_batch, t_len):
    L = SCAN_CHUNK
    c2 = lambda b: (0, 0)
    c3 = lambda b: (0, 0, 0)
    st_spec = pl.BlockSpec((None, 2, RW_PAIRS, LANES, LANES), lambda b: (b, 0, 0, 0, 0))
    return pl.pallas_call(
        functools.partial(_rwkv_kernel, t_len=t_len),
        grid=(n_batch,),
        in_specs=[
            pl.BlockSpec((t_len, RW_ZCOLS), lambda b: (b, 0)),
            st_spec,
            pl.BlockSpec((2, RW_ZCOLS), c2),
            pl.BlockSpec((1, D_RW), c2),
            pl.BlockSpec((1, D_RW), c2),
            pl.BlockSpec((1, D_RW), c2),
            pl.BlockSpec((RW_G_LORA, D_RW), c2),
            pl.BlockSpec((2, D_RW), c2),
            pl.BlockSpec((2, LANES, D_RW), c3),
            pl.BlockSpec((2, D_RW), c2),
            pl.BlockSpec((2, LANES, D_RW), c3),
            pl.BlockSpec((1, D_RW), c2),
            pl.BlockSpec((1, D_RW), c2),
            pl.BlockSpec((2, L, L), c3),
            pl.BlockSpec((LANES, LANES), c2),
        ],
        out_specs=[pl.BlockSpec((t_len, D_RW), lambda b: (b, 0)), st_spec],
        out_shape=[
            jax.ShapeDtypeStruct((n_batch * t_len, D_RW), Y_DTYPE),
            jax.ShapeDtypeStruct((n_batch, 2, RW_PAIRS, LANES, LANES), F32),
        ],
        scratch_shapes=[
            pltpu.VMEM((2, L + 2 * HALO, RW_ZCOLS), F32),
            pltpu.VMEM((2, RW_PAIRS, LANES, LANES), F32),
            pltpu.VMEM((2, t_len, D_RW), F32),
            pltpu.VMEM((2, t_len, D_RW), F32),
            pltpu.VMEM((t_len, D_RW), F32),
        ],
        compiler_params=_ARB1,
    )(z, s0, p['mu'], p['kk'], p['ka'], p['rk'], p['g2'], p['w0'], p['w2'], p['a0'], p['a2'],
      p['ln_w'], p['ln_b'], p['tri'], p['ones_bd'])


def _mlstm_kernel(z_ref, c0_ref, n0_ref, m0_ref, cw_ref, cb_ref, gb_ref, nw_ref, tri_ref,
                  y_ref, cfin_ref, nfin_ref, mfin_ref, cbuf, c_s, n_s, m_s, hd_s, *, t_len):
    L = SCAN_CHUNK
    n_ch = t_len // L
    c_s[...] = c0_ref[...]
    n_s[...] = n0_ref[...]
    m_s[...] = m0_ref[...]
    q_scale = ML_DK ** -0.5
    g_lo = 4 * D_ML

    def chunk_dir(c0, d):
        ri = lax.broadcasted_iota(jnp.int32, (L, L), 0)
        ci = lax.broadcasted_iota(jnp.int32, (L, L), 1)
        causal = (ci <= ri) if d == 0 else (ci >= ri)
        cbuf[d] = _load_ext(z_ref, c0, L, t_len, 0, 2 * D_ML)
        qk = cb_ref[...]
        for j in range(CONV_W):
            off = HALO - CONV_LEFT + j
            qk = qk + cbuf[d, off:off + L, :] * cw_ref[j:j + 1, :]
        qk = _silu(qk)
        q = qk[:, :D_ML] * q_scale
        k = qk[:, D_ML:]
        v = z_ref[pl.ds(c0, L), 2 * D_ML:3 * D_ML].astype(F32)
        gts = z_ref[pl.ds(c0, L), g_lo:g_lo + LANES].astype(F32) + gb_ref[...]
        lf_all = -_softplus(-gts)
        bcum = _dot_exact_lhs(tri_ref[d], lf_all)
        both_t = jnp.concatenate([gts, bcum], axis=0).T
        last = L - 1 if d == 0 else 0
        outs = []
        for h in range(ML_HEADS):
            col_i = d * 2 * ML_HEADS + h
            col_f = col_i + ML_HEADS
            cols = slice(h * ML_DK, (h + 1) * ML_DK)
            b_col = bcum[:, col_f:col_f + 1]
            li_col = gts[:, col_i:col_i + 1]
            b_row = both_t[col_f:col_f + 1, L:2 * L]
            li_row = both_t[col_i:col_i + 1, 0:L]
            m_prev = m_s[d, h:h + 1, 0:1]
            log_d = jnp.where(causal, b_col - b_row + li_row, -jnp.inf)
            log_inter = b_col + m_prev
            m_t = jnp.maximum(log_inter, jnp.max(log_d, axis=1, keepdims=True))
            qb = q[:, cols].astype(BF16)
            kb = k[:, cols].astype(BF16)
            vh = v[:, cols]
            s = _dot_nt(qb, kb) * jnp.exp(log_d - m_t)
            w_inter = jnp.exp(log_inter - m_t)
            c_prev = c_s[d, h]
            n_prev = n_s[d, h:h + 1, :]
            num = _dot(s.astype(BF16), vh.astype(BF16)) + w_inter * _dot(qb, c_prev.astype(BF16))
            den = jnp.sum(s, axis=1, keepdims=True) \
                + w_inter * jnp.sum(q[:, cols] * n_prev, axis=1, keepdims=True)
            outs.append(num / jnp.maximum(jnp.abs(den), jnp.exp(-m_t)))
            m_new = m_t[last:last + 1, :]
            b_last = b_col[last:last + 1, :]
            w_s = jnp.exp(b_last - b_col + li_col - m_new)
            dec = jnp.exp(b_last + m_prev - m_new)
            c_s[d, h] = dec * c_prev + _dot_tn(kb, (w_s * vh).astype(BF16))
            n_s[d, h:h + 1, :] = dec * n_prev + jnp.sum(w_s * k[:, cols], axis=0, keepdims=True)
            m_s[d, h:h + 1, :] = jnp.broadcast_to(m_new, (1, LANES))
        hd_s[d, pl.ds(c0, L), :] = jnp.concatenate(outs, axis=1)

    def body(i, carry):
        chunk_dir(pl.multiple_of(i * L, L), 0)
        chunk_dir(pl.multiple_of((n_ch - 1 - i) * L, L), 1)
        return carry

    lax.fori_loop(0, n_ch, body, 0)

    def epilogue(i, carry):
        c0 = pl.multiple_of(i * L, L)
        hs = hd_s[0, pl.ds(c0, L), :] + hd_s[1, pl.ds(c0, L), :]
        o = z_ref[pl.ds(c0, L), 3 * D_ML:4 * D_ML].astype(F32)
        outs = []
        for h in range(ML_HEADS):
            x = hs[:, h * ML_DK:(h + 1) * ML_DK]
            mu = jnp.mean(x, axis=1, keepdims=True)
            xc = x - mu
            var = jnp.mean(xc * xc, axis=1, keepdims=True)
            outs.append(xc * lax.rsqrt(var + NORM_EPS))
        y = jnp.concatenate(outs, axis=1) * nw_ref[...] * _sigmoid(o)
        y_ref[pl.ds(c0, L), :] = y.astype(y_ref.dtype)
        return carry

    lax.fori_loop(0, n_ch, epilogue, 0)
    cfin_ref[...] = c_s[...]
    nfin_ref[...] = n_s[...]
    mfin_ref[...] = m_s[...]


def _mlstm_mixer(z, c0, n0, m0, p, n_batch, t_len):
    L = SCAN_CHUNK
    c2 = lambda b: (0, 0)
    c3 = lambda b: (0, 0, 0)
    c_spec = pl.BlockSpec((None, 2, ML_HEADS, ML_DK, ML_DK), lambda b: (b, 0, 0, 0, 0))
    n_spec = pl.BlockSpec((None, 2, ML_HEADS, ML_DK), lambda b: (b, 0, 0, 0))
    return pl.pallas_call(
        functools.partial(_mlstm_kernel, t_len=t_len),
        grid=(n_batch,),
        in_specs=[
            pl.BlockSpec((t_len, ML_ZCOLS), lambda b: (b, 0)),
            c_spec, n_spec, n_spec,
            pl.BlockSpec((CONV_W, 2 * D_ML), c2),
            pl.BlockSpec((1, 2 * D_ML), c2),
            pl.BlockSpec((1, LANES), c2),
            pl.BlockSpec((1, D_ML), c2),
            pl.BlockSpec((2, L, L), c3),
        ],
        out_specs=[pl.BlockSpec((t_len, D_ML), lambda b: (b, 0)), c_spec, n_spec, n_spec],
        out_shape=[
            jax.ShapeDtypeStruct((n_batch * t_len, D_ML), Y_DTYPE),
            jax.ShapeDtypeStruct((n_batch, 2, ML_HEADS, ML_DK, ML_DK), F32),
            jax.ShapeDtypeStruct((n_batch, 2, ML_HEADS, ML_DK), F32),
            jax.ShapeDtypeStruct((n_batch, 2, ML_HEADS, ML_DK), F32),
        ],
        scratch_shapes=[
            pltpu.VMEM((2, L + 2 * HALO, 2 * D_ML), F32),
            pltpu.VMEM((2, ML_HEADS, ML_DK, ML_DK), F32),
            pltpu.VMEM((2, ML_HEADS, ML_DK), F32),
            pltpu.VMEM((2, ML_HEADS, ML_DK), F32),
            pltpu.VMEM((2, t_len, D_ML), F32),
        ],
        compiler_params=_ARB1,
    )(z, c0, n0, m0, p['conv_w'], p['conv_b'], p['gate_b'], p['norm_w'], p['tri'])


def _block_diag(w):
    n, c, d = w.shape
    return jnp.einsum('ncd,nm->ncmd', w, jnp.eye(n, dtype=w.dtype)).reshape(n * c, n * d)


def _scan_tri():
    L = SCAN_CHUNK
    lower = jnp.tril(jnp.ones((L, L), F32))
    return jnp.stack([lower, lower.T]).astype(BF16)


def _prep_even(ev_w_in, pool_w, pool_scale, rg_conv_w, rg_conv_b, rg_wa, rg_ba, rg_wx, rg_bx, rg_lam, ev_w_out, j):
    wg = jnp.concatenate([_block_diag(rg_wa[j, 0]), _block_diag(rg_wx[j, 0]),
                          _block_diag(rg_wa[j, 1]), _block_diag(rg_wx[j, 1])], axis=1)
    bg = jnp.concatenate([rg_ba[j, 0], rg_bx[j, 0], rg_ba[j, 1], rg_bx[j, 1]]).reshape(1, 4 * D_RG)
    return dict(
        w_in=[ev_w_in[j].astype(BF16)],
        w_out=ev_w_out[j].astype(BF16),
        pool_w=pool_w[j].astype(BF16),
        pool_scale=pool_scale[j].reshape(1, D_POOL),
        conv_w=rg_conv_w[j],
        conv_b=rg_conv_b[j].reshape(1, D_RG),
        wg=wg.astype(BF16),
        bg=bg,
        lam=rg_lam[j],
    )


def _pad_cols(w, cuts, widths):
    pieces = jnp.split(w, cuts, axis=-1)
    out = []
    for piece, width in zip(pieces, widths):
        extra = width - piece.shape[-1]
        if extra:
            piece = jnp.pad(piece, [(0, 0)] * (piece.ndim - 1) + [(0, extra)])
        out.append(piece)
    return jnp.concatenate(out, axis=-1)


def _prep_odd(od_w_in, rw_mu, rw_w0, rw_w2, rw_a0, rw_a2, rw_kk, rw_ka, rw_rk, rw_g2, rw_ln_w, rw_ln_b,
              ml_conv_w, ml_conv_b, ml_bi, ml_bf, ml_norm_w, od_w_out, j):
    rw_cols = 3 * D_RW + RW_G_LORA + RW_W_LORA + RW_A_LORA
    base = 3 * D_RW + RW_G_LORA
    cuts = [base, base + RW_W_LORA]
    widths = [base, LANES, LANES]
    w_rw = _pad_cols(od_w_in[j][:, :rw_cols], cuts, widths)
    w_ml = _pad_cols(od_w_in[j][:, rw_cols:], [4 * D_ML], [4 * D_ML, LANES])
    pad_rows = lambda w: jnp.pad(w, ((0, 0), (0, LANES - w.shape[1]), (0, 0)))
    gate_b = jnp.concatenate([ml_bi[j, 0], ml_bf[j, 0], ml_bi[j, 1], ml_bf[j, 1]])
    gate_b = jnp.pad(gate_b, (0, LANES - gate_b.shape[0])).reshape(1, LANES)
    hm = jnp.arange(LANES) // RW_HEAD
    rw = dict(
        mu=_pad_cols(rw_mu[j], cuts, widths),
        kk=rw_kk[j].reshape(1, D_RW), ka=rw_ka[j].reshape(1, D_RW), rk=rw_rk[j].reshape(1, D_RW),
        g2=rw_g2[j].astype(BF16),
        w0=rw_w0[j], w2=pad_rows(rw_w2[j]), a0=rw_a0[j], a2=pad_rows(rw_a2[j]),
        ln_w=rw_ln_w[j].reshape(1, D_RW), ln_b=rw_ln_b[j].reshape(1, D_RW),
        tri=_scan_tri(),
        ones_bd=(hm[:, None] == hm[None, :]).astype(BF16),
    )
    ml = dict(
        conv_w=ml_conv_w[j], conv_b=ml_conv_b[j].reshape(1, 2 * D_ML),
        gate_b=gate_b, norm_w=ml_norm_w[j].reshape(1, D_ML), tri=_scan_tri(),
    )
    return dict(w_in=[w_rw.astype(BF16), w_ml.astype(BF16)], w_out=od_w_out[j].astype(BF16), rw=rw, ml=ml)


def _rwkv_state_in(s):
    b = s.shape[0]
    s = s.reshape(b, 2, RW_PAIRS, 2, RW_HEAD, RW_HEAD)
    bd = jnp.einsum('bdpivk,ij->bdpivjk', s, jnp.eye(2, dtype=s.dtype))
    return bd.reshape(b, 2, RW_PAIRS, LANES, LANES)


def _rwkv_state_out(bd):
    b = bd.shape[0]
    x = bd.reshape(b, 2, RW_PAIRS, 2, RW_HEAD, 2, RW_HEAD)
    s = jnp.stack([x[:, :, :, 0, :, 0, :], x[:, :, :, 1, :, 1, :]], axis=3)
    return s.reshape(b, 2, RW_HEADS, RW_HEAD, RW_HEAD)


def _trunk(x, mod, rows_per_mod, mod_base, grid, states, params, n_batch, t_len):
    ev, od, dense = params
    rg0, rw0, c0, n0, m0 = states
    x = x.reshape(n_batch * t_len, D_MODEL)
    spec = (rows_per_mod, mod_base)

    (z,) = _in_proj(x, mod[0], *spec, dense['norm1_w'][0], ev['w_in'])
    y_ev, rg_f = _even_mixer(z, rg0, ev, n_batch, t_len, grid)
    x = _out_mlp([y_ev], x, mod[0], *spec, dense['norm2_w'][0], ev['w_out'],
                 dense['mlp_w1'][0], dense['mlp_w2'][0], dense['final_norm_w'], False)

    z_rw, z_ml = _in_proj(x, mod[1], *spec, dense['norm1_w'][1], od['w_in'])
    y_rw, s_f = _rwkv_mixer(z_rw, rw0, od['rw'], n_batch, t_len)
    y_ml, c_f, n_f, m_f = _mlstm_mixer(z_ml, c0, n0, m0, od['ml'], n_batch, t_len)
    y = _out_mlp([y_rw, y_ml], x, mod[1], *spec, dense['norm2_w'][1], od['w_out'],
                 dense['mlp_w1'][1], dense['mlp_w2'][1], dense['final_norm_w'], True)
    return y.reshape(n_batch, t_len, D_MODEL), (rg_f, s_f, c_f, n_f, m_f)


def kernel(x_prompt, x_sample, state_rglru, state_rwkv, state_mlstm_C, state_mlstm_n, state_mlstm_m,
           c, c_ctx, norm1_w, norm2_w, w_mod, b_mod, mlp_w1, mlp_w2, final_norm_w,
           ev_w_in, pool_w, pool_scale, rg_conv_w, rg_conv_b, rg_wa, rg_ba, rg_wx, rg_bx, rg_lam, ev_w_out,
           od_w_in, rw_mu, rw_w0, rw_w2, rw_a0, rw_a2, rw_kk, rw_ka, rw_rk, rw_g2, rw_ln_w, rw_ln_b,
           ml_conv_w, ml_conv_b, ml_bi, ml_bf, ml_norm_w, od_w_out):
    bp, tp, _ = x_prompt.shape
    bs, ts, _ = x_sample.shape

    n_cond = 16
    cond = jnp.concatenate([c_ctx[None, :], c, jnp.zeros((n_cond - 1 - bs, D_MODEL), F32)], axis=0)
    mod = _modulation(cond, w_mod, b_mod)

    ev = _prep_even(ev_w_in, pool_w, pool_scale, rg_conv_w, rg_conv_b, rg_wa, rg_ba, rg_wx, rg_bx, rg_lam,
                    ev_w_out, 0)
    od = _prep_odd(od_w_in, rw_mu, rw_w0, rw_w2, rw_a0, rw_a2, rw_kk, rw_ka, rw_rk, rw_g2, rw_ln_w, rw_ln_b,
                   ml_conv_w, ml_conv_b, ml_bi, ml_bf, ml_norm_w, od_w_out, 0)
    dense = dict(norm1_w=norm1_w, norm2_w=norm2_w, mlp_w1=mlp_w1.astype(BF16), mlp_w2=mlp_w2.astype(BF16),
                 final_norm_w=final_norm_w)
    params = (ev, od, dense)

    zero_states = (
        jnp.zeros((bp, 2, D_RG), F32),
        jnp.zeros((bp, 2, RW_PAIRS, LANES, LANES), F32),
        jnp.zeros((bp, 2, ML_HEADS, ML_DK, ML_DK), F32),
        jnp.zeros((bp, 2, ML_HEADS, ML_DK), F32),
        jnp.zeros((bp, 2, ML_HEADS, ML_DK), F32),
    )
    y_prompt, (rg_f, s_f, c_f, n_f, m_f) = _trunk(
        x_prompt, mod, bp * tp, 0, False, zero_states, params, bp, tp)

    sample_states = (
        state_rglru[:, 0],
        _rwkv_state_in(state_rwkv[:, 0]),
        state_mlstm_C[:, 0],
        state_mlstm_n[:, 0],
        jnp.broadcast_to(state_mlstm_m[:, 0][..., None], (bs, 2, ML_HEADS, ML_DK)),
    )
    y_sample, _ = _trunk(x_sample, mod, ts, 1, True, sample_states, params, bs, ts)

    return (y_prompt, y_sample,
            rg_f[:, None],
            _rwkv_state_out(s_f)[:, None],
            c_f[:, None],
            n_f[:, None],
            m_f[:, None, :, :, 0])
```

```python
import functools
import math

import jax
import jax.numpy as jnp
from jax import lax
from jax.experimental import pallas as pl
from jax.experimental.pallas import tpu as pltpu

F32 = jnp.float32
BF16 = jnp.bfloat16

D_MODEL = 1024
DEPTH = 2
GRID_W = 64
D_FF = 4 * D_MODEL
NORM_EPS = 1e-6
N_MOD = 6

D_POOL = D_MODEL // 2
POOL_WINDOWS = (2, 4, 8, 16)
POOL_GW = D_POOL // len(POOL_WINDOWS)
D_RG = D_MODEL // 2
RG_BLOCKS = 8
RG_C = 8.0
CONV_W = 4
CONV_LEFT = 2
EV_COLS = D_POOL + 2 * D_RG

D_RW = D_MODEL // 2
RW_HEAD = 64
RW_HEADS = D_RW // RW_HEAD
RW_PAIRS = RW_HEADS // 2
RW_W_LORA = 64
RW_A_LORA = 64
RW_G_LORA = 128
RW_LN_EPS = 64e-5
RW_ZCOLS = 3 * D_RW + RW_G_LORA + 128 + 128
D_ML = D_MODEL // 2
ML_HEADS = 4
ML_DK = D_ML // ML_HEADS
ML_ZCOLS = 4 * D_ML + 128

LANES = 128
HALO = 16
Z_DTYPE = BF16
Y_DTYPE = BF16
ROW_BLOCK = 512
SCAN_CHUNK = 64
ML_SCAN = 128
RG_CHUNK = 256
VMEM_LIMIT = 56 * 1024 * 1024

_ARB1 = pltpu.CompilerParams(dimension_semantics=("arbitrary",), vmem_limit_bytes=VMEM_LIMIT)
_ARB2 = pltpu.CompilerParams(dimension_semantics=("arbitrary", "arbitrary"), vmem_limit_bytes=VMEM_LIMIT)


def _dot(a, b):
    return jnp.dot(a, b, preferred_element_type=F32)


def _dot_nt(a, b):
    return lax.dot_general(a, b, (((1,), (1,)), ((), ())), preferred_element_type=F32)


def _dot_tn(a, b):
    return lax.dot_general(a, b, (((0,), (0,)), ((), ())), preferred_element_type=F32)


def _split2(x):
    hi = x.astype(BF16)
    lo = (x - hi.astype(F32)).astype(BF16)
    return hi, lo


def _dot_f32(a, b):
    ah, al = _split2(a)
    bh, bl = _split2(b)
    return _dot(ah, bh) + (_dot(ah, bl) + _dot(al, bh))


def _dot_exact_lhs(m_bf16, x):
    hi, lo = _split2(x)
    return _dot(m_bf16, hi) + _dot(m_bf16, lo)


def _dot_exact_rhs(x, m_bf16):
    hi, lo = _split2(x)
    return _dot(hi, m_bf16) + _dot(lo, m_bf16)


def _softplus(x):
    return jnp.maximum(x, 0.0) + jnp.log1p(jnp.exp(-jnp.abs(x)))


def _sigmoid(x):
    return jax.nn.sigmoid(x)


def _silu(x):
    return x * jax.nn.sigmoid(x)


def _gelu_tanh(x):
    c = math.sqrt(2.0 / math.pi)
    return x * (0.5 * (1.0 + jnp.tanh(c * (x + 0.044715 * (x * x * x)))))


def _rms_norm(x, w):
    ms = jnp.mean(x * x, axis=-1, keepdims=True)
    return x * lax.rsqrt(ms + NORM_EPS) * w


def _load_ext(z_ref, c0, n, t_len, lo, hi, raw=False):
    dt = z_ref.dtype if raw else F32
    cur = z_ref[pl.ds(c0, n), lo:hi].astype(dt)
    pb = pl.multiple_of(jnp.maximum(c0 - HALO, 0), HALO)
    pa = pl.multiple_of(jnp.minimum(c0 + n, t_len - HALO), HALO)
    before = z_ref[pl.ds(pb, HALO), lo:hi].astype(dt)
    after = z_ref[pl.ds(pa, HALO), lo:hi].astype(dt)
    before = jnp.where(c0 > 0, before, jnp.zeros_like(before))
    after = jnp.where(c0 + n < t_len, after, jnp.zeros_like(after))
    return jnp.concatenate([before, cur, after], axis=0)


def _head_sum(x, ones_bd):
    n = x.shape[0]
    groups = x.shape[1] // LANES
    xs = jnp.concatenate([x[:, g * LANES:(g + 1) * LANES] for g in range(groups)], axis=0)
    s = _dot_exact_rhs(xs, ones_bd)
    return jnp.concatenate([s[g * n:(g + 1) * n] for g in range(groups)], axis=1)


def _mod_kernel(c_ref, w_ref, b_ref, o_ref):
    s = _silu(c_ref[...])
    o_ref[...] = _dot_f32(s, w_ref[...]) + b_ref[...]


def _modulation(cond, w_mod, b_mod):
    rows = cond.shape[0]
    tn = 1536
    out = pl.pallas_call(
        _mod_kernel,
        grid=(DEPTH, N_MOD * D_MODEL // tn),
        in_specs=[
            pl.BlockSpec((rows, D_MODEL), lambda l, j: (0, 0)),
            pl.BlockSpec((None, D_MODEL, tn), lambda l, j: (l, 0, j)),
            pl.BlockSpec((None, 1, tn), lambda l, j: (l, 0, j)),
        ],
        out_specs=pl.BlockSpec((None, rows, tn), lambda l, j: (l, 0, j)),
        out_shape=jax.ShapeDtypeStruct((DEPTH, rows, N_MOD * D_MODEL), F32),
        compiler_params=_ARB2,
    )(cond, w_mod, b_mod.reshape(DEPTH, 1, N_MOD * D_MODEL))
    return out.reshape(DEPTH, rows, N_MOD, D_MODEL)


def _inproj_kernel(*refs, n_out, col_chunk):
    x_ref, mod_ref, nw_ref = refs[:3]
    w_refs = refs[3:3 + n_out]
    z_refs = refs[3 + n_out:]
    h = _rms_norm(x_ref[...], nw_ref[...])
    h = h * (1.0 + mod_ref[1:2, :]) + mod_ref[0:1, :]
    hb = h.astype(BF16)
    for w_ref, z_ref in zip(w_refs, z_refs):
        n = w_ref.shape[1]
        for c in range(0, n, col_chunk):
            e = min(c + col_chunk, n)
            z_ref[:, c:e] = _dot(hb, w_ref[:, c:e]).astype(z_ref.dtype)


def _in_proj(x, mod, rows_per_mod, mod_base, norm_w, weights):
    m = x.shape[0]
    tm = ROW_BLOCK
    n_out = len(weights)
    const = lambda i: (0, 0)
    in_specs = [
        pl.BlockSpec((tm, D_MODEL), lambda i: (i, 0)),
        pl.BlockSpec((None, N_MOD, D_MODEL), lambda i: (mod_base + (i * tm) // rows_per_mod, 0, 0)),
        pl.BlockSpec((1, D_MODEL), const),
    ] + [pl.BlockSpec(w.shape, const, pipeline_mode=pl.Buffered(1)) for w in weights]
    out_specs = [pl.BlockSpec((tm, w.shape[1]), lambda i: (i, 0)) for w in weights]
    out_shape = [jax.ShapeDtypeStruct((m, w.shape[1]), Z_DTYPE) for w in weights]
    return pl.pallas_call(
        functools.partial(_inproj_kernel, n_out=n_out, col_chunk=512),
        grid=(m // tm,),
        in_specs=in_specs,
        out_specs=out_specs,
        out_shape=out_shape,
        compiler_params=_ARB1,
    )(x, mod, norm_w.reshape(1, D_MODEL), *weights)


def _outmlp_kernel(*refs, n_y, final, ff_chunk):
    y_refs = refs[:n_y]
    x_ref, mod_ref, nw_ref, wo_ref, w1_ref, w2_ref, fw_ref, o_ref = refs[n_y:]
    y = None
    row = 0
    for y_ref in y_refs:
        k = y_ref.shape[1]
        part = _dot(y_ref[...], wo_ref[row:row + k, :])
        y = part if y is None else y + part
        row += k
    x = x_ref[...] + mod_ref[2:3, :] * y
    h = _rms_norm(x, nw_ref[...]) * (1.0 + mod_ref[4:5, :]) + mod_ref[3:4, :]
    hb = h.astype(BF16)
    acc = None
    for c in range(0, D_FF, ff_chunk):
        u = jnp.maximum(_dot(hb, w1_ref[:, c:c + ff_chunk]), 0.0)
        part = _dot((u * u).astype(BF16), w2_ref[c:c + ff_chunk, :])
        acc = part if acc is None else acc + part
    x = x + mod_ref[5:6, :] * acc
    if final:
        x = _rms_norm(x, fw_ref[...])
    o_ref[...] = x


def _out_mlp(ys, x, mod, rows_per_mod, mod_base, norm_w, w_out, w1, w2, final_w, final):
    m = x.shape[0]
    tm = ROW_BLOCK
    const = lambda i: (0, 0)
    single = dict(pipeline_mode=pl.Buffered(1))
    in_specs = [pl.BlockSpec((tm, y.shape[1]), lambda i: (i, 0)) for y in ys] + [
        pl.BlockSpec((tm, D_MODEL), lambda i: (i, 0)),
        pl.BlockSpec((None, N_MOD, D_MODEL), lambda i: (mod_base + (i * tm) // rows_per_mod, 0, 0)),
        pl.BlockSpec((1, D_MODEL), const),
        pl.BlockSpec(w_out.shape, const, **single),
        pl.BlockSpec(w1.shape, const, **single),
        pl.BlockSpec(w2.shape, const, **single),
        pl.BlockSpec((1, D_MODEL), const),
    ]
    return pl.pallas_call(
        functools.partial(_outmlp_kernel, n_y=len(ys), final=final, ff_chunk=1024),
        grid=(m // tm,),
        in_specs=in_specs,
        out_specs=pl.BlockSpec((tm, D_MODEL), lambda i: (i, 0)),
        out_shape=jax.ShapeDtypeStruct((m, D_MODEL), F32),
        compiler_params=_ARB1,
    )(*ys, x, mod, norm_w.reshape(1, D_MODEL), w_out, w1, w2, final_w.reshape(1, D_MODEL))


def _even_kernel(z_ref, h0_ref, pw_ref, ps_ref, cw_ref, cb_ref, wg_ref, bg_ref, lam_ref,
                 y_ref, hfin_ref, rowbuf, cbuf, a_s, b_s, h_s, hf_s, *, t_len, grid):
    seg = GRID_W if grid else t_len
    n_seg = t_len // seg
    n_groups = len(POOL_WINDOWS)
    pad = 8

    rowbuf[...] = jnp.zeros(rowbuf.shape, F32)
    pos = lax.broadcasted_iota(jnp.int32, (seg, POOL_GW), 0)

    def pool_segment(r):
        base = r * seg if isinstance(r, int) else pl.multiple_of(r * seg, seg)
        for g, w in enumerate(POOL_WINDOWS):
            lo_off = w // 2
            hi_off = w - 1 - lo_off
            cols = slice(g * POOL_GW, (g + 1) * POOL_GW)
            xg = z_ref[pl.ds(base, seg), cols].astype(F32)
            if grid:
                s1 = None
                for o in range(-lo_off, hi_off + 1):
                    rr = r + o
                    valid = jnp.logical_and(rr >= 0, rr < n_seg)
                    src = pl.multiple_of(jnp.clip(rr, 0, n_seg - 1) * seg, seg)
                    term = jnp.where(valid, z_ref[pl.ds(src, seg), cols].astype(F32), 0.0)
                    s1 = term if s1 is None else s1 + term
                cnt_r = jnp.minimum(r + hi_off, n_seg - 1) - jnp.maximum(r - lo_off, 0) + 1
                m1 = s1 / cnt_r.astype(F32)
            else:
                m1 = xg
            rowbuf[g, pad:pad + seg, :] = m1
            s2 = None
            for o in range(-lo_off, hi_off + 1):
                term = rowbuf[g, pad + o:pad + o + seg, :]
                s2 = term if s2 is None else s2 + term
            cnt_c = jnp.minimum(pos + hi_off, seg - 1) - jnp.maximum(pos - lo_off, 0) + 1
            d = s2 / cnt_c.astype(F32) - xg
            yg = _dot(d.astype(BF16), pw_ref[g]) * ps_ref[:, cols]
            y_ref[pl.ds(base, seg), cols] = yg.astype(y_ref.dtype)

    if n_seg == 1:
        pool_segment(0)
    else:
        def seg_body(r, carry):
            pool_segment(r)
            return carry
        lax.fori_loop(0, n_seg, seg_body, 0)

    ch = RG_CHUNK
    n_ch = t_len // ch
    rg_lo, rg_hi = D_POOL, D_POOL + D_RG

    def gates(c0, d):
        cbuf[...] = _load_ext(z_ref, c0, ch, t_len, rg_lo, rg_hi)
        xc = cb_ref[...]
        for j in range(CONV_W):
            off = HALO - CONV_LEFT + j
            xc = xc + cbuf[off:off + ch, :] * cw_ref[j:j + 1, :]
        g = _dot(xc.astype(BF16), wg_ref[:, d * 2 * D_RG:(d + 1) * 2 * D_RG]) \
            + bg_ref[:, d * 2 * D_RG:(d + 1) * 2 * D_RG]
        r = _sigmoid(g[:, :D_RG])
        i = _sigmoid(g[:, D_RG:])
        log_a = (-RG_C) * r * _softplus(-lam_ref[d:d + 1, :])
        a = jnp.exp(log_a)
        a_s[...] = a
        b_s[...] = jnp.sqrt(jnp.tanh(-log_a) * (a * a + 1.0)) * (i * xc)

    def scan_rows(h, reverse):
        def row(t, h):
            tt = (ch - 1 - t) if reverse else t
            h = a_s[pl.ds(tt, 1), :] * h + b_s[pl.ds(tt, 1), :]
            h_s[pl.ds(tt, 1), :] = h
            return h
        return lax.fori_loop(0, ch, row, h, unroll=8)

    def fwd_body(i, h):
        c0 = pl.multiple_of(i * ch, ch)
        gates(c0, 0)
        h = scan_rows(h, False)
        hf_s[pl.ds(c0, ch), :] = h_s[...]
        return h

    h_fwd = lax.fori_loop(0, n_ch, fwd_body, h0_ref[0:1, :])

    def bwd_body(i, h):
        c0 = pl.multiple_of((n_ch - 1 - i) * ch, ch)
        gates(c0, 1)
        h = scan_rows(h, True)
        u_gate = z_ref[pl.ds(c0, ch), rg_hi:rg_hi + D_RG].astype(F32)
        y = (hf_s[pl.ds(c0, ch), :] + h_s[...]) * _gelu_tanh(u_gate)
        y_ref[pl.ds(c0, ch), D_POOL:D_POOL + D_RG] = y.astype(y_ref.dtype)
        return h

    h_bwd = lax.fori_loop(0, n_ch, bwd_body, h0_ref[1:2, :])
    hfin_ref[0:1, :] = h_fwd
    hfin_ref[1:2, :] = h_bwd


def _even_mixer(z, h0, p, n_batch, t_len, grid):
    seg = GRID_W if grid else t_len
    const2 = lambda b: (0, 0)
    const3 = lambda b: (0, 0, 0)
    return pl.pallas_call(
        functools.partial(_even_kernel, t_len=t_len, grid=grid),
        grid=(n_batch,),
        in_specs=[
            pl.BlockSpec((t_len, EV_COLS), lambda b: (b, 0)),
            pl.BlockSpec((None, 2, D_RG), lambda b: (b, 0, 0)),
            pl.BlockSpec(p['pool_w'].shape, const3),
            pl.BlockSpec((1, D_POOL), const2),
            pl.BlockSpec((CONV_W, D_RG), const2),
            pl.BlockSpec((1, D_RG), const2),
            pl.BlockSpec((D_RG, 4 * D_RG), const2),
            pl.BlockSpec((1, 4 * D_RG), const2),
            pl.BlockSpec((2, D_RG), const2),
        ],
        out_specs=[
            pl.BlockSpec((t_len, D_POOL + D_RG), lambda b: (b, 0)),
            pl.BlockSpec((None, 2, D_RG), lambda b: (b, 0, 0)),
        ],
        out_shape=[
            jax.ShapeDtypeStruct((n_batch * t_len, D_POOL + D_RG), Y_DTYPE),
            jax.ShapeDtypeStruct((n_batch, 2, D_RG), F32),
        ],
        scratch_shapes=[
            pltpu.VMEM((len(POOL_WINDOWS), seg + 16, POOL_GW), F32),
            pltpu.VMEM((RG_CHUNK + 2 * HALO, D_RG), F32),
            pltpu.VMEM((RG_CHUNK, D_RG), F32),
            pltpu.VMEM((RG_CHUNK, D_RG), F32),
            pltpu.VMEM((RG_CHUNK, D_RG), F32),
            pltpu.VMEM((t_len, D_RG), F32),
        ],
        compiler_params=_ARB1,
    )(z, h0, p['pool_w'], p['pool_scale'], p['conv_w'], p['conv_b'], p['wg'], p['bg'], p['lam'])


def _rwkv_kernel(z_ref, s0_ref, mu_ref, kkw_ref, ka_ref, rk_ref, g2_ref, w0_ref, w2_ref, a0_ref, a2_ref,
                 lnw_ref, lnb_ref, tri_ref, bd_ref,
                 y_ref, sfin_ref, ebuf, s_s, yd_s, bon_s, g_s, *, t_len):
    L = SCAN_CHUNK
    n_ch = t_len // L
    L2 = 2 * L
    s_s[...] = s0_ref[...]


    def stack(x):
        first_head = lax.broadcasted_iota(jnp.int32, (L, LANES), 1) < RW_HEAD
        return jnp.concatenate([jnp.where(first_head, x, 0.0), jnp.where(first_head, 0.0, x)],
                               axis=0).astype(BF16)

    def prep(c0, d):
        ones_bd = bd_ref[...]
        ebuf[d] = _load_ext(z_ref, c0, L, t_len, 0, RW_ZCOLS)
        cur = ebuf[d, HALO:HALO + L, :]
        prev = ebuf[d, HALO - 1:HALO - 1 + L, :]
        nxt = ebuf[d, HALO + 1:HALO + 1 + L, :]
        zs = cur + mu_ref[0:1, :] * (prev - cur) + mu_ref[1:2, :] * (nxt - cur)
        r = zs[:, 0:D_RW]
        k = zs[:, D_RW:2 * D_RW]
        v = zs[:, 2 * D_RW:3 * D_RW]
        o = 3 * D_RW
        gd = zs[:, o:o + RW_G_LORA]
        wd = zs[:, o + RW_G_LORA:o + RW_G_LORA + LANES]
        ad = zs[:, o + RW_G_LORA + LANES:o + RW_G_LORA + 2 * LANES]

        kkv = k * kkw_ref[...]
        nrm = jnp.sqrt(_head_sum(kkv * kkv, ones_bd))
        kkn = kkv / jnp.maximum(nrm, 1e-12)

        w_log = -_softplus(-(w0_ref[d:d + 1, :] + _dot_f32(jnp.tanh(wd), w2_ref[d]))) - 0.5
        lw = -jnp.exp(w_log)
        a = _sigmoid(a0_ref[d:d + 1, :] + _dot_f32(ad, a2_ref[d]))
        kd = k * (1.0 + (a - 1.0) * ka_ref[...])
        bon_s[d, pl.ds(c0, L), :] = _head_sum(r * kd * rk_ref[...], ones_bd) * v
        if d == 0:
            g_s[pl.ds(c0, L), :] = _dot(_sigmoid(gd).astype(BF16), g2_ref[...])

        cum = _dot_exact_lhs(tri_ref[d], lw)
        e_pos = jnp.exp(cum)
        e_neg = jnp.exp(-cum)
        rh = r * e_pos
        kh = kd * e_neg
        bh = (kkn * a) * e_neg
        ah = -kkn * jnp.exp(cum - lw)
        last = L - 1 if d == 0 else 0
        gam = e_pos[last:last + 1, :]
        bhg = bh * gam
        khg = kh * gam

        chains = []
        for p in range(RW_PAIRS):
            cols = slice(p * LANES, (p + 1) * LANES)
            chains.append(dict(
                d=d, p=p, gam=gam[:, cols], v_st=stack(v[:, cols]),
                lhs=jnp.concatenate([stack(ah[:, cols]), stack(rh[:, cols])], axis=0),
                rhs=jnp.concatenate([stack(bh[:, cols]), stack(kh[:, cols])], axis=0),
                bk=jnp.concatenate([stack(bhg[:, cols]), stack(khg[:, cols])], axis=0)))
        return chains

    n_levels = int(math.log2(L))

    def body(i, carry):
        c0s = (pl.multiple_of(i * L, L), pl.multiple_of((n_ch - 1 - i) * L, L))
        chains = prep(c0s[0], 0) + prep(c0s[1], 1)
        ri = lax.broadcasted_iota(jnp.int32, (L2, L2), 0)
        ci = lax.broadcasted_iota(jnp.int32, (L2, L2), 1)
        eye = jnp.where(ri == ci, 1.0, 0.0)
        rl = jnp.where(ri < L, ri, ri - L)
        cl = jnp.where(ci < L, ci, ci - L)
        strict = (cl < rl, cl > rl)
        incl = (cl <= rl, cl >= rl)
        level = ri ^ ci

        for ch in chains:
            ch['sc'] = _dot_nt(ch['lhs'], ch['rhs'])
        for ch in chains:
            sc, d = ch['sc'], ch['d']
            n_ab = jnp.where(strict[d], sc[0:L2, 0:L2], 0.0)
            m_ak = jnp.where(strict[d], sc[0:L2, L2:2 * L2], 0.0).astype(BF16)
            t_rb = jnp.where(incl[d], sc[L2:2 * L2, 0:L2], 0.0)
            t_rk = jnp.where(incl[d], sc[L2:2 * L2, L2:2 * L2], 0.0)
            ch['t_rbk'] = jnp.concatenate([t_rb, t_rk], axis=1).astype(BF16)
            ch['mv'] = _dot(m_ak, ch['v_st'])
            ch['t'] = eye + jnp.where(level == 1, n_ab, 0.0)
            ch['n_lv'] = [jnp.where(lax.shift_right_logical(level, lv) == 1, n_ab, 0.0).astype(BF16)
                          for lv in range(1, n_levels)]
            del ch['sc']
        for lv in range(n_levels - 1):
            for ch in chains:
                ch['tb'] = ch['t'].astype(BF16)
                ch['x'] = _dot(ch['n_lv'][lv], ch['tb']).astype(BF16)
            for ch in chains:
                ch['t'] = ch['t'] + _dot(ch['tb'], ch['x'])

        for ch in chains:
            ch['s_prev'] = s_s[ch['d'], ch['p']]
            ch['ar_s'] = _dot_nt(ch['lhs'], ch['s_prev'].astype(BF16))
        for ch in chains:
            u_rhs = ch['ar_s'][0:L2] + ch['mv']
            u_st = _dot(ch['t'].astype(BF16), u_rhs.astype(BF16)).astype(BF16)
            ch['uv'] = jnp.concatenate([u_st, ch['v_st']], axis=0)
        for ch in chains:
            y_st = ch['ar_s'][L2:2 * L2] + _dot(ch['t_rbk'], ch['uv'])
            ch['y'] = y_st[0:L] + y_st[L:L2]
            s_s[ch['d'], ch['p']] = ch['s_prev'] * ch['gam'] + _dot_tn(ch['uv'], ch['bk'])
        for d in range(2):
            yd_s[d, pl.ds(c0s[d], L), :] = jnp.concatenate([ch['y'] for ch in chains if ch['d'] == d], axis=1)
        return carry

    lax.fori_loop(0, n_ch, body, 0)

    def epilogue(i, carry):
        c0 = pl.multiple_of(i * L, L)
        ones_bd = bd_ref[...]
        ys = yd_s[0, pl.ds(c0, L), :] + yd_s[1, pl.ds(c0, L), :]
        mu = _head_sum(ys, ones_bd) * (1.0 / RW_HEAD)
        yc = ys - mu
        var = _head_sum(yc * yc, ones_bd) * (1.0 / RW_HEAD)
        yn = yc * lax.rsqrt(var + RW_LN_EPS) * lnw_ref[...] + lnb_ref[...]
        out = (yn + (bon_s[0, pl.ds(c0, L), :] + bon_s[1, pl.ds(c0, L), :])) * g_s[pl.ds(c0, L), :]
        y_ref[pl.ds(c0, L), :] = out.astype(y_ref.dtype)
        return carry

    lax.fori_loop(0, n_ch, epilogue, 0)
    sfin_ref[...] = s_s[...]


def _rwkv_mixer(z, s0, p, n_batch, t_len):
    L = SCAN_CHUNK
    c2 = lambda b: (0, 0)
    c3 = lambda b: (0, 0, 0)
    st_spec = pl.BlockSpec((None, 2, RW_PAIRS, LANES, LANES), lambda b: (b, 0, 0, 0, 0))
    return pl.pallas_call(
        functools.partial(_rwkv_kernel, t_len=t_len),
        grid=(n_batch,),
        in_specs=[
            pl.BlockSpec((t_len, RW_ZCOLS), lambda b: (b, 0)),
            st_spec,
            pl.BlockSpec((2, RW_ZCOLS), c2),
            pl.BlockSpec((1, D_RW), c2),
            pl.BlockSpec((1, D_RW), c2),
            pl.BlockSpec((1, D_RW), c2),
            pl.BlockSpec((RW_G_LORA, D_RW), c2),
            pl.BlockSpec((2, D_RW), c2),
            pl.BlockSpec((2, LANES, D_RW), c3),
            pl.BlockSpec((2, D_RW), c2),
            pl.BlockSpec((2, LANES, D_RW), c3),
            pl.BlockSpec((1, D_RW), c2),
            pl.BlockSpec((1, D_RW), c2),
            pl.BlockSpec((2, L, L), c3),
            pl.BlockSpec((LANES, LANES), c2),
        ],
        out_specs=[pl.BlockSpec((t_len, D_RW), lambda b: (b, 0)), st_spec],
        out_shape=[
            jax.ShapeDtypeStruct((n_batch * t_len, D_RW), Y_DTYPE),
            jax.ShapeDtypeStruct((n_batch, 2, RW_PAIRS, LANES, LANES), F32),
        ],
        scratch_shapes=[
            pltpu.VMEM((2, L + 2 * HALO, RW_ZCOLS), F32),
            pltpu.VMEM((2, RW_PAIRS, LANES, LANES), F32),
            pltpu.VMEM((2, t_len, D_RW), F32),
            pltpu.VMEM((2, t_len, D_RW), F32),
            pltpu.VMEM((t_len, D_RW), F32),
        ],
        compiler_params=_ARB1,
    )(z, s0, p['mu'], p['kk'], p['ka'], p['rk'], p['g2'], p['w0'], p['w2'], p['a0'], p['a2'],
      p['ln_w'], p['ln_b'], p['tri'], p['ones_bd'])


def _mlstm_kernel(z_ref, c0_ref, n0_ref, m0_ref, cw_ref, cb_ref, gb_ref, nw_ref, tri_ref,
                  y_ref, cfin_ref, nfin_ref, mfin_ref, cbuf, c_s, n_s, m_s, hd_s, *, t_len):
    L = SCAN_CHUNK
    n_ch = t_len // L
    c_s[...] = c0_ref[...]
    n_s[...] = n0_ref[...]
    m_s[...] = m0_ref[...]
    q_scale = ML_DK ** -0.5
    g_lo = 4 * D_ML

    def chunk_dir(c0, d):
        ri = lax.broadcasted_iota(jnp.int32, (L, L), 0)
        ci = lax.broadcasted_iota(jnp.int32, (L, L), 1)
        causal = (ci <= ri) if d == 0 else (ci >= ri)
        cbuf[d] = _load_ext(z_ref, c0, L, t_len, 0, 2 * D_ML)
        qk = cb_ref[...]
        for j in range(CONV_W):
            off = HALO - CONV_LEFT + j
            qk = qk + cbuf[d, off:off + L, :] * cw_ref[j:j + 1, :]
        qk = _silu(qk)
        q = qk[:, :D_ML] * q_scale
        k = qk[:, D_ML:]
        v = z_ref[pl.ds(c0, L), 2 * D_ML:3 * D_ML].astype(F32)
        gts = z_ref[pl.ds(c0, L), g_lo:g_lo + LANES].astype(F32) + gb_ref[...]
        lf_all = -_softplus(-gts)
        bcum = _dot_exact_lhs(tri_ref[d], lf_all)
        both_t = jnp.concatenate([gts, bcum], axis=0).T
        last = L - 1 if d == 0 else 0
        chains = []
        for h in range(ML_HEADS):
            col_i = d * 2 * ML_HEADS + h
            col_f = col_i + ML_HEADS
            cols = slice(h * ML_DK, (h + 1) * ML_DK)
            b_col = bcum[:, col_f:col_f + 1]
            b_row = both_t[col_f:col_f + 1, L:2 * L]
            li_row = both_t[col_i:col_i + 1, 0:L]
            chains.append(dict(
                d=d, h=h, b_col=b_col, li_col=gts[:, col_i:col_i + 1], b_last=b_col[last:last + 1, :], last=last,
                log_d=jnp.where(causal, b_col - b_row + li_row, -jnp.inf),
                q=q[:, cols], k=k[:, cols], v=v[:, cols],
                qb=q[:, cols].astype(BF16), kb=k[:, cols].astype(BF16)))
        return chains

    def body(i, carry):
        c0s = (pl.multiple_of(i * L, L), pl.multiple_of((n_ch - 1 - i) * L, L))
        chains = chunk_dir(c0s[0], 0) + chunk_dir(c0s[1], 1)
        for ch in chains:
            d, h = ch['d'], ch['h']
            ch['qk'] = _dot_nt(ch['qb'], ch['kb'])
            ch['c_prev'] = c_s[d, h]
            ch['n_prev'] = n_s[d, h:h + 1, :]
            ch['m_prev'] = m_s[d, h:h + 1, 0:1]
            ch['qc'] = _dot(ch['qb'], ch['c_prev'].astype(BF16))
        for ch in chains:
            log_inter = ch['b_col'] + ch['m_prev']
            m_t = jnp.maximum(log_inter, jnp.max(ch['log_d'], axis=1, keepdims=True))
            ch['m_t'] = m_t
            ch['s'] = ch['qk'] * jnp.exp(ch['log_d'] - m_t)
            ch['w_inter'] = jnp.exp(log_inter - m_t)
        for ch in chains:
            ch['sv'] = _dot(ch['s'].astype(BF16), ch['v'].astype(BF16))
            last = ch['last']
            m_new = ch['m_t'][last:last + 1, :]
            ch['m_new'] = m_new
            ch['w_s'] = jnp.exp(ch['b_last'] - ch['b_col'] + ch['li_col'] - m_new)
            ch['dec'] = jnp.exp(ch['b_last'] + ch['m_prev'] - m_new)
            ch['kv'] = _dot_tn(ch['kb'], (ch['w_s'] * ch['v']).astype(BF16))
        for ch in chains:
            d, h = ch['d'], ch['h']
            num = ch['sv'] + ch['w_inter'] * ch['qc']
            den = jnp.sum(ch['s'], axis=1, keepdims=True) \
                + ch['w_inter'] * jnp.sum(ch['q'] * ch['n_prev'], axis=1, keepdims=True)
            ch['out'] = num / jnp.maximum(jnp.abs(den), jnp.exp(-ch['m_t']))
            c_s[d, h] = ch['dec'] * ch['c_prev'] + ch['kv']
            n_s[d, h:h + 1, :] = ch['dec'] * ch['n_prev'] + jnp.sum(ch['w_s'] * ch['k'], axis=0, keepdims=True)
            m_s[d, h:h + 1, :] = jnp.broadcast_to(ch['m_new'], (1, LANES))
        for d in range(2):
            hd_s[d, pl.ds(c0s[d], L), :] = jnp.concatenate([ch['out'] for ch in chains if ch['d'] == d], axis=1)
        return carry

    lax.fori_loop(0, n_ch, body, 0)

    def epilogue(i, carry):
        c0 = pl.multiple_of(i * L, L)
        hs = hd_s[0, pl.ds(c0, L), :] + hd_s[1, pl.ds(c0, L), :]
        o = z_ref[pl.ds(c0, L), 3 * D_ML:4 * D_ML].astype(F32)
        outs = []
        for h in range(ML_HEADS):
            x = hs[:, h * ML_DK:(h + 1) * ML_DK]
            mu = jnp.mean(x, axis=1, keepdims=True)
            xc = x - mu
            var = jnp.mean(xc * xc, axis=1, keepdims=True)
            outs.append(xc * lax.rsqrt(var + NORM_EPS))
        y = jnp.concatenate(outs, axis=1) * nw_ref[...] * _sigmoid(o)
        y_ref[pl.ds(c0, L), :] = y.astype(y_ref.dtype)
        return carry

    lax.fori_loop(0, n_ch, epilogue, 0)
    cfin_ref[...] = c_s[...]
    nfin_ref[...] = n_s[...]
    mfin_ref[...] = m_s[...]


def _mlstm_mixer(z, c0, n0, m0, p, n_batch, t_len):
    L = SCAN_CHUNK
    c2 = lambda b: (0, 0)
    c3 = lambda b: (0, 0, 0)
    c_spec = pl.BlockSpec((None, 2, ML_HEADS, ML_DK, ML_DK), lambda b: (b, 0, 0, 0, 0))
    n_spec = pl.BlockSpec((None, 2, ML_HEADS, ML_DK), lambda b: (b, 0, 0, 0))
    return pl.pallas_call(
        functools.partial(_mlstm_kernel, t_len=t_len),
        grid=(n_batch,),
        in_specs=[
            pl.BlockSpec((t_len, ML_ZCOLS), lambda b: (b, 0)),
            c_spec, n_spec, n_spec,
            pl.BlockSpec((CONV_W, 2 * D_ML), c2),
            pl.BlockSpec((1, 2 * D_ML), c2),
            pl.BlockSpec((1, LANES), c2),
            pl.BlockSpec((1, D_ML), c2),
            pl.BlockSpec((2, L, L), c3),
        ],
        out_specs=[pl.BlockSpec((t_len, D_ML), lambda b: (b, 0)), c_spec, n_spec, n_spec],
        out_shape=[
            jax.ShapeDtypeStruct((n_batch * t_len, D_ML), Y_DTYPE),
            jax.ShapeDtypeStruct((n_batch, 2, ML_HEADS, ML_DK, ML_DK), F32),
            jax.ShapeDtypeStruct((n_batch, 2, ML_HEADS, ML_DK), F32),
            jax.ShapeDtypeStruct((n_batch, 2, ML_HEADS, ML_DK), F32),
        ],
        scratch_shapes=[
            pltpu.VMEM((2, L + 2 * HALO, 2 * D_ML), F32),
            pltpu.VMEM((2, ML_HEADS, ML_DK, ML_DK), F32),
            pltpu.VMEM((2, ML_HEADS, ML_DK), F32),
            pltpu.VMEM((2, ML_HEADS, ML_DK), F32),
            pltpu.VMEM((2, t_len, D_ML), F32),
        ],
        compiler_params=_ARB1,
    )(z, c0, n0, m0, p['conv_w'], p['conv_b'], p['gate_b'], p['norm_w'], p['tri'])


def _mlstm2_kernel(z_ref, c0_ref, n0_ref, m0_ref, cw_ref, cb_ref, gb_ref, nw_ref, tri_ref, esel_ref, shift_ref,
                   ones_ref, y_ref, cfin_ref, nfin_ref, mfin_ref, c_s, n_s, m_s, hd_s, qk_s, *, t_len):
    L = ML_SCAN
    n_ch = t_len // L
    c_s[...] = c0_ref[...]
    n_s[...] = n0_ref[...]
    m_s[...] = m0_ref[...]
    q_scale = ML_DK ** -0.5
    g_lo = 4 * D_ML
    n_real = 4 * ML_HEADS

    def conv_chunk(i, carry):
        c0 = pl.multiple_of(i * L, L)
        ext = _load_ext(z_ref, c0, L, t_len, 0, 2 * D_ML, raw=True)
        taps = _dot(shift_ref[...], ext)
        qk = cb_ref[...] + ext[HALO:HALO + L].astype(F32) * cw_ref[CONV_LEFT:CONV_LEFT + 1, :]
        for n, j in enumerate(jj for jj in range(CONV_W) if jj != CONV_LEFT):
            qk = qk + taps[n * L:(n + 1) * L] * cw_ref[j:j + 1, :]
        qk = _silu(qk)
        qk_s[pl.ds(c0, L), 0:D_ML] = (qk[:, :D_ML] * q_scale).astype(BF16)
        qk_s[pl.ds(c0, L), D_ML:2 * D_ML] = qk[:, D_ML:].astype(BF16)
        return carry

    lax.fori_loop(0, n_ch, conv_chunk, 0)

    def prep(c0, d):
        q = qk_s[pl.ds(c0, L), 0:D_ML]
        k = qk_s[pl.ds(c0, L), D_ML:2 * D_ML]
        v = z_ref[pl.ds(c0, L), 2 * D_ML:3 * D_ML]
        gts = z_ref[pl.ds(c0, L), g_lo:g_lo + LANES].astype(F32) + gb_ref[...]
        x16 = gts.T[0:n_real, :]
        lf_hi, lf_lo = _split2(-_softplus(-x16))
        b16 = _dot_nt(lf_hi, tri_ref[d]) + _dot_nt(lf_lo, tri_ref[d])
        x8 = x16[8 * d:8 * d + 8, :]
        b8 = b16[8 * d:8 * d + 8, :]
        g8 = pltpu.roll(x8, ML_HEADS, axis=0) - b8
        cm8 = g8
        lane = lax.broadcasted_iota(jnp.int32, (8, L), 1)
        step = 1
        while step < L:
            if d == 0:
                sh = jnp.where(lane >= step, pltpu.roll(cm8, step, axis=1), -jnp.inf)
            else:
                sh = jnp.where(lane < L - step, pltpu.roll(cm8, L - step, axis=1), -jnp.inf)
            cm8 = jnp.maximum(cm8, sh)
            step *= 2
        x32 = jnp.concatenate([g8, b8, cm8, jnp.zeros((8, L), F32)], axis=0)
        hi = x32.astype(BF16)
        r1 = x32 - hi.astype(F32)
        mid = r1.astype(BF16)
        lo = (r1 - mid.astype(F32)).astype(BF16)
        cb = _dot_tn(jnp.concatenate([hi, mid, lo], axis=0), esel_ref[...])
        ri = lax.broadcasted_iota(jnp.int32, (L, L), 0)
        ci = lax.broadcasted_iota(jnp.int32, (L, L), 1)
        causal = (ci <= ri) if d == 0 else (ci >= ri)
        last = L - 1 if d == 0 else 0
        ones = ones_ref[...]
        chains = []
        for h in range(ML_HEADS):
            cols = slice(h * ML_DK, (h + 1) * ML_DK)
            chains.append(dict(
                d=d, h=h, last=last, causal=causal,
                g_row=g8[ML_HEADS + h:ML_HEADS + h + 1, :],
                g_b=cb[:, cols], b_b=cb[:, D_ML + h * ML_DK:D_ML + (h + 1) * ML_DK],
                cm_b=cb[:, 2 * D_ML + h * ML_DK:2 * D_ML + (h + 1) * ML_DK],
                qb=q[:, cols], kb=k[:, cols],
                v=v[:, cols].astype(F32), v_ones=jnp.concatenate([v[:, cols].astype(BF16), ones], axis=1)))
        return chains

    def body(i, carry):
        c0s = (pl.multiple_of(i * L, L), pl.multiple_of((n_ch - 1 - i) * L, L))
        chains = prep(c0s[0], 0) + prep(c0s[1], 1)
        for ch in chains:
            d, h = ch['d'], ch['h']
            ch['qk'] = _dot_nt(ch['qb'], ch['kb'])
            ch['c_prev'] = c_s[d, h]
            ch['n_prev'] = n_s[d, h]
            ch['m_prev'] = m_s[d, h:h + 1, :]
            ch['qcn'] = _dot(ch['qb'], jnp.concatenate([ch['c_prev'].astype(BF16),
                                                        ch['n_prev'].astype(BF16)], axis=1))
        for ch in chains:
            mu = jnp.maximum(ch['m_prev'], ch['cm_b'])
            ch['mu'] = mu
            ch['s'] = ch['qk'] * jnp.exp(jnp.where(ch['causal'], ch['g_row'] - mu, -jnp.inf))
            ch['w_inter'] = jnp.exp(ch['m_prev'] - mu)
        for ch in chains:
            last = ch['last']
            ch['sv'] = _dot(ch['s'].astype(BF16), ch['v_ones'])
            mu_last = ch['mu'][last:last + 1, :]
            ch['m_new'] = ch['b_b'][last:last + 1, :] + mu_last
            ch['dec'] = jnp.exp(ch['m_prev'] - mu_last)
            w_s = jnp.exp(ch['g_b'] - mu_last)
            ch['upd'] = _dot_tn(ch['kb'], jnp.concatenate([(w_s * ch['v']).astype(BF16),
                                                           w_s.astype(BF16)], axis=1))
        for ch in chains:
            d, h = ch['d'], ch['h']
            num = ch['sv'][:, :ML_DK] + ch['w_inter'] * ch['qcn'][:, :ML_DK]
            den = ch['sv'][:, ML_DK:] + ch['w_inter'] * ch['qcn'][:, ML_DK:]
            ch['out'] = num / jnp.maximum(jnp.abs(den), jnp.exp(-(ch['b_b'] + ch['mu'])))
            c_s[d, h] = ch['dec'] * ch['c_prev'] + ch['upd'][:, :ML_DK]
            n_s[d, h] = ch['dec'] * ch['n_prev'] + ch['upd'][:, ML_DK:]
            m_s[d, h:h + 1, :] = ch['m_new']
        for d in range(2):
            hd_s[d, pl.ds(c0s[d], L), :] = jnp.concatenate([ch['out'] for ch in chains if ch['d'] == d], axis=1)
        return carry

    lax.fori_loop(0, n_ch, body, 0)

    def epilogue(i, carry):
        c0 = pl.multiple_of(i * L, L)
        ones = ones_ref[...]
        hs = hd_s[0, pl.ds(c0, L), :] + hd_s[1, pl.ds(c0, L), :]
        o = z_ref[pl.ds(c0, L), 3 * D_ML:4 * D_ML].astype(F32)
        mu = _head_sum(hs, ones) * (1.0 / ML_DK)
        xc = hs - mu
        var = _head_sum(xc * xc, ones) * (1.0 / ML_DK)
        y = xc * lax.rsqrt(var + NORM_EPS) * nw_ref[...] * _sigmoid(o)
        y_ref[pl.ds(c0, L), :] = y.astype(y_ref.dtype)
        return carry

    lax.fori_loop(0, n_ch, epilogue, 0)
    cfin_ref[...] = c_s[...]
    nfin_ref[...] = n_s[...]
    mfin_ref[...] = m_s[...]


def _mlstm2_mixer(z, c0, n0, m0, p, n_batch, t_len):
    assert z.dtype == BF16
    L = ML_SCAN
    c2 = lambda b: (0, 0)
    c3 = lambda b: (0, 0, 0)
    c_spec = pl.BlockSpec((None, 2, ML_HEADS, ML_DK, ML_DK), lambda b: (b, 0, 0, 0, 0))
    m_spec = pl.BlockSpec((None, 2, ML_HEADS, ML_DK), lambda b: (b, 0, 0, 0))
    return pl.pallas_call(
        functools.partial(_mlstm2_kernel, t_len=t_len),
        grid=(n_batch,),
        in_specs=[
            pl.BlockSpec((t_len, ML_ZCOLS), lambda b: (b, 0)),
            c_spec, c_spec, m_spec,
            pl.BlockSpec((CONV_W, 2 * D_ML), c2),
            pl.BlockSpec((1, 2 * D_ML), c2),
            pl.BlockSpec((1, LANES), c2),
            pl.BlockSpec((1, D_ML), c2),
            pl.BlockSpec((2, L, L), c3),
            pl.BlockSpec(p['esel'].shape, c2),
            pl.BlockSpec(p['shift'].shape, c2),
            pl.BlockSpec((L, LANES), c2),
        ],
        out_specs=[pl.BlockSpec((t_len, D_ML), lambda b: (b, 0)), c_spec, c_spec, m_spec],
        out_shape=[
            jax.ShapeDtypeStruct((n_batch * t_len, D_ML), Y_DTYPE),
            jax.ShapeDtypeStruct((n_batch, 2, ML_HEADS, ML_DK, ML_DK), F32),
            jax.ShapeDtypeStruct((n_batch, 2, ML_HEADS, ML_DK, ML_DK), F32),
            jax.ShapeDtypeStruct((n_batch, 2, ML_HEADS, ML_DK), F32),
        ],
        scratch_shapes=[
            pltpu.VMEM((2, ML_HEADS, ML_DK, ML_DK), F32),
            pltpu.VMEM((2, ML_HEADS, ML_DK, ML_DK), F32),
            pltpu.VMEM((2, ML_HEADS, ML_DK), F32),
            pltpu.VMEM((2, t_len, D_ML), F32),
            pltpu.VMEM((t_len, 2 * D_ML), BF16),
        ],
        compiler_params=_ARB1,
    )(z, c0, n0, m0, p['conv_w'], p['conv_b'], p['gate_b'], p['norm_w'], p['tri'], p['esel'], p['shift'],
      p['ones'])


def _mlstm_consts():
    L = ML_SCAN
    lower = jnp.tril(jnp.ones((L, L), F32))
    tri = jnp.stack([lower, lower.T]).astype(BF16)
    rows = jnp.arange(32)
    blk = jnp.arange(3 * D_ML) // ML_DK
    src_row = 8 * (blk // ML_HEADS) + ML_HEADS + blk % ML_HEADS
    esel = (rows[:, None] == src_row[None, :]).astype(BF16)
    esel = jnp.concatenate([esel, esel, esel], axis=0)
    taps = jnp.array([j for j in range(CONV_W) if j != CONV_LEFT])
    t = jnp.arange((CONV_W - 1) * L)
    src = HALO + t % L + taps[t // L] - CONV_LEFT
    shift = (src[:, None] == jnp.arange(L + 2 * HALO)[None, :]).astype(BF16)
    return dict(tri=tri, esel=esel, shift=shift, ones=jnp.ones((L, LANES), BF16))


def _block_diag(w):
    n, c, d = w.shape
    return jnp.einsum('ncd,nm->ncmd', w, jnp.eye(n, dtype=w.dtype)).reshape(n * c, n * d)


def _scan_tri():
    L = SCAN_CHUNK
    lower = jnp.tril(jnp.ones((L, L), F32))
    return jnp.stack([lower, lower.T]).astype(BF16)


def _prep_even(ev_w_in, pool_w, pool_scale, rg_conv_w, rg_conv_b, rg_wa, rg_ba, rg_wx, rg_bx, rg_lam, ev_w_out, j):
    wg = jnp.concatenate([_block_diag(rg_wa[j, 0]), _block_diag(rg_wx[j, 0]),
                          _block_diag(rg_wa[j, 1]), _block_diag(rg_wx[j, 1])], axis=1)
    bg = jnp.concatenate([rg_ba[j, 0], rg_bx[j, 0], rg_ba[j, 1], rg_bx[j, 1]]).reshape(1, 4 * D_RG)
    return dict(
        w_in=[ev_w_in[j].astype(BF16)],
        w_out=ev_w_out[j].astype(BF16),
        pool_w=pool_w[j].astype(BF16),
        pool_scale=pool_scale[j].reshape(1, D_POOL),
        conv_w=rg_conv_w[j],
        conv_b=rg_conv_b[j].reshape(1, D_RG),
        wg=wg.astype(BF16),
        bg=bg,
        lam=rg_lam[j],
    )


def _pad_cols(w, cuts, widths):
    pieces = jnp.split(w, cuts, axis=-1)
    out = []
    for piece, width in zip(pieces, widths):
        extra = width - piece.shape[-1]
        if extra:
            piece = jnp.pad(piece, [(0, 0)] * (piece.ndim - 1) + [(0, extra)])
        out.append(piece)
    return jnp.concatenate(out, axis=-1)


def _prep_odd(od_w_in, rw_mu, rw_w0, rw_w2, rw_a0, rw_a2, rw_kk, rw_ka, rw_rk, rw_g2, rw_ln_w, rw_ln_b,
              ml_conv_w, ml_conv_b, ml_bi, ml_bf, ml_norm_w, od_w_out, j):
    rw_cols = 3 * D_RW + RW_G_LORA + RW_W_LORA + RW_A_LORA
    base = 3 * D_RW + RW_G_LORA
    cuts = [base, base + RW_W_LORA]
    widths = [base, LANES, LANES]
    w_rw = _pad_cols(od_w_in[j][:, :rw_cols], cuts, widths)
    w_ml = _pad_cols(od_w_in[j][:, rw_cols:], [4 * D_ML], [4 * D_ML, LANES])
    pad_rows = lambda w: jnp.pad(w, ((0, 0), (0, LANES - w.shape[1]), (0, 0)))
    gate_b = jnp.concatenate([ml_bi[j, 0], ml_bf[j, 0], ml_bi[j, 1], ml_bf[j, 1]])
    gate_b = jnp.pad(gate_b, (0, LANES - gate_b.shape[0])).reshape(1, LANES)
    hm = jnp.arange(LANES) // RW_HEAD
    rw = dict(
        mu=_pad_cols(rw_mu[j], cuts, widths),
        kk=rw_kk[j].reshape(1, D_RW), ka=rw_ka[j].reshape(1, D_RW), rk=rw_rk[j].reshape(1, D_RW),
        g2=rw_g2[j].astype(BF16),
        w0=rw_w0[j], w2=pad_rows(rw_w2[j]), a0=rw_a0[j], a2=pad_rows(rw_a2[j]),
        ln_w=rw_ln_w[j].reshape(1, D_RW), ln_b=rw_ln_b[j].reshape(1, D_RW),
        tri=_scan_tri(),
        ones_bd=(hm[:, None] == hm[None, :]).astype(BF16),
    )
    ml = dict(
        conv_w=ml_conv_w[j], conv_b=ml_conv_b[j].reshape(1, 2 * D_ML),
        gate_b=gate_b, norm_w=ml_norm_w[j].reshape(1, D_ML), **_mlstm_consts(),
    )
    return dict(w_in=[w_rw.astype(BF16), w_ml.astype(BF16)], w_out=od_w_out[j].astype(BF16), rw=rw, ml=ml)


def _rwkv_state_in(s):
    b = s.shape[0]
    s = s.reshape(b, 2, RW_PAIRS, 2, RW_HEAD, RW_HEAD)
    bd = jnp.einsum('bdpivk,ij->bdpivjk', s, jnp.eye(2, dtype=s.dtype))
    return bd.reshape(b, 2, RW_PAIRS, LANES, LANES)


def _rwkv_state_out(bd):
    b = bd.shape[0]
    x = bd.reshape(b, 2, RW_PAIRS, 2, RW_HEAD, 2, RW_HEAD)
    s = jnp.stack([x[:, :, :, 0, :, 0, :], x[:, :, :, 1, :, 1, :]], axis=3)
    return s.reshape(b, 2, RW_HEADS, RW_HEAD, RW_HEAD)


def _trunk(x, mod, rows_per_mod, mod_base, grid, states, params, n_batch, t_len):
    ev, od, dense = params
    rg0, rw0, c0, n0, m0 = states
    x = x.reshape(n_batch * t_len, D_MODEL)
    spec = (rows_per_mod, mod_base)

    (z,) = _in_proj(x, mod[0], *spec, dense['norm1_w'][0], ev['w_in'])
    y_ev, rg_f = _even_mixer(z, rg0, ev, n_batch, t_len, grid)
    x = _out_mlp([y_ev], x, mod[0], *spec, dense['norm2_w'][0], ev['w_out'],
                 dense['mlp_w1'][0], dense['mlp_w2'][0], dense['final_norm_w'], False)

    z_rw, z_ml = _in_proj(x, mod[1], *spec, dense['norm1_w'][1], od['w_in'])
    y_rw, s_f = _rwkv_mixer(z_rw, rw0, od['rw'], n_batch, t_len)
    y_ml, c_f, n_f, m_f = _mlstm2_mixer(z_ml, c0, n0, m0, od['ml'], n_batch, t_len)
    y = _out_mlp([y_rw, y_ml], x, mod[1], *spec, dense['norm2_w'][1], od['w_out'],
                 dense['mlp_w1'][1], dense['mlp_w2'][1], dense['final_norm_w'], True)
    return y.reshape(n_batch, t_len, D_MODEL), (rg_f, s_f, c_f, n_f, m_f)


def kernel(x_prompt, x_sample, state_rglru, state_rwkv, state_mlstm_C, state_mlstm_n, state_mlstm_m,
           c, c_ctx, norm1_w, norm2_w, w_mod, b_mod, mlp_w1, mlp_w2, final_norm_w,
           ev_w_in, pool_w, pool_scale, rg_conv_w, rg_conv_b, rg_wa, rg_ba, rg_wx, rg_bx, rg_lam, ev_w_out,
           od_w_in, rw_mu, rw_w0, rw_w2, rw_a0, rw_a2, rw_kk, rw_ka, rw_rk, rw_g2, rw_ln_w, rw_ln_b,
           ml_conv_w, ml_conv_b, ml_bi, ml_bf, ml_norm_w, od_w_out):
    bp, tp, _ = x_prompt.shape
    bs, ts, _ = x_sample.shape

    n_cond = 16
    cond = jnp.concatenate([c_ctx[None, :], c, jnp.zeros((n_cond - 1 - bs, D_MODEL), F32)], axis=0)
    mod = _modulation(cond, w_mod, b_mod)

    ev = _prep_even(ev_w_in, pool_w, pool_scale, rg_conv_w, rg_conv_b, rg_wa, rg_ba, rg_wx, rg_bx, rg_lam,
                    ev_w_out, 0)
    od = _prep_odd(od_w_in, rw_mu, rw_w0, rw_w2, rw_a0, rw_a2, rw_kk, rw_ka, rw_rk, rw_g2, rw_ln_w, rw_ln_b,
                   ml_conv_w, ml_conv_b, ml_bi, ml_bf, ml_norm_w, od_w_out, 0)
    dense = dict(norm1_w=norm1_w, norm2_w=norm2_w, mlp_w1=mlp_w1.astype(BF16), mlp_w2=mlp_w2.astype(BF16),
                 final_norm_w=final_norm_w)
    params = (ev, od, dense)

    zero_states = (
        jnp.zeros((bp, 2, D_RG), F32),
        jnp.zeros((bp, 2, RW_PAIRS, LANES, LANES), F32),
        jnp.zeros((bp, 2, ML_HEADS, ML_DK, ML_DK), F32),
        jnp.zeros((bp, 2, ML_HEADS, ML_DK, ML_DK), F32),
        jnp.zeros((bp, 2, ML_HEADS, ML_DK), F32),
    )
    y_prompt, (rg_f, s_f, c_f, n_f, m_f) = _trunk(
        x_prompt, mod, bp * tp, 0, False, zero_states, params, bp, tp)

    sample_states = (
        state_rglru[:, 0],
        _rwkv_state_in(state_rwkv[:, 0]),
        state_mlstm_C[:, 0],
        jnp.broadcast_to(state_mlstm_n[:, 0][..., None], (bs, 2, ML_HEADS, ML_DK, ML_DK)),
        jnp.broadcast_to(state_mlstm_m[:, 0][..., None], (bs, 2, ML_HEADS, ML_DK)),
    )
    y_sample, _ = _trunk(x_sample, mod, ts, 1, True, sample_states, params, bs, ts)

    return (y_prompt, y_sample,
            rg_f[:, None],
            _rwkv_state_out(s_f)[:, None],
            c_f[:, None],
            n_f[:, None, :, :, :, 0],
            m_f[:, None, :, :, 0])
```

```python
import functools
import math

import jax
import jax.numpy as jnp
from jax import lax
from jax.experimental import pallas as pl
from jax.experimental.pallas import tpu as pltpu

F32 = jnp.float32
BF16 = jnp.bfloat16

D_MODEL = 1024
DEPTH = 2
GRID_W = 64
D_FF = 4 * D_MODEL
NORM_EPS = 1e-6
N_MOD = 6

D_POOL = D_MODEL // 2
POOL_WINDOWS = (2, 4, 8, 16)
POOL_GW = D_POOL // len(POOL_WINDOWS)
D_RG = D_MODEL // 2
RG_BLOCKS = 8
RG_C = 8.0
CONV_W = 4
CONV_LEFT = 2
EV_COLS = D_POOL + 2 * D_RG

D_RW = D_MODEL // 2
RW_HEAD = 64
RW_HEADS = D_RW // RW_HEAD
RW_PAIRS = RW_HEADS // 2
RW_W_LORA = 64
RW_A_LORA = 64
RW_G_LORA = 128
RW_LN_EPS = 64e-5
RW_ZCOLS = 3 * D_RW + RW_G_LORA + 128 + 128
D_ML = D_MODEL // 2
ML_HEADS = 4
ML_DK = D_ML // ML_HEADS
ML_ZCOLS = 4 * D_ML + 128

LANES = 128
HALO = 16
Z_DTYPE = BF16
Y_DTYPE = BF16
ROW_BLOCK = 512
SCAN_CHUNK = 64
ML_SCAN = 128
RG_CHUNK = 256
CONV_ROWS = 128
VMEM_LIMIT = 56 * 1024 * 1024

_ARB1 = pltpu.CompilerParams(dimension_semantics=("arbitrary",), vmem_limit_bytes=VMEM_LIMIT)
_ARB2 = pltpu.CompilerParams(dimension_semantics=("arbitrary", "arbitrary"), vmem_limit_bytes=VMEM_LIMIT)


def _dot(a, b):
    return jnp.dot(a, b, preferred_element_type=F32)


def _dot_nt(a, b):
    return lax.dot_general(a, b, (((1,), (1,)), ((), ())), preferred_element_type=F32)


def _dot_tn(a, b):
    return lax.dot_general(a, b, (((0,), (0,)), ((), ())), preferred_element_type=F32)


def _split2(x):
    hi = x.astype(BF16)
    lo = (x - hi.astype(F32)).astype(BF16)
    return hi, lo


def _dot_f32(a, b):
    ah, al = _split2(a)
    bh, bl = _split2(b)
    return _dot(ah, bh) + (_dot(ah, bl) + _dot(al, bh))


def _dot_exact_lhs(m_bf16, x):
    hi, lo = _split2(x)
    return _dot(m_bf16, hi) + _dot(m_bf16, lo)


def _dot_exact_rhs(x, m_bf16):
    hi, lo = _split2(x)
    return _dot(hi, m_bf16) + _dot(lo, m_bf16)


def _softplus(x):
    return jnp.maximum(x, 0.0) + jnp.log1p(jnp.exp(-jnp.abs(x)))


def _sigmoid(x):
    return jax.nn.sigmoid(x)


def _silu(x):
    return x * jax.nn.sigmoid(x)


def _gelu_tanh(x):
    c = math.sqrt(2.0 / math.pi)
    return x * (0.5 * (1.0 + jnp.tanh(c * (x + 0.044715 * (x * x * x)))))


def _rms_norm(x, w):
    ms = jnp.mean(x * x, axis=-1, keepdims=True)
    return x * lax.rsqrt(ms + NORM_EPS) * w


def _load_ext(z_ref, c0, n, t_len, lo, hi, raw=False):
    dt = z_ref.dtype if raw else F32
    cur = z_ref[pl.ds(c0, n), lo:hi].astype(dt)
    pb = pl.multiple_of(jnp.maximum(c0 - HALO, 0), HALO)
    pa = pl.multiple_of(jnp.minimum(c0 + n, t_len - HALO), HALO)
    before = z_ref[pl.ds(pb, HALO), lo:hi].astype(dt)
    after = z_ref[pl.ds(pa, HALO), lo:hi].astype(dt)
    before = jnp.where(c0 > 0, before, jnp.zeros_like(before))
    after = jnp.where(c0 + n < t_len, after, jnp.zeros_like(after))
    return jnp.concatenate([before, cur, after], axis=0)


def _head_sum(x, ones_bd):
    n = x.shape[0]
    groups = x.shape[1] // LANES
    xs = jnp.concatenate([x[:, g * LANES:(g + 1) * LANES] for g in range(groups)], axis=0)
    s = _dot_exact_rhs(xs, ones_bd)
    return jnp.concatenate([s[g * n:(g + 1) * n] for g in range(groups)], axis=1)


def _mod_kernel(c_ref, w_ref, b_ref, o_ref):
    s = _silu(c_ref[...])
    o_ref[...] = _dot_f32(s, w_ref[...]) + b_ref[...]


def _modulation(cond, w_mod, b_mod):
    rows = cond.shape[0]
    tn = 1536
    out = pl.pallas_call(
        _mod_kernel,
        grid=(DEPTH, N_MOD * D_MODEL // tn),
        in_specs=[
            pl.BlockSpec((rows, D_MODEL), lambda l, j: (0, 0)),
            pl.BlockSpec((None, D_MODEL, tn), lambda l, j: (l, 0, j)),
            pl.BlockSpec((None, 1, tn), lambda l, j: (l, 0, j)),
        ],
        out_specs=pl.BlockSpec((None, rows, tn), lambda l, j: (l, 0, j)),
        out_shape=jax.ShapeDtypeStruct((DEPTH, rows, N_MOD * D_MODEL), F32),
        compiler_params=_ARB2,
    )(cond, w_mod, b_mod.reshape(DEPTH, 1, N_MOD * D_MODEL))
    return out.reshape(DEPTH, rows, N_MOD, D_MODEL)


def _inproj_kernel(*refs, n_out, col_chunk):
    x_ref, mod_ref, nw_ref = refs[:3]
    w_refs = refs[3:3 + n_out]
    z_refs = refs[3 + n_out:]
    h = _rms_norm(x_ref[...], nw_ref[...])
    h = h * (1.0 + mod_ref[1:2, :]) + mod_ref[0:1, :]
    hb = h.astype(BF16)
    for w_ref, z_ref in zip(w_refs, z_refs):
        n = w_ref.shape[1]
        for c in range(0, n, col_chunk):
            e = min(c + col_chunk, n)
            z_ref[:, c:e] = _dot(hb, w_ref[:, c:e]).astype(z_ref.dtype)


def _in_proj(x, mod, rows_per_mod, mod_base, norm_w, weights):
    m = x.shape[0]
    tm = ROW_BLOCK
    n_out = len(weights)
    const = lambda i: (0, 0)
    in_specs = [
        pl.BlockSpec((tm, D_MODEL), lambda i: (i, 0)),
        pl.BlockSpec((None, N_MOD, D_MODEL), lambda i: (mod_base + (i * tm) // rows_per_mod, 0, 0)),
        pl.BlockSpec((1, D_MODEL), const),
    ] + [pl.BlockSpec(w.shape, const, pipeline_mode=pl.Buffered(1)) for w in weights]
    out_specs = [pl.BlockSpec((tm, w.shape[1]), lambda i: (i, 0)) for w in weights]
    out_shape = [jax.ShapeDtypeStruct((m, w.shape[1]), Z_DTYPE) for w in weights]
    return pl.pallas_call(
        functools.partial(_inproj_kernel, n_out=n_out, col_chunk=512),
        grid=(m // tm,),
        in_specs=in_specs,
        out_specs=out_specs,
        out_shape=out_shape,
        compiler_params=_ARB1,
    )(x, mod, norm_w.reshape(1, D_MODEL), *weights)


def _outmlp_kernel(*refs, n_y, final, ff_chunk):
    y_refs = refs[:n_y]
    x_ref, mod_ref, nw_ref, wo_ref, w1_ref, w2_ref, fw_ref, o_ref = refs[n_y:]
    y = None
    row = 0
    for y_ref in y_refs:
        k = y_ref.shape[1]
        part = _dot(y_ref[...], wo_ref[row:row + k, :])
        y = part if y is None else y + part
        row += k
    x = x_ref[...] + mod_ref[2:3, :] * y
    h = _rms_norm(x, nw_ref[...]) * (1.0 + mod_ref[4:5, :]) + mod_ref[3:4, :]
    hb = h.astype(BF16)
    acc = None
    for c in range(0, D_FF, ff_chunk):
        u = jnp.maximum(_dot(hb, w1_ref[:, c:c + ff_chunk]), 0.0)
        part = _dot((u * u).astype(BF16), w2_ref[c:c + ff_chunk, :])
        acc = part if acc is None else acc + part
    x = x + mod_ref[5:6, :] * acc
    if final:
        x = _rms_norm(x, fw_ref[...])
    o_ref[...] = x


def _out_mlp(ys, x, mod, rows_per_mod, mod_base, norm_w, w_out, w1, w2, final_w, final):
    m = x.shape[0]
    tm = ROW_BLOCK
    const = lambda i: (0, 0)
    single = dict(pipeline_mode=pl.Buffered(1))
    in_specs = [pl.BlockSpec((tm, y.shape[1]), lambda i: (i, 0)) for y in ys] + [
        pl.BlockSpec((tm, D_MODEL), lambda i: (i, 0)),
        pl.BlockSpec((None, N_MOD, D_MODEL), lambda i: (mod_base + (i * tm) // rows_per_mod, 0, 0)),
        pl.BlockSpec((1, D_MODEL), const),
        pl.BlockSpec(w_out.shape, const, **single),
        pl.BlockSpec(w1.shape, const, **single),
        pl.BlockSpec(w2.shape, const, **single),
        pl.BlockSpec((1, D_MODEL), const),
    ]
    return pl.pallas_call(
        functools.partial(_outmlp_kernel, n_y=len(ys), final=final, ff_chunk=1024),
        grid=(m // tm,),
        in_specs=in_specs,
        out_specs=pl.BlockSpec((tm, D_MODEL), lambda i: (i, 0)),
        out_shape=jax.ShapeDtypeStruct((m, D_MODEL), F32),
        compiler_params=_ARB1,
    )(*ys, x, mod, norm_w.reshape(1, D_MODEL), w_out, w1, w2, final_w.reshape(1, D_MODEL))


def _even_kernel(z_ref, h0_ref, pw_ref, ps_ref, cw_ref, cb_ref, wg_ref, bg_ref, lam_ref, shift_ref,
                 y_ref, hfin_ref, rowbuf, a_s, b_s, h_s, hf_s, *, t_len, grid):
    seg = GRID_W if grid else t_len
    n_seg = t_len // seg
    n_groups = len(POOL_WINDOWS)
    pad = 8

    rowbuf[...] = jnp.zeros(rowbuf.shape, F32)
    pos = lax.broadcasted_iota(jnp.int32, (seg, POOL_GW), 0)

    def pool_segment(r):
        base = r * seg if isinstance(r, int) else pl.multiple_of(r * seg, seg)
        for g, w in enumerate(POOL_WINDOWS):
            lo_off = w // 2
            hi_off = w - 1 - lo_off
            cols = slice(g * POOL_GW, (g + 1) * POOL_GW)
            xg = z_ref[pl.ds(base, seg), cols].astype(F32)
            if grid:
                s1 = None
                for o in range(-lo_off, hi_off + 1):
                    rr = r + o
                    valid = jnp.logical_and(rr >= 0, rr < n_seg)
                    src = pl.multiple_of(jnp.clip(rr, 0, n_seg - 1) * seg, seg)
                    term = jnp.where(valid, z_ref[pl.ds(src, seg), cols].astype(F32), 0.0)
                    s1 = term if s1 is None else s1 + term
                cnt_r = jnp.minimum(r + hi_off, n_seg - 1) - jnp.maximum(r - lo_off, 0) + 1
                m1 = s1 / cnt_r.astype(F32)
            else:
                m1 = xg
            rowbuf[g, pad:pad + seg, :] = m1
            s2 = None
            for o in range(-lo_off, hi_off + 1):
                term = rowbuf[g, pad + o:pad + o + seg, :]
                s2 = term if s2 is None else s2 + term
            cnt_c = jnp.minimum(pos + hi_off, seg - 1) - jnp.maximum(pos - lo_off, 0) + 1
            d = s2 / cnt_c.astype(F32) - xg
            yg = _dot(d.astype(BF16), pw_ref[g]) * ps_ref[:, cols]
            y_ref[pl.ds(base, seg), cols] = yg.astype(y_ref.dtype)

    if n_seg == 1:
        pool_segment(0)
    else:
        def seg_body(r, carry):
            pool_segment(r)
            return carry
        lax.fori_loop(0, n_seg, seg_body, 0)

    ch = RG_CHUNK
    n_ch = t_len // ch
    rg_lo, rg_hi = D_POOL, D_POOL + D_RG

    def gates(c0, d):
        ext = _load_ext(z_ref, c0, ch, t_len, rg_lo, rg_hi, raw=True)
        parts = []
        for r0 in range(0, ch, CONV_ROWS):
            win = ext[r0:r0 + CONV_ROWS + 2 * HALO]
            taps = _dot(shift_ref[...], win)
            xp = cb_ref[...] + win[HALO:HALO + CONV_ROWS].astype(F32) * cw_ref[CONV_LEFT:CONV_LEFT + 1, :]
            for n, j in enumerate(jj for jj in range(CONV_W) if jj != CONV_LEFT):
                xp = xp + taps[n * CONV_ROWS:(n + 1) * CONV_ROWS] * cw_ref[j:j + 1, :]
            parts.append(xp)
        xc = jnp.concatenate(parts, axis=0)
        g = _dot(xc.astype(BF16), wg_ref[:, d * 2 * D_RG:(d + 1) * 2 * D_RG]) \
            + bg_ref[:, d * 2 * D_RG:(d + 1) * 2 * D_RG]
        r = _sigmoid(g[:, :D_RG])
        i = _sigmoid(g[:, D_RG:])
        log_a = (-RG_C) * r * _softplus(-lam_ref[d:d + 1, :])
        a = jnp.exp(log_a)
        a_s[...] = a
        b_s[...] = jnp.sqrt(jnp.tanh(-log_a) * (a * a + 1.0)) * (i * xc)

    def scan_rows(h, reverse):
        def row(t, h):
            tt = (ch - 1 - t) if reverse else t
            h = a_s[pl.ds(tt, 1), :] * h + b_s[pl.ds(tt, 1), :]
            h_s[pl.ds(tt, 1), :] = h
            return h
        return lax.fori_loop(0, ch, row, h, unroll=8)

    def fwd_body(i, h):
        c0 = pl.multiple_of(i * ch, ch)
        gates(c0, 0)
        h = scan_rows(h, False)
        hf_s[pl.ds(c0, ch), :] = h_s[...]
        return h

    h_fwd = lax.fori_loop(0, n_ch, fwd_body, h0_ref[0:1, :])

    def bwd_body(i, h):
        c0 = pl.multiple_of((n_ch - 1 - i) * ch, ch)
        gates(c0, 1)
        h = scan_rows(h, True)
        u_gate = z_ref[pl.ds(c0, ch), rg_hi:rg_hi + D_RG].astype(F32)
        y = (hf_s[pl.ds(c0, ch), :] + h_s[...]) * _gelu_tanh(u_gate)
        y_ref[pl.ds(c0, ch), D_POOL:D_POOL + D_RG] = y.astype(y_ref.dtype)
        return h

    h_bwd = lax.fori_loop(0, n_ch, bwd_body, h0_ref[1:2, :])
    hfin_ref[0:1, :] = h_fwd
    hfin_ref[1:2, :] = h_bwd


def _even_mixer(z, h0, p, n_batch, t_len, grid):
    seg = GRID_W if grid else t_len
    const2 = lambda b: (0, 0)
    const3 = lambda b: (0, 0, 0)
    return pl.pallas_call(
        functools.partial(_even_kernel, t_len=t_len, grid=grid),
        grid=(n_batch,),
        in_specs=[
            pl.BlockSpec((t_len, EV_COLS), lambda b: (b, 0)),
            pl.BlockSpec((None, 2, D_RG), lambda b: (b, 0, 0)),
            pl.BlockSpec(p['pool_w'].shape, const3),
            pl.BlockSpec((1, D_POOL), const2),
            pl.BlockSpec((CONV_W, D_RG), const2),
            pl.BlockSpec((1, D_RG), const2),
            pl.BlockSpec((D_RG, 4 * D_RG), const2),
            pl.BlockSpec((1, 4 * D_RG), const2),
            pl.BlockSpec((2, D_RG), const2),
            pl.BlockSpec(p['shift'].shape, const2),
        ],
        out_specs=[
            pl.BlockSpec((t_len, D_POOL + D_RG), lambda b: (b, 0)),
            pl.BlockSpec((None, 2, D_RG), lambda b: (b, 0, 0)),
        ],
        out_shape=[
            jax.ShapeDtypeStruct((n_batch * t_len, D_POOL + D_RG), Y_DTYPE),
            jax.ShapeDtypeStruct((n_batch, 2, D_RG), F32),
        ],
        scratch_shapes=[
            pltpu.VMEM((len(POOL_WINDOWS), seg + 16, POOL_GW), F32),
            pltpu.VMEM((RG_CHUNK, D_RG), F32),
            pltpu.VMEM((RG_CHUNK, D_RG), F32),
            pltpu.VMEM((RG_CHUNK, D_RG), F32),
            pltpu.VMEM((t_len, D_RG), F32),
        ],
        compiler_params=_ARB1,
    )(z, h0, p['pool_w'], p['pool_scale'], p['conv_w'], p['conv_b'], p['wg'], p['bg'], p['lam'], p['shift'])


def _rwkv_kernel(z_ref, s0_ref, mu_ref, kkw_ref, ka_ref, rk_ref, g2_ref, w0_ref, w2_ref, a0_ref, a2_ref,
                 lnw_ref, lnb_ref, tri_ref, bd_ref,
                 y_ref, sfin_ref, ebuf, s_s, yd_s, bon_s, g_s, *, t_len):
    L = SCAN_CHUNK
    n_ch = t_len // L
    L2 = 2 * L
    s_s[...] = s0_ref[...]


    def stack(x):
        first_head = lax.broadcasted_iota(jnp.int32, (L, LANES), 1) < RW_HEAD
        return jnp.concatenate([jnp.where(first_head, x, 0.0), jnp.where(first_head, 0.0, x)],
                               axis=0).astype(BF16)

    def prep(c0, d):
        ones_bd = bd_ref[...]
        ebuf[d] = _load_ext(z_ref, c0, L, t_len, 0, RW_ZCOLS)
        cur = ebuf[d, HALO:HALO + L, :]
        prev = ebuf[d, HALO - 1:HALO - 1 + L, :]
        nxt = ebuf[d, HALO + 1:HALO + 1 + L, :]
        zs = cur + mu_ref[0:1, :] * (prev - cur) + mu_ref[1:2, :] * (nxt - cur)
        r = zs[:, 0:D_RW]
        k = zs[:, D_RW:2 * D_RW]
        v = zs[:, 2 * D_RW:3 * D_RW]
        o = 3 * D_RW
        gd = zs[:, o:o + RW_G_LORA]
        wd = zs[:, o + RW_G_LORA:o + RW_G_LORA + LANES]
        ad = zs[:, o + RW_G_LORA + LANES:o + RW_G_LORA + 2 * LANES]

        kkv = k * kkw_ref[...]
        nrm = jnp.sqrt(_head_sum(kkv * kkv, ones_bd))
        kkn = kkv / jnp.maximum(nrm, 1e-12)

        w_log = -_softplus(-(w0_ref[d:d + 1, :] + _dot_f32(jnp.tanh(wd), w2_ref[d]))) - 0.5
        lw = -jnp.exp(w_log)
        a = _sigmoid(a0_ref[d:d + 1, :] + _dot_f32(ad, a2_ref[d]))
        kd = k * (1.0 + (a - 1.0) * ka_ref[...])
        bon_s[d, pl.ds(c0, L), :] = _head_sum(r * kd * rk_ref[...], ones_bd) * v
        if d == 0:
            g_s[pl.ds(c0, L), :] = _dot(_sigmoid(gd).astype(BF16), g2_ref[...])

        cum = _dot_exact_lhs(tri_ref[d], lw)
        e_pos = jnp.exp(cum)
        e_neg = jnp.exp(-cum)
        rh = r * e_pos
        kh = kd * e_neg
        bh = (kkn * a) * e_neg
        ah = -kkn * jnp.exp(cum - lw)
        last = L - 1 if d == 0 else 0
        gam = e_pos[last:last + 1, :]
        bhg = bh * gam
        khg = kh * gam

        chains = []
        for p in range(RW_PAIRS):
            cols = slice(p * LANES, (p + 1) * LANES)
            chains.append(dict(
                d=d, p=p, gam=gam[:, cols], v_st=stack(v[:, cols]),
                lhs=jnp.concatenate([stack(ah[:, cols]), stack(rh[:, cols])], axis=0),
                rhs=jnp.concatenate([stack(bh[:, cols]), stack(kh[:, cols])], axis=0),
                bk=jnp.concatenate([stack(bhg[:, cols]), stack(khg[:, cols])], axis=0)))
        return chains

    n_levels = int(math.log2(L))

    def body(i, carry):
        c0s = (pl.multiple_of(i * L, L), pl.multiple_of((n_ch - 1 - i) * L, L))
        chains = prep(c0s[0], 0) + prep(c0s[1], 1)
        ri = lax.broadcasted_iota(jnp.int32, (L2, L2), 0)
        ci = lax.broadcasted_iota(jnp.int32, (L2, L2), 1)
        eye = jnp.where(ri == ci, 1.0, 0.0)
        rl = jnp.where(ri < L, ri, ri - L)
        cl = jnp.where(ci < L, ci, ci - L)
        strict = (cl < rl, cl > rl)
        incl = (cl <= rl, cl >= rl)
        level = ri ^ ci

        for ch in chains:
            ch['sc'] = _dot_nt(ch['lhs'], ch['rhs'])
        for ch in chains:
            sc, d = ch['sc'], ch['d']
            n_ab = jnp.where(strict[d], sc[0:L2, 0:L2], 0.0)
            m_ak = jnp.where(strict[d], sc[0:L2, L2:2 * L2], 0.0).astype(BF16)
            t_rb = jnp.where(incl[d], sc[L2:2 * L2, 0:L2], 0.0)
            t_rk = jnp.where(incl[d], sc[L2:2 * L2, L2:2 * L2], 0.0)
            ch['t_rbk'] = jnp.concatenate([t_rb, t_rk], axis=1).astype(BF16)
            ch['mv'] = _dot(m_ak, ch['v_st'])
            ch['t'] = eye + jnp.where(level == 1, n_ab, 0.0)
            ch['n_lv'] = [jnp.where(lax.shift_right_logical(level, lv) == 1, n_ab, 0.0).astype(BF16)
                          for lv in range(1, n_levels)]
            del ch['sc']
        for lv in range(n_levels - 1):
            for ch in chains:
                ch['tb'] = ch['t'].astype(BF16)
                ch['x'] = _dot(ch['n_lv'][lv], ch['tb']).astype(BF16)
            for ch in chains:
                ch['t'] = ch['t'] + _dot(ch['tb'], ch['x'])

        for ch in chains:
            ch['s_prev'] = s_s[ch['d'], ch['p']]
            ch['ar_s'] = _dot_nt(ch['lhs'], ch['s_prev'].astype(BF16))
        for ch in chains:
            u_rhs = ch['ar_s'][0:L2] + ch['mv']
            u_st = _dot(ch['t'].astype(BF16), u_rhs.astype(BF16)).astype(BF16)
            ch['uv'] = jnp.concatenate([u_st, ch['v_st']], axis=0)
        for ch in chains:
            y_st = ch['ar_s'][L2:2 * L2] + _dot(ch['t_rbk'], ch['uv'])
            ch['y'] = y_st[0:L] + y_st[L:L2]
            s_s[ch['d'], ch['p']] = ch['s_prev'] * ch['gam'] + _dot_tn(ch['uv'], ch['bk'])
        for d in range(2):
            yd_s[d, pl.ds(c0s[d], L), :] = jnp.concatenate([ch['y'] for ch in chains if ch['d'] == d], axis=1)
        return carry

    lax.fori_loop(0, n_ch, body, 0, unroll=2)

    def epilogue(i, carry):
        c0 = pl.multiple_of(i * L, L)
        ones_bd = bd_ref[...]
        ys = yd_s[0, pl.ds(c0, L), :] + yd_s[1, pl.ds(c0, L), :]
        mu = _head_sum(ys, ones_bd) * (1.0 / RW_HEAD)
        yc = ys - mu
        var = _head_sum(yc * yc, ones_bd) * (1.0 / RW_HEAD)
        yn = yc * lax.rsqrt(var + RW_LN_EPS) * lnw_ref[...] + lnb_ref[...]
        out = (yn + (bon_s[0, pl.ds(c0, L), :] + bon_s[1, pl.ds(c0, L), :])) * g_s[pl.ds(c0, L), :]
        y_ref[pl.ds(c0, L), :] = out.astype(y_ref.dtype)
        return carry

    lax.fori_loop(0, n_ch, epilogue, 0)
    sfin_ref[...] = s_s[...]


def _rwkv_mixer(z, s0, p, n_batch, t_len):
    L = SCAN_CHUNK
    c2 = lambda b: (0, 0)
    c3 = lambda b: (0, 0, 0)
    st_spec = pl.BlockSpec((None, 2, RW_PAIRS, LANES, LANES), lambda b: (b, 0, 0, 0, 0))
    return pl.pallas_call(
        functools.partial(_rwkv_kernel, t_len=t_len),
        grid=(n_batch,),
        in_specs=[
            pl.BlockSpec((t_len, RW_ZCOLS), lambda b: (b, 0)),
            st_spec,
            pl.BlockSpec((2, RW_ZCOLS), c2),
            pl.BlockSpec((1, D_RW), c2),
            pl.BlockSpec((1, D_RW), c2),
            pl.BlockSpec((1, D_RW), c2),
            pl.BlockSpec((RW_G_LORA, D_RW), c2),
            pl.BlockSpec((2, D_RW), c2),
            pl.BlockSpec((2, LANES, D_RW), c3),
            pl.BlockSpec((2, D_RW), c2),
            pl.BlockSpec((2, LANES, D_RW), c3),
            pl.BlockSpec((1, D_RW), c2),
            pl.BlockSpec((1, D_RW), c2),
            pl.BlockSpec((2, L, L), c3),
            pl.BlockSpec((LANES, LANES), c2),
        ],
        out_specs=[pl.BlockSpec((t_len, D_RW), lambda b: (b, 0)), st_spec],
        out_shape=[
            jax.ShapeDtypeStruct((n_batch * t_len, D_RW), Y_DTYPE),
            jax.ShapeDtypeStruct((n_batch, 2, RW_PAIRS, LANES, LANES), F32),
        ],
        scratch_shapes=[
            pltpu.VMEM((2, L + 2 * HALO, RW_ZCOLS), F32),
            pltpu.VMEM((2, RW_PAIRS, LANES, LANES), F32),
            pltpu.VMEM((2, t_len, D_RW), F32),
            pltpu.VMEM((2, t_len, D_RW), F32),
            pltpu.VMEM((t_len, D_RW), F32),
        ],
        compiler_params=_ARB1,
    )(z, s0, p['mu'], p['kk'], p['ka'], p['rk'], p['g2'], p['w0'], p['w2'], p['a0'], p['a2'],
      p['ln_w'], p['ln_b'], p['tri'], p['ones_bd'])


def _mlstm2_kernel(z_ref, c0_ref, n0_ref, m0_ref, cw_ref, cb_ref, gb_ref, nw_ref, tri_ref, esel_ref, shift_ref,
                   ones_ref, y_ref, cfin_ref, nfin_ref, mfin_ref, c_s, n_s, m_s, hd_s, qk_s, *, t_len):
    L = ML_SCAN
    n_ch = t_len // L
    c_s[...] = c0_ref[...]
    n_s[...] = n0_ref[...]
    m_s[...] = m0_ref[...]
    q_scale = ML_DK ** -0.5
    g_lo = 4 * D_ML
    n_real = 4 * ML_HEADS

    def conv_chunk(i, carry):
        c0 = pl.multiple_of(i * L, L)
        ext = _load_ext(z_ref, c0, L, t_len, 0, 2 * D_ML, raw=True)
        taps = _dot(shift_ref[...], ext)
        qk = cb_ref[...] + ext[HALO:HALO + L].astype(F32) * cw_ref[CONV_LEFT:CONV_LEFT + 1, :]
        for n, j in enumerate(jj for jj in range(CONV_W) if jj != CONV_LEFT):
            qk = qk + taps[n * L:(n + 1) * L] * cw_ref[j:j + 1, :]
        qk = _silu(qk)
        qk_s[pl.ds(c0, L), 0:D_ML] = (qk[:, :D_ML] * q_scale).astype(BF16)
        qk_s[pl.ds(c0, L), D_ML:2 * D_ML] = qk[:, D_ML:].astype(BF16)
        return carry

    lax.fori_loop(0, n_ch, conv_chunk, 0)

    def prep(c0, d):
        q = qk_s[pl.ds(c0, L), 0:D_ML]
        k = qk_s[pl.ds(c0, L), D_ML:2 * D_ML]
        v = z_ref[pl.ds(c0, L), 2 * D_ML:3 * D_ML]
        gts = z_ref[pl.ds(c0, L), g_lo:g_lo + LANES].astype(F32) + gb_ref[...]
        x16 = gts.T[0:n_real, :]
        lf_hi, lf_lo = _split2(-_softplus(-x16))
        b16 = _dot_nt(lf_hi, tri_ref[d]) + _dot_nt(lf_lo, tri_ref[d])
        x8 = x16[8 * d:8 * d + 8, :]
        b8 = b16[8 * d:8 * d + 8, :]
        g8 = pltpu.roll(x8, ML_HEADS, axis=0) - b8
        cm8 = g8
        lane = lax.broadcasted_iota(jnp.int32, (8, L), 1)
        step = 1
        while step < L:
            if d == 0:
                sh = jnp.where(lane >= step, pltpu.roll(cm8, step, axis=1), -jnp.inf)
            else:
                sh = jnp.where(lane < L - step, pltpu.roll(cm8, L - step, axis=1), -jnp.inf)
            cm8 = jnp.maximum(cm8, sh)
            step *= 2
        x32 = jnp.concatenate([g8, b8, cm8, jnp.zeros((8, L), F32)], axis=0)
        hi = x32.astype(BF16)
        r1 = x32 - hi.astype(F32)
        mid = r1.astype(BF16)
        lo = (r1 - mid.astype(F32)).astype(BF16)
        cb = _dot_tn(jnp.concatenate([hi, mid, lo], axis=0), esel_ref[...])
        ri = lax.broadcasted_iota(jnp.int32, (L, L), 0)
        ci = lax.broadcasted_iota(jnp.int32, (L, L), 1)
        causal = (ci <= ri) if d == 0 else (ci >= ri)
        last = L - 1 if d == 0 else 0
        ones = ones_ref[...]
        chains = []
        for h in range(ML_HEADS):
            cols = slice(h * ML_DK, (h + 1) * ML_DK)
            chains.append(dict(
                d=d, h=h, last=last, causal=causal,
                g_row=g8[ML_HEADS + h:ML_HEADS + h + 1, :],
                g_b=cb[:, cols], b_b=cb[:, D_ML + h * ML_DK:D_ML + (h + 1) * ML_DK],
                cm_b=cb[:, 2 * D_ML + h * ML_DK:2 * D_ML + (h + 1) * ML_DK],
                qb=q[:, cols], kb=k[:, cols],
                v=v[:, cols].astype(F32), v_ones=jnp.concatenate([v[:, cols].astype(BF16), ones], axis=1)))
        return chains

    def body(i, carry):
        c0s = (pl.multiple_of(i * L, L), pl.multiple_of((n_ch - 1 - i) * L, L))
        chains = prep(c0s[0], 0) + prep(c0s[1], 1)
        for ch in chains:
            d, h = ch['d'], ch['h']
            ch['qk'] = _dot_nt(ch['qb'], ch['kb'])
            ch['c_prev'] = c_s[d, h]
            ch['n_prev'] = n_s[d, h]
            ch['m_prev'] = m_s[d, h:h + 1, :]
            ch['qcn'] = _dot(ch['qb'], jnp.concatenate([ch['c_prev'].astype(BF16),
                                                        ch['n_prev'].astype(BF16)], axis=1))
        for ch in chains:
            mu = jnp.maximum(ch['m_prev'], ch['cm_b'])
            ch['mu'] = mu
            ch['s'] = ch['qk'] * jnp.exp(jnp.where(ch['causal'], ch['g_row'] - mu, -jnp.inf))
            ch['w_inter'] = jnp.exp(ch['m_prev'] - mu)
        for ch in chains:
            last = ch['last']
            ch['sv'] = _dot(ch['s'].astype(BF16), ch['v_ones'])
            mu_last = ch['mu'][last:last + 1, :]
            ch['m_new'] = ch['b_b'][last:last + 1, :] + mu_last
            ch['dec'] = jnp.exp(ch['m_prev'] - mu_last)
            w_s = jnp.exp(ch['g_b'] - mu_last)
            ch['upd'] = _dot_tn(ch['kb'], jnp.concatenate([(w_s * ch['v']).astype(BF16),
                                                           w_s.astype(BF16)], axis=1))
        for ch in chains:
            d, h = ch['d'], ch['h']
            num = ch['sv'][:, :ML_DK] + ch['w_inter'] * ch['qcn'][:, :ML_DK]
            den = ch['sv'][:, ML_DK:] + ch['w_inter'] * ch['qcn'][:, ML_DK:]
            ch['out'] = num / jnp.maximum(jnp.abs(den), jnp.exp(-(ch['b_b'] + ch['mu'])))
            c_s[d, h] = ch['dec'] * ch['c_prev'] + ch['upd'][:, :ML_DK]
            n_s[d, h] = ch['dec'] * ch['n_prev'] + ch['upd'][:, ML_DK:]
            m_s[d, h:h + 1, :] = ch['m_new']
        for d in range(2):
            hd_s[d, pl.ds(c0s[d], L), :] = jnp.concatenate([ch['out'] for ch in chains if ch['d'] == d], axis=1)
        return carry

    lax.fori_loop(0, n_ch, body, 0, unroll=2)

    def epilogue(i, carry):
        c0 = pl.multiple_of(i * L, L)
        ones = ones_ref[...]
        hs = hd_s[0, pl.ds(c0, L), :] + hd_s[1, pl.ds(c0, L), :]
        o = z_ref[pl.ds(c0, L), 3 * D_ML:4 * D_ML].astype(F32)
        mu = _head_sum(hs, ones) * (1.0 / ML_DK)
        xc = hs - mu
        var = _head_sum(xc * xc, ones) * (1.0 / ML_DK)
        y = xc * lax.rsqrt(var + NORM_EPS) * nw_ref[...] * _sigmoid(o)
        y_ref[pl.ds(c0, L), :] = y.astype(y_ref.dtype)
        return carry

    lax.fori_loop(0, n_ch, epilogue, 0)
    cfin_ref[...] = c_s[...]
    nfin_ref[...] = n_s[...]
    mfin_ref[...] = m_s[...]


def _mlstm2_mixer(z, c0, n0, m0, p, n_batch, t_len):
    assert z.dtype == BF16
    L = ML_SCAN
    c2 = lambda b: (0, 0)
    c3 = lambda b: (0, 0, 0)
    c_spec = pl.BlockSpec((None, 2, ML_HEADS, ML_DK, ML_DK), lambda b: (b, 0, 0, 0, 0))
    m_spec = pl.BlockSpec((None, 2, ML_HEADS, ML_DK), lambda b: (b, 0, 0, 0))
    return pl.pallas_call(
        functools.partial(_mlstm2_kernel, t_len=t_len),
        grid=(n_batch,),
        in_specs=[
            pl.BlockSpec((t_len, ML_ZCOLS), lambda b: (b, 0)),
            c_spec, c_spec, m_spec,
            pl.BlockSpec((CONV_W, 2 * D_ML), c2),
            pl.BlockSpec((1, 2 * D_ML), c2),
            pl.BlockSpec((1, LANES), c2),
            pl.BlockSpec((1, D_ML), c2),
            pl.BlockSpec((2, L, L), c3),
            pl.BlockSpec(p['esel'].shape, c2),
            pl.BlockSpec(p['shift'].shape, c2),
            pl.BlockSpec((L, LANES), c2),
        ],
        out_specs=[pl.BlockSpec((t_len, D_ML), lambda b: (b, 0)), c_spec, c_spec, m_spec],
        out_shape=[
            jax.ShapeDtypeStruct((n_batch * t_len, D_ML), Y_DTYPE),
            jax.ShapeDtypeStruct((n_batch, 2, ML_HEADS, ML_DK, ML_DK), F32),
            jax.ShapeDtypeStruct((n_batch, 2, ML_HEADS, ML_DK, ML_DK), F32),
            jax.ShapeDtypeStruct((n_batch, 2, ML_HEADS, ML_DK), F32),
        ],
        scratch_shapes=[
            pltpu.VMEM((2, ML_HEADS, ML_DK, ML_DK), F32),
            pltpu.VMEM((2, ML_HEADS, ML_DK, ML_DK), F32),
            pltpu.VMEM((2, ML_HEADS, ML_DK), F32),
            pltpu.VMEM((2, t_len, D_ML), F32),
            pltpu.VMEM((t_len, 2 * D_ML), BF16),
        ],
        compiler_params=_ARB1,
    )(z, c0, n0, m0, p['conv_w'], p['conv_b'], p['gate_b'], p['norm_w'], p['tri'], p['esel'], p['shift'],
      p['ones'])


def _conv_shift(n):
    taps = jnp.array([j for j in range(CONV_W) if j != CONV_LEFT])
    t = jnp.arange((CONV_W - 1) * n)
    src = HALO + t % n + taps[t // n] - CONV_LEFT
    return (src[:, None] == jnp.arange(n + 2 * HALO)[None, :]).astype(BF16)


def _mlstm_consts():
    L = ML_SCAN
    lower = jnp.tril(jnp.ones((L, L), F32))
    tri = jnp.stack([lower, lower.T]).astype(BF16)
    rows = jnp.arange(32)
    blk = jnp.arange(3 * D_ML) // ML_DK
    src_row = 8 * (blk // ML_HEADS) + ML_HEADS + blk % ML_HEADS
    esel = (rows[:, None] == src_row[None, :]).astype(BF16)
    esel = jnp.concatenate([esel, esel, esel], axis=0)
    return dict(tri=tri, esel=esel, shift=_conv_shift(L), ones=jnp.ones((L, LANES), BF16))


def _block_diag(w):
    n, c, d = w.shape
    return jnp.einsum('ncd,nm->ncmd', w, jnp.eye(n, dtype=w.dtype)).reshape(n * c, n * d)


def _scan_tri():
    L = SCAN_CHUNK
    lower = jnp.tril(jnp.ones((L, L), F32))
    return jnp.stack([lower, lower.T]).astype(BF16)


def _prep_even(ev_w_in, pool_w, pool_scale, rg_conv_w, rg_conv_b, rg_wa, rg_ba, rg_wx, rg_bx, rg_lam, ev_w_out, j):
    wg = jnp.concatenate([_block_diag(rg_wa[j, 0]), _block_diag(rg_wx[j, 0]),
                          _block_diag(rg_wa[j, 1]), _block_diag(rg_wx[j, 1])], axis=1)
    bg = jnp.concatenate([rg_ba[j, 0], rg_bx[j, 0], rg_ba[j, 1], rg_bx[j, 1]]).reshape(1, 4 * D_RG)
    return dict(
        w_in=[ev_w_in[j].astype(BF16)],
        w_out=ev_w_out[j].astype(BF16),
        pool_w=pool_w[j].astype(BF16),
        pool_scale=pool_scale[j].reshape(1, D_POOL),
        conv_w=rg_conv_w[j],
        conv_b=rg_conv_b[j].reshape(1, D_RG),
        wg=wg.astype(BF16),
        bg=bg,
        lam=rg_lam[j],
        shift=_conv_shift(CONV_ROWS),
    )


def _pad_cols(w, cuts, widths):
    pieces = jnp.split(w, cuts, axis=-1)
    out = []
    for piece, width in zip(pieces, widths):
        extra = width - piece.shape[-1]
        if extra:
            piece = jnp.pad(piece, [(0, 0)] * (piece.ndim - 1) + [(0, extra)])
        out.append(piece)
    return jnp.concatenate(out, axis=-1)


def _prep_odd(od_w_in, rw_mu, rw_w0, rw_w2, rw_a0, rw_a2, rw_kk, rw_ka, rw_rk, rw_g2, rw_ln_w, rw_ln_b,
              ml_conv_w, ml_conv_b, ml_bi, ml_bf, ml_norm_w, od_w_out, j):
    rw_cols = 3 * D_RW + RW_G_LORA + RW_W_LORA + RW_A_LORA
    base = 3 * D_RW + RW_G_LORA
    cuts = [base, base + RW_W_LORA]
    widths = [base, LANES, LANES]
    w_rw = _pad_cols(od_w_in[j][:, :rw_cols], cuts, widths)
    w_ml = _pad_cols(od_w_in[j][:, rw_cols:], [4 * D_ML], [4 * D_ML, LANES])
    pad_rows = lambda w: jnp.pad(w, ((0, 0), (0, LANES - w.shape[1]), (0, 0)))
    gate_b = jnp.concatenate([ml_bi[j, 0], ml_bf[j, 0], ml_bi[j, 1], ml_bf[j, 1]])
    gate_b = jnp.pad(gate_b, (0, LANES - gate_b.shape[0])).reshape(1, LANES)
    hm = jnp.arange(LANES) // RW_HEAD
    rw = dict(
        mu=_pad_cols(rw_mu[j], cuts, widths),
        kk=rw_kk[j].reshape(1, D_RW), ka=rw_ka[j].reshape(1, D_RW), rk=rw_rk[j].reshape(1, D_RW),
        g2=rw_g2[j].astype(BF16),
        w0=rw_w0[j], w2=pad_rows(rw_w2[j]), a0=rw_a0[j], a2=pad_rows(rw_a2[j]),
        ln_w=rw_ln_w[j].reshape(1, D_RW), ln_b=rw_ln_b[j].reshape(1, D_RW),
        tri=_scan_tri(),
        ones_bd=(hm[:, None] == hm[None, :]).astype(BF16),
    )
    ml = dict(
        conv_w=ml_conv_w[j], conv_b=ml_conv_b[j].reshape(1, 2 * D_ML),
        gate_b=gate_b, norm_w=ml_norm_w[j].reshape(1, D_ML), **_mlstm_consts(),
    )
    return dict(w_in=[w_rw.astype(BF16), w_ml.astype(BF16)], w_out=od_w_out[j].astype(BF16), rw=rw, ml=ml)


def _rwkv_state_in(s):
    b = s.shape[0]
    s = s.reshape(b, 2, RW_PAIRS, 2, RW_HEAD, RW_HEAD)
    bd = jnp.einsum('bdpivk,ij->bdpivjk', s, jnp.eye(2, dtype=s.dtype))
    return bd.reshape(b, 2, RW_PAIRS, LANES, LANES)


def _rwkv_state_out(bd):
    b = bd.shape[0]
    x = bd.reshape(b, 2, RW_PAIRS, 2, RW_HEAD, 2, RW_HEAD)
    s = jnp.stack([x[:, :, :, 0, :, 0, :], x[:, :, :, 1, :, 1, :]], axis=3)
    return s.reshape(b, 2, RW_HEADS, RW_HEAD, RW_HEAD)


def _trunk(x, mod, rows_per_mod, mod_base, grid, states, params, n_batch, t_len):
    ev, od, dense = params
    rg0, rw0, c0, n0, m0 = states
    x = x.reshape(n_batch * t_len, D_MODEL)
    spec = (rows_per_mod, mod_base)

    (z,) = _in_proj(x, mod[0], *spec, dense['norm1_w'][0], ev['w_in'])
    y_ev, rg_f = _even_mixer(z, rg0, ev, n_batch, t_len, grid)
    x = _out_mlp([y_ev], x, mod[0], *spec, dense['norm2_w'][0], ev['w_out'],
                 dense['mlp_w1'][0], dense['mlp_w2'][0], dense['final_norm_w'], False)

    z_rw, z_ml = _in_proj(x, mod[1], *spec, dense['norm1_w'][1], od['w_in'])
    y_rw, s_f = _rwkv_mixer(z_rw, rw0, od['rw'], n_batch, t_len)
    y_ml, c_f, n_f, m_f = _mlstm2_mixer(z_ml, c0, n0, m0, od['ml'], n_batch, t_len)
    y = _out_mlp([y_rw, y_ml], x, mod[1], *spec, dense['norm2_w'][1], od['w_out'],
                 dense['mlp_w1'][1], dense['mlp_w2'][1], dense['final_norm_w'], True)
    return y.reshape(n_batch, t_len, D_MODEL), (rg_f, s_f, c_f, n_f, m_f)


def kernel(x_prompt, x_sample, state_rglru, state_rwkv, state_mlstm_C, state_mlstm_n, state_mlstm_m,
           c, c_ctx, norm1_w, norm2_w, w_mod, b_mod, mlp_w1, mlp_w2, final_norm_w,
           ev_w_in, pool_w, pool_scale, rg_conv_w, rg_conv_b, rg_wa, rg_ba, rg_wx, rg_bx, rg_lam, ev_w_out,
           od_w_in, rw_mu, rw_w0, rw_w2, rw_a0, rw_a2, rw_kk, rw_ka, rw_rk, rw_g2, rw_ln_w, rw_ln_b,
           ml_conv_w, ml_conv_b, ml_bi, ml_bf, ml_norm_w, od_w_out):
    bp, tp, _ = x_prompt.shape
    bs, ts, _ = x_sample.shape

    n_cond = 16
    cond = jnp.concatenate([c_ctx[None, :], c, jnp.zeros((n_cond - 1 - bs, D_MODEL), F32)], axis=0)
    mod = _modulation(cond, w_mod, b_mod)

    ev = _prep_even(ev_w_in, pool_w, pool_scale, rg_conv_w, rg_conv_b, rg_wa, rg_ba, rg_wx, rg_bx, rg_lam,
                    ev_w_out, 0)
    od = _prep_odd(od_w_in, rw_mu, rw_w0, rw_w2, rw_a0, rw_a2, rw_kk, rw_ka, rw_rk, rw_g2, rw_ln_w, rw_ln_b,
                   ml_conv_w, ml_conv_b, ml_bi, ml_bf, ml_norm_w, od_w_out, 0)
    dense = dict(norm1_w=norm1_w, norm2_w=norm2_w, mlp_w1=mlp_w1.astype(BF16), mlp_w2=mlp_w2.astype(BF16),
                 final_norm_w=final_norm_w)
    params = (ev, od, dense)

    zero_states = (
        jnp.zeros((bp, 2, D_RG), F32),
        jnp.zeros((bp, 2, RW_PAIRS, LANES, LANES), F32),
        jnp.zeros((bp, 2, ML_HEADS, ML_DK, ML_DK), F32),
        jnp.zeros((bp, 2, ML_HEADS, ML_DK, ML_DK), F32),
        jnp.zeros((bp, 2, ML_HEADS, ML_DK), F32),
    )
    y_prompt, (rg_f, s_f, c_f, n_f, m_f) = _trunk(
        x_prompt, mod, bp * tp, 0, False, zero_states, params, bp, tp)

    sample_states = (
        state_rglru[:, 0],
        _rwkv_state_in(state_rwkv[:, 0]),
        state_mlstm_C[:, 0],
        jnp.broadcast_to(state_mlstm_n[:, 0][..., None], (bs, 2, ML_HEADS, ML_DK, ML_DK)),
        jnp.broadcast_to(state_mlstm_m[:, 0][..., None], (bs, 2, ML_HEADS, ML_DK)),
    )
    y_sample, _ = _trunk(x_sample, mod, ts, 1, True, sample_states, params, bs, ts)

    return (y_prompt, y_sample,
            rg_f[:, None],
            _rwkv_state_out(s_f)[:, None],
            c_f[:, None],
            n_f[:, None, :, :, :, 0],
            m_f[:, None, :, :, 0])
```

```python
import functools
import math

import jax
import jax.numpy as jnp
from jax import lax
from jax.experimental import pallas as pl
from jax.experimental.pallas import tpu as pltpu

F32 = jnp.float32
BF16 = jnp.bfloat16

D_MODEL = 1024
DEPTH = 2
GRID_W = 64
D_FF = 4 * D_MODEL
NORM_EPS = 1e-6
N_MOD = 6

D_POOL = D_MODEL // 2
POOL_WINDOWS = (2, 4, 8, 16)
POOL_GW = D_POOL // len(POOL_WINDOWS)
D_RG = D_MODEL // 2
RG_BLOCKS = 8
RG_C = 8.0
CONV_W = 4
CONV_LEFT = 2
EV_COLS = D_POOL + 2 * D_RG

D_RW = D_MODEL // 2
RW_HEAD = 64
RW_HEADS = D_RW // RW_HEAD
RW_PAIRS = RW_HEADS // 2
RW_W_LORA = 64
RW_A_LORA = 64
RW_G_LORA = 128
RW_LN_EPS = 64e-5
RW_ZCOLS = 3 * D_RW + RW_G_LORA + 128 + 128
D_ML = D_MODEL // 2
ML_HEADS = 4
ML_DK = D_ML // ML_HEADS
ML_ZCOLS = 4 * D_ML + 128

LANES = 128
HALO = 16
Z_DTYPE = BF16
Y_DTYPE = BF16
ROW_BLOCK = 512
SCAN_CHUNK = 64
ML_SCAN = 128
RG_CHUNK = 256
VMEM_LIMIT = 56 * 1024 * 1024

_ARB1 = pltpu.CompilerParams(dimension_semantics=("arbitrary",), vmem_limit_bytes=VMEM_LIMIT)
_ARB2 = pltpu.CompilerParams(dimension_semantics=("arbitrary", "arbitrary"), vmem_limit_bytes=VMEM_LIMIT)


def _dot(a, b):
    return jnp.dot(a, b, preferred_element_type=F32)


def _dot_nt(a, b):
    return lax.dot_general(a, b, (((1,), (1,)), ((), ())), preferred_element_type=F32)


def _dot_tn(a, b):
    return lax.dot_general(a, b, (((0,), (0,)), ((), ())), preferred_element_type=F32)


def _split2(x):
    hi = x.astype(BF16)
    lo = (x - hi.astype(F32)).astype(BF16)
    return hi, lo


def _dot_f32(a, b):
    ah, al = _split2(a)
    bh, bl = _split2(b)
    return _dot(ah, bh) + (_dot(ah, bl) + _dot(al, bh))


def _dot_exact_lhs(m_bf16, x):
    hi, lo = _split2(x)
    return _dot(m_bf16, hi) + _dot(m_bf16, lo)


def _dot_exact_rhs(x, m_bf16):
    hi, lo = _split2(x)
    return _dot(hi, m_bf16) + _dot(lo, m_bf16)


def _softplus(x):
    return jnp.maximum(x, 0.0) + jnp.log1p(jnp.exp(-jnp.abs(x)))


def _sigmoid(x):
    return jax.nn.sigmoid(x)


def _silu(x):
    return x * jax.nn.sigmoid(x)


def _gelu_tanh(x):
    c = math.sqrt(2.0 / math.pi)
    return x * (0.5 * (1.0 + jnp.tanh(c * (x + 0.044715 * (x * x * x)))))


def _rms_norm(x, w):
    ms = jnp.mean(x * x, axis=-1, keepdims=True)
    return x * lax.rsqrt(ms + NORM_EPS) * w


def _load_ext(z_ref, c0, n, t_len, lo, hi, raw=False):
    dt = z_ref.dtype if raw else F32
    cur = z_ref[pl.ds(c0, n), lo:hi].astype(dt)
    pb = pl.multiple_of(jnp.maximum(c0 - HALO, 0), HALO)
    pa = pl.multiple_of(jnp.minimum(c0 + n, t_len - HALO), HALO)
    before = z_ref[pl.ds(pb, HALO), lo:hi].astype(dt)
    after = z_ref[pl.ds(pa, HALO), lo:hi].astype(dt)
    before = jnp.where(c0 > 0, before, jnp.zeros_like(before))
    after = jnp.where(c0 + n < t_len, after, jnp.zeros_like(after))
    return jnp.concatenate([before, cur, after], axis=0)


def _head_sum(x, ones_bd):
    n = x.shape[0]
    groups = x.shape[1] // LANES
    xs = jnp.concatenate([x[:, g * LANES:(g + 1) * LANES] for g in range(groups)], axis=0)
    s = _dot_exact_rhs(xs, ones_bd)
    return jnp.concatenate([s[g * n:(g + 1) * n] for g in range(groups)], axis=1)


def _mod_kernel(c_ref, w_ref, b_ref, o_ref):
    s = _silu(c_ref[...])
    o_ref[...] = _dot_f32(s, w_ref[...]) + b_ref[...]


def _modulation(cond, w_mod, b_mod):
    rows = cond.shape[0]
    tn = 1536
    out = pl.pallas_call(
        _mod_kernel,
        grid=(DEPTH, N_MOD * D_MODEL // tn),
        in_specs=[
            pl.BlockSpec((rows, D_MODEL), lambda l, j: (0, 0)),
            pl.BlockSpec((None, D_MODEL, tn), lambda l, j: (l, 0, j)),
            pl.BlockSpec((None, 1, tn), lambda l, j: (l, 0, j)),
        ],
        out_specs=pl.BlockSpec((None, rows, tn), lambda l, j: (l, 0, j)),
        out_shape=jax.ShapeDtypeStruct((DEPTH, rows, N_MOD * D_MODEL), F32),
        compiler_params=_ARB2,
    )(cond, w_mod, b_mod.reshape(DEPTH, 1, N_MOD * D_MODEL))
    return out.reshape(DEPTH, rows, N_MOD, D_MODEL)


def _inproj_kernel(*refs, n_out, col_chunk):
    x_ref, mod_ref, nw_ref = refs[:3]
    w_refs = refs[3:3 + n_out]
    z_refs = refs[3 + n_out:]
    h = _rms_norm(x_ref[...], nw_ref[...])
    h = h * (1.0 + mod_ref[1:2, :]) + mod_ref[0:1, :]
    hb = h.astype(BF16)
    for w_ref, z_ref in zip(w_refs, z_refs):
        n = w_ref.shape[1]
        for c in range(0, n, col_chunk):
            e = min(c + col_chunk, n)
            z_ref[:, c:e] = _dot(hb, w_ref[:, c:e]).astype(z_ref.dtype)


def _in_proj(x, mod, rows_per_mod, mod_base, norm_w, weights):
    m = x.shape[0]
    tm = ROW_BLOCK
    n_out = len(weights)
    const = lambda i: (0, 0)
    in_specs = [
        pl.BlockSpec((tm, D_MODEL), lambda i: (i, 0)),
        pl.BlockSpec((None, N_MOD, D_MODEL), lambda i: (mod_base + (i * tm) // rows_per_mod, 0, 0)),
        pl.BlockSpec((1, D_MODEL), const),
    ] + [pl.BlockSpec(w.shape, const, pipeline_mode=pl.Buffered(1)) for w in weights]
    out_specs = [pl.BlockSpec((tm, w.shape[1]), lambda i: (i, 0)) for w in weights]
    out_shape = [jax.ShapeDtypeStruct((m, w.shape[1]), Z_DTYPE) for w in weights]
    return pl.pallas_call(
        functools.partial(_inproj_kernel, n_out=n_out, col_chunk=512),
        grid=(m // tm,),
        in_specs=in_specs,
        out_specs=out_specs,
        out_shape=out_shape,
        compiler_params=_ARB1,
    )(x, mod, norm_w.reshape(1, D_MODEL), *weights)


def _outmlp_kernel(*refs, n_y, final, ff_chunk):
    y_refs = refs[:n_y]
    x_ref, mod_ref, nw_ref, wo_ref, w1_ref, w2_ref, fw_ref, o_ref = refs[n_y:]
    y = None
    row = 0
    for y_ref in y_refs:
        k = y_ref.shape[1]
        part = _dot(y_ref[...], wo_ref[row:row + k, :])
        y = part if y is None else y + part
        row += k
    x = x_ref[...] + mod_ref[2:3, :] * y
    h = _rms_norm(x, nw_ref[...]) * (1.0 + mod_ref[4:5, :]) + mod_ref[3:4, :]
    hb = h.astype(BF16)
    acc = None
    for c in range(0, D_FF, ff_chunk):
        u = jnp.maximum(_dot(hb, w1_ref[:, c:c + ff_chunk]), 0.0)
        part = _dot((u * u).astype(BF16), w2_ref[c:c + ff_chunk, :])
        acc = part if acc is None else acc + part
    x = x + mod_ref[5:6, :] * acc
    if final:
        x = _rms_norm(x, fw_ref[...])
    o_ref[...] = x


def _out_mlp(ys, x, mod, rows_per_mod, mod_base, norm_w, w_out, w1, w2, layer, final_w, final):
    m = x.shape[0]
    tm = ROW_BLOCK
    const = lambda i: (0, 0)
    single = dict(pipeline_mode=pl.Buffered(1))
    pick = lambda i: (layer, 0, 0)
    in_specs = [pl.BlockSpec((tm, y.shape[1]), lambda i: (i, 0)) for y in ys] + [
        pl.BlockSpec((tm, D_MODEL), lambda i: (i, 0)),
        pl.BlockSpec((None, N_MOD, D_MODEL), lambda i: (mod_base + (i * tm) // rows_per_mod, 0, 0)),
        pl.BlockSpec((1, D_MODEL), const),
        pl.BlockSpec(w_out.shape, const, **single),
        pl.BlockSpec((None,) + w1.shape[1:], pick, **single),
        pl.BlockSpec((None,) + w2.shape[1:], pick, **single),
        pl.BlockSpec((1, D_MODEL), const),
    ]
    return pl.pallas_call(
        functools.partial(_outmlp_kernel, n_y=len(ys), final=final, ff_chunk=1024),
        grid=(m // tm,),
        in_specs=in_specs,
        out_specs=pl.BlockSpec((tm, D_MODEL), lambda i: (i, 0)),
        out_shape=jax.ShapeDtypeStruct((m, D_MODEL), F32),
        compiler_params=_ARB1,
    )(*ys, x, mod, norm_w.reshape(1, D_MODEL), w_out, w1, w2, final_w.reshape(1, D_MODEL))


def _even_kernel(z_ref, h0_ref, pw_ref, ps_ref, cw_ref, cb_ref, wg_ref, bg_ref, lam_ref,
                 y_ref, hfin_ref, rowbuf, cbuf, a_s, b_s, h_s, hf_s, *, t_len, grid):
    seg = GRID_W if grid else t_len
    n_seg = t_len // seg
    n_groups = len(POOL_WINDOWS)
    pad = 8

    rowbuf[...] = jnp.zeros(rowbuf.shape, F32)
    pos = lax.broadcasted_iota(jnp.int32, (seg, POOL_GW), 0)

    def pool_segment(r):
        base = r * seg if isinstance(r, int) else pl.multiple_of(r * seg, seg)
        for g, w in enumerate(POOL_WINDOWS):
            lo_off = w // 2
            hi_off = w - 1 - lo_off
            cols = slice(g * POOL_GW, (g + 1) * POOL_GW)
            xg = z_ref[pl.ds(base, seg), cols].astype(F32)
            if grid:
                s1 = None
                for o in range(-lo_off, hi_off + 1):
                    rr = r + o
                    valid = jnp.logical_and(rr >= 0, rr < n_seg)
                    src = pl.multiple_of(jnp.clip(rr, 0, n_seg - 1) * seg, seg)
                    term = jnp.where(valid, z_ref[pl.ds(src, seg), cols].astype(F32), 0.0)
                    s1 = term if s1 is None else s1 + term
                cnt_r = jnp.minimum(r + hi_off, n_seg - 1) - jnp.maximum(r - lo_off, 0) + 1
                m1 = s1 / cnt_r.astype(F32)
            else:
                m1 = xg
            rowbuf[g, pad:pad + seg, :] = m1
            s2 = None
            for o in range(-lo_off, hi_off + 1):
                term = rowbuf[g, pad + o:pad + o + seg, :]
                s2 = term if s2 is None else s2 + term
            cnt_c = jnp.minimum(pos + hi_off, seg - 1) - jnp.maximum(pos - lo_off, 0) + 1
            d = s2 / cnt_c.astype(F32) - xg
            yg = _dot(d.astype(BF16), pw_ref[g]) * ps_ref[:, cols]
            y_ref[pl.ds(base, seg), cols] = yg.astype(y_ref.dtype)

    if n_seg == 1:
        pool_segment(0)
    else:
        def seg_body(r, carry):
            pool_segment(r)
            return carry
        lax.fori_loop(0, n_seg, seg_body, 0)

    ch = RG_CHUNK
    n_ch = t_len // ch
    rg_lo, rg_hi = D_POOL, D_POOL + D_RG

    def gates(c0, d):
        cbuf[...] = _load_ext(z_ref, c0, ch, t_len, rg_lo, rg_hi)
        xc = cb_ref[...]
        for j in range(CONV_W):
            off = HALO - CONV_LEFT + j
            xc = xc + cbuf[off:off + ch, :] * cw_ref[j:j + 1, :]
        g = _dot(xc.astype(BF16), wg_ref[:, d * 2 * D_RG:(d + 1) * 2 * D_RG]) \
            + bg_ref[:, d * 2 * D_RG:(d + 1) * 2 * D_RG]
        r = _sigmoid(g[:, :D_RG])
        i = _sigmoid(g[:, D_RG:])
        log_a = (-RG_C) * r * _softplus(-lam_ref[d:d + 1, :])
        a = jnp.exp(log_a)
        a_s[...] = a
        b_s[...] = jnp.sqrt(jnp.tanh(-log_a) * (a * a + 1.0)) * (i * xc)

    def scan_rows(h, reverse):
        def row(t, h):
            tt = (ch - 1 - t) if reverse else t
            h = a_s[pl.ds(tt, 1), :] * h + b_s[pl.ds(tt, 1), :]
            h_s[pl.ds(tt, 1), :] = h
            return h
        return lax.fori_loop(0, ch, row, h, unroll=8)

    def fwd_body(i, h):
        c0 = pl.multiple_of(i * ch, ch)
        gates(c0, 0)
        h = scan_rows(h, False)
        hf_s[pl.ds(c0, ch), :] = h_s[...]
        return h

    h_fwd = lax.fori_loop(0, n_ch, fwd_body, h0_ref[0:1, :])

    def bwd_body(i, h):
        c0 = pl.multiple_of((n_ch - 1 - i) * ch, ch)
        gates(c0, 1)
        h = scan_rows(h, True)
        u_gate = z_ref[pl.ds(c0, ch), rg_hi:rg_hi + D_RG].astype(F32)
        y = (hf_s[pl.ds(c0, ch), :] + h_s[...]) * _gelu_tanh(u_gate)
        y_ref[pl.ds(c0, ch), D_POOL:D_POOL + D_RG] = y.astype(y_ref.dtype)
        return h

    h_bwd = lax.fori_loop(0, n_ch, bwd_body, h0_ref[1:2, :])
    hfin_ref[0:1, :] = h_fwd
    hfin_ref[1:2, :] = h_bwd


def _even_mixer(z, h0, p, n_batch, t_len, grid):
    seg = GRID_W if grid else t_len
    const2 = lambda b: (0, 0)
    const3 = lambda b: (0, 0, 0)
    return pl.pallas_call(
        functools.partial(_even_kernel, t_len=t_len, grid=grid),
        grid=(n_batch,),
        in_specs=[
            pl.BlockSpec((t_len, EV_COLS), lambda b: (b, 0)),
            pl.BlockSpec((None, 2, D_RG), lambda b: (b, 0, 0)),
            pl.BlockSpec(p['pool_w'].shape, const3),
            pl.BlockSpec((1, D_POOL), const2),
            pl.BlockSpec((CONV_W, D_RG), const2),
            pl.BlockSpec((1, D_RG), const2),
            pl.BlockSpec((D_RG, 4 * D_RG), const2),
            pl.BlockSpec((1, 4 * D_RG), const2),
            pl.BlockSpec((2, D_RG), const2),
        ],
        out_specs=[
            pl.BlockSpec((t_len, D_POOL + D_RG), lambda b: (b, 0)),
            pl.BlockSpec((None, 2, D_RG), lambda b: (b, 0, 0)),
        ],
        out_shape=[
            jax.ShapeDtypeStruct((n_batch * t_len, D_POOL + D_RG), Y_DTYPE),
            jax.ShapeDtypeStruct((n_batch, 2, D_RG), F32),
        ],
        scratch_shapes=[
            pltpu.VMEM((len(POOL_WINDOWS), seg + 16, POOL_GW), F32),
            pltpu.VMEM((RG_CHUNK + 2 * HALO, D_RG), F32),
            pltpu.VMEM((RG_CHUNK, D_RG), F32),
            pltpu.VMEM((RG_CHUNK, D_RG), F32),
            pltpu.VMEM((RG_CHUNK, D_RG), F32),
            pltpu.VMEM((t_len, D_RG), F32),
        ],
        compiler_params=_ARB1,
    )(z, h0, p['pool_w'], p['pool_scale'], p['conv_w'], p['conv_b'], p['wg'], p['bg'], p['lam'])


def _rwkv_kernel(z_ref, s0_ref, mu_ref, kkw_ref, ka_ref, rk_ref, g2_ref, w0_ref, w2_ref, a0_ref, a2_ref,
                 lnw_ref, lnb_ref, tri_ref, bd_ref,
                 y_ref, sfin_ref, ebuf, s_s, yd_s, bon_s, g_s, *, t_len):
    L = SCAN_CHUNK
    n_ch = t_len // L
    L2 = 2 * L
    zero = jnp.zeros((RW_HEAD, RW_HEAD), F32)
    for d in range(2):
        for p in range(RW_PAIRS):
            top = jnp.concatenate([s0_ref[d, 2 * p], zero], axis=1)
            bot = jnp.concatenate([zero, s0_ref[d, 2 * p + 1]], axis=1)
            s_s[d, p] = jnp.concatenate([top, bot], axis=0)


    def stack(x):
        first_head = lax.broadcasted_iota(jnp.int32, (L, LANES), 1) < RW_HEAD
        return jnp.concatenate([jnp.where(first_head, x, 0.0), jnp.where(first_head, 0.0, x)],
                               axis=0).astype(BF16)

    def prep(c0, d):
        ones_bd = bd_ref[...]
        ebuf[d] = _load_ext(z_ref, c0, L, t_len, 0, RW_ZCOLS)
        cur = ebuf[d, HALO:HALO + L, :]
        prev = ebuf[d, HALO - 1:HALO - 1 + L, :]
        nxt = ebuf[d, HALO + 1:HALO + 1 + L, :]
        zs = cur + mu_ref[0:1, :] * (prev - cur) + mu_ref[1:2, :] * (nxt - cur)
        r = zs[:, 0:D_RW]
        k = zs[:, D_RW:2 * D_RW]
        v = zs[:, 2 * D_RW:3 * D_RW]
        o = 3 * D_RW
        gd = zs[:, o:o + RW_G_LORA]
        wd = zs[:, o + RW_G_LORA:o + RW_G_LORA + LANES]
        ad = zs[:, o + RW_G_LORA + LANES:o + RW_G_LORA + 2 * LANES]

        kkv = k * kkw_ref[...]
        nrm = jnp.sqrt(_head_sum(kkv * kkv, ones_bd))
        kkn = kkv / jnp.maximum(nrm, 1e-12)

        w_log = -_softplus(-(w0_ref[d:d + 1, :] + _dot_f32(jnp.tanh(wd), w2_ref[d]))) - 0.5
        lw = -jnp.exp(w_log)
        a = _sigmoid(a0_ref[d:d + 1, :] + _dot_f32(ad, a2_ref[d]))
        kd = k * (1.0 + (a - 1.0) * ka_ref[...])
        bon_s[d, pl.ds(c0, L), :] = _head_sum(r * kd * rk_ref[...], ones_bd) * v
        if d == 0:
            g_s[pl.ds(c0, L), :] = _dot(_sigmoid(gd).astype(BF16), g2_ref[...])

        cum = _dot_exact_lhs(tri_ref[d], lw)
        e_pos = jnp.exp(cum)
        e_neg = jnp.exp(-cum)
        rh = r * e_pos
        kh = kd * e_neg
        bh = (kkn * a) * e_neg
        ah = -kkn * jnp.exp(cum - lw)
        last = L - 1 if d == 0 else 0
        gam = e_pos[last:last + 1, :]
        bhg = bh * gam
        khg = kh * gam

        chains = []
        for p in range(RW_PAIRS):
            cols = slice(p * LANES, (p + 1) * LANES)
            chains.append(dict(
                d=d, p=p, gam=gam[:, cols], v_st=stack(v[:, cols]),
                lhs=jnp.concatenate([stack(ah[:, cols]), stack(rh[:, cols])], axis=0),
                rhs=jnp.concatenate([stack(bh[:, cols]), stack(kh[:, cols])], axis=0),
                bk=jnp.concatenate([stack(bhg[:, cols]), stack(khg[:, cols])], axis=0)))
        return chains

    n_levels = int(math.log2(L))

    def body(i, carry):
        c0s = (pl.multiple_of(i * L, L), pl.multiple_of((n_ch - 1 - i) * L, L))
        chains = prep(c0s[0], 0) + prep(c0s[1], 1)
        ri = lax.broadcasted_iota(jnp.int32, (L2, L2), 0)
        ci = lax.broadcasted_iota(jnp.int32, (L2, L2), 1)
        eye = jnp.where(ri == ci, 1.0, 0.0)
        rl = jnp.where(ri < L, ri, ri - L)
        cl = jnp.where(ci < L, ci, ci - L)
        strict = (cl < rl, cl > rl)
        incl = (cl <= rl, cl >= rl)
        level = ri ^ ci

        for ch in chains:
            ch['sc'] = _dot_nt(ch['lhs'], ch['rhs'])
        for ch in chains:
            sc, d = ch['sc'], ch['d']
            n_ab = jnp.where(strict[d], sc[0:L2, 0:L2], 0.0)
            m_ak = jnp.where(strict[d], sc[0:L2, L2:2 * L2], 0.0).astype(BF16)
            t_rb = jnp.where(incl[d], sc[L2:2 * L2, 0:L2], 0.0)
            t_rk = jnp.where(incl[d], sc[L2:2 * L2, L2:2 * L2], 0.0)
            ch['t_rbk'] = jnp.concatenate([t_rb, t_rk], axis=1).astype(BF16)
            ch['mv'] = _dot(m_ak, ch['v_st'])
            ch['t'] = eye + jnp.where(level == 1, n_ab, 0.0)
            ch['n_lv'] = [jnp.where(lax.shift_right_logical(level, lv) == 1, n_ab, 0.0).astype(BF16)
                          for lv in range(1, n_levels)]
            del ch['sc']
        for lv in range(n_levels - 1):
            for ch in chains:
                ch['tb'] = ch['t'].astype(BF16)
                ch['x'] = _dot(ch['n_lv'][lv], ch['tb']).astype(BF16)
            for ch in chains:
                ch['t'] = ch['t'] + _dot(ch['tb'], ch['x'])

        for ch in chains:
            ch['s_prev'] = s_s[ch['d'], ch['p']]
            ch['ar_s'] = _dot_nt(ch['lhs'], ch['s_prev'].astype(BF16))
        for ch in chains:
            u_rhs = ch['ar_s'][0:L2] + ch['mv']
            u_st = _dot(ch['t'].astype(BF16), u_rhs.astype(BF16)).astype(BF16)
            ch['uv'] = jnp.concatenate([u_st, ch['v_st']], axis=0)
        for ch in chains:
            y_st = ch['ar_s'][L2:2 * L2] + _dot(ch['t_rbk'], ch['uv'])
            ch['y'] = y_st[0:L] + y_st[L:L2]
            s_s[ch['d'], ch['p']] = ch['s_prev'] * ch['gam'] + _dot_tn(ch['uv'], ch['bk'])
        for d in range(2):
            yd_s[d, pl.ds(c0s[d], L), :] = jnp.concatenate([ch['y'] for ch in chains if ch['d'] == d], axis=1)
        return carry

    lax.fori_loop(0, n_ch, body, 0, unroll=2)

    def epilogue(i, carry):
        c0 = pl.multiple_of(i * L, L)
        ones_bd = bd_ref[...]
        ys = yd_s[0, pl.ds(c0, L), :] + yd_s[1, pl.ds(c0, L), :]
        mu = _head_sum(ys, ones_bd) * (1.0 / RW_HEAD)
        yc = ys - mu
        var = _head_sum(yc * yc, ones_bd) * (1.0 / RW_HEAD)
        yn = yc * lax.rsqrt(var + RW_LN_EPS) * lnw_ref[...] + lnb_ref[...]
        out = (yn + (bon_s[0, pl.ds(c0, L), :] + bon_s[1, pl.ds(c0, L), :])) * g_s[pl.ds(c0, L), :]
        y_ref[pl.ds(c0, L), :] = out.astype(y_ref.dtype)
        return carry

    lax.fori_loop(0, n_ch, epilogue, 0, unroll=2)
    for d in range(2):
        for p in range(RW_PAIRS):
            s_pair = s_s[d, p]
            sfin_ref[d, 2 * p] = s_pair[0:RW_HEAD, 0:RW_HEAD]
            sfin_ref[d, 2 * p + 1] = s_pair[RW_HEAD:, RW_HEAD:]


def _rwkv_mixer(z, s0, p, n_batch, t_len):
    L = SCAN_CHUNK
    c2 = lambda b: (0, 0)
    c3 = lambda b: (0, 0, 0)
    st_spec = pl.BlockSpec((None, 2, RW_HEADS, RW_HEAD, RW_HEAD), lambda b: (b, 0, 0, 0, 0))
    return pl.pallas_call(
        functools.partial(_rwkv_kernel, t_len=t_len),
        grid=(n_batch,),
        in_specs=[
            pl.BlockSpec((t_len, RW_ZCOLS), lambda b: (b, 0)),
            st_spec,
            pl.BlockSpec((2, RW_ZCOLS), c2),
            pl.BlockSpec((1, D_RW), c2),
            pl.BlockSpec((1, D_RW), c2),
            pl.BlockSpec((1, D_RW), c2),
            pl.BlockSpec((RW_G_LORA, D_RW), c2),
            pl.BlockSpec((2, D_RW), c2),
            pl.BlockSpec((2, LANES, D_RW), c3),
            pl.BlockSpec((2, D_RW), c2),
            pl.BlockSpec((2, LANES, D_RW), c3),
            pl.BlockSpec((1, D_RW), c2),
            pl.BlockSpec((1, D_RW), c2),
            pl.BlockSpec((2, L, L), c3),
            pl.BlockSpec((LANES, LANES), c2),
        ],
        out_specs=[pl.BlockSpec((t_len, D_RW), lambda b: (b, 0)), st_spec],
        out_shape=[
            jax.ShapeDtypeStruct((n_batch * t_len, D_RW), Y_DTYPE),
            jax.ShapeDtypeStruct((n_batch, 2, RW_HEADS, RW_HEAD, RW_HEAD), F32),
        ],
        scratch_shapes=[
            pltpu.VMEM((2, L + 2 * HALO, RW_ZCOLS), F32),
            pltpu.VMEM((2, RW_PAIRS, LANES, LANES), F32),
            pltpu.VMEM((2, t_len, D_RW), F32),
            pltpu.VMEM((2, t_len, D_RW), F32),
            pltpu.VMEM((t_len, D_RW), F32),
        ],
        compiler_params=_ARB1,
    )(z, s0, p['mu'], p['kk'], p['ka'], p['rk'], p['g2'], p['w0'], p['w2'], p['a0'], p['a2'],
      p['ln_w'], p['ln_b'], p['tri'], p['ones_bd'])


def _mlstm2_kernel(z_ref, c0_ref, n0_ref, m0_ref, cw_ref, cb_ref, gb_ref, nw_ref, tri_ref, esel_ref, shift_ref,
                   ones_ref, y_ref, cfin_ref, nfin_ref, mfin_ref, c_s, n_s, m_s, hd_s, qk_s, gx_s, gr_s,
                   *, t_len):
    L = ML_SCAN
    n_ch = t_len // L
    c_s[...] = c0_ref[...]
    n_s[...] = n0_ref[...]
    m_s[...] = m0_ref[...]
    q_scale = ML_DK ** -0.5
    g_lo = 4 * D_ML
    n_real = 4 * ML_HEADS

    def conv_chunk(i, carry):
        c0 = pl.multiple_of(i * L, L)
        ext = _load_ext(z_ref, c0, L, t_len, 0, 2 * D_ML, raw=True)
        taps = _dot(shift_ref[...], ext)
        qk = cb_ref[...] + ext[HALO:HALO + L].astype(F32) * cw_ref[CONV_LEFT:CONV_LEFT + 1, :]
        for n, j in enumerate(jj for jj in range(CONV_W) if jj != CONV_LEFT):
            qk = qk + taps[n * L:(n + 1) * L] * cw_ref[j:j + 1, :]
        qk = _silu(qk)
        qk_s[pl.ds(c0, L), 0:D_ML] = (qk[:, :D_ML] * q_scale).astype(BF16)
        qk_s[pl.ds(c0, L), D_ML:2 * D_ML] = qk[:, D_ML:].astype(BF16)

        gts = z_ref[pl.ds(c0, L), g_lo:g_lo + LANES].astype(F32) + gb_ref[...]
        x16 = gts.T[0:n_real, :]
        lf_hi, lf_lo = _split2(-_softplus(-x16))
        lane = lax.broadcasted_iota(jnp.int32, (8, L), 1)
        for d in range(2):
            b16 = _dot_nt(lf_hi, tri_ref[d]) + _dot_nt(lf_lo, tri_ref[d])
            x8 = x16[8 * d:8 * d + 8, :]
            b8 = b16[8 * d:8 * d + 8, :]
            g8 = pltpu.roll(x8, ML_HEADS, axis=0) - b8
            cm8 = g8
            step = 1
            while step < L:
                if d == 0:
                    sh = jnp.where(lane >= step, pltpu.roll(cm8, step, axis=1), -jnp.inf)
                else:
                    sh = jnp.where(lane < L - step, pltpu.roll(cm8, L - step, axis=1), -jnp.inf)
                cm8 = jnp.maximum(cm8, sh)
                step *= 2
            x32 = jnp.concatenate([g8, b8, cm8, jnp.zeros((8, L), F32)], axis=0)
            hi = x32.astype(BF16)
            r1 = x32 - hi.astype(F32)
            mid = r1.astype(BF16)
            lo = (r1 - mid.astype(F32)).astype(BF16)
            gx_s[d, i] = jnp.concatenate([hi, mid, lo], axis=0)
            gr_s[d, i] = g8
        return carry

    lax.fori_loop(0, n_ch, conv_chunk, 0, unroll=2)

    def prep(ci, d):
        c0 = pl.multiple_of(ci * L, L)
        q = qk_s[pl.ds(c0, L), 0:D_ML]
        k = qk_s[pl.ds(c0, L), D_ML:2 * D_ML]
        v = z_ref[pl.ds(c0, L), 2 * D_ML:3 * D_ML]
        g8 = gr_s[d, ci]
        cb = _dot_tn(gx_s[d, ci], esel_ref[...])
        ri = lax.broadcasted_iota(jnp.int32, (L, L), 0)
        ci = lax.broadcasted_iota(jnp.int32, (L, L), 1)
        causal = (ci <= ri) if d == 0 else (ci >= ri)
        last = L - 1 if d == 0 else 0
        ones = ones_ref[...]
        chains = []
        for h in range(ML_HEADS):
            cols = slice(h * ML_DK, (h + 1) * ML_DK)
            chains.append(dict(
                d=d, h=h, last=last, causal=causal,
                g_row=g8[ML_HEADS + h:ML_HEADS + h + 1, :],
                g_b=cb[:, cols], b_b=cb[:, D_ML + h * ML_DK:D_ML + (h + 1) * ML_DK],
                cm_b=cb[:, 2 * D_ML + h * ML_DK:2 * D_ML + (h + 1) * ML_DK],
                qb=q[:, cols], kb=k[:, cols],
                v=v[:, cols].astype(F32), v_ones=jnp.concatenate([v[:, cols].astype(BF16), ones], axis=1)))
        return chains

    def body(i, carry):
        c0s = (pl.multiple_of(i * L, L), pl.multiple_of((n_ch - 1 - i) * L, L))
        chains = prep(i, 0) + prep(n_ch - 1 - i, 1)
        for ch in chains:
            d, h = ch['d'], ch['h']
            ch['qk'] = _dot_nt(ch['qb'], ch['kb'])
            ch['c_prev'] = c_s[d, h]
            ch['n_prev'] = n_s[d, h]
            ch['m_prev'] = m_s[d, h:h + 1, :]
            ch['qcn'] = _dot(ch['qb'], jnp.concatenate([ch['c_prev'].astype(BF16),
                                                        ch['n_prev'].astype(BF16)], axis=1))
        for ch in chains:
            mu = jnp.maximum(ch['m_prev'], ch['cm_b'])
            ch['mu'] = mu
            ch['s'] = ch['qk'] * jnp.exp(jnp.where(ch['causal'], ch['g_row'] - mu, -jnp.inf))
            ch['w_inter'] = jnp.exp(ch['m_prev'] - mu)
        for ch in chains:
            last = ch['last']
            ch['sv'] = _dot(ch['s'].astype(BF16), ch['v_ones'])
            mu_last = ch['mu'][last:last + 1, :]
            ch['m_new'] = ch['b_b'][last:last + 1, :] + mu_last
            ch['dec'] = jnp.exp(ch['m_prev'] - mu_last)
            w_s = jnp.exp(ch['g_b'] - mu_last)
            ch['upd'] = _dot_tn(ch['kb'], jnp.concatenate([(w_s * ch['v']).astype(BF16),
                                                           w_s.astype(BF16)], axis=1))
        for ch in chains:
            d, h = ch['d'], ch['h']
            num = ch['sv'][:, :ML_DK] + ch['w_inter'] * ch['qcn'][:, :ML_DK]
            den = ch['sv'][:, ML_DK:] + ch['w_inter'] * ch['qcn'][:, ML_DK:]
            ch['out'] = num / jnp.maximum(jnp.abs(den), jnp.exp(-(ch['b_b'] + ch['mu'])))
            c_s[d, h] = ch['dec'] * ch['c_prev'] + ch['upd'][:, :ML_DK]
            n_s[d, h] = ch['dec'] * ch['n_prev'] + ch['upd'][:, ML_DK:]
            m_s[d, h:h + 1, :] = ch['m_new']
        for d in range(2):
            hd_s[d, pl.ds(c0s[d], L), :] = jnp.concatenate([ch['out'] for ch in chains if ch['d'] == d], axis=1)
        return carry

    lax.fori_loop(0, n_ch, body, 0, unroll=2)

    def epilogue(i, carry):
        c0 = pl.multiple_of(i * L, L)
        ones = ones_ref[...]
        hs = hd_s[0, pl.ds(c0, L), :] + hd_s[1, pl.ds(c0, L), :]
        o = z_ref[pl.ds(c0, L), 3 * D_ML:4 * D_ML].astype(F32)
        mu = _head_sum(hs, ones) * (1.0 / ML_DK)
        xc = hs - mu
        var = _head_sum(xc * xc, ones) * (1.0 / ML_DK)
        y = xc * lax.rsqrt(var + NORM_EPS) * nw_ref[...] * _sigmoid(o)
        y_ref[pl.ds(c0, L), :] = y.astype(y_ref.dtype)
        return carry

    lax.fori_loop(0, n_ch, epilogue, 0, unroll=2)
    cfin_ref[...] = c_s[...]
    nfin_ref[...] = n_s[...]
    mfin_ref[...] = m_s[...]


def _mlstm2_mixer(z, c0, n0, m0, p, n_batch, t_len):
    assert z.dtype == BF16
    L = ML_SCAN
    c2 = lambda b: (0, 0)
    c3 = lambda b: (0, 0, 0)
    c_spec = pl.BlockSpec((None, 2, ML_HEADS, ML_DK, ML_DK), lambda b: (b, 0, 0, 0, 0))
    m_spec = pl.BlockSpec((None, 2, ML_HEADS, ML_DK), lambda b: (b, 0, 0, 0))
    return pl.pallas_call(
        functools.partial(_mlstm2_kernel, t_len=t_len),
        grid=(n_batch,),
        in_specs=[
            pl.BlockSpec((t_len, ML_ZCOLS), lambda b: (b, 0)),
            c_spec, c_spec, m_spec,
            pl.BlockSpec((CONV_W, 2 * D_ML), c2),
            pl.BlockSpec((1, 2 * D_ML), c2),
            pl.BlockSpec((1, LANES), c2),
            pl.BlockSpec((1, D_ML), c2),
            pl.BlockSpec((2, L, L), c3),
            pl.BlockSpec(p['esel'].shape, c2),
            pl.BlockSpec(p['shift'].shape, c2),
            pl.BlockSpec((L, LANES), c2),
        ],
        out_specs=[pl.BlockSpec((t_len, D_ML), lambda b: (b, 0)), c_spec, c_spec, m_spec],
        out_shape=[
            jax.ShapeDtypeStruct((n_batch * t_len, D_ML), Y_DTYPE),
            jax.ShapeDtypeStruct((n_batch, 2, ML_HEADS, ML_DK, ML_DK), F32),
            jax.ShapeDtypeStruct((n_batch, 2, ML_HEADS, ML_DK, ML_DK), F32),
            jax.ShapeDtypeStruct((n_batch, 2, ML_HEADS, ML_DK), F32),
        ],
        scratch_shapes=[
            pltpu.VMEM((2, ML_HEADS, ML_DK, ML_DK), F32),
            pltpu.VMEM((2, ML_HEADS, ML_DK, ML_DK), F32),
            pltpu.VMEM((2, ML_HEADS, ML_DK), F32),
            pltpu.VMEM((2, t_len, D_ML), F32),
            pltpu.VMEM((t_len, 2 * D_ML), BF16),
            pltpu.VMEM((2, t_len // L, 96, L), BF16),
            pltpu.VMEM((2, t_len // L, 8, L), F32),
        ],
        compiler_params=_ARB1,
    )(z, c0, n0, m0, p['conv_w'], p['conv_b'], p['gate_b'], p['norm_w'], p['tri'], p['esel'], p['shift'],
      p['ones'])


def _conv_shift(n):
    taps = jnp.array([j for j in range(CONV_W) if j != CONV_LEFT])
    t = jnp.arange((CONV_W - 1) * n)
    src = HALO + t % n + taps[t // n] - CONV_LEFT
    return (src[:, None] == jnp.arange(n + 2 * HALO)[None, :]).astype(BF16)


def _mlstm_consts():
    L = ML_SCAN
    lower = jnp.tril(jnp.ones((L, L), F32))
    tri = jnp.stack([lower, lower.T]).astype(BF16)
    rows = jnp.arange(32)
    blk = jnp.arange(3 * D_ML) // ML_DK
    src_row = 8 * (blk // ML_HEADS) + ML_HEADS + blk % ML_HEADS
    esel = (rows[:, None] == src_row[None, :]).astype(BF16)
    esel = jnp.concatenate([esel, esel, esel], axis=0)
    return dict(tri=tri, esel=esel, shift=_conv_shift(L), ones=jnp.ones((L, LANES), BF16))


def _block_diag(w):
    n, c, d = w.shape
    return jnp.einsum('ncd,nm->ncmd', w, jnp.eye(n, dtype=w.dtype)).reshape(n * c, n * d)


def _scan_tri():
    L = SCAN_CHUNK
    lower = jnp.tril(jnp.ones((L, L), F32))
    return jnp.stack([lower, lower.T]).astype(BF16)


def _prep_even(ev_w_in, pool_w, pool_scale, rg_conv_w, rg_conv_b, rg_wa, rg_ba, rg_wx, rg_bx, rg_lam, ev_w_out, j):
    wg = jnp.concatenate([_block_diag(rg_wa[j, 0]), _block_diag(rg_wx[j, 0]),
                          _block_diag(rg_wa[j, 1]), _block_diag(rg_wx[j, 1])], axis=1)
    bg = jnp.concatenate([rg_ba[j, 0], rg_bx[j, 0], rg_ba[j, 1], rg_bx[j, 1]]).reshape(1, 4 * D_RG)
    return dict(
        w_in=[ev_w_in[j].astype(BF16)],
        w_out=ev_w_out[j].astype(BF16),
        pool_w=pool_w[j].astype(BF16),
        pool_scale=pool_scale[j].reshape(1, D_POOL),
        conv_w=rg_conv_w[j],
        conv_b=rg_conv_b[j].reshape(1, D_RG),
        wg=wg.astype(BF16),
        bg=bg,
        lam=rg_lam[j],
    )


def _pad_cols(w, cuts, widths):
    pieces = jnp.split(w, cuts, axis=-1)
    out = []
    for piece, width in zip(pieces, widths):
        extra = width - piece.shape[-1]
        if extra:
            piece = jnp.pad(piece, [(0, 0)] * (piece.ndim - 1) + [(0, extra)])
        out.append(piece)
    return jnp.concatenate(out, axis=-1)


def _prep_odd(od_w_in, rw_mu, rw_w0, rw_w2, rw_a0, rw_a2, rw_kk, rw_ka, rw_rk, rw_g2, rw_ln_w, rw_ln_b,
              ml_conv_w, ml_conv_b, ml_bi, ml_bf, ml_norm_w, od_w_out, j):
    rw_cols = 3 * D_RW + RW_G_LORA + RW_W_LORA + RW_A_LORA
    base = 3 * D_RW + RW_G_LORA
    cuts = [base, base + RW_W_LORA]
    widths = [base, LANES, LANES]
    w_rw = _pad_cols(od_w_in[j][:, :rw_cols], cuts, widths)
    w_ml = _pad_cols(od_w_in[j][:, rw_cols:], [4 * D_ML], [4 * D_ML, LANES])
    pad_rows = lambda w: jnp.pad(w, ((0, 0), (0, LANES - w.shape[1]), (0, 0)))
    gate_b = jnp.concatenate([ml_bi[j, 0], ml_bf[j, 0], ml_bi[j, 1], ml_bf[j, 1]])
    gate_b = jnp.pad(gate_b, (0, LANES - gate_b.shape[0])).reshape(1, LANES)
    hm = jnp.arange(LANES) // RW_HEAD
    rw = dict(
        mu=_pad_cols(rw_mu[j], cuts, widths),
        kk=rw_kk[j].reshape(1, D_RW), ka=rw_ka[j].reshape(1, D_RW), rk=rw_rk[j].reshape(1, D_RW),
        g2=rw_g2[j].astype(BF16),
        w0=rw_w0[j], w2=pad_rows(rw_w2[j]), a0=rw_a0[j], a2=pad_rows(rw_a2[j]),
        ln_w=rw_ln_w[j].reshape(1, D_RW), ln_b=rw_ln_b[j].reshape(1, D_RW),
        tri=_scan_tri(),
        ones_bd=(hm[:, None] == hm[None, :]).astype(BF16),
    )
    ml = dict(
        conv_w=ml_conv_w[j], conv_b=ml_conv_b[j].reshape(1, 2 * D_ML),
        gate_b=gate_b, norm_w=ml_norm_w[j].reshape(1, D_ML), **_mlstm_consts(),
    )
    return dict(w_in=[w_rw.astype(BF16), w_ml.astype(BF16)], w_out=od_w_out[j].astype(BF16), rw=rw, ml=ml)


def _trunk(x, mod, rows_per_mod, mod_base, grid, states, params, n_batch, t_len):
    ev, od, dense = params
    rg0, rw0, c0, n0, m0 = states
    x = x.reshape(n_batch * t_len, D_MODEL)
    spec = (rows_per_mod, mod_base)

    (z,) = _in_proj(x, mod[0], *spec, dense['norm1_w'][0], ev['w_in'])
    y_ev, rg_f = _even_mixer(z, rg0, ev, n_batch, t_len, grid)
    x = _out_mlp([y_ev], x, mod[0], *spec, dense['norm2_w'][0], ev['w_out'],
                 dense['mlp_w1'], dense['mlp_w2'], 0, dense['final_norm_w'], False)

    z_rw, z_ml = _in_proj(x, mod[1], *spec, dense['norm1_w'][1], od['w_in'])
    y_rw, s_f = _rwkv_mixer(z_rw, rw0, od['rw'], n_batch, t_len)
    y_ml, c_f, n_f, m_f = _mlstm2_mixer(z_ml, c0, n0, m0, od['ml'], n_batch, t_len)
    y = _out_mlp([y_rw, y_ml], x, mod[1], *spec, dense['norm2_w'][1], od['w_out'],
                 dense['mlp_w1'], dense['mlp_w2'], 1, dense['final_norm_w'], True)
    return y.reshape(n_batch, t_len, D_MODEL), (rg_f, s_f, c_f, n_f, m_f)


def kernel(x_prompt, x_sample, state_rglru, state_rwkv, state_mlstm_C, state_mlstm_n, state_mlstm_m,
           c, c_ctx, norm1_w, norm2_w, w_mod, b_mod, mlp_w1, mlp_w2, final_norm_w,
           ev_w_in, pool_w, pool_scale, rg_conv_w, rg_conv_b, rg_wa, rg_ba, rg_wx, rg_bx, rg_lam, ev_w_out,
           od_w_in, rw_mu, rw_w0, rw_w2, rw_a0, rw_a2, rw_kk, rw_ka, rw_rk, rw_g2, rw_ln_w, rw_ln_b,
           ml_conv_w, ml_conv_b, ml_bi, ml_bf, ml_norm_w, od_w_out):
    bp, tp, _ = x_prompt.shape
    bs, ts, _ = x_sample.shape

    n_cond = 16
    cond = jnp.concatenate([c_ctx[None, :], c, jnp.zeros((n_cond - 1 - bs, D_MODEL), F32)], axis=0)
    mod = _modulation(cond, w_mod, b_mod)

    ev = _prep_even(ev_w_in, pool_w, pool_scale, rg_conv_w, rg_conv_b, rg_wa, rg_ba, rg_wx, rg_bx, rg_lam,
                    ev_w_out, 0)
    od = _prep_odd(od_w_in, rw_mu, rw_w0, rw_w2, rw_a0, rw_a2, rw_kk, rw_ka, rw_rk, rw_g2, rw_ln_w, rw_ln_b,
                   ml_conv_w, ml_conv_b, ml_bi, ml_bf, ml_norm_w, od_w_out, 0)
    dense = dict(norm1_w=norm1_w, norm2_w=norm2_w, mlp_w1=mlp_w1.astype(BF16), mlp_w2=mlp_w2.astype(BF16),
                 final_norm_w=final_norm_w)
    params = (ev, od, dense)

    zero_states = (
        jnp.zeros((bp, 2, D_RG), F32),
        jnp.zeros((bp, 2, RW_HEADS, RW_HEAD, RW_HEAD), F32),
        jnp.zeros((bp, 2, ML_HEADS, ML_DK, ML_DK), F32),
        jnp.zeros((bp, 2, ML_HEADS, ML_DK, ML_DK), F32),
        jnp.zeros((bp, 2, ML_HEADS, ML_DK), F32),
    )
    y_prompt, (rg_f, s_f, c_f, n_f, m_f) = _trunk(
        x_prompt, mod, bp * tp, 0, False, zero_states, params, bp, tp)

    sample_states = (
        state_rglru[:, 0],
        state_rwkv[:, 0],
        state_mlstm_C[:, 0],
        jnp.broadcast_to(state_mlstm_n[:, 0][..., None], (bs, 2, ML_HEADS, ML_DK, ML_DK)),
        jnp.broadcast_to(state_mlstm_m[:, 0][..., None], (bs, 2, ML_HEADS, ML_DK)),
    )
    y_sample, _ = _trunk(x_sample, mod, ts, 1, True, sample_states, params, bs, ts)

    return (y_prompt, y_sample,
            rg_f[:, None],
            s_f[:, None],
            c_f[:, None],
            n_f[:, None, :, :, :, 0],
            m_f[:, None, :, :, 0])
```

```python
import functools
import math

import jax
import jax.numpy as jnp
from jax import lax
from jax.experimental import pallas as pl
from jax.experimental.pallas import tpu as pltpu

F32 = jnp.float32
BF16 = jnp.bfloat16

D_MODEL = 1024
DEPTH = 2
GRID_W = 64
D_FF = 4 * D_MODEL
NORM_EPS = 1e-6
N_MOD = 6

D_POOL = D_MODEL // 2
POOL_WINDOWS = (2, 4, 8, 16)
POOL_GW = D_POOL // len(POOL_WINDOWS)
D_RG = D_MODEL // 2
RG_BLOCKS = 8
RG_C = 8.0
CONV_W = 4
CONV_LEFT = 2
EV_COLS = D_POOL + 2 * D_RG

D_RW = D_MODEL // 2
RW_HEAD = 64
RW_HEADS = D_RW // RW_HEAD
RW_PAIRS = RW_HEADS // 2
RW_W_LORA = 64
RW_A_LORA = 64
RW_G_LORA = 128
RW_LN_EPS = 64e-5
RW_ZCOLS = 3 * D_RW + RW_G_LORA + 128 + 128
D_ML = D_MODEL // 2
ML_HEADS = 4
ML_DK = D_ML // ML_HEADS
ML_ZCOLS = 4 * D_ML + 128

LANES = 128
HALO = 16
Z_DTYPE = BF16
Y_DTYPE = BF16
ROW_BLOCK = 512
SCAN_CHUNK = 64
ML_SCAN = 128
RG_CHUNK = 256
VMEM_LIMIT = 56 * 1024 * 1024

_ARB1 = pltpu.CompilerParams(dimension_semantics=("arbitrary",), vmem_limit_bytes=VMEM_LIMIT)
_ARB2 = pltpu.CompilerParams(dimension_semantics=("arbitrary", "arbitrary"), vmem_limit_bytes=VMEM_LIMIT)


def _dot(a, b):
    return jnp.dot(a, b, preferred_element_type=F32)


def _dot_nt(a, b):
    return lax.dot_general(a, b, (((1,), (1,)), ((), ())), preferred_element_type=F32)


def _dot_tn(a, b):
    return lax.dot_general(a, b, (((0,), (0,)), ((), ())), preferred_element_type=F32)


def _split2(x):
    hi = x.astype(BF16)
    lo = (x - hi.astype(F32)).astype(BF16)
    return hi, lo


def _dot_f32(a, b):
    ah, al = _split2(a)
    bh, bl = _split2(b)
    return _dot(ah, bh) + (_dot(ah, bl) + _dot(al, bh))


def _dot_exact_lhs(m_bf16, x):
    hi, lo = _split2(x)
    return _dot(m_bf16, hi) + _dot(m_bf16, lo)


def _dot_exact_rhs(x, m_bf16):
    hi, lo = _split2(x)
    return _dot(hi, m_bf16) + _dot(lo, m_bf16)


def _softplus(x):
    return jnp.maximum(x, 0.0) + jnp.log1p(jnp.exp(-jnp.abs(x)))


def _sigmoid(x):
    return jax.nn.sigmoid(x)


def _silu(x):
    return x * jax.nn.sigmoid(x)


def _gelu_tanh(x):
    c = math.sqrt(2.0 / math.pi)
    return x * (0.5 * (1.0 + jnp.tanh(c * (x + 0.044715 * (x * x * x)))))


def _rms_norm(x, w):
    ms = jnp.mean(x * x, axis=-1, keepdims=True)
    return x * lax.rsqrt(ms + NORM_EPS) * w


def _load_ext(z_ref, c0, n, t_len, lo, hi, raw=False):
    dt = z_ref.dtype if raw else F32
    cur = z_ref[pl.ds(c0, n), lo:hi].astype(dt)
    pb = pl.multiple_of(jnp.maximum(c0 - HALO, 0), HALO)
    pa = pl.multiple_of(jnp.minimum(c0 + n, t_len - HALO), HALO)
    before = z_ref[pl.ds(pb, HALO), lo:hi].astype(dt)
    after = z_ref[pl.ds(pa, HALO), lo:hi].astype(dt)
    before = jnp.where(c0 > 0, before, jnp.zeros_like(before))
    after = jnp.where(c0 + n < t_len, after, jnp.zeros_like(after))
    return jnp.concatenate([before, cur, after], axis=0)


def _head_sum(x, ones_bd, single_pass=False):
    n = x.shape[0]
    groups = x.shape[1] // LANES
    xs = jnp.concatenate([x[:, g * LANES:(g + 1) * LANES] for g in range(groups)], axis=0)
    s = _dot(xs.astype(BF16), ones_bd) if single_pass else _dot_exact_rhs(xs, ones_bd)
    return jnp.concatenate([s[g * n:(g + 1) * n] for g in range(groups)], axis=1)


def _mod_kernel(c_ref, w_ref, b_ref, o_ref):
    s = _silu(c_ref[...])
    o_ref[...] = _dot_f32(s, w_ref[...]) + b_ref[...]


def _modulation(cond, w_mod, b_mod):
    rows = cond.shape[0]
    tn = 1536
    out = pl.pallas_call(
        _mod_kernel,
        grid=(DEPTH, N_MOD * D_MODEL // tn),
        in_specs=[
            pl.BlockSpec((rows, D_MODEL), lambda l, j: (0, 0)),
            pl.BlockSpec((None, D_MODEL, tn), lambda l, j: (l, 0, j)),
            pl.BlockSpec((None, 1, tn), lambda l, j: (l, 0, j)),
        ],
        out_specs=pl.BlockSpec((None, rows, tn), lambda l, j: (l, 0, j)),
        out_shape=jax.ShapeDtypeStruct((DEPTH, rows, N_MOD * D_MODEL), F32),
        compiler_params=_ARB2,
    )(cond, w_mod, b_mod.reshape(DEPTH, 1, N_MOD * D_MODEL))
    return out.reshape(DEPTH, rows, N_MOD, D_MODEL)


def _inproj_kernel(*refs, n_out, col_chunk):
    x_ref, mod_ref, nw_ref = refs[:3]
    w_refs = refs[3:3 + n_out]
    z_refs = refs[3 + n_out:]
    h = _rms_norm(x_ref[...], nw_ref[...])
    h = h * (1.0 + mod_ref[1:2, :]) + mod_ref[0:1, :]
    hb = h.astype(BF16)
    for w_ref, z_ref in zip(w_refs, z_refs):
        n = w_ref.shape[1]
        for c in range(0, n, col_chunk):
            e = min(c + col_chunk, n)
            z_ref[:, c:e] = _dot(hb, w_ref[:, c:e]).astype(z_ref.dtype)


def _in_proj(x, mod, rows_per_mod, mod_base, norm_w, weights):
    m = x.shape[0]
    tm = ROW_BLOCK
    n_out = len(weights)
    const = lambda i: (0, 0)
    in_specs = [
        pl.BlockSpec((tm, D_MODEL), lambda i: (i, 0)),
        pl.BlockSpec((None, N_MOD, D_MODEL), lambda i: (mod_base + (i * tm) // rows_per_mod, 0, 0)),
        pl.BlockSpec((1, D_MODEL), const),
    ] + [pl.BlockSpec(w.shape, const, pipeline_mode=pl.Buffered(1)) for w in weights]
    out_specs = [pl.BlockSpec((tm, w.shape[1]), lambda i: (i, 0)) for w in weights]
    out_shape = [jax.ShapeDtypeStruct((m, w.shape[1]), Z_DTYPE) for w in weights]
    return pl.pallas_call(
        functools.partial(_inproj_kernel, n_out=n_out, col_chunk=512),
        grid=(m // tm,),
        in_specs=in_specs,
        out_specs=out_specs,
        out_shape=out_shape,
        compiler_params=_ARB1,
    )(x, mod, norm_w.reshape(1, D_MODEL), *weights)


def _outmlp_kernel(*refs, n_y, final, ff_chunk):
    y_refs = refs[:n_y]
    x_ref, mod_ref, nw_ref, wo_ref, w1_ref, w2_ref, fw_ref, o_ref = refs[n_y:]
    y = None
    row = 0
    for y_ref in y_refs:
        k = y_ref.shape[1]
        part = _dot(y_ref[...], wo_ref[row:row + k, :])
        y = part if y is None else y + part
        row += k
    x = x_ref[...] + mod_ref[2:3, :] * y
    h = _rms_norm(x, nw_ref[...]) * (1.0 + mod_ref[4:5, :]) + mod_ref[3:4, :]
    hb = h.astype(BF16)
    acc = None
    for c in range(0, D_FF, ff_chunk):
        u = jnp.maximum(_dot(hb, w1_ref[:, c:c + ff_chunk]), 0.0)
        part = _dot((u * u).astype(BF16), w2_ref[c:c + ff_chunk, :])
        acc = part if acc is None else acc + part
    x = x + mod_ref[5:6, :] * acc
    if final:
        x = _rms_norm(x, fw_ref[...])
    o_ref[...] = x


def _out_mlp(ys, x, mod, rows_per_mod, mod_base, norm_w, w_out, w1, w2, layer, final_w, final):
    m = x.shape[0]
    tm = ROW_BLOCK
    const = lambda i: (0, 0)
    single = dict(pipeline_mode=pl.Buffered(1))
    pick = lambda i: (layer, 0, 0)
    in_specs = [pl.BlockSpec((tm, y.shape[1]), lambda i: (i, 0)) for y in ys] + [
        pl.BlockSpec((tm, D_MODEL), lambda i: (i, 0)),
        pl.BlockSpec((None, N_MOD, D_MODEL), lambda i: (mod_base + (i * tm) // rows_per_mod, 0, 0)),
        pl.BlockSpec((1, D_MODEL), const),
        pl.BlockSpec(w_out.shape, const, **single),
        pl.BlockSpec((None,) + w1.shape[1:], pick, **single),
        pl.BlockSpec((None,) + w2.shape[1:], pick, **single),
        pl.BlockSpec((1, D_MODEL), const),
    ]
    return pl.pallas_call(
        functools.partial(_outmlp_kernel, n_y=len(ys), final=final, ff_chunk=1024),
        grid=(m // tm,),
        in_specs=in_specs,
        out_specs=pl.BlockSpec((tm, D_MODEL), lambda i: (i, 0)),
        out_shape=jax.ShapeDtypeStruct((m, D_MODEL), F32),
        compiler_params=_ARB1,
    )(*ys, x, mod, norm_w.reshape(1, D_MODEL), w_out, w1, w2, final_w.reshape(1, D_MODEL))


def _even_kernel(z_ref, h0_ref, pw_ref, ps_ref, cw_ref, cb_ref, wg_ref, bg_ref, lam_ref,
                 y_ref, hfin_ref, rowbuf, cbuf, a_s, b_s, h_s, hf_s, *, t_len, grid):
    seg = GRID_W if grid else t_len
    n_seg = t_len // seg
    n_groups = len(POOL_WINDOWS)
    pad = 8

    rowbuf[...] = jnp.zeros(rowbuf.shape, F32)
    pos = lax.broadcasted_iota(jnp.int32, (seg, POOL_GW), 0)

    def pool_segment(r):
        base = r * seg if isinstance(r, int) else pl.multiple_of(r * seg, seg)
        for g, w in enumerate(POOL_WINDOWS):
            lo_off = w // 2
            hi_off = w - 1 - lo_off
            cols = slice(g * POOL_GW, (g + 1) * POOL_GW)
            xg = z_ref[pl.ds(base, seg), cols].astype(F32)
            if grid:
                s1 = None
                for o in range(-lo_off, hi_off + 1):
                    rr = r + o
                    valid = jnp.logical_and(rr >= 0, rr < n_seg)
                    src = pl.multiple_of(jnp.clip(rr, 0, n_seg - 1) * seg, seg)
                    term = jnp.where(valid, z_ref[pl.ds(src, seg), cols].astype(F32), 0.0)
                    s1 = term if s1 is None else s1 + term
                cnt_r = jnp.minimum(r + hi_off, n_seg - 1) - jnp.maximum(r - lo_off, 0) + 1
                m1 = s1 / cnt_r.astype(F32)
            else:
                m1 = xg
            rowbuf[g, pad:pad + seg, :] = m1
            s2 = None
            for o in range(-lo_off, hi_off + 1):
                term = rowbuf[g, pad + o:pad + o + seg, :]
                s2 = term if s2 is None else s2 + term
            cnt_c = jnp.minimum(pos + hi_off, seg - 1) - jnp.maximum(pos - lo_off, 0) + 1
            d = s2 / cnt_c.astype(F32) - xg
            yg = _dot(d.astype(BF16), pw_ref[g]) * ps_ref[:, cols]
            y_ref[pl.ds(base, seg), cols] = yg.astype(y_ref.dtype)

    if n_seg == 1:
        pool_segment(0)
    else:
        def seg_body(r, carry):
            pool_segment(r)
            return carry
        lax.fori_loop(0, n_seg, seg_body, 0)

    ch = RG_CHUNK
    n_ch = t_len // ch
    rg_lo, rg_hi = D_POOL, D_POOL + D_RG

    def gates(c0, d):
        cbuf[...] = _load_ext(z_ref, c0, ch, t_len, rg_lo, rg_hi)
        xc = cb_ref[...]
        for j in range(CONV_W):
            off = HALO - CONV_LEFT + j
            xc = xc + cbuf[off:off + ch, :] * cw_ref[j:j + 1, :]
        g = _dot(xc.astype(BF16), wg_ref[:, d * 2 * D_RG:(d + 1) * 2 * D_RG]) \
            + bg_ref[:, d * 2 * D_RG:(d + 1) * 2 * D_RG]
        r = _sigmoid(g[:, :D_RG])
        i = _sigmoid(g[:, D_RG:])
        log_a = (-RG_C) * r * _softplus(-lam_ref[d:d + 1, :])
        a = jnp.exp(log_a)
        a_s[...] = a
        b_s[...] = jnp.sqrt(jnp.tanh(-log_a) * (a * a + 1.0)) * (i * xc)

    def scan_rows(h, reverse):
        def row(t, h):
            tt = (ch - 1 - t) if reverse else t
            h = a_s[pl.ds(tt, 1), :] * h + b_s[pl.ds(tt, 1), :]
            h_s[pl.ds(tt, 1), :] = h
            return h
        return lax.fori_loop(0, ch, row, h, unroll=8)

    def fwd_body(i, h):
        c0 = pl.multiple_of(i * ch, ch)
        gates(c0, 0)
        h = scan_rows(h, False)
        hf_s[pl.ds(c0, ch), :] = h_s[...]
        return h

    h_fwd = lax.fori_loop(0, n_ch, fwd_body, h0_ref[0:1, :])

    def bwd_body(i, h):
        c0 = pl.multiple_of((n_ch - 1 - i) * ch, ch)
        gates(c0, 1)
        h = scan_rows(h, True)
        u_gate = z_ref[pl.ds(c0, ch), rg_hi:rg_hi + D_RG].astype(F32)
        y = (hf_s[pl.ds(c0, ch), :] + h_s[...]) * _gelu_tanh(u_gate)
        y_ref[pl.ds(c0, ch), D_POOL:D_POOL + D_RG] = y.astype(y_ref.dtype)
        return h

    h_bwd = lax.fori_loop(0, n_ch, bwd_body, h0_ref[1:2, :])
    hfin_ref[0:1, :] = h_fwd
    hfin_ref[1:2, :] = h_bwd


def _even_mixer(z, h0, p, n_batch, t_len, grid):
    seg = GRID_W if grid else t_len
    const2 = lambda b: (0, 0)
    const3 = lambda b: (0, 0, 0)
    return pl.pallas_call(
        functools.partial(_even_kernel, t_len=t_len, grid=grid),
        grid=(n_batch,),
        in_specs=[
            pl.BlockSpec((t_len, EV_COLS), lambda b: (b, 0)),
            pl.BlockSpec((None, 2, D_RG), lambda b: (b, 0, 0)),
            pl.BlockSpec(p['pool_w'].shape, const3),
            pl.BlockSpec((1, D_POOL), const2),
            pl.BlockSpec((CONV_W, D_RG), const2),
            pl.BlockSpec((1, D_RG), const2),
            pl.BlockSpec((D_RG, 4 * D_RG), const2),
            pl.BlockSpec((1, 4 * D_RG), const2),
            pl.BlockSpec((2, D_RG), const2),
        ],
        out_specs=[
            pl.BlockSpec((t_len, D_POOL + D_RG), lambda b: (b, 0)),
            pl.BlockSpec((None, 2, D_RG), lambda b: (b, 0, 0)),
        ],
        out_shape=[
            jax.ShapeDtypeStruct((n_batch * t_len, D_POOL + D_RG), Y_DTYPE),
            jax.ShapeDtypeStruct((n_batch, 2, D_RG), F32),
        ],
        scratch_shapes=[
            pltpu.VMEM((len(POOL_WINDOWS), seg + 16, POOL_GW), F32),
            pltpu.VMEM((RG_CHUNK + 2 * HALO, D_RG), F32),
            pltpu.VMEM((RG_CHUNK, D_RG), F32),
            pltpu.VMEM((RG_CHUNK, D_RG), F32),
            pltpu.VMEM((RG_CHUNK, D_RG), F32),
            pltpu.VMEM((t_len, D_RG), F32),
        ],
        compiler_params=_ARB1,
    )(z, h0, p['pool_w'], p['pool_scale'], p['conv_w'], p['conv_b'], p['wg'], p['bg'], p['lam'])


def _rwkv_kernel(z_ref, s0_ref, mu_ref, kkw_ref, ka_ref, rk_ref, g2_ref, w0_ref, w2_ref, a0_ref, a2_ref,
                 lnw_ref, lnb_ref, tri_ref, bd_ref,
                 y_ref, sfin_ref, ebuf, s_s, yd_s, bon_s, g_s, *, t_len):
    L = SCAN_CHUNK
    n_ch = t_len // L
    L2 = 2 * L
    zero = jnp.zeros((RW_HEAD, RW_HEAD), F32)
    for d in range(2):
        for p in range(RW_PAIRS):
            top = jnp.concatenate([s0_ref[d, 2 * p], zero], axis=1)
            bot = jnp.concatenate([zero, s0_ref[d, 2 * p + 1]], axis=1)
            s_s[d, p] = jnp.concatenate([top, bot], axis=0)


    def stack(x):
        first_head = lax.broadcasted_iota(jnp.int32, (L, LANES), 1) < RW_HEAD
        return jnp.concatenate([jnp.where(first_head, x, 0.0), jnp.where(first_head, 0.0, x)],
                               axis=0).astype(BF16)

    def prep(c0, d):
        ones_bd = bd_ref[...]
        ebuf[d] = _load_ext(z_ref, c0, L, t_len, 0, RW_ZCOLS)
        cur = ebuf[d, HALO:HALO + L, :]
        prev = ebuf[d, HALO - 1:HALO - 1 + L, :]
        nxt = ebuf[d, HALO + 1:HALO + 1 + L, :]
        zs = cur + mu_ref[0:1, :] * (prev - cur) + mu_ref[1:2, :] * (nxt - cur)
        r = zs[:, 0:D_RW]
        k = zs[:, D_RW:2 * D_RW]
        v = zs[:, 2 * D_RW:3 * D_RW]
        o = 3 * D_RW
        gd = zs[:, o:o + RW_G_LORA]
        wd = zs[:, o + RW_G_LORA:o + RW_G_LORA + LANES]
        ad = zs[:, o + RW_G_LORA + LANES:o + RW_G_LORA + 2 * LANES]

        kkv = k * kkw_ref[...]
        nrm = jnp.sqrt(_head_sum(kkv * kkv, ones_bd))
        kkn = kkv / jnp.maximum(nrm, 1e-12)

        tw_hi, tw_lo = _split2(jnp.tanh(wd))
        w_lora = _dot(tw_hi, w2_ref[d, 0]) + (_dot(tw_hi, w2_ref[d, 1]) + _dot(tw_lo, w2_ref[d, 0]))
        w_log = -_softplus(-(w0_ref[d:d + 1, :] + w_lora)) - 0.5
        lw = -jnp.exp(w_log)
        a = _sigmoid(a0_ref[d:d + 1, :] + _dot(ad.astype(BF16), a2_ref[d]))
        kd = k * (1.0 + (a - 1.0) * ka_ref[...])
        bon_s[d, pl.ds(c0, L), :] = _head_sum(r * kd * rk_ref[...], ones_bd, single_pass=True) * v
        if d == 0:
            g_s[pl.ds(c0, L), :] = _dot(_sigmoid(gd).astype(BF16), g2_ref[...])

        cum = _dot_exact_lhs(tri_ref[d], lw)
        e_pos = jnp.exp(cum)
        e_neg = jnp.exp(-cum)
        rh = r * e_pos
        kh = kd * e_neg
        bh = (kkn * a) * e_neg
        ah = -kkn * jnp.exp(cum - lw)
        last = L - 1 if d == 0 else 0
        gam = e_pos[last:last + 1, :]
        bhg = bh * gam
        khg = kh * gam

        chains = []
        for p in range(RW_PAIRS):
            cols = slice(p * LANES, (p + 1) * LANES)
            chains.append(dict(
                d=d, p=p, gam=gam[:, cols], v_st=stack(v[:, cols]),
                lhs=jnp.concatenate([stack(ah[:, cols]), stack(rh[:, cols])], axis=0),
                rhs=jnp.concatenate([stack(bh[:, cols]), stack(kh[:, cols])], axis=0),
                bk=jnp.concatenate([stack(bhg[:, cols]), stack(khg[:, cols])], axis=0)))
        return chains

    n_levels = int(math.log2(L))

    def state_free(chains):
        ri = lax.broadcasted_iota(jnp.int32, (L2, L2), 0)
        ci = lax.broadcasted_iota(jnp.int32, (L2, L2), 1)
        eye = jnp.where(ri == ci, 1.0, 0.0)
        rl = jnp.where(ri < L, ri, ri - L)
        cl = jnp.where(ci < L, ci, ci - L)
        strict = (cl < rl, cl > rl)
        incl = (cl <= rl, cl >= rl)
        level = ri ^ ci

        for ch in chains:
            ch['sc'] = _dot_nt(ch['lhs'], ch['rhs'])
        for ch in chains:
            sc, d = ch['sc'], ch['d']
            n_ab = jnp.where(strict[d], sc[0:L2, 0:L2], 0.0)
            m_ak = jnp.where(strict[d], sc[0:L2, L2:2 * L2], 0.0).astype(BF16)
            t_rb = jnp.where(incl[d], sc[L2:2 * L2, 0:L2], 0.0)
            t_rk = jnp.where(incl[d], sc[L2:2 * L2, L2:2 * L2], 0.0)
            ch['t_rbk'] = jnp.concatenate([t_rb, t_rk], axis=1).astype(BF16)
            ch['mv'] = _dot(m_ak, ch['v_st'])
            ch['t'] = eye + jnp.where(level == 1, n_ab, 0.0)
            ch['n_lv'] = [jnp.where(lax.shift_right_logical(level, lv) == 1, n_ab, 0.0).astype(BF16)
                          for lv in range(1, n_levels)]
            del ch['sc']
        for lv in range(n_levels - 1):
            for ch in chains:
                ch['tb'] = ch['t'].astype(BF16)
                ch['x'] = _dot(ch['n_lv'][lv], ch['tb']).astype(BF16)
            for ch in chains:
                ch['t'] = ch['t'] + _dot(ch['tb'], ch['x'])

    def state_step(chains, c0s):
        for ch in chains:
            ch['s_prev'] = s_s[ch['d'], ch['p']]
            ch['ar_s'] = _dot_nt(ch['lhs'], ch['s_prev'].astype(BF16))
        for ch in chains:
            u_rhs = ch['ar_s'][0:L2] + ch['mv']
            u_st = _dot(ch['t'].astype(BF16), u_rhs.astype(BF16)).astype(BF16)
            ch['uv'] = jnp.concatenate([u_st, ch['v_st']], axis=0)
        for ch in chains:
            y_st = ch['ar_s'][L2:2 * L2] + _dot(ch['t_rbk'], ch['uv'])
            ch['y'] = y_st[0:L] + y_st[L:L2]
            s_s[ch['d'], ch['p']] = ch['s_prev'] * ch['gam'] + _dot_tn(ch['uv'], ch['bk'])
        for d in range(2):
            yd_s[d, pl.ds(c0s[d], L), :] = jnp.concatenate([ch['y'] for ch in chains if ch['d'] == d], axis=1)

    steps = 2

    def body(i, carry):
        groups = []
        for j in range(steps):
            step = i * steps + j
            c0s = (pl.multiple_of(step * L, L), pl.multiple_of((n_ch - 1 - step) * L, L))
            groups.append((prep(c0s[0], 0) + prep(c0s[1], 1), c0s))
        state_free([ch for chains, _ in groups for ch in chains])
        for chains, c0s in groups:
            state_step(chains, c0s)
        return carry

    lax.fori_loop(0, n_ch // steps, body, 0)

    def epilogue(i, carry):
        c0 = pl.multiple_of(i * L, L)
        ones_bd = bd_ref[...]
        ys = yd_s[0, pl.ds(c0, L), :] + yd_s[1, pl.ds(c0, L), :]
        mu = _head_sum(ys, ones_bd) * (1.0 / RW_HEAD)
        yc = ys - mu
        var = _head_sum(yc * yc, ones_bd) * (1.0 / RW_HEAD)
        yn = yc * lax.rsqrt(var + RW_LN_EPS) * lnw_ref[...] + lnb_ref[...]
        out = (yn + (bon_s[0, pl.ds(c0, L), :] + bon_s[1, pl.ds(c0, L), :])) * g_s[pl.ds(c0, L), :]
        y_ref[pl.ds(c0, L), :] = out.astype(y_ref.dtype)
        return carry

    lax.fori_loop(0, n_ch, epilogue, 0, unroll=2)
    for d in range(2):
        for p in range(RW_PAIRS):
            s_pair = s_s[d, p]
            sfin_ref[d, 2 * p] = s_pair[0:RW_HEAD, 0:RW_HEAD]
            sfin_ref[d, 2 * p + 1] = s_pair[RW_HEAD:, RW_HEAD:]


def _rwkv_mixer(z, s0, p, n_batch, t_len):
    L = SCAN_CHUNK
    c2 = lambda b: (0, 0)
    c3 = lambda b: (0, 0, 0)
    st_spec = pl.BlockSpec((None, 2, RW_HEADS, RW_HEAD, RW_HEAD), lambda b: (b, 0, 0, 0, 0))
    return pl.pallas_call(
        functools.partial(_rwkv_kernel, t_len=t_len),
        grid=(n_batch,),
        in_specs=[
            pl.BlockSpec((t_len, RW_ZCOLS), lambda b: (b, 0)),
            st_spec,
            pl.BlockSpec((2, RW_ZCOLS), c2),
            pl.BlockSpec((1, D_RW), c2),
            pl.BlockSpec((1, D_RW), c2),
            pl.BlockSpec((1, D_RW), c2),
            pl.BlockSpec((RW_G_LORA, D_RW), c2),
            pl.BlockSpec((2, D_RW), c2),
            pl.BlockSpec((2, 2, LANES, D_RW), lambda b: (0, 0, 0, 0)),
            pl.BlockSpec((2, D_RW), c2),
            pl.BlockSpec((2, LANES, D_RW), c3),
            pl.BlockSpec((1, D_RW), c2),
            pl.BlockSpec((1, D_RW), c2),
            pl.BlockSpec((2, L, L), c3),
            pl.BlockSpec((LANES, LANES), c2),
        ],
        out_specs=[pl.BlockSpec((t_len, D_RW), lambda b: (b, 0)), st_spec],
        out_shape=[
            jax.ShapeDtypeStruct((n_batch * t_len, D_RW), Y_DTYPE),
            jax.ShapeDtypeStruct((n_batch, 2, RW_HEADS, RW_HEAD, RW_HEAD), F32),
        ],
        scratch_shapes=[
            pltpu.VMEM((2, L + 2 * HALO, RW_ZCOLS), F32),
            pltpu.VMEM((2, RW_PAIRS, LANES, LANES), F32),
            pltpu.VMEM((2, t_len, D_RW), F32),
            pltpu.VMEM((2, t_len, D_RW), F32),
            pltpu.VMEM((t_len, D_RW), F32),
        ],
        compiler_params=_ARB1,
    )(z, s0, p['mu'], p['kk'], p['ka'], p['rk'], p['g2'], p['w0'], p['w2'], p['a0'], p['a2'],
      p['ln_w'], p['ln_b'], p['tri'], p['ones_bd'])


def _mlstm2_kernel(z_ref, c0_ref, n0_ref, m0_ref, cw_ref, cb_ref, gb_ref, nw_ref, tri_ref, esel_ref, shift_ref,
                   ones_ref, y_ref, cfin_ref, nfin_ref, mfin_ref, c_s, n_s, m_s, hd_s, qk_s, gx_s, gr_s,
                   *, t_len):
    L = ML_SCAN
    n_ch = t_len // L
    c_s[...] = c0_ref[...]
    n_s[...] = n0_ref[...]
    m_s[...] = m0_ref[...]
    q_scale = ML_DK ** -0.5
    g_lo = 4 * D_ML
    n_real = 4 * ML_HEADS

    def conv_chunk(i, carry):
        c0 = pl.multiple_of(i * L, L)
        ext = _load_ext(z_ref, c0, L, t_len, 0, 2 * D_ML, raw=True)
        taps = _dot(shift_ref[...], ext)
        qk = cb_ref[...] + ext[HALO:HALO + L].astype(F32) * cw_ref[CONV_LEFT:CONV_LEFT + 1, :]
        for n, j in enumerate(jj for jj in range(CONV_W) if jj != CONV_LEFT):
            qk = qk + taps[n * L:(n + 1) * L] * cw_ref[j:j + 1, :]
        qk = _silu(qk)
        qk_s[pl.ds(c0, L), 0:D_ML] = (qk[:, :D_ML] * q_scale).astype(BF16)
        qk_s[pl.ds(c0, L), D_ML:2 * D_ML] = qk[:, D_ML:].astype(BF16)

        gts = z_ref[pl.ds(c0, L), g_lo:g_lo + LANES].astype(F32) + gb_ref[...]
        x16 = gts.T[0:n_real, :]
        lf_hi, lf_lo = _split2(-_softplus(-x16))
        lane = lax.broadcasted_iota(jnp.int32, (8, L), 1)
        for d in range(2):
            b16 = _dot_nt(lf_hi, tri_ref[d]) + _dot_nt(lf_lo, tri_ref[d])
            x8 = x16[8 * d:8 * d + 8, :]
            b8 = b16[8 * d:8 * d + 8, :]
            g8 = pltpu.roll(x8, ML_HEADS, axis=0) - b8
            cm8 = g8
            step = 1
            while step < L:
                if d == 0:
                    sh = jnp.where(lane >= step, pltpu.roll(cm8, step, axis=1), -jnp.inf)
                else:
                    sh = jnp.where(lane < L - step, pltpu.roll(cm8, L - step, axis=1), -jnp.inf)
                cm8 = jnp.maximum(cm8, sh)
                step *= 2
            x32 = jnp.concatenate([g8, b8, cm8, jnp.zeros((8, L), F32)], axis=0)
            hi = x32.astype(BF16)
            r1 = x32 - hi.astype(F32)
            mid = r1.astype(BF16)
            lo = (r1 - mid.astype(F32)).astype(BF16)
            gx_s[d, i] = jnp.concatenate([hi, mid, lo], axis=0)
            gr_s[d, i] = g8
        return carry

    lax.fori_loop(0, n_ch, conv_chunk, 0, unroll=2)

    def prep(ci, d):
        c0 = pl.multiple_of(ci * L, L)
        q = qk_s[pl.ds(c0, L), 0:D_ML]
        k = qk_s[pl.ds(c0, L), D_ML:2 * D_ML]
        v = z_ref[pl.ds(c0, L), 2 * D_ML:3 * D_ML]
        g8 = gr_s[d, ci]
        cb = _dot_tn(gx_s[d, ci], esel_ref[...])
        ri = lax.broadcasted_iota(jnp.int32, (L, L), 0)
        ci = lax.broadcasted_iota(jnp.int32, (L, L), 1)
        causal = (ci <= ri) if d == 0 else (ci >= ri)
        last = L - 1 if d == 0 else 0
        ones = ones_ref[...]
        chains = []
        for h in range(ML_HEADS):
            cols = slice(h * ML_DK, (h + 1) * ML_DK)
            chains.append(dict(
                d=d, h=h, last=last, causal=causal,
                g_row=g8[ML_HEADS + h:ML_HEADS + h + 1, :],
                g_b=cb[:, cols], b_b=cb[:, D_ML + h * ML_DK:D_ML + (h + 1) * ML_DK],
                cm_b=cb[:, 2 * D_ML + h * ML_DK:2 * D_ML + (h + 1) * ML_DK],
                qb=q[:, cols], kb=k[:, cols],
                v=v[:, cols].astype(F32), v_ones=jnp.concatenate([v[:, cols].astype(BF16), ones], axis=1)))
        return chains

    def body(i, carry):
        c0s = (pl.multiple_of(i * L, L), pl.multiple_of((n_ch - 1 - i) * L, L))
        chains = prep(i, 0) + prep(n_ch - 1 - i, 1)
        for ch in chains:
            d, h = ch['d'], ch['h']
            ch['qk'] = _dot_nt(ch['qb'], ch['kb'])
            ch['c_prev'] = c_s[d, h]
            ch['n_prev'] = n_s[d, h]
            ch['m_prev'] = m_s[d, h:h + 1, :]
            ch['qcn'] = _dot(ch['qb'], jnp.concatenate([ch['c_prev'].astype(BF16),
                                                        ch['n_prev'].astype(BF16)], axis=1))
        for ch in chains:
            mu = jnp.maximum(ch['m_prev'], ch['cm_b'])
            ch['mu'] = mu
            ch['s'] = ch['qk'] * jnp.exp(jnp.where(ch['causal'], ch['g_row'] - mu, -jnp.inf))
            ch['w_inter'] = jnp.exp(ch['m_prev'] - mu)
        for ch in chains:
            last = ch['last']
            ch['sv'] = _dot(ch['s'].astype(BF16), ch['v_ones'])
            mu_last = ch['mu'][last:last + 1, :]
            ch['m_new'] = ch['b_b'][last:last + 1, :] + mu_last
            ch['dec'] = jnp.exp(ch['m_prev'] - mu_last)
            w_s = jnp.exp(ch['g_b'] - mu_last)
            ch['upd'] = _dot_tn(ch['kb'], jnp.concatenate([(w_s * ch['v']).astype(BF16),
                                                           w_s.astype(BF16)], axis=1))
        for ch in chains:
            d, h = ch['d'], ch['h']
            num = ch['sv'][:, :ML_DK] + ch['w_inter'] * ch['qcn'][:, :ML_DK]
            den = ch['sv'][:, ML_DK:] + ch['w_inter'] * ch['qcn'][:, ML_DK:]
            ch['out'] = num / jnp.maximum(jnp.abs(den), jnp.exp(-(ch['b_b'] + ch['mu'])))
            c_s[d, h] = ch['dec'] * ch['c_prev'] + ch['upd'][:, :ML_DK]
            n_s[d, h] = ch['dec'] * ch['n_prev'] + ch['upd'][:, ML_DK:]
            m_s[d, h:h + 1, :] = ch['m_new']
        for d in range(2):
            hd_s[d, pl.ds(c0s[d], L), :] = jnp.concatenate([ch['out'] for ch in chains if ch['d'] == d], axis=1)
        return carry

    lax.fori_loop(0, n_ch, body, 0, unroll=2)

    def epilogue(i, carry):
        c0 = pl.multiple_of(i * L, L)
        ones = ones_ref[...]
        hs = hd_s[0, pl.ds(c0, L), :] + hd_s[1, pl.ds(c0, L), :]
        o = z_ref[pl.ds(c0, L), 3 * D_ML:4 * D_ML].astype(F32)
        mu = _head_sum(hs, ones) * (1.0 / ML_DK)
        xc = hs - mu
        var = _head_sum(xc * xc, ones) * (1.0 / ML_DK)
        y = xc * lax.rsqrt(var + NORM_EPS) * nw_ref[...] * _sigmoid(o)
        y_ref[pl.ds(c0, L), :] = y.astype(y_ref.dtype)
        return carry

    lax.fori_loop(0, n_ch, epilogue, 0, unroll=2)
    cfin_ref[...] = c_s[...]
    nfin_ref[...] = n_s[...]
    mfin_ref[...] = m_s[...]


def _mlstm2_mixer(z, c0, n0, m0, p, n_batch, t_len):
    assert z.dtype == BF16
    L = ML_SCAN
    c2 = lambda b: (0, 0)
    c3 = lambda b: (0, 0, 0)
    c_spec = pl.BlockSpec((None, 2, ML_HEADS, ML_DK, ML_DK), lambda b: (b, 0, 0, 0, 0))
    m_spec = pl.BlockSpec((None, 2, ML_HEADS, ML_DK), lambda b: (b, 0, 0, 0))
    return pl.pallas_call(
        functools.partial(_mlstm2_kernel, t_len=t_len),
        grid=(n_batch,),
        in_specs=[
            pl.BlockSpec((t_len, ML_ZCOLS), lambda b: (b, 0)),
            c_spec, c_spec, m_spec,
            pl.BlockSpec((CONV_W, 2 * D_ML), c2),
            pl.BlockSpec((1, 2 * D_ML), c2),
            pl.BlockSpec((1, LANES), c2),
            pl.BlockSpec((1, D_ML), c2),
            pl.BlockSpec((2, L, L), c3),
            pl.BlockSpec(p['esel'].shape, c2),
            pl.BlockSpec(p['shift'].shape, c2),
            pl.BlockSpec((L, LANES), c2),
        ],
        out_specs=[pl.BlockSpec((t_len, D_ML), lambda b: (b, 0)), c_spec, c_spec, m_spec],
        out_shape=[
            jax.ShapeDtypeStruct((n_batch * t_len, D_ML), Y_DTYPE),
            jax.ShapeDtypeStruct((n_batch, 2, ML_HEADS, ML_DK, ML_DK), F32),
            jax.ShapeDtypeStruct((n_batch, 2, ML_HEADS, ML_DK, ML_DK), F32),
            jax.ShapeDtypeStruct((n_batch, 2, ML_HEADS, ML_DK), F32),
        ],
        scratch_shapes=[
            pltpu.VMEM((2, ML_HEADS, ML_DK, ML_DK), F32),
            pltpu.VMEM((2, ML_HEADS, ML_DK, ML_DK), F32),
            pltpu.VMEM((2, ML_HEADS, ML_DK), F32),
            pltpu.VMEM((2, t_len, D_ML), F32),
            pltpu.VMEM((t_len, 2 * D_ML), BF16),
            pltpu.VMEM((2, t_len // L, 96, L), BF16),
            pltpu.VMEM((2, t_len // L, 8, L), F32),
        ],
        compiler_params=_ARB1,
    )(z, c0, n0, m0, p['conv_w'], p['conv_b'], p['gate_b'], p['norm_w'], p['tri'], p['esel'], p['shift'],
      p['ones'])


def _conv_shift(n):
    taps = jnp.array([j for j in range(CONV_W) if j != CONV_LEFT])
    t = jnp.arange((CONV_W - 1) * n)
    src = HALO + t % n + taps[t // n] - CONV_LEFT
    return (src[:, None] == jnp.arange(n + 2 * HALO)[None, :]).astype(BF16)


def _mlstm_consts():
    L = ML_SCAN
    lower = jnp.tril(jnp.ones((L, L), F32))
    tri = jnp.stack([lower, lower.T]).astype(BF16)
    rows = jnp.arange(32)
    blk = jnp.arange(3 * D_ML) // ML_DK
    src_row = 8 * (blk // ML_HEADS) + ML_HEADS + blk % ML_HEADS
    esel = (rows[:, None] == src_row[None, :]).astype(BF16)
    esel = jnp.concatenate([esel, esel, esel], axis=0)
    return dict(tri=tri, esel=esel, shift=_conv_shift(L), ones=jnp.ones((L, LANES), BF16))


def _block_diag(w):
    n, c, d = w.shape
    return jnp.einsum('ncd,nm->ncmd', w, jnp.eye(n, dtype=w.dtype)).reshape(n * c, n * d)


def _scan_tri():
    L = SCAN_CHUNK
    lower = jnp.tril(jnp.ones((L, L), F32))
    return jnp.stack([lower, lower.T]).astype(BF16)


def _prep_even(ev_w_in, pool_w, pool_scale, rg_conv_w, rg_conv_b, rg_wa, rg_ba, rg_wx, rg_bx, rg_lam, ev_w_out, j):
    wg = jnp.concatenate([_block_diag(rg_wa[j, 0]), _block_diag(rg_wx[j, 0]),
                          _block_diag(rg_wa[j, 1]), _block_diag(rg_wx[j, 1])], axis=1)
    bg = jnp.concatenate([rg_ba[j, 0], rg_bx[j, 0], rg_ba[j, 1], rg_bx[j, 1]]).reshape(1, 4 * D_RG)
    return dict(
        w_in=[ev_w_in[j].astype(BF16)],
        w_out=ev_w_out[j].astype(BF16),
        pool_w=pool_w[j].astype(BF16),
        pool_scale=pool_scale[j].reshape(1, D_POOL),
        conv_w=rg_conv_w[j],
        conv_b=rg_conv_b[j].reshape(1, D_RG),
        wg=wg.astype(BF16),
        bg=bg,
        lam=rg_lam[j],
    )


def _pad_cols(w, cuts, widths):
    pieces = jnp.split(w, cuts, axis=-1)
    out = []
    for piece, width in zip(pieces, widths):
        extra = width - piece.shape[-1]
        if extra:
            piece = jnp.pad(piece, [(0, 0)] * (piece.ndim - 1) + [(0, extra)])
        out.append(piece)
    return jnp.concatenate(out, axis=-1)


def _prep_odd(od_w_in, rw_mu, rw_w0, rw_w2, rw_a0, rw_a2, rw_kk, rw_ka, rw_rk, rw_g2, rw_ln_w, rw_ln_b,
              ml_conv_w, ml_conv_b, ml_bi, ml_bf, ml_norm_w, od_w_out, j):
    rw_cols = 3 * D_RW + RW_G_LORA + RW_W_LORA + RW_A_LORA
    base = 3 * D_RW + RW_G_LORA
    cuts = [base, base + RW_W_LORA]
    widths = [base, LANES, LANES]
    w_rw = _pad_cols(od_w_in[j][:, :rw_cols], cuts, widths)
    w_ml = _pad_cols(od_w_in[j][:, rw_cols:], [4 * D_ML], [4 * D_ML, LANES])
    pad_rows = lambda w: jnp.pad(w, ((0, 0), (0, LANES - w.shape[1]), (0, 0)))
    gate_b = jnp.concatenate([ml_bi[j, 0], ml_bf[j, 0], ml_bi[j, 1], ml_bf[j, 1]])
    gate_b = jnp.pad(gate_b, (0, LANES - gate_b.shape[0])).reshape(1, LANES)
    hm = jnp.arange(LANES) // RW_HEAD
    rw = dict(
        mu=_pad_cols(rw_mu[j], cuts, widths),
        kk=rw_kk[j].reshape(1, D_RW), ka=rw_ka[j].reshape(1, D_RW), rk=rw_rk[j].reshape(1, D_RW),
        g2=rw_g2[j].astype(BF16),
        w0=rw_w0[j], w2=jnp.stack(_split2(pad_rows(rw_w2[j])), axis=1),
        a0=rw_a0[j], a2=pad_rows(rw_a2[j]).astype(BF16),
        ln_w=rw_ln_w[j].reshape(1, D_RW), ln_b=rw_ln_b[j].reshape(1, D_RW),
        tri=_scan_tri(),
        ones_bd=(hm[:, None] == hm[None, :]).astype(BF16),
    )
    ml = dict(
        conv_w=ml_conv_w[j], conv_b=ml_conv_b[j].reshape(1, 2 * D_ML),
        gate_b=gate_b, norm_w=ml_norm_w[j].reshape(1, D_ML), **_mlstm_consts(),
    )
    return dict(w_in=[w_rw.astype(BF16), w_ml.astype(BF16)], w_out=od_w_out[j].astype(BF16), rw=rw, ml=ml)


def _trunk(x, mod, rows_per_mod, mod_base, grid, states, params, n_batch, t_len):
    ev, od, dense = params
    rg0, rw0, c0, n0, m0 = states
    x = x.reshape(n_batch * t_len, D_MODEL)
    spec = (rows_per_mod, mod_base)

    (z,) = _in_proj(x, mod[0], *spec, dense['norm1_w'][0], ev['w_in'])
    y_ev, rg_f = _even_mixer(z, rg0, ev, n_batch, t_len, grid)
    x = _out_mlp([y_ev], x, mod[0], *spec, dense['norm2_w'][0], ev['w_out'],
                 dense['mlp_w1'], dense['mlp_w2'], 0, dense['final_norm_w'], False)

    z_rw, z_ml = _in_proj(x, mod[1], *spec, dense['norm1_w'][1], od['w_in'])
    y_rw, s_f = _rwkv_mixer(z_rw, rw0, od['rw'], n_batch, t_len)
    y_ml, c_f, n_f, m_f = _mlstm2_mixer(z_ml, c0, n0, m0, od['ml'], n_batch, t_len)
    y = _out_mlp([y_rw, y_ml], x, mod[1], *spec, dense['norm2_w'][1], od['w_out'],
                 dense['mlp_w1'], dense['mlp_w2'], 1, dense['final_norm_w'], True)
    return y.reshape(n_batch, t_len, D_MODEL), (rg_f, s_f, c_f, n_f, m_f)


def kernel(x_prompt, x_sample, state_rglru, state_rwkv, state_mlstm_C, state_mlstm_n, state_mlstm_m,
           c, c_ctx, norm1_w, norm2_w, w_mod, b_mod, mlp_w1, mlp_w2, final_norm_w,
           ev_w_in, pool_w, pool_scale, rg_conv_w, rg_conv_b, rg_wa, rg_ba, rg_wx, rg_bx, rg_lam, ev_w_out,
           od_w_in, rw_mu, rw_w0, rw_w2, rw_a0, rw_a2, rw_kk, rw_ka, rw_rk, rw_g2, rw_ln_w, rw_ln_b,
           ml_conv_w, ml_conv_b, ml_bi, ml_bf, ml_norm_w, od_w_out):
    bp, tp, _ = x_prompt.shape
    bs, ts, _ = x_sample.shape

    n_cond = 16
    cond = jnp.concatenate([c_ctx[None, :], c, jnp.zeros((n_cond - 1 - bs, D_MODEL), F32)], axis=0)
    mod = _modulation(cond, w_mod, b_mod)

    ev = _prep_even(ev_w_in, pool_w, pool_scale, rg_conv_w, rg_conv_b, rg_wa, rg_ba, rg_wx, rg_bx, rg_lam,
                    ev_w_out, 0)
    od = _prep_odd(od_w_in, rw_mu, rw_w0, rw_w2, rw_a0, rw_a2, rw_kk, rw_ka, rw_rk, rw_g2, rw_ln_w, rw_ln_b,
                   ml_conv_w, ml_conv_b, ml_bi, ml_bf, ml_norm_w, od_w_out, 0)
    dense = dict(norm1_w=norm1_w, norm2_w=norm2_w, mlp_w1=mlp_w1.astype(BF16), mlp_w2=mlp_w2.astype(BF16),
                 final_norm_w=final_norm_w)
    params = (ev, od, dense)

    zero_states = (
        jnp.zeros((bp, 2, D_RG), F32),
        jnp.zeros((bp, 2, RW_HEADS, RW_HEAD, RW_HEAD), F32),
        jnp.zeros((bp, 2, ML_HEADS, ML_DK, ML_DK), F32),
        jnp.zeros((bp, 2, ML_HEADS, ML_DK, ML_DK), F32),
        jnp.zeros((bp, 2, ML_HEADS, ML_DK), F32),
    )
    y_prompt, (rg_f, s_f, c_f, n_f, m_f) = _trunk(
        x_prompt, mod, bp * tp, 0, False, zero_states, params, bp, tp)

    sample_states = (
        state_rglru[:, 0],
        state_rwkv[:, 0],
        state_mlstm_C[:, 0],
        jnp.broadcast_to(state_mlstm_n[:, 0][..., None], (bs, 2, ML_HEADS, ML_DK, ML_DK)),
        jnp.broadcast_to(state_mlstm_m[:, 0][..., None], (bs, 2, ML_HEADS, ML_DK)),
    )
    y_sample, _ = _trunk(x_sample, mod, ts, 1, True, sample_states, params, bs, ts)

    return (y_prompt, y_sample,
            rg_f[:, None],
            s_f[:, None],
            c_f[:, None],
            n_f[:, None, :, :, :, 0],
            m_f[:, None, :, :, 0])
```

```python
import functools
import math

import jax
import jax.numpy as jnp
from jax import lax
from jax.experimental import pallas as pl
from jax.experimental.pallas import tpu as pltpu

F32 = jnp.float32
BF16 = jnp.bfloat16

D_MODEL = 1024
DEPTH = 2
GRID_W = 64
D_FF = 4 * D_MODEL
NORM_EPS = 1e-6
N_MOD = 6

D_POOL = D_MODEL // 2
POOL_WINDOWS = (2, 4, 8, 16)
POOL_GW = D_POOL // len(POOL_WINDOWS)
D_RG = D_MODEL // 2
RG_BLOCKS = 8
RG_C = 8.0
CONV_W = 4
CONV_LEFT = 2
EV_COLS = D_POOL + 2 * D_RG

D_RW = D_MODEL // 2
RW_HEAD = 64
RW_HEADS = D_RW // RW_HEAD
RW_PAIRS = RW_HEADS // 2
RW_W_LORA = 64
RW_A_LORA = 64
RW_G_LORA = 128
RW_LN_EPS = 64e-5
RW_ZCOLS = 3 * D_RW + RW_G_LORA + 128 + 128
D_ML = D_MODEL // 2
ML_HEADS = 4
ML_DK = D_ML // ML_HEADS
ML_ZCOLS = 4 * D_ML + 128

LANES = 128
HALO = 16
Z_DTYPE = BF16
Y_DTYPE = BF16
ROW_BLOCK = 512
SCAN_CHUNK = 64
ML_SCAN = 128
RG_CHUNK = 256
VMEM_LIMIT = 56 * 1024 * 1024

_ARB1 = pltpu.CompilerParams(dimension_semantics=("arbitrary",), vmem_limit_bytes=VMEM_LIMIT)
_ARB2 = pltpu.CompilerParams(dimension_semantics=("arbitrary", "arbitrary"), vmem_limit_bytes=VMEM_LIMIT)


def _dot(a, b):
    return jnp.dot(a, b, preferred_element_type=F32)


def _dot_nt(a, b):
    return lax.dot_general(a, b, (((1,), (1,)), ((), ())), preferred_element_type=F32)


def _dot_tn(a, b):
    return lax.dot_general(a, b, (((0,), (0,)), ((), ())), preferred_element_type=F32)


def _split2(x):
    hi = x.astype(BF16)
    lo = (x - hi.astype(F32)).astype(BF16)
    return hi, lo


def _dot_f32(a, b):
    ah, al = _split2(a)
    bh, bl = _split2(b)
    return _dot(ah, bh) + (_dot(ah, bl) + _dot(al, bh))


def _dot_exact_lhs(m_bf16, x):
    hi, lo = _split2(x)
    return _dot(m_bf16, hi) + _dot(m_bf16, lo)


def _dot_exact_rhs(x, m_bf16):
    hi, lo = _split2(x)
    return _dot(hi, m_bf16) + _dot(lo, m_bf16)


def _softplus(x):
    return jnp.maximum(x, 0.0) + jnp.log1p(jnp.exp(-jnp.abs(x)))


def _sigmoid(x):
    return jax.nn.sigmoid(x)


def _silu(x):
    return x * jax.nn.sigmoid(x)


def _gelu_tanh(x):
    c = math.sqrt(2.0 / math.pi)
    return x * (0.5 * (1.0 + jnp.tanh(c * (x + 0.044715 * (x * x * x)))))


def _rms_norm(x, w):
    ms = jnp.mean(x * x, axis=-1, keepdims=True)
    return x * lax.rsqrt(ms + NORM_EPS) * w


def _load_ext(z_ref, c0, n, t_len, lo, hi, raw=False):
    dt = z_ref.dtype if raw else F32
    cur = z_ref[pl.ds(c0, n), lo:hi].astype(dt)
    pb = pl.multiple_of(jnp.maximum(c0 - HALO, 0), HALO)
    pa = pl.multiple_of(jnp.minimum(c0 + n, t_len - HALO), HALO)
    before = z_ref[pl.ds(pb, HALO), lo:hi].astype(dt)
    after = z_ref[pl.ds(pa, HALO), lo:hi].astype(dt)
    before = jnp.where(c0 > 0, before, jnp.zeros_like(before))
    after = jnp.where(c0 + n < t_len, after, jnp.zeros_like(after))
    return jnp.concatenate([before, cur, after], axis=0)


def _head_sum(x, ones_bd, single_pass=False):
    n = x.shape[0]
    groups = x.shape[1] // LANES
    xs = jnp.concatenate([x[:, g * LANES:(g + 1) * LANES] for g in range(groups)], axis=0)
    s = _dot(xs.astype(BF16), ones_bd) if single_pass else _dot_exact_rhs(xs, ones_bd)
    return jnp.concatenate([s[g * n:(g + 1) * n] for g in range(groups)], axis=1)


def _mod_kernel(c_ref, w_ref, b_ref, o_ref):
    s = _silu(c_ref[...])
    o_ref[...] = _dot_f32(s, w_ref[...]) + b_ref[...]


def _modulation(cond, w_mod, b_mod):
    rows = cond.shape[0]
    tn = 1536
    out = pl.pallas_call(
        _mod_kernel,
        grid=(DEPTH, N_MOD * D_MODEL // tn),
        in_specs=[
            pl.BlockSpec((rows, D_MODEL), lambda l, j: (0, 0)),
            pl.BlockSpec((None, D_MODEL, tn), lambda l, j: (l, 0, j)),
            pl.BlockSpec((None, 1, tn), lambda l, j: (l, 0, j)),
        ],
        out_specs=pl.BlockSpec((None, rows, tn), lambda l, j: (l, 0, j)),
        out_shape=jax.ShapeDtypeStruct((DEPTH, rows, N_MOD * D_MODEL), F32),
        compiler_params=_ARB2,
    )(cond, w_mod, b_mod.reshape(DEPTH, 1, N_MOD * D_MODEL))
    return out.reshape(DEPTH, rows, N_MOD, D_MODEL)


def _inproj_kernel(*refs, n_out, col_chunk):
    x_ref, mod_ref, nw_ref = refs[:3]
    w_refs = refs[3:3 + n_out]
    z_refs = refs[3 + n_out:]
    h = _rms_norm(x_ref[...], nw_ref[...])
    h = h * (1.0 + mod_ref[1:2, :]) + mod_ref[0:1, :]
    hb = h.astype(BF16)
    for w_ref, z_ref in zip(w_refs, z_refs):
        n = w_ref.shape[1]
        for c in range(0, n, col_chunk):
            e = min(c + col_chunk, n)
            z_ref[:, c:e] = _dot(hb, w_ref[:, c:e]).astype(z_ref.dtype)


def _in_proj(x, mod, rows_per_mod, mod_base, norm_w, weights):
    m = x.shape[0]
    tm = ROW_BLOCK
    n_out = len(weights)
    const = lambda i: (0, 0)
    in_specs = [
        pl.BlockSpec((tm, D_MODEL), lambda i: (i, 0)),
        pl.BlockSpec((None, N_MOD, D_MODEL), lambda i: (mod_base + (i * tm) // rows_per_mod, 0, 0)),
        pl.BlockSpec((1, D_MODEL), const),
    ] + [pl.BlockSpec(w.shape, const, pipeline_mode=pl.Buffered(1)) for w in weights]
    out_specs = [pl.BlockSpec((tm, w.shape[1]), lambda i: (i, 0)) for w in weights]
    out_shape = [jax.ShapeDtypeStruct((m, w.shape[1]), Z_DTYPE) for w in weights]
    return pl.pallas_call(
        functools.partial(_inproj_kernel, n_out=n_out, col_chunk=512),
        grid=(m // tm,),
        in_specs=in_specs,
        out_specs=out_specs,
        out_shape=out_shape,
        compiler_params=_ARB1,
    )(x, mod, norm_w.reshape(1, D_MODEL), *weights)


def _outmlp_kernel(*refs, n_y, final, ff_chunk):
    y_refs = refs[:n_y]
    x_ref, mod_ref, nw_ref, wo_ref, w1_ref, w2_ref, fw_ref, o_ref = refs[n_y:]
    y = None
    row = 0
    for y_ref in y_refs:
        k = y_ref.shape[1]
        part = _dot(y_ref[...], wo_ref[row:row + k, :])
        y = part if y is None else y + part
        row += k
    x = x_ref[...] + mod_ref[2:3, :] * y
    h = _rms_norm(x, nw_ref[...]) * (1.0 + mod_ref[4:5, :]) + mod_ref[3:4, :]
    hb = h.astype(BF16)
    acc = None
    for c in range(0, D_FF, ff_chunk):
        u = jnp.maximum(_dot(hb, w1_ref[:, c:c + ff_chunk]), 0.0)
        part = _dot((u * u).astype(BF16), w2_ref[c:c + ff_chunk, :])
        acc = part if acc is None else acc + part
    x = x + mod_ref[5:6, :] * acc
    if final:
        x = _rms_norm(x, fw_ref[...])
    o_ref[...] = x


def _out_mlp(ys, x, mod, rows_per_mod, mod_base, norm_w, w_out, w1, w2, layer, final_w, final):
    m = x.shape[0]
    tm = ROW_BLOCK
    const = lambda i: (0, 0)
    single = dict(pipeline_mode=pl.Buffered(1))
    pick = lambda i: (layer, 0, 0)
    in_specs = [pl.BlockSpec((tm, y.shape[1]), lambda i: (i, 0)) for y in ys] + [
        pl.BlockSpec((tm, D_MODEL), lambda i: (i, 0)),
        pl.BlockSpec((None, N_MOD, D_MODEL), lambda i: (mod_base + (i * tm) // rows_per_mod, 0, 0)),
        pl.BlockSpec((1, D_MODEL), const),
        pl.BlockSpec(w_out.shape, const, **single),
        pl.BlockSpec((None,) + w1.shape[1:], pick, **single),
        pl.BlockSpec((None,) + w2.shape[1:], pick, **single),
        pl.BlockSpec((1, D_MODEL), const),
    ]
    return pl.pallas_call(
        functools.partial(_outmlp_kernel, n_y=len(ys), final=final, ff_chunk=1024),
        grid=(m // tm,),
        in_specs=in_specs,
        out_specs=pl.BlockSpec((tm, D_MODEL), lambda i: (i, 0)),
        out_shape=jax.ShapeDtypeStruct((m, D_MODEL), F32),
        compiler_params=_ARB1,
    )(*ys, x, mod, norm_w.reshape(1, D_MODEL), w_out, w1, w2, final_w.reshape(1, D_MODEL))


def _even_kernel(z_ref, h0_ref, pw_ref, ps_ref, cw_ref, cb_ref, wg_ref, bg_ref, lam_ref, band_ref,
                 y_ref, hfin_ref, cbuf, a_s, b_s, h_s, hf_s, *, t_len, grid):
    seg = GRID_W if grid else t_len
    n_seg = t_len // seg
    n_groups = len(POOL_WINDOWS)

    pos = lax.broadcasted_iota(jnp.int32, (seg, POOL_GW), 0)
    offs = [(w // 2, w - 1 - w // 2) for w in POOL_WINDOWS]
    gcols = [slice(g * POOL_GW, (g + 1) * POOL_GW) for g in range(n_groups)]

    def grid_row(rr, cols):
        valid = jnp.logical_and(rr >= 0, rr < n_seg)
        src = pl.multiple_of(jnp.clip(rr, 0, n_seg - 1) * seg, seg)
        return jnp.where(valid, z_ref[pl.ds(src, seg), cols].astype(F32), 0.0)

    def pool_rows(rows, run):
        items = []
        for r in rows:
            base = r * seg if isinstance(r, int) else pl.multiple_of(r * seg, seg)
            new_run = []
            for g, (lo_off, hi_off) in enumerate(offs):
                cols = gcols[g]
                if grid:
                    s1 = run[g] + grid_row(r + hi_off, cols) - grid_row(r - lo_off - 1, cols)
                    new_run.append(s1)
                    cnt_r = jnp.minimum(r + hi_off, n_seg - 1) - jnp.maximum(r - lo_off, 0) + 1
                    m1 = s1 / cnt_r.astype(F32)
                    hi = m1.astype(BF16)
                    r1 = m1 - hi.astype(F32)
                    mid = r1.astype(BF16)
                    parts = jnp.concatenate([hi, mid, (r1 - mid.astype(F32)).astype(BF16)], axis=0)
                else:
                    parts = z_ref[pl.ds(base, seg), cols]
                items.append(dict(g=g, base=base, parts=parts))
            run = new_run
        for it in items:
            it['s2'] = _dot(band_ref[it['g']], it['parts'])
        for it in items:
            lo_off, hi_off = offs[it['g']]
            cnt_c = jnp.minimum(pos + hi_off, seg - 1) - jnp.maximum(pos - lo_off, 0) + 1
            xg = z_ref[pl.ds(it['base'], seg), gcols[it['g']]].astype(F32)
            it['d'] = (it['s2'] / cnt_c.astype(F32) - xg).astype(BF16)
        for it in items:
            it['y'] = _dot(it['d'], pw_ref[it['g']])
        for it in items:
            cols = gcols[it['g']]
            y_ref[pl.ds(it['base'], seg), cols] = (it['y'] * ps_ref[:, cols]).astype(y_ref.dtype)
        return tuple(run) if grid else None

    if grid:
        init = []
        for g, (lo_off, hi_off) in enumerate(offs):
            s1 = jnp.zeros((seg, POOL_GW), F32)
            for o in range(min(hi_off, n_seg)):
                s1 = s1 + z_ref[o * seg:(o + 1) * seg, gcols[g]].astype(F32)
            init.append(s1)
        per_trip = 8
        lax.fori_loop(0, n_seg // per_trip,
                      lambda i, run: pool_rows([i * per_trip + j for j in range(per_trip)], run), tuple(init))
    else:
        pool_rows([0], None)

    ch = RG_CHUNK
    n_ch = t_len // ch
    rg_lo, rg_hi = D_POOL, D_POOL + D_RG

    def gates(c0, d):
        cbuf[...] = _load_ext(z_ref, c0, ch, t_len, rg_lo, rg_hi)
        xc = cb_ref[...]
        for j in range(CONV_W):
            off = HALO - CONV_LEFT + j
            xc = xc + cbuf[off:off + ch, :] * cw_ref[j:j + 1, :]
        g = _dot(xc.astype(BF16), wg_ref[:, d * 2 * D_RG:(d + 1) * 2 * D_RG]) \
            + bg_ref[:, d * 2 * D_RG:(d + 1) * 2 * D_RG]
        r = _sigmoid(g[:, :D_RG])
        i = _sigmoid(g[:, D_RG:])
        log_a = (-RG_C) * r * _softplus(-lam_ref[d:d + 1, :])
        a = jnp.exp(log_a)
        a_s[...] = a
        one_minus_a2 = jnp.tanh(-log_a) * (a * a + 1.0)
        root = jnp.where(one_minus_a2 > 0.0, one_minus_a2 * lax.rsqrt(one_minus_a2), 0.0)
        b_s[...] = root * (i * xc)

    def scan_rows(h, reverse):
        def row(t, h):
            tt = (ch - 1 - t) if reverse else t
            h = a_s[pl.ds(tt, 1), :] * h + b_s[pl.ds(tt, 1), :]
            h_s[pl.ds(tt, 1), :] = h
            return h
        return lax.fori_loop(0, ch, row, h, unroll=8)

    def fwd_body(i, h):
        c0 = pl.multiple_of(i * ch, ch)
        gates(c0, 0)
        h = scan_rows(h, False)
        hf_s[pl.ds(c0, ch), :] = h_s[...]
        return h

    h_fwd = lax.fori_loop(0, n_ch, fwd_body, h0_ref[0:1, :])

    def bwd_body(i, h):
        c0 = pl.multiple_of((n_ch - 1 - i) * ch, ch)
        gates(c0, 1)
        h = scan_rows(h, True)
        u_gate = z_ref[pl.ds(c0, ch), rg_hi:rg_hi + D_RG].astype(F32)
        y = (hf_s[pl.ds(c0, ch), :] + h_s[...]) * _gelu_tanh(u_gate)
        y_ref[pl.ds(c0, ch), D_POOL:D_POOL + D_RG] = y.astype(y_ref.dtype)
        return h

    h_bwd = lax.fori_loop(0, n_ch, bwd_body, h0_ref[1:2, :])
    hfin_ref[0:1, :] = h_fwd
    hfin_ref[1:2, :] = h_bwd


def _even_mixer(z, h0, p, n_batch, t_len, grid):
    assert z.dtype == BF16
    seg = GRID_W if grid else t_len
    const2 = lambda b: (0, 0)
    const3 = lambda b: (0, 0, 0)
    idx = jnp.arange(seg)
    delta = idx[None, :] - idx[:, None]
    band = jnp.stack([jnp.logical_and(delta >= -(w // 2), delta <= w - 1 - w // 2) for w in POOL_WINDOWS])
    band = jnp.tile(band.astype(BF16), (1, 1, 3 if grid else 1))
    return pl.pallas_call(
        functools.partial(_even_kernel, t_len=t_len, grid=grid),
        grid=(n_batch,),
        in_specs=[
            pl.BlockSpec((t_len, EV_COLS), lambda b: (b, 0)),
            pl.BlockSpec((None, 2, D_RG), lambda b: (b, 0, 0)),
            pl.BlockSpec(p['pool_w'].shape, const3),
            pl.BlockSpec((1, D_POOL), const2),
            pl.BlockSpec((CONV_W, D_RG), const2),
            pl.BlockSpec((1, D_RG), const2),
            pl.BlockSpec((D_RG, 4 * D_RG), const2),
            pl.BlockSpec((1, 4 * D_RG), const2),
            pl.BlockSpec((2, D_RG), const2),
            pl.BlockSpec(band.shape, const3),
        ],
        out_specs=[
            pl.BlockSpec((t_len, D_POOL + D_RG), lambda b: (b, 0)),
            pl.BlockSpec((None, 2, D_RG), lambda b: (b, 0, 0)),
        ],
        out_shape=[
            jax.ShapeDtypeStruct((n_batch * t_len, D_POOL + D_RG), Y_DTYPE),
            jax.ShapeDtypeStruct((n_batch, 2, D_RG), F32),
        ],
        scratch_shapes=[
            pltpu.VMEM((RG_CHUNK + 2 * HALO, D_RG), F32),
            pltpu.VMEM((RG_CHUNK, D_RG), F32),
            pltpu.VMEM((RG_CHUNK, D_RG), F32),
            pltpu.VMEM((RG_CHUNK, D_RG), F32),
            pltpu.VMEM((t_len, D_RG), F32),
        ],
        compiler_params=_ARB1,
    )(z, h0, p['pool_w'], p['pool_scale'], p['conv_w'], p['conv_b'], p['wg'], p['bg'], p['lam'], band)


def _rwkv_kernel(z_ref, s0_ref, mu_ref, kkw_ref, ka_ref, rk_ref, g2_ref, w0_ref, w2_ref, a0_ref, a2_ref,
                 lnw_ref, lnb_ref, tri_ref, bd_ref,
                 y_ref, sfin_ref, ebuf, s_s, yd_s, bon_s, g_s, *, t_len):
    L = SCAN_CHUNK
    n_ch = t_len // L
    L2 = 2 * L
    zero = jnp.zeros((RW_HEAD, RW_HEAD), F32)
    for d in range(2):
        for p in range(RW_PAIRS):
            top = jnp.concatenate([s0_ref[d, 2 * p], zero], axis=1)
            bot = jnp.concatenate([zero, s0_ref[d, 2 * p + 1]], axis=1)
            s_s[d, p] = jnp.concatenate([top, bot], axis=0)


    def stack(x):
        first_head = lax.broadcasted_iota(jnp.int32, (L, LANES), 1) < RW_HEAD
        return jnp.concatenate([jnp.where(first_head, x, 0.0), jnp.where(first_head, 0.0, x)],
                               axis=0).astype(BF16)

    def prep(c0, d):
        ones_bd = bd_ref[...]
        ebuf[d] = _load_ext(z_ref, c0, L, t_len, 0, RW_ZCOLS)
        cur = ebuf[d, HALO:HALO + L, :]
        prev = ebuf[d, HALO - 1:HALO - 1 + L, :]
        nxt = ebuf[d, HALO + 1:HALO + 1 + L, :]
        zs = cur + mu_ref[0:1, :] * (prev - cur) + mu_ref[1:2, :] * (nxt - cur)
        r = zs[:, 0:D_RW]
        k = zs[:, D_RW:2 * D_RW]
        v = zs[:, 2 * D_RW:3 * D_RW]
        o = 3 * D_RW
        gd = zs[:, o:o + RW_G_LORA]
        wd = zs[:, o + RW_G_LORA:o + RW_G_LORA + LANES]
        ad = zs[:, o + RW_G_LORA + LANES:o + RW_G_LORA + 2 * LANES]

        kkv = k * kkw_ref[...]
        nrm = jnp.sqrt(_head_sum(kkv * kkv, ones_bd))
        kkn = kkv / jnp.maximum(nrm, 1e-12)

        tw_hi, tw_lo = _split2(jnp.tanh(wd))
        w_lora = _dot(tw_hi, w2_ref[d, 0]) + (_dot(tw_hi, w2_ref[d, 1]) + _dot(tw_lo, w2_ref[d, 0]))
        w_log = -_softplus(-(w0_ref[d:d + 1, :] + w_lora)) - 0.5
        lw = -jnp.exp(w_log)
        a = _sigmoid(a0_ref[d:d + 1, :] + _dot(ad.astype(BF16), a2_ref[d]))
        kd = k * (1.0 + (a - 1.0) * ka_ref[...])
        bon_s[d, pl.ds(c0, L), :] = _head_sum(r * kd * rk_ref[...], ones_bd, single_pass=True) * v
        if d == 0:
            g_s[pl.ds(c0, L), :] = _dot(_sigmoid(gd).astype(BF16), g2_ref[...])

        cum = _dot_exact_lhs(tri_ref[d], lw)
        e_pos = jnp.exp(cum)
        e_neg = jnp.exp(-cum)
        rh = r * e_pos
        kh = kd * e_neg
        bh = (kkn * a) * e_neg
        ah = -kkn * jnp.exp(cum - lw)
        last = L - 1 if d == 0 else 0
        gam = e_pos[last:last + 1, :]
        bhg = bh * gam
        khg = kh * gam

        chains = []
        for p in range(RW_PAIRS):
            cols = slice(p * LANES, (p + 1) * LANES)
            chains.append(dict(
                d=d, p=p, gam=gam[:, cols], v_st=stack(v[:, cols]),
                lhs=jnp.concatenate([stack(ah[:, cols]), stack(rh[:, cols])], axis=0),
                rhs=jnp.concatenate([stack(bh[:, cols]), stack(kh[:, cols])], axis=0),
                bk=jnp.concatenate([stack(bhg[:, cols]), stack(khg[:, cols])], axis=0)))
        return chains

    n_levels = int(math.log2(L))

    def state_free(chains):
        ri = lax.broadcasted_iota(jnp.int32, (L2, L2), 0)
        ci = lax.broadcasted_iota(jnp.int32, (L2, L2), 1)
        eye = jnp.where(ri == ci, 1.0, 0.0)
        rl = jnp.where(ri < L, ri, ri - L)
        cl = jnp.where(ci < L, ci, ci - L)
        strict = (cl < rl, cl > rl)
        incl = (cl <= rl, cl >= rl)
        level = ri ^ ci

        for ch in chains:
            ch['sc'] = _dot_nt(ch['lhs'], ch['rhs'])
        for ch in chains:
            sc, d = ch['sc'], ch['d']
            n_ab = jnp.where(strict[d], sc[0:L2, 0:L2], 0.0)
            m_ak = jnp.where(strict[d], sc[0:L2, L2:2 * L2], 0.0).astype(BF16)
            t_rb = jnp.where(incl[d], sc[L2:2 * L2, 0:L2], 0.0)
            t_rk = jnp.where(incl[d], sc[L2:2 * L2, L2:2 * L2], 0.0)
            ch['t_rbk'] = jnp.concatenate([t_rb, t_rk], axis=1).astype(BF16)
            ch['mv'] = _dot(m_ak, ch['v_st'])
            ch['t'] = eye + jnp.where(level == 1, n_ab, 0.0)
            ch['n_lv'] = [jnp.where(lax.shift_right_logical(level, lv) == 1, n_ab, 0.0).astype(BF16)
                          for lv in range(1, n_levels)]
            del ch['sc']
        for lv in range(n_levels - 1):
            for ch in chains:
                ch['tb'] = ch['t'].astype(BF16)
                ch['x'] = _dot(ch['n_lv'][lv], ch['tb']).astype(BF16)
            for ch in chains:
                ch['t'] = ch['t'] + _dot(ch['tb'], ch['x'])

    def state_step(chains, c0s):
        for ch in chains:
            ch['s_prev'] = s_s[ch['d'], ch['p']]
            ch['ar_s'] = _dot_nt(ch['lhs'], ch['s_prev'].astype(BF16))
        for ch in chains:
            u_rhs = ch['ar_s'][0:L2] + ch['mv']
            u_st = _dot(ch['t'].astype(BF16), u_rhs.astype(BF16)).astype(BF16)
            ch['uv'] = jnp.concatenate([u_st, ch['v_st']], axis=0)
        for ch in chains:
            y_st = ch['ar_s'][L2:2 * L2] + _dot(ch['t_rbk'], ch['uv'])
            ch['y'] = y_st[0:L] + y_st[L:L2]
            s_s[ch['d'], ch['p']] = ch['s_prev'] * ch['gam'] + _dot_tn(ch['uv'], ch['bk'])
        for d in range(2):
            yd_s[d, pl.ds(c0s[d], L), :] = jnp.concatenate([ch['y'] for ch in chains if ch['d'] == d], axis=1)

    steps = 2

    def body(i, carry):
        groups = []
        for j in range(steps):
            step = i * steps + j
            c0s = (pl.multiple_of(step * L, L), pl.multiple_of((n_ch - 1 - step) * L, L))
            groups.append((prep(c0s[0], 0) + prep(c0s[1], 1), c0s))
        state_free([ch for chains, _ in groups for ch in chains])
        for chains, c0s in groups:
            state_step(chains, c0s)
        return carry

    lax.fori_loop(0, n_ch // steps, body, 0)

    def epilogue(i, carry):
        c0 = pl.multiple_of(i * L, L)
        ones_bd = bd_ref[...]
        ys = yd_s[0, pl.ds(c0, L), :] + yd_s[1, pl.ds(c0, L), :]
        mu = _head_sum(ys, ones_bd) * (1.0 / RW_HEAD)
        yc = ys - mu
        var = _head_sum(yc * yc, ones_bd) * (1.0 / RW_HEAD)
        yn = yc * lax.rsqrt(var + RW_LN_EPS) * lnw_ref[...] + lnb_ref[...]
        out = (yn + (bon_s[0, pl.ds(c0, L), :] + bon_s[1, pl.ds(c0, L), :])) * g_s[pl.ds(c0, L), :]
        y_ref[pl.ds(c0, L), :] = out.astype(y_ref.dtype)
        return carry

    lax.fori_loop(0, n_ch, epilogue, 0, unroll=2)
    for d in range(2):
        for p in range(RW_PAIRS):
            s_pair = s_s[d, p]
            sfin_ref[d, 2 * p] = s_pair[0:RW_HEAD, 0:RW_HEAD]
            sfin_ref[d, 2 * p + 1] = s_pair[RW_HEAD:, RW_HEAD:]


def _rwkv_mixer(z, s0, p, n_batch, t_len):
    L = SCAN_CHUNK
    c2 = lambda b: (0, 0)
    c3 = lambda b: (0, 0, 0)
    st_spec = pl.BlockSpec((None, 2, RW_HEADS, RW_HEAD, RW_HEAD), lambda b: (b, 0, 0, 0, 0))
    return pl.pallas_call(
        functools.partial(_rwkv_kernel, t_len=t_len),
        grid=(n_batch,),
        in_specs=[
            pl.BlockSpec((t_len, RW_ZCOLS), lambda b: (b, 0)),
            st_spec,
            pl.BlockSpec((2, RW_ZCOLS), c2),
            pl.BlockSpec((1, D_RW), c2),
            pl.BlockSpec((1, D_RW), c2),
            pl.BlockSpec((1, D_RW), c2),
            pl.BlockSpec((RW_G_LORA, D_RW), c2),
            pl.BlockSpec((2, D_RW), c2),
            pl.BlockSpec((2, 2, LANES, D_RW), lambda b: (0, 0, 0, 0)),
            pl.BlockSpec((2, D_RW), c2),
            pl.BlockSpec((2, LANES, D_RW), c3),
            pl.BlockSpec((1, D_RW), c2),
            pl.BlockSpec((1, D_RW), c2),
            pl.BlockSpec((2, L, L), c3),
            pl.BlockSpec((LANES, LANES), c2),
        ],
        out_specs=[pl.BlockSpec((t_len, D_RW), lambda b: (b, 0)), st_spec],
        out_shape=[
            jax.ShapeDtypeStruct((n_batch * t_len, D_RW), Y_DTYPE),
            jax.ShapeDtypeStruct((n_batch, 2, RW_HEADS, RW_HEAD, RW_HEAD), F32),
        ],
        scratch_shapes=[
            pltpu.VMEM((2, L + 2 * HALO, RW_ZCOLS), F32),
            pltpu.VMEM((2, RW_PAIRS, LANES, LANES), F32),
            pltpu.VMEM((2, t_len, D_RW), F32),
            pltpu.VMEM((2, t_len, D_RW), F32),
            pltpu.VMEM((t_len, D_RW), F32),
        ],
        compiler_params=_ARB1,
    )(z, s0, p['mu'], p['kk'], p['ka'], p['rk'], p['g2'], p['w0'], p['w2'], p['a0'], p['a2'],
      p['ln_w'], p['ln_b'], p['tri'], p['ones_bd'])


def _mlstm2_kernel(z_ref, c0_ref, n0_ref, m0_ref, cw_ref, cb_ref, gb_ref, nw_ref, tri_ref, esel_ref, shift_ref,
                   ones_ref, y_ref, cfin_ref, nfin_ref, mfin_ref, c_s, n_s, m_s, hd_s, qk_s, gx_s, gr_s,
                   *, t_len):
    L = ML_SCAN
    n_ch = t_len // L
    c_s[...] = c0_ref[...]
    n_s[...] = n0_ref[...]
    m_s[...] = m0_ref[...]
    q_scale = ML_DK ** -0.5
    g_lo = 4 * D_ML
    n_real = 4 * ML_HEADS

    def conv_chunk(i, carry):
        c0 = pl.multiple_of(i * L, L)
        ext = _load_ext(z_ref, c0, L, t_len, 0, 2 * D_ML, raw=True)
        taps = _dot(shift_ref[...], ext)
        qk = cb_ref[...] + ext[HALO:HALO + L].astype(F32) * cw_ref[CONV_LEFT:CONV_LEFT + 1, :]
        for n, j in enumerate(jj for jj in range(CONV_W) if jj != CONV_LEFT):
            qk = qk + taps[n * L:(n + 1) * L] * cw_ref[j:j + 1, :]
        qk = _silu(qk)
        qk_s[pl.ds(c0, L), 0:D_ML] = (qk[:, :D_ML] * q_scale).astype(BF16)
        qk_s[pl.ds(c0, L), D_ML:2 * D_ML] = qk[:, D_ML:].astype(BF16)

        gts = z_ref[pl.ds(c0, L), g_lo:g_lo + LANES].astype(F32) + gb_ref[...]
        x16 = gts.T[0:n_real, :]
        lf_hi, lf_lo = _split2(-_softplus(-x16))
        lane = lax.broadcasted_iota(jnp.int32, (8, L), 1)
        for d in range(2):
            b16 = _dot_nt(lf_hi, tri_ref[d]) + _dot_nt(lf_lo, tri_ref[d])
            x8 = x16[8 * d:8 * d + 8, :]
            b8 = b16[8 * d:8 * d + 8, :]
            g8 = pltpu.roll(x8, ML_HEADS, axis=0) - b8
            cm8 = g8
            step = 1
            while step < L:
                if d == 0:
                    sh = jnp.where(lane >= step, pltpu.roll(cm8, step, axis=1), -jnp.inf)
                else:
                    sh = jnp.where(lane < L - step, pltpu.roll(cm8, L - step, axis=1), -jnp.inf)
                cm8 = jnp.maximum(cm8, sh)
                step *= 2
            x32 = jnp.concatenate([g8, b8, cm8, jnp.zeros((8, L), F32)], axis=0)
            hi = x32.astype(BF16)
            r1 = x32 - hi.astype(F32)
            mid = r1.astype(BF16)
            lo = (r1 - mid.astype(F32)).astype(BF16)
            gx_s[d, i] = jnp.concatenate([hi, mid, lo], axis=0)
            gr_s[d, i] = g8
        return carry

    lax.fori_loop(0, n_ch, conv_chunk, 0, unroll=2)

    def prep(ci, d):
        c0 = pl.multiple_of(ci * L, L)
        q = qk_s[pl.ds(c0, L), 0:D_ML]
        k = qk_s[pl.ds(c0, L), D_ML:2 * D_ML]
        v = z_ref[pl.ds(c0, L), 2 * D_ML:3 * D_ML]
        g8 = gr_s[d, ci]
        cb = _dot_tn(gx_s[d, ci], esel_ref[...])
        ri = lax.broadcasted_iota(jnp.int32, (L, L), 0)
        ci = lax.broadcasted_iota(jnp.int32, (L, L), 1)
        causal = (ci <= ri) if d == 0 else (ci >= ri)
        last = L - 1 if d == 0 else 0
        ones = ones_ref[...]
        chains = []
        for h in range(ML_HEADS):
            cols = slice(h * ML_DK, (h + 1) * ML_DK)
            chains.append(dict(
                d=d, h=h, last=last, causal=causal,
                g_row=g8[ML_HEADS + h:ML_HEADS + h + 1, :],
                g_b=cb[:, cols], b_b=cb[:, D_ML + h * ML_DK:D_ML + (h + 1) * ML_DK],
                cm_b=cb[:, 2 * D_ML + h * ML_DK:2 * D_ML + (h + 1) * ML_DK],
                qb=q[:, cols], kb=k[:, cols],
                v=v[:, cols].astype(F32), v_ones=jnp.concatenate([v[:, cols].astype(BF16), ones], axis=1)))
        return chains

    def body(i, carry):
        c0s = (pl.multiple_of(i * L, L), pl.multiple_of((n_ch - 1 - i) * L, L))
        chains = prep(i, 0) + prep(n_ch - 1 - i, 1)
        for ch in chains:
            d, h = ch['d'], ch['h']
            ch['qk'] = _dot_nt(ch['qb'], ch['kb'])
            ch['c_prev'] = c_s[d, h]
            ch['n_prev'] = n_s[d, h]
            ch['m_prev'] = m_s[d, h:h + 1, :]
            ch['qcn'] = _dot(ch['qb'], jnp.concatenate([ch['c_prev'].astype(BF16),
                                                        ch['n_prev'].astype(BF16)], axis=1))
        for ch in chains:
            mu = jnp.maximum(ch['m_prev'], ch['cm_b'])
            ch['mu'] = mu
            ch['s'] = ch['qk'] * jnp.exp(jnp.where(ch['causal'], ch['g_row'] - mu, -jnp.inf))
            ch['w_inter'] = jnp.exp(ch['m_prev'] - mu)
        for ch in chains:
            last = ch['last']
            ch['sv'] = _dot(ch['s'].astype(BF16), ch['v_ones'])
            mu_last = ch['mu'][last:last + 1, :]
            ch['m_new'] = ch['b_b'][last:last + 1, :] + mu_last
            ch['dec'] = jnp.exp(ch['m_prev'] - mu_last)
            w_s = jnp.exp(ch['g_b'] - mu_last)
            ch['upd'] = _dot_tn(ch['kb'], jnp.concatenate([(w_s * ch['v']).astype(BF16),
                                                           w_s.astype(BF16)], axis=1))
        for ch in chains:
            d, h = ch['d'], ch['h']
            num = ch['sv'][:, :ML_DK] + ch['w_inter'] * ch['qcn'][:, :ML_DK]
            den = ch['sv'][:, ML_DK:] + ch['w_inter'] * ch['qcn'][:, ML_DK:]
            ch['out'] = num / jnp.maximum(jnp.abs(den), jnp.exp(-(ch['b_b'] + ch['mu'])))
            c_s[d, h] = ch['dec'] * ch['c_prev'] + ch['upd'][:, :ML_DK]
            n_s[d, h] = ch['dec'] * ch['n_prev'] + ch['upd'][:, ML_DK:]
            m_s[d, h:h + 1, :] = ch['m_new']
        for d in range(2):
            hd_s[d, pl.ds(c0s[d], L), :] = jnp.concatenate([ch['out'] for ch in chains if ch['d'] == d], axis=1)
        return carry

    lax.fori_loop(0, n_ch, body, 0, unroll=2)

    def epilogue(i, carry):
        c0 = pl.multiple_of(i * L, L)
        ones = ones_ref[...]
        hs = hd_s[0, pl.ds(c0, L), :] + hd_s[1, pl.ds(c0, L), :]
        o = z_ref[pl.ds(c0, L), 3 * D_ML:4 * D_ML].astype(F32)
        mu = _head_sum(hs, ones) * (1.0 / ML_DK)
        xc = hs - mu
        var = _head_sum(xc * xc, ones) * (1.0 / ML_DK)
        y = xc * lax.rsqrt(var + NORM_EPS) * nw_ref[...] * _sigmoid(o)
        y_ref[pl.ds(c0, L), :] = y.astype(y_ref.dtype)
        return carry

    lax.fori_loop(0, n_ch, epilogue, 0, unroll=2)
    cfin_ref[...] = c_s[...]
    nfin_ref[...] = n_s[...]
    mfin_ref[...] = m_s[...]


def _mlstm2_mixer(z, c0, n0, m0, p, n_batch, t_len):
    assert z.dtype == BF16
    L = ML_SCAN
    c2 = lambda b: (0, 0)
    c3 = lambda b: (0, 0, 0)
    c_spec = pl.BlockSpec((None, 2, ML_HEADS, ML_DK, ML_DK), lambda b: (b, 0, 0, 0, 0))
    m_spec = pl.BlockSpec((None, 2, ML_HEADS, ML_DK), lambda b: (b, 0, 0, 0))
    return pl.pallas_call(
        functools.partial(_mlstm2_kernel, t_len=t_len),
        grid=(n_batch,),
        in_specs=[
            pl.BlockSpec((t_len, ML_ZCOLS), lambda b: (b, 0)),
            c_spec, c_spec, m_spec,
            pl.BlockSpec((CONV_W, 2 * D_ML), c2),
            pl.BlockSpec((1, 2 * D_ML), c2),
            pl.BlockSpec((1, LANES), c2),
            pl.BlockSpec((1, D_ML), c2),
            pl.BlockSpec((2, L, L), c3),
            pl.BlockSpec(p['esel'].shape, c2),
            pl.BlockSpec(p['shift'].shape, c2),
            pl.BlockSpec((L, LANES), c2),
        ],
        out_specs=[pl.BlockSpec((t_len, D_ML), lambda b: (b, 0)), c_spec, c_spec, m_spec],
        out_shape=[
            jax.ShapeDtypeStruct((n_batch * t_len, D_ML), Y_DTYPE),
            jax.ShapeDtypeStruct((n_batch, 2, ML_HEADS, ML_DK, ML_DK), F32),
            jax.ShapeDtypeStruct((n_batch, 2, ML_HEADS, ML_DK, ML_DK), F32),
            jax.ShapeDtypeStruct((n_batch, 2, ML_HEADS, ML_DK), F32),
        ],
        scratch_shapes=[
            pltpu.VMEM((2, ML_HEADS, ML_DK, ML_DK), F32),
            pltpu.VMEM((2, ML_HEADS, ML_DK, ML_DK), F32),
            pltpu.VMEM((2, ML_HEADS, ML_DK), F32),
            pltpu.VMEM((2, t_len, D_ML), F32),
            pltpu.VMEM((t_len, 2 * D_ML), BF16),
            pltpu.VMEM((2, t_len // L, 96, L), BF16),
            pltpu.VMEM((2, t_len // L, 8, L), F32),
        ],
        compiler_params=_ARB1,
    )(z, c0, n0, m0, p['conv_w'], p['conv_b'], p['gate_b'], p['norm_w'], p['tri'], p['esel'], p['shift'],
      p['ones'])


def _conv_shift(n):
    taps = jnp.array([j for j in range(CONV_W) if j != CONV_LEFT])
    t = jnp.arange((CONV_W - 1) * n)
    src = HALO + t % n + taps[t // n] - CONV_LEFT
    return (src[:, None] == jnp.arange(n + 2 * HALO)[None, :]).astype(BF16)


def _mlstm_consts():
    L = ML_SCAN
    lower = jnp.tril(jnp.ones((L, L), F32))
    tri = jnp.stack([lower, lower.T]).astype(BF16)
    rows = jnp.arange(32)
    blk = jnp.arange(3 * D_ML) // ML_DK
    src_row = 8 * (blk // ML_HEADS) + ML_HEADS + blk % ML_HEADS
    esel = (rows[:, None] == src_row[None, :]).astype(BF16)
    esel = jnp.concatenate([esel, esel, esel], axis=0)
    return dict(tri=tri, esel=esel, shift=_conv_shift(L), ones=jnp.ones((L, LANES), BF16))


def _block_diag(w):
    n, c, d = w.shape
    return jnp.einsum('ncd,nm->ncmd', w, jnp.eye(n, dtype=w.dtype)).reshape(n * c, n * d)


def _scan_tri():
    L = SCAN_CHUNK
    lower = jnp.tril(jnp.ones((L, L), F32))
    return jnp.stack([lower, lower.T]).astype(BF16)


def _prep_even(ev_w_in, pool_w, pool_scale, rg_conv_w, rg_conv_b, rg_wa, rg_ba, rg_wx, rg_bx, rg_lam, ev_w_out, j):
    wg = jnp.concatenate([_block_diag(rg_wa[j, 0]), _block_diag(rg_wx[j, 0]),
                          _block_diag(rg_wa[j, 1]), _block_diag(rg_wx[j, 1])], axis=1)
    bg = jnp.concatenate([rg_ba[j, 0], rg_bx[j, 0], rg_ba[j, 1], rg_bx[j, 1]]).reshape(1, 4 * D_RG)
    return dict(
        w_in=[ev_w_in[j].astype(BF16)],
        w_out=ev_w_out[j].astype(BF16),
        pool_w=pool_w[j].astype(BF16),
        pool_scale=pool_scale[j].reshape(1, D_POOL),
        conv_w=rg_conv_w[j],
        conv_b=rg_conv_b[j].reshape(1, D_RG),
        wg=wg.astype(BF16),
        bg=bg,
        lam=rg_lam[j],
    )


def _pad_cols(w, cuts, widths):
    pieces = jnp.split(w, cuts, axis=-1)
    out = []
    for piece, width in zip(pieces, widths):
        extra = width - piece.shape[-1]
        if extra:
            piece = jnp.pad(piece, [(0, 0)] * (piece.ndim - 1) + [(0, extra)])
        out.append(piece)
    return jnp.concatenate(out, axis=-1)


def _prep_odd(od_w_in, rw_mu, rw_w0, rw_w2, rw_a0, rw_a2, rw_kk, rw_ka, rw_rk, rw_g2, rw_ln_w, rw_ln_b,
              ml_conv_w, ml_conv_b, ml_bi, ml_bf, ml_norm_w, od_w_out, j):
    rw_cols = 3 * D_RW + RW_G_LORA + RW_W_LORA + RW_A_LORA
    base = 3 * D_RW + RW_G_LORA
    cuts = [base, base + RW_W_LORA]
    widths = [base, LANES, LANES]
    w_rw = _pad_cols(od_w_in[j][:, :rw_cols], cuts, widths)
    w_ml = _pad_cols(od_w_in[j][:, rw_cols:], [4 * D_ML], [4 * D_ML, LANES])
    pad_rows = lambda w: jnp.pad(w, ((0, 0), (0, LANES - w.shape[1]), (0, 0)))
    gate_b = jnp.concatenate([ml_bi[j, 0], ml_bf[j, 0], ml_bi[j, 1], ml_bf[j, 1]])
    gate_b = jnp.pad(gate_b, (0, LANES - gate_b.shape[0])).reshape(1, LANES)
    hm = jnp.arange(LANES) // RW_HEAD
    rw = dict(
        mu=_pad_cols(rw_mu[j], cuts, widths),
        kk=rw_kk[j].reshape(1, D_RW), ka=rw_ka[j].reshape(1, D_RW), rk=rw_rk[j].reshape(1, D_RW),
        g2=rw_g2[j].astype(BF16),
        w0=rw_w0[j], w2=jnp.stack(_split2(pad_rows(rw_w2[j])), axis=1),
        a0=rw_a0[j], a2=pad_rows(rw_a2[j]).astype(BF16),
        ln_w=rw_ln_w[j].reshape(1, D_RW), ln_b=rw_ln_b[j].reshape(1, D_RW),
        tri=_scan_tri(),
        ones_bd=(hm[:, None] == hm[None, :]).astype(BF16),
    )
    ml = dict(
        conv_w=ml_conv_w[j], conv_b=ml_conv_b[j].reshape(1, 2 * D_ML),
        gate_b=gate_b, norm_w=ml_norm_w[j].reshape(1, D_ML), **_mlstm_consts(),
    )
    return dict(w_in=[w_rw.astype(BF16), w_ml.astype(BF16)], w_out=od_w_out[j].astype(BF16), rw=rw, ml=ml)


def _trunk(x, mod, rows_per_mod, mod_base, grid, states, params, n_batch, t_len):
    ev, od, dense = params
    rg0, rw0, c0, n0, m0 = states
    x = x.reshape(n_batch * t_len, D_MODEL)
    spec = (rows_per_mod, mod_base)

    (z,) = _in_proj(x, mod[0], *spec, dense['norm1_w'][0], ev['w_in'])
    y_ev, rg_f = _even_mixer(z, rg0, ev, n_batch, t_len, grid)
    x = _out_mlp([y_ev], x, mod[0], *spec, dense['norm2_w'][0], ev['w_out'],
                 dense['mlp_w1'], dense['mlp_w2'], 0, dense['final_norm_w'], False)

    z_rw, z_ml = _in_proj(x, mod[1], *spec, dense['norm1_w'][1], od['w_in'])
    y_rw, s_f = _rwkv_mixer(z_rw, rw0, od['rw'], n_batch, t_len)
    y_ml, c_f, n_f, m_f = _mlstm2_mixer(z_ml, c0, n0, m0, od['ml'], n_batch, t_len)
    y = _out_mlp([y_rw, y_ml], x, mod[1], *spec, dense['norm2_w'][1], od['w_out'],
                 dense['mlp_w1'], dense['mlp_w2'], 1, dense['final_norm_w'], True)
    return y.reshape(n_batch, t_len, D_MODEL), (rg_f, s_f, c_f, n_f, m_f)


def kernel(x_prompt, x_sample, state_rglru, state_rwkv, state_mlstm_C, state_mlstm_n, state_mlstm_m,
           c, c_ctx, norm1_w, norm2_w, w_mod, b_mod, mlp_w1, mlp_w2, final_norm_w,
           ev_w_in, pool_w, pool_scale, rg_conv_w, rg_conv_b, rg_wa, rg_ba, rg_wx, rg_bx, rg_lam, ev_w_out,
           od_w_in, rw_mu, rw_w0, rw_w2, rw_a0, rw_a2, rw_kk, rw_ka, rw_rk, rw_g2, rw_ln_w, rw_ln_b,
           ml_conv_w, ml_conv_b, ml_bi, ml_bf, ml_norm_w, od_w_out):
    bp, tp, _ = x_prompt.shape
    bs, ts, _ = x_sample.shape

    n_cond = 16
    cond = jnp.concatenate([c_ctx[None, :], c, jnp.zeros((n_cond - 1 - bs, D_MODEL), F32)], axis=0)
    mod = _modulation(cond, w_mod, b_mod)

    ev = _prep_even(ev_w_in, pool_w, pool_scale, rg_conv_w, rg_conv_b, rg_wa, rg_ba, rg_wx, rg_bx, rg_lam,
                    ev_w_out, 0)
    od = _prep_odd(od_w_in, rw_mu, rw_w0, rw_w2, rw_a0, rw_a2, rw_kk, rw_ka, rw_rk, rw_g2, rw_ln_w, rw_ln_b,
                   ml_conv_w, ml_conv_b, ml_bi, ml_bf, ml_norm_w, od_w_out, 0)
    dense = dict(norm1_w=norm1_w, norm2_w=norm2_w, mlp_w1=mlp_w1.astype(BF16), mlp_w2=mlp_w2.astype(BF16),
                 final_norm_w=final_norm_w)
    params = (ev, od, dense)

    zero_states = (
        jnp.zeros((bp, 2, D_RG), F32),
        jnp.zeros((bp, 2, RW_HEADS, RW_HEAD, RW_HEAD), F32),
        jnp.zeros((bp, 2, ML_HEADS, ML_DK, ML_DK), F32),
        jnp.zeros((bp, 2, ML_HEADS, ML_DK, ML_DK), F32),
        jnp.zeros((bp, 2, ML_HEADS, ML_DK), F32),
    )
    y_prompt, (rg_f, s_f, c_f, n_f, m_f) = _trunk(
        x_prompt, mod, bp * tp, 0, False, zero_states, params, bp, tp)

    sample_states = (
        state_rglru[:, 0],
        state_rwkv[:, 0],
        state_mlstm_C[:, 0],
        jnp.broadcast_to(state_mlstm_n[:, 0][..., None], (bs, 2, ML_HEADS, ML_DK, ML_DK)),
        jnp.broadcast_to(state_mlstm_m[:, 0][..., None], (bs, 2, ML_HEADS, ML_DK)),
    )
    y_sample, _ = _trunk(x_sample, mod, ts, 1, True, sample_states, params, bs, ts)

    return (y_prompt, y_sample,
            rg_f[:, None],
            s_f[:, None],
            c_f[:, None],
            n_f[:, None, :, :, :, 0],
            m_f[:, None, :, :, 0])
```

```python
import functools
import math

import jax
import jax.numpy as jnp
from jax import lax
from jax.experimental import pallas as pl
from jax.experimental.pallas import tpu as pltpu

F32 = jnp.float32
BF16 = jnp.bfloat16

D_MODEL = 1024
DEPTH = 2
GRID_W = 64
D_FF = 4 * D_MODEL
NORM_EPS = 1e-6
N_MOD = 6

D_POOL = D_MODEL // 2
POOL_WINDOWS = (2, 4, 8, 16)
POOL_GW = D_POOL // len(POOL_WINDOWS)
D_RG = D_MODEL // 2
RG_BLOCKS = 8
RG_C = 8.0
CONV_W = 4
CONV_LEFT = 2
EV_COLS = D_POOL + 2 * D_RG

D_RW = D_MODEL // 2
RW_HEAD = 64
RW_HEADS = D_RW // RW_HEAD
RW_PAIRS = RW_HEADS // 2
RW_W_LORA = 64
RW_A_LORA = 64
RW_G_LORA = 128
RW_LN_EPS = 64e-5
RW_ZCOLS = 3 * D_RW + RW_G_LORA + 128 + 128
D_ML = D_MODEL // 2
ML_HEADS = 4
ML_DK = D_ML // ML_HEADS
ML_ZCOLS = 4 * D_ML + 128

LANES = 128
HALO = 16
Z_DTYPE = BF16
Y_DTYPE = BF16
ROW_BLOCK = 512
SCAN_CHUNK = 64
ML_SCAN = 128
RG_CHUNK = 256
VMEM_LIMIT = 56 * 1024 * 1024

_ARB1 = pltpu.CompilerParams(dimension_semantics=("arbitrary",), vmem_limit_bytes=VMEM_LIMIT)
_ARB2 = pltpu.CompilerParams(dimension_semantics=("arbitrary", "arbitrary"), vmem_limit_bytes=VMEM_LIMIT)


def _dot(a, b):
    return jnp.dot(a, b, preferred_element_type=F32)


def _dot_nt(a, b):
    return lax.dot_general(a, b, (((1,), (1,)), ((), ())), preferred_element_type=F32)


def _dot_tn(a, b):
    return lax.dot_general(a, b, (((0,), (0,)), ((), ())), preferred_element_type=F32)


def _split2(x):
    hi = x.astype(BF16)
    lo = (x - hi.astype(F32)).astype(BF16)
    return hi, lo


def _dot_f32(a, b):
    ah, al = _split2(a)
    bh, bl = _split2(b)
    return _dot(ah, bh) + (_dot(ah, bl) + _dot(al, bh))


def _dot_exact_lhs(m_bf16, x):
    hi, lo = _split2(x)
    return _dot(m_bf16, hi) + _dot(m_bf16, lo)


def _dot_exact_rhs(x, m_bf16):
    hi, lo = _split2(x)
    return _dot(hi, m_bf16) + _dot(lo, m_bf16)


def _softplus(x):
    return jnp.maximum(x, 0.0) + jnp.log1p(jnp.exp(-jnp.abs(x)))


def _sigmoid(x):
    return jax.nn.sigmoid(x)


def _silu(x):
    return x * jax.nn.sigmoid(x)


def _gelu_tanh(x):
    c = math.sqrt(2.0 / math.pi)
    return x * (0.5 * (1.0 + jnp.tanh(c * (x + 0.044715 * (x * x * x)))))


def _rms_norm(x, w):
    ms = jnp.mean(x * x, axis=-1, keepdims=True)
    return x * lax.rsqrt(ms + NORM_EPS) * w


def _load_ext(z_ref, c0, n, t_len, lo, hi, raw=False):
    dt = z_ref.dtype if raw else F32
    cur = z_ref[pl.ds(c0, n), lo:hi].astype(dt)
    pb = pl.multiple_of(jnp.maximum(c0 - HALO, 0), HALO)
    pa = pl.multiple_of(jnp.minimum(c0 + n, t_len - HALO), HALO)
    before = z_ref[pl.ds(pb, HALO), lo:hi].astype(dt)
    after = z_ref[pl.ds(pa, HALO), lo:hi].astype(dt)
    before = jnp.where(c0 > 0, before, jnp.zeros_like(before))
    after = jnp.where(c0 + n < t_len, after, jnp.zeros_like(after))
    return jnp.concatenate([before, cur, after], axis=0)


def _head_sum(x, ones_bd, single_pass=False):
    n = x.shape[0]
    groups = x.shape[1] // LANES
    xs = jnp.concatenate([x[:, g * LANES:(g + 1) * LANES] for g in range(groups)], axis=0)
    s = _dot(xs.astype(BF16), ones_bd) if single_pass else _dot_exact_rhs(xs, ones_bd)
    return jnp.concatenate([s[g * n:(g + 1) * n] for g in range(groups)], axis=1)


def _mod_kernel(c_ref, w_ref, b_ref, o_ref):
    s = _silu(c_ref[...])
    o_ref[...] = _dot_f32(s, w_ref[...]) + b_ref[...]


def _modulation(cond, w_mod, b_mod):
    rows = cond.shape[0]
    tn = 1536
    out = pl.pallas_call(
        _mod_kernel,
        grid=(DEPTH, N_MOD * D_MODEL // tn),
        in_specs=[
            pl.BlockSpec((rows, D_MODEL), lambda l, j: (0, 0)),
            pl.BlockSpec((None, D_MODEL, tn), lambda l, j: (l, 0, j)),
            pl.BlockSpec((None, 1, tn), lambda l, j: (l, 0, j)),
        ],
        out_specs=pl.BlockSpec((None, rows, tn), lambda l, j: (l, 0, j)),
        out_shape=jax.ShapeDtypeStruct((DEPTH, rows, N_MOD * D_MODEL), F32),
        compiler_params=_ARB2,
    )(cond, w_mod, b_mod.reshape(DEPTH, 1, N_MOD * D_MODEL))
    return out.reshape(DEPTH, rows, N_MOD, D_MODEL)


def _inproj_kernel(*refs, n_out, col_chunk):
    x_ref, mod_ref, nw_ref = refs[:3]
    w_refs = refs[3:3 + n_out]
    z_refs = refs[3 + n_out:]
    h = _rms_norm(x_ref[...], nw_ref[...])
    h = h * (1.0 + mod_ref[1:2, :]) + mod_ref[0:1, :]
    hb = h.astype(BF16)
    for w_ref, z_ref in zip(w_refs, z_refs):
        n = w_ref.shape[1]
        for c in range(0, n, col_chunk):
            e = min(c + col_chunk, n)
            z_ref[:, c:e] = _dot(hb, w_ref[:, c:e]).astype(z_ref.dtype)


def _in_proj(x, mod, rows_per_mod, mod_base, norm_w, weights):
    m = x.shape[0]
    tm = ROW_BLOCK
    n_out = len(weights)
    const = lambda i: (0, 0)
    in_specs = [
        pl.BlockSpec((tm, D_MODEL), lambda i: (i, 0)),
        pl.BlockSpec((None, N_MOD, D_MODEL), lambda i: (mod_base + (i * tm) // rows_per_mod, 0, 0)),
        pl.BlockSpec((1, D_MODEL), const),
    ] + [pl.BlockSpec(w.shape, const, pipeline_mode=pl.Buffered(1)) for w in weights]
    out_specs = [pl.BlockSpec((tm, w.shape[1]), lambda i: (i, 0)) for w in weights]
    out_shape = [jax.ShapeDtypeStruct((m, w.shape[1]), Z_DTYPE) for w in weights]
    return pl.pallas_call(
        functools.partial(_inproj_kernel, n_out=n_out, col_chunk=512),
        grid=(m // tm,),
        in_specs=in_specs,
        out_specs=out_specs,
        out_shape=out_shape,
        compiler_params=_ARB1,
    )(x, mod, norm_w.reshape(1, D_MODEL), *weights)


def _outmlp_kernel(*refs, n_y, final, ff_chunk):
    y_refs = refs[:n_y]
    x_ref, mod_ref, nw_ref, wo_ref, w1_ref, w2_ref, fw_ref, o_ref = refs[n_y:]
    y = None
    row = 0
    for y_ref in y_refs:
        k = y_ref.shape[1]
        part = _dot(y_ref[...], wo_ref[row:row + k, :])
        y = part if y is None else y + part
        row += k
    x = x_ref[...] + mod_ref[2:3, :] * y
    h = _rms_norm(x, nw_ref[...]) * (1.0 + mod_ref[4:5, :]) + mod_ref[3:4, :]
    hb = h.astype(BF16)
    acc = None
    for c in range(0, D_FF, ff_chunk):
        u = jnp.maximum(_dot(hb, w1_ref[:, c:c + ff_chunk]), 0.0)
        part = _dot((u * u).astype(BF16), w2_ref[c:c + ff_chunk, :])
        acc = part if acc is None else acc + part
    x = x + mod_ref[5:6, :] * acc
    if final:
        x = _rms_norm(x, fw_ref[...])
    o_ref[...] = x


def _out_mlp(ys, x, mod, rows_per_mod, mod_base, norm_w, w_out, w1, w2, layer, final_w, final):
    m = x.shape[0]
    tm = ROW_BLOCK
    const = lambda i: (0, 0)
    single = dict(pipeline_mode=pl.Buffered(1))
    pick = lambda i: (layer, 0, 0)
    in_specs = [pl.BlockSpec((tm, y.shape[1]), lambda i: (i, 0)) for y in ys] + [
        pl.BlockSpec((tm, D_MODEL), lambda i: (i, 0)),
        pl.BlockSpec((None, N_MOD, D_MODEL), lambda i: (mod_base + (i * tm) // rows_per_mod, 0, 0)),
        pl.BlockSpec((1, D_MODEL), const),
        pl.BlockSpec(w_out.shape, const, **single),
        pl.BlockSpec((None,) + w1.shape[1:], pick, **single),
        pl.BlockSpec((None,) + w2.shape[1:], pick, **single),
        pl.BlockSpec((1, D_MODEL), const),
    ]
    return pl.pallas_call(
        functools.partial(_outmlp_kernel, n_y=len(ys), final=final, ff_chunk=1024),
        grid=(m // tm,),
        in_specs=in_specs,
        out_specs=pl.BlockSpec((tm, D_MODEL), lambda i: (i, 0)),
        out_shape=jax.ShapeDtypeStruct((m, D_MODEL), F32),
        compiler_params=_ARB1,
    )(*ys, x, mod, norm_w.reshape(1, D_MODEL), w_out, w1, w2, final_w.reshape(1, D_MODEL))


def _even_kernel(z_ref, h0_ref, pw_ref, ps_ref, cw_ref, cb_ref, wg_ref, bg_ref, lam_ref, band_ref,
                 y_ref, hfin_ref, cbuf, a_s, b_s, h_s, hf_s, *, t_len, grid):
    seg = GRID_W if grid else t_len
    n_seg = t_len // seg
    n_groups = len(POOL_WINDOWS)

    pos = lax.broadcasted_iota(jnp.int32, (seg, POOL_GW), 0)
    offs = [(w // 2, w - 1 - w // 2) for w in POOL_WINDOWS]
    gcols = [slice(g * POOL_GW, (g + 1) * POOL_GW) for g in range(n_groups)]

    def grid_row(rr, cols):
        valid = jnp.logical_and(rr >= 0, rr < n_seg)
        src = pl.multiple_of(jnp.clip(rr, 0, n_seg - 1) * seg, seg)
        return jnp.where(valid, z_ref[pl.ds(src, seg), cols].astype(F32), 0.0)

    def pool_rows(rows, run):
        items = []
        for r in rows:
            base = r * seg if isinstance(r, int) else pl.multiple_of(r * seg, seg)
            new_run = []
            for g, (lo_off, hi_off) in enumerate(offs):
                cols = gcols[g]
                if grid:
                    s1 = run[g] + grid_row(r + hi_off, cols) - grid_row(r - lo_off - 1, cols)
                    new_run.append(s1)
                    cnt_r = jnp.minimum(r + hi_off, n_seg - 1) - jnp.maximum(r - lo_off, 0) + 1
                    m1 = s1 / cnt_r.astype(F32)
                    hi = m1.astype(BF16)
                    r1 = m1 - hi.astype(F32)
                    mid = r1.astype(BF16)
                    parts = jnp.concatenate([hi, mid, (r1 - mid.astype(F32)).astype(BF16)], axis=0)
                else:
                    parts = z_ref[pl.ds(base, seg), cols]
                items.append(dict(g=g, base=base, parts=parts))
            run = new_run
        for it in items:
            it['s2'] = _dot(band_ref[it['g']], it['parts'])
        for it in items:
            lo_off, hi_off = offs[it['g']]
            cnt_c = jnp.minimum(pos + hi_off, seg - 1) - jnp.maximum(pos - lo_off, 0) + 1
            xg = z_ref[pl.ds(it['base'], seg), gcols[it['g']]].astype(F32)
            it['d'] = (it['s2'] / cnt_c.astype(F32) - xg).astype(BF16)
        for it in items:
            it['y'] = _dot(it['d'], pw_ref[it['g']])
        for it in items:
            cols = gcols[it['g']]
            y_ref[pl.ds(it['base'], seg), cols] = (it['y'] * ps_ref[:, cols]).astype(y_ref.dtype)
        return tuple(run) if grid else None

    if grid:
        init = []
        for g, (lo_off, hi_off) in enumerate(offs):
            s1 = jnp.zeros((seg, POOL_GW), F32)
            for o in range(min(hi_off, n_seg)):
                s1 = s1 + z_ref[o * seg:(o + 1) * seg, gcols[g]].astype(F32)
            init.append(s1)
        per_trip = 8
        lax.fori_loop(0, n_seg // per_trip,
                      lambda i, run: pool_rows([i * per_trip + j for j in range(per_trip)], run), tuple(init))
    else:
        pool_rows([0], None)

    ch = RG_CHUNK
    n_ch = t_len // ch
    rg_lo, rg_hi = D_POOL, D_POOL + D_RG

    def gates(c0, d):
        cbuf[...] = _load_ext(z_ref, c0, ch, t_len, rg_lo, rg_hi)
        xc = cb_ref[...]
        for j in range(CONV_W):
            off = HALO - CONV_LEFT + j
            xc = xc + cbuf[off:off + ch, :] * cw_ref[j:j + 1, :]
        g = _dot(xc.astype(BF16), wg_ref[:, d * 2 * D_RG:(d + 1) * 2 * D_RG]) \
            + bg_ref[:, d * 2 * D_RG:(d + 1) * 2 * D_RG]
        r = _sigmoid(g[:, :D_RG])
        i = _sigmoid(g[:, D_RG:])
        log_a = (-RG_C) * r * _softplus(-lam_ref[d:d + 1, :])
        a = jnp.exp(log_a)
        a_s[...] = a
        one_minus_a2 = jnp.tanh(-log_a) * (a * a + 1.0)
        root = jnp.where(one_minus_a2 > 0.0, one_minus_a2 * lax.rsqrt(one_minus_a2), 0.0)
        b_s[...] = root * (i * xc)

    def scan_rows(h, reverse):
        def row(t, h):
            tt = (ch - 1 - t) if reverse else t
            h = a_s[pl.ds(tt, 1), :] * h + b_s[pl.ds(tt, 1), :]
            h_s[pl.ds(tt, 1), :] = h
            return h
        return lax.fori_loop(0, ch, row, h, unroll=8)

    def fwd_body(i, h):
        c0 = pl.multiple_of(i * ch, ch)
        gates(c0, 0)
        h = scan_rows(h, False)
        hf_s[pl.ds(c0, ch), :] = h_s[...]
        return h

    h_fwd = lax.fori_loop(0, n_ch, fwd_body, h0_ref[0:1, :])

    def bwd_body(i, h):
        c0 = pl.multiple_of((n_ch - 1 - i) * ch, ch)
        gates(c0, 1)
        h = scan_rows(h, True)
        u_gate = z_ref[pl.ds(c0, ch), rg_hi:rg_hi + D_RG].astype(F32)
        y = (hf_s[pl.ds(c0, ch), :] + h_s[...]) * _gelu_tanh(u_gate)
        y_ref[pl.ds(c0, ch), D_POOL:D_POOL + D_RG] = y.astype(y_ref.dtype)
        return h

    h_bwd = lax.fori_loop(0, n_ch, bwd_body, h0_ref[1:2, :])
    hfin_ref[0:1, :] = h_fwd
    hfin_ref[1:2, :] = h_bwd


def _even_mixer(z, h0, p, n_batch, t_len, grid):
    assert z.dtype == BF16
    seg = GRID_W if grid else t_len
    const2 = lambda b: (0, 0)
    const3 = lambda b: (0, 0, 0)
    idx = jnp.arange(seg)
    delta = idx[None, :] - idx[:, None]
    band = jnp.stack([jnp.logical_and(delta >= -(w // 2), delta <= w - 1 - w // 2) for w in POOL_WINDOWS])
    band = jnp.tile(band.astype(BF16), (1, 1, 3 if grid else 1))
    return pl.pallas_call(
        functools.partial(_even_kernel, t_len=t_len, grid=grid),
        grid=(n_batch,),
        in_specs=[
            pl.BlockSpec((t_len, EV_COLS), lambda b: (b, 0)),
            pl.BlockSpec((None, 2, D_RG), lambda b: (b, 0, 0)),
            pl.BlockSpec(p['pool_w'].shape, const3),
            pl.BlockSpec((1, D_POOL), const2),
            pl.BlockSpec((CONV_W, D_RG), const2),
            pl.BlockSpec((1, D_RG), const2),
            pl.BlockSpec((D_RG, 4 * D_RG), const2),
            pl.BlockSpec((1, 4 * D_RG), const2),
            pl.BlockSpec((2, D_RG), const2),
            pl.BlockSpec(band.shape, const3),
        ],
        out_specs=[
            pl.BlockSpec((t_len, D_POOL + D_RG), lambda b: (b, 0)),
            pl.BlockSpec((None, 2, D_RG), lambda b: (b, 0, 0)),
        ],
        out_shape=[
            jax.ShapeDtypeStruct((n_batch * t_len, D_POOL + D_RG), Y_DTYPE),
            jax.ShapeDtypeStruct((n_batch, 2, D_RG), F32),
        ],
        scratch_shapes=[
            pltpu.VMEM((RG_CHUNK + 2 * HALO, D_RG), F32),
            pltpu.VMEM((RG_CHUNK, D_RG), F32),
            pltpu.VMEM((RG_CHUNK, D_RG), F32),
            pltpu.VMEM((RG_CHUNK, D_RG), F32),
            pltpu.VMEM((t_len, D_RG), F32),
        ],
        compiler_params=_ARB1,
    )(z, h0, p['pool_w'], p['pool_scale'], p['conv_w'], p['conv_b'], p['wg'], p['bg'], p['lam'], band)


def _rwkv_kernel(z_ref, s0_ref, mu_ref, kkw_ref, ka_ref, rk_ref, g2_ref, w0_ref, w2_ref, a0_ref, a2_ref,
                 lnw_ref, lnb_ref, tri_ref, bd_ref,
                 y_ref, sfin_ref, ebuf, s_s, yd_s, bon_s, g_s, pre_s, gam_s, *, t_len):
    L = SCAN_CHUNK
    n_ch = t_len // L
    L2 = 2 * L
    zero = jnp.zeros((RW_HEAD, RW_HEAD), F32)
    for d in range(2):
        for p in range(RW_PAIRS):
            top = jnp.concatenate([s0_ref[d, 2 * p], zero], axis=1)
            bot = jnp.concatenate([zero, s0_ref[d, 2 * p + 1]], axis=1)
            s_s[d, p] = jnp.concatenate([top, bot], axis=0)


    def stack(x):
        first_head = lax.broadcasted_iota(jnp.int32, (L, LANES), 1) < RW_HEAD
        zero = jnp.zeros_like(x)
        return jnp.concatenate([jnp.where(first_head, x, zero), jnp.where(first_head, zero, x)], axis=0)


    def prep(c0, d, slot, item):
        ones_bd = bd_ref[...]
        ebuf[item] = _load_ext(z_ref, c0, L, t_len, 0, RW_ZCOLS)
        yield

        def shifted(lo, width):
            cur = ebuf[item, HALO:HALO + L, lo:lo + width]
            prev = ebuf[item, HALO - 1:HALO - 1 + L, lo:lo + width]
            nxt = ebuf[item, HALO + 1:HALO + 1 + L, lo:lo + width]
            return cur + mu_ref[0:1, lo:lo + width] * (prev - cur) + mu_ref[1:2, lo:lo + width] * (nxt - cur)

        o = 3 * D_RW
        tw_hi, tw_lo = _split2(jnp.tanh(shifted(o + RW_G_LORA, LANES)))
        w_lora = _dot(tw_hi, w2_ref[d, 0]) + (_dot(tw_hi, w2_ref[d, 1]) + _dot(tw_lo, w2_ref[d, 0]))
        yield
        w_log = -_softplus(-(w0_ref[d:d + 1, :] + w_lora)) - 0.5
        lw = -jnp.exp(w_log)
        yield
        cum = _dot_exact_lhs(tri_ref[d], lw)
        yield
        ad = shifted(o + RW_G_LORA + LANES, LANES)
        a = _sigmoid(a0_ref[d:d + 1, :] + _dot(ad.astype(BF16), a2_ref[d]))
        yield
        if d == 0:
            gd = shifted(o, RW_G_LORA)
            g_s[pl.ds(c0, L), :] = _dot(_sigmoid(gd).astype(BF16), g2_ref[...]).astype(g_s.dtype)
            yield
        last = L - 1 if d == 0 else 0

        for p in range(RW_PAIRS):
            lo = p * LANES
            cols = slice(lo, lo + LANES)
            r = shifted(lo, LANES)
            k = shifted(D_RW + lo, LANES)
            v = shifted(2 * D_RW + lo, LANES)
            yield
            kkv = k * kkw_ref[:, cols]
            nrm = jnp.sqrt(_dot_exact_rhs(kkv * kkv, ones_bd))
            kkn = kkv / jnp.maximum(nrm, 1e-12)
            kd = k * (1.0 + (a[:, cols] - 1.0) * ka_ref[:, cols])
            bonus = _dot((r * kd * rk_ref[:, cols]).astype(BF16), ones_bd) * v
            bon_s[d, pl.ds(c0, L), cols] = bonus.astype(bon_s.dtype)
            yield
            e_pos = jnp.exp(cum[:, cols])
            e_neg = jnp.exp(-cum[:, cols])
            gam = e_pos[last:last + 1, :]
            rh = r * e_pos
            kh = kd * e_neg
            bh = (kkn * a[:, cols]) * e_neg
            ah = -kkn * jnp.exp(cum[:, cols] - lw[:, cols])
            for n, x in enumerate((ah, rh, bh, kh, bh * gam, kh * gam, v)):
                pre_s[slot, item, n, :, cols] = x.astype(BF16)
            gam_s[slot, item, 0:1, cols] = gam
            yield

    def load_chains(c0s, slot, base_item):
        chains = []
        for d in range(2):
            item = base_item + d
            for p in range(RW_PAIRS):
                cols = slice(p * LANES, (p + 1) * LANES)
                ah, rh, bh, kh, bhg, khg, v = (stack(pre_s[slot, item, n, :, cols]) for n in range(7))
                chains.append(dict(
                    d=d, p=p, gam=gam_s[slot, item, 0:1, cols], v_st=v,
                    lhs=jnp.concatenate([ah, rh], axis=0),
                    rhs=jnp.concatenate([bh, kh], axis=0),
                    bk=jnp.concatenate([bhg, khg], axis=0)))
        return chains

    n_levels = int(math.log2(L))
    per_yield = 1
    MM_PER_PREP = 3

    def state_free(chains):
        ri = lax.broadcasted_iota(jnp.int32, (L2, L2), 0)
        ci = lax.broadcasted_iota(jnp.int32, (L2, L2), 1)
        eye = jnp.where(ri == ci, 1.0, 0.0)
        rl = jnp.where(ri < L, ri, ri - L)
        cl = jnp.where(ci < L, ci, ci - L)
        strict = (cl < rl, cl > rl)
        incl = (cl <= rl, cl >= rl)
        level = ri ^ ci

        for n, ch in enumerate(chains):
            ch['sc'] = _dot_nt(ch['lhs'], ch['rhs'])
            if n % per_yield == per_yield - 1:
                yield
        for n, ch in enumerate(chains):
            sc, d = ch['sc'], ch['d']
            n_ab = jnp.where(strict[d], sc[0:L2, 0:L2], 0.0)
            m_ak = jnp.where(strict[d], sc[0:L2, L2:2 * L2], 0.0).astype(BF16)
            t_rb = jnp.where(incl[d], sc[L2:2 * L2, 0:L2], 0.0)
            t_rk = jnp.where(incl[d], sc[L2:2 * L2, L2:2 * L2], 0.0)
            ch['t_rbk'] = jnp.concatenate([t_rb, t_rk], axis=1).astype(BF16)
            ch['mv'] = _dot(m_ak, ch['v_st'])
            ch['t'] = eye + jnp.where(level == 1, n_ab, 0.0)
            ch['n_lv'] = [jnp.where(lax.shift_right_logical(level, lv) == 1, n_ab, 0.0).astype(BF16)
                          for lv in range(1, n_levels)]
            del ch['sc']
            if n % per_yield == per_yield - 1:
                yield
        for lv in range(n_levels - 1):
            for n, ch in enumerate(chains):
                ch['tb'] = ch['t'].astype(BF16)
                ch['x'] = _dot(ch['n_lv'][lv], ch['tb']).astype(BF16)
                if n % per_yield == per_yield - 1:
                    yield
            for n, ch in enumerate(chains):
                ch['t'] = ch['t'] + _dot(ch['tb'], ch['x'])
                if n % per_yield == per_yield - 1:
                    yield

    def state_step(chains, c0s):
        for ch in chains:
            ch['s_prev'] = s_s[ch['d'], ch['p']]
            ch['ar_s'] = _dot_nt(ch['lhs'], ch['s_prev'].astype(BF16))
        yield
        for ch in chains:
            u_rhs = ch['ar_s'][0:L2] + ch['mv']
            u_st = _dot(ch['t'].astype(BF16), u_rhs.astype(BF16)).astype(BF16)
            ch['uv'] = jnp.concatenate([u_st, ch['v_st']], axis=0)
        yield
        for ch in chains:
            y_st = ch['ar_s'][L2:2 * L2] + _dot(ch['t_rbk'], ch['uv'])
            ch['y'] = y_st[0:L] + y_st[L:L2]
            s_s[ch['d'], ch['p']] = ch['s_prev'] * ch['gam'] + _dot_tn(ch['uv'], ch['bk'])
        for d in range(2):
            y_dir = jnp.concatenate([ch['y'] for ch in chains if ch['d'] == d], axis=1)
            yd_s[d, pl.ds(c0s[d], L), :] = y_dir.astype(yd_s.dtype)
        yield

    steps = 2
    n_trips = n_ch // steps

    def starts(trip, j):
        step = trip * steps + j
        if isinstance(step, int):
            return step * L, (n_ch - 1 - step) * L
        return pl.multiple_of(step * L, L), pl.multiple_of((n_ch - 1 - step) * L, L)

    def prep_trip(trip, slot):
        for j in range(steps):
            c0s = starts(trip, j)
            for d in range(2):
                yield from prep(c0s[d], d, slot, 2 * j + d)

    def matmul_trip(trip, slot):
        groups = [(load_chains(starts(trip, j), slot, 2 * j), starts(trip, j)) for j in range(steps)]
        yield from state_free([ch for chains, _ in groups for ch in chains])
        for chains, c0s in groups:
            yield from state_step(chains, c0s)

    def interleave(*weighted):
        live = list(weighted)
        done = object()
        while live:
            for entry in list(live):
                stream, weight = entry
                for _ in range(weight):
                    if next(stream, done) is done:
                        live.remove(entry)
                        break

    interleave((prep_trip(0, 0), 1))

    def body(i, carry):
        interleave((matmul_trip(2 * i, 0), MM_PER_PREP), (prep_trip(2 * i + 1, 1), 1))
        interleave((matmul_trip(2 * i + 1, 1), MM_PER_PREP), (prep_trip(2 * i + 2, 0), 1))
        return carry

    lax.fori_loop(0, n_trips // 2 - 1, body, 0)
    interleave((matmul_trip(n_trips - 2, 0), MM_PER_PREP), (prep_trip(n_trips - 1, 1), 1))
    interleave((matmul_trip(n_trips - 1, 1), 1))

    def epilogue(i, carry):
        c0 = pl.multiple_of(i * L, L)
        ones_bd = bd_ref[...]
        ys = yd_s[0, pl.ds(c0, L), :].astype(F32) + yd_s[1, pl.ds(c0, L), :].astype(F32)
        mu = _head_sum(ys, ones_bd) * (1.0 / RW_HEAD)
        yc = ys - mu
        var = _head_sum(yc * yc, ones_bd) * (1.0 / RW_HEAD)
        yn = yc * lax.rsqrt(var + RW_LN_EPS) * lnw_ref[...] + lnb_ref[...]
        bonus = bon_s[0, pl.ds(c0, L), :].astype(F32) + bon_s[1, pl.ds(c0, L), :].astype(F32)
        out = (yn + bonus) * g_s[pl.ds(c0, L), :].astype(F32)
        y_ref[pl.ds(c0, L), :] = out.astype(y_ref.dtype)
        return carry

    lax.fori_loop(0, n_ch, epilogue, 0, unroll=2)
    for d in range(2):
        for p in range(RW_PAIRS):
            s_pair = s_s[d, p]
            sfin_ref[d, 2 * p] = s_pair[0:RW_HEAD, 0:RW_HEAD]
            sfin_ref[d, 2 * p + 1] = s_pair[RW_HEAD:, RW_HEAD:]


def _rwkv_mixer(z, s0, p, n_batch, t_len):
    L = SCAN_CHUNK
    c2 = lambda b: (0, 0)
    c3 = lambda b: (0, 0, 0)
    st_spec = pl.BlockSpec((None, 2, RW_HEADS, RW_HEAD, RW_HEAD), lambda b: (b, 0, 0, 0, 0))
    return pl.pallas_call(
        functools.partial(_rwkv_kernel, t_len=t_len),
        grid=(n_batch,),
        in_specs=[
            pl.BlockSpec((t_len, RW_ZCOLS), lambda b: (b, 0)),
            st_spec,
            pl.BlockSpec((2, RW_ZCOLS), c2),
            pl.BlockSpec((1, D_RW), c2),
            pl.BlockSpec((1, D_RW), c2),
            pl.BlockSpec((1, D_RW), c2),
            pl.BlockSpec((RW_G_LORA, D_RW), c2),
            pl.BlockSpec((2, D_RW), c2),
            pl.BlockSpec((2, 2, LANES, D_RW), lambda b: (0, 0, 0, 0)),
            pl.BlockSpec((2, D_RW), c2),
            pl.BlockSpec((2, LANES, D_RW), c3),
            pl.BlockSpec((1, D_RW), c2),
            pl.BlockSpec((1, D_RW), c2),
            pl.BlockSpec((2, L, L), c3),
            pl.BlockSpec((LANES, LANES), c2),
        ],
        out_specs=[pl.BlockSpec((t_len, D_RW), lambda b: (b, 0)), st_spec],
        out_shape=[
            jax.ShapeDtypeStruct((n_batch * t_len, D_RW), Y_DTYPE),
            jax.ShapeDtypeStruct((n_batch, 2, RW_HEADS, RW_HEAD, RW_HEAD), F32),
        ],
        scratch_shapes=[
            pltpu.VMEM((4, L + 2 * HALO, RW_ZCOLS), F32),
            pltpu.VMEM((2, RW_PAIRS, LANES, LANES), F32),
            pltpu.VMEM((2, t_len, D_RW), BF16),
            pltpu.VMEM((2, t_len, D_RW), BF16),
            pltpu.VMEM((t_len, D_RW), BF16),
            pltpu.VMEM((2, 4, 7, L, D_RW), BF16),
            pltpu.VMEM((2, 4, 8, D_RW), F32),
        ],
        compiler_params=_ARB1,
    )(z, s0, p['mu'], p['kk'], p['ka'], p['rk'], p['g2'], p['w0'], p['w2'], p['a0'], p['a2'],
      p['ln_w'], p['ln_b'], p['tri'], p['ones_bd'])


def _mlstm2_kernel(z_ref, c0_ref, n0_ref, m0_ref, cw_ref, cb_ref, gb_ref, nw_ref, tri_ref, esel_ref, shift_ref,
                   ones_ref, y_ref, cfin_ref, nfin_ref, mfin_ref, c_s, n_s, m_s, hd_s, qk_s, gx_s, gr_s,
                   *, t_len):
    L = ML_SCAN
    n_ch = t_len // L
    c_s[...] = c0_ref[...]
    n_s[...] = n0_ref[...]
    m_s[...] = m0_ref[...]
    q_scale = ML_DK ** -0.5
    g_lo = 4 * D_ML
    n_real = 4 * ML_HEADS

    def conv_chunk(i, carry):
        c0 = pl.multiple_of(i * L, L)
        ext = _load_ext(z_ref, c0, L, t_len, 0, 2 * D_ML, raw=True)
        taps = _dot(shift_ref[...], ext)
        qk = cb_ref[...] + ext[HALO:HALO + L].astype(F32) * cw_ref[CONV_LEFT:CONV_LEFT + 1, :]
        for n, j in enumerate(jj for jj in range(CONV_W) if jj != CONV_LEFT):
            qk = qk + taps[n * L:(n + 1) * L] * cw_ref[j:j + 1, :]
        qk = _silu(qk)
        qk_s[pl.ds(c0, L), 0:D_ML] = (qk[:, :D_ML] * q_scale).astype(BF16)
        qk_s[pl.ds(c0, L), D_ML:2 * D_ML] = qk[:, D_ML:].astype(BF16)

        gts = z_ref[pl.ds(c0, L), g_lo:g_lo + LANES].astype(F32) + gb_ref[...]
        x16 = gts.T[0:n_real, :]
        lf_hi, lf_lo = _split2(-_softplus(-x16))
        lane = lax.broadcasted_iota(jnp.int32, (8, L), 1)
        for d in range(2):
            b16 = _dot_nt(lf_hi, tri_ref[d]) + _dot_nt(lf_lo, tri_ref[d])
            x8 = x16[8 * d:8 * d + 8, :]
            b8 = b16[8 * d:8 * d + 8, :]
            g8 = pltpu.roll(x8, ML_HEADS, axis=0) - b8
            cm8 = g8
            step = 1
            while step < L:
                if d == 0:
                    sh = jnp.where(lane >= step, pltpu.roll(cm8, step, axis=1), -jnp.inf)
                else:
                    sh = jnp.where(lane < L - step, pltpu.roll(cm8, L - step, axis=1), -jnp.inf)
                cm8 = jnp.maximum(cm8, sh)
                step *= 2
            x32 = jnp.concatenate([g8, b8, cm8, jnp.zeros((8, L), F32)], axis=0)
            hi = x32.astype(BF16)
            r1 = x32 - hi.astype(F32)
            mid = r1.astype(BF16)
            lo = (r1 - mid.astype(F32)).astype(BF16)
            gx_s[d, i] = jnp.concatenate([hi, mid, lo], axis=0)
            gr_s[d, i] = g8
        return carry

    lax.fori_loop(0, n_ch, conv_chunk, 0, unroll=2)

    def prep(ci, d):
        c0 = pl.multiple_of(ci * L, L)
        q = qk_s[pl.ds(c0, L), 0:D_ML]
        k = qk_s[pl.ds(c0, L), D_ML:2 * D_ML]
        v = z_ref[pl.ds(c0, L), 2 * D_ML:3 * D_ML]
        g8 = gr_s[d, ci]
        cb = _dot_tn(gx_s[d, ci], esel_ref[...])
        ri = lax.broadcasted_iota(jnp.int32, (L, L), 0)
        ci = lax.broadcasted_iota(jnp.int32, (L, L), 1)
        causal = (ci <= ri) if d == 0 else (ci >= ri)
        last = L - 1 if d == 0 else 0
        ones = ones_ref[...]
        chains = []
        for h in range(ML_HEADS):
            cols = slice(h * ML_DK, (h + 1) * ML_DK)
            chains.append(dict(
                d=d, h=h, last=last, causal=causal,
                g_row=g8[ML_HEADS + h:ML_HEADS + h + 1, :],
                g_b=cb[:, cols], b_b=cb[:, D_ML + h * ML_DK:D_ML + (h + 1) * ML_DK],
                cm_b=cb[:, 2 * D_ML + h * ML_DK:2 * D_ML + (h + 1) * ML_DK],
                qb=q[:, cols], kb=k[:, cols],
                v=v[:, cols].astype(F32), v_ones=jnp.concatenate([v[:, cols].astype(BF16), ones], axis=1)))
        return chains

    def body(i, carry):
        c0s = (pl.multiple_of(i * L, L), pl.multiple_of((n_ch - 1 - i) * L, L))
        chains = prep(i, 0) + prep(n_ch - 1 - i, 1)
        for ch in chains:
            d, h = ch['d'], ch['h']
            ch['qk'] = _dot_nt(ch['qb'], ch['kb'])
            ch['c_prev'] = c_s[d, h]
            ch['n_prev'] = n_s[d, h]
            ch['m_prev'] = m_s[d, h:h + 1, :]
            ch['qcn'] = _dot(ch['qb'], jnp.concatenate([ch['c_prev'].astype(BF16),
                                                        ch['n_prev'].astype(BF16)], axis=1))
        for ch in chains:
            mu = jnp.maximum(ch['m_prev'], ch['cm_b'])
            ch['mu'] = mu
            ch['s'] = ch['qk'] * jnp.exp(jnp.where(ch['causal'], ch['g_row'] - mu, -jnp.inf))
            ch['w_inter'] = jnp.exp(ch['m_prev'] - mu)
        for ch in chains:
            last = ch['last']
            ch['sv'] = _dot(ch['s'].astype(BF16), ch['v_ones'])
            mu_last = ch['mu'][last:last + 1, :]
            ch['m_new'] = ch['b_b'][last:last + 1, :] + mu_last
            ch['dec'] = jnp.exp(ch['m_prev'] - mu_last)
            w_s = jnp.exp(ch['g_b'] - mu_last)
            ch['upd'] = _dot_tn(ch['kb'], jnp.concatenate([(w_s * ch['v']).astype(BF16),
                                                           w_s.astype(BF16)], axis=1))
        for ch in chains:
            d, h = ch['d'], ch['h']
            num = ch['sv'][:, :ML_DK] + ch['w_inter'] * ch['qcn'][:, :ML_DK]
            den = ch['sv'][:, ML_DK:] + ch['w_inter'] * ch['qcn'][:, ML_DK:]
            ch['out'] = num / jnp.maximum(jnp.abs(den), jnp.exp(-(ch['b_b'] + ch['mu'])))
            c_s[d, h] = ch['dec'] * ch['c_prev'] + ch['upd'][:, :ML_DK]
            n_s[d, h] = ch['dec'] * ch['n_prev'] + ch['upd'][:, ML_DK:]
            m_s[d, h:h + 1, :] = ch['m_new']
        for d in range(2):
            hd_s[d, pl.ds(c0s[d], L), :] = jnp.concatenate([ch['out'] for ch in chains if ch['d'] == d], axis=1)
        return carry

    lax.fori_loop(0, n_ch, body, 0, unroll=2)

    def epilogue(i, carry):
        c0 = pl.multiple_of(i * L, L)
        ones = ones_ref[...]
        hs = hd_s[0, pl.ds(c0, L), :] + hd_s[1, pl.ds(c0, L), :]
        o = z_ref[pl.ds(c0, L), 3 * D_ML:4 * D_ML].astype(F32)
        mu = _head_sum(hs, ones) * (1.0 / ML_DK)
        xc = hs - mu
        var = _head_sum(xc * xc, ones) * (1.0 / ML_DK)
        y = xc * lax.rsqrt(var + NORM_EPS) * nw_ref[...] * _sigmoid(o)
        y_ref[pl.ds(c0, L), :] = y.astype(y_ref.dtype)
        return carry

    lax.fori_loop(0, n_ch, epilogue, 0, unroll=2)
    cfin_ref[...] = c_s[...]
    nfin_ref[...] = n_s[...]
    mfin_ref[...] = m_s[...]


def _mlstm2_mixer(z, c0, n0, m0, p, n_batch, t_len):
    assert z.dtype == BF16
    L = ML_SCAN
    c2 = lambda b: (0, 0)
    c3 = lambda b: (0, 0, 0)
    c_spec = pl.BlockSpec((None, 2, ML_HEADS, ML_DK, ML_DK), lambda b: (b, 0, 0, 0, 0))
    m_spec = pl.BlockSpec((None, 2, ML_HEADS, ML_DK), lambda b: (b, 0, 0, 0))
    return pl.pallas_call(
        functools.partial(_mlstm2_kernel, t_len=t_len),
        grid=(n_batch,),
        in_specs=[
            pl.BlockSpec((t_len, ML_ZCOLS), lambda b: (b, 0)),
            c_spec, c_spec, m_spec,
            pl.BlockSpec((CONV_W, 2 * D_ML), c2),
            pl.BlockSpec((1, 2 * D_ML), c2),
            pl.BlockSpec((1, LANES), c2),
            pl.BlockSpec((1, D_ML), c2),
            pl.BlockSpec((2, L, L), c3),
            pl.BlockSpec(p['esel'].shape, c2),
            pl.BlockSpec(p['shift'].shape, c2),
            pl.BlockSpec((L, LANES), c2),
        ],
        out_specs=[pl.BlockSpec((t_len, D_ML), lambda b: (b, 0)), c_spec, c_spec, m_spec],
        out_shape=[
            jax.ShapeDtypeStruct((n_batch * t_len, D_ML), Y_DTYPE),
            jax.ShapeDtypeStruct((n_batch, 2, ML_HEADS, ML_DK, ML_DK), F32),
            jax.ShapeDtypeStruct((n_batch, 2, ML_HEADS, ML_DK, ML_DK), F32),
            jax.ShapeDtypeStruct((n_batch, 2, ML_HEADS, ML_DK), F32),
        ],
        scratch_shapes=[
            pltpu.VMEM((2, ML_HEADS, ML_DK, ML_DK), F32),
            pltpu.VMEM((2, ML_HEADS, ML_DK, ML_DK), F32),
            pltpu.VMEM((2, ML_HEADS, ML_DK), F32),
            pltpu.VMEM((2, t_len, D_ML), F32),
            pltpu.VMEM((t_len, 2 * D_ML), BF16),
            pltpu.VMEM((2, t_len // L, 96, L), BF16),
            pltpu.VMEM((2, t_len // L, 8, L), F32),
        ],
        compiler_params=_ARB1,
    )(z, c0, n0, m0, p['conv_w'], p['conv_b'], p['gate_b'], p['norm_w'], p['tri'], p['esel'], p['shift'],
      p['ones'])


def _conv_shift(n):
    taps = jnp.array([j for j in range(CONV_W) if j != CONV_LEFT])
    t = jnp.arange((CONV_W - 1) * n)
    src = HALO + t % n + taps[t // n] - CONV_LEFT
    return (src[:, None] == jnp.arange(n + 2 * HALO)[None, :]).astype(BF16)


def _mlstm_consts():
    L = ML_SCAN
    lower = jnp.tril(jnp.ones((L, L), F32))
    tri = jnp.stack([lower, lower.T]).astype(BF16)
    rows = jnp.arange(32)
    blk = jnp.arange(3 * D_ML) // ML_DK
    src_row = 8 * (blk // ML_HEADS) + ML_HEADS + blk % ML_HEADS
    esel = (rows[:, None] == src_row[None, :]).astype(BF16)
    esel = jnp.concatenate([esel, esel, esel], axis=0)
    return dict(tri=tri, esel=esel, shift=_conv_shift(L), ones=jnp.ones((L, LANES), BF16))


def _block_diag(w):
    n, c, d = w.shape
    return jnp.einsum('ncd,nm->ncmd', w, jnp.eye(n, dtype=w.dtype)).reshape(n * c, n * d)


def _scan_tri():
    L = SCAN_CHUNK
    lower = jnp.tril(jnp.ones((L, L), F32))
    return jnp.stack([lower, lower.T]).astype(BF16)


def _prep_even(ev_w_in, pool_w, pool_scale, rg_conv_w, rg_conv_b, rg_wa, rg_ba, rg_wx, rg_bx, rg_lam, ev_w_out, j):
    wg = jnp.concatenate([_block_diag(rg_wa[j, 0]), _block_diag(rg_wx[j, 0]),
                          _block_diag(rg_wa[j, 1]), _block_diag(rg_wx[j, 1])], axis=1)
    bg = jnp.concatenate([rg_ba[j, 0], rg_bx[j, 0], rg_ba[j, 1], rg_bx[j, 1]]).reshape(1, 4 * D_RG)
    return dict(
        w_in=[ev_w_in[j].astype(BF16)],
        w_out=ev_w_out[j].astype(BF16),
        pool_w=pool_w[j].astype(BF16),
        pool_scale=pool_scale[j].reshape(1, D_POOL),
        conv_w=rg_conv_w[j],
        conv_b=rg_conv_b[j].reshape(1, D_RG),
        wg=wg.astype(BF16),
        bg=bg,
        lam=rg_lam[j],
    )


def _pad_cols(w, cuts, widths):
    pieces = jnp.split(w, cuts, axis=-1)
    out = []
    for piece, width in zip(pieces, widths):
        extra = width - piece.shape[-1]
        if extra:
            piece = jnp.pad(piece, [(0, 0)] * (piece.ndim - 1) + [(0, extra)])
        out.append(piece)
    return jnp.concatenate(out, axis=-1)


def _prep_odd(od_w_in, rw_mu, rw_w0, rw_w2, rw_a0, rw_a2, rw_kk, rw_ka, rw_rk, rw_g2, rw_ln_w, rw_ln_b,
              ml_conv_w, ml_conv_b, ml_bi, ml_bf, ml_norm_w, od_w_out, j):
    rw_cols = 3 * D_RW + RW_G_LORA + RW_W_LORA + RW_A_LORA
    base = 3 * D_RW + RW_G_LORA
    cuts = [base, base + RW_W_LORA]
    widths = [base, LANES, LANES]
    w_rw = _pad_cols(od_w_in[j][:, :rw_cols], cuts, widths)
    w_ml = _pad_cols(od_w_in[j][:, rw_cols:], [4 * D_ML], [4 * D_ML, LANES])
    pad_rows = lambda w: jnp.pad(w, ((0, 0), (0, LANES - w.shape[1]), (0, 0)))
    gate_b = jnp.concatenate([ml_bi[j, 0], ml_bf[j, 0], ml_bi[j, 1], ml_bf[j, 1]])
    gate_b = jnp.pad(gate_b, (0, LANES - gate_b.shape[0])).reshape(1, LANES)
    hm = jnp.arange(LANES) // RW_HEAD
    rw = dict(
        mu=_pad_cols(rw_mu[j], cuts, widths),
        kk=rw_kk[j].reshape(1, D_RW), ka=rw_ka[j].reshape(1, D_RW), rk=rw_rk[j].reshape(1, D_RW),
        g2=rw_g2[j].astype(BF16),
        w0=rw_w0[j], w2=jnp.stack(_split2(pad_rows(rw_w2[j])), axis=1),
        a0=rw_a0[j], a2=pad_rows(rw_a2[j]).astype(BF16),
        ln_w=rw_ln_w[j].reshape(1, D_RW), ln_b=rw_ln_b[j].reshape(1, D_RW),
        tri=_scan_tri(),
        ones_bd=(hm[:, None] == hm[None, :]).astype(BF16),
    )
    ml = dict(
        conv_w=ml_conv_w[j], conv_b=ml_conv_b[j].reshape(1, 2 * D_ML),
        gate_b=gate_b, norm_w=ml_norm_w[j].reshape(1, D_ML), **_mlstm_consts(),
    )
    return dict(w_in=[w_rw.astype(BF16), w_ml.astype(BF16)], w_out=od_w_out[j].astype(BF16), rw=rw, ml=ml)


def _trunk(x, mod, rows_per_mod, mod_base, grid, states, params, n_batch, t_len):
    ev, od, dense = params
    rg0, rw0, c0, n0, m0 = states
    x = x.reshape(n_batch * t_len, D_MODEL)
    spec = (rows_per_mod, mod_base)

    (z,) = _in_proj(x, mod[0], *spec, dense['norm1_w'][0], ev['w_in'])
    y_ev, rg_f = _even_mixer(z, rg0, ev, n_batch, t_len, grid)
    x = _out_mlp([y_ev], x, mod[0], *spec, dense['norm2_w'][0], ev['w_out'],
                 dense['mlp_w1'], dense['mlp_w2'], 0, dense['final_norm_w'], False)

    z_rw, z_ml = _in_proj(x, mod[1], *spec, dense['norm1_w'][1], od['w_in'])
    y_rw, s_f = _rwkv_mixer(z_rw, rw0, od['rw'], n_batch, t_len)
    y_ml, c_f, n_f, m_f = _mlstm2_mixer(z_ml, c0, n0, m0, od['ml'], n_batch, t_len)
    y = _out_mlp([y_rw, y_ml], x, mod[1], *spec, dense['norm2_w'][1], od['w_out'],
                 dense['mlp_w1'], dense['mlp_w2'], 1, dense['final_norm_w'], True)
    return y.reshape(n_batch, t_len, D_MODEL), (rg_f, s_f, c_f, n_f, m_f)


def kernel(x_prompt, x_sample, state_rglru, state_rwkv, state_mlstm_C, state_mlstm_n, state_mlstm_m,
           c, c_ctx, norm1_w, norm2_w, w_mod, b_mod, mlp_w1, mlp_w2, final_norm_w,
           ev_w_in, pool_w, pool_scale, rg_conv_w, rg_conv_b, rg_wa, rg_ba, rg_wx, rg_bx, rg_lam, ev_w_out,
           od_w_in, rw_mu, rw_w0, rw_w2, rw_a0, rw_a2, rw_kk, rw_ka, rw_rk, rw_g2, rw_ln_w, rw_ln_b,
           ml_conv_w, ml_conv_b, ml_bi, ml_bf, ml_norm_w, od_w_out):
    bp, tp, _ = x_prompt.shape
    bs, ts, _ = x_sample.shape

    n_cond = 16
    cond = jnp.concatenate([c_ctx[None, :], c, jnp.zeros((n_cond - 1 - bs, D_MODEL), F32)], axis=0)
    mod = _modulation(cond, w_mod, b_mod)

    ev = _prep_even(ev_w_in, pool_w, pool_scale, rg_conv_w, rg_conv_b, rg_wa, rg_ba, rg_wx, rg_bx, rg_lam,
                    ev_w_out, 0)
    od = _prep_odd(od_w_in, rw_mu, rw_w0, rw_w2, rw_a0, rw_a2, rw_kk, rw_ka, rw_rk, rw_g2, rw_ln_w, rw_ln_b,
                   ml_conv_w, ml_conv_b, ml_bi, ml_bf, ml_norm_w, od_w_out, 0)
    dense = dict(norm1_w=norm1_w, norm2_w=norm2_w, mlp_w1=mlp_w1.astype(BF16), mlp_w2=mlp_w2.astype(BF16),
                 final_norm_w=final_norm_w)
    params = (ev, od, dense)

    zero_states = (
        jnp.zeros((bp, 2, D_RG), F32),
        jnp.zeros((bp, 2, RW_HEADS, RW_HEAD, RW_HEAD), F32),
        jnp.zeros((bp, 2, ML_HEADS, ML_DK, ML_DK), F32),
        jnp.zeros((bp, 2, ML_HEADS, ML_DK, ML_DK), F32),
        jnp.zeros((bp, 2, ML_HEADS, ML_DK), F32),
    )
    y_prompt, (rg_f, s_f, c_f, n_f, m_f) = _trunk(
        x_prompt, mod, bp * tp, 0, False, zero_states, params, bp, tp)

    sample_states = (
        state_rglru[:, 0],
        state_rwkv[:, 0],
        state_mlstm_C[:, 0],
        jnp.broadcast_to(state_mlstm_n[:, 0][..., None], (bs, 2, ML_HEADS, ML_DK, ML_DK)),
        jnp.broadcast_to(state_mlstm_m[:, 0][..., None], (bs, 2, ML_HEADS, ML_DK)),
    )
    y_sample, _ = _trunk(x_sample, mod, ts, 1, True, sample_states, params, bs, ts)

    return (y_prompt, y_sample,
            rg_f[:, None],
            s_f[:, None],
            c_f[:, None],
            n_f[:, None, :, :, :, 0],
            m_f[:, None, :, :, 0])
```

```python
import functools
import math

import jax
import jax.numpy as jnp
from jax import lax
from jax.experimental import pallas as pl
from jax.experimental.pallas import tpu as pltpu

F32 = jnp.float32
BF16 = jnp.bfloat16

D_MODEL = 1024
DEPTH = 2
GRID_W = 64
D_FF = 4 * D_MODEL
NORM_EPS = 1e-6
N_MOD = 6

D_POOL = D_MODEL // 2
POOL_WINDOWS = (2, 4, 8, 16)
POOL_GW = D_POOL // len(POOL_WINDOWS)
D_RG = D_MODEL // 2
RG_BLOCKS = 8
RG_C = 8.0
CONV_W = 4
CONV_LEFT = 2
EV_COLS = D_POOL + 2 * D_RG

D_RW = D_MODEL // 2
RW_HEAD = 64
RW_HEADS = D_RW // RW_HEAD
RW_PAIRS = RW_HEADS // 2
RW_W_LORA = 64
RW_A_LORA = 64
RW_G_LORA = 128
RW_LN_EPS = 64e-5
RW_ZCOLS = 3 * D_RW + RW_G_LORA + 128 + 128
D_ML = D_MODEL // 2
ML_HEADS = 4
ML_DK = D_ML // ML_HEADS
ML_ZCOLS = 4 * D_ML + 128

LANES = 128
HALO = 16
Z_DTYPE = BF16
Y_DTYPE = BF16
ROW_BLOCK = 512
SCAN_CHUNK = 64
ML_SCAN = 128
RG_CHUNK = 256
VMEM_LIMIT = 56 * 1024 * 1024

_ARB1 = pltpu.CompilerParams(dimension_semantics=("arbitrary",), vmem_limit_bytes=VMEM_LIMIT)
_ARB2 = pltpu.CompilerParams(dimension_semantics=("arbitrary", "arbitrary"), vmem_limit_bytes=VMEM_LIMIT)


def _dot(a, b):
    return jnp.dot(a, b, preferred_element_type=F32)


def _dot_nt(a, b):
    return lax.dot_general(a, b, (((1,), (1,)), ((), ())), preferred_element_type=F32)


def _dot_tn(a, b):
    return lax.dot_general(a, b, (((0,), (0,)), ((), ())), preferred_element_type=F32)


def _split2(x):
    hi = x.astype(BF16)
    lo = (x - hi.astype(F32)).astype(BF16)
    return hi, lo


def _dot_f32(a, b):
    ah, al = _split2(a)
    bh, bl = _split2(b)
    return _dot(ah, bh) + (_dot(ah, bl) + _dot(al, bh))


def _dot_exact_lhs(m_bf16, x):
    hi, lo = _split2(x)
    return _dot(m_bf16, hi) + _dot(m_bf16, lo)


def _dot_exact_rhs(x, m_bf16):
    hi, lo = _split2(x)
    return _dot(hi, m_bf16) + _dot(lo, m_bf16)


def _softplus(x):
    return jnp.maximum(x, 0.0) + jnp.log1p(jnp.exp(-jnp.abs(x)))


def _sigmoid(x):
    return jax.nn.sigmoid(x)


def _silu(x):
    return x * jax.nn.sigmoid(x)


def _gelu_tanh(x):
    c = math.sqrt(2.0 / math.pi)
    return x * (0.5 * (1.0 + jnp.tanh(c * (x + 0.044715 * (x * x * x)))))


def _rms_norm(x, w):
    ms = jnp.mean(x * x, axis=-1, keepdims=True)
    return x * lax.rsqrt(ms + NORM_EPS) * w


def _load_ext(z_ref, c0, n, t_len, lo, hi, raw=False):
    dt = z_ref.dtype if raw else F32
    cur = z_ref[pl.ds(c0, n), lo:hi].astype(dt)
    pb = pl.multiple_of(jnp.maximum(c0 - HALO, 0), HALO)
    pa = pl.multiple_of(jnp.minimum(c0 + n, t_len - HALO), HALO)
    before = z_ref[pl.ds(pb, HALO), lo:hi].astype(dt)
    after = z_ref[pl.ds(pa, HALO), lo:hi].astype(dt)
    before = jnp.where(c0 > 0, before, jnp.zeros_like(before))
    after = jnp.where(c0 + n < t_len, after, jnp.zeros_like(after))
    return jnp.concatenate([before, cur, after], axis=0)


def _head_sum(x, ones_bd, single_pass=False):
    n = x.shape[0]
    groups = x.shape[1] // LANES
    xs = jnp.concatenate([x[:, g * LANES:(g + 1) * LANES] for g in range(groups)], axis=0)
    s = _dot(xs.astype(BF16), ones_bd) if single_pass else _dot_exact_rhs(xs, ones_bd)
    return jnp.concatenate([s[g * n:(g + 1) * n] for g in range(groups)], axis=1)


def _mod_kernel(c_ref, w_ref, b_ref, o_ref):
    s = _silu(c_ref[...])
    o_ref[...] = _dot_f32(s, w_ref[...]) + b_ref[...]


def _modulation(cond, w_mod, b_mod):
    rows = cond.shape[0]
    tn = 1536
    out = pl.pallas_call(
        _mod_kernel,
        grid=(DEPTH, N_MOD * D_MODEL // tn),
        in_specs=[
            pl.BlockSpec((rows, D_MODEL), lambda l, j: (0, 0)),
            pl.BlockSpec((None, D_MODEL, tn), lambda l, j: (l, 0, j)),
            pl.BlockSpec((None, 1, tn), lambda l, j: (l, 0, j)),
        ],
        out_specs=pl.BlockSpec((None, rows, tn), lambda l, j: (l, 0, j)),
        out_shape=jax.ShapeDtypeStruct((DEPTH, rows, N_MOD * D_MODEL), F32),
        compiler_params=_ARB2,
    )(cond, w_mod, b_mod.reshape(DEPTH, 1, N_MOD * D_MODEL))
    return out.reshape(DEPTH, rows, N_MOD, D_MODEL)


def _inproj_kernel(*refs, n_out, col_chunk):
    x_ref, mod_ref, nw_ref = refs[:3]
    w_refs = refs[3:3 + n_out]
    z_refs = refs[3 + n_out:]
    h = _rms_norm(x_ref[...], nw_ref[...])
    h = h * (1.0 + mod_ref[1:2, :]) + mod_ref[0:1, :]
    hb = h.astype(BF16)
    for w_ref, z_ref in zip(w_refs, z_refs):
        n = w_ref.shape[1]
        for c in range(0, n, col_chunk):
            e = min(c + col_chunk, n)
            z_ref[:, c:e] = _dot(hb, w_ref[:, c:e]).astype(z_ref.dtype)


def _in_proj(x, mod, rows_per_mod, mod_base, norm_w, weights):
    m = x.shape[0]
    tm = ROW_BLOCK
    n_out = len(weights)
    const = lambda i: (0, 0)
    in_specs = [
        pl.BlockSpec((tm, D_MODEL), lambda i: (i, 0)),
        pl.BlockSpec((None, N_MOD, D_MODEL), lambda i: (mod_base + (i * tm) // rows_per_mod, 0, 0)),
        pl.BlockSpec((1, D_MODEL), const),
    ] + [pl.BlockSpec(w.shape, const, pipeline_mode=pl.Buffered(1)) for w in weights]
    out_specs = [pl.BlockSpec((tm, w.shape[1]), lambda i: (i, 0)) for w in weights]
    out_shape = [jax.ShapeDtypeStruct((m, w.shape[1]), Z_DTYPE) for w in weights]
    return pl.pallas_call(
        functools.partial(_inproj_kernel, n_out=n_out, col_chunk=512),
        grid=(m // tm,),
        in_specs=in_specs,
        out_specs=out_specs,
        out_shape=out_shape,
        compiler_params=_ARB1,
    )(x, mod, norm_w.reshape(1, D_MODEL), *weights)


def _outmlp_kernel(*refs, n_y, final, ff_chunk):
    y_refs = refs[:n_y]
    x_ref, mod_ref, nw_ref, wo_ref, w1_ref, w2_ref, fw_ref, o_ref = refs[n_y:]
    y = None
    row = 0
    for y_ref in y_refs:
        k = y_ref.shape[1]
        part = _dot(y_ref[...], wo_ref[row:row + k, :])
        y = part if y is None else y + part
        row += k
    x = x_ref[...] + mod_ref[2:3, :] * y
    h = _rms_norm(x, nw_ref[...]) * (1.0 + mod_ref[4:5, :]) + mod_ref[3:4, :]
    hb = h.astype(BF16)
    acc = None
    for c in range(0, D_FF, ff_chunk):
        u = jnp.maximum(_dot(hb, w1_ref[:, c:c + ff_chunk]), 0.0)
        part = _dot((u * u).astype(BF16), w2_ref[c:c + ff_chunk, :])
        acc = part if acc is None else acc + part
    x = x + mod_ref[5:6, :] * acc
    if final:
        x = _rms_norm(x, fw_ref[...])
    o_ref[...] = x


def _out_mlp(ys, x, mod, rows_per_mod, mod_base, norm_w, w_out, w1, w2, layer, final_w, final):
    m = x.shape[0]
    tm = ROW_BLOCK
    const = lambda i: (0, 0)
    single = dict(pipeline_mode=pl.Buffered(1))
    pick = lambda i: (layer, 0, 0)
    in_specs = [pl.BlockSpec((tm, y.shape[1]), lambda i: (i, 0)) for y in ys] + [
        pl.BlockSpec((tm, D_MODEL), lambda i: (i, 0)),
        pl.BlockSpec((None, N_MOD, D_MODEL), lambda i: (mod_base + (i * tm) // rows_per_mod, 0, 0)),
        pl.BlockSpec((1, D_MODEL), const),
        pl.BlockSpec(w_out.shape, const, **single),
        pl.BlockSpec((None,) + w1.shape[1:], pick, **single),
        pl.BlockSpec((None,) + w2.shape[1:], pick, **single),
        pl.BlockSpec((1, D_MODEL), const),
    ]
    return pl.pallas_call(
        functools.partial(_outmlp_kernel, n_y=len(ys), final=final, ff_chunk=1024),
        grid=(m // tm,),
        in_specs=in_specs,
        out_specs=pl.BlockSpec((tm, D_MODEL), lambda i: (i, 0)),
        out_shape=jax.ShapeDtypeStruct((m, D_MODEL), F32),
        compiler_params=_ARB1,
    )(*ys, x, mod, norm_w.reshape(1, D_MODEL), w_out, w1, w2, final_w.reshape(1, D_MODEL))


def _even_kernel(z_ref, h0_ref, pw_ref, ps_ref, cw_ref, cb_ref, wg_ref, bg_ref, lam_ref, band_ref,
                 y_ref, hfin_ref, cbuf, a_s, b_s, h_s, hf_s, *, t_len, grid):
    seg = GRID_W if grid else t_len
    n_seg = t_len // seg
    n_groups = len(POOL_WINDOWS)

    pos = lax.broadcasted_iota(jnp.int32, (seg, POOL_GW), 0)
    offs = [(w // 2, w - 1 - w // 2) for w in POOL_WINDOWS]
    gcols = [slice(g * POOL_GW, (g + 1) * POOL_GW) for g in range(n_groups)]

    def grid_row(rr, cols):
        valid = jnp.logical_and(rr >= 0, rr < n_seg)
        src = pl.multiple_of(jnp.clip(rr, 0, n_seg - 1) * seg, seg)
        return jnp.where(valid, z_ref[pl.ds(src, seg), cols].astype(F32), 0.0)

    def pool_rows(rows, run):
        items = []
        for r in rows:
            base = r * seg if isinstance(r, int) else pl.multiple_of(r * seg, seg)
            new_run = []
            for g, (lo_off, hi_off) in enumerate(offs):
                cols = gcols[g]
                if grid:
                    s1 = run[g] + grid_row(r + hi_off, cols) - grid_row(r - lo_off - 1, cols)
                    new_run.append(s1)
                    cnt_r = jnp.minimum(r + hi_off, n_seg - 1) - jnp.maximum(r - lo_off, 0) + 1
                    m1 = s1 / cnt_r.astype(F32)
                    hi = m1.astype(BF16)
                    r1 = m1 - hi.astype(F32)
                    mid = r1.astype(BF16)
                    parts = jnp.concatenate([hi, mid, (r1 - mid.astype(F32)).astype(BF16)], axis=0)
                else:
                    parts = z_ref[pl.ds(base, seg), cols]
                items.append(dict(g=g, base=base, parts=parts))
            run = new_run
        for it in items:
            it['s2'] = _dot(band_ref[it['g']], it['parts'])
        for it in items:
            lo_off, hi_off = offs[it['g']]
            cnt_c = jnp.minimum(pos + hi_off, seg - 1) - jnp.maximum(pos - lo_off, 0) + 1
            xg = z_ref[pl.ds(it['base'], seg), gcols[it['g']]].astype(F32)
            it['d'] = (it['s2'] / cnt_c.astype(F32) - xg).astype(BF16)
        for it in items:
            it['y'] = _dot(it['d'], pw_ref[it['g']])
        for it in items:
            cols = gcols[it['g']]
            y_ref[pl.ds(it['base'], seg), cols] = (it['y'] * ps_ref[:, cols]).astype(y_ref.dtype)
        return tuple(run) if grid else None

    if grid:
        init = []
        for g, (lo_off, hi_off) in enumerate(offs):
            s1 = jnp.zeros((seg, POOL_GW), F32)
            for o in range(min(hi_off, n_seg)):
                s1 = s1 + z_ref[o * seg:(o + 1) * seg, gcols[g]].astype(F32)
            init.append(s1)
        per_trip = 8
        lax.fori_loop(0, n_seg // per_trip,
                      lambda i, run: pool_rows([i * per_trip + j for j in range(per_trip)], run), tuple(init))
    else:
        pool_rows([0], None)

    ch = RG_CHUNK
    n_ch = t_len // ch
    rg_lo, rg_hi = D_POOL, D_POOL + D_RG

    def gates(c0, d):
        cbuf[...] = _load_ext(z_ref, c0, ch, t_len, rg_lo, rg_hi)
        xc = cb_ref[...]
        for j in range(CONV_W):
            off = HALO - CONV_LEFT + j
            xc = xc + cbuf[off:off + ch, :] * cw_ref[j:j + 1, :]
        g = _dot(xc.astype(BF16), wg_ref[:, d * 2 * D_RG:(d + 1) * 2 * D_RG]) \
            + bg_ref[:, d * 2 * D_RG:(d + 1) * 2 * D_RG]
        r = _sigmoid(g[:, :D_RG])
        i = _sigmoid(g[:, D_RG:])
        log_a = (-RG_C) * r * _softplus(-lam_ref[d:d + 1, :])
        a = jnp.exp(log_a)
        a_s[...] = a
        one_minus_a2 = jnp.tanh(-log_a) * (a * a + 1.0)
        root = jnp.where(one_minus_a2 > 0.0, one_minus_a2 * lax.rsqrt(one_minus_a2), 0.0)
        b_s[...] = root * (i * xc)

    def scan_rows(h, reverse):
        def row(t, h):
            tt = (ch - 1 - t) if reverse else t
            h = a_s[pl.ds(tt, 1), :] * h + b_s[pl.ds(tt, 1), :]
            h_s[pl.ds(tt, 1), :] = h
            return h
        return lax.fori_loop(0, ch, row, h, unroll=8)

    def fwd_body(i, h):
        c0 = pl.multiple_of(i * ch, ch)
        gates(c0, 0)
        h = scan_rows(h, False)
        hf_s[pl.ds(c0, ch), :] = h_s[...]
        return h

    h_fwd = lax.fori_loop(0, n_ch, fwd_body, h0_ref[0:1, :])

    def bwd_body(i, h):
        c0 = pl.multiple_of((n_ch - 1 - i) * ch, ch)
        gates(c0, 1)
        h = scan_rows(h, True)
        u_gate = z_ref[pl.ds(c0, ch), rg_hi:rg_hi + D_RG].astype(F32)
        y = (hf_s[pl.ds(c0, ch), :] + h_s[...]) * _gelu_tanh(u_gate)
        y_ref[pl.ds(c0, ch), D_POOL:D_POOL + D_RG] = y.astype(y_ref.dtype)
        return h

    h_bwd = lax.fori_loop(0, n_ch, bwd_body, h0_ref[1:2, :])
    hfin_ref[0:1, :] = h_fwd
    hfin_ref[1:2, :] = h_bwd


def _even_mixer(z, h0, p, n_batch, t_len, grid):
    assert z.dtype == BF16
    seg = GRID_W if grid else t_len
    const2 = lambda b: (0, 0)
    const3 = lambda b: (0, 0, 0)
    idx = jnp.arange(seg)
    delta = idx[None, :] - idx[:, None]
    band = jnp.stack([jnp.logical_and(delta >= -(w // 2), delta <= w - 1 - w // 2) for w in POOL_WINDOWS])
    band = jnp.tile(band.astype(BF16), (1, 1, 3 if grid else 1))
    return pl.pallas_call(
        functools.partial(_even_kernel, t_len=t_len, grid=grid),
        grid=(n_batch,),
        in_specs=[
            pl.BlockSpec((t_len, EV_COLS), lambda b: (b, 0)),
            pl.BlockSpec((None, 2, D_RG), lambda b: (b, 0, 0)),
            pl.BlockSpec(p['pool_w'].shape, const3),
            pl.BlockSpec((1, D_POOL), const2),
            pl.BlockSpec((CONV_W, D_RG), const2),
            pl.BlockSpec((1, D_RG), const2),
            pl.BlockSpec((D_RG, 4 * D_RG), const2),
            pl.BlockSpec((1, 4 * D_RG), const2),
            pl.BlockSpec((2, D_RG), const2),
            pl.BlockSpec(band.shape, const3),
        ],
        out_specs=[
            pl.BlockSpec((t_len, D_POOL + D_RG), lambda b: (b, 0)),
            pl.BlockSpec((None, 2, D_RG), lambda b: (b, 0, 0)),
        ],
        out_shape=[
            jax.ShapeDtypeStruct((n_batch * t_len, D_POOL + D_RG), Y_DTYPE),
            jax.ShapeDtypeStruct((n_batch, 2, D_RG), F32),
        ],
        scratch_shapes=[
            pltpu.VMEM((RG_CHUNK + 2 * HALO, D_RG), F32),
            pltpu.VMEM((RG_CHUNK, D_RG), F32),
            pltpu.VMEM((RG_CHUNK, D_RG), F32),
            pltpu.VMEM((RG_CHUNK, D_RG), F32),
            pltpu.VMEM((t_len, D_RG), F32),
        ],
        compiler_params=_ARB1,
    )(z, h0, p['pool_w'], p['pool_scale'], p['conv_w'], p['conv_b'], p['wg'], p['bg'], p['lam'], band)


def _rwkv_kernel(z_ref, s0_ref, mu_ref, kkw_ref, ka_ref, rk_ref, g2_ref, w0_ref, w2_ref, a0_ref, a2_ref,
                 lnw_ref, lnb_ref, tri_ref, bd_ref,
                 y_ref, sfin_ref, ebuf, s_s, yd_s, bon_s, g_s, *, t_len):
    L = SCAN_CHUNK
    n_ch = t_len // L
    L2 = 2 * L
    zero = jnp.zeros((RW_HEAD, RW_HEAD), F32)
    for d in range(2):
        for p in range(RW_PAIRS):
            top = jnp.concatenate([s0_ref[d, 2 * p], zero], axis=1)
            bot = jnp.concatenate([zero, s0_ref[d, 2 * p + 1]], axis=1)
            s_s[d, p] = jnp.concatenate([top, bot], axis=0)

    def stack(x):
        first_head = lax.broadcasted_iota(jnp.int32, (L, LANES), 1) < RW_HEAD
        return jnp.concatenate([jnp.where(first_head, x, 0.0), jnp.where(first_head, 0.0, x)],
                               axis=0).astype(BF16)

    def prep(c0, d):
        ones_bd = bd_ref[...]
        ebuf[d] = _load_ext(z_ref, c0, L, t_len, 0, RW_ZCOLS)
        cur = ebuf[d, HALO:HALO + L, :]
        prev = ebuf[d, HALO - 1:HALO - 1 + L, :]
        nxt = ebuf[d, HALO + 1:HALO + 1 + L, :]
        zs = cur + mu_ref[0:1, :] * (prev - cur) + mu_ref[1:2, :] * (nxt - cur)
        r = zs[:, 0:D_RW]
        k = zs[:, D_RW:2 * D_RW]
        v = zs[:, 2 * D_RW:3 * D_RW]
        o = 3 * D_RW
        gd = zs[:, o:o + RW_G_LORA]
        wd = zs[:, o + RW_G_LORA:o + RW_G_LORA + LANES]
        ad = zs[:, o + RW_G_LORA + LANES:o + RW_G_LORA + 2 * LANES]

        kkv = k * kkw_ref[...]
        nrm = jnp.sqrt(_head_sum(kkv * kkv, ones_bd))
        kkn = kkv / jnp.maximum(nrm, 1e-12)

        tw_hi, tw_lo = _split2(jnp.tanh(wd))
        w_lora = _dot(tw_hi, w2_ref[d, 0]) + (_dot(tw_hi, w2_ref[d, 1]) + _dot(tw_lo, w2_ref[d, 0]))
        w_log = -_softplus(-(w0_ref[d:d + 1, :] + w_lora)) - 0.5
        lw = -jnp.exp(w_log)
        a = _sigmoid(a0_ref[d:d + 1, :] + _dot(ad.astype(BF16), a2_ref[d]))
        kd = k * (1.0 + (a - 1.0) * ka_ref[...])
        bon_s[d, pl.ds(c0, L), :] = _head_sum(r * kd * rk_ref[...], ones_bd, single_pass=True) * v
        if d == 0:
            g_s[pl.ds(c0, L), :] = _dot(_sigmoid(gd).astype(BF16), g2_ref[...])

        cum = _dot_exact_lhs(tri_ref[d], lw)
        e_pos = jnp.exp(cum)
        e_neg = jnp.exp(-cum)
        rh = r * e_pos
        kh = kd * e_neg
        bh = (kkn * a) * e_neg
        ah = -kkn * jnp.exp(cum - lw)
        last = L - 1 if d == 0 else 0
        gam = e_pos[last:last + 1, :]
        bhg = bh * gam
        khg = kh * gam

        chains = []
        for p in range(RW_PAIRS):
            cols = slice(p * LANES, (p + 1) * LANES)
            chains.append(dict(
                d=d, p=p, gam=gam[:, cols], v_st=stack(v[:, cols]),
                lhs=jnp.concatenate([stack(ah[:, cols]), stack(rh[:, cols])], axis=0),
                rhs=jnp.concatenate([stack(bh[:, cols]), stack(kh[:, cols])], axis=0),
                bk=jnp.concatenate([stack(bhg[:, cols]), stack(khg[:, cols])], axis=0)))
        return chains

    n_levels = int(math.log2(L))

    def state_free(chains):
        ri = lax.broadcasted_iota(jnp.int32, (L2, L2), 0)
        ci = lax.broadcasted_iota(jnp.int32, (L2, L2), 1)
        eye = jnp.where(ri == ci, 1.0, 0.0)
        rl = jnp.where(ri < L, ri, ri - L)
        cl = jnp.where(ci < L, ci, ci - L)
        strict = (cl < rl, cl > rl)
        incl = (cl <= rl, cl >= rl)
        level = ri ^ ci

        for ch in chains:
            ch['sc'] = _dot_nt(ch['lhs'], ch['rhs'])
        for ch in chains:
            sc, d = ch['sc'], ch['d']
            n_ab = jnp.where(strict[d], sc[0:L2, 0:L2], 0.0)
            m_ak = jnp.where(strict[d], sc[0:L2, L2:2 * L2], 0.0).astype(BF16)
            t_rb = jnp.where(incl[d], sc[L2:2 * L2, 0:L2], 0.0)
            t_rk = jnp.where(incl[d], sc[L2:2 * L2, L2:2 * L2], 0.0)
            ch['t_rbk'] = jnp.concatenate([t_rb, t_rk], axis=1).astype(BF16)
            ch['mv'] = _dot(m_ak, ch['v_st'])
            ch['t'] = eye + jnp.where(level == 1, n_ab, 0.0)
            ch['n_lv'] = [jnp.where(lax.shift_right_logical(level, lv) == 1, n_ab, 0.0).astype(BF16)
                          for lv in range(1, n_levels)]
            del ch['sc']
        for lv in range(n_levels - 1):
            for ch in chains:
                ch['tb'] = ch['t'].astype(BF16)
                ch['x'] = _dot(ch['n_lv'][lv], ch['tb']).astype(BF16)
            for ch in chains:
                ch['t'] = ch['t'] + _dot(ch['tb'], ch['x'])

    def state_step(chains, c0s):
        for ch in chains:
            ch['s_prev'] = s_s[ch['d'], ch['p']]
            ch['ar_s'] = _dot_nt(ch['lhs'], ch['s_prev'].astype(BF16))
        for ch in chains:
            u_rhs = ch['ar_s'][0:L2] + ch['mv']
            u_st = _dot(ch['t'].astype(BF16), u_rhs.astype(BF16)).astype(BF16)
            ch['uv'] = jnp.concatenate([u_st, ch['v_st']], axis=0)
        for ch in chains:
            y_st = ch['ar_s'][L2:2 * L2] + _dot(ch['t_rbk'], ch['uv'])
            ch['y'] = y_st[0:L] + y_st[L:L2]
            s_s[ch['d'], ch['p']] = ch['s_prev'] * ch['gam'] + _dot_tn(ch['uv'], ch['bk'])
        for d in range(2):
            yd_s[d, pl.ds(c0s[d], L), :] = jnp.concatenate([ch['y'] for ch in chains if ch['d'] == d], axis=1)

    steps = 2

    def body(i, carry):
        groups = []
        for j in range(steps):
            step = i * steps + j
            c0s = (pl.multiple_of(step * L, L), pl.multiple_of((n_ch - 1 - step) * L, L))
            groups.append((prep(c0s[0], 0) + prep(c0s[1], 1), c0s))
        state_free([ch for chains, _ in groups for ch in chains])
        for chains, c0s in groups:
            state_step(chains, c0s)
        return carry

    lax.fori_loop(0, n_ch // steps, body, 0)

    def epilogue(i, carry):
        c0 = pl.multiple_of(i * L, L)
        ones_bd = bd_ref[...]
        ys = yd_s[0, pl.ds(c0, L), :] + yd_s[1, pl.ds(c0, L), :]
        mu = _head_sum(ys, ones_bd) * (1.0 / RW_HEAD)
        yc = ys - mu
        var = _head_sum(yc * yc, ones_bd) * (1.0 / RW_HEAD)
        yn = yc * lax.rsqrt(var + RW_LN_EPS) * lnw_ref[...] + lnb_ref[...]
        out = (yn + (bon_s[0, pl.ds(c0, L), :] + bon_s[1, pl.ds(c0, L), :])) * g_s[pl.ds(c0, L), :]
        y_ref[pl.ds(c0, L), :] = out.astype(y_ref.dtype)
        return carry

    lax.fori_loop(0, n_ch, epilogue, 0, unroll=2)
    for d in range(2):
        for p in range(RW_PAIRS):
            s_pair = s_s[d, p]
            sfin_ref[d, 2 * p] = s_pair[0:RW_HEAD, 0:RW_HEAD]
            sfin_ref[d, 2 * p + 1] = s_pair[RW_HEAD:, RW_HEAD:]


def _rwkv_mixer(z, s0, p, n_batch, t_len):
    L = SCAN_CHUNK
    c2 = lambda b: (0, 0)
    c3 = lambda b: (0, 0, 0)
    st_spec = pl.BlockSpec((None, 2, RW_HEADS, RW_HEAD, RW_HEAD), lambda b: (b, 0, 0, 0, 0))
    return pl.pallas_call(
        functools.partial(_rwkv_kernel, t_len=t_len),
        grid=(n_batch,),
        in_specs=[
            pl.BlockSpec((t_len, RW_ZCOLS), lambda b: (b, 0)),
            st_spec,
            pl.BlockSpec((2, RW_ZCOLS), c2),
            pl.BlockSpec((1, D_RW), c2),
            pl.BlockSpec((1, D_RW), c2),
            pl.BlockSpec((1, D_RW), c2),
            pl.BlockSpec((RW_G_LORA, D_RW), c2),
            pl.BlockSpec((2, D_RW), c2),
            pl.BlockSpec((2, 2, LANES, D_RW), lambda b: (0, 0, 0, 0)),
            pl.BlockSpec((2, D_RW), c2),
            pl.BlockSpec((2, LANES, D_RW), c3),
            pl.BlockSpec((1, D_RW), c2),
            pl.BlockSpec((1, D_RW), c2),
            pl.BlockSpec((2, L, L), c3),
            pl.BlockSpec((LANES, LANES), c2),
        ],
        out_specs=[pl.BlockSpec((t_len, D_RW), lambda b: (b, 0)), st_spec],
        out_shape=[
            jax.ShapeDtypeStruct((n_batch * t_len, D_RW), Y_DTYPE),
            jax.ShapeDtypeStruct((n_batch, 2, RW_HEADS, RW_HEAD, RW_HEAD), F32),
        ],
        scratch_shapes=[
            pltpu.VMEM((2, L + 2 * HALO, RW_ZCOLS), F32),
            pltpu.VMEM((2, RW_PAIRS, LANES, LANES), F32),
            pltpu.VMEM((2, t_len, D_RW), F32),
            pltpu.VMEM((2, t_len, D_RW), F32),
            pltpu.VMEM((t_len, D_RW), F32),
        ],
        compiler_params=_ARB1,
    )(z, s0, p['mu'], p['kk'], p['ka'], p['rk'], p['g2'], p['w0'], p['w2'], p['a0'], p['a2'],
      p['ln_w'], p['ln_b'], p['tri'], p['ones_bd'])


def _mlstm_kernel(z_ref, c0_ref, n0_ref, m0_ref, cw_ref, cb_ref, gb_ref, nw_ref, tri_ref, esel_ref, shift_ref,
                  ones_ref, y_ref, cfin_ref, nfin_ref, mfin_ref, c_s, n_s, m_s, hd_s, qk_s, gx_s, gr_s,
                  *, t_len):
    L = ML_SCAN
    n_ch = t_len // L
    c_s[...] = c0_ref[...]
    n_s[...] = n0_ref[...]
    m_s[...] = m0_ref[...]
    q_scale = ML_DK ** -0.5
    g_lo = 4 * D_ML
    n_real = 4 * ML_HEADS

    def conv_chunk(i, carry):
        c0 = pl.multiple_of(i * L, L)
        gts = z_ref[pl.ds(c0, L), g_lo:g_lo + LANES].astype(F32) + gb_ref[...]
        x16 = gts.T[0:n_real, :]
        lf_hi, lf_lo = _split2(-_softplus(-x16))
        lane = lax.broadcasted_iota(jnp.int32, (8, L), 1)
        stats = []
        for d in range(2):
            b16 = _dot_nt(lf_hi, tri_ref[d]) + _dot_nt(lf_lo, tri_ref[d])
            x8 = x16[8 * d:8 * d + 8, :]
            b8 = b16[8 * d:8 * d + 8, :]
            g8 = pltpu.roll(x8, ML_HEADS, axis=0) - b8
            stats.append([g8, b8, g8])
        step = 1
        while step < L:
            for d in range(2):
                cm8 = stats[d][2]
                if d == 0:
                    sh = jnp.where(lane >= step, pltpu.roll(cm8, step, axis=1), -jnp.inf)
                else:
                    sh = jnp.where(lane < L - step, pltpu.roll(cm8, L - step, axis=1), -jnp.inf)
                stats[d][2] = jnp.maximum(cm8, sh)
            step *= 2

        ext = _load_ext(z_ref, c0, L, t_len, 0, 2 * D_ML, raw=True)
        taps = _dot(shift_ref[...], ext)
        qk = cb_ref[...] + ext[HALO:HALO + L].astype(F32) * cw_ref[CONV_LEFT:CONV_LEFT + 1, :]
        for n, j in enumerate(jj for jj in range(CONV_W) if jj != CONV_LEFT):
            qk = qk + taps[n * L:(n + 1) * L] * cw_ref[j:j + 1, :]
        qk = _silu(qk)
        qk_s[pl.ds(c0, L), 0:D_ML] = (qk[:, :D_ML] * q_scale).astype(BF16)
        qk_s[pl.ds(c0, L), D_ML:2 * D_ML] = qk[:, D_ML:].astype(BF16)

        for d, (g8, b8, cm8) in enumerate(stats):
            x32 = jnp.concatenate([g8, b8, cm8, jnp.zeros((8, L), F32)], axis=0)
            hi = x32.astype(BF16)
            r1 = x32 - hi.astype(F32)
            mid = r1.astype(BF16)
            lo = (r1 - mid.astype(F32)).astype(BF16)
            gx_s[d, i] = jnp.concatenate([hi, mid, lo], axis=0)
            gr_s[d, i] = g8
        return carry

    lax.fori_loop(0, n_ch, conv_chunk, 0, unroll=2)

    def prep(ci, d):
        c0 = pl.multiple_of(ci * L, L)
        q = qk_s[pl.ds(c0, L), 0:D_ML]
        k = qk_s[pl.ds(c0, L), D_ML:2 * D_ML]
        v = z_ref[pl.ds(c0, L), 2 * D_ML:3 * D_ML]
        g8 = gr_s[d, ci]
        cb = _dot_tn(gx_s[d, ci], esel_ref[...])
        ri = lax.broadcasted_iota(jnp.int32, (L, L), 0)
        ci = lax.broadcasted_iota(jnp.int32, (L, L), 1)
        causal = (ci <= ri) if d == 0 else (ci >= ri)
        last = L - 1 if d == 0 else 0
        ones = ones_ref[...]
        chains = []
        for h in range(ML_HEADS):
            cols = slice(h * ML_DK, (h + 1) * ML_DK)
            chains.append(dict(
                d=d, h=h, last=last, causal=causal,
                g_row=g8[ML_HEADS + h:ML_HEADS + h + 1, :],
                g_b=cb[:, cols], b_b=cb[:, D_ML + h * ML_DK:D_ML + (h + 1) * ML_DK],
                cm_b=cb[:, 2 * D_ML + h * ML_DK:2 * D_ML + (h + 1) * ML_DK],
                qb=q[:, cols], kb=k[:, cols],
                v=v[:, cols].astype(F32), v_ones=jnp.concatenate([v[:, cols].astype(BF16), ones], axis=1)))
        return chains

    def body(i, carry):
        c0s = (pl.multiple_of(i * L, L), pl.multiple_of((n_ch - 1 - i) * L, L))
        chains = prep(i, 0) + prep(n_ch - 1 - i, 1)
        for ch in chains:
            d, h = ch['d'], ch['h']
            ch['qk'] = _dot_nt(ch['qb'], ch['kb'])
            ch['c_prev'] = c_s[d, h]
            ch['n_prev'] = n_s[d, h]
            ch['m_prev'] = m_s[d, h:h + 1, :]
            ch['qcn'] = _dot(ch['qb'], jnp.concatenate([ch['c_prev'].astype(BF16),
                                                        ch['n_prev'].astype(BF16)], axis=1))
        for ch in chains:
            mu = jnp.maximum(ch['m_prev'], ch['cm_b'])
            ch['mu'] = mu
            ch['s'] = ch['qk'] * jnp.exp(jnp.where(ch['causal'], ch['g_row'] - mu, -jnp.inf))
            ch['w_inter'] = jnp.exp(ch['m_prev'] - mu)
        for ch in chains:
            last = ch['last']
            ch['sv'] = _dot(ch['s'].astype(BF16), ch['v_ones'])
            mu_last = ch['mu'][last:last + 1, :]
            ch['m_new'] = ch['b_b'][last:last + 1, :] + mu_last
            ch['dec'] = jnp.exp(ch['m_prev'] - mu_last)
            w_s = jnp.exp(ch['g_b'] - mu_last)
            ch['upd'] = _dot_tn(ch['kb'], jnp.concatenate([(w_s * ch['v']).astype(BF16),
                                                           w_s.astype(BF16)], axis=1))
        for ch in chains:
            d, h = ch['d'], ch['h']
            num = ch['sv'][:, :ML_DK] + ch['w_inter'] * ch['qcn'][:, :ML_DK]
            den = ch['sv'][:, ML_DK:] + ch['w_inter'] * ch['qcn'][:, ML_DK:]
            ch['out'] = num / jnp.maximum(jnp.abs(den), jnp.exp(-(ch['b_b'] + ch['mu'])))
            c_s[d, h] = ch['dec'] * ch['c_prev'] + ch['upd'][:, :ML_DK]
            n_s[d, h] = ch['dec'] * ch['n_prev'] + ch['upd'][:, ML_DK:]
            m_s[d, h:h + 1, :] = ch['m_new']
        for d in range(2):
            hd_s[d, pl.ds(c0s[d], L), :] = jnp.concatenate([ch['out'] for ch in chains if ch['d'] == d], axis=1)
        return carry

    lax.fori_loop(0, n_ch, body, 0, unroll=2)

    def epilogue(i, carry):
        c0 = pl.multiple_of(i * L, L)
        ones = ones_ref[...]
        hs = hd_s[0, pl.ds(c0, L), :] + hd_s[1, pl.ds(c0, L), :]
        o = z_ref[pl.ds(c0, L), 3 * D_ML:4 * D_ML].astype(F32)
        mu = _head_sum(hs, ones) * (1.0 / ML_DK)
        xc = hs - mu
        var = _head_sum(xc * xc, ones) * (1.0 / ML_DK)
        y = xc * lax.rsqrt(var + NORM_EPS) * nw_ref[...] * _sigmoid(o)
        y_ref[pl.ds(c0, L), :] = y.astype(y_ref.dtype)
        return carry

    lax.fori_loop(0, n_ch, epilogue, 0, unroll=2)
    cfin_ref[...] = c_s[...]
    nfin_ref[...] = n_s[...]
    mfin_ref[...] = m_s[...]


def _mlstm_mixer(z, c0, n0, m0, p, n_batch, t_len):
    assert z.dtype == BF16
    L = ML_SCAN
    c2 = lambda b: (0, 0)
    c3 = lambda b: (0, 0, 0)
    c_spec = pl.BlockSpec((None, 2, ML_HEADS, ML_DK, ML_DK), lambda b: (b, 0, 0, 0, 0))
    m_spec = pl.BlockSpec((None, 2, ML_HEADS, ML_DK), lambda b: (b, 0, 0, 0))
    return pl.pallas_call(
        functools.partial(_mlstm_kernel, t_len=t_len),
        grid=(n_batch,),
        in_specs=[
            pl.BlockSpec((t_len, ML_ZCOLS), lambda b: (b, 0)),
            c_spec, c_spec, m_spec,
            pl.BlockSpec((CONV_W, 2 * D_ML), c2),
            pl.BlockSpec((1, 2 * D_ML), c2),
            pl.BlockSpec((1, LANES), c2),
            pl.BlockSpec((1, D_ML), c2),
            pl.BlockSpec((2, L, L), c3),
            pl.BlockSpec(p['esel'].shape, c2),
            pl.BlockSpec(p['shift'].shape, c2),
            pl.BlockSpec((L, LANES), c2),
        ],
        out_specs=[pl.BlockSpec((t_len, D_ML), lambda b: (b, 0)), c_spec, c_spec, m_spec],
        out_shape=[
            jax.ShapeDtypeStruct((n_batch * t_len, D_ML), Y_DTYPE),
            jax.ShapeDtypeStruct((n_batch, 2, ML_HEADS, ML_DK, ML_DK), F32),
            jax.ShapeDtypeStruct((n_batch, 2, ML_HEADS, ML_DK, ML_DK), F32),
            jax.ShapeDtypeStruct((n_batch, 2, ML_HEADS, ML_DK), F32),
        ],
        scratch_shapes=[
            pltpu.VMEM((2, ML_HEADS, ML_DK, ML_DK), F32),
            pltpu.VMEM((2, ML_HEADS, ML_DK, ML_DK), F32),
            pltpu.VMEM((2, ML_HEADS, ML_DK), F32),
            pltpu.VMEM((2, t_len, D_ML), F32),
            pltpu.VMEM((t_len, 2 * D_ML), BF16),
            pltpu.VMEM((2, t_len // L, 96, L), BF16),
            pltpu.VMEM((2, t_len // L, 8, L), F32),
        ],
        compiler_params=_ARB1,
    )(z, c0, n0, m0, p['conv_w'], p['conv_b'], p['gate_b'], p['norm_w'], p['tri'], p['esel'], p['shift'],
      p['ones'])


def _conv_shift(n):
    taps = jnp.array([j for j in range(CONV_W) if j != CONV_LEFT])
    t = jnp.arange((CONV_W - 1) * n)
    src = HALO + t % n + taps[t // n] - CONV_LEFT
    return (src[:, None] == jnp.arange(n + 2 * HALO)[None, :]).astype(BF16)


def _mlstm_consts():
    L = ML_SCAN
    lower = jnp.tril(jnp.ones((L, L), F32))
    tri = jnp.stack([lower, lower.T]).astype(BF16)
    rows = jnp.arange(32)
    blk = jnp.arange(3 * D_ML) // ML_DK
    src_row = 8 * (blk // ML_HEADS) + ML_HEADS + blk % ML_HEADS
    esel = (rows[:, None] == src_row[None, :]).astype(BF16)
    esel = jnp.concatenate([esel, esel, esel], axis=0)
    return dict(tri=tri, esel=esel, shift=_conv_shift(L), ones=jnp.ones((L, LANES), BF16))


def _block_diag(w):
    n, c, d = w.shape
    return jnp.einsum('ncd,nm->ncmd', w, jnp.eye(n, dtype=w.dtype)).reshape(n * c, n * d)


def _scan_tri():
    L = SCAN_CHUNK
    lower = jnp.tril(jnp.ones((L, L), F32))
    return jnp.stack([lower, lower.T]).astype(BF16)


def _prep_even(ev_w_in, pool_w, pool_scale, rg_conv_w, rg_conv_b, rg_wa, rg_ba, rg_wx, rg_bx, rg_lam, ev_w_out, j):
    wg = jnp.concatenate([_block_diag(rg_wa[j, 0]), _block_diag(rg_wx[j, 0]),
                          _block_diag(rg_wa[j, 1]), _block_diag(rg_wx[j, 1])], axis=1)
    bg = jnp.concatenate([rg_ba[j, 0], rg_bx[j, 0], rg_ba[j, 1], rg_bx[j, 1]]).reshape(1, 4 * D_RG)
    return dict(
        w_in=[ev_w_in[j].astype(BF16)],
        w_out=ev_w_out[j].astype(BF16),
        pool_w=pool_w[j].astype(BF16),
        pool_scale=pool_scale[j].reshape(1, D_POOL),
        conv_w=rg_conv_w[j],
        conv_b=rg_conv_b[j].reshape(1, D_RG),
        wg=wg.astype(BF16),
        bg=bg,
        lam=rg_lam[j],
    )


def _pad_cols(w, cuts, widths):
    pieces = jnp.split(w, cuts, axis=-1)
    out = []
    for piece, width in zip(pieces, widths):
        extra = width - piece.shape[-1]
        if extra:
            piece = jnp.pad(piece, [(0, 0)] * (piece.ndim - 1) + [(0, extra)])
        out.append(piece)
    return jnp.concatenate(out, axis=-1)


def _prep_odd(od_w_in, rw_mu, rw_w0, rw_w2, rw_a0, rw_a2, rw_kk, rw_ka, rw_rk, rw_g2, rw_ln_w, rw_ln_b,
              ml_conv_w, ml_conv_b, ml_bi, ml_bf, ml_norm_w, od_w_out, j):
    rw_cols = 3 * D_RW + RW_G_LORA + RW_W_LORA + RW_A_LORA
    base = 3 * D_RW + RW_G_LORA
    cuts = [base, base + RW_W_LORA]
    widths = [base, LANES, LANES]
    w_rw = _pad_cols(od_w_in[j][:, :rw_cols], cuts, widths)
    w_ml = _pad_cols(od_w_in[j][:, rw_cols:], [4 * D_ML], [4 * D_ML, LANES])
    pad_rows = lambda w: jnp.pad(w, ((0, 0), (0, LANES - w.shape[1]), (0, 0)))
    gate_b = jnp.concatenate([ml_bi[j, 0], ml_bf[j, 0], ml_bi[j, 1], ml_bf[j, 1]])
    gate_b = jnp.pad(gate_b, (0, LANES - gate_b.shape[0])).reshape(1, LANES)
    hm = jnp.arange(LANES) // RW_HEAD
    rw = dict(
        mu=_pad_cols(rw_mu[j], cuts, widths),
        kk=rw_kk[j].reshape(1, D_RW), ka=rw_ka[j].reshape(1, D_RW), rk=rw_rk[j].reshape(1, D_RW),
        g2=rw_g2[j].astype(BF16),
        w0=rw_w0[j], w2=jnp.stack(_split2(pad_rows(rw_w2[j])), axis=1),
        a0=rw_a0[j], a2=pad_rows(rw_a2[j]).astype(BF16),
        ln_w=rw_ln_w[j].reshape(1, D_RW), ln_b=rw_ln_b[j].reshape(1, D_RW),
        tri=_scan_tri(),
        ones_bd=(hm[:, None] == hm[None, :]).astype(BF16),
    )
    ml = dict(
        conv_w=ml_conv_w[j], conv_b=ml_conv_b[j].reshape(1, 2 * D_ML),
        gate_b=gate_b, norm_w=ml_norm_w[j].reshape(1, D_ML), **_mlstm_consts(),
    )
    return dict(w_in=[w_rw.astype(BF16), w_ml.astype(BF16)], w_out=od_w_out[j].astype(BF16), rw=rw, ml=ml)


def _trunk(x, mod, rows_per_mod, mod_base, grid, states, params, n_batch, t_len):
    ev, od, dense = params
    rg0, rw0, c0, n0, m0 = states
    x = x.reshape(n_batch * t_len, D_MODEL)
    spec = (rows_per_mod, mod_base)

    (z,) = _in_proj(x, mod[0], *spec, dense['norm1_w'][0], ev['w_in'])
    y_ev, rg_f = _even_mixer(z, rg0, ev, n_batch, t_len, grid)
    x = _out_mlp([y_ev], x, mod[0], *spec, dense['norm2_w'][0], ev['w_out'],
                 dense['mlp_w1'], dense['mlp_w2'], 0, dense['final_norm_w'], False)

    z_rw, z_ml = _in_proj(x, mod[1], *spec, dense['norm1_w'][1], od['w_in'])
    y_rw, s_f = _rwkv_mixer(z_rw, rw0, od['rw'], n_batch, t_len)
    y_ml, c_f, n_f, m_f = _mlstm_mixer(z_ml, c0, n0, m0, od['ml'], n_batch, t_len)
    y = _out_mlp([y_rw, y_ml], x, mod[1], *spec, dense['norm2_w'][1], od['w_out'],
                 dense['mlp_w1'], dense['mlp_w2'], 1, dense['final_norm_w'], True)
    return y.reshape(n_batch, t_len, D_MODEL), (rg_f, s_f, c_f, n_f, m_f)


def kernel(x_prompt, x_sample, state_rglru, state_rwkv, state_mlstm_C, state_mlstm_n, state_mlstm_m,
           c, c_ctx, norm1_w, norm2_w, w_mod, b_mod, mlp_w1, mlp_w2, final_norm_w,
           ev_w_in, pool_w, pool_scale, rg_conv_w, rg_conv_b, rg_wa, rg_ba, rg_wx, rg_bx, rg_lam, ev_w_out,
           od_w_in, rw_mu, rw_w0, rw_w2, rw_a0, rw_a2, rw_kk, rw_ka, rw_rk, rw_g2, rw_ln_w, rw_ln_b,
           ml_conv_w, ml_conv_b, ml_bi, ml_bf, ml_norm_w, od_w_out):
    bp, tp, _ = x_prompt.shape
    bs, ts, _ = x_sample.shape

    n_cond = 16
    cond = jnp.concatenate([c_ctx[None, :], c, jnp.zeros((n_cond - 1 - bs, D_MODEL), F32)], axis=0)
    mod = _modulation(cond, w_mod, b_mod)

    ev = _prep_even(ev_w_in, pool_w, pool_scale, rg_conv_w, rg_conv_b, rg_wa, rg_ba, rg_wx, rg_bx, rg_lam,
                    ev_w_out, 0)
    od = _prep_odd(od_w_in, rw_mu, rw_w0, rw_w2, rw_a0, rw_a2, rw_kk, rw_ka, rw_rk, rw_g2, rw_ln_w, rw_ln_b,
                   ml_conv_w, ml_conv_b, ml_bi, ml_bf, ml_norm_w, od_w_out, 0)
    dense = dict(norm1_w=norm1_w, norm2_w=norm2_w, mlp_w1=mlp_w1.astype(BF16), mlp_w2=mlp_w2.astype(BF16),
                 final_norm_w=final_norm_w)
    params = (ev, od, dense)

    zero_states = (
        jnp.zeros((bp, 2, D_RG), F32),
        jnp.zeros((bp, 2, RW_HEADS, RW_HEAD, RW_HEAD), F32),
        jnp.zeros((bp, 2, ML_HEADS, ML_DK, ML_DK), F32),
        jnp.zeros((bp, 2, ML_HEADS, ML_DK, ML_DK), F32),
        jnp.zeros((bp, 2, ML_HEADS, ML_DK), F32),
    )
    y_prompt, (rg_f, s_f, c_f, n_f, m_f) = _trunk(
        x_prompt, mod, bp * tp, 0, False, zero_states, params, bp, tp)

    sample_states = (
        state_rglru[:, 0],
        state_rwkv[:, 0],
        state_mlstm_C[:, 0],
        jnp.broadcast_to(state_mlstm_n[:, 0][..., None], (bs, 2, ML_HEADS, ML_DK, ML_DK)),
        jnp.broadcast_to(state_mlstm_m[:, 0][..., None], (bs, 2, ML_HEADS, ML_DK)),
    )
    y_sample, _ = _trunk(x_sample, mod, ts, 1, True, sample_states, params, bs, ts)

    return (y_prompt, y_sample,
            rg_f[:, None],
            s_f[:, None],
            c_f[:, None],
            n_f[:, None, :, :, :, 0],
            m_f[:, None, :, :, 0])
```

```python
import functools
import math

import jax
import jax.numpy as jnp
from jax import lax
from jax.experimental import pallas as pl
from jax.experimental.pallas import tpu as pltpu

F32 = jnp.float32
BF16 = jnp.bfloat16

D_MODEL = 1024
DEPTH = 2
GRID_W = 64
D_FF = 4 * D_MODEL
NORM_EPS = 1e-6
N_MOD = 6

D_POOL = D_MODEL // 2
POOL_WINDOWS = (2, 4, 8, 16)
POOL_GW = D_POOL // len(POOL_WINDOWS)
D_RG = D_MODEL // 2
RG_BLOCKS = 8
RG_C = 8.0
CONV_W = 4
CONV_LEFT = 2
EV_COLS = D_POOL + 2 * D_RG

D_RW = D_MODEL // 2
RW_HEAD = 64
RW_HEADS = D_RW // RW_HEAD
RW_PAIRS = RW_HEADS // 2
RW_W_LORA = 64
RW_A_LORA = 64
RW_G_LORA = 128
RW_LN_EPS = 64e-5
RW_ZCOLS = 3 * D_RW + RW_G_LORA + 128 + 128
D_ML = D_MODEL // 2
ML_HEADS = 4
ML_DK = D_ML // ML_HEADS
ML_ZCOLS = 4 * D_ML + 128

LANES = 128
HALO = 16
Z_DTYPE = BF16
Y_DTYPE = BF16
ROW_BLOCK = 512
SCAN_CHUNK = 64
ML_SCAN = 128
RG_CHUNK = 256
POOL_SEGS_PER_TRIP = 8
ML_UNROLL = 4
EPILOGUE_UNROLL = 8
MOD_COL_BLOCK = 1536
N_COND_ROWS = 16
VMEM_LIMIT = 56 * 1024 * 1024

_ARB1 = pltpu.CompilerParams(dimension_semantics=("arbitrary",), vmem_limit_bytes=VMEM_LIMIT)
_ARB2 = pltpu.CompilerParams(dimension_semantics=("arbitrary", "arbitrary"), vmem_limit_bytes=VMEM_LIMIT)


def _dot(a, b):
    return jnp.dot(a, b, preferred_element_type=F32)


def _dot_nt(a, b):
    return lax.dot_general(a, b, (((1,), (1,)), ((), ())), preferred_element_type=F32)


def _dot_tn(a, b):
    return lax.dot_general(a, b, (((0,), (0,)), ((), ())), preferred_element_type=F32)


def _split2(x):
    hi = x.astype(BF16)
    lo = (x - hi.astype(F32)).astype(BF16)
    return hi, lo


def _dot_f32(a, b):
    ah, al = _split2(a)
    bh, bl = _split2(b)
    return _dot(ah, bh) + (_dot(ah, bl) + _dot(al, bh))


def _dot_exact_lhs(m_bf16, x):
    hi, lo = _split2(x)
    return _dot(m_bf16, hi) + _dot(m_bf16, lo)


def _dot_exact_rhs(x, m_bf16):
    hi, lo = _split2(x)
    return _dot(hi, m_bf16) + _dot(lo, m_bf16)


def _softplus(x):
    return jnp.maximum(x, 0.0) + jnp.log1p(jnp.exp(-jnp.abs(x)))


def _sigmoid(x):
    return jax.nn.sigmoid(x)


def _silu(x):
    return x * jax.nn.sigmoid(x)


def _gelu_tanh(x):
    c = math.sqrt(2.0 / math.pi)
    return x * (0.5 * (1.0 + jnp.tanh(c * (x + 0.044715 * (x * x * x)))))


def _rms_norm(x, w):
    ms = jnp.mean(x * x, axis=-1, keepdims=True)
    return x * lax.rsqrt(ms + NORM_EPS) * w


def _load_ext(z_ref, c0, n, t_len, lo, hi, raw=False):
    dt = z_ref.dtype if raw else F32
    cur = z_ref[pl.ds(c0, n), lo:hi].astype(dt)
    pb = pl.multiple_of(jnp.maximum(c0 - HALO, 0), HALO)
    pa = pl.multiple_of(jnp.minimum(c0 + n, t_len - HALO), HALO)
    before = z_ref[pl.ds(pb, HALO), lo:hi].astype(dt)
    after = z_ref[pl.ds(pa, HALO), lo:hi].astype(dt)
    before = jnp.where(c0 > 0, before, jnp.zeros_like(before))
    after = jnp.where(c0 + n < t_len, after, jnp.zeros_like(after))
    return jnp.concatenate([before, cur, after], axis=0)


def _head_sum(x, ones_bd, single_pass=False):
    n = x.shape[0]
    groups = x.shape[1] // LANES
    xs = jnp.concatenate([x[:, g * LANES:(g + 1) * LANES] for g in range(groups)], axis=0)
    s = _dot(xs.astype(BF16), ones_bd) if single_pass else _dot_exact_rhs(xs, ones_bd)
    return jnp.concatenate([s[g * n:(g + 1) * n] for g in range(groups)], axis=1)


def _mod_kernel(c_ref, w_ref, b_ref, o_ref):
    s = _silu(c_ref[...])
    o_ref[...] = _dot_f32(s, w_ref[...]) + b_ref[...]


def _modulation(cond, w_mod, b_mod):
    rows = cond.shape[0]
    tn = MOD_COL_BLOCK
    out = pl.pallas_call(
        _mod_kernel,
        grid=(DEPTH, N_MOD * D_MODEL // tn),
        in_specs=[
            pl.BlockSpec((rows, D_MODEL), lambda l, j: (0, 0)),
            pl.BlockSpec((None, D_MODEL, tn), lambda l, j: (l, 0, j)),
            pl.BlockSpec((None, 1, tn), lambda l, j: (l, 0, j)),
        ],
        out_specs=pl.BlockSpec((None, rows, tn), lambda l, j: (l, 0, j)),
        out_shape=jax.ShapeDtypeStruct((DEPTH, rows, N_MOD * D_MODEL), F32),
        compiler_params=_ARB2,
    )(cond, w_mod, b_mod.reshape(DEPTH, 1, N_MOD * D_MODEL))
    return out.reshape(DEPTH, rows, N_MOD, D_MODEL)


def _inproj_kernel(*refs, n_out, col_chunk):
    x_ref, mod_ref, nw_ref = refs[:3]
    w_refs = refs[3:3 + n_out]
    z_refs = refs[3 + n_out:]
    h = _rms_norm(x_ref[...], nw_ref[...])
    h = h * (1.0 + mod_ref[1:2, :]) + mod_ref[0:1, :]
    hb = h.astype(BF16)
    for w_ref, z_ref in zip(w_refs, z_refs):
        n = w_ref.shape[1]
        for c in range(0, n, col_chunk):
            e = min(c + col_chunk, n)
            z_ref[:, c:e] = _dot(hb, w_ref[:, c:e]).astype(z_ref.dtype)


def _in_proj(x, mod, rows_per_mod, mod_base, norm_w, weights):
    m = x.shape[0]
    tm = ROW_BLOCK
    n_out = len(weights)
    const = lambda i: (0, 0)
    in_specs = [
        pl.BlockSpec((tm, D_MODEL), lambda i: (i, 0)),
        pl.BlockSpec((None, N_MOD, D_MODEL), lambda i: (mod_base + (i * tm) // rows_per_mod, 0, 0)),
        pl.BlockSpec((1, D_MODEL), const),
    ] + [pl.BlockSpec(w.shape, const, pipeline_mode=pl.Buffered(1)) for w in weights]
    out_specs = [pl.BlockSpec((tm, w.shape[1]), lambda i: (i, 0)) for w in weights]
    out_shape = [jax.ShapeDtypeStruct((m, w.shape[1]), Z_DTYPE) for w in weights]
    return pl.pallas_call(
        functools.partial(_inproj_kernel, n_out=n_out, col_chunk=512),
        grid=(m // tm,),
        in_specs=in_specs,
        out_specs=out_specs,
        out_shape=out_shape,
        compiler_params=_ARB1,
    )(x, mod, norm_w.reshape(1, D_MODEL), *weights)


def _outmlp_kernel(*refs, n_y, final, ff_chunk):
    y_refs = refs[:n_y]
    x_ref, mod_ref, nw_ref, wo_ref, w1_ref, w2_ref, fw_ref, o_ref = refs[n_y:]
    y = None
    row = 0
    for y_ref in y_refs:
        k = y_ref.shape[1]
        part = _dot(y_ref[...], wo_ref[row:row + k, :])
        y = part if y is None else y + part
        row += k
    x = x_ref[...] + mod_ref[2:3, :] * y
    h = _rms_norm(x, nw_ref[...]) * (1.0 + mod_ref[4:5, :]) + mod_ref[3:4, :]
    hb = h.astype(BF16)
    acc = None
    for c in range(0, D_FF, ff_chunk):
        u = jnp.maximum(_dot(hb, w1_ref[:, c:c + ff_chunk]), 0.0)
        part = _dot((u * u).astype(BF16), w2_ref[c:c + ff_chunk, :])
        acc = part if acc is None else acc + part
    x = x + mod_ref[5:6, :] * acc
    if final:
        x = _rms_norm(x, fw_ref[...])
    o_ref[...] = x


def _out_mlp(ys, x, mod, rows_per_mod, mod_base, norm_w, w_out, w1, w2, layer, final_w, final):
    m = x.shape[0]
    tm = ROW_BLOCK
    const = lambda i: (0, 0)
    single = dict(pipeline_mode=pl.Buffered(1))
    pick = lambda i: (layer, 0, 0)
    in_specs = [pl.BlockSpec((tm, y.shape[1]), lambda i: (i, 0)) for y in ys] + [
        pl.BlockSpec((tm, D_MODEL), lambda i: (i, 0)),
        pl.BlockSpec((None, N_MOD, D_MODEL), lambda i: (mod_base + (i * tm) // rows_per_mod, 0, 0)),
        pl.BlockSpec((1, D_MODEL), const),
        pl.BlockSpec(w_out.shape, const, **single),
        pl.BlockSpec((None,) + w1.shape[1:], pick, **single),
        pl.BlockSpec((None,) + w2.shape[1:], pick, **single),
        pl.BlockSpec((1, D_MODEL), const),
    ]
    return pl.pallas_call(
        functools.partial(_outmlp_kernel, n_y=len(ys), final=final, ff_chunk=1024),
        grid=(m // tm,),
        in_specs=in_specs,
        out_specs=pl.BlockSpec((tm, D_MODEL), lambda i: (i, 0)),
        out_shape=jax.ShapeDtypeStruct((m, D_MODEL), F32),
        compiler_params=_ARB1,
    )(*ys, x, mod, norm_w.reshape(1, D_MODEL), w_out, w1, w2, final_w.reshape(1, D_MODEL))


def _even_kernel(z_ref, h0_ref, pw_ref, ps_ref, cw_ref, cb_ref, wg_ref, bg_ref, lam_ref, band_ref,
                 y_ref, hfin_ref, cbuf, a_s, b_s, h_s, hf_s, *, t_len, grid):
    seg = GRID_W if grid else t_len
    n_seg = t_len // seg
    n_groups = len(POOL_WINDOWS)

    pos = lax.broadcasted_iota(jnp.int32, (seg, POOL_GW), 0)
    offs = [(w // 2, w - 1 - w // 2) for w in POOL_WINDOWS]
    gcols = [slice(g * POOL_GW, (g + 1) * POOL_GW) for g in range(n_groups)]

    def grid_row(rr, cols):
        valid = jnp.logical_and(rr >= 0, rr < n_seg)
        src = pl.multiple_of(jnp.clip(rr, 0, n_seg - 1) * seg, seg)
        return jnp.where(valid, z_ref[pl.ds(src, seg), cols].astype(F32), 0.0)

    def pool_rows(rows, run):
        items = []
        for r in rows:
            base = r * seg if isinstance(r, int) else pl.multiple_of(r * seg, seg)
            new_run = []
            for g, (lo_off, hi_off) in enumerate(offs):
                cols = gcols[g]
                if grid:
                    s1 = run[g] + grid_row(r + hi_off, cols) - grid_row(r - lo_off - 1, cols)
                    new_run.append(s1)
                    cnt_r = jnp.minimum(r + hi_off, n_seg - 1) - jnp.maximum(r - lo_off, 0) + 1
                    m1 = s1 / cnt_r.astype(F32)
                    hi = m1.astype(BF16)
                    r1 = m1 - hi.astype(F32)
                    mid = r1.astype(BF16)
                    parts = jnp.concatenate([hi, mid, (r1 - mid.astype(F32)).astype(BF16)], axis=0)
                else:
                    parts = z_ref[pl.ds(base, seg), cols]
                items.append(dict(g=g, base=base, parts=parts))
            run = new_run
        for it in items:
            it['s2'] = _dot(band_ref[it['g']], it['parts'])
        for it in items:
            lo_off, hi_off = offs[it['g']]
            cnt_c = jnp.minimum(pos + hi_off, seg - 1) - jnp.maximum(pos - lo_off, 0) + 1
            xg = z_ref[pl.ds(it['base'], seg), gcols[it['g']]].astype(F32)
            it['d'] = (it['s2'] / cnt_c.astype(F32) - xg).astype(BF16)
        for it in items:
            it['y'] = _dot(it['d'], pw_ref[it['g']])
        for it in items:
            cols = gcols[it['g']]
            y_ref[pl.ds(it['base'], seg), cols] = (it['y'] * ps_ref[:, cols]).astype(y_ref.dtype)
        return tuple(run) if grid else None

    if grid:
        init = []
        for g, (lo_off, hi_off) in enumerate(offs):
            s1 = jnp.zeros((seg, POOL_GW), F32)
            for o in range(min(hi_off, n_seg)):
                s1 = s1 + z_ref[o * seg:(o + 1) * seg, gcols[g]].astype(F32)
            init.append(s1)
        per_trip = POOL_SEGS_PER_TRIP
        lax.fori_loop(0, n_seg // per_trip,
                      lambda i, run: pool_rows([i * per_trip + j for j in range(per_trip)], run), tuple(init))
    else:
        pool_rows([0], None)

    ch = RG_CHUNK
    n_ch = t_len // ch
    rg_lo, rg_hi = D_POOL, D_POOL + D_RG

    def gates(c0, d):
        cbuf[...] = _load_ext(z_ref, c0, ch, t_len, rg_lo, rg_hi)
        xc = cb_ref[...]
        for j in range(CONV_W):
            off = HALO - CONV_LEFT + j
            xc = xc + cbuf[off:off + ch, :] * cw_ref[j:j + 1, :]
        g = _dot(xc.astype(BF16), wg_ref[:, d * 2 * D_RG:(d + 1) * 2 * D_RG]) \
            + bg_ref[:, d * 2 * D_RG:(d + 1) * 2 * D_RG]
        r = _sigmoid(g[:, :D_RG])
        i = _sigmoid(g[:, D_RG:])
        log_a = (-RG_C) * r * _softplus(-lam_ref[d:d + 1, :])
        a = jnp.exp(log_a)
        a_s[...] = a
        one_minus_a2 = jnp.tanh(-log_a) * (a * a + 1.0)
        root = jnp.where(one_minus_a2 > 0.0, one_minus_a2 * lax.rsqrt(one_minus_a2), 0.0)
        b_s[...] = root * (i * xc)

    def scan_rows(h, reverse):
        def row(t, h):
            tt = (ch - 1 - t) if reverse else t
            h = a_s[pl.ds(tt, 1), :] * h + b_s[pl.ds(tt, 1), :]
            h_s[pl.ds(tt, 1), :] = h
            return h
        return lax.fori_loop(0, ch, row, h, unroll=8)

    def fwd_body(i, h):
        c0 = pl.multiple_of(i * ch, ch)
        gates(c0, 0)
        h = scan_rows(h, False)
        hf_s[pl.ds(c0, ch), :] = h_s[...]
        return h

    h_fwd = lax.fori_loop(0, n_ch, fwd_body, h0_ref[0:1, :])

    def bwd_body(i, h):
        c0 = pl.multiple_of((n_ch - 1 - i) * ch, ch)
        gates(c0, 1)
        h = scan_rows(h, True)
        u_gate = z_ref[pl.ds(c0, ch), rg_hi:rg_hi + D_RG].astype(F32)
        y = (hf_s[pl.ds(c0, ch), :] + h_s[...]) * _gelu_tanh(u_gate)
        y_ref[pl.ds(c0, ch), D_POOL:D_POOL + D_RG] = y.astype(y_ref.dtype)
        return h

    h_bwd = lax.fori_loop(0, n_ch, bwd_body, h0_ref[1:2, :])
    hfin_ref[0:1, :] = h_fwd
    hfin_ref[1:2, :] = h_bwd


def _even_mixer(z, h0, p, n_batch, t_len, grid):
    assert z.dtype == BF16
    seg = GRID_W if grid else t_len
    const2 = lambda b: (0, 0)
    const3 = lambda b: (0, 0, 0)
    idx = jnp.arange(seg)
    delta = idx[None, :] - idx[:, None]
    band = jnp.stack([jnp.logical_and(delta >= -(w // 2), delta <= w - 1 - w // 2) for w in POOL_WINDOWS])
    band = jnp.tile(band.astype(BF16), (1, 1, 3 if grid else 1))
    return pl.pallas_call(
        functools.partial(_even_kernel, t_len=t_len, grid=grid),
        grid=(n_batch,),
        in_specs=[
            pl.BlockSpec((t_len, EV_COLS), lambda b: (b, 0)),
            pl.BlockSpec((None, 2, D_RG), lambda b: (b, 0, 0)),
            pl.BlockSpec(p['pool_w'].shape, const3),
            pl.BlockSpec((1, D_POOL), const2),
            pl.BlockSpec((CONV_W, D_RG), const2),
            pl.BlockSpec((1, D_RG), const2),
            pl.BlockSpec((D_RG, 4 * D_RG), const2),
            pl.BlockSpec((1, 4 * D_RG), const2),
            pl.BlockSpec((2, D_RG), const2),
            pl.BlockSpec(band.shape, const3),
        ],
        out_specs=[
            pl.BlockSpec((t_len, D_POOL + D_RG), lambda b: (b, 0)),
            pl.BlockSpec((None, 2, D_RG), lambda b: (b, 0, 0)),
        ],
        out_shape=[
            jax.ShapeDtypeStruct((n_batch * t_len, D_POOL + D_RG), Y_DTYPE),
            jax.ShapeDtypeStruct((n_batch, 2, D_RG), F32),
        ],
        scratch_shapes=[
            pltpu.VMEM((RG_CHUNK + 2 * HALO, D_RG), F32),
            pltpu.VMEM((RG_CHUNK, D_RG), F32),
            pltpu.VMEM((RG_CHUNK, D_RG), F32),
            pltpu.VMEM((RG_CHUNK, D_RG), F32),
            pltpu.VMEM((t_len, D_RG), F32),
        ],
        compiler_params=_ARB1,
    )(z, h0, p['pool_w'], p['pool_scale'], p['conv_w'], p['conv_b'], p['wg'], p['bg'], p['lam'], band)


def _rwkv_kernel(z_ref, s0_ref, mu_ref, kkw_ref, ka_ref, rk_ref, g2_ref, w0_ref, w2_ref, a0_ref, a2_ref,
                 lnw_ref, lnb_ref, tri_ref, bd_ref,
                 y_ref, sfin_ref, ebuf, s_s, yd_s, bon_s, g_s, *, t_len):
    L = SCAN_CHUNK
    n_ch = t_len // L
    L2 = 2 * L
    zero = jnp.zeros((RW_HEAD, RW_HEAD), F32)
    for d in range(2):
        for p in range(RW_PAIRS):
            top = jnp.concatenate([s0_ref[d, 2 * p], zero], axis=1)
            bot = jnp.concatenate([zero, s0_ref[d, 2 * p + 1]], axis=1)
            s_s[d, p] = jnp.concatenate([top, bot], axis=0)

    def stack(x):
        first_head = lax.broadcasted_iota(jnp.int32, (L, LANES), 1) < RW_HEAD
        return jnp.concatenate([jnp.where(first_head, x, 0.0), jnp.where(first_head, 0.0, x)],
                               axis=0).astype(BF16)

    def prep(c0, d):
        ones_bd = bd_ref[...]
        ebuf[d] = _load_ext(z_ref, c0, L, t_len, 0, RW_ZCOLS)
        cur = ebuf[d, HALO:HALO + L, :]
        prev = ebuf[d, HALO - 1:HALO - 1 + L, :]
        nxt = ebuf[d, HALO + 1:HALO + 1 + L, :]
        zs = cur + mu_ref[0:1, :] * (prev - cur) + mu_ref[1:2, :] * (nxt - cur)
        r = zs[:, 0:D_RW]
        k = zs[:, D_RW:2 * D_RW]
        v = zs[:, 2 * D_RW:3 * D_RW]
        o = 3 * D_RW
        gd = zs[:, o:o + RW_G_LORA]
        wd = zs[:, o + RW_G_LORA:o + RW_G_LORA + LANES]
        ad = zs[:, o + RW_G_LORA + LANES:o + RW_G_LORA + 2 * LANES]

        kkv = k * kkw_ref[...]
        nrm = jnp.sqrt(_head_sum(kkv * kkv, ones_bd))
        kkn = kkv / jnp.maximum(nrm, 1e-12)

        tw_hi, tw_lo = _split2(jnp.tanh(wd))
        w_lora = _dot(tw_hi, w2_ref[d, 0]) + (_dot(tw_hi, w2_ref[d, 1]) + _dot(tw_lo, w2_ref[d, 0]))
        w_log = -_softplus(-(w0_ref[d:d + 1, :] + w_lora)) - 0.5
        lw = -jnp.exp(w_log)
        a = _sigmoid(a0_ref[d:d + 1, :] + _dot(ad.astype(BF16), a2_ref[d]))
        kd = k * (1.0 + (a - 1.0) * ka_ref[...])
        bon_s[d, pl.ds(c0, L), :] = _head_sum(r * kd * rk_ref[...], ones_bd, single_pass=True) * v
        if d == 0:
            g_s[pl.ds(c0, L), :] = _dot(_sigmoid(gd).astype(BF16), g2_ref[...])

        cum = _dot_exact_lhs(tri_ref[d], lw)
        e_pos = jnp.exp(cum)
        e_neg = jnp.exp(-cum)
        rh = r * e_pos
        kh = kd * e_neg
        bh = (kkn * a) * e_neg
        ah = -kkn * jnp.exp(cum - lw)
        last = L - 1 if d == 0 else 0
        gam = e_pos[last:last + 1, :]
        bhg = bh * gam
        khg = kh * gam

        chains = []
        for p in range(RW_PAIRS):
            cols = slice(p * LANES, (p + 1) * LANES)
            chains.append(dict(
                d=d, p=p, gam=gam[:, cols], v_st=stack(v[:, cols]),
                lhs=jnp.concatenate([stack(ah[:, cols]), stack(rh[:, cols])], axis=0),
                rhs=jnp.concatenate([stack(bh[:, cols]), stack(kh[:, cols])], axis=0),
                bk=jnp.concatenate([stack(bhg[:, cols]), stack(khg[:, cols])], axis=0)))
        return chains

    n_levels = int(math.log2(L))

    def state_free(chains):
        ri = lax.broadcasted_iota(jnp.int32, (L2, L2), 0)
        ci = lax.broadcasted_iota(jnp.int32, (L2, L2), 1)
        eye = jnp.where(ri == ci, 1.0, 0.0)
        rl = jnp.where(ri < L, ri, ri - L)
        cl = jnp.where(ci < L, ci, ci - L)
        strict = (cl < rl, cl > rl)
        incl = (cl <= rl, cl >= rl)
        level = ri ^ ci

        for ch in chains:
            ch['sc'] = _dot_nt(ch['lhs'], ch['rhs'])
        for ch in chains:
            sc, d = ch['sc'], ch['d']
            n_ab = jnp.where(strict[d], sc[0:L2, 0:L2], 0.0)
            m_ak = jnp.where(strict[d], sc[0:L2, L2:2 * L2], 0.0).astype(BF16)
            t_rb = jnp.where(incl[d], sc[L2:2 * L2, 0:L2], 0.0)
            t_rk = jnp.where(incl[d], sc[L2:2 * L2, L2:2 * L2], 0.0)
            ch['t_rbk'] = jnp.concatenate([t_rb, t_rk], axis=1).astype(BF16)
            ch['mv'] = _dot(m_ak, ch['v_st'])
            ch['t'] = eye + jnp.where(level == 1, n_ab, 0.0)
            ch['n_lv'] = [jnp.where(lax.shift_right_logical(level, lv) == 1, n_ab, 0.0).astype(BF16)
                          for lv in range(1, n_levels)]
            del ch['sc']
        for lv in range(n_levels - 1):
            for ch in chains:
                ch['tb'] = ch['t'].astype(BF16)
                ch['x'] = _dot(ch['n_lv'][lv], ch['tb']).astype(BF16)
            for ch in chains:
                ch['t'] = ch['t'] + _dot(ch['tb'], ch['x'])

    def state_step(chains, c0s):
        for ch in chains:
            ch['s_prev'] = s_s[ch['d'], ch['p']]
            ch['ar_s'] = _dot_nt(ch['lhs'], ch['s_prev'].astype(BF16))
        for ch in chains:
            u_rhs = ch['ar_s'][0:L2] + ch['mv']
            u_st = _dot(ch['t'].astype(BF16), u_rhs.astype(BF16)).astype(BF16)
            ch['uv'] = jnp.concatenate([u_st, ch['v_st']], axis=0)
        for ch in chains:
            y_st = ch['ar_s'][L2:2 * L2] + _dot(ch['t_rbk'], ch['uv'])
            ch['y'] = y_st[0:L] + y_st[L:L2]
            s_s[ch['d'], ch['p']] = ch['s_prev'] * ch['gam'] + _dot_tn(ch['uv'], ch['bk'])
        for d in range(2):
            yd_s[d, pl.ds(c0s[d], L), :] = jnp.concatenate([ch['y'] for ch in chains if ch['d'] == d], axis=1)

    steps = 2

    def body(i, carry):
        groups = []
        for j in range(steps):
            step = i * steps + j
            c0s = (pl.multiple_of(step * L, L), pl.multiple_of((n_ch - 1 - step) * L, L))
            groups.append((prep(c0s[0], 0) + prep(c0s[1], 1), c0s))
        state_free([ch for chains, _ in groups for ch in chains])
        for chains, c0s in groups:
            state_step(chains, c0s)
        return carry

    lax.fori_loop(0, n_ch // steps, body, 0)

    def epilogue(i, carry):
        c0 = pl.multiple_of(i * L, L)
        ones_bd = bd_ref[...]
        ys = yd_s[0, pl.ds(c0, L), :] + yd_s[1, pl.ds(c0, L), :]
        mu = _head_sum(ys, ones_bd) * (1.0 / RW_HEAD)
        yc = ys - mu
        var = _head_sum(yc * yc, ones_bd) * (1.0 / RW_HEAD)
        yn = yc * lax.rsqrt(var + RW_LN_EPS) * lnw_ref[...] + lnb_ref[...]
        out = (yn + (bon_s[0, pl.ds(c0, L), :] + bon_s[1, pl.ds(c0, L), :])) * g_s[pl.ds(c0, L), :]
        y_ref[pl.ds(c0, L), :] = out.astype(y_ref.dtype)
        return carry

    lax.fori_loop(0, n_ch, epilogue, 0, unroll=min(EPILOGUE_UNROLL, n_ch))
    for d in range(2):
        for p in range(RW_PAIRS):
            s_pair = s_s[d, p]
            sfin_ref[d, 2 * p] = s_pair[0:RW_HEAD, 0:RW_HEAD]
            sfin_ref[d, 2 * p + 1] = s_pair[RW_HEAD:, RW_HEAD:]


def _rwkv_mixer(z, s0, p, n_batch, t_len):
    L = SCAN_CHUNK
    c2 = lambda b: (0, 0)
    c3 = lambda b: (0, 0, 0)
    st_spec = pl.BlockSpec((None, 2, RW_HEADS, RW_HEAD, RW_HEAD), lambda b: (b, 0, 0, 0, 0))
    return pl.pallas_call(
        functools.partial(_rwkv_kernel, t_len=t_len),
        grid=(n_batch,),
        in_specs=[
            pl.BlockSpec((t_len, RW_ZCOLS), lambda b: (b, 0)),
            st_spec,
            pl.BlockSpec((2, RW_ZCOLS), c2),
            pl.BlockSpec((1, D_RW), c2),
            pl.BlockSpec((1, D_RW), c2),
            pl.BlockSpec((1, D_RW), c2),
            pl.BlockSpec((RW_G_LORA, D_RW), c2),
            pl.BlockSpec((2, D_RW), c2),
            pl.BlockSpec((2, 2, LANES, D_RW), lambda b: (0, 0, 0, 0)),
            pl.BlockSpec((2, D_RW), c2),
            pl.BlockSpec((2, LANES, D_RW), c3),
            pl.BlockSpec((1, D_RW), c2),
            pl.BlockSpec((1, D_RW), c2),
            pl.BlockSpec((2, L, L), c3),
            pl.BlockSpec((LANES, LANES), c2),
        ],
        out_specs=[pl.BlockSpec((t_len, D_RW), lambda b: (b, 0)), st_spec],
        out_shape=[
            jax.ShapeDtypeStruct((n_batch * t_len, D_RW), Y_DTYPE),
            jax.ShapeDtypeStruct((n_batch, 2, RW_HEADS, RW_HEAD, RW_HEAD), F32),
        ],
        scratch_shapes=[
            pltpu.VMEM((2, L + 2 * HALO, RW_ZCOLS), F32),
            pltpu.VMEM((2, RW_PAIRS, LANES, LANES), F32),
            pltpu.VMEM((2, t_len, D_RW), F32),
            pltpu.VMEM((2, t_len, D_RW), F32),
            pltpu.VMEM((t_len, D_RW), F32),
        ],
        compiler_params=_ARB1,
    )(z, s0, p['mu'], p['kk'], p['ka'], p['rk'], p['g2'], p['w0'], p['w2'], p['a0'], p['a2'],
      p['ln_w'], p['ln_b'], p['tri'], p['ones_bd'])


def _mlstm_kernel(z_ref, c0_ref, n0_ref, m0_ref, cw_ref, cb_ref, gb_ref, nw_ref, tri_ref, esel_ref, shift_ref,
                  ones_ref, y_ref, cfin_ref, nfin_ref, mfin_ref, c_s, n_s, m_s, hd_s, qk_s, gx_s, gr_s,
                  *, t_len):
    L = ML_SCAN
    n_ch = t_len // L
    c_s[...] = c0_ref[...]
    n_s[...] = n0_ref[...]
    m_s[...] = m0_ref[...]
    q_scale = ML_DK ** -0.5
    g_lo = 4 * D_ML
    n_real = 4 * ML_HEADS

    def conv_chunk(i, carry):
        c0 = pl.multiple_of(i * L, L)
        gts = z_ref[pl.ds(c0, L), g_lo:g_lo + LANES].astype(F32) + gb_ref[...]
        x16 = gts.T[0:n_real, :]
        lf_hi, lf_lo = _split2(-_softplus(-x16))
        lane = lax.broadcasted_iota(jnp.int32, (8, L), 1)
        stats = []
        for d in range(2):
            b16 = _dot_nt(lf_hi, tri_ref[d]) + _dot_nt(lf_lo, tri_ref[d])
            x8 = x16[8 * d:8 * d + 8, :]
            b8 = b16[8 * d:8 * d + 8, :]
            g8 = pltpu.roll(x8, ML_HEADS, axis=0) - b8
            stats.append([g8, b8, g8])
        step = 1
        while step < L:
            for d in range(2):
                cm8 = stats[d][2]
                if d == 0:
                    sh = jnp.where(lane >= step, pltpu.roll(cm8, step, axis=1), -jnp.inf)
                else:
                    sh = jnp.where(lane < L - step, pltpu.roll(cm8, L - step, axis=1), -jnp.inf)
                stats[d][2] = jnp.maximum(cm8, sh)
            step *= 2

        ext = _load_ext(z_ref, c0, L, t_len, 0, 2 * D_ML, raw=True)
        taps = _dot(shift_ref[...], ext)
        qk = cb_ref[...] + ext[HALO:HALO + L].astype(F32) * cw_ref[CONV_LEFT:CONV_LEFT + 1, :]
        for n, j in enumerate(jj for jj in range(CONV_W) if jj != CONV_LEFT):
            qk = qk + taps[n * L:(n + 1) * L] * cw_ref[j:j + 1, :]
        qk = _silu(qk)
        qk_s[pl.ds(c0, L), 0:D_ML] = (qk[:, :D_ML] * q_scale).astype(BF16)
        qk_s[pl.ds(c0, L), D_ML:2 * D_ML] = qk[:, D_ML:].astype(BF16)

        for d, (g8, b8, cm8) in enumerate(stats):
            x32 = jnp.concatenate([g8, b8, cm8, jnp.zeros((8, L), F32)], axis=0)
            hi = x32.astype(BF16)
            r1 = x32 - hi.astype(F32)
            mid = r1.astype(BF16)
            lo = (r1 - mid.astype(F32)).astype(BF16)
            gx_s[d, i] = jnp.concatenate([hi, mid, lo], axis=0)
            gr_s[d, i] = g8
        return carry

    lax.fori_loop(0, n_ch, conv_chunk, 0, unroll=min(ML_UNROLL, n_ch))

    def prep(ci, d):
        c0 = pl.multiple_of(ci * L, L)
        q = qk_s[pl.ds(c0, L), 0:D_ML]
        k = qk_s[pl.ds(c0, L), D_ML:2 * D_ML]
        v = z_ref[pl.ds(c0, L), 2 * D_ML:3 * D_ML]
        g8 = gr_s[d, ci]
        cb = _dot_tn(gx_s[d, ci], esel_ref[...])
        ri = lax.broadcasted_iota(jnp.int32, (L, L), 0)
        ci = lax.broadcasted_iota(jnp.int32, (L, L), 1)
        causal = (ci <= ri) if d == 0 else (ci >= ri)
        last = L - 1 if d == 0 else 0
        ones = ones_ref[...]
        chains = []
        for h in range(ML_HEADS):
            cols = slice(h * ML_DK, (h + 1) * ML_DK)
            chains.append(dict(
                d=d, h=h, last=last, causal=causal,
                g_row=g8[ML_HEADS + h:ML_HEADS + h + 1, :],
                g_b=cb[:, cols], b_b=cb[:, D_ML + h * ML_DK:D_ML + (h + 1) * ML_DK],
                cm_b=cb[:, 2 * D_ML + h * ML_DK:2 * D_ML + (h + 1) * ML_DK],
                qb=q[:, cols], kb=k[:, cols],
                v=v[:, cols].astype(F32), v_ones=jnp.concatenate([v[:, cols].astype(BF16), ones], axis=1)))
        return chains

    def body(i, carry):
        c0s = (pl.multiple_of(i * L, L), pl.multiple_of((n_ch - 1 - i) * L, L))
        chains = prep(i, 0) + prep(n_ch - 1 - i, 1)
        for ch in chains:
            d, h = ch['d'], ch['h']
            ch['qk'] = _dot_nt(ch['qb'], ch['kb'])
            ch['c_prev'] = c_s[d, h]
            ch['n_prev'] = n_s[d, h]
            ch['m_prev'] = m_s[d, h:h + 1, :]
            ch['qcn'] = _dot(ch['qb'], jnp.concatenate([ch['c_prev'].astype(BF16),
                                                        ch['n_prev'].astype(BF16)], axis=1))
        for ch in chains:
            mu = jnp.maximum(ch['m_prev'], ch['cm_b'])
            ch['mu'] = mu
            ch['s'] = ch['qk'] * jnp.exp(jnp.where(ch['causal'], ch['g_row'] - mu, -jnp.inf))
            ch['w_inter'] = jnp.exp(ch['m_prev'] - mu)
        for ch in chains:
            last = ch['last']
            ch['sv'] = _dot(ch['s'].astype(BF16), ch['v_ones'])
            mu_last = ch['mu'][last:last + 1, :]
            ch['m_new'] = ch['b_b'][last:last + 1, :] + mu_last
            ch['dec'] = jnp.exp(ch['m_prev'] - mu_last)
            w_s = jnp.exp(ch['g_b'] - mu_last)
            ch['upd'] = _dot_tn(ch['kb'], jnp.concatenate([(w_s * ch['v']).astype(BF16),
                                                           w_s.astype(BF16)], axis=1))
        for ch in chains:
            d, h = ch['d'], ch['h']
            num = ch['sv'][:, :ML_DK] + ch['w_inter'] * ch['qcn'][:, :ML_DK]
            den = ch['sv'][:, ML_DK:] + ch['w_inter'] * ch['qcn'][:, ML_DK:]
            ch['out'] = num / jnp.maximum(jnp.abs(den), jnp.exp(-(ch['b_b'] + ch['mu'])))
            c_s[d, h] = ch['dec'] * ch['c_prev'] + ch['upd'][:, :ML_DK]
            n_s[d, h] = ch['dec'] * ch['n_prev'] + ch['upd'][:, ML_DK:]
            m_s[d, h:h + 1, :] = ch['m_new']
        for d in range(2):
            hd_s[d, pl.ds(c0s[d], L), :] = jnp.concatenate([ch['out'] for ch in chains if ch['d'] == d], axis=1)
        return carry

    lax.fori_loop(0, n_ch, body, 0, unroll=min(ML_UNROLL, n_ch))

    def epilogue(i, carry):
        c0 = pl.multiple_of(i * L, L)
        ones = ones_ref[...]
        hs = hd_s[0, pl.ds(c0, L), :] + hd_s[1, pl.ds(c0, L), :]
        o = z_ref[pl.ds(c0, L), 3 * D_ML:4 * D_ML].astype(F32)
        mu = _head_sum(hs, ones) * (1.0 / ML_DK)
        xc = hs - mu
        var = _head_sum(xc * xc, ones) * (1.0 / ML_DK)
        y = xc * lax.rsqrt(var + NORM_EPS) * nw_ref[...] * _sigmoid(o)
        y_ref[pl.ds(c0, L), :] = y.astype(y_ref.dtype)
        return carry

    lax.fori_loop(0, n_ch, epilogue, 0, unroll=min(EPILOGUE_UNROLL, n_ch))
    cfin_ref[...] = c_s[...]
    nfin_ref[...] = n_s[...]
    mfin_ref[...] = m_s[...]


def _mlstm_mixer(z, c0, n0, m0, p, n_batch, t_len):
    assert z.dtype == BF16
    L = ML_SCAN
    c2 = lambda b: (0, 0)
    c3 = lambda b: (0, 0, 0)
    c_spec = pl.BlockSpec((None, 2, ML_HEADS, ML_DK, ML_DK), lambda b: (b, 0, 0, 0, 0))
    m_spec = pl.BlockSpec((None, 2, ML_HEADS, ML_DK), lambda b: (b, 0, 0, 0))
    return pl.pallas_call(
        functools.partial(_mlstm_kernel, t_len=t_len),
        grid=(n_batch,),
        in_specs=[
            pl.BlockSpec((t_len, ML_ZCOLS), lambda b: (b, 0)),
            c_spec, c_spec, m_spec,
            pl.BlockSpec((CONV_W, 2 * D_ML), c2),
            pl.BlockSpec((1, 2 * D_ML), c2),
            pl.BlockSpec((1, LANES), c2),
            pl.BlockSpec((1, D_ML), c2),
            pl.BlockSpec((2, L, L), c3),
            pl.BlockSpec(p['esel'].shape, c2),
            pl.BlockSpec(p['shift'].shape, c2),
            pl.BlockSpec((L, LANES), c2),
        ],
        out_specs=[pl.BlockSpec((t_len, D_ML), lambda b: (b, 0)), c_spec, c_spec, m_spec],
        out_shape=[
            jax.ShapeDtypeStruct((n_batch * t_len, D_ML), Y_DTYPE),
            jax.ShapeDtypeStruct((n_batch, 2, ML_HEADS, ML_DK, ML_DK), F32),
            jax.ShapeDtypeStruct((n_batch, 2, ML_HEADS, ML_DK, ML_DK), F32),
            jax.ShapeDtypeStruct((n_batch, 2, ML_HEADS, ML_DK), F32),
        ],
        scratch_shapes=[
            pltpu.VMEM((2, ML_HEADS, ML_DK, ML_DK), F32),
            pltpu.VMEM((2, ML_HEADS, ML_DK, ML_DK), F32),
            pltpu.VMEM((2, ML_HEADS, ML_DK), F32),
            pltpu.VMEM((2, t_len, D_ML), F32),
            pltpu.VMEM((t_len, 2 * D_ML), BF16),
            pltpu.VMEM((2, t_len // L, 96, L), BF16),
            pltpu.VMEM((2, t_len // L, 8, L), F32),
        ],
        compiler_params=_ARB1,
    )(z, c0, n0, m0, p['conv_w'], p['conv_b'], p['gate_b'], p['norm_w'], p['tri'], p['esel'], p['shift'],
      p['ones'])


def _conv_shift(n):
    taps = jnp.array([j for j in range(CONV_W) if j != CONV_LEFT])
    t = jnp.arange((CONV_W - 1) * n)
    src = HALO + t % n + taps[t // n] - CONV_LEFT
    return (src[:, None] == jnp.arange(n + 2 * HALO)[None, :]).astype(BF16)


def _mlstm_consts():
    L = ML_SCAN
    lower = jnp.tril(jnp.ones((L, L), F32))
    tri = jnp.stack([lower, lower.T]).astype(BF16)
    rows = jnp.arange(32)
    blk = jnp.arange(3 * D_ML) // ML_DK
    src_row = 8 * (blk // ML_HEADS) + ML_HEADS + blk % ML_HEADS
    esel = (rows[:, None] == src_row[None, :]).astype(BF16)
    esel = jnp.concatenate([esel, esel, esel], axis=0)
    return dict(tri=tri, esel=esel, shift=_conv_shift(L), ones=jnp.ones((L, LANES), BF16))


def _block_diag(w):
    n, c, d = w.shape
    return jnp.einsum('ncd,nm->ncmd', w, jnp.eye(n, dtype=w.dtype)).reshape(n * c, n * d)


def _scan_tri():
    L = SCAN_CHUNK
    lower = jnp.tril(jnp.ones((L, L), F32))
    return jnp.stack([lower, lower.T]).astype(BF16)


def _prep_even(ev_w_in, pool_w, pool_scale, rg_conv_w, rg_conv_b, rg_wa, rg_ba, rg_wx, rg_bx, rg_lam, ev_w_out, j):
    wg = jnp.concatenate([_block_diag(rg_wa[j, 0]), _block_diag(rg_wx[j, 0]),
                          _block_diag(rg_wa[j, 1]), _block_diag(rg_wx[j, 1])], axis=1)
    bg = jnp.concatenate([rg_ba[j, 0], rg_bx[j, 0], rg_ba[j, 1], rg_bx[j, 1]]).reshape(1, 4 * D_RG)
    return dict(
        w_in=[ev_w_in[j].astype(BF16)],
        w_out=ev_w_out[j].astype(BF16),
        pool_w=pool_w[j].astype(BF16),
        pool_scale=pool_scale[j].reshape(1, D_POOL),
        conv_w=rg_conv_w[j],
        conv_b=rg_conv_b[j].reshape(1, D_RG),
        wg=wg.astype(BF16),
        bg=bg,
        lam=rg_lam[j],
    )


def _pad_cols(w, cuts, widths):
    pieces = jnp.split(w, cuts, axis=-1)
    out = []
    for piece, width in zip(pieces, widths):
        extra = width - piece.shape[-1]
        if extra:
            piece = jnp.pad(piece, [(0, 0)] * (piece.ndim - 1) + [(0, extra)])
        out.append(piece)
    return jnp.concatenate(out, axis=-1)


def _prep_odd(od_w_in, rw_mu, rw_w0, rw_w2, rw_a0, rw_a2, rw_kk, rw_ka, rw_rk, rw_g2, rw_ln_w, rw_ln_b,
              ml_conv_w, ml_conv_b, ml_bi, ml_bf, ml_norm_w, od_w_out, j):
    rw_cols = 3 * D_RW + RW_G_LORA + RW_W_LORA + RW_A_LORA
    base = 3 * D_RW + RW_G_LORA
    cuts = [base, base + RW_W_LORA]
    widths = [base, LANES, LANES]
    w_rw = _pad_cols(od_w_in[j][:, :rw_cols], cuts, widths)
    w_ml = _pad_cols(od_w_in[j][:, rw_cols:], [4 * D_ML], [4 * D_ML, LANES])
    pad_rows = lambda w: jnp.pad(w, ((0, 0), (0, LANES - w.shape[1]), (0, 0)))
    gate_b = jnp.concatenate([ml_bi[j, 0], ml_bf[j, 0], ml_bi[j, 1], ml_bf[j, 1]])
    gate_b = jnp.pad(gate_b, (0, LANES - gate_b.shape[0])).reshape(1, LANES)
    hm = jnp.arange(LANES) // RW_HEAD
    rw = dict(
        mu=_pad_cols(rw_mu[j], cuts, widths),
        kk=rw_kk[j].reshape(1, D_RW), ka=rw_ka[j].reshape(1, D_RW), rk=rw_rk[j].reshape(1, D_RW),
        g2=rw_g2[j].astype(BF16),
        w0=rw_w0[j], w2=jnp.stack(_split2(pad_rows(rw_w2[j])), axis=1),
        a0=rw_a0[j], a2=pad_rows(rw_a2[j]).astype(BF16),
        ln_w=rw_ln_w[j].reshape(1, D_RW), ln_b=rw_ln_b[j].reshape(1, D_RW),
        tri=_scan_tri(),
        ones_bd=(hm[:, None] == hm[None, :]).astype(BF16),
    )
    ml = dict(
        conv_w=ml_conv_w[j], conv_b=ml_conv_b[j].reshape(1, 2 * D_ML),
        gate_b=gate_b, norm_w=ml_norm_w[j].reshape(1, D_ML), **_mlstm_consts(),
    )
    return dict(w_in=[w_rw.astype(BF16), w_ml.astype(BF16)], w_out=od_w_out[j].astype(BF16), rw=rw, ml=ml)


def _trunk(x, mod, rows_per_mod, mod_base, grid, states, params, n_batch, t_len):
    ev, od, dense = params
    rg0, rw0, c0, n0, m0 = states
    x = x.reshape(n_batch * t_len, D_MODEL)
    spec = (rows_per_mod, mod_base)

    (z,) = _in_proj(x, mod[0], *spec, dense['norm1_w'][0], ev['w_in'])
    y_ev, rg_f = _even_mixer(z, rg0, ev, n_batch, t_len, grid)
    x = _out_mlp([y_ev], x, mod[0], *spec, dense['norm2_w'][0], ev['w_out'],
                 dense['mlp_w1'], dense['mlp_w2'], 0, dense['final_norm_w'], False)

    z_rw, z_ml = _in_proj(x, mod[1], *spec, dense['norm1_w'][1], od['w_in'])
    y_rw, s_f = _rwkv_mixer(z_rw, rw0, od['rw'], n_batch, t_len)
    y_ml, c_f, n_f, m_f = _mlstm_mixer(z_ml, c0, n0, m0, od['ml'], n_batch, t_len)
    y = _out_mlp([y_rw, y_ml], x, mod[1], *spec, dense['norm2_w'][1], od['w_out'],
                 dense['mlp_w1'], dense['mlp_w2'], 1, dense['final_norm_w'], True)
    return y.reshape(n_batch, t_len, D_MODEL), (rg_f, s_f, c_f, n_f, m_f)


def kernel(x_prompt, x_sample, state_rglru, state_rwkv, state_mlstm_C, state_mlstm_n, state_mlstm_m,
           c, c_ctx, norm1_w, norm2_w, w_mod, b_mod, mlp_w1, mlp_w2, final_norm_w,
           ev_w_in, pool_w, pool_scale, rg_conv_w, rg_conv_b, rg_wa, rg_ba, rg_wx, rg_bx, rg_lam, ev_w_out,
           od_w_in, rw_mu, rw_w0, rw_w2, rw_a0, rw_a2, rw_kk, rw_ka, rw_rk, rw_g2, rw_ln_w, rw_ln_b,
           ml_conv_w, ml_conv_b, ml_bi, ml_bf, ml_norm_w, od_w_out):
    bp, tp, _ = x_prompt.shape
    bs, ts, _ = x_sample.shape

    assert bs + 1 <= N_COND_ROWS and ts % ROW_BLOCK == 0 and (bp * tp) % ROW_BLOCK == 0
    cond = jnp.concatenate([c_ctx[None, :], c, jnp.zeros((N_COND_ROWS - 1 - bs, D_MODEL), F32)], axis=0)
    mod = _modulation(cond, w_mod, b_mod)

    ev = _prep_even(ev_w_in, pool_w, pool_scale, rg_conv_w, rg_conv_b, rg_wa, rg_ba, rg_wx, rg_bx, rg_lam,
                    ev_w_out, 0)
    od = _prep_odd(od_w_in, rw_mu, rw_w0, rw_w2, rw_a0, rw_a2, rw_kk, rw_ka, rw_rk, rw_g2, rw_ln_w, rw_ln_b,
                   ml_conv_w, ml_conv_b, ml_bi, ml_bf, ml_norm_w, od_w_out, 0)
    dense = dict(norm1_w=norm1_w, norm2_w=norm2_w, mlp_w1=mlp_w1.astype(BF16), mlp_w2=mlp_w2.astype(BF16),
                 final_norm_w=final_norm_w)
    params = (ev, od, dense)

    zero_states = (
        jnp.zeros((bp, 2, D_RG), F32),
        jnp.zeros((bp, 2, RW_HEADS, RW_HEAD, RW_HEAD), F32),
        jnp.zeros((bp, 2, ML_HEADS, ML_DK, ML_DK), F32),
        jnp.zeros((bp, 2, ML_HEADS, ML_DK, ML_DK), F32),
        jnp.zeros((bp, 2, ML_HEADS, ML_DK), F32),
    )
    y_prompt, (rg_f, s_f, c_f, n_f, m_f) = _trunk(
        x_prompt, mod, bp * tp, 0, False, zero_states, params, bp, tp)

    sample_states = (
        state_rglru[:, 0],
        state_rwkv[:, 0],
        state_mlstm_C[:, 0],
        jnp.broadcast_to(state_mlstm_n[:, 0][..., None], (bs, 2, ML_HEADS, ML_DK, ML_DK)),
        jnp.broadcast_to(state_mlstm_m[:, 0][..., None], (bs, 2, ML_HEADS, ML_DK)),
    )
    y_sample, _ = _trunk(x_sample, mod, ts, 1, True, sample_states, params, bs, ts)

    return (y_prompt, y_sample,
            rg_f[:, None],
            s_f[:, None],
            c_f[:, None],
            n_f[:, None, :, :, :, 0],
            m_f[:, None, :, :, 0])
```

```python
import functools
import math

import jax
import jax.numpy as jnp
from jax import lax
from jax.experimental import pallas as pl
from jax.experimental.pallas import tpu as pltpu

F32 = jnp.float32
BF16 = jnp.bfloat16

D_MODEL = 1024
DEPTH = 2
GRID_W = 64
D_FF = 4 * D_MODEL
NORM_EPS = 1e-6
N_MOD = 6

D_POOL = D_MODEL // 2
POOL_WINDOWS = (2, 4, 8, 16)
POOL_GW = D_POOL // len(POOL_WINDOWS)
D_RG = D_MODEL // 2
RG_BLOCKS = 8
RG_C = 8.0
CONV_W = 4
CONV_LEFT = 2
EV_COLS = D_POOL + 2 * D_RG

D_RW = D_MODEL // 2
RW_HEAD = 64
RW_HEADS = D_RW // RW_HEAD
RW_PAIRS = RW_HEADS // 2
RW_W_LORA = 64
RW_A_LORA = 64
RW_G_LORA = 128
RW_LN_EPS = 64e-5
RW_ZCOLS = 3 * D_RW + RW_G_LORA + 128 + 128
D_ML = D_MODEL // 2
ML_HEADS = 4
ML_DK = D_ML // ML_HEADS
ML_ZCOLS = 4 * D_ML + 128

LANES = 128
HALO = 16
Z_DTYPE = BF16
Y_DTYPE = BF16
ROW_BLOCK = 512
SCAN_CHUNK = 64
ML_SCAN = 128
RG_CHUNK = 256
POOL_SEGS_PER_TRIP = 8
ML_UNROLL = 4
EPILOGUE_UNROLL = 8
MOD_COL_BLOCK = 1536
N_COND_ROWS = 16
VMEM_LIMIT = 56 * 1024 * 1024

_ARB1 = pltpu.CompilerParams(dimension_semantics=("arbitrary",), vmem_limit_bytes=VMEM_LIMIT)
_ARB2 = pltpu.CompilerParams(dimension_semantics=("arbitrary", "arbitrary"), vmem_limit_bytes=VMEM_LIMIT)


def _dot(a, b):
    return jnp.dot(a, b, preferred_element_type=F32)


def _dot_nt(a, b):
    return lax.dot_general(a, b, (((1,), (1,)), ((), ())), preferred_element_type=F32)


def _dot_tn(a, b):
    return lax.dot_general(a, b, (((0,), (0,)), ((), ())), preferred_element_type=F32)


def _split2(x):
    hi = x.astype(BF16)
    lo = (x - hi.astype(F32)).astype(BF16)
    return hi, lo


def _dot_f32(a, b):
    ah, al = _split2(a)
    bh, bl = _split2(b)
    return _dot(ah, bh) + (_dot(ah, bl) + _dot(al, bh))


def _dot_exact_lhs(m_bf16, x):
    hi, lo = _split2(x)
    return _dot(m_bf16, hi) + _dot(m_bf16, lo)


def _dot_exact_rhs(x, m_bf16):
    hi, lo = _split2(x)
    return _dot(hi, m_bf16) + _dot(lo, m_bf16)


def _softplus(x):
    return jnp.maximum(x, 0.0) + jnp.log1p(jnp.exp(-jnp.abs(x)))


def _sigmoid(x):
    return jax.nn.sigmoid(x)


def _silu(x):
    return x * jax.nn.sigmoid(x)


def _gelu_tanh(x):
    c = math.sqrt(2.0 / math.pi)
    return x * (0.5 * (1.0 + jnp.tanh(c * (x + 0.044715 * (x * x * x)))))


def _rms_norm(x, w):
    ms = jnp.mean(x * x, axis=-1, keepdims=True)
    return x * lax.rsqrt(ms + NORM_EPS) * w


def _load_ext(z_ref, c0, n, t_len, lo, hi, raw=False):
    dt = z_ref.dtype if raw else F32
    cur = z_ref[pl.ds(c0, n), lo:hi].astype(dt)
    pb = pl.multiple_of(jnp.maximum(c0 - HALO, 0), HALO)
    pa = pl.multiple_of(jnp.minimum(c0 + n, t_len - HALO), HALO)
    before = z_ref[pl.ds(pb, HALO), lo:hi].astype(dt)
    after = z_ref[pl.ds(pa, HALO), lo:hi].astype(dt)
    before = jnp.where(c0 > 0, before, jnp.zeros_like(before))
    after = jnp.where(c0 + n < t_len, after, jnp.zeros_like(after))
    return jnp.concatenate([before, cur, after], axis=0)


def _head_sum(x, ones_bd, single_pass=False):
    n = x.shape[0]
    groups = x.shape[1] // LANES
    xs = jnp.concatenate([x[:, g * LANES:(g + 1) * LANES] for g in range(groups)], axis=0)
    s = _dot(xs.astype(BF16), ones_bd) if single_pass else _dot_exact_rhs(xs, ones_bd)
    return jnp.concatenate([s[g * n:(g + 1) * n] for g in range(groups)], axis=1)


def _mod_kernel(c_ref, w_ref, b_ref, o_ref):
    s = _silu(c_ref[...])
    o_ref[...] = _dot_f32(s, w_ref[...]) + b_ref[...]


def _modulation(cond, w_mod, b_mod):
    rows = cond.shape[0]
    tn = MOD_COL_BLOCK
    out = pl.pallas_call(
        _mod_kernel,
        grid=(DEPTH, N_MOD * D_MODEL // tn),
        in_specs=[
            pl.BlockSpec((rows, D_MODEL), lambda l, j: (0, 0)),
            pl.BlockSpec((None, D_MODEL, tn), lambda l, j: (l, 0, j)),
            pl.BlockSpec((None, 1, tn), lambda l, j: (l, 0, j)),
        ],
        out_specs=pl.BlockSpec((None, rows, tn), lambda l, j: (l, 0, j)),
        out_shape=jax.ShapeDtypeStruct((DEPTH, rows, N_MOD * D_MODEL), F32),
        compiler_params=_ARB2,
    )(cond, w_mod, b_mod.reshape(DEPTH, 1, N_MOD * D_MODEL))
    return out.reshape(DEPTH, rows, N_MOD, D_MODEL)


def _inproj_kernel(*refs, n_out, col_chunk):
    x_ref, mod_ref, nw_ref = refs[:3]
    w_refs = refs[3:3 + n_out]
    z_refs = refs[3 + n_out:]
    h = _rms_norm(x_ref[...], nw_ref[...])
    h = h * (1.0 + mod_ref[1:2, :]) + mod_ref[0:1, :]
    hb = h.astype(BF16)
    for w_ref, z_ref in zip(w_refs, z_refs):
        n = w_ref.shape[1]
        for c in range(0, n, col_chunk):
            e = min(c + col_chunk, n)
            z_ref[:, c:e] = _dot(hb, w_ref[:, c:e]).astype(z_ref.dtype)


def _in_proj(x, mod, rows_per_mod, mod_base, norm_w, weights):
    m = x.shape[0]
    tm = ROW_BLOCK
    n_out = len(weights)
    const = lambda i: (0, 0)
    in_specs = [
        pl.BlockSpec((tm, D_MODEL), lambda i: (i, 0)),
        pl.BlockSpec((None, N_MOD, D_MODEL), lambda i: (mod_base + (i * tm) // rows_per_mod, 0, 0)),
        pl.BlockSpec((1, D_MODEL), const),
    ] + [pl.BlockSpec(w.shape, const, pipeline_mode=pl.Buffered(1)) for w in weights]
    out_specs = [pl.BlockSpec((tm, w.shape[1]), lambda i: (i, 0)) for w in weights]
    out_shape = [jax.ShapeDtypeStruct((m, w.shape[1]), Z_DTYPE) for w in weights]
    return pl.pallas_call(
        functools.partial(_inproj_kernel, n_out=n_out, col_chunk=512),
        grid=(m // tm,),
        in_specs=in_specs,
        out_specs=out_specs,
        out_shape=out_shape,
        compiler_params=_ARB1,
    )(x, mod, norm_w.reshape(1, D_MODEL), *weights)


def _outmlp_kernel(*refs, n_y, final, ff_chunk):
    y_refs = refs[:n_y]
    x_ref, mod_ref, nw_ref, wo_ref, w1_ref, w2_ref, fw_ref, o_ref = refs[n_y:]
    y = None
    row = 0
    for y_ref in y_refs:
        k = y_ref.shape[1]
        part = _dot(y_ref[...], wo_ref[row:row + k, :])
        y = part if y is None else y + part
        row += k
    x = x_ref[...] + mod_ref[2:3, :] * y
    h = _rms_norm(x, nw_ref[...]) * (1.0 + mod_ref[4:5, :]) + mod_ref[3:4, :]
    hb = h.astype(BF16)
    acc = None
    for c in range(0, D_FF, ff_chunk):
        u = jnp.maximum(_dot(hb, w1_ref[:, c:c + ff_chunk]), 0.0)
        part = _dot((u * u).astype(BF16), w2_ref[c:c + ff_chunk, :])
        acc = part if acc is None else acc + part
    x = x + mod_ref[5:6, :] * acc
    if final:
        x = _rms_norm(x, fw_ref[...])
    o_ref[...] = x


def _out_mlp(ys, x, mod, rows_per_mod, mod_base, norm_w, w_out, w1, w2, layer, final_w, final):
    m = x.shape[0]
    tm = ROW_BLOCK
    const = lambda i: (0, 0)
    single = dict(pipeline_mode=pl.Buffered(1))
    pick = lambda i: (layer, 0, 0)
    in_specs = [pl.BlockSpec((tm, y.shape[1]), lambda i: (i, 0)) for y in ys] + [
        pl.BlockSpec((tm, D_MODEL), lambda i: (i, 0)),
        pl.BlockSpec((None, N_MOD, D_MODEL), lambda i: (mod_base + (i * tm) // rows_per_mod, 0, 0)),
        pl.BlockSpec((1, D_MODEL), const),
        pl.BlockSpec(w_out.shape, const, **single),
        pl.BlockSpec((None,) + w1.shape[1:], pick, **single),
        pl.BlockSpec((None,) + w2.shape[1:], pick, **single),
        pl.BlockSpec((1, D_MODEL), const),
    ]
    return pl.pallas_call(
        functools.partial(_outmlp_kernel, n_y=len(ys), final=final, ff_chunk=1024),
        grid=(m // tm,),
        in_specs=in_specs,
        out_specs=pl.BlockSpec((tm, D_MODEL), lambda i: (i, 0)),
        out_shape=jax.ShapeDtypeStruct((m, D_MODEL), F32),
        compiler_params=_ARB1,
    )(*ys, x, mod, norm_w.reshape(1, D_MODEL), w_out, w1, w2, final_w.reshape(1, D_MODEL))


def _even_kernel(z_ref, h0_ref, pw_ref, ps_ref, cw_ref, cb_ref, wg_ref, bg_ref, lam_ref, band_ref,
                 y_ref, hfin_ref, cbuf, a_s, b_s, h_s, hf_s, *, t_len, grid):
    seg = GRID_W if grid else t_len
    n_seg = t_len // seg
    n_groups = len(POOL_WINDOWS)

    pos = lax.broadcasted_iota(jnp.int32, (seg, POOL_GW), 0)
    offs = [(w // 2, w - 1 - w // 2) for w in POOL_WINDOWS]
    gcols = [slice(g * POOL_GW, (g + 1) * POOL_GW) for g in range(n_groups)]

    def grid_row(rr, cols):
        valid = jnp.logical_and(rr >= 0, rr < n_seg)
        src = pl.multiple_of(jnp.clip(rr, 0, n_seg - 1) * seg, seg)
        return jnp.where(valid, z_ref[pl.ds(src, seg), cols].astype(F32), 0.0)

    def pool_rows(rows, run):
        items = []
        for r in rows:
            base = r * seg if isinstance(r, int) else pl.multiple_of(r * seg, seg)
            new_run = []
            for g, (lo_off, hi_off) in enumerate(offs):
                cols = gcols[g]
                if grid:
                    s1 = run[g] + grid_row(r + hi_off, cols) - grid_row(r - lo_off - 1, cols)
                    new_run.append(s1)
                    cnt_r = jnp.minimum(r + hi_off, n_seg - 1) - jnp.maximum(r - lo_off, 0) + 1
                    m1 = s1 / cnt_r.astype(F32)
                    hi = m1.astype(BF16)
                    r1 = m1 - hi.astype(F32)
                    mid = r1.astype(BF16)
                    parts = jnp.concatenate([hi, mid, (r1 - mid.astype(F32)).astype(BF16)], axis=0)
                else:
                    parts = z_ref[pl.ds(base, seg), cols]
                items.append(dict(g=g, base=base, parts=parts))
            run = new_run
        for it in items:
            it['s2'] = _dot(band_ref[it['g']], it['parts'])
        for it in items:
            lo_off, hi_off = offs[it['g']]
            cnt_c = jnp.minimum(pos + hi_off, seg - 1) - jnp.maximum(pos - lo_off, 0) + 1
            xg = z_ref[pl.ds(it['base'], seg), gcols[it['g']]].astype(F32)
            it['d'] = (it['s2'] / cnt_c.astype(F32) - xg).astype(BF16)
        for it in items:
            it['y'] = _dot(it['d'], pw_ref[it['g']])
        for it in items:
            cols = gcols[it['g']]
            y_ref[pl.ds(it['base'], seg), cols] = (it['y'] * ps_ref[:, cols]).astype(y_ref.dtype)
        return tuple(run) if grid else None

    if grid:
        init = []
        for g, (lo_off, hi_off) in enumerate(offs):
            s1 = jnp.zeros((seg, POOL_GW), F32)
            for o in range(min(hi_off, n_seg)):
                s1 = s1 + z_ref[o * seg:(o + 1) * seg, gcols[g]].astype(F32)
            init.append(s1)
        per_trip = POOL_SEGS_PER_TRIP
        lax.fori_loop(0, n_seg // per_trip,
                      lambda i, run: pool_rows([i * per_trip + j for j in range(per_trip)], run), tuple(init))
    else:
        pool_rows([0], None)

    ch = RG_CHUNK
    n_ch = t_len // ch
    rg_lo, rg_hi = D_POOL, D_POOL + D_RG

    def gates(c0, d):
        cbuf[...] = _load_ext(z_ref, c0, ch, t_len, rg_lo, rg_hi)
        xc = cb_ref[...]
        for j in range(CONV_W):
            off = HALO - CONV_LEFT + j
            xc = xc + cbuf[off:off + ch, :] * cw_ref[j:j + 1, :]
        g = _dot(xc.astype(BF16), wg_ref[:, d * 2 * D_RG:(d + 1) * 2 * D_RG]) \
            + bg_ref[:, d * 2 * D_RG:(d + 1) * 2 * D_RG]
        r = _sigmoid(g[:, :D_RG])
        i = _sigmoid(g[:, D_RG:])
        log_a = (-RG_C) * r * _softplus(-lam_ref[d:d + 1, :])
        a = jnp.exp(log_a)
        a_s[...] = a
        one_minus_a2 = jnp.tanh(-log_a) * (a * a + 1.0)
        root = jnp.where(one_minus_a2 > 0.0, one_minus_a2 * lax.rsqrt(one_minus_a2), 0.0)
        b_s[...] = root * (i * xc)

    def scan_rows(h, reverse):
        def row(t, h):
            tt = (ch - 1 - t) if reverse else t
            h = a_s[pl.ds(tt, 1), :] * h + b_s[pl.ds(tt, 1), :]
            h_s[pl.ds(tt, 1), :] = h
            return h
        return lax.fori_loop(0, ch, row, h, unroll=8)

    def fwd_body(i, h):
        c0 = pl.multiple_of(i * ch, ch)
        gates(c0, 0)
        h = scan_rows(h, False)
        hf_s[pl.ds(c0, ch), :] = h_s[...]
        return h

    h_fwd = lax.fori_loop(0, n_ch, fwd_body, h0_ref[0:1, :])

    def bwd_body(i, h):
        c0 = pl.multiple_of((n_ch - 1 - i) * ch, ch)
        gates(c0, 1)
        h = scan_rows(h, True)
        u_gate = z_ref[pl.ds(c0, ch), rg_hi:rg_hi + D_RG].astype(F32)
        y = (hf_s[pl.ds(c0, ch), :] + h_s[...]) * _gelu_tanh(u_gate)
        y_ref[pl.ds(c0, ch), D_POOL:D_POOL + D_RG] = y.astype(y_ref.dtype)
        return h

    h_bwd = lax.fori_loop(0, n_ch, bwd_body, h0_ref[1:2, :])
    hfin_ref[0:1, :] = h_fwd
    hfin_ref[1:2, :] = h_bwd


def _even_mixer(z, h0, p, n_batch, t_len, grid):
    assert z.dtype == BF16
    seg = GRID_W if grid else t_len
    const2 = lambda b: (0, 0)
    const3 = lambda b: (0, 0, 0)
    idx = jnp.arange(seg)
    delta = idx[None, :] - idx[:, None]
    band = jnp.stack([jnp.logical_and(delta >= -(w // 2), delta <= w - 1 - w // 2) for w in POOL_WINDOWS])
    band = jnp.tile(band.astype(BF16), (1, 1, 3 if grid else 1))
    return pl.pallas_call(
        functools.partial(_even_kernel, t_len=t_len, grid=grid),
        grid=(n_batch,),
        in_specs=[
            pl.BlockSpec((t_len, EV_COLS), lambda b: (b, 0)),
            pl.BlockSpec((None, 2, D_RG), lambda b: (b, 0, 0)),
            pl.BlockSpec(p['pool_w'].shape, const3),
            pl.BlockSpec((1, D_POOL), const2),
            pl.BlockSpec((CONV_W, D_RG), const2),
            pl.BlockSpec((1, D_RG), const2),
            pl.BlockSpec((D_RG, 4 * D_RG), const2),
            pl.BlockSpec((1, 4 * D_RG), const2),
            pl.BlockSpec((2, D_RG), const2),
            pl.BlockSpec(band.shape, const3),
        ],
        out_specs=[
            pl.BlockSpec((t_len, D_POOL + D_RG), lambda b: (b, 0)),
            pl.BlockSpec((None, 2, D_RG), lambda b: (b, 0, 0)),
        ],
        out_shape=[
            jax.ShapeDtypeStruct((n_batch * t_len, D_POOL + D_RG), Y_DTYPE),
            jax.ShapeDtypeStruct((n_batch, 2, D_RG), F32),
        ],
        scratch_shapes=[
            pltpu.VMEM((RG_CHUNK + 2 * HALO, D_RG), F32),
            pltpu.VMEM((RG_CHUNK, D_RG), F32),
            pltpu.VMEM((RG_CHUNK, D_RG), F32),
            pltpu.VMEM((RG_CHUNK, D_RG), F32),
            pltpu.VMEM((t_len, D_RG), F32),
        ],
        compiler_params=_ARB1,
    )(z, h0, p['pool_w'], p['pool_scale'], p['conv_w'], p['conv_b'], p['wg'], p['bg'], p['lam'], band)


def _rwkv_kernel(z_ref, s0_ref, mu_ref, kkw_ref, ka_ref, rk_ref, g2_ref, w0_ref, w2_ref, a0_ref, a2_ref,
                 lnw_ref, lnb_ref, tri_ref, bd_ref,
                 y_ref, sfin_ref, ebuf, s_s, yd_s, bon_s, g_s, *, t_len):
    L = SCAN_CHUNK
    n_ch = t_len // L
    L2 = 2 * L
    zero = jnp.zeros((RW_HEAD, RW_HEAD), F32)
    for d in range(2):
        for p in range(RW_PAIRS):
            top = jnp.concatenate([s0_ref[d, 2 * p], zero], axis=1)
            bot = jnp.concatenate([zero, s0_ref[d, 2 * p + 1]], axis=1)
            s_s[d, p] = jnp.concatenate([top, bot], axis=0)

    def stack(x):
        first_head = lax.broadcasted_iota(jnp.int32, (L, LANES), 1) < RW_HEAD
        return jnp.concatenate([jnp.where(first_head, x, 0.0), jnp.where(first_head, 0.0, x)],
                               axis=0).astype(BF16)

    def prep(items):
        ones_bd = bd_ref[...]
        o = 3 * D_RW
        sts = []
        for n, (c0, d) in enumerate(items):
            ebuf[n] = _load_ext(z_ref, c0, L, t_len, 0, RW_ZCOLS)
            cur = ebuf[n, HALO:HALO + L, :]
            prev = ebuf[n, HALO - 1:HALO - 1 + L, :]
            nxt = ebuf[n, HALO + 1:HALO + 1 + L, :]
            zs = cur + mu_ref[0:1, :] * (prev - cur) + mu_ref[1:2, :] * (nxt - cur)
            sts.append(dict(c0=c0, d=d, r=zs[:, 0:D_RW], k=zs[:, D_RW:2 * D_RW], v=zs[:, 2 * D_RW:3 * D_RW],
                            gd=zs[:, o:o + RW_G_LORA], wd=zs[:, o + RW_G_LORA:o + RW_G_LORA + LANES],
                            ad=zs[:, o + RW_G_LORA + LANES:o + RW_G_LORA + 2 * LANES]))
        for st in sts:
            d = st['d']
            st['kkv'] = st['k'] * kkw_ref[...]
            st['ss'] = _head_sum(st['kkv'] * st['kkv'], ones_bd)
            tw_hi, tw_lo = _split2(jnp.tanh(st['wd']))
            st['w_lora'] = _dot(tw_hi, w2_ref[d, 0]) + (_dot(tw_hi, w2_ref[d, 1]) + _dot(tw_lo, w2_ref[d, 0]))
            st['a_lora'] = _dot(st['ad'].astype(BF16), a2_ref[d])
            if d == 0:
                g = _dot(_sigmoid(st['gd']).astype(BF16), g2_ref[...])
                g_s[pl.ds(st['c0'], L), :] = g.astype(g_s.dtype)
        for st in sts:
            d = st['d']
            st['kkn'] = st['kkv'] / jnp.maximum(jnp.sqrt(st['ss']), 1e-12)
            w_log = -_softplus(-(w0_ref[d:d + 1, :] + st['w_lora'])) - 0.5
            st['lw'] = -jnp.exp(w_log)
            st['a'] = _sigmoid(a0_ref[d:d + 1, :] + st['a_lora'])
            st['kd'] = st['k'] * (1.0 + (st['a'] - 1.0) * ka_ref[...])
            st['bon'] = _head_sum(st['r'] * st['kd'] * rk_ref[...], ones_bd, single_pass=True)
            st['cum'] = _dot_exact_lhs(tri_ref[d], st['lw'])
        out = []
        for st in sts:
            d, r, v, kd, kkn, a, cum = st['d'], st['r'], st['v'], st['kd'], st['kkn'], st['a'], st['cum']
            bon_s[d, pl.ds(st['c0'], L), :] = (st['bon'] * v).astype(bon_s.dtype)
            e_pos = jnp.exp(cum)
            e_neg = jnp.exp(-cum)
            rh = r * e_pos
            kh = kd * e_neg
            bh = (kkn * a) * e_neg
            ah = -kkn * jnp.exp(cum - st['lw'])
            last = L - 1 if d == 0 else 0
            gam = e_pos[last:last + 1, :]
            bhg = bh * gam
            khg = kh * gam
            chains = []
            for p in range(RW_PAIRS):
                cols = slice(p * LANES, (p + 1) * LANES)
                chains.append(dict(
                    d=d, p=p, gam=gam[:, cols], v_st=stack(v[:, cols]),
                    lhs=jnp.concatenate([stack(ah[:, cols]), stack(rh[:, cols])], axis=0),
                    rhs=jnp.concatenate([stack(bh[:, cols]), stack(kh[:, cols])], axis=0),
                    bk=jnp.concatenate([stack(bhg[:, cols]), stack(khg[:, cols])], axis=0)))
            out.append(chains)
        return out

    n_levels = int(math.log2(L))

    def state_free(chains):
        ri = lax.broadcasted_iota(jnp.int32, (L2, L2), 0)
        ci = lax.broadcasted_iota(jnp.int32, (L2, L2), 1)
        eye = jnp.where(ri == ci, 1.0, 0.0)
        rl = jnp.where(ri < L, ri, ri - L)
        cl = jnp.where(ci < L, ci, ci - L)
        strict = (cl < rl, cl > rl)
        incl = (cl <= rl, cl >= rl)
        level = ri ^ ci

        for ch in chains:
            ch['sc'] = _dot_nt(ch['lhs'], ch['rhs'])
        for ch in chains:
            sc, d = ch['sc'], ch['d']
            n_ab = jnp.where(strict[d], sc[0:L2, 0:L2], 0.0)
            m_ak = jnp.where(strict[d], sc[0:L2, L2:2 * L2], 0.0).astype(BF16)
            t_rb = jnp.where(incl[d], sc[L2:2 * L2, 0:L2], 0.0)
            t_rk = jnp.where(incl[d], sc[L2:2 * L2, L2:2 * L2], 0.0)
            ch['t_rbk'] = jnp.concatenate([t_rb, t_rk], axis=1).astype(BF16)
            ch['mv'] = _dot(m_ak, ch['v_st'])
            ch['t'] = eye + jnp.where(level == 1, n_ab, 0.0)
            ch['n_lv'] = [jnp.where(lax.shift_right_logical(level, lv) == 1, n_ab, 0.0).astype(BF16)
                          for lv in range(1, n_levels)]
            del ch['sc']
        for lv in range(n_levels - 1):
            for ch in chains:
                ch['tb'] = ch['t'].astype(BF16)
                ch['x'] = _dot(ch['n_lv'][lv], ch['tb']).astype(BF16)
            for ch in chains:
                ch['t'] = ch['t'] + _dot(ch['tb'], ch['x'])

    def state_step(chains, c0s):
        for ch in chains:
            ch['s_prev'] = s_s[ch['d'], ch['p']]
            ch['ar_s'] = _dot_nt(ch['lhs'], ch['s_prev'].astype(BF16))
        for ch in chains:
            u_rhs = ch['ar_s'][0:L2] + ch['mv']
            u_st = _dot(ch['t'].astype(BF16), u_rhs.astype(BF16)).astype(BF16)
            ch['uv'] = jnp.concatenate([u_st, ch['v_st']], axis=0)
        for ch in chains:
            y_st = ch['ar_s'][L2:2 * L2] + _dot(ch['t_rbk'], ch['uv'])
            ch['y'] = y_st[0:L] + y_st[L:L2]
            s_s[ch['d'], ch['p']] = ch['s_prev'] * ch['gam'] + _dot_tn(ch['uv'], ch['bk'])
        for d in range(2):
            yd_s[d, pl.ds(c0s[d], L), :] = jnp.concatenate([ch['y'] for ch in chains if ch['d'] == d], axis=1)

    steps = 2

    def body(i, carry):
        starts = []
        for j in range(steps):
            step = i * steps + j
            starts.append((pl.multiple_of(step * L, L), pl.multiple_of((n_ch - 1 - step) * L, L)))
        prepared = prep([(c0s[d], d) for c0s in starts for d in range(2)])
        groups = [(prepared[2 * j] + prepared[2 * j + 1], starts[j]) for j in range(steps)]
        state_free([ch for chains, _ in groups for ch in chains])
        for chains, c0s in groups:
            state_step(chains, c0s)
        return carry

    lax.fori_loop(0, n_ch // steps, body, 0)

    def epilogue(i, carry):
        c0 = pl.multiple_of(i * L, L)
        ones_bd = bd_ref[...]
        ys = yd_s[0, pl.ds(c0, L), :] + yd_s[1, pl.ds(c0, L), :]
        mu = _head_sum(ys, ones_bd) * (1.0 / RW_HEAD)
        yc = ys - mu
        var = _head_sum(yc * yc, ones_bd) * (1.0 / RW_HEAD)
        yn = yc * lax.rsqrt(var + RW_LN_EPS) * lnw_ref[...] + lnb_ref[...]
        bonus = bon_s[0, pl.ds(c0, L), :].astype(F32) + bon_s[1, pl.ds(c0, L), :].astype(F32)
        out = (yn + bonus) * g_s[pl.ds(c0, L), :].astype(F32)
        y_ref[pl.ds(c0, L), :] = out.astype(y_ref.dtype)
        return carry

    lax.fori_loop(0, n_ch, epilogue, 0, unroll=min(EPILOGUE_UNROLL, n_ch))
    for d in range(2):
        for p in range(RW_PAIRS):
            s_pair = s_s[d, p]
            sfin_ref[d, 2 * p] = s_pair[0:RW_HEAD, 0:RW_HEAD]
            sfin_ref[d, 2 * p + 1] = s_pair[RW_HEAD:, RW_HEAD:]


def _rwkv_mixer(z, s0, p, n_batch, t_len):
    L = SCAN_CHUNK
    c2 = lambda b: (0, 0)
    c3 = lambda b: (0, 0, 0)
    st_spec = pl.BlockSpec((None, 2, RW_HEADS, RW_HEAD, RW_HEAD), lambda b: (b, 0, 0, 0, 0))
    return pl.pallas_call(
        functools.partial(_rwkv_kernel, t_len=t_len),
        grid=(n_batch,),
        in_specs=[
            pl.BlockSpec((t_len, RW_ZCOLS), lambda b: (b, 0)),
            st_spec,
            pl.BlockSpec((2, RW_ZCOLS), c2),
            pl.BlockSpec((1, D_RW), c2),
            pl.BlockSpec((1, D_RW), c2),
            pl.BlockSpec((1, D_RW), c2),
            pl.BlockSpec((RW_G_LORA, D_RW), c2),
            pl.BlockSpec((2, D_RW), c2),
            pl.BlockSpec((2, 2, LANES, D_RW), lambda b: (0, 0, 0, 0)),
            pl.BlockSpec((2, D_RW), c2),
            pl.BlockSpec((2, LANES, D_RW), c3),
            pl.BlockSpec((1, D_RW), c2),
            pl.BlockSpec((1, D_RW), c2),
            pl.BlockSpec((2, L, L), c3),
            pl.BlockSpec((LANES, LANES), c2),
        ],
        out_specs=[pl.BlockSpec((t_len, D_RW), lambda b: (b, 0)), st_spec],
        out_shape=[
            jax.ShapeDtypeStruct((n_batch * t_len, D_RW), Y_DTYPE),
            jax.ShapeDtypeStruct((n_batch, 2, RW_HEADS, RW_HEAD, RW_HEAD), F32),
        ],
        scratch_shapes=[
            pltpu.VMEM((4, L + 2 * HALO, RW_ZCOLS), F32),
            pltpu.VMEM((2, RW_PAIRS, LANES, LANES), F32),
            pltpu.VMEM((2, t_len, D_RW), F32),
            pltpu.VMEM((2, t_len, D_RW), BF16),
            pltpu.VMEM((t_len, D_RW), BF16),
        ],
        compiler_params=_ARB1,
    )(z, s0, p['mu'], p['kk'], p['ka'], p['rk'], p['g2'], p['w0'], p['w2'], p['a0'], p['a2'],
      p['ln_w'], p['ln_b'], p['tri'], p['ones_bd'])


def _mlstm_kernel(z_ref, c0_ref, n0_ref, m0_ref, cw_ref, cb_ref, gb_ref, nw_ref, tri_ref, esel_ref, shift_ref,
                  ones_ref, y_ref, cfin_ref, nfin_ref, mfin_ref, c_s, n_s, m_s, hd_s, qk_s, gx_s, gr_s,
                  *, t_len):
    L = ML_SCAN
    n_ch = t_len // L
    c_s[...] = c0_ref[...]
    n_s[...] = n0_ref[...]
    m_s[...] = m0_ref[...]
    q_scale = ML_DK ** -0.5
    g_lo = 4 * D_ML
    n_real = 4 * ML_HEADS

    def conv_chunk(i, carry):
        c0 = pl.multiple_of(i * L, L)
        gts = z_ref[pl.ds(c0, L), g_lo:g_lo + LANES].astype(F32) + gb_ref[...]
        x16 = gts.T[0:n_real, :]
        lf_hi, lf_lo = _split2(-_softplus(-x16))
        lane = lax.broadcasted_iota(jnp.int32, (8, L), 1)
        stats = []
        for d in range(2):
            b16 = _dot_nt(lf_hi, tri_ref[d]) + _dot_nt(lf_lo, tri_ref[d])
            x8 = x16[8 * d:8 * d + 8, :]
            b8 = b16[8 * d:8 * d + 8, :]
            g8 = pltpu.roll(x8, ML_HEADS, axis=0) - b8
            stats.append([g8, b8, g8])
        step = 1
        while step < L:
            for d in range(2):
                cm8 = stats[d][2]
                if d == 0:
                    sh = jnp.where(lane >= step, pltpu.roll(cm8, step, axis=1), -jnp.inf)
                else:
                    sh = jnp.where(lane < L - step, pltpu.roll(cm8, L - step, axis=1), -jnp.inf)
                stats[d][2] = jnp.maximum(cm8, sh)
            step *= 2

        ext = _load_ext(z_ref, c0, L, t_len, 0, 2 * D_ML, raw=True)
        taps = _dot(shift_ref[...], ext)
        qk = cb_ref[...] + ext[HALO:HALO + L].astype(F32) * cw_ref[CONV_LEFT:CONV_LEFT + 1, :]
        for n, j in enumerate(jj for jj in range(CONV_W) if jj != CONV_LEFT):
            qk = qk + taps[n * L:(n + 1) * L] * cw_ref[j:j + 1, :]
        qk = _silu(qk)
        qk_s[pl.ds(c0, L), 0:D_ML] = (qk[:, :D_ML] * q_scale).astype(BF16)
        qk_s[pl.ds(c0, L), D_ML:2 * D_ML] = qk[:, D_ML:].astype(BF16)

        for d, (g8, b8, cm8) in enumerate(stats):
            x32 = jnp.concatenate([g8, b8, cm8, jnp.zeros((8, L), F32)], axis=0)
            hi = x32.astype(BF16)
            r1 = x32 - hi.astype(F32)
            mid = r1.astype(BF16)
            lo = (r1 - mid.astype(F32)).astype(BF16)
            gx_s[d, i] = jnp.concatenate([hi, mid, lo], axis=0)
            gr_s[d, i] = g8
        return carry

    lax.fori_loop(0, n_ch, conv_chunk, 0, unroll=min(ML_UNROLL, n_ch))

    def prep(ci, d):
        c0 = pl.multiple_of(ci * L, L)
        q = qk_s[pl.ds(c0, L), 0:D_ML]
        k = qk_s[pl.ds(c0, L), D_ML:2 * D_ML]
        v = z_ref[pl.ds(c0, L), 2 * D_ML:3 * D_ML]
        g8 = gr_s[d, ci]
        cb = _dot_tn(gx_s[d, ci], esel_ref[...])
        ri = lax.broadcasted_iota(jnp.int32, (L, L), 0)
        ci = lax.broadcasted_iota(jnp.int32, (L, L), 1)
        causal = (ci <= ri) if d == 0 else (ci >= ri)
        last = L - 1 if d == 0 else 0
        ones = ones_ref[...]
        chains = []
        for h in range(ML_HEADS):
            cols = slice(h * ML_DK, (h + 1) * ML_DK)
            chains.append(dict(
                d=d, h=h, last=last, causal=causal,
                g_row=g8[ML_HEADS + h:ML_HEADS + h + 1, :],
                g_b=cb[:, cols], b_b=cb[:, D_ML + h * ML_DK:D_ML + (h + 1) * ML_DK],
                cm_b=cb[:, 2 * D_ML + h * ML_DK:2 * D_ML + (h + 1) * ML_DK],
                qb=q[:, cols], kb=k[:, cols],
                v=v[:, cols].astype(F32), v_ones=jnp.concatenate([v[:, cols].astype(BF16), ones], axis=1)))
        return chains

    def body(i, carry):
        c0s = (pl.multiple_of(i * L, L), pl.multiple_of((n_ch - 1 - i) * L, L))
        chains = prep(i, 0) + prep(n_ch - 1 - i, 1)
        for ch in chains:
            d, h = ch['d'], ch['h']
            ch['qk'] = _dot_nt(ch['qb'], ch['kb'])
            ch['c_prev'] = c_s[d, h]
            ch['n_prev'] = n_s[d, h]
            ch['m_prev'] = m_s[d, h:h + 1, :]
            ch['qcn'] = _dot(ch['qb'], jnp.concatenate([ch['c_prev'].astype(BF16),
                                                        ch['n_prev'].astype(BF16)], axis=1))
        for ch in chains:
            mu = jnp.maximum(ch['m_prev'], ch['cm_b'])
            ch['mu'] = mu
            ch['s'] = ch['qk'] * jnp.exp(jnp.where(ch['causal'], ch['g_row'] - mu, -jnp.inf))
            ch['w_inter'] = jnp.exp(ch['m_prev'] - mu)
        for ch in chains:
            last = ch['last']
            ch['sv'] = _dot(ch['s'].astype(BF16), ch['v_ones'])
            mu_last = ch['mu'][last:last + 1, :]
            ch['m_new'] = ch['b_b'][last:last + 1, :] + mu_last
            ch['dec'] = jnp.exp(ch['m_prev'] - mu_last)
            w_s = jnp.exp(ch['g_b'] - mu_last)
            ch['upd'] = _dot_tn(ch['kb'], jnp.concatenate([(w_s * ch['v']).astype(BF16),
                                                           w_s.astype(BF16)], axis=1))
        for ch in chains:
            d, h = ch['d'], ch['h']
            num = ch['sv'][:, :ML_DK] + ch['w_inter'] * ch['qcn'][:, :ML_DK]
            den = ch['sv'][:, ML_DK:] + ch['w_inter'] * ch['qcn'][:, ML_DK:]
            ch['out'] = num / jnp.maximum(jnp.abs(den), jnp.exp(-(ch['b_b'] + ch['mu'])))
            c_s[d, h] = ch['dec'] * ch['c_prev'] + ch['upd'][:, :ML_DK]
            n_s[d, h] = ch['dec'] * ch['n_prev'] + ch['upd'][:, ML_DK:]
            m_s[d, h:h + 1, :] = ch['m_new']
        for d in range(2):
            hd_s[d, pl.ds(c0s[d], L), :] = jnp.concatenate([ch['out'] for ch in chains if ch['d'] == d], axis=1)
        return carry

    lax.fori_loop(0, n_ch, body, 0, unroll=min(ML_UNROLL, n_ch))

    def epilogue(i, carry):
        c0 = pl.multiple_of(i * L, L)
        ones = ones_ref[...]
        hs = hd_s[0, pl.ds(c0, L), :] + hd_s[1, pl.ds(c0, L), :]
        o = z_ref[pl.ds(c0, L), 3 * D_ML:4 * D_ML].astype(F32)
        mu = _head_sum(hs, ones) * (1.0 / ML_DK)
        xc = hs - mu
        var = _head_sum(xc * xc, ones) * (1.0 / ML_DK)
        y = xc * lax.rsqrt(var + NORM_EPS) * nw_ref[...] * _sigmoid(o)
        y_ref[pl.ds(c0, L), :] = y.astype(y_ref.dtype)
        return carry

    lax.fori_loop(0, n_ch, epilogue, 0, unroll=min(EPILOGUE_UNROLL, n_ch))
    cfin_ref[...] = c_s[...]
    nfin_ref[...] = n_s[...]
    mfin_ref[...] = m_s[...]


def _mlstm_mixer(z, c0, n0, m0, p, n_batch, t_len):
    assert z.dtype == BF16
    L = ML_SCAN
    c2 = lambda b: (0, 0)
    c3 = lambda b: (0, 0, 0)
    c_spec = pl.BlockSpec((None, 2, ML_HEADS, ML_DK, ML_DK), lambda b: (b, 0, 0, 0, 0))
    m_spec = pl.BlockSpec((None, 2, ML_HEADS, ML_DK), lambda b: (b, 0, 0, 0))
    return pl.pallas_call(
        functools.partial(_mlstm_kernel, t_len=t_len),
        grid=(n_batch,),
        in_specs=[
            pl.BlockSpec((t_len, ML_ZCOLS), lambda b: (b, 0)),
            c_spec, c_spec, m_spec,
            pl.BlockSpec((CONV_W, 2 * D_ML), c2),
            pl.BlockSpec((1, 2 * D_ML), c2),
            pl.BlockSpec((1, LANES), c2),
            pl.BlockSpec((1, D_ML), c2),
            pl.BlockSpec((2, L, L), c3),
            pl.BlockSpec(p['esel'].shape, c2),
            pl.BlockSpec(p['shift'].shape, c2),
            pl.BlockSpec((L, LANES), c2),
        ],
        out_specs=[pl.BlockSpec((t_len, D_ML), lambda b: (b, 0)), c_spec, c_spec, m_spec],
        out_shape=[
            jax.ShapeDtypeStruct((n_batch * t_len, D_ML), Y_DTYPE),
            jax.ShapeDtypeStruct((n_batch, 2, ML_HEADS, ML_DK, ML_DK), F32),
            jax.ShapeDtypeStruct((n_batch, 2, ML_HEADS, ML_DK, ML_DK), F32),
            jax.ShapeDtypeStruct((n_batch, 2, ML_HEADS, ML_DK), F32),
        ],
        scratch_shapes=[
            pltpu.VMEM((2, ML_HEADS, ML_DK, ML_DK), F32),
            pltpu.VMEM((2, ML_HEADS, ML_DK, ML_DK), F32),
            pltpu.VMEM((2, ML_HEADS, ML_DK), F32),
            pltpu.VMEM((2, t_len, D_ML), F32),
            pltpu.VMEM((t_len, 2 * D_ML), BF16),
            pltpu.VMEM((2, t_len // L, 96, L), BF16),
            pltpu.VMEM((2, t_len // L, 8, L), F32),
        ],
        compiler_params=_ARB1,
    )(z, c0, n0, m0, p['conv_w'], p['conv_b'], p['gate_b'], p['norm_w'], p['tri'], p['esel'], p['shift'],
      p['ones'])


def _conv_shift(n):
    taps = jnp.array([j for j in range(CONV_W) if j != CONV_LEFT])
    t = jnp.arange((CONV_W - 1) * n)
    src = HALO + t % n + taps[t // n] - CONV_LEFT
    return (src[:, None] == jnp.arange(n + 2 * HALO)[None, :]).astype(BF16)


def _mlstm_consts():
    L = ML_SCAN
    lower = jnp.tril(jnp.ones((L, L), F32))
    tri = jnp.stack([lower, lower.T]).astype(BF16)
    rows = jnp.arange(32)
    blk = jnp.arange(3 * D_ML) // ML_DK
    src_row = 8 * (blk // ML_HEADS) + ML_HEADS + blk % ML_HEADS
    esel = (rows[:, None] == src_row[None, :]).astype(BF16)
    esel = jnp.concatenate([esel, esel, esel], axis=0)
    return dict(tri=tri, esel=esel, shift=_conv_shift(L), ones=jnp.ones((L, LANES), BF16))


def _block_diag(w):
    n, c, d = w.shape
    return jnp.einsum('ncd,nm->ncmd', w, jnp.eye(n, dtype=w.dtype)).reshape(n * c, n * d)


def _scan_tri():
    L = SCAN_CHUNK
    lower = jnp.tril(jnp.ones((L, L), F32))
    return jnp.stack([lower, lower.T]).astype(BF16)


def _prep_even(ev_w_in, pool_w, pool_scale, rg_conv_w, rg_conv_b, rg_wa, rg_ba, rg_wx, rg_bx, rg_lam, ev_w_out, j):
    wg = jnp.concatenate([_block_diag(rg_wa[j, 0]), _block_diag(rg_wx[j, 0]),
                          _block_diag(rg_wa[j, 1]), _block_diag(rg_wx[j, 1])], axis=1)
    bg = jnp.concatenate([rg_ba[j, 0], rg_bx[j, 0], rg_ba[j, 1], rg_bx[j, 1]]).reshape(1, 4 * D_RG)
    return dict(
        w_in=[ev_w_in[j].astype(BF16)],
        w_out=ev_w_out[j].astype(BF16),
        pool_w=pool_w[j].astype(BF16),
        pool_scale=pool_scale[j].reshape(1, D_POOL),
        conv_w=rg_conv_w[j],
        conv_b=rg_conv_b[j].reshape(1, D_RG),
        wg=wg.astype(BF16),
        bg=bg,
        lam=rg_lam[j],
    )


def _pad_cols(w, cuts, widths):
    pieces = jnp.split(w, cuts, axis=-1)
    out = []
    for piece, width in zip(pieces, widths):
        extra = width - piece.shape[-1]
        if extra:
            piece = jnp.pad(piece, [(0, 0)] * (piece.ndim - 1) + [(0, extra)])
        out.append(piece)
    return jnp.concatenate(out, axis=-1)


def _prep_odd(od_w_in, rw_mu, rw_w0, rw_w2, rw_a0, rw_a2, rw_kk, rw_ka, rw_rk, rw_g2, rw_ln_w, rw_ln_b,
              ml_conv_w, ml_conv_b, ml_bi, ml_bf, ml_norm_w, od_w_out, j):
    rw_cols = 3 * D_RW + RW_G_LORA + RW_W_LORA + RW_A_LORA
    base = 3 * D_RW + RW_G_LORA
    cuts = [base, base + RW_W_LORA]
    widths = [base, LANES, LANES]
    w_rw = _pad_cols(od_w_in[j][:, :rw_cols], cuts, widths)
    w_ml = _pad_cols(od_w_in[j][:, rw_cols:], [4 * D_ML], [4 * D_ML, LANES])
    pad_rows = lambda w: jnp.pad(w, ((0, 0), (0, LANES - w.shape[1]), (0, 0)))
    gate_b = jnp.concatenate([ml_bi[j, 0], ml_bf[j, 0], ml_bi[j, 1], ml_bf[j, 1]])
    gate_b = jnp.pad(gate_b, (0, LANES - gate_b.shape[0])).reshape(1, LANES)
    hm = jnp.arange(LANES) // RW_HEAD
    rw = dict(
        mu=_pad_cols(rw_mu[j], cuts, widths),
        kk=rw_kk[j].reshape(1, D_RW), ka=rw_ka[j].reshape(1, D_RW), rk=rw_rk[j].reshape(1, D_RW),
        g2=rw_g2[j].astype(BF16),
        w0=rw_w0[j], w2=jnp.stack(_split2(pad_rows(rw_w2[j])), axis=1),
        a0=rw_a0[j], a2=pad_rows(rw_a2[j]).astype(BF16),
        ln_w=rw_ln_w[j].reshape(1, D_RW), ln_b=rw_ln_b[j].reshape(1, D_RW),
        tri=_scan_tri(),
        ones_bd=(hm[:, None] == hm[None, :]).astype(BF16),
    )
    ml = dict(
        conv_w=ml_conv_w[j], conv_b=ml_conv_b[j].reshape(1, 2 * D_ML),
        gate_b=gate_b, norm_w=ml_norm_w[j].reshape(1, D_ML), **_mlstm_consts(),
    )
    return dict(w_in=[w_rw.astype(BF16), w_ml.astype(BF16)], w_out=od_w_out[j].astype(BF16), rw=rw, ml=ml)


def _trunk(x, mod, rows_per_mod, mod_base, grid, states, params, n_batch, t_len):
    ev, od, dense = params
    rg0, rw0, c0, n0, m0 = states
    x = x.reshape(n_batch * t_len, D_MODEL)
    spec = (rows_per_mod, mod_base)

    (z,) = _in_proj(x, mod[0], *spec, dense['norm1_w'][0], ev['w_in'])
    y_ev, rg_f = _even_mixer(z, rg0, ev, n_batch, t_len, grid)
    x = _out_mlp([y_ev], x, mod[0], *spec, dense['norm2_w'][0], ev['w_out'],
                 dense['mlp_w1'], dense['mlp_w2'], 0, dense['final_norm_w'], False)

    z_rw, z_ml = _in_proj(x, mod[1], *spec, dense['norm1_w'][1], od['w_in'])
    y_rw, s_f = _rwkv_mixer(z_rw, rw0, od['rw'], n_batch, t_len)
    y_ml, c_f, n_f, m_f = _mlstm_mixer(z_ml, c0, n0, m0, od['ml'], n_batch, t_len)
    y = _out_mlp([y_rw, y_ml], x, mod[1], *spec, dense['norm2_w'][1], od['w_out'],
                 dense['mlp_w1'], dense['mlp_w2'], 1, dense['final_norm_w'], True)
    return y.reshape(n_batch, t_len, D_MODEL), (rg_f, s_f, c_f, n_f, m_f)


def kernel(x_prompt, x_sample, state_rglru, state_rwkv, state_mlstm_C, state_mlstm_n, state_mlstm_m,
           c, c_ctx, norm1_w, norm2_w, w_mod, b_mod, mlp_w1, mlp_w2, final_norm_w,
           ev_w_in, pool_w, pool_scale, rg_conv_w, rg_conv_b, rg_wa, rg_ba, rg_wx, rg_bx, rg_lam, ev_w_out,
           od_w_in, rw_mu, rw_w0, rw_w2, rw_a0, rw_a2, rw_kk, rw_ka, rw_rk, rw_g2, rw_ln_w, rw_ln_b,
           ml_conv_w, ml_conv_b, ml_bi, ml_bf, ml_norm_w, od_w_out):
    bp, tp, _ = x_prompt.shape
    bs, ts, _ = x_sample.shape

    assert bs + 1 <= N_COND_ROWS and ts % ROW_BLOCK == 0 and (bp * tp) % ROW_BLOCK == 0
    cond = jnp.concatenate([c_ctx[None, :], c, jnp.zeros((N_COND_ROWS - 1 - bs, D_MODEL), F32)], axis=0)
    mod = _modulation(cond, w_mod, b_mod)

    ev = _prep_even(ev_w_in, pool_w, pool_scale, rg_conv_w, rg_conv_b, rg_wa, rg_ba, rg_wx, rg_bx, rg_lam,
                    ev_w_out, 0)
    od = _prep_odd(od_w_in, rw_mu, rw_w0, rw_w2, rw_a0, rw_a2, rw_kk, rw_ka, rw_rk, rw_g2, rw_ln_w, rw_ln_b,
                   ml_conv_w, ml_conv_b, ml_bi, ml_bf, ml_norm_w, od_w_out, 0)
    dense = dict(norm1_w=norm1_w, norm2_w=norm2_w, mlp_w1=mlp_w1.astype(BF16), mlp_w2=mlp_w2.astype(BF16),
                 final_norm_w=final_norm_w)
    params = (ev, od, dense)

    zero_states = (
        jnp.zeros((bp, 2, D_RG), F32),
        jnp.zeros((bp, 2, RW_HEADS, RW_HEAD, RW_HEAD), F32),
        jnp.zeros((bp, 2, ML_HEADS, ML_DK, ML_DK), F32),
        jnp.zeros((bp, 2, ML_HEADS, ML_DK, ML_DK), F32),
        jnp.zeros((bp, 2, ML_HEADS, ML_DK), F32),
    )
    y_prompt, (rg_f, s_f, c_f, n_f, m_f) = _trunk(
        x_prompt, mod, bp * tp, 0, False, zero_states, params, bp, tp)

    sample_states = (
        state_rglru[:, 0],
        state_rwkv[:, 0],
        state_mlstm_C[:, 0],
        jnp.broadcast_to(state_mlstm_n[:, 0][..., None], (bs, 2, ML_HEADS, ML_DK, ML_DK)),
        jnp.broadcast_to(state_mlstm_m[:, 0][..., None], (bs, 2, ML_HEADS, ML_DK)),
    )
    y_sample, _ = _trunk(x_sample, mod, ts, 1, True, sample_states, params, bs, ts)

    return (y_prompt, y_sample,
            rg_f[:, None],
            s_f[:, None],
            c_f[:, None],
            n_f[:, None, :, :, :, 0],
            m_f[:, None, :, :, 0])
```

```python
import functools
import math

import jax
import jax.numpy as jnp
from jax import lax
from jax.experimental import pallas as pl
from jax.experimental.pallas import tpu as pltpu

F32 = jnp.float32
BF16 = jnp.bfloat16

D_MODEL = 1024
DEPTH = 2
GRID_W = 64
D_FF = 4 * D_MODEL
NORM_EPS = 1e-6
N_MOD = 6

D_POOL = D_MODEL // 2
POOL_WINDOWS = (2, 4, 8, 16)
POOL_GW = D_POOL // len(POOL_WINDOWS)
D_RG = D_MODEL // 2
RG_BLOCKS = 8
RG_C = 8.0
CONV_W = 4
CONV_LEFT = 2
EV_COLS = D_POOL + 2 * D_RG

D_RW = D_MODEL // 2
RW_HEAD = 64
RW_HEADS = D_RW // RW_HEAD
RW_PAIRS = RW_HEADS // 2
RW_W_LORA = 64
RW_A_LORA = 64
RW_G_LORA = 128
RW_LN_EPS = 64e-5
RW_ZCOLS = 3 * D_RW + RW_G_LORA + 128 + 128
D_ML = D_MODEL // 2
ML_HEADS = 4
ML_DK = D_ML // ML_HEADS
ML_ZCOLS = 4 * D_ML + 128

LANES = 128
HALO = 16
Z_DTYPE = BF16
Y_DTYPE = BF16
ROW_BLOCK = 512
SCAN_CHUNK = 64
ML_SCAN = 128
RG_CHUNK = 256
POOL_SEGS_PER_TRIP = 8
ML_UNROLL = 4
EPILOGUE_UNROLL = 8
MOD_COL_BLOCK = 1536
N_COND_ROWS = 16
VMEM_LIMIT = 56 * 1024 * 1024

_ARB1 = pltpu.CompilerParams(dimension_semantics=("arbitrary",), vmem_limit_bytes=VMEM_LIMIT)
_ARB2 = pltpu.CompilerParams(dimension_semantics=("arbitrary", "arbitrary"), vmem_limit_bytes=VMEM_LIMIT)


def _dot(a, b):
    return jnp.dot(a, b, preferred_element_type=F32)


def _dot_nt(a, b):
    return lax.dot_general(a, b, (((1,), (1,)), ((), ())), preferred_element_type=F32)


def _dot_tn(a, b):
    return lax.dot_general(a, b, (((0,), (0,)), ((), ())), preferred_element_type=F32)


def _split2(x):
    hi = x.astype(BF16)
    lo = (x - hi.astype(F32)).astype(BF16)
    return hi, lo


def _dot_f32(a, b):
    ah, al = _split2(a)
    bh, bl = _split2(b)
    return _dot(ah, bh) + (_dot(ah, bl) + _dot(al, bh))


def _dot_exact_lhs(m_bf16, x):
    hi, lo = _split2(x)
    return _dot(m_bf16, hi) + _dot(m_bf16, lo)


def _dot_exact_rhs(x, m_bf16):
    hi, lo = _split2(x)
    return _dot(hi, m_bf16) + _dot(lo, m_bf16)


def _softplus(x):
    return jnp.maximum(x, 0.0) + jnp.log1p(jnp.exp(-jnp.abs(x)))


def _sigmoid(x):
    return jax.nn.sigmoid(x)


def _silu(x):
    return x * jax.nn.sigmoid(x)


def _gelu_tanh(x):
    c = math.sqrt(2.0 / math.pi)
    return x * (0.5 * (1.0 + jnp.tanh(c * (x + 0.044715 * (x * x * x)))))


def _rms_norm(x, w):
    ms = jnp.mean(x * x, axis=-1, keepdims=True)
    return x * lax.rsqrt(ms + NORM_EPS) * w


def _load_ext(z_ref, c0, n, t_len, lo, hi, raw=False):
    dt = z_ref.dtype if raw else F32
    cur = z_ref[pl.ds(c0, n), lo:hi].astype(dt)
    pb = pl.multiple_of(jnp.maximum(c0 - HALO, 0), HALO)
    pa = pl.multiple_of(jnp.minimum(c0 + n, t_len - HALO), HALO)
    before = z_ref[pl.ds(pb, HALO), lo:hi].astype(dt)
    after = z_ref[pl.ds(pa, HALO), lo:hi].astype(dt)
    before = jnp.where(c0 > 0, before, jnp.zeros_like(before))
    after = jnp.where(c0 + n < t_len, after, jnp.zeros_like(after))
    return jnp.concatenate([before, cur, after], axis=0)


def _head_sum(x, ones_bd, single_pass=False):
    n = x.shape[0]
    groups = x.shape[1] // LANES
    xs = jnp.concatenate([x[:, g * LANES:(g + 1) * LANES] for g in range(groups)], axis=0)
    s = _dot(xs.astype(BF16), ones_bd) if single_pass else _dot_exact_rhs(xs, ones_bd)
    return jnp.concatenate([s[g * n:(g + 1) * n] for g in range(groups)], axis=1)


def _mod_kernel(c_ref, w_ref, b_ref, o_ref):
    s = _silu(c_ref[...])
    o_ref[...] = _dot_f32(s, w_ref[...]) + b_ref[...]


def _modulation(cond, w_mod, b_mod):
    rows = cond.shape[0]
    tn = MOD_COL_BLOCK
    out = pl.pallas_call(
        _mod_kernel,
        grid=(DEPTH, N_MOD * D_MODEL // tn),
        in_specs=[
            pl.BlockSpec((rows, D_MODEL), lambda l, j: (0, 0)),
            pl.BlockSpec((None, D_MODEL, tn), lambda l, j: (l, 0, j)),
            pl.BlockSpec((None, 1, tn), lambda l, j: (l, 0, j)),
        ],
        out_specs=pl.BlockSpec((None, rows, tn), lambda l, j: (l, 0, j)),
        out_shape=jax.ShapeDtypeStruct((DEPTH, rows, N_MOD * D_MODEL), F32),
        compiler_params=_ARB2,
    )(cond, w_mod, b_mod.reshape(DEPTH, 1, N_MOD * D_MODEL))
    return out.reshape(DEPTH, rows, N_MOD, D_MODEL)


def _inproj_kernel(*refs, n_out, col_chunk):
    x_ref, mod_ref, nw_ref = refs[:3]
    w_refs = refs[3:3 + n_out]
    z_refs = refs[3 + n_out:]
    h = _rms_norm(x_ref[...], nw_ref[...])
    h = h * (1.0 + mod_ref[1:2, :]) + mod_ref[0:1, :]
    hb = h.astype(BF16)
    for w_ref, z_ref in zip(w_refs, z_refs):
        n = w_ref.shape[1]
        for c in range(0, n, col_chunk):
            e = min(c + col_chunk, n)
            z_ref[:, c:e] = _dot(hb, w_ref[:, c:e]).astype(z_ref.dtype)


def _in_proj(x, mod, rows_per_mod, mod_base, norm_w, weights):
    m = x.shape[0]
    tm = ROW_BLOCK
    n_out = len(weights)
    const = lambda i: (0, 0)
    in_specs = [
        pl.BlockSpec((tm, D_MODEL), lambda i: (i, 0)),
        pl.BlockSpec((None, N_MOD, D_MODEL), lambda i: (mod_base + (i * tm) // rows_per_mod, 0, 0)),
        pl.BlockSpec((1, D_MODEL), const),
    ] + [pl.BlockSpec(w.shape, const, pipeline_mode=pl.Buffered(1)) for w in weights]
    out_specs = [pl.BlockSpec((tm, w.shape[1]), lambda i: (i, 0)) for w in weights]
    out_shape = [jax.ShapeDtypeStruct((m, w.shape[1]), Z_DTYPE) for w in weights]
    return pl.pallas_call(
        functools.partial(_inproj_kernel, n_out=n_out, col_chunk=512),
        grid=(m // tm,),
        in_specs=in_specs,
        out_specs=out_specs,
        out_shape=out_shape,
        compiler_params=_ARB1,
    )(x, mod, norm_w.reshape(1, D_MODEL), *weights)


def _outmlp_kernel(*refs, n_y, final, ff_chunk):
    y_refs = refs[:n_y]
    x_ref, mod_ref, nw_ref, wo_ref, w1_ref, w2_ref, fw_ref, o_ref = refs[n_y:]
    y = None
    row = 0
    for y_ref in y_refs:
        k = y_ref.shape[1]
        part = _dot(y_ref[...], wo_ref[row:row + k, :])
        y = part if y is None else y + part
        row += k
    x = x_ref[...] + mod_ref[2:3, :] * y
    h = _rms_norm(x, nw_ref[...]) * (1.0 + mod_ref[4:5, :]) + mod_ref[3:4, :]
    hb = h.astype(BF16)
    acc = None
    for c in range(0, D_FF, ff_chunk):
        u = jnp.maximum(_dot(hb, w1_ref[:, c:c + ff_chunk]), 0.0)
        part = _dot((u * u).astype(BF16), w2_ref[c:c + ff_chunk, :])
        acc = part if acc is None else acc + part
    x = x + mod_ref[5:6, :] * acc
    if final:
        x = _rms_norm(x, fw_ref[...])
    o_ref[...] = x


def _out_mlp(ys, x, mod, rows_per_mod, mod_base, norm_w, w_out, w1, w2, layer, final_w, final):
    m = x.shape[0]
    tm = ROW_BLOCK
    const = lambda i: (0, 0)
    single = dict(pipeline_mode=pl.Buffered(1))
    pick = lambda i: (layer, 0, 0)
    in_specs = [pl.BlockSpec((tm, y.shape[1]), lambda i: (i, 0)) for y in ys] + [
        pl.BlockSpec((tm, D_MODEL), lambda i: (i, 0)),
        pl.BlockSpec((None, N_MOD, D_MODEL), lambda i: (mod_base + (i * tm) // rows_per_mod, 0, 0)),
        pl.BlockSpec((1, D_MODEL), const),
        pl.BlockSpec(w_out.shape, const, **single),
        pl.BlockSpec((None,) + w1.shape[1:], pick, **single),
        pl.BlockSpec((None,) + w2.shape[1:], pick, **single),
        pl.BlockSpec((1, D_MODEL), const),
    ]
    return pl.pallas_call(
        functools.partial(_outmlp_kernel, n_y=len(ys), final=final, ff_chunk=1024),
        grid=(m // tm,),
        in_specs=in_specs,
        out_specs=pl.BlockSpec((tm, D_MODEL), lambda i: (i, 0)),
        out_shape=jax.ShapeDtypeStruct((m, D_MODEL), F32),
        compiler_params=_ARB1,
    )(*ys, x, mod, norm_w.reshape(1, D_MODEL), w_out, w1, w2, final_w.reshape(1, D_MODEL))


def _even_kernel(z_ref, h0_ref, pw_ref, ps_ref, cw_ref, cb_ref, wg_ref, bg_ref, lam_ref, band_ref,
                 y_ref, hfin_ref, cbuf, a_s, b_s, h_s, hf_s, *, t_len, grid):
    seg = GRID_W if grid else t_len
    n_seg = t_len // seg
    n_groups = len(POOL_WINDOWS)

    pos = lax.broadcasted_iota(jnp.int32, (seg, POOL_GW), 0)
    offs = [(w // 2, w - 1 - w // 2) for w in POOL_WINDOWS]
    gcols = [slice(g * POOL_GW, (g + 1) * POOL_GW) for g in range(n_groups)]

    def grid_row(rr, cols):
        valid = jnp.logical_and(rr >= 0, rr < n_seg)
        src = pl.multiple_of(jnp.clip(rr, 0, n_seg - 1) * seg, seg)
        return jnp.where(valid, z_ref[pl.ds(src, seg), cols].astype(F32), 0.0)

    def pool_rows(rows, run):
        items = []
        for r in rows:
            base = r * seg if isinstance(r, int) else pl.multiple_of(r * seg, seg)
            new_run = []
            for g, (lo_off, hi_off) in enumerate(offs):
                cols = gcols[g]
                if grid:
                    s1 = run[g] + grid_row(r + hi_off, cols) - grid_row(r - lo_off - 1, cols)
                    new_run.append(s1)
                    cnt_r = jnp.minimum(r + hi_off, n_seg - 1) - jnp.maximum(r - lo_off, 0) + 1
                    m1 = s1 / cnt_r.astype(F32)
                    hi = m1.astype(BF16)
                    r1 = m1 - hi.astype(F32)
                    mid = r1.astype(BF16)
                    parts = jnp.concatenate([hi, mid, (r1 - mid.astype(F32)).astype(BF16)], axis=0)
                else:
                    parts = z_ref[pl.ds(base, seg), cols]
                items.append(dict(g=g, base=base, parts=parts))
            run = new_run
        for it in items:
            it['s2'] = _dot(band_ref[it['g']], it['parts'])
        for it in items:
            lo_off, hi_off = offs[it['g']]
            cnt_c = jnp.minimum(pos + hi_off, seg - 1) - jnp.maximum(pos - lo_off, 0) + 1
            xg = z_ref[pl.ds(it['base'], seg), gcols[it['g']]].astype(F32)
            it['d'] = (it['s2'] / cnt_c.astype(F32) - xg).astype(BF16)
        for it in items:
            it['y'] = _dot(it['d'], pw_ref[it['g']])
        for it in items:
            cols = gcols[it['g']]
            y_ref[pl.ds(it['base'], seg), cols] = (it['y'] * ps_ref[:, cols]).astype(y_ref.dtype)
        return tuple(run) if grid else None

    if grid:
        init = []
        for g, (lo_off, hi_off) in enumerate(offs):
            s1 = jnp.zeros((seg, POOL_GW), F32)
            for o in range(min(hi_off, n_seg)):
                s1 = s1 + z_ref[o * seg:(o + 1) * seg, gcols[g]].astype(F32)
            init.append(s1)
        per_trip = POOL_SEGS_PER_TRIP
        lax.fori_loop(0, n_seg // per_trip,
                      lambda i, run: pool_rows([i * per_trip + j for j in range(per_trip)], run), tuple(init))
    else:
        pool_rows([0], None)

    ch = RG_CHUNK
    n_ch = t_len // ch
    rg_lo, rg_hi = D_POOL, D_POOL + D_RG

    def gates(c0, d):
        cbuf[...] = _load_ext(z_ref, c0, ch, t_len, rg_lo, rg_hi)
        xc = cb_ref[...]
        for j in range(CONV_W):
            off = HALO - CONV_LEFT + j
            xc = xc + cbuf[off:off + ch, :] * cw_ref[j:j + 1, :]
        g = _dot(xc.astype(BF16), wg_ref[:, d * 2 * D_RG:(d + 1) * 2 * D_RG]) \
            + bg_ref[:, d * 2 * D_RG:(d + 1) * 2 * D_RG]
        r = _sigmoid(g[:, :D_RG])
        i = _sigmoid(g[:, D_RG:])
        log_a = (-RG_C) * r * _softplus(-lam_ref[d:d + 1, :])
        a = jnp.exp(log_a)
        a_s[...] = a
        one_minus_a2 = jnp.tanh(-log_a) * (a * a + 1.0)
        root = jnp.where(one_minus_a2 > 0.0, one_minus_a2 * lax.rsqrt(one_minus_a2), 0.0)
        b_s[...] = root * (i * xc)

    def scan_rows(h, reverse):
        def row(t, h):
            tt = (ch - 1 - t) if reverse else t
            h = a_s[pl.ds(tt, 1), :] * h + b_s[pl.ds(tt, 1), :]
            h_s[pl.ds(tt, 1), :] = h
            return h
        return lax.fori_loop(0, ch, row, h, unroll=8)

    def fwd_body(i, h):
        c0 = pl.multiple_of(i * ch, ch)
        gates(c0, 0)
        h = scan_rows(h, False)
        hf_s[pl.ds(c0, ch), :] = h_s[...]
        return h

    h_fwd = lax.fori_loop(0, n_ch, fwd_body, h0_ref[0:1, :])

    def bwd_body(i, h):
        c0 = pl.multiple_of((n_ch - 1 - i) * ch, ch)
        gates(c0, 1)
        h = scan_rows(h, True)
        u_gate = z_ref[pl.ds(c0, ch), rg_hi:rg_hi + D_RG].astype(F32)
        y = (hf_s[pl.ds(c0, ch), :] + h_s[...]) * _gelu_tanh(u_gate)
        y_ref[pl.ds(c0, ch), D_POOL:D_POOL + D_RG] = y.astype(y_ref.dtype)
        return h

    h_bwd = lax.fori_loop(0, n_ch, bwd_body, h0_ref[1:2, :])
    hfin_ref[0:1, :] = h_fwd
    hfin_ref[1:2, :] = h_bwd


def _even_mixer(z, h0, p, n_batch, t_len, grid):
    assert z.dtype == BF16
    seg = GRID_W if grid else t_len
    const2 = lambda b: (0, 0)
    const3 = lambda b: (0, 0, 0)
    idx = jnp.arange(seg)
    delta = idx[None, :] - idx[:, None]
    band = jnp.stack([jnp.logical_and(delta >= -(w // 2), delta <= w - 1 - w // 2) for w in POOL_WINDOWS])
    band = jnp.tile(band.astype(BF16), (1, 1, 3 if grid else 1))
    return pl.pallas_call(
        functools.partial(_even_kernel, t_len=t_len, grid=grid),
        grid=(n_batch,),
        in_specs=[
            pl.BlockSpec((t_len, EV_COLS), lambda b: (b, 0)),
            pl.BlockSpec((None, 2, D_RG), lambda b: (b, 0, 0)),
            pl.BlockSpec(p['pool_w'].shape, const3),
            pl.BlockSpec((1, D_POOL), const2),
            pl.BlockSpec((CONV_W, D_RG), const2),
            pl.BlockSpec((1, D_RG), const2),
            pl.BlockSpec((D_RG, 4 * D_RG), const2),
            pl.BlockSpec((1, 4 * D_RG), const2),
            pl.BlockSpec((2, D_RG), const2),
            pl.BlockSpec(band.shape, const3),
        ],
        out_specs=[
            pl.BlockSpec((t_len, D_POOL + D_RG), lambda b: (b, 0)),
            pl.BlockSpec((None, 2, D_RG), lambda b: (b, 0, 0)),
        ],
        out_shape=[
            jax.ShapeDtypeStruct((n_batch * t_len, D_POOL + D_RG), Y_DTYPE),
            jax.ShapeDtypeStruct((n_batch, 2, D_RG), F32),
        ],
        scratch_shapes=[
            pltpu.VMEM((RG_CHUNK + 2 * HALO, D_RG), F32),
            pltpu.VMEM((RG_CHUNK, D_RG), F32),
            pltpu.VMEM((RG_CHUNK, D_RG), F32),
            pltpu.VMEM((RG_CHUNK, D_RG), F32),
            pltpu.VMEM((t_len, D_RG), F32),
        ],
        compiler_params=_ARB1,
    )(z, h0, p['pool_w'], p['pool_scale'], p['conv_w'], p['conv_b'], p['wg'], p['bg'], p['lam'], band)


def _rwkv_kernel(*refs, t_len, zero_init):
    L = SCAN_CHUNK
    n_ch = t_len // L
    L2 = 2 * L
    z_ref = refs[0]
    (mu_ref, kkw_ref, ka_ref, rk_ref, g2_ref, w0_ref, w2_ref, a0_ref, a2_ref, lnw_ref, lnb_ref, tri_ref, bd_ref,
     y_ref, sfin_ref, ebuf, s_s, yd_s, bon_s, g_s) = refs[1 if zero_init else 2:]
    if zero_init:
        s_s[...] = jnp.zeros(s_s.shape, F32)
    else:
        s0_ref = refs[1]
        zero = jnp.zeros((RW_HEAD, RW_HEAD), F32)
        for d in range(2):
            for p in range(RW_PAIRS):
                top = jnp.concatenate([s0_ref[d, 2 * p], zero], axis=1)
                bot = jnp.concatenate([zero, s0_ref[d, 2 * p + 1]], axis=1)
                s_s[d, p] = jnp.concatenate([top, bot], axis=0)

    def stack(x):
        first_head = lax.broadcasted_iota(jnp.int32, (L, LANES), 1) < RW_HEAD
        return jnp.concatenate([jnp.where(first_head, x, 0.0), jnp.where(first_head, 0.0, x)],
                               axis=0).astype(BF16)

    def prep(items):
        ones_bd = bd_ref[...]
        o = 3 * D_RW
        sts = []
        for n, (c0, d) in enumerate(items):
            ebuf[n] = _load_ext(z_ref, c0, L, t_len, 0, RW_ZCOLS)
            cur = ebuf[n, HALO:HALO + L, :]
            prev = ebuf[n, HALO - 1:HALO - 1 + L, :]
            nxt = ebuf[n, HALO + 1:HALO + 1 + L, :]
            zs = cur + mu_ref[0:1, :] * (prev - cur) + mu_ref[1:2, :] * (nxt - cur)
            sts.append(dict(c0=c0, d=d, r=zs[:, 0:D_RW], k=zs[:, D_RW:2 * D_RW], v=zs[:, 2 * D_RW:3 * D_RW],
                            gd=zs[:, o:o + RW_G_LORA], wd=zs[:, o + RW_G_LORA:o + RW_G_LORA + LANES],
                            ad=zs[:, o + RW_G_LORA + LANES:o + RW_G_LORA + 2 * LANES]))
        for st in sts:
            d = st['d']
            st['kkv'] = st['k'] * kkw_ref[...]
            st['ss'] = _head_sum(st['kkv'] * st['kkv'], ones_bd)
            tw_hi, tw_lo = _split2(jnp.tanh(st['wd']))
            st['w_lora'] = _dot(tw_hi, w2_ref[d, 0]) + (_dot(tw_hi, w2_ref[d, 1]) + _dot(tw_lo, w2_ref[d, 0]))
            st['a_lora'] = _dot(st['ad'].astype(BF16), a2_ref[d])
            if d == 0:
                g = _dot(_sigmoid(st['gd']).astype(BF16), g2_ref[...])
                g_s[pl.ds(st['c0'], L), :] = g.astype(g_s.dtype)
        for st in sts:
            d = st['d']
            st['kkn'] = st['kkv'] / jnp.maximum(jnp.sqrt(st['ss']), 1e-12)
            w_log = -_softplus(-(w0_ref[d:d + 1, :] + st['w_lora'])) - 0.5
            st['lw'] = -jnp.exp(w_log)
            st['a'] = _sigmoid(a0_ref[d:d + 1, :] + st['a_lora'])
            st['kd'] = st['k'] * (1.0 + (st['a'] - 1.0) * ka_ref[...])
            st['bon'] = _head_sum(st['r'] * st['kd'] * rk_ref[...], ones_bd, single_pass=True)
            st['cum'] = _dot_exact_lhs(tri_ref[d], st['lw'])
        out = []
        for st in sts:
            d, r, v, kd, kkn, a, cum = st['d'], st['r'], st['v'], st['kd'], st['kkn'], st['a'], st['cum']
            bon_s[d, pl.ds(st['c0'], L), :] = (st['bon'] * v).astype(bon_s.dtype)
            e_pos = jnp.exp(cum)
            e_neg = jnp.exp(-cum)
            rh = r * e_pos
            kh = kd * e_neg
            bh = (kkn * a) * e_neg
            ah = -kkn * jnp.exp(cum - st['lw'])
            last = L - 1 if d == 0 else 0
            gam = e_pos[last:last + 1, :]
            bhg = bh * gam
            khg = kh * gam
            chains = []
            for p in range(RW_PAIRS):
                cols = slice(p * LANES, (p + 1) * LANES)
                chains.append(dict(
                    d=d, p=p, gam=gam[:, cols], v_st=stack(v[:, cols]),
                    lhs=jnp.concatenate([stack(ah[:, cols]), stack(rh[:, cols])], axis=0),
                    rhs=jnp.concatenate([stack(bh[:, cols]), stack(kh[:, cols])], axis=0),
                    bk=jnp.concatenate([stack(bhg[:, cols]), stack(khg[:, cols])], axis=0)))
            out.append(chains)
        return out

    n_levels = int(math.log2(L))

    def state_free(chains):
        ri = lax.broadcasted_iota(jnp.int32, (L2, L2), 0)
        ci = lax.broadcasted_iota(jnp.int32, (L2, L2), 1)
        eye = jnp.where(ri == ci, 1.0, 0.0)
        rl = jnp.where(ri < L, ri, ri - L)
        cl = jnp.where(ci < L, ci, ci - L)
        strict = (cl < rl, cl > rl)
        incl = (cl <= rl, cl >= rl)
        level = ri ^ ci

        for ch in chains:
            ch['sc'] = _dot_nt(ch['lhs'], ch['rhs'])
        for ch in chains:
            sc, d = ch['sc'], ch['d']
            n_ab = jnp.where(strict[d], sc[0:L2, 0:L2], 0.0)
            m_ak = jnp.where(strict[d], sc[0:L2, L2:2 * L2], 0.0).astype(BF16)
            t_rb = jnp.where(incl[d], sc[L2:2 * L2, 0:L2], 0.0)
            t_rk = jnp.where(incl[d], sc[L2:2 * L2, L2:2 * L2], 0.0)
            ch['t_rbk'] = jnp.concatenate([t_rb, t_rk], axis=1).astype(BF16)
            ch['mv'] = _dot(m_ak, ch['v_st'])
            ch['t'] = eye + jnp.where(level == 1, n_ab, 0.0)
            ch['n_lv'] = [jnp.where(lax.shift_right_logical(level, lv) == 1, n_ab, 0.0).astype(BF16)
                          for lv in range(1, n_levels)]
            del ch['sc']
        for lv in range(n_levels - 1):
            for ch in chains:
                ch['tb'] = ch['t'].astype(BF16)
                ch['x'] = _dot(ch['n_lv'][lv], ch['tb']).astype(BF16)
            for ch in chains:
                ch['t'] = ch['t'] + _dot(ch['tb'], ch['x'])

    def state_step(chains, c0s):
        for ch in chains:
            ch['s_prev'] = s_s[ch['d'], ch['p']]
            ch['ar_s'] = _dot_nt(ch['lhs'], ch['s_prev'].astype(BF16))
        for ch in chains:
            u_rhs = ch['ar_s'][0:L2] + ch['mv']
            u_st = _dot(ch['t'].astype(BF16), u_rhs.astype(BF16)).astype(BF16)
            ch['uv'] = jnp.concatenate([u_st, ch['v_st']], axis=0)
        for ch in chains:
            y_st = ch['ar_s'][L2:2 * L2] + _dot(ch['t_rbk'], ch['uv'])
            ch['y'] = y_st[0:L] + y_st[L:L2]
            s_s[ch['d'], ch['p']] = ch['s_prev'] * ch['gam'] + _dot_tn(ch['uv'], ch['bk'])
        for d in range(2):
            yd_s[d, pl.ds(c0s[d], L), :] = jnp.concatenate([ch['y'] for ch in chains if ch['d'] == d], axis=1)

    steps = 2

    def body(i, carry):
        starts = []
        for j in range(steps):
            step = i * steps + j
            starts.append((pl.multiple_of(step * L, L), pl.multiple_of((n_ch - 1 - step) * L, L)))
        prepared = prep([(c0s[d], d) for c0s in starts for d in range(2)])
        groups = [(prepared[2 * j] + prepared[2 * j + 1], starts[j]) for j in range(steps)]
        state_free([ch for chains, _ in groups for ch in chains])
        for chains, c0s in groups:
            state_step(chains, c0s)
        return carry

    lax.fori_loop(0, n_ch // steps, body, 0)

    def epilogue(i, carry):
        c0 = pl.multiple_of(i * L, L)
        ones_bd = bd_ref[...]
        ys = yd_s[0, pl.ds(c0, L), :] + yd_s[1, pl.ds(c0, L), :]
        mu = _head_sum(ys, ones_bd) * (1.0 / RW_HEAD)
        yc = ys - mu
        var = _head_sum(yc * yc, ones_bd) * (1.0 / RW_HEAD)
        yn = yc * lax.rsqrt(var + RW_LN_EPS) * lnw_ref[...] + lnb_ref[...]
        bonus = bon_s[0, pl.ds(c0, L), :].astype(F32) + bon_s[1, pl.ds(c0, L), :].astype(F32)
        out = (yn + bonus) * g_s[pl.ds(c0, L), :].astype(F32)
        y_ref[pl.ds(c0, L), :] = out.astype(y_ref.dtype)
        return carry

    lax.fori_loop(0, n_ch, epilogue, 0, unroll=min(EPILOGUE_UNROLL, n_ch))
    for d in range(2):
        for p in range(RW_PAIRS):
            s_pair = s_s[d, p]
            sfin_ref[d, 2 * p] = s_pair[0:RW_HEAD, 0:RW_HEAD]
            sfin_ref[d, 2 * p + 1] = s_pair[RW_HEAD:, RW_HEAD:]


def _rwkv_mixer(z, s0, p, n_batch, t_len):
    L = SCAN_CHUNK
    c2 = lambda b: (0, 0)
    c3 = lambda b: (0, 0, 0)
    st_spec = pl.BlockSpec((None, 2, RW_HEADS, RW_HEAD, RW_HEAD), lambda b: (b, 0, 0, 0, 0))
    zero_init = s0 is None
    return pl.pallas_call(
        functools.partial(_rwkv_kernel, t_len=t_len, zero_init=zero_init),
        grid=(n_batch,),
        in_specs=[pl.BlockSpec((t_len, RW_ZCOLS), lambda b: (b, 0))] + ([] if zero_init else [st_spec]) + [
            pl.BlockSpec((2, RW_ZCOLS), c2),
            pl.BlockSpec((1, D_RW), c2),
            pl.BlockSpec((1, D_RW), c2),
            pl.BlockSpec((1, D_RW), c2),
            pl.BlockSpec((RW_G_LORA, D_RW), c2),
            pl.BlockSpec((2, D_RW), c2),
            pl.BlockSpec((2, 2, LANES, D_RW), lambda b: (0, 0, 0, 0)),
            pl.BlockSpec((2, D_RW), c2),
            pl.BlockSpec((2, LANES, D_RW), c3),
            pl.BlockSpec((1, D_RW), c2),
            pl.BlockSpec((1, D_RW), c2),
            pl.BlockSpec((2, L, L), c3),
            pl.BlockSpec((LANES, LANES), c2),
        ],
        out_specs=[pl.BlockSpec((t_len, D_RW), lambda b: (b, 0)), st_spec],
        out_shape=[
            jax.ShapeDtypeStruct((n_batch * t_len, D_RW), Y_DTYPE),
            jax.ShapeDtypeStruct((n_batch, 2, RW_HEADS, RW_HEAD, RW_HEAD), F32),
        ],
        scratch_shapes=[
            pltpu.VMEM((4, L + 2 * HALO, RW_ZCOLS), F32),
            pltpu.VMEM((2, RW_PAIRS, LANES, LANES), F32),
            pltpu.VMEM((2, t_len, D_RW), F32),
            pltpu.VMEM((2, t_len, D_RW), BF16),
            pltpu.VMEM((t_len, D_RW), BF16),
        ],
        compiler_params=_ARB1,
    )(z, *(() if zero_init else (s0,)), p['mu'], p['kk'], p['ka'], p['rk'], p['g2'], p['w0'], p['w2'], p['a0'],
      p['a2'], p['ln_w'], p['ln_b'], p['tri'], p['ones_bd'])


def _mlstm_kernel(*refs, t_len, zero_init):
    L = ML_SCAN
    n_ch = t_len // L
    z_ref = refs[0]
    (cw_ref, cb_ref, gb_ref, nw_ref, tri_ref, esel_ref, shift_ref, ones_ref,
     y_ref, cfin_ref, nfin_ref, mfin_ref, c_s, n_s, m_s, hd_s, qk_s, gx_s, gr_s) = refs[1 if zero_init else 4:]
    if zero_init:
        c_s[...] = jnp.zeros(c_s.shape, F32)
        n_s[...] = jnp.zeros(n_s.shape, F32)
        m_s[...] = jnp.zeros(m_s.shape, F32)
    else:
        c0_ref, n0_ref, m0_ref = refs[1:4]
        c_s[...] = c0_ref[...]
        m_s[...] = m0_ref[...]
        for d in range(2):
            for h in range(ML_HEADS):
                n_s[d, h] = jnp.broadcast_to(n0_ref[d, h:h + 1, :], (ML_DK, ML_DK)).T
    q_scale = ML_DK ** -0.5
    g_lo = 4 * D_ML
    n_real = 4 * ML_HEADS

    def conv_chunk(i, carry):
        c0 = pl.multiple_of(i * L, L)
        gts = z_ref[pl.ds(c0, L), g_lo:g_lo + LANES].astype(F32) + gb_ref[...]
        x16 = gts.T[0:n_real, :]
        lf_hi, lf_lo = _split2(-_softplus(-x16))
        lane = lax.broadcasted_iota(jnp.int32, (8, L), 1)
        stats = []
        for d in range(2):
            b16 = _dot_nt(lf_hi, tri_ref[d]) + _dot_nt(lf_lo, tri_ref[d])
            x8 = x16[8 * d:8 * d + 8, :]
            b8 = b16[8 * d:8 * d + 8, :]
            g8 = pltpu.roll(x8, ML_HEADS, axis=0) - b8
            stats.append([g8, b8, g8])
        step = 1
        while step < L:
            for d in range(2):
                cm8 = stats[d][2]
                if d == 0:
                    sh = jnp.where(lane >= step, pltpu.roll(cm8, step, axis=1), -jnp.inf)
                else:
                    sh = jnp.where(lane < L - step, pltpu.roll(cm8, L - step, axis=1), -jnp.inf)
                stats[d][2] = jnp.maximum(cm8, sh)
            step *= 2

        ext = _load_ext(z_ref, c0, L, t_len, 0, 2 * D_ML, raw=True)
        taps = _dot(shift_ref[...], ext)
        qk = cb_ref[...] + ext[HALO:HALO + L].astype(F32) * cw_ref[CONV_LEFT:CONV_LEFT + 1, :]
        for n, j in enumerate(jj for jj in range(CONV_W) if jj != CONV_LEFT):
            qk = qk + taps[n * L:(n + 1) * L] * cw_ref[j:j + 1, :]
        qk = _silu(qk)
        qk_s[pl.ds(c0, L), 0:D_ML] = (qk[:, :D_ML] * q_scale).astype(BF16)
        qk_s[pl.ds(c0, L), D_ML:2 * D_ML] = qk[:, D_ML:].astype(BF16)

        for d, (g8, b8, cm8) in enumerate(stats):
            x32 = jnp.concatenate([g8, b8, cm8, jnp.zeros((8, L), F32)], axis=0)
            hi = x32.astype(BF16)
            r1 = x32 - hi.astype(F32)
            mid = r1.astype(BF16)
            lo = (r1 - mid.astype(F32)).astype(BF16)
            gx_s[d, i] = jnp.concatenate([hi, mid, lo], axis=0)
            gr_s[d, i] = g8
        return carry

    lax.fori_loop(0, n_ch, conv_chunk, 0, unroll=min(ML_UNROLL, n_ch))

    def prep(ci, d):
        c0 = pl.multiple_of(ci * L, L)
        q = qk_s[pl.ds(c0, L), 0:D_ML]
        k = qk_s[pl.ds(c0, L), D_ML:2 * D_ML]
        v = z_ref[pl.ds(c0, L), 2 * D_ML:3 * D_ML]
        g8 = gr_s[d, ci]
        cb = _dot_tn(gx_s[d, ci], esel_ref[...])
        ri = lax.broadcasted_iota(jnp.int32, (L, L), 0)
        ci = lax.broadcasted_iota(jnp.int32, (L, L), 1)
        causal = (ci <= ri) if d == 0 else (ci >= ri)
        last = L - 1 if d == 0 else 0
        ones = ones_ref[...]
        chains = []
        for h in range(ML_HEADS):
            cols = slice(h * ML_DK, (h + 1) * ML_DK)
            chains.append(dict(
                d=d, h=h, last=last, causal=causal,
                g_row=g8[ML_HEADS + h:ML_HEADS + h + 1, :],
                g_b=cb[:, cols], b_b=cb[:, D_ML + h * ML_DK:D_ML + (h + 1) * ML_DK],
                cm_b=cb[:, 2 * D_ML + h * ML_DK:2 * D_ML + (h + 1) * ML_DK],
                qb=q[:, cols], kb=k[:, cols],
                v=v[:, cols].astype(F32), v_ones=jnp.concatenate([v[:, cols].astype(BF16), ones], axis=1)))
        return chains

    def body(i, carry):
        c0s = (pl.multiple_of(i * L, L), pl.multiple_of((n_ch - 1 - i) * L, L))
        chains = prep(i, 0) + prep(n_ch - 1 - i, 1)
        for ch in chains:
            d, h = ch['d'], ch['h']
            ch['qk'] = _dot_nt(ch['qb'], ch['kb'])
            ch['c_prev'] = c_s[d, h]
            ch['n_prev'] = n_s[d, h]
            ch['m_prev'] = m_s[d, h:h + 1, :]
            ch['qcn'] = _dot(ch['qb'], jnp.concatenate([ch['c_prev'].astype(BF16),
                                                        ch['n_prev'].astype(BF16)], axis=1))
        for ch in chains:
            mu = jnp.maximum(ch['m_prev'], ch['cm_b'])
            ch['mu'] = mu
            ch['s'] = ch['qk'] * jnp.exp(jnp.where(ch['causal'], ch['g_row'] - mu, -jnp.inf))
            ch['w_inter'] = jnp.exp(ch['m_prev'] - mu)
        for ch in chains:
            last = ch['last']
            ch['sv'] = _dot(ch['s'].astype(BF16), ch['v_ones'])
            mu_last = ch['mu'][last:last + 1, :]
            ch['m_new'] = ch['b_b'][last:last + 1, :] + mu_last
            ch['dec'] = jnp.exp(ch['m_prev'] - mu_last)
            w_s = jnp.exp(ch['g_b'] - mu_last)
            ch['upd'] = _dot_tn(ch['kb'], jnp.concatenate([(w_s * ch['v']).astype(BF16),
                                                           w_s.astype(BF16)], axis=1))
        for ch in chains:
            d, h = ch['d'], ch['h']
            num = ch['sv'][:, :ML_DK] + ch['w_inter'] * ch['qcn'][:, :ML_DK]
            den = ch['sv'][:, ML_DK:] + ch['w_inter'] * ch['qcn'][:, ML_DK:]
            ch['out'] = num / jnp.maximum(jnp.abs(den), jnp.exp(-(ch['b_b'] + ch['mu'])))
            c_s[d, h] = ch['dec'] * ch['c_prev'] + ch['upd'][:, :ML_DK]
            n_s[d, h] = ch['dec'] * ch['n_prev'] + ch['upd'][:, ML_DK:]
            m_s[d, h:h + 1, :] = ch['m_new']
        for d in range(2):
            hd_s[d, pl.ds(c0s[d], L), :] = jnp.concatenate([ch['out'] for ch in chains if ch['d'] == d], axis=1)
        return carry

    lax.fori_loop(0, n_ch, body, 0, unroll=min(ML_UNROLL, n_ch))

    def epilogue(i, carry):
        c0 = pl.multiple_of(i * L, L)
        ones = ones_ref[...]
        hs = hd_s[0, pl.ds(c0, L), :] + hd_s[1, pl.ds(c0, L), :]
        o = z_ref[pl.ds(c0, L), 3 * D_ML:4 * D_ML].astype(F32)
        mu = _head_sum(hs, ones) * (1.0 / ML_DK)
        xc = hs - mu
        var = _head_sum(xc * xc, ones) * (1.0 / ML_DK)
        y = xc * lax.rsqrt(var + NORM_EPS) * nw_ref[...] * _sigmoid(o)
        y_ref[pl.ds(c0, L), :] = y.astype(y_ref.dtype)
        return carry

    lax.fori_loop(0, n_ch, epilogue, 0, unroll=min(EPILOGUE_UNROLL, n_ch))
    cfin_ref[...] = c_s[...]
    mfin_ref[...] = m_s[...]
    for d in range(2):
        for h in range(ML_HEADS):
            nfin_ref[d, h:h + 1, :] = n_s[d, h].T[0:1, :]


def _mlstm_mixer(z, states, p, n_batch, t_len):
    assert z.dtype == BF16
    L = ML_SCAN
    c2 = lambda b: (0, 0)
    c3 = lambda b: (0, 0, 0)
    c_spec = pl.BlockSpec((None, 2, ML_HEADS, ML_DK, ML_DK), lambda b: (b, 0, 0, 0, 0))
    m_spec = pl.BlockSpec((None, 2, ML_HEADS, ML_DK), lambda b: (b, 0, 0, 0))
    zero_init = states is None
    state_specs = [] if zero_init else [c_spec, m_spec, m_spec]
    return pl.pallas_call(
        functools.partial(_mlstm_kernel, t_len=t_len, zero_init=zero_init),
        grid=(n_batch,),
        in_specs=[pl.BlockSpec((t_len, ML_ZCOLS), lambda b: (b, 0))] + state_specs + [
            pl.BlockSpec((CONV_W, 2 * D_ML), c2),
            pl.BlockSpec((1, 2 * D_ML), c2),
            pl.BlockSpec((1, LANES), c2),
            pl.BlockSpec((1, D_ML), c2),
            pl.BlockSpec((2, L, L), c3),
            pl.BlockSpec(p['esel'].shape, c2),
            pl.BlockSpec(p['shift'].shape, c2),
            pl.BlockSpec((L, LANES), c2),
        ],
        out_specs=[pl.BlockSpec((t_len, D_ML), lambda b: (b, 0)), c_spec, m_spec, m_spec],
        out_shape=[
            jax.ShapeDtypeStruct((n_batch * t_len, D_ML), Y_DTYPE),
            jax.ShapeDtypeStruct((n_batch, 2, ML_HEADS, ML_DK, ML_DK), F32),
            jax.ShapeDtypeStruct((n_batch, 2, ML_HEADS, ML_DK), F32),
            jax.ShapeDtypeStruct((n_batch, 2, ML_HEADS, ML_DK), F32),
        ],
        scratch_shapes=[
            pltpu.VMEM((2, ML_HEADS, ML_DK, ML_DK), F32),
            pltpu.VMEM((2, ML_HEADS, ML_DK, ML_DK), F32),
            pltpu.VMEM((2, ML_HEADS, ML_DK), F32),
            pltpu.VMEM((2, t_len, D_ML), F32),
            pltpu.VMEM((t_len, 2 * D_ML), BF16),
            pltpu.VMEM((2, t_len // L, 96, L), BF16),
            pltpu.VMEM((2, t_len // L, 8, L), F32),
        ],
        compiler_params=_ARB1,
    )(z, *(() if zero_init else states), p['conv_w'], p['conv_b'], p['gate_b'], p['norm_w'], p['tri'], p['esel'],
      p['shift'], p['ones'])


def _conv_shift(n):
    taps = jnp.array([j for j in range(CONV_W) if j != CONV_LEFT])
    t = jnp.arange((CONV_W - 1) * n)
    src = HALO + t % n + taps[t // n] - CONV_LEFT
    return (src[:, None] == jnp.arange(n + 2 * HALO)[None, :]).astype(BF16)


def _mlstm_consts():
    L = ML_SCAN
    lower = jnp.tril(jnp.ones((L, L), F32))
    tri = jnp.stack([lower, lower.T]).astype(BF16)
    rows = jnp.arange(32)
    blk = jnp.arange(3 * D_ML) // ML_DK
    src_row = 8 * (blk // ML_HEADS) + ML_HEADS + blk % ML_HEADS
    esel = (rows[:, None] == src_row[None, :]).astype(BF16)
    esel = jnp.concatenate([esel, esel, esel], axis=0)
    return dict(tri=tri, esel=esel, shift=_conv_shift(L), ones=jnp.ones((L, LANES), BF16))


def _block_diag(w):
    n, c, d = w.shape
    return jnp.einsum('ncd,nm->ncmd', w, jnp.eye(n, dtype=w.dtype)).reshape(n * c, n * d)


def _scan_tri():
    L = SCAN_CHUNK
    lower = jnp.tril(jnp.ones((L, L), F32))
    return jnp.stack([lower, lower.T]).astype(BF16)


def _prep_even(ev_w_in, pool_w, pool_scale, rg_conv_w, rg_conv_b, rg_wa, rg_ba, rg_wx, rg_bx, rg_lam, ev_w_out, j):
    wg = jnp.concatenate([_block_diag(rg_wa[j, 0]), _block_diag(rg_wx[j, 0]),
                          _block_diag(rg_wa[j, 1]), _block_diag(rg_wx[j, 1])], axis=1)
    bg = jnp.concatenate([rg_ba[j, 0], rg_bx[j, 0], rg_ba[j, 1], rg_bx[j, 1]]).reshape(1, 4 * D_RG)
    return dict(
        w_in=[ev_w_in[j].astype(BF16)],
        w_out=ev_w_out[j].astype(BF16),
        pool_w=pool_w[j].astype(BF16),
        pool_scale=pool_scale[j].reshape(1, D_POOL),
        conv_w=rg_conv_w[j],
        conv_b=rg_conv_b[j].reshape(1, D_RG),
        wg=wg.astype(BF16),
        bg=bg,
        lam=rg_lam[j],
    )


def _pad_cols(w, cuts, widths):
    pieces = jnp.split(w, cuts, axis=-1)
    out = []
    for piece, width in zip(pieces, widths):
        extra = width - piece.shape[-1]
        if extra:
            piece = jnp.pad(piece, [(0, 0)] * (piece.ndim - 1) + [(0, extra)])
        out.append(piece)
    return jnp.concatenate(out, axis=-1)


def _prep_odd(od_w_in, rw_mu, rw_w0, rw_w2, rw_a0, rw_a2, rw_kk, rw_ka, rw_rk, rw_g2, rw_ln_w, rw_ln_b,
              ml_conv_w, ml_conv_b, ml_bi, ml_bf, ml_norm_w, od_w_out, j):
    rw_cols = 3 * D_RW + RW_G_LORA + RW_W_LORA + RW_A_LORA
    base = 3 * D_RW + RW_G_LORA
    cuts = [base, base + RW_W_LORA]
    widths = [base, LANES, LANES]
    w_rw = _pad_cols(od_w_in[j][:, :rw_cols], cuts, widths)
    w_ml = _pad_cols(od_w_in[j][:, rw_cols:], [4 * D_ML], [4 * D_ML, LANES])
    pad_rows = lambda w: jnp.pad(w, ((0, 0), (0, LANES - w.shape[1]), (0, 0)))
    gate_b = jnp.concatenate([ml_bi[j, 0], ml_bf[j, 0], ml_bi[j, 1], ml_bf[j, 1]])
    gate_b = jnp.pad(gate_b, (0, LANES - gate_b.shape[0])).reshape(1, LANES)
    hm = jnp.arange(LANES) // RW_HEAD
    rw = dict(
        mu=_pad_cols(rw_mu[j], cuts, widths),
        kk=rw_kk[j].reshape(1, D_RW), ka=rw_ka[j].reshape(1, D_RW), rk=rw_rk[j].reshape(1, D_RW),
        g2=rw_g2[j].astype(BF16),
        w0=rw_w0[j], w2=jnp.stack(_split2(pad_rows(rw_w2[j])), axis=1),
        a0=rw_a0[j], a2=pad_rows(rw_a2[j]).astype(BF16),
        ln_w=rw_ln_w[j].reshape(1, D_RW), ln_b=rw_ln_b[j].reshape(1, D_RW),
        tri=_scan_tri(),
        ones_bd=(hm[:, None] == hm[None, :]).astype(BF16),
    )
    ml = dict(
        conv_w=ml_conv_w[j], conv_b=ml_conv_b[j].reshape(1, 2 * D_ML),
        gate_b=gate_b, norm_w=ml_norm_w[j].reshape(1, D_ML), **_mlstm_consts(),
    )
    return dict(w_in=[w_rw.astype(BF16), w_ml.astype(BF16)], w_out=od_w_out[j].astype(BF16), rw=rw, ml=ml)


def _trunk(x, mod, rows_per_mod, mod_base, grid, states, params, n_batch, t_len):
    ev, od, dense = params
    rg0, rw0, ml0 = states
    x = x.reshape(n_batch * t_len, D_MODEL)
    spec = (rows_per_mod, mod_base)

    (z,) = _in_proj(x, mod[0], *spec, dense['norm1_w'][0], ev['w_in'])
    y_ev, rg_f = _even_mixer(z, rg0, ev, n_batch, t_len, grid)
    x = _out_mlp([y_ev], x, mod[0], *spec, dense['norm2_w'][0], ev['w_out'],
                 dense['mlp_w1'], dense['mlp_w2'], 0, dense['final_norm_w'], False)

    z_rw, z_ml = _in_proj(x, mod[1], *spec, dense['norm1_w'][1], od['w_in'])
    y_rw, s_f = _rwkv_mixer(z_rw, rw0, od['rw'], n_batch, t_len)
    y_ml, c_f, n_f, m_f = _mlstm_mixer(z_ml, ml0, od['ml'], n_batch, t_len)
    y = _out_mlp([y_rw, y_ml], x, mod[1], *spec, dense['norm2_w'][1], od['w_out'],
                 dense['mlp_w1'], dense['mlp_w2'], 1, dense['final_norm_w'], True)
    return y.reshape(n_batch, t_len, D_MODEL), (rg_f, s_f, c_f, n_f, m_f)


def kernel(x_prompt, x_sample, state_rglru, state_rwkv, state_mlstm_C, state_mlstm_n, state_mlstm_m,
           c, c_ctx, norm1_w, norm2_w, w_mod, b_mod, mlp_w1, mlp_w2, final_norm_w,
           ev_w_in, pool_w, pool_scale, rg_conv_w, rg_conv_b, rg_wa, rg_ba, rg_wx, rg_bx, rg_lam, ev_w_out,
           od_w_in, rw_mu, rw_w0, rw_w2, rw_a0, rw_a2, rw_kk, rw_ka, rw_rk, rw_g2, rw_ln_w, rw_ln_b,
           ml_conv_w, ml_conv_b, ml_bi, ml_bf, ml_norm_w, od_w_out):
    bp, tp, _ = x_prompt.shape
    bs, ts, _ = x_sample.shape

    assert bs + 1 <= N_COND_ROWS and ts % ROW_BLOCK == 0 and (bp * tp) % ROW_BLOCK == 0
    cond = jnp.concatenate([c_ctx[None, :], c, jnp.zeros((N_COND_ROWS - 1 - bs, D_MODEL), F32)], axis=0)
    mod = _modulation(cond, w_mod, b_mod)

    ev = _prep_even(ev_w_in, pool_w, pool_scale, rg_conv_w, rg_conv_b, rg_wa, rg_ba, rg_wx, rg_bx, rg_lam,
                    ev_w_out, 0)
    od = _prep_odd(od_w_in, rw_mu, rw_w0, rw_w2, rw_a0, rw_a2, rw_kk, rw_ka, rw_rk, rw_g2, rw_ln_w, rw_ln_b,
                   ml_conv_w, ml_conv_b, ml_bi, ml_bf, ml_norm_w, od_w_out, 0)
    dense = dict(norm1_w=norm1_w, norm2_w=norm2_w, mlp_w1=mlp_w1.astype(BF16), mlp_w2=mlp_w2.astype(BF16),
                 final_norm_w=final_norm_w)
    params = (ev, od, dense)

    zero_states = (jnp.zeros((bp, 2, D_RG), F32), None, None)
    y_prompt, (rg_f, s_f, c_f, n_f, m_f) = _trunk(
        x_prompt, mod, bp * tp, 0, False, zero_states, params, bp, tp)

    sample_states = (
        state_rglru[:, 0],
        state_rwkv[:, 0],
        (state_mlstm_C[:, 0], state_mlstm_n[:, 0],
         jnp.broadcast_to(state_mlstm_m[:, 0][..., None], (bs, 2, ML_HEADS, ML_DK))),
    )
    y_sample, _ = _trunk(x_sample, mod, ts, 1, True, sample_states, params, bs, ts)

    return (y_prompt, y_sample,
            rg_f[:, None],
            s_f[:, None],
            c_f[:, None],
            n_f[:, None],
            m_f[:, None, :, :, 0])
```

```python
import functools
import math

import jax
import jax.numpy as jnp
from jax import lax
from jax.experimental import pallas as pl
from jax.experimental.pallas import tpu as pltpu

F32 = jnp.float32
BF16 = jnp.bfloat16

D_MODEL = 1024
DEPTH = 2
GRID_W = 64
D_FF = 4 * D_MODEL
NORM_EPS = 1e-6
N_MOD = 6

D_POOL = D_MODEL // 2
POOL_WINDOWS = (2, 4, 8, 16)
POOL_GW = D_POOL // len(POOL_WINDOWS)
D_RG = D_MODEL // 2
RG_BLOCKS = 8
RG_C = 8.0
CONV_W = 4
CONV_LEFT = 2
EV_COLS = D_POOL + 2 * D_RG

D_RW = D_MODEL // 2
RW_HEAD = 64
RW_HEADS = D_RW // RW_HEAD
RW_PAIRS = RW_HEADS // 2
RW_W_LORA = 64
RW_A_LORA = 64
RW_G_LORA = 128
RW_LN_EPS = 64e-5
RW_ZCOLS = 3 * D_RW + RW_G_LORA + 128 + 128
D_ML = D_MODEL // 2
ML_HEADS = 4
ML_DK = D_ML // ML_HEADS
ML_ZCOLS = 4 * D_ML + 128

LANES = 128
HALO = 16
Z_DTYPE = BF16
Y_DTYPE = BF16
ROW_BLOCK = 512
SCAN_CHUNK = 64
ML_SCAN = 128
RG_CHUNK = 256
POOL_SEGS_PER_TRIP = 8
ML_UNROLL = 4
EPILOGUE_UNROLL = 8
MOD_COL_BLOCK = 1536
N_COND_ROWS = 16
VMEM_LIMIT = 56 * 1024 * 1024

_ARB1 = pltpu.CompilerParams(dimension_semantics=("arbitrary",), vmem_limit_bytes=VMEM_LIMIT)
_ARB2 = pltpu.CompilerParams(dimension_semantics=("arbitrary", "arbitrary"), vmem_limit_bytes=VMEM_LIMIT)


def _dot(a, b):
    return jnp.dot(a, b, preferred_element_type=F32)


def _dot_nt(a, b):
    return lax.dot_general(a, b, (((1,), (1,)), ((), ())), preferred_element_type=F32)


def _dot_tn(a, b):
    return lax.dot_general(a, b, (((0,), (0,)), ((), ())), preferred_element_type=F32)


def _split2(x):
    hi = x.astype(BF16)
    lo = (x - hi.astype(F32)).astype(BF16)
    return hi, lo


def _dot_f32(a, b):
    ah, al = _split2(a)
    bh, bl = _split2(b)
    return _dot(ah, bh) + (_dot(ah, bl) + _dot(al, bh))


def _dot_exact_lhs(m_bf16, x):
    hi, lo = _split2(x)
    return _dot(m_bf16, hi) + _dot(m_bf16, lo)


def _dot_exact_rhs(x, m_bf16):
    hi, lo = _split2(x)
    return _dot(hi, m_bf16) + _dot(lo, m_bf16)


def _softplus(x):
    return jnp.maximum(x, 0.0) + jnp.log1p(jnp.exp(-jnp.abs(x)))


def _sigmoid(x):
    return jax.nn.sigmoid(x)


def _silu(x):
    return x * jax.nn.sigmoid(x)


def _gelu_tanh(x):
    c = math.sqrt(2.0 / math.pi)
    return x * (0.5 * (1.0 + jnp.tanh(c * (x + 0.044715 * (x * x * x)))))


def _rms_norm(x, w):
    ms = jnp.mean(x * x, axis=-1, keepdims=True)
    return x * lax.rsqrt(ms + NORM_EPS) * w


def _load_ext(z_ref, c0, n, t_len, lo, hi, raw=False):
    dt = z_ref.dtype if raw else F32
    cur = z_ref[pl.ds(c0, n), lo:hi].astype(dt)
    pb = pl.multiple_of(jnp.maximum(c0 - HALO, 0), HALO)
    pa = pl.multiple_of(jnp.minimum(c0 + n, t_len - HALO), HALO)
    before = z_ref[pl.ds(pb, HALO), lo:hi].astype(dt)
    after = z_ref[pl.ds(pa, HALO), lo:hi].astype(dt)
    before = jnp.where(c0 > 0, before, jnp.zeros_like(before))
    after = jnp.where(c0 + n < t_len, after, jnp.zeros_like(after))
    return jnp.concatenate([before, cur, after], axis=0)


def _head_sum(x, ones_bd, single_pass=False):
    n = x.shape[0]
    groups = x.shape[1] // LANES
    xs = jnp.concatenate([x[:, g * LANES:(g + 1) * LANES] for g in range(groups)], axis=0)
    s = _dot(xs.astype(BF16), ones_bd) if single_pass else _dot_exact_rhs(xs, ones_bd)
    return jnp.concatenate([s[g * n:(g + 1) * n] for g in range(groups)], axis=1)


def _mod_kernel(c_ref, w_ref, b_ref, o_ref):
    s = _silu(c_ref[...])
    o_ref[...] = _dot_f32(s, w_ref[...]) + b_ref[...]


def _modulation(cond, w_mod, b_mod):
    rows = cond.shape[0]
    tn = MOD_COL_BLOCK
    out = pl.pallas_call(
        _mod_kernel,
        grid=(DEPTH, N_MOD * D_MODEL // tn),
        in_specs=[
            pl.BlockSpec((rows, D_MODEL), lambda l, j: (0, 0)),
            pl.BlockSpec((None, D_MODEL, tn), lambda l, j: (l, 0, j)),
            pl.BlockSpec((None, 1, tn), lambda l, j: (l, 0, j)),
        ],
        out_specs=pl.BlockSpec((None, rows, tn), lambda l, j: (l, 0, j)),
        out_shape=jax.ShapeDtypeStruct((DEPTH, rows, N_MOD * D_MODEL), F32),
        compiler_params=_ARB2,
    )(cond, w_mod, b_mod.reshape(DEPTH, 1, N_MOD * D_MODEL))
    return out.reshape(DEPTH, rows, N_MOD, D_MODEL)


def _inproj_kernel(*refs, n_out, col_chunk):
    x_ref, mod_ref, nw_ref = refs[:3]
    w_refs = refs[3:3 + n_out]
    z_refs = refs[3 + n_out:]
    h = _rms_norm(x_ref[...], nw_ref[...])
    h = h * (1.0 + mod_ref[1:2, :]) + mod_ref[0:1, :]
    hb = h.astype(BF16)
    for w_ref, z_ref in zip(w_refs, z_refs):
        n = w_ref.shape[1]
        for c in range(0, n, col_chunk):
            e = min(c + col_chunk, n)
            z_ref[:, c:e] = _dot(hb, w_ref[:, c:e]).astype(z_ref.dtype)


def _in_proj(x, mod, rows_per_mod, mod_base, norm_w, weights):
    m = x.shape[0]
    tm = ROW_BLOCK
    n_out = len(weights)
    const = lambda i: (0, 0)
    in_specs = [
        pl.BlockSpec((tm, D_MODEL), lambda i: (i, 0)),
        pl.BlockSpec((None, N_MOD, D_MODEL), lambda i: (mod_base + (i * tm) // rows_per_mod, 0, 0)),
        pl.BlockSpec((1, D_MODEL), const),
    ] + [pl.BlockSpec(w.shape, const, pipeline_mode=pl.Buffered(1)) for w in weights]
    out_specs = [pl.BlockSpec((tm, w.shape[1]), lambda i: (i, 0)) for w in weights]
    out_shape = [jax.ShapeDtypeStruct((m, w.shape[1]), Z_DTYPE) for w in weights]
    return pl.pallas_call(
        functools.partial(_inproj_kernel, n_out=n_out, col_chunk=512),
        grid=(m // tm,),
        in_specs=in_specs,
        out_specs=out_specs,
        out_shape=out_shape,
        compiler_params=_ARB1,
    )(x, mod, norm_w.reshape(1, D_MODEL), *weights)


def _outmlp_kernel(*refs, n_y, final, ff_chunk):
    y_refs = refs[:n_y]
    x_ref, mod_ref, nw_ref, wo_ref, w1_ref, w2_ref, fw_ref, o_ref = refs[n_y:]
    y = None
    row = 0
    for y_ref in y_refs:
        k = y_ref.shape[1]
        part = _dot(y_ref[...], wo_ref[row:row + k, :])
        y = part if y is None else y + part
        row += k
    x = x_ref[...] + mod_ref[2:3, :] * y
    h = _rms_norm(x, nw_ref[...]) * (1.0 + mod_ref[4:5, :]) + mod_ref[3:4, :]
    hb = h.astype(BF16)
    acc = None
    for c in range(0, D_FF, ff_chunk):
        u = jnp.maximum(_dot(hb, w1_ref[:, c:c + ff_chunk]), 0.0)
        part = _dot((u * u).astype(BF16), w2_ref[c:c + ff_chunk, :])
        acc = part if acc is None else acc + part
    x = x + mod_ref[5:6, :] * acc
    if final:
        x = _rms_norm(x, fw_ref[...])
    o_ref[...] = x


def _out_mlp(ys, x, mod, rows_per_mod, mod_base, norm_w, w_out, w1, w2, layer, final_w, final):
    m = x.shape[0]
    tm = ROW_BLOCK
    const = lambda i: (0, 0)
    single = dict(pipeline_mode=pl.Buffered(1))
    pick = lambda i: (layer, 0, 0)
    in_specs = [pl.BlockSpec((tm, y.shape[1]), lambda i: (i, 0)) for y in ys] + [
        pl.BlockSpec((tm, D_MODEL), lambda i: (i, 0)),
        pl.BlockSpec((None, N_MOD, D_MODEL), lambda i: (mod_base + (i * tm) // rows_per_mod, 0, 0)),
        pl.BlockSpec((1, D_MODEL), const),
        pl.BlockSpec(w_out.shape, const, **single),
        pl.BlockSpec((None,) + w1.shape[1:], pick, **single),
        pl.BlockSpec((None,) + w2.shape[1:], pick, **single),
        pl.BlockSpec((1, D_MODEL), const),
    ]
    return pl.pallas_call(
        functools.partial(_outmlp_kernel, n_y=len(ys), final=final, ff_chunk=1024),
        grid=(m // tm,),
        in_specs=in_specs,
        out_specs=pl.BlockSpec((tm, D_MODEL), lambda i: (i, 0)),
        out_shape=jax.ShapeDtypeStruct((m, D_MODEL), F32),
        compiler_params=_ARB1,
    )(*ys, x, mod, norm_w.reshape(1, D_MODEL), w_out, w1, w2, final_w.reshape(1, D_MODEL))


def _even_kernel(z_ref, h0_ref, pw_ref, ps_ref, cw_ref, cb_ref, wg_ref, bg_ref, lam_ref, band_ref,
                 y_ref, hfin_ref, cbuf, a_s, b_s, h_s, hf_s, *, t_len, grid):
    seg = GRID_W if grid else t_len
    n_seg = t_len // seg
    n_groups = len(POOL_WINDOWS)

    pos = lax.broadcasted_iota(jnp.int32, (seg, POOL_GW), 0)
    offs = [(w // 2, w - 1 - w // 2) for w in POOL_WINDOWS]
    gcols = [slice(g * POOL_GW, (g + 1) * POOL_GW) for g in range(n_groups)]

    def grid_row(rr, cols):
        valid = jnp.logical_and(rr >= 0, rr < n_seg)
        src = pl.multiple_of(jnp.clip(rr, 0, n_seg - 1) * seg, seg)
        return jnp.where(valid, z_ref[pl.ds(src, seg), cols].astype(F32), 0.0)

    def pool_rows(rows, run):
        items = []
        for r in rows:
            base = r * seg if isinstance(r, int) else pl.multiple_of(r * seg, seg)
            new_run = []
            for g, (lo_off, hi_off) in enumerate(offs):
                cols = gcols[g]
                if grid:
                    s1 = run[g] + grid_row(r + hi_off, cols) - grid_row(r - lo_off - 1, cols)
                    new_run.append(s1)
                    cnt_r = jnp.minimum(r + hi_off, n_seg - 1) - jnp.maximum(r - lo_off, 0) + 1
                    m1 = s1 / cnt_r.astype(F32)
                    hi = m1.astype(BF16)
                    r1 = m1 - hi.astype(F32)
                    mid = r1.astype(BF16)
                    parts = jnp.concatenate([hi, mid, (r1 - mid.astype(F32)).astype(BF16)], axis=0)
                else:
                    parts = z_ref[pl.ds(base, seg), cols]
                items.append(dict(g=g, base=base, parts=parts))
            run = new_run
        for it in items:
            it['s2'] = _dot(band_ref[it['g']], it['parts'])
        for it in items:
            lo_off, hi_off = offs[it['g']]
            cnt_c = jnp.minimum(pos + hi_off, seg - 1) - jnp.maximum(pos - lo_off, 0) + 1
            xg = z_ref[pl.ds(it['base'], seg), gcols[it['g']]].astype(F32)
            it['d'] = (it['s2'] / cnt_c.astype(F32) - xg).astype(BF16)
        for it in items:
            it['y'] = _dot(it['d'], pw_ref[it['g']])
        for it in items:
            cols = gcols[it['g']]
            y_ref[pl.ds(it['base'], seg), cols] = (it['y'] * ps_ref[:, cols]).astype(y_ref.dtype)
        return tuple(run) if grid else None

    if grid:
        init = []
        for g, (lo_off, hi_off) in enumerate(offs):
            s1 = jnp.zeros((seg, POOL_GW), F32)
            for o in range(min(hi_off, n_seg)):
                s1 = s1 + z_ref[o * seg:(o + 1) * seg, gcols[g]].astype(F32)
            init.append(s1)
        per_trip = POOL_SEGS_PER_TRIP
        lax.fori_loop(0, n_seg // per_trip,
                      lambda i, run: pool_rows([i * per_trip + j for j in range(per_trip)], run), tuple(init))
    else:
        pool_rows([0], None)

    ch = RG_CHUNK
    n_ch = t_len // ch
    rg_lo, rg_hi = D_POOL, D_POOL + D_RG

    def gates(c0, d):
        cbuf[...] = _load_ext(z_ref, c0, ch, t_len, rg_lo, rg_hi)
        xc = cb_ref[...]
        for j in range(CONV_W):
            off = HALO - CONV_LEFT + j
            xc = xc + cbuf[off:off + ch, :] * cw_ref[j:j + 1, :]
        g = _dot(xc.astype(BF16), wg_ref[:, d * 2 * D_RG:(d + 1) * 2 * D_RG]) \
            + bg_ref[:, d * 2 * D_RG:(d + 1) * 2 * D_RG]
        r = _sigmoid(g[:, :D_RG])
        i = _sigmoid(g[:, D_RG:])
        log_a = (-RG_C) * r * _softplus(-lam_ref[d:d + 1, :])
        a = jnp.exp(log_a)
        a_s[...] = a
        one_minus_a2 = jnp.tanh(-log_a) * (a * a + 1.0)
        root = jnp.where(one_minus_a2 > 0.0, one_minus_a2 * lax.rsqrt(one_minus_a2), 0.0)
        b_s[...] = root * (i * xc)

    def scan_rows(h, reverse):
        def row(t, h):
            tt = (ch - 1 - t) if reverse else t
            h = a_s[pl.ds(tt, 1), :] * h + b_s[pl.ds(tt, 1), :]
            h_s[pl.ds(tt, 1), :] = h
            return h
        return lax.fori_loop(0, ch, row, h, unroll=8)

    def fwd_body(i, h):
        c0 = pl.multiple_of(i * ch, ch)
        gates(c0, 0)
        h = scan_rows(h, False)
        hf_s[pl.ds(c0, ch), :] = h_s[...]
        return h

    h_fwd = lax.fori_loop(0, n_ch, fwd_body, h0_ref[0:1, :])

    def bwd_body(i, h):
        c0 = pl.multiple_of((n_ch - 1 - i) * ch, ch)
        gates(c0, 1)
        h = scan_rows(h, True)
        u_gate = z_ref[pl.ds(c0, ch), rg_hi:rg_hi + D_RG].astype(F32)
        y = (hf_s[pl.ds(c0, ch), :] + h_s[...]) * _gelu_tanh(u_gate)
        y_ref[pl.ds(c0, ch), D_POOL:D_POOL + D_RG] = y.astype(y_ref.dtype)
        return h

    h_bwd = lax.fori_loop(0, n_ch, bwd_body, h0_ref[1:2, :])
    hfin_ref[0:1, :] = h_fwd
    hfin_ref[1:2, :] = h_bwd


def _even_mixer(z, h0, p, n_batch, t_len, grid):
    assert z.dtype == BF16
    seg = GRID_W if grid else t_len
    const2 = lambda b: (0, 0)
    const3 = lambda b: (0, 0, 0)
    idx = jnp.arange(seg)
    delta = idx[None, :] - idx[:, None]
    band = jnp.stack([jnp.logical_and(delta >= -(w // 2), delta <= w - 1 - w // 2) for w in POOL_WINDOWS])
    band = jnp.tile(band.astype(BF16), (1, 1, 3 if grid else 1))
    return pl.pallas_call(
        functools.partial(_even_kernel, t_len=t_len, grid=grid),
        grid=(n_batch,),
        in_specs=[
            pl.BlockSpec((t_len, EV_COLS), lambda b: (b, 0)),
            pl.BlockSpec((None, 2, D_RG), lambda b: (b, 0, 0)),
            pl.BlockSpec(p['pool_w'].shape, const3),
            pl.BlockSpec((1, D_POOL), const2),
            pl.BlockSpec((CONV_W, D_RG), const2),
            pl.BlockSpec((1, D_RG), const2),
            pl.BlockSpec((D_RG, 4 * D_RG), const2),
            pl.BlockSpec((1, 4 * D_RG), const2),
            pl.BlockSpec((2, D_RG), const2),
            pl.BlockSpec(band.shape, const3),
        ],
        out_specs=[
            pl.BlockSpec((t_len, D_POOL + D_RG), lambda b: (b, 0)),
            pl.BlockSpec((None, 2, D_RG), lambda b: (b, 0, 0)),
        ],
        out_shape=[
            jax.ShapeDtypeStruct((n_batch * t_len, D_POOL + D_RG), Y_DTYPE),
            jax.ShapeDtypeStruct((n_batch, 2, D_RG), F32),
        ],
        scratch_shapes=[
            pltpu.VMEM((RG_CHUNK + 2 * HALO, D_RG), F32),
            pltpu.VMEM((RG_CHUNK, D_RG), F32),
            pltpu.VMEM((RG_CHUNK, D_RG), F32),
            pltpu.VMEM((RG_CHUNK, D_RG), F32),
            pltpu.VMEM((t_len, D_RG), F32),
        ],
        compiler_params=_ARB1,
    )(z, h0, p['pool_w'], p['pool_scale'], p['conv_w'], p['conv_b'], p['wg'], p['bg'], p['lam'], band)


def _rwkv_kernel(*refs, t_len, zero_init):
    L = SCAN_CHUNK
    n_ch = t_len // L
    L2 = 2 * L
    z_ref = refs[0]
    (mu_ref, kkw_ref, ka_ref, rk_ref, g2_ref, w0_ref, w2_ref, a0_ref, a2_ref, lnw_ref, lnb_ref, tri_ref, bd_ref,
     y_ref, sfin_ref, ebuf, s_s, yd_s, bon_s, g_s, zs_s, kkn_s) = refs[1 if zero_init else 2:]
    if zero_init:
        s_s[...] = jnp.zeros(s_s.shape, F32)
    else:
        s0_ref = refs[1]
        zero = jnp.zeros((RW_HEAD, RW_HEAD), F32)
        for d in range(2):
            for p in range(RW_PAIRS):
                top = jnp.concatenate([s0_ref[d, 2 * p], zero], axis=1)
                bot = jnp.concatenate([zero, s0_ref[d, 2 * p + 1]], axis=1)
                s_s[d, p] = jnp.concatenate([top, bot], axis=0)

    def stack(x):
        first_head = lax.broadcasted_iota(jnp.int32, (L, LANES), 1) < RW_HEAD
        return jnp.concatenate([jnp.where(first_head, x, 0.0), jnp.where(first_head, 0.0, x)],
                               axis=0).astype(BF16)

    def shift_pair(i, carry):
        ones_bd = bd_ref[...]
        o = 3 * D_RW
        sts = []
        for n in range(2):
            c0 = pl.multiple_of((2 * i + n) * L, L)
            ebuf[n] = _load_ext(z_ref, c0, L, t_len, 0, RW_ZCOLS)
            cur = ebuf[n, HALO:HALO + L, :]
            prev = ebuf[n, HALO - 1:HALO - 1 + L, :]
            nxt = ebuf[n, HALO + 1:HALO + 1 + L, :]
            zs = cur + mu_ref[0:1, :] * (prev - cur) + mu_ref[1:2, :] * (nxt - cur)
            zs_s[pl.ds(c0, L), :] = zs.astype(BF16)
            kkv = zs[:, D_RW:2 * D_RW] * kkw_ref[...]
            sts.append(dict(c0=c0, kkv=kkv, ss=_head_sum(kkv * kkv, ones_bd),
                            g=_dot(_sigmoid(zs[:, o:o + RW_G_LORA]).astype(BF16), g2_ref[...])))
        for st in sts:
            rows = pl.ds(st['c0'], L)
            kkn_s[rows, :] = (st['kkv'] / jnp.maximum(jnp.sqrt(st['ss']), 1e-12)).astype(BF16)
            g_s[rows, :] = st['g'].astype(g_s.dtype)
        return carry

    lax.fori_loop(0, n_ch // 2, shift_pair, 0)

    def prep(items):
        ones_bd = bd_ref[...]
        o = 3 * D_RW
        sts = []
        for c0, d in items:
            rows = pl.ds(c0, L)
            sts.append(dict(c0=c0, d=d, r=zs_s[rows, 0:D_RW].astype(F32), k=zs_s[rows, D_RW:2 * D_RW].astype(F32),
                            v=zs_s[rows, 2 * D_RW:3 * D_RW].astype(F32), kkn=kkn_s[rows, :].astype(F32),
                            wd=zs_s[rows, o + RW_G_LORA:o + RW_G_LORA + LANES].astype(F32),
                            ad=zs_s[rows, o + RW_G_LORA + LANES:o + RW_G_LORA + 2 * LANES]))
        for st in sts:
            d = st['d']
            tw_hi, tw_lo = _split2(jnp.tanh(st['wd']))
            st['w_lora'] = _dot(tw_hi, w2_ref[d, 0]) + (_dot(tw_hi, w2_ref[d, 1]) + _dot(tw_lo, w2_ref[d, 0]))
            st['a_lora'] = _dot(st['ad'], a2_ref[d])
        for st in sts:
            d = st['d']
            w_log = -_softplus(-(w0_ref[d:d + 1, :] + st['w_lora'])) - 0.5
            st['lw'] = -jnp.exp(w_log)
            st['a'] = _sigmoid(a0_ref[d:d + 1, :] + st['a_lora'])
            st['kd'] = st['k'] * (1.0 + (st['a'] - 1.0) * ka_ref[...])
            st['bon'] = _head_sum(st['r'] * st['kd'] * rk_ref[...], ones_bd, single_pass=True)
            st['cum'] = _dot_exact_lhs(tri_ref[d], st['lw'])
        out = []
        for st in sts:
            d, r, v, kd, kkn, a, cum = st['d'], st['r'], st['v'], st['kd'], st['kkn'], st['a'], st['cum']
            bon_s[d, pl.ds(st['c0'], L), :] = (st['bon'] * v).astype(bon_s.dtype)
            e_pos = jnp.exp(cum)
            e_neg = jnp.exp(-cum)
            rh = r * e_pos
            kh = kd * e_neg
            bh = (kkn * a) * e_neg
            ah = -kkn * jnp.exp(cum - st['lw'])
            last = L - 1 if d == 0 else 0
            gam = e_pos[last:last + 1, :]
            bhg = bh * gam
            khg = kh * gam
            chains = []
            for p in range(RW_PAIRS):
                cols = slice(p * LANES, (p + 1) * LANES)
                chains.append(dict(
                    d=d, p=p, gam=gam[:, cols], v_st=stack(v[:, cols]),
                    lhs=jnp.concatenate([stack(ah[:, cols]), stack(rh[:, cols])], axis=0),
                    rhs=jnp.concatenate([stack(bh[:, cols]), stack(kh[:, cols])], axis=0),
                    bk=jnp.concatenate([stack(bhg[:, cols]), stack(khg[:, cols])], axis=0)))
            out.append(chains)
        return out

    n_levels = int(math.log2(L))

    def state_free(chains):
        ri = lax.broadcasted_iota(jnp.int32, (L2, L2), 0)
        ci = lax.broadcasted_iota(jnp.int32, (L2, L2), 1)
        eye = jnp.where(ri == ci, 1.0, 0.0)
        rl = jnp.where(ri < L, ri, ri - L)
        cl = jnp.where(ci < L, ci, ci - L)
        strict = (cl < rl, cl > rl)
        incl = (cl <= rl, cl >= rl)
        level = ri ^ ci

        for ch in chains:
            ch['sc'] = _dot_nt(ch['lhs'], ch['rhs'])
        for ch in chains:
            sc, d = ch['sc'], ch['d']
            n_ab = jnp.where(strict[d], sc[0:L2, 0:L2], 0.0)
            m_ak = jnp.where(strict[d], sc[0:L2, L2:2 * L2], 0.0).astype(BF16)
            t_rb = jnp.where(incl[d], sc[L2:2 * L2, 0:L2], 0.0)
            t_rk = jnp.where(incl[d], sc[L2:2 * L2, L2:2 * L2], 0.0)
            ch['t_rbk'] = jnp.concatenate([t_rb, t_rk], axis=1).astype(BF16)
            ch['mv'] = _dot(m_ak, ch['v_st'])
            ch['t'] = eye + jnp.where(level == 1, n_ab, 0.0)
            ch['n_lv'] = [jnp.where(lax.shift_right_logical(level, lv) == 1, n_ab, 0.0).astype(BF16)
                          for lv in range(1, n_levels)]
            del ch['sc']
        for lv in range(n_levels - 1):
            for ch in chains:
                ch['tb'] = ch['t'].astype(BF16)
                ch['x'] = _dot(ch['n_lv'][lv], ch['tb']).astype(BF16)
            for ch in chains:
                ch['t'] = ch['t'] + _dot(ch['tb'], ch['x'])

    def state_step(chains, c0s):
        for ch in chains:
            ch['s_prev'] = s_s[ch['d'], ch['p']]
            ch['ar_s'] = _dot_nt(ch['lhs'], ch['s_prev'].astype(BF16))
        for ch in chains:
            u_rhs = ch['ar_s'][0:L2] + ch['mv']
            u_st = _dot(ch['t'].astype(BF16), u_rhs.astype(BF16)).astype(BF16)
            ch['uv'] = jnp.concatenate([u_st, ch['v_st']], axis=0)
        for ch in chains:
            y_st = ch['ar_s'][L2:2 * L2] + _dot(ch['t_rbk'], ch['uv'])
            ch['y'] = y_st[0:L] + y_st[L:L2]
            s_s[ch['d'], ch['p']] = ch['s_prev'] * ch['gam'] + _dot_tn(ch['uv'], ch['bk'])
        for d in range(2):
            yd_s[d, pl.ds(c0s[d], L), :] = jnp.concatenate([ch['y'] for ch in chains if ch['d'] == d], axis=1)

    steps = 2

    def body(i, carry):
        starts = []
        for j in range(steps):
            step = i * steps + j
            starts.append((pl.multiple_of(step * L, L), pl.multiple_of((n_ch - 1 - step) * L, L)))
        prepared = prep([(c0s[d], d) for c0s in starts for d in range(2)])
        groups = [(prepared[2 * j] + prepared[2 * j + 1], starts[j]) for j in range(steps)]
        state_free([ch for chains, _ in groups for ch in chains])
        for chains, c0s in groups:
            state_step(chains, c0s)
        return carry

    lax.fori_loop(0, n_ch // steps, body, 0)

    def epilogue(i, carry):
        c0 = pl.multiple_of(i * L, L)
        ones_bd = bd_ref[...]
        ys = yd_s[0, pl.ds(c0, L), :] + yd_s[1, pl.ds(c0, L), :]
        mu = _head_sum(ys, ones_bd) * (1.0 / RW_HEAD)
        yc = ys - mu
        var = _head_sum(yc * yc, ones_bd) * (1.0 / RW_HEAD)
        yn = yc * lax.rsqrt(var + RW_LN_EPS) * lnw_ref[...] + lnb_ref[...]
        bonus = bon_s[0, pl.ds(c0, L), :].astype(F32) + bon_s[1, pl.ds(c0, L), :].astype(F32)
        out = (yn + bonus) * g_s[pl.ds(c0, L), :].astype(F32)
        y_ref[pl.ds(c0, L), :] = out.astype(y_ref.dtype)
        return carry

    lax.fori_loop(0, n_ch, epilogue, 0, unroll=min(EPILOGUE_UNROLL, n_ch))
    for d in range(2):
        for p in range(RW_PAIRS):
            s_pair = s_s[d, p]
            sfin_ref[d, 2 * p] = s_pair[0:RW_HEAD, 0:RW_HEAD]
            sfin_ref[d, 2 * p + 1] = s_pair[RW_HEAD:, RW_HEAD:]


def _rwkv_mixer(z, s0, p, n_batch, t_len):
    L = SCAN_CHUNK
    c2 = lambda b: (0, 0)
    c3 = lambda b: (0, 0, 0)
    st_spec = pl.BlockSpec((None, 2, RW_HEADS, RW_HEAD, RW_HEAD), lambda b: (b, 0, 0, 0, 0))
    zero_init = s0 is None
    return pl.pallas_call(
        functools.partial(_rwkv_kernel, t_len=t_len, zero_init=zero_init),
        grid=(n_batch,),
        in_specs=[pl.BlockSpec((t_len, RW_ZCOLS), lambda b: (b, 0), pipeline_mode=pl.Buffered(1))]
        + ([] if zero_init else [st_spec]) + [
            pl.BlockSpec((2, RW_ZCOLS), c2),
            pl.BlockSpec((1, D_RW), c2),
            pl.BlockSpec((1, D_RW), c2),
            pl.BlockSpec((1, D_RW), c2),
            pl.BlockSpec((RW_G_LORA, D_RW), c2),
            pl.BlockSpec((2, D_RW), c2),
            pl.BlockSpec((2, 2, LANES, D_RW), lambda b: (0, 0, 0, 0)),
            pl.BlockSpec((2, D_RW), c2),
            pl.BlockSpec((2, LANES, D_RW), c3),
            pl.BlockSpec((1, D_RW), c2),
            pl.BlockSpec((1, D_RW), c2),
            pl.BlockSpec((2, L, L), c3),
            pl.BlockSpec((LANES, LANES), c2),
        ],
        out_specs=[pl.BlockSpec((t_len, D_RW), lambda b: (b, 0)), st_spec],
        out_shape=[
            jax.ShapeDtypeStruct((n_batch * t_len, D_RW), Y_DTYPE),
            jax.ShapeDtypeStruct((n_batch, 2, RW_HEADS, RW_HEAD, RW_HEAD), F32),
        ],
        scratch_shapes=[
            pltpu.VMEM((2, L + 2 * HALO, RW_ZCOLS), F32),
            pltpu.VMEM((2, RW_PAIRS, LANES, LANES), F32),
            pltpu.VMEM((2, t_len, D_RW), F32),
            pltpu.VMEM((2, t_len, D_RW), BF16),
            pltpu.VMEM((t_len, D_RW), BF16),
            pltpu.VMEM((t_len, RW_ZCOLS), BF16),
            pltpu.VMEM((t_len, D_RW), BF16),
        ],
        compiler_params=_ARB1,
    )(z, *(() if zero_init else (s0,)), p['mu'], p['kk'], p['ka'], p['rk'], p['g2'], p['w0'], p['w2'], p['a0'],
      p['a2'], p['ln_w'], p['ln_b'], p['tri'], p['ones_bd'])


def _mlstm_kernel(*refs, t_len, zero_init):
    L = ML_SCAN
    n_ch = t_len // L
    z_ref = refs[0]
    (cw_ref, cb_ref, gb_ref, nw_ref, tri_ref, esel_ref, shift_ref, ones_ref,
     y_ref, cfin_ref, nfin_ref, mfin_ref, c_s, n_s, m_s, hd_s, qk_s, gx_s, gr_s) = refs[1 if zero_init else 4:]
    if zero_init:
        c_s[...] = jnp.zeros(c_s.shape, F32)
        n_s[...] = jnp.zeros(n_s.shape, F32)
        m_s[...] = jnp.zeros(m_s.shape, F32)
    else:
        c0_ref, n0_ref, m0_ref = refs[1:4]
        c_s[...] = c0_ref[...]
        m_s[...] = m0_ref[...]
        for d in range(2):
            for h in range(ML_HEADS):
                n_s[d, h] = jnp.broadcast_to(n0_ref[d, h:h + 1, :], (ML_DK, ML_DK)).T
    q_scale = ML_DK ** -0.5
    g_lo = 4 * D_ML
    n_real = 4 * ML_HEADS

    def conv_chunk(i, carry):
        c0 = pl.multiple_of(i * L, L)
        gts = z_ref[pl.ds(c0, L), g_lo:g_lo + LANES].astype(F32) + gb_ref[...]
        x16 = gts.T[0:n_real, :]
        lf_hi, lf_lo = _split2(-_softplus(-x16))
        lane = lax.broadcasted_iota(jnp.int32, (8, L), 1)
        stats = []
        for d in range(2):
            b16 = _dot_nt(lf_hi, tri_ref[d]) + _dot_nt(lf_lo, tri_ref[d])
            x8 = x16[8 * d:8 * d + 8, :]
            b8 = b16[8 * d:8 * d + 8, :]
            g8 = pltpu.roll(x8, ML_HEADS, axis=0) - b8
            stats.append([g8, b8, g8])
        step = 1
        while step < L:
            for d in range(2):
                cm8 = stats[d][2]
                if d == 0:
                    sh = jnp.where(lane >= step, pltpu.roll(cm8, step, axis=1), -jnp.inf)
                else:
                    sh = jnp.where(lane < L - step, pltpu.roll(cm8, L - step, axis=1), -jnp.inf)
                stats[d][2] = jnp.maximum(cm8, sh)
            step *= 2

        ext = _load_ext(z_ref, c0, L, t_len, 0, 2 * D_ML, raw=True)
        taps = _dot(shift_ref[...], ext)
        qk = cb_ref[...] + ext[HALO:HALO + L].astype(F32) * cw_ref[CONV_LEFT:CONV_LEFT + 1, :]
        for n, j in enumerate(jj for jj in range(CONV_W) if jj != CONV_LEFT):
            qk = qk + taps[n * L:(n + 1) * L] * cw_ref[j:j + 1, :]
        qk = _silu(qk)
        qk_s[pl.ds(c0, L), 0:D_ML] = (qk[:, :D_ML] * q_scale).astype(BF16)
        qk_s[pl.ds(c0, L), D_ML:2 * D_ML] = qk[:, D_ML:].astype(BF16)

        for d, (g8, b8, cm8) in enumerate(stats):
            x32 = jnp.concatenate([g8, b8, cm8, jnp.zeros((8, L), F32)], axis=0)
            hi = x32.astype(BF16)
            r1 = x32 - hi.astype(F32)
            mid = r1.astype(BF16)
            lo = (r1 - mid.astype(F32)).astype(BF16)
            gx_s[d, i] = jnp.concatenate([hi, mid, lo], axis=0)
            gr_s[d, i] = g8
        return carry

    lax.fori_loop(0, n_ch, conv_chunk, 0, unroll=min(ML_UNROLL, n_ch))

    def prep(ci, d):
        c0 = pl.multiple_of(ci * L, L)
        q = qk_s[pl.ds(c0, L), 0:D_ML]
        k = qk_s[pl.ds(c0, L), D_ML:2 * D_ML]
        v = z_ref[pl.ds(c0, L), 2 * D_ML:3 * D_ML]
        g8 = gr_s[d, ci]
        cb = _dot_tn(gx_s[d, ci], esel_ref[...])
        ri = lax.broadcasted_iota(jnp.int32, (L, L), 0)
        ci = lax.broadcasted_iota(jnp.int32, (L, L), 1)
        causal = (ci <= ri) if d == 0 else (ci >= ri)
        last = L - 1 if d == 0 else 0
        ones = ones_ref[...]
        chains = []
        for h in range(ML_HEADS):
            cols = slice(h * ML_DK, (h + 1) * ML_DK)
            chains.append(dict(
                d=d, h=h, last=last, causal=causal,
                g_row=g8[ML_HEADS + h:ML_HEADS + h + 1, :],
                g_b=cb[:, cols], b_b=cb[:, D_ML + h * ML_DK:D_ML + (h + 1) * ML_DK],
                cm_b=cb[:, 2 * D_ML + h * ML_DK:2 * D_ML + (h + 1) * ML_DK],
                qb=q[:, cols], kb=k[:, cols],
                v=v[:, cols].astype(F32), v_ones=jnp.concatenate([v[:, cols].astype(BF16), ones], axis=1)))
        return chains

    def body(i, carry):
        c0s = (pl.multiple_of(i * L, L), pl.multiple_of((n_ch - 1 - i) * L, L))
        chains = prep(i, 0) + prep(n_ch - 1 - i, 1)
        for ch in chains:
            d, h = ch['d'], ch['h']
            ch['qk'] = _dot_nt(ch['qb'], ch['kb'])
            ch['c_prev'] = c_s[d, h]
            ch['n_prev'] = n_s[d, h]
            ch['m_prev'] = m_s[d, h:h + 1, :]
            ch['qcn'] = _dot(ch['qb'], jnp.concatenate([ch['c_prev'].astype(BF16),
                                                        ch['n_prev'].astype(BF16)], axis=1))
        for ch in chains:
            mu = jnp.maximum(ch['m_prev'], ch['cm_b'])
            ch['mu'] = mu
            ch['s'] = ch['qk'] * jnp.exp(jnp.where(ch['causal'], ch['g_row'] - mu, -jnp.inf))
            ch['w_inter'] = jnp.exp(ch['m_prev'] - mu)
        for ch in chains:
            last = ch['last']
            ch['sv'] = _dot(ch['s'].astype(BF16), ch['v_ones'])
            mu_last = ch['mu'][last:last + 1, :]
            ch['m_new'] = ch['b_b'][last:last + 1, :] + mu_last
            ch['dec'] = jnp.exp(ch['m_prev'] - mu_last)
            w_s = jnp.exp(ch['g_b'] - mu_last)
            ch['upd'] = _dot_tn(ch['kb'], jnp.concatenate([(w_s * ch['v']).astype(BF16),
                                                           w_s.astype(BF16)], axis=1))
        for ch in chains:
            d, h = ch['d'], ch['h']
            num = ch['sv'][:, :ML_DK] + ch['w_inter'] * ch['qcn'][:, :ML_DK]
            den = ch['sv'][:, ML_DK:] + ch['w_inter'] * ch['qcn'][:, ML_DK:]
            ch['out'] = num / jnp.maximum(jnp.abs(den), jnp.exp(-(ch['b_b'] + ch['mu'])))
            c_s[d, h] = ch['dec'] * ch['c_prev'] + ch['upd'][:, :ML_DK]
            n_s[d, h] = ch['dec'] * ch['n_prev'] + ch['upd'][:, ML_DK:]
            m_s[d, h:h + 1, :] = ch['m_new']
        for d in range(2):
            hd_s[d, pl.ds(c0s[d], L), :] = jnp.concatenate([ch['out'] for ch in chains if ch['d'] == d], axis=1)
        return carry

    lax.fori_loop(0, n_ch, body, 0, unroll=min(ML_UNROLL, n_ch))

    def epilogue(i, carry):
        c0 = pl.multiple_of(i * L, L)
        ones = ones_ref[...]
        hs = hd_s[0, pl.ds(c0, L), :] + hd_s[1, pl.ds(c0, L), :]
        o = z_ref[pl.ds(c0, L), 3 * D_ML:4 * D_ML].astype(F32)
        mu = _head_sum(hs, ones) * (1.0 / ML_DK)
        xc = hs - mu
        var = _head_sum(xc * xc, ones) * (1.0 / ML_DK)
        y = xc * lax.rsqrt(var + NORM_EPS) * nw_ref[...] * _sigmoid(o)
        y_ref[pl.ds(c0, L), :] = y.astype(y_ref.dtype)
        return carry

    lax.fori_loop(0, n_ch, epilogue, 0, unroll=min(EPILOGUE_UNROLL, n_ch))
    cfin_ref[...] = c_s[...]
    mfin_ref[...] = m_s[...]
    for d in range(2):
        for h in range(ML_HEADS):
            nfin_ref[d, h:h + 1, :] = n_s[d, h].T[0:1, :]


def _mlstm_mixer(z, states, p, n_batch, t_len):
    assert z.dtype == BF16
    L = ML_SCAN
    c2 = lambda b: (0, 0)
    c3 = lambda b: (0, 0, 0)
    c_spec = pl.BlockSpec((None, 2, ML_HEADS, ML_DK, ML_DK), lambda b: (b, 0, 0, 0, 0))
    m_spec = pl.BlockSpec((None, 2, ML_HEADS, ML_DK), lambda b: (b, 0, 0, 0))
    zero_init = states is None
    state_specs = [] if zero_init else [c_spec, m_spec, m_spec]
    return pl.pallas_call(
        functools.partial(_mlstm_kernel, t_len=t_len, zero_init=zero_init),
        grid=(n_batch,),
        in_specs=[pl.BlockSpec((t_len, ML_ZCOLS), lambda b: (b, 0))] + state_specs + [
            pl.BlockSpec((CONV_W, 2 * D_ML), c2),
            pl.BlockSpec((1, 2 * D_ML), c2),
            pl.BlockSpec((1, LANES), c2),
            pl.BlockSpec((1, D_ML), c2),
            pl.BlockSpec((2, L, L), c3),
            pl.BlockSpec(p['esel'].shape, c2),
            pl.BlockSpec(p['shift'].shape, c2),
            pl.BlockSpec((L, LANES), c2),
        ],
        out_specs=[pl.BlockSpec((t_len, D_ML), lambda b: (b, 0)), c_spec, m_spec, m_spec],
        out_shape=[
            jax.ShapeDtypeStruct((n_batch * t_len, D_ML), Y_DTYPE),
            jax.ShapeDtypeStruct((n_batch, 2, ML_HEADS, ML_DK, ML_DK), F32),
            jax.ShapeDtypeStruct((n_batch, 2, ML_HEADS, ML_DK), F32),
            jax.ShapeDtypeStruct((n_batch, 2, ML_HEADS, ML_DK), F32),
        ],
        scratch_shapes=[
            pltpu.VMEM((2, ML_HEADS, ML_DK, ML_DK), F32),
            pltpu.VMEM((2, ML_HEADS, ML_DK, ML_DK), F32),
            pltpu.VMEM((2, ML_HEADS, ML_DK), F32),
            pltpu.VMEM((2, t_len, D_ML), F32),
            pltpu.VMEM((t_len, 2 * D_ML), BF16),
            pltpu.VMEM((2, t_len // L, 96, L), BF16),
            pltpu.VMEM((2, t_len // L, 8, L), F32),
        ],
        compiler_params=_ARB1,
    )(z, *(() if zero_init else states), p['conv_w'], p['conv_b'], p['gate_b'], p['norm_w'], p['tri'], p['esel'],
      p['shift'], p['ones'])


def _conv_shift(n):
    taps = jnp.array([j for j in range(CONV_W) if j != CONV_LEFT])
    t = jnp.arange((CONV_W - 1) * n)
    src = HALO + t % n + taps[t // n] - CONV_LEFT
    return (src[:, None] == jnp.arange(n + 2 * HALO)[None, :]).astype(BF16)


def _mlstm_consts():
    L = ML_SCAN
    lower = jnp.tril(jnp.ones((L, L), F32))
    tri = jnp.stack([lower, lower.T]).astype(BF16)
    rows = jnp.arange(32)
    blk = jnp.arange(3 * D_ML) // ML_DK
    src_row = 8 * (blk // ML_HEADS) + ML_HEADS + blk % ML_HEADS
    esel = (rows[:, None] == src_row[None, :]).astype(BF16)
    esel = jnp.concatenate([esel, esel, esel], axis=0)
    return dict(tri=tri, esel=esel, shift=_conv_shift(L), ones=jnp.ones((L, LANES), BF16))


def _block_diag(w):
    n, c, d = w.shape
    return jnp.einsum('ncd,nm->ncmd', w, jnp.eye(n, dtype=w.dtype)).reshape(n * c, n * d)


def _scan_tri():
    L = SCAN_CHUNK
    lower = jnp.tril(jnp.ones((L, L), F32))
    return jnp.stack([lower, lower.T]).astype(BF16)


def _prep_even(ev_w_in, pool_w, pool_scale, rg_conv_w, rg_conv_b, rg_wa, rg_ba, rg_wx, rg_bx, rg_lam, ev_w_out, j):
    wg = jnp.concatenate([_block_diag(rg_wa[j, 0]), _block_diag(rg_wx[j, 0]),
                          _block_diag(rg_wa[j, 1]), _block_diag(rg_wx[j, 1])], axis=1)
    bg = jnp.concatenate([rg_ba[j, 0], rg_bx[j, 0], rg_ba[j, 1], rg_bx[j, 1]]).reshape(1, 4 * D_RG)
    return dict(
        w_in=[ev_w_in[j].astype(BF16)],
        w_out=ev_w_out[j].astype(BF16),
        pool_w=pool_w[j].astype(BF16),
        pool_scale=pool_scale[j].reshape(1, D_POOL),
        conv_w=rg_conv_w[j],
        conv_b=rg_conv_b[j].reshape(1, D_RG),
        wg=wg.astype(BF16),
        bg=bg,
        lam=rg_lam[j],
    )


def _pad_cols(w, cuts, widths):
    pieces = jnp.split(w, cuts, axis=-1)
    out = []
    for piece, width in zip(pieces, widths):
        extra = width - piece.shape[-1]
        if extra:
            piece = jnp.pad(piece, [(0, 0)] * (piece.ndim - 1) + [(0, extra)])
        out.append(piece)
    return jnp.concatenate(out, axis=-1)


def _prep_odd(od_w_in, rw_mu, rw_w0, rw_w2, rw_a0, rw_a2, rw_kk, rw_ka, rw_rk, rw_g2, rw_ln_w, rw_ln_b,
              ml_conv_w, ml_conv_b, ml_bi, ml_bf, ml_norm_w, od_w_out, j):
    rw_cols = 3 * D_RW + RW_G_LORA + RW_W_LORA + RW_A_LORA
    base = 3 * D_RW + RW_G_LORA
    cuts = [base, base + RW_W_LORA]
    widths = [base, LANES, LANES]
    w_rw = _pad_cols(od_w_in[j][:, :rw_cols], cuts, widths)
    w_ml = _pad_cols(od_w_in[j][:, rw_cols:], [4 * D_ML], [4 * D_ML, LANES])
    pad_rows = lambda w: jnp.pad(w, ((0, 0), (0, LANES - w.shape[1]), (0, 0)))
    gate_b = jnp.concatenate([ml_bi[j, 0], ml_bf[j, 0], ml_bi[j, 1], ml_bf[j, 1]])
    gate_b = jnp.pad(gate_b, (0, LANES - gate_b.shape[0])).reshape(1, LANES)
    hm = jnp.arange(LANES) // RW_HEAD
    rw = dict(
        mu=_pad_cols(rw_mu[j], cuts, widths),
        kk=rw_kk[j].reshape(1, D_RW), ka=rw_ka[j].reshape(1, D_RW), rk=rw_rk[j].reshape(1, D_RW),
        g2=rw_g2[j].astype(BF16),
        w0=rw_w0[j], w2=jnp.stack(_split2(pad_rows(rw_w2[j])), axis=1),
        a0=rw_a0[j], a2=pad_rows(rw_a2[j]).astype(BF16),
        ln_w=rw_ln_w[j].reshape(1, D_RW), ln_b=rw_ln_b[j].reshape(1, D_RW),
        tri=_scan_tri(),
        ones_bd=(hm[:, None] == hm[None, :]).astype(BF16),
    )
    ml = dict(
        conv_w=ml_conv_w[j], conv_b=ml_conv_b[j].reshape(1, 2 * D_ML),
        gate_b=gate_b, norm_w=ml_norm_w[j].reshape(1, D_ML), **_mlstm_consts(),
    )
    return dict(w_in=[w_rw.astype(BF16), w_ml.astype(BF16)], w_out=od_w_out[j].astype(BF16), rw=rw, ml=ml)


def _trunk(x, mod, rows_per_mod, mod_base, grid, states, params, n_batch, t_len):
    ev, od, dense = params
    rg0, rw0, ml0 = states
    x = x.reshape(n_batch * t_len, D_MODEL)
    spec = (rows_per_mod, mod_base)

    (z,) = _in_proj(x, mod[0], *spec, dense['norm1_w'][0], ev['w_in'])
    y_ev, rg_f = _even_mixer(z, rg0, ev, n_batch, t_len, grid)
    x = _out_mlp([y_ev], x, mod[0], *spec, dense['norm2_w'][0], ev['w_out'],
                 dense['mlp_w1'], dense['mlp_w2'], 0, dense['final_norm_w'], False)

    z_rw, z_ml = _in_proj(x, mod[1], *spec, dense['norm1_w'][1], od['w_in'])
    y_rw, s_f = _rwkv_mixer(z_rw, rw0, od['rw'], n_batch, t_len)
    y_ml, c_f, n_f, m_f = _mlstm_mixer(z_ml, ml0, od['ml'], n_batch, t_len)
    y = _out_mlp([y_rw, y_ml], x, mod[1], *spec, dense['norm2_w'][1], od['w_out'],
                 dense['mlp_w1'], dense['mlp_w2'], 1, dense['final_norm_w'], True)
    return y.reshape(n_batch, t_len, D_MODEL), (rg_f, s_f, c_f, n_f, m_f)


def kernel(x_prompt, x_sample, state_rglru, state_rwkv, state_mlstm_C, state_mlstm_n, state_mlstm_m,
           c, c_ctx, norm1_w, norm2_w, w_mod, b_mod, mlp_w1, mlp_w2, final_norm_w,
           ev_w_in, pool_w, pool_scale, rg_conv_w, rg_conv_b, rg_wa, rg_ba, rg_wx, rg_bx, rg_lam, ev_w_out,
           od_w_in, rw_mu, rw_w0, rw_w2, rw_a0, rw_a2, rw_kk, rw_ka, rw_rk, rw_g2, rw_ln_w, rw_ln_b,
           ml_conv_w, ml_conv_b, ml_bi, ml_bf, ml_norm_w, od_w_out):
    bp, tp, _ = x_prompt.shape
    bs, ts, _ = x_sample.shape

    assert bs + 1 <= N_COND_ROWS and ts % ROW_BLOCK == 0 and (bp * tp) % ROW_BLOCK == 0
    cond = jnp.concatenate([c_ctx[None, :], c, jnp.zeros((N_COND_ROWS - 1 - bs, D_MODEL), F32)], axis=0)
    mod = _modulation(cond, w_mod, b_mod)

    ev = _prep_even(ev_w_in, pool_w, pool_scale, rg_conv_w, rg_conv_b, rg_wa, rg_ba, rg_wx, rg_bx, rg_lam,
                    ev_w_out, 0)
    od = _prep_odd(od_w_in, rw_mu, rw_w0, rw_w2, rw_a0, rw_a2, rw_kk, rw_ka, rw_rk, rw_g2, rw_ln_w, rw_ln_b,
                   ml_conv_w, ml_conv_b, ml_bi, ml_bf, ml_norm_w, od_w_out, 0)
    dense = dict(norm1_w=norm1_w, norm2_w=norm2_w, mlp_w1=mlp_w1.astype(BF16), mlp_w2=mlp_w2.astype(BF16),
                 final_norm_w=final_norm_w)
    params = (ev, od, dense)

    zero_states = (jnp.zeros((bp, 2, D_RG), F32), None, None)
    y_prompt, (rg_f, s_f, c_f, n_f, m_f) = _trunk(
        x_prompt, mod, bp * tp, 0, False, zero_states, params, bp, tp)

    sample_states = (
        state_rglru[:, 0],
        state_rwkv[:, 0],
        (state_mlstm_C[:, 0], state_mlstm_n[:, 0],
         jnp.broadcast_to(state_mlstm_m[:, 0][..., None], (bs, 2, ML_HEADS, ML_DK))),
    )
    y_sample, _ = _trunk(x_sample, mod, ts, 1, True, sample_states, params, bs, ts)

    return (y_prompt, y_sample,
            rg_f[:, None],
            s_f[:, None],
            c_f[:, None],
            n_f[:, None],
            m_f[:, None, :, :, 0])
```

```python
import functools
import math

import jax
import jax.numpy as jnp
from jax import lax
from jax.experimental import pallas as pl
from jax.experimental.pallas import tpu as pltpu

F32 = jnp.float32
BF16 = jnp.bfloat16

D_MODEL = 1024
DEPTH = 2
GRID_W = 64
D_FF = 4 * D_MODEL
NORM_EPS = 1e-6
N_MOD = 6

D_POOL = D_MODEL // 2
POOL_WINDOWS = (2, 4, 8, 16)
POOL_GW = D_POOL // len(POOL_WINDOWS)
D_RG = D_MODEL // 2
RG_BLOCKS = 8
RG_C = 8.0
CONV_W = 4
CONV_LEFT = 2
EV_COLS = D_POOL + 2 * D_RG

D_RW = D_MODEL // 2
RW_HEAD = 64
RW_HEADS = D_RW // RW_HEAD
RW_PAIRS = RW_HEADS // 2
RW_W_LORA = 64
RW_A_LORA = 64
RW_G_LORA = 128
RW_LN_EPS = 64e-5
RW_ZCOLS = 3 * D_RW + RW_G_LORA + 128 + 128
D_ML = D_MODEL // 2
ML_HEADS = 4
ML_DK = D_ML // ML_HEADS
ML_ZCOLS = 4 * D_ML + 128

LANES = 128
HALO = 16
Z_DTYPE = BF16
Y_DTYPE = BF16
ROW_BLOCK = 512
SCAN_CHUNK = 64
ML_SCAN = 128
RG_CHUNK = 256
POOL_SEGS_PER_TRIP = 8
ML_UNROLL = 4
EPILOGUE_UNROLL = 8
MOD_COL_BLOCK = 1536
N_COND_ROWS = 16
Z_DOUBLE_BUFFER_MAX = 4 * 1024 * 1024
VMEM_LIMIT = 56 * 1024 * 1024

_ARB1 = pltpu.CompilerParams(dimension_semantics=("arbitrary",), vmem_limit_bytes=VMEM_LIMIT)
_ARB2 = pltpu.CompilerParams(dimension_semantics=("arbitrary", "arbitrary"), vmem_limit_bytes=VMEM_LIMIT)


def _dot(a, b):
    return jnp.dot(a, b, preferred_element_type=F32)


def _dot_nt(a, b):
    return lax.dot_general(a, b, (((1,), (1,)), ((), ())), preferred_element_type=F32)


def _dot_tn(a, b):
    return lax.dot_general(a, b, (((0,), (0,)), ((), ())), preferred_element_type=F32)


def _split2(x):
    hi = x.astype(BF16)
    lo = (x - hi.astype(F32)).astype(BF16)
    return hi, lo


def _dot_f32(a, b):
    ah, al = _split2(a)
    bh, bl = _split2(b)
    return _dot(ah, bh) + (_dot(ah, bl) + _dot(al, bh))


def _dot_exact_lhs(m_bf16, x):
    hi, lo = _split2(x)
    return _dot(m_bf16, hi) + _dot(m_bf16, lo)


def _dot_exact_rhs(x, m_bf16):
    hi, lo = _split2(x)
    return _dot(hi, m_bf16) + _dot(lo, m_bf16)


def _softplus(x):
    return jnp.maximum(x, 0.0) + jnp.log1p(jnp.exp(-jnp.abs(x)))


def _sigmoid(x):
    return jax.nn.sigmoid(x)


def _silu(x):
    return x * jax.nn.sigmoid(x)


def _gelu_tanh(x):
    c = math.sqrt(2.0 / math.pi)
    return x * (0.5 * (1.0 + jnp.tanh(c * (x + 0.044715 * (x * x * x)))))


def _rms_norm(x, w):
    ms = jnp.mean(x * x, axis=-1, keepdims=True)
    return x * lax.rsqrt(ms + NORM_EPS) * w


def _load_ext(z_ref, c0, n, t_len, lo, hi, raw=False):
    dt = z_ref.dtype if raw else F32
    cur = z_ref[pl.ds(c0, n), lo:hi].astype(dt)
    pb = pl.multiple_of(jnp.maximum(c0 - HALO, 0), HALO)
    pa = pl.multiple_of(jnp.minimum(c0 + n, t_len - HALO), HALO)
    before = z_ref[pl.ds(pb, HALO), lo:hi].astype(dt)
    after = z_ref[pl.ds(pa, HALO), lo:hi].astype(dt)
    before = jnp.where(c0 > 0, before, jnp.zeros_like(before))
    after = jnp.where(c0 + n < t_len, after, jnp.zeros_like(after))
    return jnp.concatenate([before, cur, after], axis=0)


def _head_sum(x, ones_bd, single_pass=False):
    n = x.shape[0]
    groups = x.shape[1] // LANES
    xs = jnp.concatenate([x[:, g * LANES:(g + 1) * LANES] for g in range(groups)], axis=0)
    s = _dot(xs.astype(BF16), ones_bd) if single_pass else _dot_exact_rhs(xs, ones_bd)
    return jnp.concatenate([s[g * n:(g + 1) * n] for g in range(groups)], axis=1)


def _mod_kernel(c_ref, w_ref, b_ref, o_ref):
    s = _silu(c_ref[...])
    o_ref[...] = _dot_f32(s, w_ref[...]) + b_ref[...]


def _modulation(cond, w_mod, b_mod):
    rows = cond.shape[0]
    tn = MOD_COL_BLOCK
    out = pl.pallas_call(
        _mod_kernel,
        grid=(DEPTH, N_MOD * D_MODEL // tn),
        in_specs=[
            pl.BlockSpec((rows, D_MODEL), lambda l, j: (0, 0)),
            pl.BlockSpec((None, D_MODEL, tn), lambda l, j: (l, 0, j)),
            pl.BlockSpec((None, 1, tn), lambda l, j: (l, 0, j)),
        ],
        out_specs=pl.BlockSpec((None, rows, tn), lambda l, j: (l, 0, j)),
        out_shape=jax.ShapeDtypeStruct((DEPTH, rows, N_MOD * D_MODEL), F32),
        compiler_params=_ARB2,
    )(cond, w_mod, b_mod.reshape(DEPTH, 1, N_MOD * D_MODEL))
    return out.reshape(DEPTH, rows, N_MOD, D_MODEL)


def _inproj_kernel(*refs, n_out, col_chunk):
    x_ref, mod_ref, nw_ref = refs[:3]
    w_refs = refs[3:3 + n_out]
    z_refs = refs[3 + n_out:]
    h = _rms_norm(x_ref[...], nw_ref[...])
    h = h * (1.0 + mod_ref[1:2, :]) + mod_ref[0:1, :]
    hb = h.astype(BF16)
    for w_ref, z_ref in zip(w_refs, z_refs):
        n = w_ref.shape[1]
        for c in range(0, n, col_chunk):
            e = min(c + col_chunk, n)
            z_ref[:, c:e] = _dot(hb, w_ref[:, c:e]).astype(z_ref.dtype)


def _in_proj(x, mod, rows_per_mod, mod_base, norm_w, weights):
    m = x.shape[0]
    tm = ROW_BLOCK
    n_out = len(weights)
    const = lambda i: (0, 0)
    in_specs = [
        pl.BlockSpec((tm, D_MODEL), lambda i: (i, 0)),
        pl.BlockSpec((None, N_MOD, D_MODEL), lambda i: (mod_base + (i * tm) // rows_per_mod, 0, 0)),
        pl.BlockSpec((1, D_MODEL), const),
    ] + [pl.BlockSpec(w.shape, const, pipeline_mode=pl.Buffered(1)) for w in weights]
    out_specs = [pl.BlockSpec((tm, w.shape[1]), lambda i: (i, 0)) for w in weights]
    out_shape = [jax.ShapeDtypeStruct((m, w.shape[1]), Z_DTYPE) for w in weights]
    return pl.pallas_call(
        functools.partial(_inproj_kernel, n_out=n_out, col_chunk=512),
        grid=(m // tm,),
        in_specs=in_specs,
        out_specs=out_specs,
        out_shape=out_shape,
        compiler_params=_ARB1,
    )(x, mod, norm_w.reshape(1, D_MODEL), *weights)


def _outmlp_kernel(*refs, n_y, final, ff_chunk):
    y_refs = refs[:n_y]
    x_ref, mod_ref, nw_ref, wo_ref, w1_ref, w2_ref, fw_ref, o_ref = refs[n_y:]
    y = None
    row = 0
    for y_ref in y_refs:
        k = y_ref.shape[1]
        part = _dot(y_ref[...], wo_ref[row:row + k, :])
        y = part if y is None else y + part
        row += k
    x = x_ref[...] + mod_ref[2:3, :] * y
    h = _rms_norm(x, nw_ref[...]) * (1.0 + mod_ref[4:5, :]) + mod_ref[3:4, :]
    hb = h.astype(BF16)
    acc = None
    for c in range(0, D_FF, ff_chunk):
        u = jnp.maximum(_dot(hb, w1_ref[:, c:c + ff_chunk]), 0.0)
        part = _dot((u * u).astype(BF16), w2_ref[c:c + ff_chunk, :])
        acc = part if acc is None else acc + part
    x = x + mod_ref[5:6, :] * acc
    if final:
        x = _rms_norm(x, fw_ref[...])
    o_ref[...] = x


def _out_mlp(ys, x, mod, rows_per_mod, mod_base, norm_w, w_out, w1, w2, layer, final_w, final):
    m = x.shape[0]
    tm = ROW_BLOCK
    const = lambda i: (0, 0)
    single = dict(pipeline_mode=pl.Buffered(1))
    pick = lambda i: (layer, 0, 0)
    in_specs = [pl.BlockSpec((tm, y.shape[1]), lambda i: (i, 0)) for y in ys] + [
        pl.BlockSpec((tm, D_MODEL), lambda i: (i, 0)),
        pl.BlockSpec((None, N_MOD, D_MODEL), lambda i: (mod_base + (i * tm) // rows_per_mod, 0, 0)),
        pl.BlockSpec((1, D_MODEL), const),
        pl.BlockSpec(w_out.shape, const, **single),
        pl.BlockSpec((None,) + w1.shape[1:], pick, **single),
        pl.BlockSpec((None,) + w2.shape[1:], pick, **single),
        pl.BlockSpec((1, D_MODEL), const),
    ]
    return pl.pallas_call(
        functools.partial(_outmlp_kernel, n_y=len(ys), final=final, ff_chunk=1024),
        grid=(m // tm,),
        in_specs=in_specs,
        out_specs=pl.BlockSpec((tm, D_MODEL), lambda i: (i, 0)),
        out_shape=jax.ShapeDtypeStruct((m, D_MODEL), F32),
        compiler_params=_ARB1,
    )(*ys, x, mod, norm_w.reshape(1, D_MODEL), w_out, w1, w2, final_w.reshape(1, D_MODEL))


def _even_kernel(z_ref, h0_ref, pw_ref, ps_ref, cw_ref, cb_ref, wg_ref, bg_ref, lam_ref, band_ref,
                 y_ref, hfin_ref, cbuf, a_s, b_s, h_s, hf_s, *, t_len, grid):
    seg = GRID_W if grid else t_len
    n_seg = t_len // seg
    n_groups = len(POOL_WINDOWS)

    pos = lax.broadcasted_iota(jnp.int32, (seg, POOL_GW), 0)
    offs = [(w // 2, w - 1 - w // 2) for w in POOL_WINDOWS]
    gcols = [slice(g * POOL_GW, (g + 1) * POOL_GW) for g in range(n_groups)]

    def grid_row(rr, cols):
        valid = jnp.logical_and(rr >= 0, rr < n_seg)
        src = pl.multiple_of(jnp.clip(rr, 0, n_seg - 1) * seg, seg)
        return jnp.where(valid, z_ref[pl.ds(src, seg), cols].astype(F32), 0.0)

    def pool_rows(rows, run):
        items = []
        for r in rows:
            base = r * seg if isinstance(r, int) else pl.multiple_of(r * seg, seg)
            new_run = []
            for g, (lo_off, hi_off) in enumerate(offs):
                cols = gcols[g]
                if grid:
                    s1 = run[g] + grid_row(r + hi_off, cols) - grid_row(r - lo_off - 1, cols)
                    new_run.append(s1)
                    cnt_r = jnp.minimum(r + hi_off, n_seg - 1) - jnp.maximum(r - lo_off, 0) + 1
                    m1 = s1 / cnt_r.astype(F32)
                    hi = m1.astype(BF16)
                    r1 = m1 - hi.astype(F32)
                    mid = r1.astype(BF16)
                    parts = jnp.concatenate([hi, mid, (r1 - mid.astype(F32)).astype(BF16)], axis=0)
                else:
                    parts = z_ref[pl.ds(base, seg), cols]
                items.append(dict(g=g, base=base, parts=parts))
            run = new_run
        for it in items:
            it['s2'] = _dot(band_ref[it['g']], it['parts'])
        for it in items:
            lo_off, hi_off = offs[it['g']]
            cnt_c = jnp.minimum(pos + hi_off, seg - 1) - jnp.maximum(pos - lo_off, 0) + 1
            xg = z_ref[pl.ds(it['base'], seg), gcols[it['g']]].astype(F32)
            it['d'] = (it['s2'] / cnt_c.astype(F32) - xg).astype(BF16)
        for it in items:
            it['y'] = _dot(it['d'], pw_ref[it['g']])
        for it in items:
            cols = gcols[it['g']]
            y_ref[pl.ds(it['base'], seg), cols] = (it['y'] * ps_ref[:, cols]).astype(y_ref.dtype)
        return tuple(run) if grid else None

    if grid:
        init = []
        for g, (lo_off, hi_off) in enumerate(offs):
            s1 = jnp.zeros((seg, POOL_GW), F32)
            for o in range(min(hi_off, n_seg)):
                s1 = s1 + z_ref[o * seg:(o + 1) * seg, gcols[g]].astype(F32)
            init.append(s1)
        per_trip = POOL_SEGS_PER_TRIP
        lax.fori_loop(0, n_seg // per_trip,
                      lambda i, run: pool_rows([i * per_trip + j for j in range(per_trip)], run), tuple(init))
    else:
        pool_rows([0], None)

    ch = RG_CHUNK
    n_ch = t_len // ch
    rg_lo, rg_hi = D_POOL, D_POOL + D_RG

    def gates(c0, d):
        cbuf[...] = _load_ext(z_ref, c0, ch, t_len, rg_lo, rg_hi)
        xc = cb_ref[...]
        for j in range(CONV_W):
            off = HALO - CONV_LEFT + j
            xc = xc + cbuf[off:off + ch, :] * cw_ref[j:j + 1, :]
        g = _dot(xc.astype(BF16), wg_ref[:, d * 2 * D_RG:(d + 1) * 2 * D_RG]) \
            + bg_ref[:, d * 2 * D_RG:(d + 1) * 2 * D_RG]
        r = _sigmoid(g[:, :D_RG])
        i = _sigmoid(g[:, D_RG:])
        log_a = (-RG_C) * r * _softplus(-lam_ref[d:d + 1, :])
        a = jnp.exp(log_a)
        a_s[...] = a
        one_minus_a2 = jnp.tanh(-log_a) * (a * a + 1.0)
        root = jnp.where(one_minus_a2 > 0.0, one_minus_a2 * lax.rsqrt(one_minus_a2), 0.0)
        b_s[...] = root * (i * xc)

    def scan_rows(h, reverse):
        def row(t, h):
            tt = (ch - 1 - t) if reverse else t
            h = a_s[pl.ds(tt, 1), :] * h + b_s[pl.ds(tt, 1), :]
            h_s[pl.ds(tt, 1), :] = h
            return h
        return lax.fori_loop(0, ch, row, h, unroll=8)

    def fwd_body(i, h):
        c0 = pl.multiple_of(i * ch, ch)
        gates(c0, 0)
        h = scan_rows(h, False)
        hf_s[pl.ds(c0, ch), :] = h_s[...]
        return h

    h_fwd = lax.fori_loop(0, n_ch, fwd_body, h0_ref[0:1, :])

    def bwd_body(i, h):
        c0 = pl.multiple_of((n_ch - 1 - i) * ch, ch)
        gates(c0, 1)
        h = scan_rows(h, True)
        u_gate = z_ref[pl.ds(c0, ch), rg_hi:rg_hi + D_RG].astype(F32)
        y = (hf_s[pl.ds(c0, ch), :] + h_s[...]) * _gelu_tanh(u_gate)
        y_ref[pl.ds(c0, ch), D_POOL:D_POOL + D_RG] = y.astype(y_ref.dtype)
        return h

    h_bwd = lax.fori_loop(0, n_ch, bwd_body, h0_ref[1:2, :])
    hfin_ref[0:1, :] = h_fwd
    hfin_ref[1:2, :] = h_bwd


def _even_mixer(z, h0, p, n_batch, t_len, grid):
    assert z.dtype == BF16
    seg = GRID_W if grid else t_len
    const2 = lambda b: (0, 0)
    const3 = lambda b: (0, 0, 0)
    idx = jnp.arange(seg)
    delta = idx[None, :] - idx[:, None]
    band = jnp.stack([jnp.logical_and(delta >= -(w // 2), delta <= w - 1 - w // 2) for w in POOL_WINDOWS])
    band = jnp.tile(band.astype(BF16), (1, 1, 3 if grid else 1))
    return pl.pallas_call(
        functools.partial(_even_kernel, t_len=t_len, grid=grid),
        grid=(n_batch,),
        in_specs=[
            pl.BlockSpec((t_len, EV_COLS), lambda b: (b, 0)),
            pl.BlockSpec((None, 2, D_RG), lambda b: (b, 0, 0)),
            pl.BlockSpec(p['pool_w'].shape, const3),
            pl.BlockSpec((1, D_POOL), const2),
            pl.BlockSpec((CONV_W, D_RG), const2),
            pl.BlockSpec((1, D_RG), const2),
            pl.BlockSpec((D_RG, 4 * D_RG), const2),
            pl.BlockSpec((1, 4 * D_RG), const2),
            pl.BlockSpec((2, D_RG), const2),
            pl.BlockSpec(band.shape, const3),
        ],
        out_specs=[
            pl.BlockSpec((t_len, D_POOL + D_RG), lambda b: (b, 0)),
            pl.BlockSpec((None, 2, D_RG), lambda b: (b, 0, 0)),
        ],
        out_shape=[
            jax.ShapeDtypeStruct((n_batch * t_len, D_POOL + D_RG), Y_DTYPE),
            jax.ShapeDtypeStruct((n_batch, 2, D_RG), F32),
        ],
        scratch_shapes=[
            pltpu.VMEM((RG_CHUNK + 2 * HALO, D_RG), F32),
            pltpu.VMEM((RG_CHUNK, D_RG), F32),
            pltpu.VMEM((RG_CHUNK, D_RG), F32),
            pltpu.VMEM((RG_CHUNK, D_RG), F32),
            pltpu.VMEM((t_len, D_RG), F32),
        ],
        compiler_params=_ARB1,
    )(z, h0, p['pool_w'], p['pool_scale'], p['conv_w'], p['conv_b'], p['wg'], p['bg'], p['lam'], band)


def _rwkv_kernel(*refs, t_len, zero_init):
    L = SCAN_CHUNK
    n_ch = t_len // L
    L2 = 2 * L
    z_ref = refs[0]
    (mu_ref, kkw_ref, ka_ref, rk_ref, g2_ref, w0_ref, w2_ref, a0_ref, a2_ref, lnw_ref, lnb_ref, tri_ref, bd_ref,
     y_ref, sfin_ref, ebuf, s_s, yd_s, bon_s, g_s, zs_s, kkn_s) = refs[1 if zero_init else 2:]
    if zero_init:
        s_s[...] = jnp.zeros(s_s.shape, F32)
    else:
        s0_ref = refs[1]
        zero = jnp.zeros((RW_HEAD, RW_HEAD), F32)
        for d in range(2):
            for p in range(RW_PAIRS):
                top = jnp.concatenate([s0_ref[d, 2 * p], zero], axis=1)
                bot = jnp.concatenate([zero, s0_ref[d, 2 * p + 1]], axis=1)
                s_s[d, p] = jnp.concatenate([top, bot], axis=0)

    def stack(x):
        first_head = lax.broadcasted_iota(jnp.int32, (L, LANES), 1) < RW_HEAD
        return jnp.concatenate([jnp.where(first_head, x, 0.0), jnp.where(first_head, 0.0, x)],
                               axis=0).astype(BF16)

    def shift_pair(i, carry):
        ones_bd = bd_ref[...]
        o = 3 * D_RW
        sts = []
        for n in range(2):
            c0 = pl.multiple_of((2 * i + n) * L, L)
            ebuf[n] = _load_ext(z_ref, c0, L, t_len, 0, RW_ZCOLS)
            cur = ebuf[n, HALO:HALO + L, :]
            prev = ebuf[n, HALO - 1:HALO - 1 + L, :]
            nxt = ebuf[n, HALO + 1:HALO + 1 + L, :]
            zs = cur + mu_ref[0:1, :] * (prev - cur) + mu_ref[1:2, :] * (nxt - cur)
            zs_s[pl.ds(c0, L), :] = zs.astype(BF16)
            kkv = zs[:, D_RW:2 * D_RW] * kkw_ref[...]
            sts.append(dict(c0=c0, kkv=kkv, ss=_head_sum(kkv * kkv, ones_bd),
                            g=_dot(_sigmoid(zs[:, o:o + RW_G_LORA]).astype(BF16), g2_ref[...])))
        for st in sts:
            rows = pl.ds(st['c0'], L)
            kkn_s[rows, :] = (st['kkv'] / jnp.maximum(jnp.sqrt(st['ss']), 1e-12)).astype(BF16)
            g_s[rows, :] = st['g'].astype(g_s.dtype)
        return carry

    lax.fori_loop(0, n_ch // 2, shift_pair, 0)

    def prep(items):
        ones_bd = bd_ref[...]
        o = 3 * D_RW
        sts = []
        for c0, d in items:
            rows = pl.ds(c0, L)
            sts.append(dict(c0=c0, d=d, r=zs_s[rows, 0:D_RW].astype(F32), k=zs_s[rows, D_RW:2 * D_RW].astype(F32),
                            v=zs_s[rows, 2 * D_RW:3 * D_RW].astype(F32), kkn=kkn_s[rows, :].astype(F32),
                            wd=zs_s[rows, o + RW_G_LORA:o + RW_G_LORA + LANES].astype(F32),
                            ad=zs_s[rows, o + RW_G_LORA + LANES:o + RW_G_LORA + 2 * LANES]))
        for st in sts:
            d = st['d']
            tw_hi, tw_lo = _split2(jnp.tanh(st['wd']))
            st['w_lora'] = _dot(tw_hi, w2_ref[d, 0]) + (_dot(tw_hi, w2_ref[d, 1]) + _dot(tw_lo, w2_ref[d, 0]))
            st['a_lora'] = _dot(st['ad'], a2_ref[d])
        for st in sts:
            d = st['d']
            w_log = -_softplus(-(w0_ref[d:d + 1, :] + st['w_lora'])) - 0.5
            st['lw'] = -jnp.exp(w_log)
            st['a'] = _sigmoid(a0_ref[d:d + 1, :] + st['a_lora'])
            st['kd'] = st['k'] * (1.0 + (st['a'] - 1.0) * ka_ref[...])
            st['bon'] = _head_sum(st['r'] * st['kd'] * rk_ref[...], ones_bd, single_pass=True)
            st['cum'] = _dot_exact_lhs(tri_ref[d], st['lw'])
        out = []
        for st in sts:
            d, r, v, kd, kkn, a, cum = st['d'], st['r'], st['v'], st['kd'], st['kkn'], st['a'], st['cum']
            bon_s[d, pl.ds(st['c0'], L), :] = (st['bon'] * v).astype(bon_s.dtype)
            e_pos = jnp.exp(cum)
            e_neg = jnp.exp(-cum)
            rh = r * e_pos
            kh = kd * e_neg
            bh = (kkn * a) * e_neg
            ah = -kkn * jnp.exp(cum - st['lw'])
            last = L - 1 if d == 0 else 0
            gam = e_pos[last:last + 1, :]
            bhg = bh * gam
            khg = kh * gam
            chains = []
            for p in range(RW_PAIRS):
                cols = slice(p * LANES, (p + 1) * LANES)
                chains.append(dict(
                    d=d, p=p, gam=gam[:, cols], v_st=stack(v[:, cols]),
                    lhs=jnp.concatenate([stack(ah[:, cols]), stack(rh[:, cols])], axis=0),
                    rhs=jnp.concatenate([stack(bh[:, cols]), stack(kh[:, cols])], axis=0),
                    bk=jnp.concatenate([stack(bhg[:, cols]), stack(khg[:, cols])], axis=0)))
            out.append(chains)
        return out

    n_levels = int(math.log2(L))

    def state_free(chains):
        ri = lax.broadcasted_iota(jnp.int32, (L2, L2), 0)
        ci = lax.broadcasted_iota(jnp.int32, (L2, L2), 1)
        eye = jnp.where(ri == ci, 1.0, 0.0)
        rl = jnp.where(ri < L, ri, ri - L)
        cl = jnp.where(ci < L, ci, ci - L)
        strict = (cl < rl, cl > rl)
        incl = (cl <= rl, cl >= rl)
        level = ri ^ ci

        for ch in chains:
            ch['sc'] = _dot_nt(ch['lhs'], ch['rhs'])
        for ch in chains:
            sc, d = ch['sc'], ch['d']
            n_ab = jnp.where(strict[d], sc[0:L2, 0:L2], 0.0)
            m_ak = jnp.where(strict[d], sc[0:L2, L2:2 * L2], 0.0).astype(BF16)
            t_rb = jnp.where(incl[d], sc[L2:2 * L2, 0:L2], 0.0)
            t_rk = jnp.where(incl[d], sc[L2:2 * L2, L2:2 * L2], 0.0)
            ch['t_rbk'] = jnp.concatenate([t_rb, t_rk], axis=1).astype(BF16)
            ch['mv'] = _dot(m_ak, ch['v_st'])
            ch['t'] = eye + jnp.where(level == 1, n_ab, 0.0)
            ch['n_lv'] = [jnp.where(lax.shift_right_logical(level, lv) == 1, n_ab, 0.0).astype(BF16)
                          for lv in range(1, n_levels)]
            del ch['sc']
        for lv in range(n_levels - 1):
            for ch in chains:
                ch['tb'] = ch['t'].astype(BF16)
                ch['x'] = _dot(ch['n_lv'][lv], ch['tb']).astype(BF16)
            for ch in chains:
                ch['t'] = ch['t'] + _dot(ch['tb'], ch['x'])

    def state_step(chains, c0s):
        for ch in chains:
            ch['s_prev'] = s_s[ch['d'], ch['p']]
            ch['ar_s'] = _dot_nt(ch['lhs'], ch['s_prev'].astype(BF16))
        for ch in chains:
            u_rhs = ch['ar_s'][0:L2] + ch['mv']
            u_st = _dot(ch['t'].astype(BF16), u_rhs.astype(BF16)).astype(BF16)
            ch['uv'] = jnp.concatenate([u_st, ch['v_st']], axis=0)
        for ch in chains:
            y_st = ch['ar_s'][L2:2 * L2] + _dot(ch['t_rbk'], ch['uv'])
            ch['y'] = y_st[0:L] + y_st[L:L2]
            s_s[ch['d'], ch['p']] = ch['s_prev'] * ch['gam'] + _dot_tn(ch['uv'], ch['bk'])
        for d in range(2):
            yd_s[d, pl.ds(c0s[d], L), :] = jnp.concatenate([ch['y'] for ch in chains if ch['d'] == d], axis=1)

    steps = 2

    def body(i, carry):
        starts = []
        for j in range(steps):
            step = i * steps + j
            starts.append((pl.multiple_of(step * L, L), pl.multiple_of((n_ch - 1 - step) * L, L)))
        prepared = prep([(c0s[d], d) for c0s in starts for d in range(2)])
        groups = [(prepared[2 * j] + prepared[2 * j + 1], starts[j]) for j in range(steps)]
        state_free([ch for chains, _ in groups for ch in chains])
        for chains, c0s in groups:
            state_step(chains, c0s)
        return carry

    lax.fori_loop(0, n_ch // steps, body, 0)

    def epilogue(i, carry):
        c0 = pl.multiple_of(i * L, L)
        ones_bd = bd_ref[...]
        ys = yd_s[0, pl.ds(c0, L), :] + yd_s[1, pl.ds(c0, L), :]
        mu = _head_sum(ys, ones_bd) * (1.0 / RW_HEAD)
        yc = ys - mu
        var = _head_sum(yc * yc, ones_bd) * (1.0 / RW_HEAD)
        yn = yc * lax.rsqrt(var + RW_LN_EPS) * lnw_ref[...] + lnb_ref[...]
        bonus = bon_s[0, pl.ds(c0, L), :].astype(F32) + bon_s[1, pl.ds(c0, L), :].astype(F32)
        out = (yn + bonus) * g_s[pl.ds(c0, L), :].astype(F32)
        y_ref[pl.ds(c0, L), :] = out.astype(y_ref.dtype)
        return carry

    lax.fori_loop(0, n_ch, epilogue, 0, unroll=min(EPILOGUE_UNROLL, n_ch))
    for d in range(2):
        for p in range(RW_PAIRS):
            s_pair = s_s[d, p]
            sfin_ref[d, 2 * p] = s_pair[0:RW_HEAD, 0:RW_HEAD]
            sfin_ref[d, 2 * p + 1] = s_pair[RW_HEAD:, RW_HEAD:]


def _rwkv_mixer(z, s0, p, n_batch, t_len):
    L = SCAN_CHUNK
    c2 = lambda b: (0, 0)
    c3 = lambda b: (0, 0, 0)
    st_spec = pl.BlockSpec((None, 2, RW_HEADS, RW_HEAD, RW_HEAD), lambda b: (b, 0, 0, 0, 0))
    zero_init = s0 is None
    z_block_bytes = t_len * RW_ZCOLS * z.dtype.itemsize
    return pl.pallas_call(
        functools.partial(_rwkv_kernel, t_len=t_len, zero_init=zero_init),
        grid=(n_batch,),
        in_specs=[pl.BlockSpec((t_len, RW_ZCOLS), lambda b: (b, 0),
                               **(dict(pipeline_mode=pl.Buffered(1)) if z_block_bytes > Z_DOUBLE_BUFFER_MAX else {}))]
        + ([] if zero_init else [st_spec]) + [
            pl.BlockSpec((2, RW_ZCOLS), c2),
            pl.BlockSpec((1, D_RW), c2),
            pl.BlockSpec((1, D_RW), c2),
            pl.BlockSpec((1, D_RW), c2),
            pl.BlockSpec((RW_G_LORA, D_RW), c2),
            pl.BlockSpec((2, D_RW), c2),
            pl.BlockSpec((2, 2, LANES, D_RW), lambda b: (0, 0, 0, 0)),
            pl.BlockSpec((2, D_RW), c2),
            pl.BlockSpec((2, LANES, D_RW), c3),
            pl.BlockSpec((1, D_RW), c2),
            pl.BlockSpec((1, D_RW), c2),
            pl.BlockSpec((2, L, L), c3),
            pl.BlockSpec((LANES, LANES), c2),
        ],
        out_specs=[pl.BlockSpec((t_len, D_RW), lambda b: (b, 0)), st_spec],
        out_shape=[
            jax.ShapeDtypeStruct((n_batch * t_len, D_RW), Y_DTYPE),
            jax.ShapeDtypeStruct((n_batch, 2, RW_HEADS, RW_HEAD, RW_HEAD), F32),
        ],
        scratch_shapes=[
            pltpu.VMEM((2, L + 2 * HALO, RW_ZCOLS), F32),
            pltpu.VMEM((2, RW_PAIRS, LANES, LANES), F32),
            pltpu.VMEM((2, t_len, D_RW), F32),
            pltpu.VMEM((2, t_len, D_RW), BF16),
            pltpu.VMEM((t_len, D_RW), BF16),
            pltpu.VMEM((t_len, RW_ZCOLS), BF16),
            pltpu.VMEM((t_len, D_RW), BF16),
        ],
        compiler_params=_ARB1,
    )(z, *(() if zero_init else (s0,)), p['mu'], p['kk'], p['ka'], p['rk'], p['g2'], p['w0'], p['w2'], p['a0'],
      p['a2'], p['ln_w'], p['ln_b'], p['tri'], p['ones_bd'])


def _mlstm_kernel(*refs, t_len, zero_init):
    L = ML_SCAN
    n_ch = t_len // L
    z_ref = refs[0]
    (cw_ref, cb_ref, gb_ref, nw_ref, tri_ref, esel_ref, shift_ref, ones_ref,
     y_ref, cfin_ref, nfin_ref, mfin_ref, c_s, n_s, m_s, hd_s, qk_s, gx_s, gr_s) = refs[1 if zero_init else 4:]
    if zero_init:
        c_s[...] = jnp.zeros(c_s.shape, F32)
        n_s[...] = jnp.zeros(n_s.shape, F32)
        m_s[...] = jnp.zeros(m_s.shape, F32)
    else:
        c0_ref, n0_ref, m0_ref = refs[1:4]
        c_s[...] = c0_ref[...]
        m_s[...] = m0_ref[...]
        for d in range(2):
            for h in range(ML_HEADS):
                n_s[d, h] = jnp.broadcast_to(n0_ref[d, h:h + 1, :], (ML_DK, ML_DK)).T
    q_scale = ML_DK ** -0.5
    g_lo = 4 * D_ML
    n_real = 4 * ML_HEADS

    def conv_chunk(i, carry):
        c0 = pl.multiple_of(i * L, L)
        gts = z_ref[pl.ds(c0, L), g_lo:g_lo + LANES].astype(F32) + gb_ref[...]
        x16 = gts.T[0:n_real, :]
        lf_hi, lf_lo = _split2(-_softplus(-x16))
        lane = lax.broadcasted_iota(jnp.int32, (8, L), 1)
        stats = []
        for d in range(2):
            b16 = _dot_nt(lf_hi, tri_ref[d]) + _dot_nt(lf_lo, tri_ref[d])
            x8 = x16[8 * d:8 * d + 8, :]
            b8 = b16[8 * d:8 * d + 8, :]
            g8 = pltpu.roll(x8, ML_HEADS, axis=0) - b8
            stats.append([g8, b8, g8])
        step = 1
        while step < L:
            for d in range(2):
                cm8 = stats[d][2]
                if d == 0:
                    sh = jnp.where(lane >= step, pltpu.roll(cm8, step, axis=1), -jnp.inf)
                else:
                    sh = jnp.where(lane < L - step, pltpu.roll(cm8, L - step, axis=1), -jnp.inf)
                stats[d][2] = jnp.maximum(cm8, sh)
            step *= 2

        ext = _load_ext(z_ref, c0, L, t_len, 0, 2 * D_ML, raw=True)
        taps = _dot(shift_ref[...], ext)
        qk = cb_ref[...] + ext[HALO:HALO + L].astype(F32) * cw_ref[CONV_LEFT:CONV_LEFT + 1, :]
        for n, j in enumerate(jj for jj in range(CONV_W) if jj != CONV_LEFT):
            qk = qk + taps[n * L:(n + 1) * L] * cw_ref[j:j + 1, :]
        qk = _silu(qk)
        qk_s[pl.ds(c0, L), 0:D_ML] = (qk[:, :D_ML] * q_scale).astype(BF16)
        qk_s[pl.ds(c0, L), D_ML:2 * D_ML] = qk[:, D_ML:].astype(BF16)

        for d, (g8, b8, cm8) in enumerate(stats):
            x32 = jnp.concatenate([g8, b8, cm8, jnp.zeros((8, L), F32)], axis=0)
            hi = x32.astype(BF16)
            r1 = x32 - hi.astype(F32)
            mid = r1.astype(BF16)
            lo = (r1 - mid.astype(F32)).astype(BF16)
            gx_s[d, i] = jnp.concatenate([hi, mid, lo], axis=0)
            gr_s[d, i] = g8
        return carry

    lax.fori_loop(0, n_ch, conv_chunk, 0, unroll=min(ML_UNROLL, n_ch))

    def prep(ci, d):
        c0 = pl.multiple_of(ci * L, L)
        q = qk_s[pl.ds(c0, L), 0:D_ML]
        k = qk_s[pl.ds(c0, L), D_ML:2 * D_ML]
        v = z_ref[pl.ds(c0, L), 2 * D_ML:3 * D_ML]
        g8 = gr_s[d, ci]
        cb = _dot_tn(gx_s[d, ci], esel_ref[...])
        ri = lax.broadcasted_iota(jnp.int32, (L, L), 0)
        ci = lax.broadcasted_iota(jnp.int32, (L, L), 1)
        causal = (ci <= ri) if d == 0 else (ci >= ri)
        last = L - 1 if d == 0 else 0
        ones = ones_ref[...]
        chains = []
        for h in range(ML_HEADS):
            cols = slice(h * ML_DK, (h + 1) * ML_DK)
            chains.append(dict(
                d=d, h=h, last=last, causal=causal,
                g_row=g8[ML_HEADS + h:ML_HEADS + h + 1, :],
                g_b=cb[:, cols], b_b=cb[:, D_ML + h * ML_DK:D_ML + (h + 1) * ML_DK],
                cm_b=cb[:, 2 * D_ML + h * ML_DK:2 * D_ML + (h + 1) * ML_DK],
                qb=q[:, cols], kb=k[:, cols],
                v=v[:, cols].astype(F32), v_ones=jnp.concatenate([v[:, cols].astype(BF16), ones], axis=1)))
        return chains

    def body(i, carry):
        c0s = (pl.multiple_of(i * L, L), pl.multiple_of((n_ch - 1 - i) * L, L))
        chains = prep(i, 0) + prep(n_ch - 1 - i, 1)
        for ch in chains:
            d, h = ch['d'], ch['h']
            ch['qk'] = _dot_nt(ch['qb'], ch['kb'])
            ch['c_prev'] = c_s[d, h]
            ch['n_prev'] = n_s[d, h]
            ch['m_prev'] = m_s[d, h:h + 1, :]
            ch['qcn'] = _dot(ch['qb'], jnp.concatenate([ch['c_prev'].astype(BF16),
                                                        ch['n_prev'].astype(BF16)], axis=1))
        for ch in chains:
            mu = jnp.maximum(ch['m_prev'], ch['cm_b'])
            ch['mu'] = mu
            ch['s'] = ch['qk'] * jnp.exp(jnp.where(ch['causal'], ch['g_row'] - mu, -jnp.inf))
            ch['w_inter'] = jnp.exp(ch['m_prev'] - mu)
        for ch in chains:
            last = ch['last']
            ch['sv'] = _dot(ch['s'].astype(BF16), ch['v_ones'])
            mu_last = ch['mu'][last:last + 1, :]
            ch['m_new'] = ch['b_b'][last:last + 1, :] + mu_last
            ch['dec'] = jnp.exp(ch['m_prev'] - mu_last)
            w_s = jnp.exp(ch['g_b'] - mu_last)
            ch['upd'] = _dot_tn(ch['kb'], jnp.concatenate([(w_s * ch['v']).astype(BF16),
                                                           w_s.astype(BF16)], axis=1))
        for ch in chains:
            d, h = ch['d'], ch['h']
            num = ch['sv'][:, :ML_DK] + ch['w_inter'] * ch['qcn'][:, :ML_DK]
            den = ch['sv'][:, ML_DK:] + ch['w_inter'] * ch['qcn'][:, ML_DK:]
            ch['out'] = num / jnp.maximum(jnp.abs(den), jnp.exp(-(ch['b_b'] + ch['mu'])))
            c_s[d, h] = ch['dec'] * ch['c_prev'] + ch['upd'][:, :ML_DK]
            n_s[d, h] = ch['dec'] * ch['n_prev'] + ch['upd'][:, ML_DK:]
            m_s[d, h:h + 1, :] = ch['m_new']
        for d in range(2):
            hd_s[d, pl.ds(c0s[d], L), :] = jnp.concatenate([ch['out'] for ch in chains if ch['d'] == d], axis=1)
        return carry

    lax.fori_loop(0, n_ch, body, 0, unroll=min(ML_UNROLL, n_ch))

    def epilogue(i, carry):
        c0 = pl.multiple_of(i * L, L)
        ones = ones_ref[...]
        hs = hd_s[0, pl.ds(c0, L), :] + hd_s[1, pl.ds(c0, L), :]
        o = z_ref[pl.ds(c0, L), 3 * D_ML:4 * D_ML].astype(F32)
        mu = _head_sum(hs, ones) * (1.0 / ML_DK)
        xc = hs - mu
        var = _head_sum(xc * xc, ones) * (1.0 / ML_DK)
        y = xc * lax.rsqrt(var + NORM_EPS) * nw_ref[...] * _sigmoid(o)
        y_ref[pl.ds(c0, L), :] = y.astype(y_ref.dtype)
        return carry

    lax.fori_loop(0, n_ch, epilogue, 0, unroll=min(EPILOGUE_UNROLL, n_ch))
    cfin_ref[...] = c_s[...]
    mfin_ref[...] = m_s[...]
    for d in range(2):
        for h in range(ML_HEADS):
            nfin_ref[d, h:h + 1, :] = n_s[d, h].T[0:1, :]


def _mlstm_mixer(z, states, p, n_batch, t_len):
    assert z.dtype == BF16
    L = ML_SCAN
    c2 = lambda b: (0, 0)
    c3 = lambda b: (0, 0, 0)
    c_spec = pl.BlockSpec((None, 2, ML_HEADS, ML_DK, ML_DK), lambda b: (b, 0, 0, 0, 0))
    m_spec = pl.BlockSpec((None, 2, ML_HEADS, ML_DK), lambda b: (b, 0, 0, 0))
    zero_init = states is None
    state_specs = [] if zero_init else [c_spec, m_spec, m_spec]
    return pl.pallas_call(
        functools.partial(_mlstm_kernel, t_len=t_len, zero_init=zero_init),
        grid=(n_batch,),
        in_specs=[pl.BlockSpec((t_len, ML_ZCOLS), lambda b: (b, 0))] + state_specs + [
            pl.BlockSpec((CONV_W, 2 * D_ML), c2),
            pl.BlockSpec((1, 2 * D_ML), c2),
            pl.BlockSpec((1, LANES), c2),
            pl.BlockSpec((1, D_ML), c2),
            pl.BlockSpec((2, L, L), c3),
            pl.BlockSpec(p['esel'].shape, c2),
            pl.BlockSpec(p['shift'].shape, c2),
            pl.BlockSpec((L, LANES), c2),
        ],
        out_specs=[pl.BlockSpec((t_len, D_ML), lambda b: (b, 0)), c_spec, m_spec, m_spec],
        out_shape=[
            jax.ShapeDtypeStruct((n_batch * t_len, D_ML), Y_DTYPE),
            jax.ShapeDtypeStruct((n_batch, 2, ML_HEADS, ML_DK, ML_DK), F32),
            jax.ShapeDtypeStruct((n_batch, 2, ML_HEADS, ML_DK), F32),
            jax.ShapeDtypeStruct((n_batch, 2, ML_HEADS, ML_DK), F32),
        ],
        scratch_shapes=[
            pltpu.VMEM((2, ML_HEADS, ML_DK, ML_DK), F32),
            pltpu.VMEM((2, ML_HEADS, ML_DK, ML_DK), F32),
            pltpu.VMEM((2, ML_HEADS, ML_DK), F32),
            pltpu.VMEM((2, t_len, D_ML), F32),
            pltpu.VMEM((t_len, 2 * D_ML), BF16),
            pltpu.VMEM((2, t_len // L, 96, L), BF16),
            pltpu.VMEM((2, t_len // L, 8, L), F32),
        ],
        compiler_params=_ARB1,
    )(z, *(() if zero_init else states), p['conv_w'], p['conv_b'], p['gate_b'], p['norm_w'], p['tri'], p['esel'],
      p['shift'], p['ones'])


def _conv_shift(n):
    taps = jnp.array([j for j in range(CONV_W) if j != CONV_LEFT])
    t = jnp.arange((CONV_W - 1) * n)
    src = HALO + t % n + taps[t // n] - CONV_LEFT
    return (src[:, None] == jnp.arange(n + 2 * HALO)[None, :]).astype(BF16)


def _mlstm_consts():
    L = ML_SCAN
    lower = jnp.tril(jnp.ones((L, L), F32))
    tri = jnp.stack([lower, lower.T]).astype(BF16)
    rows = jnp.arange(32)
    blk = jnp.arange(3 * D_ML) // ML_DK
    src_row = 8 * (blk // ML_HEADS) + ML_HEADS + blk % ML_HEADS
    esel = (rows[:, None] == src_row[None, :]).astype(BF16)
    esel = jnp.concatenate([esel, esel, esel], axis=0)
    return dict(tri=tri, esel=esel, shift=_conv_shift(L), ones=jnp.ones((L, LANES), BF16))


def _block_diag(w):
    n, c, d = w.shape
    return jnp.einsum('ncd,nm->ncmd', w, jnp.eye(n, dtype=w.dtype)).reshape(n * c, n * d)


def _scan_tri():
    L = SCAN_CHUNK
    lower = jnp.tril(jnp.ones((L, L), F32))
    return jnp.stack([lower, lower.T]).astype(BF16)


def _prep_even(ev_w_in, pool_w, pool_scale, rg_conv_w, rg_conv_b, rg_wa, rg_ba, rg_wx, rg_bx, rg_lam, ev_w_out, j):
    wg = jnp.concatenate([_block_diag(rg_wa[j, 0]), _block_diag(rg_wx[j, 0]),
                          _block_diag(rg_wa[j, 1]), _block_diag(rg_wx[j, 1])], axis=1)
    bg = jnp.concatenate([rg_ba[j, 0], rg_bx[j, 0], rg_ba[j, 1], rg_bx[j, 1]]).reshape(1, 4 * D_RG)
    return dict(
        w_in=[ev_w_in[j].astype(BF16)],
        w_out=ev_w_out[j].astype(BF16),
        pool_w=pool_w[j].astype(BF16),
        pool_scale=pool_scale[j].reshape(1, D_POOL),
        conv_w=rg_conv_w[j],
        conv_b=rg_conv_b[j].reshape(1, D_RG),
        wg=wg.astype(BF16),
        bg=bg,
        lam=rg_lam[j],
    )


def _pad_cols(w, cuts, widths):
    pieces = jnp.split(w, cuts, axis=-1)
    out = []
    for piece, width in zip(pieces, widths):
        extra = width - piece.shape[-1]
        if extra:
            piece = jnp.pad(piece, [(0, 0)] * (piece.ndim - 1) + [(0, extra)])
        out.append(piece)
    return jnp.concatenate(out, axis=-1)


def _prep_odd(od_w_in, rw_mu, rw_w0, rw_w2, rw_a0, rw_a2, rw_kk, rw_ka, rw_rk, rw_g2, rw_ln_w, rw_ln_b,
              ml_conv_w, ml_conv_b, ml_bi, ml_bf, ml_norm_w, od_w_out, j):
    rw_cols = 3 * D_RW + RW_G_LORA + RW_W_LORA + RW_A_LORA
    base = 3 * D_RW + RW_G_LORA
    cuts = [base, base + RW_W_LORA]
    widths = [base, LANES, LANES]
    w_rw = _pad_cols(od_w_in[j][:, :rw_cols], cuts, widths)
    w_ml = _pad_cols(od_w_in[j][:, rw_cols:], [4 * D_ML], [4 * D_ML, LANES])
    pad_rows = lambda w: jnp.pad(w, ((0, 0), (0, LANES - w.shape[1]), (0, 0)))
    gate_b = jnp.concatenate([ml_bi[j, 0], ml_bf[j, 0], ml_bi[j, 1], ml_bf[j, 1]])
    gate_b = jnp.pad(gate_b, (0, LANES - gate_b.shape[0])).reshape(1, LANES)
    hm = jnp.arange(LANES) // RW_HEAD
    rw = dict(
        mu=_pad_cols(rw_mu[j], cuts, widths),
        kk=rw_kk[j].reshape(1, D_RW), ka=rw_ka[j].reshape(1, D_RW), rk=rw_rk[j].reshape(1, D_RW),
        g2=rw_g2[j].astype(BF16),
        w0=rw_w0[j], w2=jnp.stack(_split2(pad_rows(rw_w2[j])), axis=1),
        a0=rw_a0[j], a2=pad_rows(rw_a2[j]).astype(BF16),
        ln_w=rw_ln_w[j].reshape(1, D_RW), ln_b=rw_ln_b[j].reshape(1, D_RW),
        tri=_scan_tri(),
        ones_bd=(hm[:, None] == hm[None, :]).astype(BF16),
    )
    ml = dict(
        conv_w=ml_conv_w[j], conv_b=ml_conv_b[j].reshape(1, 2 * D_ML),
        gate_b=gate_b, norm_w=ml_norm_w[j].reshape(1, D_ML), **_mlstm_consts(),
    )
    return dict(w_in=[w_rw.astype(BF16), w_ml.astype(BF16)], w_out=od_w_out[j].astype(BF16), rw=rw, ml=ml)


def _trunk(x, mod, rows_per_mod, mod_base, grid, states, params, n_batch, t_len):
    ev, od, dense = params
    rg0, rw0, ml0 = states
    x = x.reshape(n_batch * t_len, D_MODEL)
    spec = (rows_per_mod, mod_base)

    (z,) = _in_proj(x, mod[0], *spec, dense['norm1_w'][0], ev['w_in'])
    y_ev, rg_f = _even_mixer(z, rg0, ev, n_batch, t_len, grid)
    x = _out_mlp([y_ev], x, mod[0], *spec, dense['norm2_w'][0], ev['w_out'],
                 dense['mlp_w1'], dense['mlp_w2'], 0, dense['final_norm_w'], False)

    z_rw, z_ml = _in_proj(x, mod[1], *spec, dense['norm1_w'][1], od['w_in'])
    y_rw, s_f = _rwkv_mixer(z_rw, rw0, od['rw'], n_batch, t_len)
    y_ml, c_f, n_f, m_f = _mlstm_mixer(z_ml, ml0, od['ml'], n_batch, t_len)
    y = _out_mlp([y_rw, y_ml], x, mod[1], *spec, dense['norm2_w'][1], od['w_out'],
                 dense['mlp_w1'], dense['mlp_w2'], 1, dense['final_norm_w'], True)
    return y.reshape(n_batch, t_len, D_MODEL), (rg_f, s_f, c_f, n_f, m_f)


def kernel(x_prompt, x_sample, state_rglru, state_rwkv, state_mlstm_C, state_mlstm_n, state_mlstm_m,
           c, c_ctx, norm1_w, norm2_w, w_mod, b_mod, mlp_w1, mlp_w2, final_norm_w,
           ev_w_in, pool_w, pool_scale, rg_conv_w, rg_conv_b, rg_wa, rg_ba, rg_wx, rg_bx, rg_lam, ev_w_out,
           od_w_in, rw_mu, rw_w0, rw_w2, rw_a0, rw_a2, rw_kk, rw_ka, rw_rk, rw_g2, rw_ln_w, rw_ln_b,
           ml_conv_w, ml_conv_b, ml_bi, ml_bf, ml_norm_w, od_w_out):
    bp, tp, _ = x_prompt.shape
    bs, ts, _ = x_sample.shape

    assert bs + 1 <= N_COND_ROWS and ts % ROW_BLOCK == 0 and (bp * tp) % ROW_BLOCK == 0
    cond = jnp.concatenate([c_ctx[None, :], c, jnp.zeros((N_COND_ROWS - 1 - bs, D_MODEL), F32)], axis=0)
    mod = _modulation(cond, w_mod, b_mod)

    ev = _prep_even(ev_w_in, pool_w, pool_scale, rg_conv_w, rg_conv_b, rg_wa, rg_ba, rg_wx, rg_bx, rg_lam,
                    ev_w_out, 0)
    od = _prep_odd(od_w_in, rw_mu, rw_w0, rw_w2, rw_a0, rw_a2, rw_kk, rw_ka, rw_rk, rw_g2, rw_ln_w, rw_ln_b,
                   ml_conv_w, ml_conv_b, ml_bi, ml_bf, ml_norm_w, od_w_out, 0)
    dense = dict(norm1_w=norm1_w, norm2_w=norm2_w, mlp_w1=mlp_w1.astype(BF16), mlp_w2=mlp_w2.astype(BF16),
                 final_norm_w=final_norm_w)
    params = (ev, od, dense)

    zero_states = (jnp.zeros((bp, 2, D_RG), F32), None, None)
    y_prompt, (rg_f, s_f, c_f, n_f, m_f) = _trunk(
        x_prompt, mod, bp * tp, 0, False, zero_states, params, bp, tp)

    sample_states = (
        state_rglru[:, 0],
        state_rwkv[:, 0],
        (state_mlstm_C[:, 0], state_mlstm_n[:, 0],
         jnp.broadcast_to(state_mlstm_m[:, 0][..., None], (bs, 2, ML_HEADS, ML_DK))),
    )
    y_sample, _ = _trunk(x_sample, mod, ts, 1, True, sample_states, params, bs, ts)

    return (y_prompt, y_sample,
            rg_f[:, None],
            s_f[:, None],
            c_f[:, None],
            n_f[:, None],
            m_f[:, None, :, :, 0])
```

```python
import functools
import math

import jax
import jax.numpy as jnp
from jax import lax
from jax.experimental import pallas as pl
from jax.experimental.pallas import tpu as pltpu

F32 = jnp.float32
BF16 = jnp.bfloat16

D_MODEL = 1024
DEPTH = 2
GRID_W = 64
D_FF = 4 * D_MODEL
NORM_EPS = 1e-6
N_MOD = 6

D_POOL = D_MODEL // 2
POOL_WINDOWS = (2, 4, 8, 16)
POOL_GW = D_POOL // len(POOL_WINDOWS)
D_RG = D_MODEL // 2
RG_BLOCKS = 8
RG_C = 8.0
CONV_W = 4
CONV_LEFT = 2
EV_COLS = D_POOL + 2 * D_RG

D_RW = D_MODEL // 2
RW_HEAD = 64
RW_HEADS = D_RW // RW_HEAD
RW_PAIRS = RW_HEADS // 2
RW_W_LORA = 64
RW_A_LORA = 64
RW_G_LORA = 128
RW_LN_EPS = 64e-5
RW_ZCOLS = 3 * D_RW + RW_G_LORA + 128 + 128
D_ML = D_MODEL // 2
ML_HEADS = 4
ML_DK = D_ML // ML_HEADS
ML_ZCOLS = 4 * D_ML + 128

LANES = 128
HALO = 16
Z_DTYPE = BF16
Y_DTYPE = BF16
ROW_BLOCK = 512
SCAN_CHUNK = 64
ML_SCAN = 128
RG_CHUNK = 256
POOL_SEGS_PER_TRIP = 8
ML_UNROLL = 4
EPILOGUE_UNROLL = 8
MOD_COL_BLOCK = 1536
N_COND_ROWS = 16
Z_DOUBLE_BUFFER_MAX = 8 * 1024 * 1024
VMEM_LIMIT = 56 * 1024 * 1024

_ARB1 = pltpu.CompilerParams(dimension_semantics=("arbitrary",), vmem_limit_bytes=VMEM_LIMIT)
_ARB2 = pltpu.CompilerParams(dimension_semantics=("arbitrary", "arbitrary"), vmem_limit_bytes=VMEM_LIMIT)


def _dot(a, b):
    return jnp.dot(a, b, preferred_element_type=F32)


def _dot_nt(a, b):
    return lax.dot_general(a, b, (((1,), (1,)), ((), ())), preferred_element_type=F32)


def _dot_tn(a, b):
    return lax.dot_general(a, b, (((0,), (0,)), ((), ())), preferred_element_type=F32)


def _split2(x):
    hi = x.astype(BF16)
    lo = (x - hi.astype(F32)).astype(BF16)
    return hi, lo


def _dot_f32(a, b):
    ah, al = _split2(a)
    bh, bl = _split2(b)
    return _dot(ah, bh) + (_dot(ah, bl) + _dot(al, bh))


def _dot_exact_lhs(m_bf16, x):
    hi, lo = _split2(x)
    return _dot(m_bf16, hi) + _dot(m_bf16, lo)


def _dot_exact_rhs(x, m_bf16):
    hi, lo = _split2(x)
    return _dot(hi, m_bf16) + _dot(lo, m_bf16)


def _softplus(x):
    return jnp.maximum(x, 0.0) + jnp.log1p(jnp.exp(-jnp.abs(x)))


def _sigmoid(x):
    return jax.nn.sigmoid(x)


def _silu(x):
    return x * jax.nn.sigmoid(x)


def _gelu_tanh(x):
    c = math.sqrt(2.0 / math.pi)
    return x * (0.5 * (1.0 + jnp.tanh(c * (x + 0.044715 * (x * x * x)))))


def _rms_norm(x, w):
    ms = jnp.mean(x * x, axis=-1, keepdims=True)
    return x * lax.rsqrt(ms + NORM_EPS) * w


def _load_ext(z_ref, c0, n, t_len, lo, hi, raw=False):
    dt = z_ref.dtype if raw else F32
    cur = z_ref[pl.ds(c0, n), lo:hi].astype(dt)
    pb = pl.multiple_of(jnp.maximum(c0 - HALO, 0), HALO)
    pa = pl.multiple_of(jnp.minimum(c0 + n, t_len - HALO), HALO)
    before = z_ref[pl.ds(pb, HALO), lo:hi].astype(dt)
    after = z_ref[pl.ds(pa, HALO), lo:hi].astype(dt)
    before = jnp.where(c0 > 0, before, jnp.zeros_like(before))
    after = jnp.where(c0 + n < t_len, after, jnp.zeros_like(after))
    return jnp.concatenate([before, cur, after], axis=0)


def _head_sum(x, ones_bd, single_pass=False):
    n = x.shape[0]
    groups = x.shape[1] // LANES
    xs = jnp.concatenate([x[:, g * LANES:(g + 1) * LANES] for g in range(groups)], axis=0)
    s = _dot(xs.astype(BF16), ones_bd) if single_pass else _dot_exact_rhs(xs, ones_bd)
    return jnp.concatenate([s[g * n:(g + 1) * n] for g in range(groups)], axis=1)


def _mod_kernel(c_ref, w_ref, b_ref, o_ref):
    s = _silu(c_ref[...])
    o_ref[...] = _dot_f32(s, w_ref[...]) + b_ref[...]


def _modulation(cond, w_mod, b_mod):
    rows = cond.shape[0]
    tn = MOD_COL_BLOCK
    out = pl.pallas_call(
        _mod_kernel,
        grid=(DEPTH, N_MOD * D_MODEL // tn),
        in_specs=[
            pl.BlockSpec((rows, D_MODEL), lambda l, j: (0, 0)),
            pl.BlockSpec((None, D_MODEL, tn), lambda l, j: (l, 0, j)),
            pl.BlockSpec((None, 1, tn), lambda l, j: (l, 0, j)),
        ],
        out_specs=pl.BlockSpec((None, rows, tn), lambda l, j: (l, 0, j)),
        out_shape=jax.ShapeDtypeStruct((DEPTH, rows, N_MOD * D_MODEL), F32),
        compiler_params=_ARB2,
    )(cond, w_mod, b_mod.reshape(DEPTH, 1, N_MOD * D_MODEL))
    return out.reshape(DEPTH, rows, N_MOD, D_MODEL)


def _inproj_kernel(*refs, n_out, col_chunk):
    x_ref, mod_ref, nw_ref = refs[:3]
    w_refs = refs[3:3 + n_out]
    z_refs = refs[3 + n_out:]
    h = _rms_norm(x_ref[...], nw_ref[...])
    h = h * (1.0 + mod_ref[1:2, :]) + mod_ref[0:1, :]
    hb = h.astype(BF16)
    for w_ref, z_ref in zip(w_refs, z_refs):
        n = w_ref.shape[1]
        for c in range(0, n, col_chunk):
            e = min(c + col_chunk, n)
            z_ref[:, c:e] = _dot(hb, w_ref[:, c:e]).astype(z_ref.dtype)


def _in_proj(x, mod, rows_per_mod, mod_base, norm_w, weights):
    m = x.shape[0]
    tm = ROW_BLOCK
    n_out = len(weights)
    const = lambda i: (0, 0)
    in_specs = [
        pl.BlockSpec((tm, D_MODEL), lambda i: (i, 0)),
        pl.BlockSpec((None, N_MOD, D_MODEL), lambda i: (mod_base + (i * tm) // rows_per_mod, 0, 0)),
        pl.BlockSpec((1, D_MODEL), const),
    ] + [pl.BlockSpec(w.shape, const, pipeline_mode=pl.Buffered(1)) for w in weights]
    out_specs = [pl.BlockSpec((tm, w.shape[1]), lambda i: (i, 0)) for w in weights]
    out_shape = [jax.ShapeDtypeStruct((m, w.shape[1]), Z_DTYPE) for w in weights]
    return pl.pallas_call(
        functools.partial(_inproj_kernel, n_out=n_out, col_chunk=512),
        grid=(m // tm,),
        in_specs=in_specs,
        out_specs=out_specs,
        out_shape=out_shape,
        compiler_params=_ARB1,
    )(x, mod, norm_w.reshape(1, D_MODEL), *weights)


def _outmlp_kernel(*refs, n_y, final, ff_chunk):
    y_refs = refs[:n_y]
    x_ref, mod_ref, nw_ref, wo_ref, w1_ref, w2_ref, fw_ref, o_ref = refs[n_y:]
    y = None
    row = 0
    for y_ref in y_refs:
        k = y_ref.shape[1]
        part = _dot(y_ref[...], wo_ref[row:row + k, :])
        y = part if y is None else y + part
        row += k
    x = x_ref[...] + mod_ref[2:3, :] * y
    h = _rms_norm(x, nw_ref[...]) * (1.0 + mod_ref[4:5, :]) + mod_ref[3:4, :]
    hb = h.astype(BF16)
    acc = None
    for c in range(0, D_FF, ff_chunk):
        u = jnp.maximum(_dot(hb, w1_ref[:, c:c + ff_chunk]), 0.0)
        part = _dot((u * u).astype(BF16), w2_ref[c:c + ff_chunk, :])
        acc = part if acc is None else acc + part
    x = x + mod_ref[5:6, :] * acc
    if final:
        x = _rms_norm(x, fw_ref[...])
    o_ref[...] = x


def _out_mlp(ys, x, mod, rows_per_mod, mod_base, norm_w, w_out, w1, w2, layer, final_w, final):
    m = x.shape[0]
    tm = ROW_BLOCK
    const = lambda i: (0, 0)
    single = dict(pipeline_mode=pl.Buffered(1))
    pick = lambda i: (layer, 0, 0)
    in_specs = [pl.BlockSpec((tm, y.shape[1]), lambda i: (i, 0)) for y in ys] + [
        pl.BlockSpec((tm, D_MODEL), lambda i: (i, 0)),
        pl.BlockSpec((None, N_MOD, D_MODEL), lambda i: (mod_base + (i * tm) // rows_per_mod, 0, 0)),
        pl.BlockSpec((1, D_MODEL), const),
        pl.BlockSpec(w_out.shape, const, **single),
        pl.BlockSpec((None,) + w1.shape[1:], pick, **single),
        pl.BlockSpec((None,) + w2.shape[1:], pick, **single),
        pl.BlockSpec((1, D_MODEL), const),
    ]
    return pl.pallas_call(
        functools.partial(_outmlp_kernel, n_y=len(ys), final=final, ff_chunk=1024),
        grid=(m // tm,),
        in_specs=in_specs,
        out_specs=pl.BlockSpec((tm, D_MODEL), lambda i: (i, 0)),
        out_shape=jax.ShapeDtypeStruct((m, D_MODEL), F32),
        compiler_params=_ARB1,
    )(*ys, x, mod, norm_w.reshape(1, D_MODEL), w_out, w1, w2, final_w.reshape(1, D_MODEL))


def _even_kernel(z_ref, h0_ref, pw_ref, ps_ref, cw_ref, cb_ref, wg_ref, bg_ref, lam_ref, band_ref,
                 y_ref, hfin_ref, cbuf, a_s, b_s, h_s, hf_s, *, t_len, grid):
    seg = GRID_W if grid else t_len
    n_seg = t_len // seg
    n_groups = len(POOL_WINDOWS)

    pos = lax.broadcasted_iota(jnp.int32, (seg, POOL_GW), 0)
    offs = [(w // 2, w - 1 - w // 2) for w in POOL_WINDOWS]
    gcols = [slice(g * POOL_GW, (g + 1) * POOL_GW) for g in range(n_groups)]

    def grid_row(rr, cols):
        valid = jnp.logical_and(rr >= 0, rr < n_seg)
        src = pl.multiple_of(jnp.clip(rr, 0, n_seg - 1) * seg, seg)
        return jnp.where(valid, z_ref[pl.ds(src, seg), cols].astype(F32), 0.0)

    def pool_rows(rows, run):
        items = []
        for r in rows:
            base = r * seg if isinstance(r, int) else pl.multiple_of(r * seg, seg)
            new_run = []
            for g, (lo_off, hi_off) in enumerate(offs):
                cols = gcols[g]
                if grid:
                    s1 = run[g] + grid_row(r + hi_off, cols) - grid_row(r - lo_off - 1, cols)
                    new_run.append(s1)
                    cnt_r = jnp.minimum(r + hi_off, n_seg - 1) - jnp.maximum(r - lo_off, 0) + 1
                    m1 = s1 / cnt_r.astype(F32)
                    hi = m1.astype(BF16)
                    r1 = m1 - hi.astype(F32)
                    mid = r1.astype(BF16)
                    parts = jnp.concatenate([hi, mid, (r1 - mid.astype(F32)).astype(BF16)], axis=0)
                else:
                    parts = z_ref[pl.ds(base, seg), cols]
                items.append(dict(g=g, base=base, parts=parts))
            run = new_run
        for it in items:
            it['s2'] = _dot(band_ref[it['g']], it['parts'])
        for it in items:
            lo_off, hi_off = offs[it['g']]
            cnt_c = jnp.minimum(pos + hi_off, seg - 1) - jnp.maximum(pos - lo_off, 0) + 1
            xg = z_ref[pl.ds(it['base'], seg), gcols[it['g']]].astype(F32)
            it['d'] = (it['s2'] / cnt_c.astype(F32) - xg).astype(BF16)
        for it in items:
            it['y'] = _dot(it['d'], pw_ref[it['g']])
        for it in items:
            cols = gcols[it['g']]
            y_ref[pl.ds(it['base'], seg), cols] = (it['y'] * ps_ref[:, cols]).astype(y_ref.dtype)
        return tuple(run) if grid else None

    if grid:
        init = []
        for g, (lo_off, hi_off) in enumerate(offs):
            s1 = jnp.zeros((seg, POOL_GW), F32)
            for o in range(min(hi_off, n_seg)):
                s1 = s1 + z_ref[o * seg:(o + 1) * seg, gcols[g]].astype(F32)
            init.append(s1)
        per_trip = POOL_SEGS_PER_TRIP
        lax.fori_loop(0, n_seg // per_trip,
                      lambda i, run: pool_rows([i * per_trip + j for j in range(per_trip)], run), tuple(init))
    else:
        pool_rows([0], None)

    ch = RG_CHUNK
    n_ch = t_len // ch
    rg_lo, rg_hi = D_POOL, D_POOL + D_RG

    def gates(c0, d):
        cbuf[...] = _load_ext(z_ref, c0, ch, t_len, rg_lo, rg_hi)
        xc = cb_ref[...]
        for j in range(CONV_W):
            off = HALO - CONV_LEFT + j
            xc = xc + cbuf[off:off + ch, :] * cw_ref[j:j + 1, :]
        g = _dot(xc.astype(BF16), wg_ref[:, d * 2 * D_RG:(d + 1) * 2 * D_RG]) \
            + bg_ref[:, d * 2 * D_RG:(d + 1) * 2 * D_RG]
        r = _sigmoid(g[:, :D_RG])
        i = _sigmoid(g[:, D_RG:])
        log_a = (-RG_C) * r * _softplus(-lam_ref[d:d + 1, :])
        a = jnp.exp(log_a)
        a_s[...] = a
        one_minus_a2 = jnp.tanh(-log_a) * (a * a + 1.0)
        root = jnp.where(one_minus_a2 > 0.0, one_minus_a2 * lax.rsqrt(one_minus_a2), 0.0)
        b_s[...] = root * (i * xc)

    def scan_rows(h, reverse):
        def row(t, h):
            tt = (ch - 1 - t) if reverse else t
            h = a_s[pl.ds(tt, 1), :] * h + b_s[pl.ds(tt, 1), :]
            h_s[pl.ds(tt, 1), :] = h
            return h
        return lax.fori_loop(0, ch, row, h, unroll=8)

    def fwd_body(i, h):
        c0 = pl.multiple_of(i * ch, ch)
        gates(c0, 0)
        h = scan_rows(h, False)
        hf_s[pl.ds(c0, ch), :] = h_s[...]
        return h

    h_fwd = lax.fori_loop(0, n_ch, fwd_body, h0_ref[0:1, :])

    def bwd_body(i, h):
        c0 = pl.multiple_of((n_ch - 1 - i) * ch, ch)
        gates(c0, 1)
        h = scan_rows(h, True)
        u_gate = z_ref[pl.ds(c0, ch), rg_hi:rg_hi + D_RG].astype(F32)
        y = (hf_s[pl.ds(c0, ch), :] + h_s[...]) * _gelu_tanh(u_gate)
        y_ref[pl.ds(c0, ch), D_POOL:D_POOL + D_RG] = y.astype(y_ref.dtype)
        return h

    h_bwd = lax.fori_loop(0, n_ch, bwd_body, h0_ref[1:2, :])
    hfin_ref[0:1, :] = h_fwd
    hfin_ref[1:2, :] = h_bwd


def _even_mixer(z, h0, p, n_batch, t_len, grid):
    assert z.dtype == BF16
    seg = GRID_W if grid else t_len
    const2 = lambda b: (0, 0)
    const3 = lambda b: (0, 0, 0)
    idx = jnp.arange(seg)
    delta = idx[None, :] - idx[:, None]
    band = jnp.stack([jnp.logical_and(delta >= -(w // 2), delta <= w - 1 - w // 2) for w in POOL_WINDOWS])
    band = jnp.tile(band.astype(BF16), (1, 1, 3 if grid else 1))
    return pl.pallas_call(
        functools.partial(_even_kernel, t_len=t_len, grid=grid),
        grid=(n_batch,),
        in_specs=[
            pl.BlockSpec((t_len, EV_COLS), lambda b: (b, 0)),
            pl.BlockSpec((None, 2, D_RG), lambda b: (b, 0, 0)),
            pl.BlockSpec(p['pool_w'].shape, const3),
            pl.BlockSpec((1, D_POOL), const2),
            pl.BlockSpec((CONV_W, D_RG), const2),
            pl.BlockSpec((1, D_RG), const2),
            pl.BlockSpec((D_RG, 4 * D_RG), const2),
            pl.BlockSpec((1, 4 * D_RG), const2),
            pl.BlockSpec((2, D_RG), const2),
            pl.BlockSpec(band.shape, const3),
        ],
        out_specs=[
            pl.BlockSpec((t_len, D_POOL + D_RG), lambda b: (b, 0)),
            pl.BlockSpec((None, 2, D_RG), lambda b: (b, 0, 0)),
        ],
        out_shape=[
            jax.ShapeDtypeStruct((n_batch * t_len, D_POOL + D_RG), Y_DTYPE),
            jax.ShapeDtypeStruct((n_batch, 2, D_RG), F32),
        ],
        scratch_shapes=[
            pltpu.VMEM((RG_CHUNK + 2 * HALO, D_RG), F32),
            pltpu.VMEM((RG_CHUNK, D_RG), F32),
            pltpu.VMEM((RG_CHUNK, D_RG), F32),
            pltpu.VMEM((RG_CHUNK, D_RG), F32),
            pltpu.VMEM((t_len, D_RG), F32),
        ],
        compiler_params=_ARB1,
    )(z, h0, p['pool_w'], p['pool_scale'], p['conv_w'], p['conv_b'], p['wg'], p['bg'], p['lam'], band)


def _rwkv_kernel(*refs, t_len, zero_init):
    L = SCAN_CHUNK
    n_ch = t_len // L
    L2 = 2 * L
    z_ref = refs[0]
    (mu_ref, kkw_ref, ka_ref, rk_ref, g2_ref, w0_ref, w2_ref, a0_ref, a2_ref, lnw_ref, lnb_ref, tri_ref, bd_ref,
     y_ref, sfin_ref, ebuf, s_s, yd_s, bon_s, g_s, zs_s, kkn_s) = refs[1 if zero_init else 2:]
    if zero_init:
        s_s[...] = jnp.zeros(s_s.shape, F32)
    else:
        s0_ref = refs[1]
        zero = jnp.zeros((RW_HEAD, RW_HEAD), F32)
        for d in range(2):
            for p in range(RW_PAIRS):
                top = jnp.concatenate([s0_ref[d, 2 * p], zero], axis=1)
                bot = jnp.concatenate([zero, s0_ref[d, 2 * p + 1]], axis=1)
                s_s[d, p] = jnp.concatenate([top, bot], axis=0)

    def stack(x):
        first_head = lax.broadcasted_iota(jnp.int32, (L, LANES), 1) < RW_HEAD
        return jnp.concatenate([jnp.where(first_head, x, 0.0), jnp.where(first_head, 0.0, x)],
                               axis=0).astype(BF16)

    def shift_pair(i, carry):
        ones_bd = bd_ref[...]
        o = 3 * D_RW
        sts = []
        for n in range(2):
            c0 = pl.multiple_of((2 * i + n) * L, L)
            ebuf[n] = _load_ext(z_ref, c0, L, t_len, 0, RW_ZCOLS)
            cur = ebuf[n, HALO:HALO + L, :]
            prev = ebuf[n, HALO - 1:HALO - 1 + L, :]
            nxt = ebuf[n, HALO + 1:HALO + 1 + L, :]
            zs = cur + mu_ref[0:1, :] * (prev - cur) + mu_ref[1:2, :] * (nxt - cur)
            zs_s[pl.ds(c0, L), :] = zs.astype(BF16)
            kkv = zs[:, D_RW:2 * D_RW] * kkw_ref[...]
            sts.append(dict(c0=c0, kkv=kkv, ss=_head_sum(kkv * kkv, ones_bd),
                            g=_dot(_sigmoid(zs[:, o:o + RW_G_LORA]).astype(BF16), g2_ref[...])))
        for st in sts:
            rows = pl.ds(st['c0'], L)
            kkn_s[rows, :] = (st['kkv'] / jnp.maximum(jnp.sqrt(st['ss']), 1e-12)).astype(BF16)
            g_s[rows, :] = st['g'].astype(g_s.dtype)
        return carry

    lax.fori_loop(0, n_ch // 2, shift_pair, 0)

    def prep(items):
        ones_bd = bd_ref[...]
        o = 3 * D_RW
        sts = []
        for c0, d in items:
            rows = pl.ds(c0, L)
            sts.append(dict(c0=c0, d=d, r=zs_s[rows, 0:D_RW].astype(F32), k=zs_s[rows, D_RW:2 * D_RW].astype(F32),
                            v=zs_s[rows, 2 * D_RW:3 * D_RW].astype(F32), kkn=kkn_s[rows, :].astype(F32),
                            wd=zs_s[rows, o + RW_G_LORA:o + RW_G_LORA + LANES].astype(F32),
                            ad=zs_s[rows, o + RW_G_LORA + LANES:o + RW_G_LORA + 2 * LANES]))
        for st in sts:
            d = st['d']
            tw_hi, tw_lo = _split2(jnp.tanh(st['wd']))
            st['w_lora'] = _dot(tw_hi, w2_ref[d, 0]) + (_dot(tw_hi, w2_ref[d, 1]) + _dot(tw_lo, w2_ref[d, 0]))
            st['a_lora'] = _dot(st['ad'], a2_ref[d])
        for st in sts:
            d = st['d']
            w_log = -_softplus(-(w0_ref[d:d + 1, :] + st['w_lora'])) - 0.5
            st['lw'] = -jnp.exp(w_log)
            st['a'] = _sigmoid(a0_ref[d:d + 1, :] + st['a_lora'])
            st['kd'] = st['k'] * (1.0 + (st['a'] - 1.0) * ka_ref[...])
            st['bon'] = _head_sum(st['r'] * st['kd'] * rk_ref[...], ones_bd, single_pass=True)
            st['cum'] = _dot_exact_lhs(tri_ref[d], st['lw'])
        out = []
        for st in sts:
            d, r, v, kd, kkn, a, cum = st['d'], st['r'], st['v'], st['kd'], st['kkn'], st['a'], st['cum']
            bon_s[d, pl.ds(st['c0'], L), :] = (st['bon'] * v).astype(bon_s.dtype)
            e_pos = jnp.exp(cum)
            e_neg = jnp.exp(-cum)
            rh = r * e_pos
            kh = kd * e_neg
            bh = (kkn * a) * e_neg
            ah = -kkn * jnp.exp(cum - st['lw'])
            last = L - 1 if d == 0 else 0
            gam = e_pos[last:last + 1, :]
            bhg = bh * gam
            khg = kh * gam
            chains = []
            for p in range(RW_PAIRS):
                cols = slice(p * LANES, (p + 1) * LANES)
                chains.append(dict(
                    d=d, p=p, gam=gam[:, cols], v_st=stack(v[:, cols]),
                    lhs=jnp.concatenate([stack(ah[:, cols]), stack(rh[:, cols])], axis=0),
                    rhs=jnp.concatenate([stack(bh[:, cols]), stack(kh[:, cols])], axis=0),
                    bk=jnp.concatenate([stack(bhg[:, cols]), stack(khg[:, cols])], axis=0)))
            out.append(chains)
        return out

    n_levels = int(math.log2(L))

    def state_free(chains):
        ri = lax.broadcasted_iota(jnp.int32, (L2, L2), 0)
        ci = lax.broadcasted_iota(jnp.int32, (L2, L2), 1)
        eye = jnp.where(ri == ci, 1.0, 0.0)
        rl = jnp.where(ri < L, ri, ri - L)
        cl = jnp.where(ci < L, ci, ci - L)
        strict = (cl < rl, cl > rl)
        incl = (cl <= rl, cl >= rl)
        level = ri ^ ci

        for ch in chains:
            ch['sc'] = _dot_nt(ch['lhs'], ch['rhs'])
        for ch in chains:
            sc, d = ch['sc'], ch['d']
            n_ab = jnp.where(strict[d], sc[0:L2, 0:L2], 0.0)
            m_ak = jnp.where(strict[d], sc[0:L2, L2:2 * L2], 0.0).astype(BF16)
            t_rb = jnp.where(incl[d], sc[L2:2 * L2, 0:L2], 0.0)
            t_rk = jnp.where(incl[d], sc[L2:2 * L2, L2:2 * L2], 0.0)
            ch['t_rbk'] = jnp.concatenate([t_rb, t_rk], axis=1).astype(BF16)
            ch['mv'] = _dot(m_ak, ch['v_st'])
            ch['t'] = eye + jnp.where(level == 1, n_ab, 0.0)
            ch['n_lv'] = [jnp.where(lax.shift_right_logical(level, lv) == 1, n_ab, 0.0).astype(BF16)
                          for lv in range(1, n_levels)]
            del ch['sc']
        for lv in range(n_levels - 1):
            for ch in chains:
                ch['tb'] = ch['t'].astype(BF16)
                ch['x'] = _dot(ch['n_lv'][lv], ch['tb']).astype(BF16)
            for ch in chains:
                ch['t'] = ch['t'] + _dot(ch['tb'], ch['x'])

    def state_step(chains, c0s):
        for ch in chains:
            ch['s_prev'] = s_s[ch['d'], ch['p']]
            ch['ar_s'] = _dot_nt(ch['lhs'], ch['s_prev'].astype(BF16))
        for ch in chains:
            u_rhs = ch['ar_s'][0:L2] + ch['mv']
            u_st = _dot(ch['t'].astype(BF16), u_rhs.astype(BF16)).astype(BF16)
            ch['uv'] = jnp.concatenate([u_st, ch['v_st']], axis=0)
        for ch in chains:
            y_st = ch['ar_s'][L2:2 * L2] + _dot(ch['t_rbk'], ch['uv'])
            ch['y'] = y_st[0:L] + y_st[L:L2]
            s_s[ch['d'], ch['p']] = ch['s_prev'] * ch['gam'] + _dot_tn(ch['uv'], ch['bk'])
        for d in range(2):
            y_dir = jnp.concatenate([ch['y'] for ch in chains if ch['d'] == d], axis=1)
            yd_s[d, pl.ds(c0s[d], L), :] = y_dir.astype(yd_s.dtype)

    steps = 2

    def body(i, carry):
        starts = []
        for j in range(steps):
            step = i * steps + j
            starts.append((pl.multiple_of(step * L, L), pl.multiple_of((n_ch - 1 - step) * L, L)))
        prepared = prep([(c0s[d], d) for c0s in starts for d in range(2)])
        groups = [(prepared[2 * j] + prepared[2 * j + 1], starts[j]) for j in range(steps)]
        state_free([ch for chains, _ in groups for ch in chains])
        for chains, c0s in groups:
            state_step(chains, c0s)
        return carry

    lax.fori_loop(0, n_ch // steps, body, 0)

    def epilogue(i, carry):
        c0 = pl.multiple_of(i * L, L)
        ones_bd = bd_ref[...]
        ys = yd_s[0, pl.ds(c0, L), :].astype(F32) + yd_s[1, pl.ds(c0, L), :].astype(F32)
        mu = _head_sum(ys, ones_bd) * (1.0 / RW_HEAD)
        yc = ys - mu
        var = _head_sum(yc * yc, ones_bd) * (1.0 / RW_HEAD)
        yn = yc * lax.rsqrt(var + RW_LN_EPS) * lnw_ref[...] + lnb_ref[...]
        bonus = bon_s[0, pl.ds(c0, L), :].astype(F32) + bon_s[1, pl.ds(c0, L), :].astype(F32)
        out = (yn + bonus) * g_s[pl.ds(c0, L), :].astype(F32)
        y_ref[pl.ds(c0, L), :] = out.astype(y_ref.dtype)
        return carry

    lax.fori_loop(0, n_ch, epilogue, 0, unroll=min(EPILOGUE_UNROLL, n_ch))
    for d in range(2):
        for p in range(RW_PAIRS):
            s_pair = s_s[d, p]
            sfin_ref[d, 2 * p] = s_pair[0:RW_HEAD, 0:RW_HEAD]
            sfin_ref[d, 2 * p + 1] = s_pair[RW_HEAD:, RW_HEAD:]


def _rwkv_mixer(z, s0, p, n_batch, t_len):
    L = SCAN_CHUNK
    c2 = lambda b: (0, 0)
    c3 = lambda b: (0, 0, 0)
    st_spec = pl.BlockSpec((None, 2, RW_HEADS, RW_HEAD, RW_HEAD), lambda b: (b, 0, 0, 0, 0))
    zero_init = s0 is None
    z_block_bytes = t_len * RW_ZCOLS * z.dtype.itemsize
    return pl.pallas_call(
        functools.partial(_rwkv_kernel, t_len=t_len, zero_init=zero_init),
        grid=(n_batch,),
        in_specs=[pl.BlockSpec((t_len, RW_ZCOLS), lambda b: (b, 0),
                               **(dict(pipeline_mode=pl.Buffered(1)) if z_block_bytes > Z_DOUBLE_BUFFER_MAX else {}))]
        + ([] if zero_init else [st_spec]) + [
            pl.BlockSpec((2, RW_ZCOLS), c2),
            pl.BlockSpec((1, D_RW), c2),
            pl.BlockSpec((1, D_RW), c2),
            pl.BlockSpec((1, D_RW), c2),
            pl.BlockSpec((RW_G_LORA, D_RW), c2),
            pl.BlockSpec((2, D_RW), c2),
            pl.BlockSpec((2, 2, LANES, D_RW), lambda b: (0, 0, 0, 0)),
            pl.BlockSpec((2, D_RW), c2),
            pl.BlockSpec((2, LANES, D_RW), c3),
            pl.BlockSpec((1, D_RW), c2),
            pl.BlockSpec((1, D_RW), c2),
            pl.BlockSpec((2, L, L), c3),
            pl.BlockSpec((LANES, LANES), c2),
        ],
        out_specs=[pl.BlockSpec((t_len, D_RW), lambda b: (b, 0)), st_spec],
        out_shape=[
            jax.ShapeDtypeStruct((n_batch * t_len, D_RW), Y_DTYPE),
            jax.ShapeDtypeStruct((n_batch, 2, RW_HEADS, RW_HEAD, RW_HEAD), F32),
        ],
        scratch_shapes=[
            pltpu.VMEM((2, L + 2 * HALO, RW_ZCOLS), F32),
            pltpu.VMEM((2, RW_PAIRS, LANES, LANES), F32),
            pltpu.VMEM((2, t_len, D_RW), BF16),
            pltpu.VMEM((2, t_len, D_RW), BF16),
            pltpu.VMEM((t_len, D_RW), BF16),
            pltpu.VMEM((t_len, RW_ZCOLS), BF16),
            pltpu.VMEM((t_len, D_RW), BF16),
        ],
        compiler_params=_ARB1,
    )(z, *(() if zero_init else (s0,)), p['mu'], p['kk'], p['ka'], p['rk'], p['g2'], p['w0'], p['w2'], p['a0'],
      p['a2'], p['ln_w'], p['ln_b'], p['tri'], p['ones_bd'])


def _mlstm_kernel(*refs, t_len, zero_init):
    L = ML_SCAN
    n_ch = t_len // L
    z_ref = refs[0]
    (cw_ref, cb_ref, gb_ref, nw_ref, tri_ref, esel_ref, shift_ref, ones_ref,
     y_ref, cfin_ref, nfin_ref, mfin_ref, c_s, n_s, m_s, hd_s, qk_s, gx_s, gr_s) = refs[1 if zero_init else 4:]
    if zero_init:
        c_s[...] = jnp.zeros(c_s.shape, F32)
        n_s[...] = jnp.zeros(n_s.shape, F32)
        m_s[...] = jnp.zeros(m_s.shape, F32)
    else:
        c0_ref, n0_ref, m0_ref = refs[1:4]
        c_s[...] = c0_ref[...]
        m_s[...] = m0_ref[...]
        for d in range(2):
            for h in range(ML_HEADS):
                n_s[d, h] = jnp.broadcast_to(n0_ref[d, h:h + 1, :], (ML_DK, ML_DK)).T
    q_scale = ML_DK ** -0.5
    g_lo = 4 * D_ML
    n_real = 4 * ML_HEADS

    def conv_chunk(i, carry):
        c0 = pl.multiple_of(i * L, L)
        gts = z_ref[pl.ds(c0, L), g_lo:g_lo + LANES].astype(F32) + gb_ref[...]
        x16 = gts.T[0:n_real, :]
        lf_hi, lf_lo = _split2(-_softplus(-x16))
        lane = lax.broadcasted_iota(jnp.int32, (8, L), 1)
        stats = []
        for d in range(2):
            b16 = _dot_nt(lf_hi, tri_ref[d]) + _dot_nt(lf_lo, tri_ref[d])
            x8 = x16[8 * d:8 * d + 8, :]
            b8 = b16[8 * d:8 * d + 8, :]
            g8 = pltpu.roll(x8, ML_HEADS, axis=0) - b8
            stats.append([g8, b8, g8])
        step = 1
        while step < L:
            for d in range(2):
                cm8 = stats[d][2]
                if d == 0:
                    sh = jnp.where(lane >= step, pltpu.roll(cm8, step, axis=1), -jnp.inf)
                else:
                    sh = jnp.where(lane < L - step, pltpu.roll(cm8, L - step, axis=1), -jnp.inf)
                stats[d][2] = jnp.maximum(cm8, sh)
            step *= 2

        ext = _load_ext(z_ref, c0, L, t_len, 0, 2 * D_ML, raw=True)
        taps = _dot(shift_ref[...], ext)
        qk = cb_ref[...] + ext[HALO:HALO + L].astype(F32) * cw_ref[CONV_LEFT:CONV_LEFT + 1, :]
        for n, j in enumerate(jj for jj in range(CONV_W) if jj != CONV_LEFT):
            qk = qk + taps[n * L:(n + 1) * L] * cw_ref[j:j + 1, :]
        qk = _silu(qk)
        qk_s[pl.ds(c0, L), 0:D_ML] = (qk[:, :D_ML] * q_scale).astype(BF16)
        qk_s[pl.ds(c0, L), D_ML:2 * D_ML] = qk[:, D_ML:].astype(BF16)

        for d, (g8, b8, cm8) in enumerate(stats):
            x32 = jnp.concatenate([g8, b8, cm8, jnp.zeros((8, L), F32)], axis=0)
            hi = x32.astype(BF16)
            r1 = x32 - hi.astype(F32)
            mid = r1.astype(BF16)
            lo = (r1 - mid.astype(F32)).astype(BF16)
            gx_s[d, i] = jnp.concatenate([hi, mid, lo], axis=0)
            gr_s[d, i] = g8
        return carry

    lax.fori_loop(0, n_ch, conv_chunk, 0, unroll=min(ML_UNROLL, n_ch))

    def prep(ci, d):
        c0 = pl.multiple_of(ci * L, L)
        q = qk_s[pl.ds(c0, L), 0:D_ML]
        k = qk_s[pl.ds(c0, L), D_ML:2 * D_ML]
        v = z_ref[pl.ds(c0, L), 2 * D_ML:3 * D_ML]
        g8 = gr_s[d, ci]
        cb = _dot_tn(gx_s[d, ci], esel_ref[...])
        ri = lax.broadcasted_iota(jnp.int32, (L, L), 0)
        ci = lax.broadcasted_iota(jnp.int32, (L, L), 1)
        causal = (ci <= ri) if d == 0 else (ci >= ri)
        last = L - 1 if d == 0 else 0
        ones = ones_ref[...]
        chains = []
        for h in range(ML_HEADS):
            cols = slice(h * ML_DK, (h + 1) * ML_DK)
            chains.append(dict(
                d=d, h=h, last=last, causal=causal,
                g_row=g8[ML_HEADS + h:ML_HEADS + h + 1, :],
                g_b=cb[:, cols], b_b=cb[:, D_ML + h * ML_DK:D_ML + (h + 1) * ML_DK],
                cm_b=cb[:, 2 * D_ML + h * ML_DK:2 * D_ML + (h + 1) * ML_DK],
                qb=q[:, cols], kb=k[:, cols],
                v=v[:, cols].astype(F32), v_ones=jnp.concatenate([v[:, cols].astype(BF16), ones], axis=1)))
        return chains

    def body(i, carry):
        c0s = (pl.multiple_of(i * L, L), pl.multiple_of((n_ch - 1 - i) * L, L))
        chains = prep(i, 0) + prep(n_ch - 1 - i, 1)
        for ch in chains:
            d, h = ch['d'], ch['h']
            ch['qk'] = _dot_nt(ch['qb'], ch['kb'])
            ch['c_prev'] = c_s[d, h]
            ch['n_prev'] = n_s[d, h]
            ch['m_prev'] = m_s[d, h:h + 1, :]
            ch['qcn'] = _dot(ch['qb'], jnp.concatenate([ch['c_prev'].astype(BF16),
                                                        ch['n_prev'].astype(BF16)], axis=1))
        for ch in chains:
            mu = jnp.maximum(ch['m_prev'], ch['cm_b'])
            ch['mu'] = mu
            ch['s'] = ch['qk'] * jnp.exp(jnp.where(ch['causal'], ch['g_row'] - mu, -jnp.inf))
            ch['w_inter'] = jnp.exp(ch['m_prev'] - mu)
        for ch in chains:
            last = ch['last']
            ch['sv'] = _dot(ch['s'].astype(BF16), ch['v_ones'])
            mu_last = ch['mu'][last:last + 1, :]
            ch['m_new'] = ch['b_b'][last:last + 1, :] + mu_last
            ch['dec'] = jnp.exp(ch['m_prev'] - mu_last)
            w_s = jnp.exp(ch['g_b'] - mu_last)
            ch['upd'] = _dot_tn(ch['kb'], jnp.concatenate([(w_s * ch['v']).astype(BF16),
                                                           w_s.astype(BF16)], axis=1))
        for ch in chains:
            d, h = ch['d'], ch['h']
            num = ch['sv'][:, :ML_DK] + ch['w_inter'] * ch['qcn'][:, :ML_DK]
            den = ch['sv'][:, ML_DK:] + ch['w_inter'] * ch['qcn'][:, ML_DK:]
            ch['out'] = num / jnp.maximum(jnp.abs(den), jnp.exp(-(ch['b_b'] + ch['mu'])))
            c_s[d, h] = ch['dec'] * ch['c_prev'] + ch['upd'][:, :ML_DK]
            n_s[d, h] = ch['dec'] * ch['n_prev'] + ch['upd'][:, ML_DK:]
            m_s[d, h:h + 1, :] = ch['m_new']
        for d in range(2):
            hd_s[d, pl.ds(c0s[d], L), :] = jnp.concatenate([ch['out'] for ch in chains if ch['d'] == d], axis=1)
        return carry

    lax.fori_loop(0, n_ch, body, 0, unroll=min(ML_UNROLL, n_ch))

    def epilogue(i, carry):
        c0 = pl.multiple_of(i * L, L)
        ones = ones_ref[...]
        hs = hd_s[0, pl.ds(c0, L), :] + hd_s[1, pl.ds(c0, L), :]
        o = z_ref[pl.ds(c0, L), 3 * D_ML:4 * D_ML].astype(F32)
        mu = _head_sum(hs, ones) * (1.0 / ML_DK)
        xc = hs - mu
        var = _head_sum(xc * xc, ones) * (1.0 / ML_DK)
        y = xc * lax.rsqrt(var + NORM_EPS) * nw_ref[...] * _sigmoid(o)
        y_ref[pl.ds(c0, L), :] = y.astype(y_ref.dtype)
        return carry

    lax.fori_loop(0, n_ch, epilogue, 0, unroll=min(EPILOGUE_UNROLL, n_ch))
    cfin_ref[...] = c_s[...]
    mfin_ref[...] = m_s[...]
    for d in range(2):
        for h in range(ML_HEADS):
            nfin_ref[d, h:h + 1, :] = n_s[d, h].T[0:1, :]


def _mlstm_mixer(z, states, p, n_batch, t_len):
    assert z.dtype == BF16
    L = ML_SCAN
    c2 = lambda b: (0, 0)
    c3 = lambda b: (0, 0, 0)
    c_spec = pl.BlockSpec((None, 2, ML_HEADS, ML_DK, ML_DK), lambda b: (b, 0, 0, 0, 0))
    m_spec = pl.BlockSpec((None, 2, ML_HEADS, ML_DK), lambda b: (b, 0, 0, 0))
    zero_init = states is None
    state_specs = [] if zero_init else [c_spec, m_spec, m_spec]
    return pl.pallas_call(
        functools.partial(_mlstm_kernel, t_len=t_len, zero_init=zero_init),
        grid=(n_batch,),
        in_specs=[pl.BlockSpec((t_len, ML_ZCOLS), lambda b: (b, 0))] + state_specs + [
            pl.BlockSpec((CONV_W, 2 * D_ML), c2),
            pl.BlockSpec((1, 2 * D_ML), c2),
            pl.BlockSpec((1, LANES), c2),
            pl.BlockSpec((1, D_ML), c2),
            pl.BlockSpec((2, L, L), c3),
            pl.BlockSpec(p['esel'].shape, c2),
            pl.BlockSpec(p['shift'].shape, c2),
            pl.BlockSpec((L, LANES), c2),
        ],
        out_specs=[pl.BlockSpec((t_len, D_ML), lambda b: (b, 0)), c_spec, m_spec, m_spec],
        out_shape=[
            jax.ShapeDtypeStruct((n_batch * t_len, D_ML), Y_DTYPE),
            jax.ShapeDtypeStruct((n_batch, 2, ML_HEADS, ML_DK, ML_DK), F32),
            jax.ShapeDtypeStruct((n_batch, 2, ML_HEADS, ML_DK), F32),
            jax.ShapeDtypeStruct((n_batch, 2, ML_HEADS, ML_DK), F32),
        ],
        scratch_shapes=[
            pltpu.VMEM((2, ML_HEADS, ML_DK, ML_DK), F32),
            pltpu.VMEM((2, ML_HEADS, ML_DK, ML_DK), F32),
            pltpu.VMEM((2, ML_HEADS, ML_DK), F32),
            pltpu.VMEM((2, t_len, D_ML), F32),
            pltpu.VMEM((t_len, 2 * D_ML), BF16),
            pltpu.VMEM((2, t_len // L, 96, L), BF16),
            pltpu.VMEM((2, t_len // L, 8, L), F32),
        ],
        compiler_params=_ARB1,
    )(z, *(() if zero_init else states), p['conv_w'], p['conv_b'], p['gate_b'], p['norm_w'], p['tri'], p['esel'],
      p['shift'], p['ones'])


def _conv_shift(n):
    taps = jnp.array([j for j in range(CONV_W) if j != CONV_LEFT])
    t = jnp.arange((CONV_W - 1) * n)
    src = HALO + t % n + taps[t // n] - CONV_LEFT
    return (src[:, None] == jnp.arange(n + 2 * HALO)[None, :]).astype(BF16)


def _mlstm_consts():
    L = ML_SCAN
    lower = jnp.tril(jnp.ones((L, L), F32))
    tri = jnp.stack([lower, lower.T]).astype(BF16)
    rows = jnp.arange(32)
    blk = jnp.arange(3 * D_ML) // ML_DK
    src_row = 8 * (blk // ML_HEADS) + ML_HEADS + blk % ML_HEADS
    esel = (rows[:, None] == src_row[None, :]).astype(BF16)
    esel = jnp.concatenate([esel, esel, esel], axis=0)
    return dict(tri=tri, esel=esel, shift=_conv_shift(L), ones=jnp.ones((L, LANES), BF16))


def _block_diag(w):
    n, c, d = w.shape
    return jnp.einsum('ncd,nm->ncmd', w, jnp.eye(n, dtype=w.dtype)).reshape(n * c, n * d)


def _scan_tri():
    L = SCAN_CHUNK
    lower = jnp.tril(jnp.ones((L, L), F32))
    return jnp.stack([lower, lower.T]).astype(BF16)


def _prep_even(ev_w_in, pool_w, pool_scale, rg_conv_w, rg_conv_b, rg_wa, rg_ba, rg_wx, rg_bx, rg_lam, ev_w_out, j):
    wg = jnp.concatenate([_block_diag(rg_wa[j, 0]), _block_diag(rg_wx[j, 0]),
                          _block_diag(rg_wa[j, 1]), _block_diag(rg_wx[j, 1])], axis=1)
    bg = jnp.concatenate([rg_ba[j, 0], rg_bx[j, 0], rg_ba[j, 1], rg_bx[j, 1]]).reshape(1, 4 * D_RG)
    return dict(
        w_in=[ev_w_in[j].astype(BF16)],
        w_out=ev_w_out[j].astype(BF16),
        pool_w=pool_w[j].astype(BF16),
        pool_scale=pool_scale[j].reshape(1, D_POOL),
        conv_w=rg_conv_w[j],
        conv_b=rg_conv_b[j].reshape(1, D_RG),
        wg=wg.astype(BF16),
        bg=bg,
        lam=rg_lam[j],
    )


def _pad_cols(w, cuts, widths):
    pieces = jnp.split(w, cuts, axis=-1)
    out = []
    for piece, width in zip(pieces, widths):
        extra = width - piece.shape[-1]
        if extra:
            piece = jnp.pad(piece, [(0, 0)] * (piece.ndim - 1) + [(0, extra)])
        out.append(piece)
    return jnp.concatenate(out, axis=-1)


def _prep_odd(od_w_in, rw_mu, rw_w0, rw_w2, rw_a0, rw_a2, rw_kk, rw_ka, rw_rk, rw_g2, rw_ln_w, rw_ln_b,
              ml_conv_w, ml_conv_b, ml_bi, ml_bf, ml_norm_w, od_w_out, j):
    rw_cols = 3 * D_RW + RW_G_LORA + RW_W_LORA + RW_A_LORA
    base = 3 * D_RW + RW_G_LORA
    cuts = [base, base + RW_W_LORA]
    widths = [base, LANES, LANES]
    w_rw = _pad_cols(od_w_in[j][:, :rw_cols], cuts, widths)
    w_ml = _pad_cols(od_w_in[j][:, rw_cols:], [4 * D_ML], [4 * D_ML, LANES])
    pad_rows = lambda w: jnp.pad(w, ((0, 0), (0, LANES - w.shape[1]), (0, 0)))
    gate_b = jnp.concatenate([ml_bi[j, 0], ml_bf[j, 0], ml_bi[j, 1], ml_bf[j, 1]])
    gate_b = jnp.pad(gate_b, (0, LANES - gate_b.shape[0])).reshape(1, LANES)
    hm = jnp.arange(LANES) // RW_HEAD
    rw = dict(
        mu=_pad_cols(rw_mu[j], cuts, widths),
        kk=rw_kk[j].reshape(1, D_RW), ka=rw_ka[j].reshape(1, D_RW), rk=rw_rk[j].reshape(1, D_RW),
        g2=rw_g2[j].astype(BF16),
        w0=rw_w0[j], w2=jnp.stack(_split2(pad_rows(rw_w2[j])), axis=1),
        a0=rw_a0[j], a2=pad_rows(rw_a2[j]).astype(BF16),
        ln_w=rw_ln_w[j].reshape(1, D_RW), ln_b=rw_ln_b[j].reshape(1, D_RW),
        tri=_scan_tri(),
        ones_bd=(hm[:, None] == hm[None, :]).astype(BF16),
    )
    ml = dict(
        conv_w=ml_conv_w[j], conv_b=ml_conv_b[j].reshape(1, 2 * D_ML),
        gate_b=gate_b, norm_w=ml_norm_w[j].reshape(1, D_ML), **_mlstm_consts(),
    )
    return dict(w_in=[w_rw.astype(BF16), w_ml.astype(BF16)], w_out=od_w_out[j].astype(BF16), rw=rw, ml=ml)


def _trunk(x, mod, rows_per_mod, mod_base, grid, states, params, n_batch, t_len):
    ev, od, dense = params
    rg0, rw0, ml0 = states
    x = x.reshape(n_batch * t_len, D_MODEL)
    spec = (rows_per_mod, mod_base)

    (z,) = _in_proj(x, mod[0], *spec, dense['norm1_w'][0], ev['w_in'])
    y_ev, rg_f = _even_mixer(z, rg0, ev, n_batch, t_len, grid)
    x = _out_mlp([y_ev], x, mod[0], *spec, dense['norm2_w'][0], ev['w_out'],
                 dense['mlp_w1'], dense['mlp_w2'], 0, dense['final_norm_w'], False)

    z_rw, z_ml = _in_proj(x, mod[1], *spec, dense['norm1_w'][1], od['w_in'])
    y_rw, s_f = _rwkv_mixer(z_rw, rw0, od['rw'], n_batch, t_len)
    y_ml, c_f, n_f, m_f = _mlstm_mixer(z_ml, ml0, od['ml'], n_batch, t_len)
    y = _out_mlp([y_rw, y_ml], x, mod[1], *spec, dense['norm2_w'][1], od['w_out'],
                 dense['mlp_w1'], dense['mlp_w2'], 1, dense['final_norm_w'], True)
    return y.reshape(n_batch, t_len, D_MODEL), (rg_f, s_f, c_f, n_f, m_f)


def kernel(x_prompt, x_sample, state_rglru, state_rwkv, state_mlstm_C, state_mlstm_n, state_mlstm_m,
           c, c_ctx, norm1_w, norm2_w, w_mod, b_mod, mlp_w1, mlp_w2, final_norm_w,
           ev_w_in, pool_w, pool_scale, rg_conv_w, rg_conv_b, rg_wa, rg_ba, rg_wx, rg_bx, rg_lam, ev_w_out,
           od_w_in, rw_mu, rw_w0, rw_w2, rw_a0, rw_a2, rw_kk, rw_ka, rw_rk, rw_g2, rw_ln_w, rw_ln_b,
           ml_conv_w, ml_conv_b, ml_bi, ml_bf, ml_norm_w, od_w_out):
    bp, tp, _ = x_prompt.shape
    bs, ts, _ = x_sample.shape

    assert bs + 1 <= N_COND_ROWS and ts % ROW_BLOCK == 0 and (bp * tp) % ROW_BLOCK == 0
    cond = jnp.concatenate([c_ctx[None, :], c, jnp.zeros((N_COND_ROWS - 1 - bs, D_MODEL), F32)], axis=0)
    mod = _modulation(cond, w_mod, b_mod)

    ev = _prep_even(ev_w_in, pool_w, pool_scale, rg_conv_w, rg_conv_b, rg_wa, rg_ba, rg_wx, rg_bx, rg_lam,
                    ev_w_out, 0)
    od = _prep_odd(od_w_in, rw_mu, rw_w0, rw_w2, rw_a0, rw_a2, rw_kk, rw_ka, rw_rk, rw_g2, rw_ln_w, rw_ln_b,
                   ml_conv_w, ml_conv_b, ml_bi, ml_bf, ml_norm_w, od_w_out, 0)
    dense = dict(norm1_w=norm1_w, norm2_w=norm2_w, mlp_w1=mlp_w1.astype(BF16), mlp_w2=mlp_w2.astype(BF16),
                 final_norm_w=final_norm_w)
    params = (ev, od, dense)

    zero_states = (jnp.zeros((bp, 2, D_RG), F32), None, None)
    y_prompt, (rg_f, s_f, c_f, n_f, m_f) = _trunk(
        x_prompt, mod, bp * tp, 0, False, zero_states, params, bp, tp)

    sample_states = (
        state_rglru[:, 0],
        state_rwkv[:, 0],
        (state_mlstm_C[:, 0], state_mlstm_n[:, 0],
         jnp.broadcast_to(state_mlstm_m[:, 0][..., None], (bs, 2, ML_HEADS, ML_DK))),
    )
    y_sample, _ = _trunk(x_sample, mod, ts, 1, True, sample_states, params, bs, ts)

    return (y_prompt, y_sample,
            rg_f[:, None],
            s_f[:, None],
            c_f[:, None],
            n_f[:, None],
            m_f[:, None, :, :, 0])
```

```python
import functools
import math

import jax
import jax.numpy as jnp
from jax import lax
from jax.experimental import pallas as pl
from jax.experimental.pallas import tpu as pltpu

F32 = jnp.float32
BF16 = jnp.bfloat16

D_MODEL = 1024
DEPTH = 2
GRID_W = 64
D_FF = 4 * D_MODEL
NORM_EPS = 1e-6
N_MOD = 6

D_POOL = D_MODEL // 2
POOL_WINDOWS = (2, 4, 8, 16)
POOL_GW = D_POOL // len(POOL_WINDOWS)
D_RG = D_MODEL // 2
RG_BLOCKS = 8
RG_C = 8.0
CONV_W = 4
CONV_LEFT = 2
EV_COLS = D_POOL + 2 * D_RG

D_RW = D_MODEL // 2
RW_HEAD = 64
RW_HEADS = D_RW // RW_HEAD
RW_PAIRS = RW_HEADS // 2
RW_W_LORA = 64
RW_A_LORA = 64
RW_G_LORA = 128
RW_LN_EPS = 64e-5
RW_ZCOLS = 3 * D_RW + RW_G_LORA + 128 + 128
D_ML = D_MODEL // 2
ML_HEADS = 4
ML_DK = D_ML // ML_HEADS
ML_ZCOLS = 4 * D_ML + 128

LANES = 128
HALO = 16
Z_DTYPE = BF16
Y_DTYPE = BF16
ROW_BLOCK = 512
SCAN_CHUNK = 64
ML_SCAN = 128
RG_CHUNK = 256
POOL_SEGS_PER_TRIP = 8
ML_UNROLL = 4
EPILOGUE_UNROLL = 8
MOD_COL_BLOCK = 1536
N_COND_ROWS = 16
Z_DOUBLE_BUFFER_MAX = 8 * 1024 * 1024
VMEM_LIMIT = 56 * 1024 * 1024

_ARB1 = pltpu.CompilerParams(dimension_semantics=("arbitrary",), vmem_limit_bytes=VMEM_LIMIT)
_ARB2 = pltpu.CompilerParams(dimension_semantics=("arbitrary", "arbitrary"), vmem_limit_bytes=VMEM_LIMIT)


def _dot(a, b):
    return jnp.dot(a, b, preferred_element_type=F32)


def _dot_nt(a, b):
    return lax.dot_general(a, b, (((1,), (1,)), ((), ())), preferred_element_type=F32)


def _dot_tn(a, b):
    return lax.dot_general(a, b, (((0,), (0,)), ((), ())), preferred_element_type=F32)


def _split2(x):
    hi = x.astype(BF16)
    lo = (x - hi.astype(F32)).astype(BF16)
    return hi, lo


def _dot_f32(a, b):
    ah, al = _split2(a)
    bh, bl = _split2(b)
    return _dot(ah, bh) + (_dot(ah, bl) + _dot(al, bh))


def _dot_exact_lhs(m_bf16, x):
    hi, lo = _split2(x)
    return _dot(m_bf16, hi) + _dot(m_bf16, lo)


def _dot_exact_rhs(x, m_bf16):
    hi, lo = _split2(x)
    return _dot(hi, m_bf16) + _dot(lo, m_bf16)


def _softplus(x):
    return jnp.maximum(x, 0.0) + jnp.log1p(jnp.exp(-jnp.abs(x)))


def _sigmoid(x):
    return jax.nn.sigmoid(x)


def _silu(x):
    return x * jax.nn.sigmoid(x)


def _gelu_tanh(x):
    c = math.sqrt(2.0 / math.pi)
    return x * (0.5 * (1.0 + jnp.tanh(c * (x + 0.044715 * (x * x * x)))))


def _rms_norm(x, w):
    ms = jnp.mean(x * x, axis=-1, keepdims=True)
    return x * lax.rsqrt(ms + NORM_EPS) * w


def _load_ext(z_ref, c0, n, t_len, lo, hi, raw=False):
    dt = z_ref.dtype if raw else F32
    cur = z_ref[pl.ds(c0, n), lo:hi].astype(dt)
    pb = pl.multiple_of(jnp.maximum(c0 - HALO, 0), HALO)
    pa = pl.multiple_of(jnp.minimum(c0 + n, t_len - HALO), HALO)
    before = z_ref[pl.ds(pb, HALO), lo:hi].astype(dt)
    after = z_ref[pl.ds(pa, HALO), lo:hi].astype(dt)
    before = jnp.where(c0 > 0, before, jnp.zeros_like(before))
    after = jnp.where(c0 + n < t_len, after, jnp.zeros_like(after))
    return jnp.concatenate([before, cur, after], axis=0)


def _head_sum(x, ones_bd, single_pass=False):
    n = x.shape[0]
    groups = x.shape[1] // LANES
    xs = jnp.concatenate([x[:, g * LANES:(g + 1) * LANES] for g in range(groups)], axis=0)
    s = _dot(xs.astype(BF16), ones_bd) if single_pass else _dot_exact_rhs(xs, ones_bd)
    return jnp.concatenate([s[g * n:(g + 1) * n] for g in range(groups)], axis=1)


def _mod_kernel(c_ref, w_ref, b_ref, o_ref):
    s = _silu(c_ref[...])
    o_ref[...] = _dot_f32(s, w_ref[...]) + b_ref[...]


def _modulation(cond, w_mod, b_mod):
    rows = cond.shape[0]
    tn = MOD_COL_BLOCK
    out = pl.pallas_call(
        _mod_kernel,
        grid=(DEPTH, N_MOD * D_MODEL // tn),
        in_specs=[
            pl.BlockSpec((rows, D_MODEL), lambda l, j: (0, 0)),
            pl.BlockSpec((None, D_MODEL, tn), lambda l, j: (l, 0, j)),
            pl.BlockSpec((None, 1, tn), lambda l, j: (l, 0, j)),
        ],
        out_specs=pl.BlockSpec((None, rows, tn), lambda l, j: (l, 0, j)),
        out_shape=jax.ShapeDtypeStruct((DEPTH, rows, N_MOD * D_MODEL), F32),
        compiler_params=_ARB2,
    )(cond, w_mod, b_mod.reshape(DEPTH, 1, N_MOD * D_MODEL))
    return out.reshape(DEPTH, rows, N_MOD, D_MODEL)


def _inproj_kernel(*refs, n_out, col_chunk):
    x_ref, mod_ref, nw_ref = refs[:3]
    w_refs = refs[3:3 + n_out]
    z_refs = refs[3 + n_out:]
    h = _rms_norm(x_ref[...], nw_ref[...])
    h = h * (1.0 + mod_ref[1:2, :]) + mod_ref[0:1, :]
    hb = h.astype(BF16)
    for w_ref, z_ref in zip(w_refs, z_refs):
        n = w_ref.shape[1]
        for c in range(0, n, col_chunk):
            e = min(c + col_chunk, n)
            z_ref[:, c:e] = _dot(hb, w_ref[:, c:e]).astype(z_ref.dtype)


def _in_proj(x, mod, rows_per_mod, mod_base, norm_w, weights):
    m = x.shape[0]
    tm = ROW_BLOCK
    n_out = len(weights)
    const = lambda i: (0, 0)
    in_specs = [
        pl.BlockSpec((tm, D_MODEL), lambda i: (i, 0)),
        pl.BlockSpec((None, N_MOD, D_MODEL), lambda i: (mod_base + (i * tm) // rows_per_mod, 0, 0)),
        pl.BlockSpec((1, D_MODEL), const),
    ] + [pl.BlockSpec(w.shape, const, pipeline_mode=pl.Buffered(1)) for w in weights]
    out_specs = [pl.BlockSpec((tm, w.shape[1]), lambda i: (i, 0)) for w in weights]
    out_shape = [jax.ShapeDtypeStruct((m, w.shape[1]), Z_DTYPE) for w in weights]
    return pl.pallas_call(
        functools.partial(_inproj_kernel, n_out=n_out, col_chunk=512),
        grid=(m // tm,),
        in_specs=in_specs,
        out_specs=out_specs,
        out_shape=out_shape,
        compiler_params=_ARB1,
    )(x, mod, norm_w.reshape(1, D_MODEL), *weights)


def _outmlp_kernel(*refs, n_y, final, ff_chunk):
    y_refs = refs[:n_y]
    x_ref, mod_ref, nw_ref, wo_ref, w1_ref, w2_ref, fw_ref, o_ref = refs[n_y:]
    y = None
    row = 0
    for y_ref in y_refs:
        k = y_ref.shape[1]
        part = _dot(y_ref[...], wo_ref[row:row + k, :])
        y = part if y is None else y + part
        row += k
    x = x_ref[...] + mod_ref[2:3, :] * y
    h = _rms_norm(x, nw_ref[...]) * (1.0 + mod_ref[4:5, :]) + mod_ref[3:4, :]
    hb = h.astype(BF16)
    acc = None
    for c in range(0, D_FF, ff_chunk):
        u = jnp.maximum(_dot(hb, w1_ref[:, c:c + ff_chunk]), 0.0)
        part = _dot((u * u).astype(BF16), w2_ref[c:c + ff_chunk, :])
        acc = part if acc is None else acc + part
    x = x + mod_ref[5:6, :] * acc
    if final:
        x = _rms_norm(x, fw_ref[...])
    o_ref[...] = x


def _out_mlp(ys, x, mod, rows_per_mod, mod_base, norm_w, w_out, w1, w2, layer, final_w, final):
    m = x.shape[0]
    tm = ROW_BLOCK
    const = lambda i: (0, 0)
    single = dict(pipeline_mode=pl.Buffered(1))
    pick = lambda i: (layer, 0, 0)
    in_specs = [pl.BlockSpec((tm, y.shape[1]), lambda i: (i, 0)) for y in ys] + [
        pl.BlockSpec((tm, D_MODEL), lambda i: (i, 0)),
        pl.BlockSpec((None, N_MOD, D_MODEL), lambda i: (mod_base + (i * tm) // rows_per_mod, 0, 0)),
        pl.BlockSpec((1, D_MODEL), const),
        pl.BlockSpec(w_out.shape, const, **single),
        pl.BlockSpec((None,) + w1.shape[1:], pick, **single),
        pl.BlockSpec((None,) + w2.shape[1:], pick, **single),
        pl.BlockSpec((1, D_MODEL), const),
    ]
    return pl.pallas_call(
        functools.partial(_outmlp_kernel, n_y=len(ys), final=final, ff_chunk=1024),
        grid=(m // tm,),
        in_specs=in_specs,
        out_specs=pl.BlockSpec((tm, D_MODEL), lambda i: (i, 0)),
        out_shape=jax.ShapeDtypeStruct((m, D_MODEL), F32),
        compiler_params=_ARB1,
    )(*ys, x, mod, norm_w.reshape(1, D_MODEL), w_out, w1, w2, final_w.reshape(1, D_MODEL))


def _even_kernel(z_ref, h0_ref, pw_ref, ps_ref, cw_ref, cb_ref, wg_ref, bg_ref, lam_ref, band_ref,
                 y_ref, hfin_ref, cbuf, a_s, b_s, h_s, hf_s, *, t_len, grid):
    seg = GRID_W if grid else t_len
    n_seg = t_len // seg
    n_groups = len(POOL_WINDOWS)

    pos = lax.broadcasted_iota(jnp.int32, (seg, POOL_GW), 0)
    offs = [(w // 2, w - 1 - w // 2) for w in POOL_WINDOWS]
    gcols = [slice(g * POOL_GW, (g + 1) * POOL_GW) for g in range(n_groups)]

    def grid_row(rr, cols):
        valid = jnp.logical_and(rr >= 0, rr < n_seg)
        src = pl.multiple_of(jnp.clip(rr, 0, n_seg - 1) * seg, seg)
        return jnp.where(valid, z_ref[pl.ds(src, seg), cols].astype(F32), 0.0)

    def pool_rows(rows, run):
        items = []
        for r in rows:
            base = r * seg if isinstance(r, int) else pl.multiple_of(r * seg, seg)
            new_run = []
            for g, (lo_off, hi_off) in enumerate(offs):
                cols = gcols[g]
                if grid:
                    s1 = run[g] + grid_row(r + hi_off, cols) - grid_row(r - lo_off - 1, cols)
                    new_run.append(s1)
                    cnt_r = jnp.minimum(r + hi_off, n_seg - 1) - jnp.maximum(r - lo_off, 0) + 1
                    m1 = s1 / cnt_r.astype(F32)
                    hi = m1.astype(BF16)
                    r1 = m1 - hi.astype(F32)
                    mid = r1.astype(BF16)
                    parts = jnp.concatenate([hi, mid, (r1 - mid.astype(F32)).astype(BF16)], axis=0)
                else:
                    parts = z_ref[pl.ds(base, seg), cols]
                items.append(dict(g=g, base=base, parts=parts))
            run = new_run
        for it in items:
            it['s2'] = _dot(band_ref[it['g']], it['parts'])
        for it in items:
            lo_off, hi_off = offs[it['g']]
            cnt_c = jnp.minimum(pos + hi_off, seg - 1) - jnp.maximum(pos - lo_off, 0) + 1
            xg = z_ref[pl.ds(it['base'], seg), gcols[it['g']]].astype(F32)
            it['d'] = (it['s2'] / cnt_c.astype(F32) - xg).astype(BF16)
        for it in items:
            it['y'] = _dot(it['d'], pw_ref[it['g']])
        for it in items:
            cols = gcols[it['g']]
            y_ref[pl.ds(it['base'], seg), cols] = (it['y'] * ps_ref[:, cols]).astype(y_ref.dtype)
        return tuple(run) if grid else None

    if grid:
        init = []
        for g, (lo_off, hi_off) in enumerate(offs):
            s1 = jnp.zeros((seg, POOL_GW), F32)
            for o in range(min(hi_off, n_seg)):
                s1 = s1 + z_ref[o * seg:(o + 1) * seg, gcols[g]].astype(F32)
            init.append(s1)
        per_trip = POOL_SEGS_PER_TRIP
        lax.fori_loop(0, n_seg // per_trip,
                      lambda i, run: pool_rows([i * per_trip + j for j in range(per_trip)], run), tuple(init))
    else:
        pool_rows([0], None)

    ch = RG_CHUNK
    n_ch = t_len // ch
    rg_lo, rg_hi = D_POOL, D_POOL + D_RG

    def gates(c0, d):
        cbuf[...] = _load_ext(z_ref, c0, ch, t_len, rg_lo, rg_hi)
        xc = cb_ref[...]
        for j in range(CONV_W):
            off = HALO - CONV_LEFT + j
            xc = xc + cbuf[off:off + ch, :] * cw_ref[j:j + 1, :]
        g = _dot(xc.astype(BF16), wg_ref[:, d * 2 * D_RG:(d + 1) * 2 * D_RG]) \
            + bg_ref[:, d * 2 * D_RG:(d + 1) * 2 * D_RG]
        r = _sigmoid(g[:, :D_RG])
        i = _sigmoid(g[:, D_RG:])
        log_a = (-RG_C) * r * _softplus(-lam_ref[d:d + 1, :])
        a = jnp.exp(log_a)
        a_s[...] = a
        one_minus_a2 = jnp.tanh(-log_a) * (a * a + 1.0)
        root = jnp.where(one_minus_a2 > 0.0, one_minus_a2 * lax.rsqrt(one_minus_a2), 0.0)
        b_s[...] = root * (i * xc)

    def scan_rows(h, reverse):
        def row(t, h):
            tt = (ch - 1 - t) if reverse else t
            h = a_s[pl.ds(tt, 1), :] * h + b_s[pl.ds(tt, 1), :]
            h_s[pl.ds(tt, 1), :] = h
            return h
        return lax.fori_loop(0, ch, row, h, unroll=8)

    def fwd_body(i, h):
        c0 = pl.multiple_of(i * ch, ch)
        gates(c0, 0)
        h = scan_rows(h, False)
        hf_s[pl.ds(c0, ch), :] = h_s[...]
        return h

    h_fwd = lax.fori_loop(0, n_ch, fwd_body, h0_ref[0:1, :])

    def bwd_body(i, h):
        c0 = pl.multiple_of((n_ch - 1 - i) * ch, ch)
        gates(c0, 1)
        h = scan_rows(h, True)
        u_gate = z_ref[pl.ds(c0, ch), rg_hi:rg_hi + D_RG].astype(F32)
        y = (hf_s[pl.ds(c0, ch), :] + h_s[...]) * _gelu_tanh(u_gate)
        y_ref[pl.ds(c0, ch), D_POOL:D_POOL + D_RG] = y.astype(y_ref.dtype)
        return h

    h_bwd = lax.fori_loop(0, n_ch, bwd_body, h0_ref[1:2, :])
    hfin_ref[0:1, :] = h_fwd
    hfin_ref[1:2, :] = h_bwd


def _even_mixer(z, h0, p, n_batch, t_len, grid):
    assert z.dtype == BF16
    seg = GRID_W if grid else t_len
    const2 = lambda b: (0, 0)
    const3 = lambda b: (0, 0, 0)
    idx = jnp.arange(seg)
    delta = idx[None, :] - idx[:, None]
    band = jnp.stack([jnp.logical_and(delta >= -(w // 2), delta <= w - 1 - w // 2) for w in POOL_WINDOWS])
    band = jnp.tile(band.astype(BF16), (1, 1, 3 if grid else 1))
    return pl.pallas_call(
        functools.partial(_even_kernel, t_len=t_len, grid=grid),
        grid=(n_batch,),
        in_specs=[
            pl.BlockSpec((t_len, EV_COLS), lambda b: (b, 0)),
            pl.BlockSpec((None, 2, D_RG), lambda b: (b, 0, 0)),
            pl.BlockSpec(p['pool_w'].shape, const3),
            pl.BlockSpec((1, D_POOL), const2),
            pl.BlockSpec((CONV_W, D_RG), const2),
            pl.BlockSpec((1, D_RG), const2),
            pl.BlockSpec((D_RG, 4 * D_RG), const2),
            pl.BlockSpec((1, 4 * D_RG), const2),
            pl.BlockSpec((2, D_RG), const2),
            pl.BlockSpec(band.shape, const3),
        ],
        out_specs=[
            pl.BlockSpec((t_len, D_POOL + D_RG), lambda b: (b, 0)),
            pl.BlockSpec((None, 2, D_RG), lambda b: (b, 0, 0)),
        ],
        out_shape=[
            jax.ShapeDtypeStruct((n_batch * t_len, D_POOL + D_RG), Y_DTYPE),
            jax.ShapeDtypeStruct((n_batch, 2, D_RG), F32),
        ],
        scratch_shapes=[
            pltpu.VMEM((RG_CHUNK + 2 * HALO, D_RG), F32),
            pltpu.VMEM((RG_CHUNK, D_RG), F32),
            pltpu.VMEM((RG_CHUNK, D_RG), F32),
            pltpu.VMEM((RG_CHUNK, D_RG), F32),
            pltpu.VMEM((t_len, D_RG), F32),
        ],
        compiler_params=_ARB1,
    )(z, h0, p['pool_w'], p['pool_scale'], p['conv_w'], p['conv_b'], p['wg'], p['bg'], p['lam'], band)


def _rwkv_kernel(*refs, t_len, zero_init):
    L = SCAN_CHUNK
    n_ch = t_len // L
    L2 = 2 * L
    z_ref = refs[0]
    (mu_ref, kkw_ref, ka_ref, rk_ref, g2_ref, w0_ref, w2_ref, a0_ref, a2_ref, lnw_ref, lnb_ref, tri_ref, bd_ref,
     y_ref, sfin_ref, ebuf, s_s, yd_s, bon_s, g_s, zs_s, kkn_s) = refs[1 if zero_init else 2:]
    if zero_init:
        s_s[...] = jnp.zeros(s_s.shape, F32)
    else:
        s0_ref = refs[1]
        zero = jnp.zeros((RW_HEAD, RW_HEAD), F32)
        for d in range(2):
            for p in range(RW_PAIRS):
                top = jnp.concatenate([s0_ref[d, 2 * p], zero], axis=1)
                bot = jnp.concatenate([zero, s0_ref[d, 2 * p + 1]], axis=1)
                s_s[d, p] = jnp.concatenate([top, bot], axis=0)

    def stack(x):
        first_head = lax.broadcasted_iota(jnp.int32, (L, LANES), 1) < RW_HEAD
        return jnp.concatenate([jnp.where(first_head, x, 0.0), jnp.where(first_head, 0.0, x)],
                               axis=0).astype(BF16)

    def shift_pair(i, carry):
        ones_bd = bd_ref[...]
        o = 3 * D_RW
        sts = []
        for n in range(2):
            c0 = pl.multiple_of((2 * i + n) * L, L)
            ebuf[n] = _load_ext(z_ref, c0, L, t_len, 0, RW_ZCOLS)
            cur = ebuf[n, HALO:HALO + L, :]
            prev = ebuf[n, HALO - 1:HALO - 1 + L, :]
            nxt = ebuf[n, HALO + 1:HALO + 1 + L, :]
            zs = cur + mu_ref[0:1, :] * (prev - cur) + mu_ref[1:2, :] * (nxt - cur)
            zs_s[pl.ds(c0, L), :] = zs.astype(BF16)
            kkv = zs[:, D_RW:2 * D_RW] * kkw_ref[...]
            sts.append(dict(c0=c0, kkv=kkv, ss=_head_sum(kkv * kkv, ones_bd),
                            g=_dot(_sigmoid(zs[:, o:o + RW_G_LORA]).astype(BF16), g2_ref[...])))
        for st in sts:
            rows = pl.ds(st['c0'], L)
            kkn_s[rows, :] = (st['kkv'] / jnp.maximum(jnp.sqrt(st['ss']), 1e-12)).astype(BF16)
            g_s[rows, :] = st['g'].astype(g_s.dtype)
        return carry

    lax.fori_loop(0, n_ch // 2, shift_pair, 0)

    def prep(items):
        ones_bd = bd_ref[...]
        o = 3 * D_RW
        sts = []
        for c0, d in items:
            rows = pl.ds(c0, L)
            sts.append(dict(c0=c0, d=d, r=zs_s[rows, 0:D_RW].astype(F32), k=zs_s[rows, D_RW:2 * D_RW].astype(F32),
                            v=zs_s[rows, 2 * D_RW:3 * D_RW].astype(F32), kkn=kkn_s[rows, :].astype(F32),
                            wd=zs_s[rows, o + RW_G_LORA:o + RW_G_LORA + LANES].astype(F32),
                            ad=zs_s[rows, o + RW_G_LORA + LANES:o + RW_G_LORA + 2 * LANES]))
        for st in sts:
            d = st['d']
            tw_hi, tw_lo = _split2(jnp.tanh(st['wd']))
            st['w_lora'] = _dot(tw_hi, w2_ref[d, 0]) + (_dot(tw_hi, w2_ref[d, 1]) + _dot(tw_lo, w2_ref[d, 0]))
            st['a_lora'] = _dot(st['ad'], a2_ref[d])
        for st in sts:
            d = st['d']
            w_log = -_softplus(-(w0_ref[d:d + 1, :] + st['w_lora'])) - 0.5
            st['lw'] = -jnp.exp(w_log)
            st['a'] = _sigmoid(a0_ref[d:d + 1, :] + st['a_lora'])
            st['kd'] = st['k'] * (1.0 + (st['a'] - 1.0) * ka_ref[...])
            st['bon'] = _head_sum(st['r'] * st['kd'] * rk_ref[...], ones_bd, single_pass=True)
            st['cum'] = _dot_exact_lhs(tri_ref[d], st['lw'])
        out = []
        for st in sts:
            d, r, v, kd, kkn, a, cum = st['d'], st['r'], st['v'], st['kd'], st['kkn'], st['a'], st['cum']
            bon_s[d, pl.ds(st['c0'], L), :] = (st['bon'] * v).astype(bon_s.dtype)
            e_pos = jnp.exp(cum)
            e_neg = jnp.exp(-cum)
            rh = r * e_pos
            kh = kd * e_neg
            bh = (kkn * a) * e_neg
            ah = -kkn * jnp.exp(cum - st['lw'])
            last = L - 1 if d == 0 else 0
            gam = e_pos[last:last + 1, :]
            bhg = bh * gam
            khg = kh * gam
            chains = []
            for p in range(RW_PAIRS):
                cols = slice(p * LANES, (p + 1) * LANES)
                chains.append(dict(
                    d=d, p=p, gam=gam[:, cols], v_st=stack(v[:, cols]),
                    lhs=jnp.concatenate([stack(ah[:, cols]), stack(rh[:, cols])], axis=0),
                    rhs=jnp.concatenate([stack(bh[:, cols]), stack(kh[:, cols])], axis=0),
                    bk=jnp.concatenate([stack(bhg[:, cols]), stack(khg[:, cols])], axis=0)))
            out.append(chains)
        return out

    n_levels = int(math.log2(L))

    def state_free(chains):
        ri = lax.broadcasted_iota(jnp.int32, (L2, L2), 0)
        ci = lax.broadcasted_iota(jnp.int32, (L2, L2), 1)
        eye = jnp.where(ri == ci, 1.0, 0.0)
        rl = jnp.where(ri < L, ri, ri - L)
        cl = jnp.where(ci < L, ci, ci - L)
        strict = (cl < rl, cl > rl)
        incl = (cl <= rl, cl >= rl)
        level = ri ^ ci

        for ch in chains:
            ch['sc'] = _dot_nt(ch['lhs'], ch['rhs'])
        for ch in chains:
            sc, d = ch['sc'], ch['d']
            n_ab = jnp.where(strict[d], sc[0:L2, 0:L2], 0.0)
            m_ak = jnp.where(strict[d], sc[0:L2, L2:2 * L2], 0.0).astype(BF16)
            t_rb = jnp.where(incl[d], sc[L2:2 * L2, 0:L2], 0.0)
            t_rk = jnp.where(incl[d], sc[L2:2 * L2, L2:2 * L2], 0.0)
            ch['t_rbk'] = jnp.concatenate([t_rb, t_rk], axis=1).astype(BF16)
            ch['mv'] = _dot(m_ak, ch['v_st'])
            ch['t'] = eye + jnp.where(level == 1, n_ab, 0.0)
            ch['n_lv'] = [jnp.where(lax.shift_right_logical(level, lv) == 1, n_ab, 0.0).astype(BF16)
                          for lv in range(1, n_levels)]
            del ch['sc']
        for lv in range(n_levels - 1):
            for ch in chains:
                ch['tb'] = ch['t'].astype(BF16)
                ch['x'] = _dot(ch['n_lv'][lv], ch['tb']).astype(BF16)
            for ch in chains:
                ch['t'] = ch['t'] + _dot(ch['tb'], ch['x'])

    def state_step(chains, c0s):
        for ch in chains:
            ch['s_prev'] = s_s[ch['d'], ch['p']]
            ch['ar_s'] = _dot_nt(ch['lhs'], ch['s_prev'].astype(BF16))
        for ch in chains:
            u_rhs = ch['ar_s'][0:L2] + ch['mv']
            u_st = _dot(ch['t'].astype(BF16), u_rhs.astype(BF16)).astype(BF16)
            ch['uv'] = jnp.concatenate([u_st, ch['v_st']], axis=0)
        for ch in chains:
            y_st = ch['ar_s'][L2:2 * L2] + _dot(ch['t_rbk'], ch['uv'])
            ch['y'] = y_st[0:L] + y_st[L:L2]
            s_s[ch['d'], ch['p']] = ch['s_prev'] * ch['gam'] + _dot_tn(ch['uv'], ch['bk'])
        for d in range(2):
            y_dir = jnp.concatenate([ch['y'] for ch in chains if ch['d'] == d], axis=1)
            yd_s[d, pl.ds(c0s[d], L), :] = y_dir.astype(yd_s.dtype)

    steps = 2

    def body(i, carry):
        starts = []
        for j in range(steps):
            step = i * steps + j
            starts.append((pl.multiple_of(step * L, L), pl.multiple_of((n_ch - 1 - step) * L, L)))
        prepared = prep([(c0s[d], d) for c0s in starts for d in range(2)])
        groups = [(prepared[2 * j] + prepared[2 * j + 1], starts[j]) for j in range(steps)]
        state_free([ch for chains, _ in groups for ch in chains])
        for chains, c0s in groups:
            state_step(chains, c0s)
        return carry

    lax.fori_loop(0, n_ch // steps, body, 0)

    def epilogue(i, carry):
        c0 = pl.multiple_of(i * L, L)
        ones_bd = bd_ref[...]
        ys = yd_s[0, pl.ds(c0, L), :].astype(F32) + yd_s[1, pl.ds(c0, L), :].astype(F32)
        mu = _head_sum(ys, ones_bd) * (1.0 / RW_HEAD)
        yc = ys - mu
        var = _head_sum(yc * yc, ones_bd) * (1.0 / RW_HEAD)
        yn = yc * lax.rsqrt(var + RW_LN_EPS) * lnw_ref[...] + lnb_ref[...]
        bonus = bon_s[0, pl.ds(c0, L), :].astype(F32) + bon_s[1, pl.ds(c0, L), :].astype(F32)
        out = (yn + bonus) * g_s[pl.ds(c0, L), :].astype(F32)
        y_ref[pl.ds(c0, L), :] = out.astype(y_ref.dtype)
        return carry

    lax.fori_loop(0, n_ch, epilogue, 0, unroll=min(EPILOGUE_UNROLL, n_ch))
    for d in range(2):
        for p in range(RW_PAIRS):
            s_pair = s_s[d, p]
            sfin_ref[d, 2 * p] = s_pair[0:RW_HEAD, 0:RW_HEAD]
            sfin_ref[d, 2 * p + 1] = s_pair[RW_HEAD:, RW_HEAD:]


def _rwkv_mixer(z, s0, p, n_batch, t_len):
    L = SCAN_CHUNK
    c2 = lambda b: (0, 0)
    c3 = lambda b: (0, 0, 0)
    st_spec = pl.BlockSpec((None, 2, RW_HEADS, RW_HEAD, RW_HEAD), lambda b: (b, 0, 0, 0, 0))
    zero_init = s0 is None
    z_block_bytes = t_len * RW_ZCOLS * z.dtype.itemsize
    return pl.pallas_call(
        functools.partial(_rwkv_kernel, t_len=t_len, zero_init=zero_init),
        grid=(n_batch,),
        in_specs=[pl.BlockSpec((t_len, RW_ZCOLS), lambda b: (b, 0),
                               **(dict(pipeline_mode=pl.Buffered(1)) if z_block_bytes > Z_DOUBLE_BUFFER_MAX else {}))]
        + ([] if zero_init else [st_spec]) + [
            pl.BlockSpec((2, RW_ZCOLS), c2),
            pl.BlockSpec((1, D_RW), c2),
            pl.BlockSpec((1, D_RW), c2),
            pl.BlockSpec((1, D_RW), c2),
            pl.BlockSpec((RW_G_LORA, D_RW), c2),
            pl.BlockSpec((2, D_RW), c2),
            pl.BlockSpec((2, 2, LANES, D_RW), lambda b: (0, 0, 0, 0)),
            pl.BlockSpec((2, D_RW), c2),
            pl.BlockSpec((2, LANES, D_RW), c3),
            pl.BlockSpec((1, D_RW), c2),
            pl.BlockSpec((1, D_RW), c2),
            pl.BlockSpec((2, L, L), c3),
            pl.BlockSpec((LANES, LANES), c2),
        ],
        out_specs=[pl.BlockSpec((t_len, D_RW), lambda b: (b, 0)), st_spec],
        out_shape=[
            jax.ShapeDtypeStruct((n_batch * t_len, D_RW), Y_DTYPE),
            jax.ShapeDtypeStruct((n_batch, 2, RW_HEADS, RW_HEAD, RW_HEAD), F32),
        ],
        scratch_shapes=[
            pltpu.VMEM((2, L + 2 * HALO, RW_ZCOLS), F32),
            pltpu.VMEM((2, RW_PAIRS, LANES, LANES), F32),
            pltpu.VMEM((2, t_len, D_RW), BF16),
            pltpu.VMEM((2, t_len, D_RW), BF16),
            pltpu.VMEM((t_len, D_RW), BF16),
            pltpu.VMEM((t_len, RW_ZCOLS), BF16),
            pltpu.VMEM((t_len, D_RW), BF16),
        ],
        compiler_params=_ARB1,
    )(z, *(() if zero_init else (s0,)), p['mu'], p['kk'], p['ka'], p['rk'], p['g2'], p['w0'], p['w2'], p['a0'],
      p['a2'], p['ln_w'], p['ln_b'], p['tri'], p['ones_bd'])


def _mlstm_kernel(*refs, t_len, zero_init):
    L = ML_SCAN
    n_ch = t_len // L
    z_ref = refs[0]
    (cw_ref, cb_ref, gb_ref, nw_ref, tri_ref, esel_ref, shift_ref, ones_ref,
     y_ref, cfin_ref, nfin_ref, mfin_ref, c_s, n_s, m_s, hd_s, qk_s, gx_s, gr_s) = refs[1 if zero_init else 4:]
    if zero_init:
        c_s[...] = jnp.zeros(c_s.shape, F32)
        n_s[...] = jnp.zeros(n_s.shape, F32)
        m_s[...] = jnp.zeros(m_s.shape, F32)
    else:
        c0_ref, n0_ref, m0_ref = refs[1:4]
        c_s[...] = c0_ref[...]
        m_s[...] = m0_ref[...]
        for d in range(2):
            for h in range(ML_HEADS):
                n_s[d, h] = jnp.broadcast_to(n0_ref[d, h:h + 1, :], (ML_DK, ML_DK)).T
    q_scale = ML_DK ** -0.5
    g_lo = 4 * D_ML
    n_real = 4 * ML_HEADS

    def conv_chunk(i, carry):
        c0 = pl.multiple_of(i * L, L)
        gts = z_ref[pl.ds(c0, L), g_lo:g_lo + LANES].astype(F32) + gb_ref[...]
        x16 = gts.T[0:n_real, :]
        lf_hi, lf_lo = _split2(-_softplus(-x16))
        lane = lax.broadcasted_iota(jnp.int32, (8, L), 1)
        stats = []
        for d in range(2):
            b16 = _dot_nt(lf_hi, tri_ref[d]) + _dot_nt(lf_lo, tri_ref[d])
            x8 = x16[8 * d:8 * d + 8, :]
            b8 = b16[8 * d:8 * d + 8, :]
            g8 = pltpu.roll(x8, ML_HEADS, axis=0) - b8
            stats.append([g8, b8, g8])
        step = 1
        while step < L:
            for d in range(2):
                cm8 = stats[d][2]
                if d == 0:
                    sh = jnp.where(lane >= step, pltpu.roll(cm8, step, axis=1), -jnp.inf)
                else:
                    sh = jnp.where(lane < L - step, pltpu.roll(cm8, L - step, axis=1), -jnp.inf)
                stats[d][2] = jnp.maximum(cm8, sh)
            step *= 2

        ext = _load_ext(z_ref, c0, L, t_len, 0, 2 * D_ML, raw=True)
        taps = _dot(shift_ref[...], ext)
        qk = cb_ref[...] + ext[HALO:HALO + L].astype(F32) * cw_ref[CONV_LEFT:CONV_LEFT + 1, :]
        for n, j in enumerate(jj for jj in range(CONV_W) if jj != CONV_LEFT):
            qk = qk + taps[n * L:(n + 1) * L] * cw_ref[j:j + 1, :]
        qk = _silu(qk)
        qk_s[pl.ds(c0, L), 0:D_ML] = (qk[:, :D_ML] * q_scale).astype(BF16)
        qk_s[pl.ds(c0, L), D_ML:2 * D_ML] = qk[:, D_ML:].astype(BF16)

        for d, (g8, b8, cm8) in enumerate(stats):
            x32 = jnp.concatenate([g8, b8, cm8, jnp.zeros((8, L), F32)], axis=0)
            hi = x32.astype(BF16)
            r1 = x32 - hi.astype(F32)
            mid = r1.astype(BF16)
            lo = (r1 - mid.astype(F32)).astype(BF16)
            gx_s[d, i] = jnp.concatenate([hi, mid, lo], axis=0)
            gr_s[d, i] = g8
        return carry

    lax.fori_loop(0, n_ch, conv_chunk, 0, unroll=min(ML_UNROLL, n_ch))

    def prep(ci, d):
        c0 = pl.multiple_of(ci * L, L)
        q = qk_s[pl.ds(c0, L), 0:D_ML]
        k = qk_s[pl.ds(c0, L), D_ML:2 * D_ML]
        v = z_ref[pl.ds(c0, L), 2 * D_ML:3 * D_ML]
        g8 = gr_s[d, ci]
        cb = _dot_tn(gx_s[d, ci], esel_ref[...])
        ri = lax.broadcasted_iota(jnp.int32, (L, L), 0)
        ci = lax.broadcasted_iota(jnp.int32, (L, L), 1)
        causal = (ci <= ri) if d == 0 else (ci >= ri)
        last = L - 1 if d == 0 else 0
        ones = ones_ref[...]
        chains = []
        for h in range(ML_HEADS):
            cols = slice(h * ML_DK, (h + 1) * ML_DK)
            chains.append(dict(
                d=d, h=h, last=last, causal=causal,
                g_row=g8[ML_HEADS + h:ML_HEADS + h + 1, :],
                g_b=cb[:, cols], b_b=cb[:, D_ML + h * ML_DK:D_ML + (h + 1) * ML_DK],
                cm_b=cb[:, 2 * D_ML + h * ML_DK:2 * D_ML + (h + 1) * ML_DK],
                qb=q[:, cols], kb=k[:, cols],
                v=v[:, cols].astype(F32), v_ones=jnp.concatenate([v[:, cols].astype(BF16), ones], axis=1)))
        return chains

    def body(i, carry):
        c0s = (pl.multiple_of(i * L, L), pl.multiple_of((n_ch - 1 - i) * L, L))
        chains = prep(i, 0) + prep(n_ch - 1 - i, 1)
        for ch in chains:
            d, h = ch['d'], ch['h']
            ch['qk'] = _dot_nt(ch['qb'], ch['kb'])
            ch['c_prev'] = c_s[d, h]
            ch['n_prev'] = n_s[d, h]
            ch['m_prev'] = m_s[d, h:h + 1, :]
            ch['qcn'] = _dot(ch['qb'], jnp.concatenate([ch['c_prev'].astype(BF16),
                                                        ch['n_prev'].astype(BF16)], axis=1))
        for ch in chains:
            mu = jnp.maximum(ch['m_prev'], ch['cm_b'])
            ch['mu'] = mu
            ch['s'] = ch['qk'] * jnp.exp(jnp.where(ch['causal'], ch['g_row'] - mu, -jnp.inf))
            ch['w_inter'] = jnp.exp(ch['m_prev'] - mu)
        for ch in chains:
            last = ch['last']
            ch['sv'] = _dot(ch['s'].astype(BF16), ch['v_ones'])
            mu_last = ch['mu'][last:last + 1, :]
            ch['m_new'] = ch['b_b'][last:last + 1, :] + mu_last
            ch['dec'] = jnp.exp(ch['m_prev'] - mu_last)
            w_s = jnp.exp(ch['g_b'] - mu_last)
            ch['upd'] = _dot_tn(ch['kb'], jnp.concatenate([(w_s * ch['v']).astype(BF16),
                                                           w_s.astype(BF16)], axis=1))
        for ch in chains:
            d, h = ch['d'], ch['h']
            num = ch['sv'][:, :ML_DK] + ch['w_inter'] * ch['qcn'][:, :ML_DK]
            den = ch['sv'][:, ML_DK:] + ch['w_inter'] * ch['qcn'][:, ML_DK:]
            ch['out'] = num / jnp.maximum(jnp.abs(den), jnp.exp(-(ch['b_b'] + ch['mu'])))
            c_s[d, h] = ch['dec'] * ch['c_prev'] + ch['upd'][:, :ML_DK]
            n_s[d, h] = ch['dec'] * ch['n_prev'] + ch['upd'][:, ML_DK:]
            m_s[d, h:h + 1, :] = ch['m_new']
        for d in range(2):
            h_dir = jnp.concatenate([ch['out'] for ch in chains if ch['d'] == d], axis=1)
            hd_s[d, pl.ds(c0s[d], L), :] = h_dir.astype(hd_s.dtype)
        return carry

    lax.fori_loop(0, n_ch, body, 0, unroll=min(ML_UNROLL, n_ch))

    def epilogue(i, carry):
        c0 = pl.multiple_of(i * L, L)
        ones = ones_ref[...]
        hs = hd_s[0, pl.ds(c0, L), :].astype(F32) + hd_s[1, pl.ds(c0, L), :].astype(F32)
        o = z_ref[pl.ds(c0, L), 3 * D_ML:4 * D_ML].astype(F32)
        mu = _head_sum(hs, ones) * (1.0 / ML_DK)
        xc = hs - mu
        var = _head_sum(xc * xc, ones) * (1.0 / ML_DK)
        y = xc * lax.rsqrt(var + NORM_EPS) * nw_ref[...] * _sigmoid(o)
        y_ref[pl.ds(c0, L), :] = y.astype(y_ref.dtype)
        return carry

    lax.fori_loop(0, n_ch, epilogue, 0, unroll=min(EPILOGUE_UNROLL, n_ch))
    cfin_ref[...] = c_s[...]
    mfin_ref[...] = m_s[...]
    for d in range(2):
        for h in range(ML_HEADS):
            nfin_ref[d, h:h + 1, :] = n_s[d, h].T[0:1, :]


def _mlstm_mixer(z, states, p, n_batch, t_len):
    assert z.dtype == BF16
    L = ML_SCAN
    c2 = lambda b: (0, 0)
    c3 = lambda b: (0, 0, 0)
    c_spec = pl.BlockSpec((None, 2, ML_HEADS, ML_DK, ML_DK), lambda b: (b, 0, 0, 0, 0))
    m_spec = pl.BlockSpec((None, 2, ML_HEADS, ML_DK), lambda b: (b, 0, 0, 0))
    zero_init = states is None
    state_specs = [] if zero_init else [c_spec, m_spec, m_spec]
    return pl.pallas_call(
        functools.partial(_mlstm_kernel, t_len=t_len, zero_init=zero_init),
        grid=(n_batch,),
        in_specs=[pl.BlockSpec((t_len, ML_ZCOLS), lambda b: (b, 0))] + state_specs + [
            pl.BlockSpec((CONV_W, 2 * D_ML), c2),
            pl.BlockSpec((1, 2 * D_ML), c2),
            pl.BlockSpec((1, LANES), c2),
            pl.BlockSpec((1, D_ML), c2),
            pl.BlockSpec((2, L, L), c3),
            pl.BlockSpec(p['esel'].shape, c2),
            pl.BlockSpec(p['shift'].shape, c2),
            pl.BlockSpec((L, LANES), c2),
        ],
        out_specs=[pl.BlockSpec((t_len, D_ML), lambda b: (b, 0)), c_spec, m_spec, m_spec],
        out_shape=[
            jax.ShapeDtypeStruct((n_batch * t_len, D_ML), Y_DTYPE),
            jax.ShapeDtypeStruct((n_batch, 2, ML_HEADS, ML_DK, ML_DK), F32),
            jax.ShapeDtypeStruct((n_batch, 2, ML_HEADS, ML_DK), F32),
            jax.ShapeDtypeStruct((n_batch, 2, ML_HEADS, ML_DK), F32),
        ],
        scratch_shapes=[
            pltpu.VMEM((2, ML_HEADS, ML_DK, ML_DK), F32),
            pltpu.VMEM((2, ML_HEADS, ML_DK, ML_DK), F32),
            pltpu.VMEM((2, ML_HEADS, ML_DK), F32),
            pltpu.VMEM((2, t_len, D_ML), BF16),
            pltpu.VMEM((t_len, 2 * D_ML), BF16),
            pltpu.VMEM((2, t_len // L, 96, L), BF16),
            pltpu.VMEM((2, t_len // L, 8, L), F32),
        ],
        compiler_params=_ARB1,
    )(z, *(() if zero_init else states), p['conv_w'], p['conv_b'], p['gate_b'], p['norm_w'], p['tri'], p['esel'],
      p['shift'], p['ones'])


def _conv_shift(n):
    taps = jnp.array([j for j in range(CONV_W) if j != CONV_LEFT])
    t = jnp.arange((CONV_W - 1) * n)
    src = HALO + t % n + taps[t // n] - CONV_LEFT
    return (src[:, None] == jnp.arange(n + 2 * HALO)[None, :]).astype(BF16)


def _mlstm_consts():
    L = ML_SCAN
    lower = jnp.tril(jnp.ones((L, L), F32))
    tri = jnp.stack([lower, lower.T]).astype(BF16)
    rows = jnp.arange(32)
    blk = jnp.arange(3 * D_ML) // ML_DK
    src_row = 8 * (blk // ML_HEADS) + ML_HEADS + blk % ML_HEADS
    esel = (rows[:, None] == src_row[None, :]).astype(BF16)
    esel = jnp.concatenate([esel, esel, esel], axis=0)
    return dict(tri=tri, esel=esel, shift=_conv_shift(L), ones=jnp.ones((L, LANES), BF16))


def _block_diag(w):
    n, c, d = w.shape
    return jnp.einsum('ncd,nm->ncmd', w, jnp.eye(n, dtype=w.dtype)).reshape(n * c, n * d)


def _scan_tri():
    L = SCAN_CHUNK
    lower = jnp.tril(jnp.ones((L, L), F32))
    return jnp.stack([lower, lower.T]).astype(BF16)


def _prep_even(ev_w_in, pool_w, pool_scale, rg_conv_w, rg_conv_b, rg_wa, rg_ba, rg_wx, rg_bx, rg_lam, ev_w_out, j):
    wg = jnp.concatenate([_block_diag(rg_wa[j, 0]), _block_diag(rg_wx[j, 0]),
                          _block_diag(rg_wa[j, 1]), _block_diag(rg_wx[j, 1])], axis=1)
    bg = jnp.concatenate([rg_ba[j, 0], rg_bx[j, 0], rg_ba[j, 1], rg_bx[j, 1]]).reshape(1, 4 * D_RG)
    return dict(
        w_in=[ev_w_in[j].astype(BF16)],
        w_out=ev_w_out[j].astype(BF16),
        pool_w=pool_w[j].astype(BF16),
        pool_scale=pool_scale[j].reshape(1, D_POOL),
        conv_w=rg_conv_w[j],
        conv_b=rg_conv_b[j].reshape(1, D_RG),
        wg=wg.astype(BF16),
        bg=bg,
        lam=rg_lam[j],
    )


def _pad_cols(w, cuts, widths):
    pieces = jnp.split(w, cuts, axis=-1)
    out = []
    for piece, width in zip(pieces, widths):
        extra = width - piece.shape[-1]
        if extra:
            piece = jnp.pad(piece, [(0, 0)] * (piece.ndim - 1) + [(0, extra)])
        out.append(piece)
    return jnp.concatenate(out, axis=-1)


def _prep_odd(od_w_in, rw_mu, rw_w0, rw_w2, rw_a0, rw_a2, rw_kk, rw_ka, rw_rk, rw_g2, rw_ln_w, rw_ln_b,
              ml_conv_w, ml_conv_b, ml_bi, ml_bf, ml_norm_w, od_w_out, j):
    rw_cols = 3 * D_RW + RW_G_LORA + RW_W_LORA + RW_A_LORA
    base = 3 * D_RW + RW_G_LORA
    cuts = [base, base + RW_W_LORA]
    widths = [base, LANES, LANES]
    w_rw = _pad_cols(od_w_in[j][:, :rw_cols], cuts, widths)
    w_ml = _pad_cols(od_w_in[j][:, rw_cols:], [4 * D_ML], [4 * D_ML, LANES])
    pad_rows = lambda w: jnp.pad(w, ((0, 0), (0, LANES - w.shape[1]), (0, 0)))
    gate_b = jnp.concatenate([ml_bi[j, 0], ml_bf[j, 0], ml_bi[j, 1], ml_bf[j, 1]])
    gate_b = jnp.pad(gate_b, (0, LANES - gate_b.shape[0])).reshape(1, LANES)
    hm = jnp.arange(LANES) // RW_HEAD
    rw = dict(
        mu=_pad_cols(rw_mu[j], cuts, widths),
        kk=rw_kk[j].reshape(1, D_RW), ka=rw_ka[j].reshape(1, D_RW), rk=rw_rk[j].reshape(1, D_RW),
        g2=rw_g2[j].astype(BF16),
        w0=rw_w0[j], w2=jnp.stack(_split2(pad_rows(rw_w2[j])), axis=1),
        a0=rw_a0[j], a2=pad_rows(rw_a2[j]).astype(BF16),
        ln_w=rw_ln_w[j].reshape(1, D_RW), ln_b=rw_ln_b[j].reshape(1, D_RW),
        tri=_scan_tri(),
        ones_bd=(hm[:, None] == hm[None, :]).astype(BF16),
    )
    ml = dict(
        conv_w=ml_conv_w[j], conv_b=ml_conv_b[j].reshape(1, 2 * D_ML),
        gate_b=gate_b, norm_w=ml_norm_w[j].reshape(1, D_ML), **_mlstm_consts(),
    )
    return dict(w_in=[w_rw.astype(BF16), w_ml.astype(BF16)], w_out=od_w_out[j].astype(BF16), rw=rw, ml=ml)


def _trunk(x, mod, rows_per_mod, mod_base, grid, states, params, n_batch, t_len):
    ev, od, dense = params
    rg0, rw0, ml0 = states
    x = x.reshape(n_batch * t_len, D_MODEL)
    spec = (rows_per_mod, mod_base)

    (z,) = _in_proj(x, mod[0], *spec, dense['norm1_w'][0], ev['w_in'])
    y_ev, rg_f = _even_mixer(z, rg0, ev, n_batch, t_len, grid)
    x = _out_mlp([y_ev], x, mod[0], *spec, dense['norm2_w'][0], ev['w_out'],
                 dense['mlp_w1'], dense['mlp_w2'], 0, dense['final_norm_w'], False)

    z_rw, z_ml = _in_proj(x, mod[1], *spec, dense['norm1_w'][1], od['w_in'])
    y_rw, s_f = _rwkv_mixer(z_rw, rw0, od['rw'], n_batch, t_len)
    y_ml, c_f, n_f, m_f = _mlstm_mixer(z_ml, ml0, od['ml'], n_batch, t_len)
    y = _out_mlp([y_rw, y_ml], x, mod[1], *spec, dense['norm2_w'][1], od['w_out'],
                 dense['mlp_w1'], dense['mlp_w2'], 1, dense['final_norm_w'], True)
    return y.reshape(n_batch, t_len, D_MODEL), (rg_f, s_f, c_f, n_f, m_f)


def kernel(x_prompt, x_sample, state_rglru, state_rwkv, state_mlstm_C, state_mlstm_n, state_mlstm_m,
           c, c_ctx, norm1_w, norm2_w, w_mod, b_mod, mlp_w1, mlp_w2, final_norm_w,
           ev_w_in, pool_w, pool_scale, rg_conv_w, rg_conv_b, rg_wa, rg_ba, rg_wx, rg_bx, rg_lam, ev_w_out,
           od_w_in, rw_mu, rw_w0, rw_w2, rw_a0, rw_a2, rw_kk, rw_ka, rw_rk, rw_g2, rw_ln_w, rw_ln_b,
           ml_conv_w, ml_conv_b, ml_bi, ml_bf, ml_norm_w, od_w_out):
    bp, tp, _ = x_prompt.shape
    bs, ts, _ = x_sample.shape

    assert bs + 1 <= N_COND_ROWS and ts % ROW_BLOCK == 0 and (bp * tp) % ROW_BLOCK == 0
    cond = jnp.concatenate([c_ctx[None, :], c, jnp.zeros((N_COND_ROWS - 1 - bs, D_MODEL), F32)], axis=0)
    mod = _modulation(cond, w_mod, b_mod)

    ev = _prep_even(ev_w_in, pool_w, pool_scale, rg_conv_w, rg_conv_b, rg_wa, rg_ba, rg_wx, rg_bx, rg_lam,
                    ev_w_out, 0)
    od = _prep_odd(od_w_in, rw_mu, rw_w0, rw_w2, rw_a0, rw_a2, rw_kk, rw_ka, rw_rk, rw_g2, rw_ln_w, rw_ln_b,
                   ml_conv_w, ml_conv_b, ml_bi, ml_bf, ml_norm_w, od_w_out, 0)
    dense = dict(norm1_w=norm1_w, norm2_w=norm2_w, mlp_w1=mlp_w1.astype(BF16), mlp_w2=mlp_w2.astype(BF16),
                 final_norm_w=final_norm_w)
    params = (ev, od, dense)

    zero_states = (jnp.zeros((bp, 2, D_RG), F32), None, None)
    y_prompt, (rg_f, s_f, c_f, n_f, m_f) = _trunk(
        x_prompt, mod, bp * tp, 0, False, zero_states, params, bp, tp)

    sample_states = (
        state_rglru[:, 0],
        state_rwkv[:, 0],
        (state_mlstm_C[:, 0], state_mlstm_n[:, 0],
         jnp.broadcast_to(state_mlstm_m[:, 0][..., None], (bs, 2, ML_HEADS, ML_DK))),
    )
    y_sample, _ = _trunk(x_sample, mod, ts, 1, True, sample_states, params, bs, ts)

    return (y_prompt, y_sample,
            rg_f[:, None],
            s_f[:, None],
            c_f[:, None],
            n_f[:, None],
            m_f[:, None, :, :, 0])
```

```python
import functools
import math

import jax
import jax.numpy as jnp
from jax import lax
from jax.experimental import pallas as pl
from jax.experimental.pallas import tpu as pltpu

F32 = jnp.float32
BF16 = jnp.bfloat16

D_MODEL = 1024
DEPTH = 2
GRID_W = 64
D_FF = 4 * D_MODEL
NORM_EPS = 1e-6
N_MOD = 6

D_POOL = D_MODEL // 2
POOL_WINDOWS = (2, 4, 8, 16)
POOL_GW = D_POOL // len(POOL_WINDOWS)
D_RG = D_MODEL // 2
RG_BLOCKS = 8
RG_C = 8.0
CONV_W = 4
CONV_LEFT = 2
EV_COLS = D_POOL + 2 * D_RG

D_RW = D_MODEL // 2
RW_HEAD = 64
RW_HEADS = D_RW // RW_HEAD
RW_PAIRS = RW_HEADS // 2
RW_W_LORA = 64
RW_A_LORA = 64
RW_G_LORA = 128
RW_LN_EPS = 64e-5
RW_ZCOLS = 3 * D_RW + RW_G_LORA + 128 + 128
D_ML = D_MODEL // 2
ML_HEADS = 4
ML_DK = D_ML // ML_HEADS
ML_ZCOLS = 4 * D_ML + 128

LANES = 128
HALO = 16
Z_DTYPE = BF16
Y_DTYPE = BF16
ROW_BLOCK = 512
IN_ROW_BLOCK = 1024
SCAN_CHUNK = 64
ML_SCAN = 128
RG_CHUNK = 256
POOL_SEGS_PER_TRIP = 8
ML_UNROLL = 4
EPILOGUE_UNROLL = 8
MOD_COL_BLOCK = 1536
N_COND_ROWS = 16
Z_DOUBLE_BUFFER_MAX = 8 * 1024 * 1024
VMEM_LIMIT = 56 * 1024 * 1024

_ARB1 = pltpu.CompilerParams(dimension_semantics=("arbitrary",), vmem_limit_bytes=VMEM_LIMIT)
_ARB2 = pltpu.CompilerParams(dimension_semantics=("arbitrary", "arbitrary"), vmem_limit_bytes=VMEM_LIMIT)


def _dot(a, b):
    return jnp.dot(a, b, preferred_element_type=F32)


def _dot_nt(a, b):
    return lax.dot_general(a, b, (((1,), (1,)), ((), ())), preferred_element_type=F32)


def _dot_tn(a, b):
    return lax.dot_general(a, b, (((0,), (0,)), ((), ())), preferred_element_type=F32)


def _split2(x):
    hi = x.astype(BF16)
    lo = (x - hi.astype(F32)).astype(BF16)
    return hi, lo


def _dot_f32(a, b):
    ah, al = _split2(a)
    bh, bl = _split2(b)
    return _dot(ah, bh) + (_dot(ah, bl) + _dot(al, bh))


def _dot_exact_lhs(m_bf16, x):
    hi, lo = _split2(x)
    return _dot(m_bf16, hi) + _dot(m_bf16, lo)


def _dot_exact_rhs(x, m_bf16):
    hi, lo = _split2(x)
    return _dot(hi, m_bf16) + _dot(lo, m_bf16)


def _softplus(x):
    return jnp.maximum(x, 0.0) + jnp.log1p(jnp.exp(-jnp.abs(x)))


def _sigmoid(x):
    return jax.nn.sigmoid(x)


def _silu(x):
    return x * jax.nn.sigmoid(x)


def _gelu_tanh(x):
    c = math.sqrt(2.0 / math.pi)
    return x * (0.5 * (1.0 + jnp.tanh(c * (x + 0.044715 * (x * x * x)))))


def _rms_norm(x, w):
    ms = jnp.mean(x * x, axis=-1, keepdims=True)
    return x * lax.rsqrt(ms + NORM_EPS) * w


def _load_ext(z_ref, c0, n, t_len, lo, hi, raw=False):
    dt = z_ref.dtype if raw else F32
    cur = z_ref[pl.ds(c0, n), lo:hi].astype(dt)
    pb = pl.multiple_of(jnp.maximum(c0 - HALO, 0), HALO)
    pa = pl.multiple_of(jnp.minimum(c0 + n, t_len - HALO), HALO)
    before = z_ref[pl.ds(pb, HALO), lo:hi].astype(dt)
    after = z_ref[pl.ds(pa, HALO), lo:hi].astype(dt)
    before = jnp.where(c0 > 0, before, jnp.zeros_like(before))
    after = jnp.where(c0 + n < t_len, after, jnp.zeros_like(after))
    return jnp.concatenate([before, cur, after], axis=0)


def _head_sum(x, ones_bd, single_pass=False):
    n = x.shape[0]
    groups = x.shape[1] // LANES
    xs = jnp.concatenate([x[:, g * LANES:(g + 1) * LANES] for g in range(groups)], axis=0)
    s = _dot(xs.astype(BF16), ones_bd) if single_pass else _dot_exact_rhs(xs, ones_bd)
    return jnp.concatenate([s[g * n:(g + 1) * n] for g in range(groups)], axis=1)


def _mod_kernel(c_ref, w_ref, b_ref, o_ref):
    s = _silu(c_ref[...])
    o_ref[...] = _dot_f32(s, w_ref[...]) + b_ref[...]


def _modulation(cond, w_mod, b_mod):
    rows = cond.shape[0]
    tn = MOD_COL_BLOCK
    out = pl.pallas_call(
        _mod_kernel,
        grid=(DEPTH, N_MOD * D_MODEL // tn),
        in_specs=[
            pl.BlockSpec((rows, D_MODEL), lambda l, j: (0, 0)),
            pl.BlockSpec((None, D_MODEL, tn), lambda l, j: (l, 0, j)),
            pl.BlockSpec((None, 1, tn), lambda l, j: (l, 0, j)),
        ],
        out_specs=pl.BlockSpec((None, rows, tn), lambda l, j: (l, 0, j)),
        out_shape=jax.ShapeDtypeStruct((DEPTH, rows, N_MOD * D_MODEL), F32),
        compiler_params=_ARB2,
    )(cond, w_mod, b_mod.reshape(DEPTH, 1, N_MOD * D_MODEL))
    return out.reshape(DEPTH, rows, N_MOD, D_MODEL)


def _inproj_kernel(*refs, n_out, col_chunk):
    x_ref, mod_ref, nw_ref = refs[:3]
    w_refs = refs[3:3 + n_out]
    z_refs = refs[3 + n_out:]
    h = _rms_norm(x_ref[...], nw_ref[...])
    h = h * (1.0 + mod_ref[1:2, :]) + mod_ref[0:1, :]
    hb = h.astype(BF16)
    for w_ref, z_ref in zip(w_refs, z_refs):
        n = w_ref.shape[1]
        for c in range(0, n, col_chunk):
            e = min(c + col_chunk, n)
            z_ref[:, c:e] = _dot(hb, w_ref[:, c:e]).astype(z_ref.dtype)


def _in_proj(x, mod, rows_per_mod, mod_base, norm_w, weights):
    m = x.shape[0]
    tm = IN_ROW_BLOCK
    n_out = len(weights)
    const = lambda i: (0, 0)
    in_specs = [
        pl.BlockSpec((tm, D_MODEL), lambda i: (i, 0)),
        pl.BlockSpec((None, N_MOD, D_MODEL), lambda i: (mod_base + (i * tm) // rows_per_mod, 0, 0)),
        pl.BlockSpec((1, D_MODEL), const),
    ] + [pl.BlockSpec(w.shape, const, pipeline_mode=pl.Buffered(1)) for w in weights]
    out_specs = [pl.BlockSpec((tm, w.shape[1]), lambda i: (i, 0)) for w in weights]
    out_shape = [jax.ShapeDtypeStruct((m, w.shape[1]), Z_DTYPE) for w in weights]
    return pl.pallas_call(
        functools.partial(_inproj_kernel, n_out=n_out, col_chunk=512),
        grid=(m // tm,),
        in_specs=in_specs,
        out_specs=out_specs,
        out_shape=out_shape,
        compiler_params=_ARB1,
    )(x, mod, norm_w.reshape(1, D_MODEL), *weights)


def _outmlp_kernel(*refs, n_y, final, ff_chunk):
    y_refs = refs[:n_y]
    x_ref, mod_ref, nw_ref, wo_ref, w1_ref, w2_ref, fw_ref, o_ref = refs[n_y:]
    y = None
    row = 0
    for y_ref in y_refs:
        k = y_ref.shape[1]
        part = _dot(y_ref[...], wo_ref[row:row + k, :])
        y = part if y is None else y + part
        row += k
    x = x_ref[...] + mod_ref[2:3, :] * y
    h = _rms_norm(x, nw_ref[...]) * (1.0 + mod_ref[4:5, :]) + mod_ref[3:4, :]
    hb = h.astype(BF16)
    acc = None
    for c in range(0, D_FF, ff_chunk):
        u = jnp.maximum(_dot(hb, w1_ref[:, c:c + ff_chunk]), 0.0)
        part = _dot((u * u).astype(BF16), w2_ref[c:c + ff_chunk, :])
        acc = part if acc is None else acc + part
    x = x + mod_ref[5:6, :] * acc
    if final:
        x = _rms_norm(x, fw_ref[...])
    o_ref[...] = x


def _out_mlp(ys, x, mod, rows_per_mod, mod_base, norm_w, w_out, w1, w2, layer, final_w, final):
    m = x.shape[0]
    tm = ROW_BLOCK
    const = lambda i: (0, 0)
    single = dict(pipeline_mode=pl.Buffered(1))
    pick = lambda i: (layer, 0, 0)
    in_specs = [pl.BlockSpec((tm, y.shape[1]), lambda i: (i, 0)) for y in ys] + [
        pl.BlockSpec((tm, D_MODEL), lambda i: (i, 0)),
        pl.BlockSpec((None, N_MOD, D_MODEL), lambda i: (mod_base + (i * tm) // rows_per_mod, 0, 0)),
        pl.BlockSpec((1, D_MODEL), const),
        pl.BlockSpec(w_out.shape, const, **single),
        pl.BlockSpec((None,) + w1.shape[1:], pick, **single),
        pl.BlockSpec((None,) + w2.shape[1:], pick, **single),
        pl.BlockSpec((1, D_MODEL), const),
    ]
    return pl.pallas_call(
        functools.partial(_outmlp_kernel, n_y=len(ys), final=final, ff_chunk=1024),
        grid=(m // tm,),
        in_specs=in_specs,
        out_specs=pl.BlockSpec((tm, D_MODEL), lambda i: (i, 0)),
        out_shape=jax.ShapeDtypeStruct((m, D_MODEL), F32),
        compiler_params=_ARB1,
    )(*ys, x, mod, norm_w.reshape(1, D_MODEL), w_out, w1, w2, final_w.reshape(1, D_MODEL))


def _even_kernel(z_ref, h0_ref, pw_ref, ps_ref, cw_ref, cb_ref, wg_ref, bg_ref, lam_ref, band_ref,
                 y_ref, hfin_ref, cbuf, a_s, b_s, h_s, hf_s, *, t_len, grid):
    seg = GRID_W if grid else t_len
    n_seg = t_len // seg
    n_groups = len(POOL_WINDOWS)

    pos = lax.broadcasted_iota(jnp.int32, (seg, POOL_GW), 0)
    offs = [(w // 2, w - 1 - w // 2) for w in POOL_WINDOWS]
    gcols = [slice(g * POOL_GW, (g + 1) * POOL_GW) for g in range(n_groups)]

    def grid_row(rr, cols):
        valid = jnp.logical_and(rr >= 0, rr < n_seg)
        src = pl.multiple_of(jnp.clip(rr, 0, n_seg - 1) * seg, seg)
        return jnp.where(valid, z_ref[pl.ds(src, seg), cols].astype(F32), 0.0)

    def pool_rows(rows, run):
        items = []
        for r in rows:
            base = r * seg if isinstance(r, int) else pl.multiple_of(r * seg, seg)
            new_run = []
            for g, (lo_off, hi_off) in enumerate(offs):
                cols = gcols[g]
                if grid:
                    s1 = run[g] + grid_row(r + hi_off, cols) - grid_row(r - lo_off - 1, cols)
                    new_run.append(s1)
                    cnt_r = jnp.minimum(r + hi_off, n_seg - 1) - jnp.maximum(r - lo_off, 0) + 1
                    m1 = s1 / cnt_r.astype(F32)
                    hi = m1.astype(BF16)
                    r1 = m1 - hi.astype(F32)
                    mid = r1.astype(BF16)
                    parts = jnp.concatenate([hi, mid, (r1 - mid.astype(F32)).astype(BF16)], axis=0)
                else:
                    parts = z_ref[pl.ds(base, seg), cols]
                items.append(dict(g=g, base=base, parts=parts))
            run = new_run
        for it in items:
            it['s2'] = _dot(band_ref[it['g']], it['parts'])
        for it in items:
            lo_off, hi_off = offs[it['g']]
            cnt_c = jnp.minimum(pos + hi_off, seg - 1) - jnp.maximum(pos - lo_off, 0) + 1
            xg = z_ref[pl.ds(it['base'], seg), gcols[it['g']]].astype(F32)
            it['d'] = (it['s2'] / cnt_c.astype(F32) - xg).astype(BF16)
        for it in items:
            it['y'] = _dot(it['d'], pw_ref[it['g']])
        for it in items:
            cols = gcols[it['g']]
            y_ref[pl.ds(it['base'], seg), cols] = (it['y'] * ps_ref[:, cols]).astype(y_ref.dtype)
        return tuple(run) if grid else None

    if grid:
        init = []
        for g, (lo_off, hi_off) in enumerate(offs):
            s1 = jnp.zeros((seg, POOL_GW), F32)
            for o in range(min(hi_off, n_seg)):
                s1 = s1 + z_ref[o * seg:(o + 1) * seg, gcols[g]].astype(F32)
            init.append(s1)
        per_trip = POOL_SEGS_PER_TRIP
        lax.fori_loop(0, n_seg // per_trip,
                      lambda i, run: pool_rows([i * per_trip + j for j in range(per_trip)], run), tuple(init))
    else:
        pool_rows([0], None)

    ch = RG_CHUNK
    n_ch = t_len // ch
    rg_lo, rg_hi = D_POOL, D_POOL + D_RG

    def gates(c0, d):
        cbuf[...] = _load_ext(z_ref, c0, ch, t_len, rg_lo, rg_hi)
        xc = cb_ref[...]
        for j in range(CONV_W):
            off = HALO - CONV_LEFT + j
            xc = xc + cbuf[off:off + ch, :] * cw_ref[j:j + 1, :]
        g = _dot(xc.astype(BF16), wg_ref[:, d * 2 * D_RG:(d + 1) * 2 * D_RG]) \
            + bg_ref[:, d * 2 * D_RG:(d + 1) * 2 * D_RG]
        r = _sigmoid(g[:, :D_RG])
        i = _sigmoid(g[:, D_RG:])
        log_a = (-RG_C) * r * _softplus(-lam_ref[d:d + 1, :])
        a = jnp.exp(log_a)
        a_s[...] = a
        one_minus_a2 = jnp.tanh(-log_a) * (a * a + 1.0)
        root = jnp.where(one_minus_a2 > 0.0, one_minus_a2 * lax.rsqrt(one_minus_a2), 0.0)
        b_s[...] = root * (i * xc)

    def scan_rows(h, reverse):
        def row(t, h):
            tt = (ch - 1 - t) if reverse else t
            h = a_s[pl.ds(tt, 1), :] * h + b_s[pl.ds(tt, 1), :]
            h_s[pl.ds(tt, 1), :] = h
            return h
        return lax.fori_loop(0, ch, row, h, unroll=8)

    def fwd_body(i, h):
        c0 = pl.multiple_of(i * ch, ch)
        gates(c0, 0)
        h = scan_rows(h, False)
        hf_s[pl.ds(c0, ch), :] = h_s[...]
        return h

    h_fwd = lax.fori_loop(0, n_ch, fwd_body, h0_ref[0:1, :])

    def bwd_body(i, h):
        c0 = pl.multiple_of((n_ch - 1 - i) * ch, ch)
        gates(c0, 1)
        h = scan_rows(h, True)
        u_gate = z_ref[pl.ds(c0, ch), rg_hi:rg_hi + D_RG].astype(F32)
        y = (hf_s[pl.ds(c0, ch), :] + h_s[...]) * _gelu_tanh(u_gate)
        y_ref[pl.ds(c0, ch), D_POOL:D_POOL + D_RG] = y.astype(y_ref.dtype)
        return h

    h_bwd = lax.fori_loop(0, n_ch, bwd_body, h0_ref[1:2, :])
    hfin_ref[0:1, :] = h_fwd
    hfin_ref[1:2, :] = h_bwd


def _even_mixer(z, h0, p, n_batch, t_len, grid):
    assert z.dtype == BF16
    seg = GRID_W if grid else t_len
    const2 = lambda b: (0, 0)
    const3 = lambda b: (0, 0, 0)
    idx = jnp.arange(seg)
    delta = idx[None, :] - idx[:, None]
    band = jnp.stack([jnp.logical_and(delta >= -(w // 2), delta <= w - 1 - w // 2) for w in POOL_WINDOWS])
    band = jnp.tile(band.astype(BF16), (1, 1, 3 if grid else 1))
    return pl.pallas_call(
        functools.partial(_even_kernel, t_len=t_len, grid=grid),
        grid=(n_batch,),
        in_specs=[
            pl.BlockSpec((t_len, EV_COLS), lambda b: (b, 0)),
            pl.BlockSpec((None, 2, D_RG), lambda b: (b, 0, 0)),
            pl.BlockSpec(p['pool_w'].shape, const3),
            pl.BlockSpec((1, D_POOL), const2),
            pl.BlockSpec((CONV_W, D_RG), const2),
            pl.BlockSpec((1, D_RG), const2),
            pl.BlockSpec((D_RG, 4 * D_RG), const2),
            pl.BlockSpec((1, 4 * D_RG), const2),
            pl.BlockSpec((2, D_RG), const2),
            pl.BlockSpec(band.shape, const3),
        ],
        out_specs=[
            pl.BlockSpec((t_len, D_POOL + D_RG), lambda b: (b, 0)),
            pl.BlockSpec((None, 2, D_RG), lambda b: (b, 0, 0)),
        ],
        out_shape=[
            jax.ShapeDtypeStruct((n_batch * t_len, D_POOL + D_RG), Y_DTYPE),
            jax.ShapeDtypeStruct((n_batch, 2, D_RG), F32),
        ],
        scratch_shapes=[
            pltpu.VMEM((RG_CHUNK + 2 * HALO, D_RG), F32),
            pltpu.VMEM((RG_CHUNK, D_RG), F32),
            pltpu.VMEM((RG_CHUNK, D_RG), F32),
            pltpu.VMEM((RG_CHUNK, D_RG), F32),
            pltpu.VMEM((t_len, D_RG), F32),
        ],
        compiler_params=_ARB1,
    )(z, h0, p['pool_w'], p['pool_scale'], p['conv_w'], p['conv_b'], p['wg'], p['bg'], p['lam'], band)


def _rwkv_kernel(*refs, t_len, zero_init):
    L = SCAN_CHUNK
    n_ch = t_len // L
    L2 = 2 * L
    z_ref = refs[0]
    (mu_ref, kkw_ref, ka_ref, rk_ref, g2_ref, w0_ref, w2_ref, a0_ref, a2_ref, lnw_ref, lnb_ref, tri_ref, bd_ref,
     y_ref, sfin_ref, ebuf, s_s, yd_s, bon_s, g_s, zs_s, kkn_s) = refs[1 if zero_init else 2:]
    if zero_init:
        s_s[...] = jnp.zeros(s_s.shape, F32)
    else:
        s0_ref = refs[1]
        zero = jnp.zeros((RW_HEAD, RW_HEAD), F32)
        for d in range(2):
            for p in range(RW_PAIRS):
                top = jnp.concatenate([s0_ref[d, 2 * p], zero], axis=1)
                bot = jnp.concatenate([zero, s0_ref[d, 2 * p + 1]], axis=1)
                s_s[d, p] = jnp.concatenate([top, bot], axis=0)

    def stack(x):
        first_head = lax.broadcasted_iota(jnp.int32, (L, LANES), 1) < RW_HEAD
        return jnp.concatenate([jnp.where(first_head, x, 0.0), jnp.where(first_head, 0.0, x)],
                               axis=0).astype(BF16)

    def shift_pair(i, carry):
        ones_bd = bd_ref[...]
        o = 3 * D_RW
        sts = []
        for n in range(2):
            c0 = pl.multiple_of((2 * i + n) * L, L)
            ebuf[n] = _load_ext(z_ref, c0, L, t_len, 0, RW_ZCOLS)
            cur = ebuf[n, HALO:HALO + L, :]
            prev = ebuf[n, HALO - 1:HALO - 1 + L, :]
            nxt = ebuf[n, HALO + 1:HALO + 1 + L, :]
            zs = cur + mu_ref[0:1, :] * (prev - cur) + mu_ref[1:2, :] * (nxt - cur)
            zs_s[pl.ds(c0, L), :] = zs.astype(BF16)
            kkv = zs[:, D_RW:2 * D_RW] * kkw_ref[...]
            sts.append(dict(c0=c0, kkv=kkv, ss=_head_sum(kkv * kkv, ones_bd),
                            g=_dot(_sigmoid(zs[:, o:o + RW_G_LORA]).astype(BF16), g2_ref[...])))
        for st in sts:
            rows = pl.ds(st['c0'], L)
            kkn_s[rows, :] = (st['kkv'] / jnp.maximum(jnp.sqrt(st['ss']), 1e-12)).astype(BF16)
            g_s[rows, :] = st['g'].astype(g_s.dtype)
        return carry

    lax.fori_loop(0, n_ch // 2, shift_pair, 0)

    def prep(items):
        ones_bd = bd_ref[...]
        o = 3 * D_RW
        sts = []
        for c0, d in items:
            rows = pl.ds(c0, L)
            sts.append(dict(c0=c0, d=d, r=zs_s[rows, 0:D_RW].astype(F32), k=zs_s[rows, D_RW:2 * D_RW].astype(F32),
                            v=zs_s[rows, 2 * D_RW:3 * D_RW].astype(F32), kkn=kkn_s[rows, :].astype(F32),
                            wd=zs_s[rows, o + RW_G_LORA:o + RW_G_LORA + LANES].astype(F32),
                            ad=zs_s[rows, o + RW_G_LORA + LANES:o + RW_G_LORA + 2 * LANES]))
        for st in sts:
            d = st['d']
            tw_hi, tw_lo = _split2(jnp.tanh(st['wd']))
            st['w_lora'] = _dot(tw_hi, w2_ref[d, 0]) + (_dot(tw_hi, w2_ref[d, 1]) + _dot(tw_lo, w2_ref[d, 0]))
            st['a_lora'] = _dot(st['ad'], a2_ref[d])
        for st in sts:
            d = st['d']
            w_log = -_softplus(-(w0_ref[d:d + 1, :] + st['w_lora'])) - 0.5
            st['lw'] = -jnp.exp(w_log)
            st['a'] = _sigmoid(a0_ref[d:d + 1, :] + st['a_lora'])
            st['kd'] = st['k'] * (1.0 + (st['a'] - 1.0) * ka_ref[...])
            st['bon'] = _head_sum(st['r'] * st['kd'] * rk_ref[...], ones_bd, single_pass=True)
            st['cum'] = _dot_exact_lhs(tri_ref[d], st['lw'])
        out = []
        for st in sts:
            d, r, v, kd, kkn, a, cum = st['d'], st['r'], st['v'], st['kd'], st['kkn'], st['a'], st['cum']
            bon_s[d, pl.ds(st['c0'], L), :] = (st['bon'] * v).astype(bon_s.dtype)
            e_pos = jnp.exp(cum)
            e_neg = jnp.exp(-cum)
            rh = r * e_pos
            kh = kd * e_neg
            bh = (kkn * a) * e_neg
            ah = -kkn * jnp.exp(cum - st['lw'])
            last = L - 1 if d == 0 else 0
            gam = e_pos[last:last + 1, :]
            bhg = bh * gam
            khg = kh * gam
            chains = []
            for p in range(RW_PAIRS):
                cols = slice(p * LANES, (p + 1) * LANES)
                chains.append(dict(
                    d=d, p=p, gam=gam[:, cols], v_st=stack(v[:, cols]),
                    lhs=jnp.concatenate([stack(ah[:, cols]), stack(rh[:, cols])], axis=0),
                    rhs=jnp.concatenate([stack(bh[:, cols]), stack(kh[:, cols])], axis=0),
                    bk=jnp.concatenate([stack(bhg[:, cols]), stack(khg[:, cols])], axis=0)))
            out.append(chains)
        return out

    n_levels = int(math.log2(L))

    def state_free(chains):
        ri = lax.broadcasted_iota(jnp.int32, (L2, L2), 0)
        ci = lax.broadcasted_iota(jnp.int32, (L2, L2), 1)
        eye = jnp.where(ri == ci, 1.0, 0.0)
        rl = jnp.where(ri < L, ri, ri - L)
        cl = jnp.where(ci < L, ci, ci - L)
        strict = (cl < rl, cl > rl)
        incl = (cl <= rl, cl >= rl)
        level = ri ^ ci

        for ch in chains:
            ch['sc'] = _dot_nt(ch['lhs'], ch['rhs'])
        for ch in chains:
            sc, d = ch['sc'], ch['d']
            n_ab = jnp.where(strict[d], sc[0:L2, 0:L2], 0.0)
            m_ak = jnp.where(strict[d], sc[0:L2, L2:2 * L2], 0.0).astype(BF16)
            t_rb = jnp.where(incl[d], sc[L2:2 * L2, 0:L2], 0.0)
            t_rk = jnp.where(incl[d], sc[L2:2 * L2, L2:2 * L2], 0.0)
            ch['t_rbk'] = jnp.concatenate([t_rb, t_rk], axis=1).astype(BF16)
            ch['mv'] = _dot(m_ak, ch['v_st'])
            ch['t'] = eye + jnp.where(level == 1, n_ab, 0.0)
            ch['n_lv'] = [jnp.where(lax.shift_right_logical(level, lv) == 1, n_ab, 0.0).astype(BF16)
                          for lv in range(1, n_levels)]
            del ch['sc']
        for lv in range(n_levels - 1):
            for ch in chains:
                ch['tb'] = ch['t'].astype(BF16)
                ch['x'] = _dot(ch['n_lv'][lv], ch['tb']).astype(BF16)
            for ch in chains:
                ch['t'] = ch['t'] + _dot(ch['tb'], ch['x'])

    def state_step(chains, c0s):
        for ch in chains:
            ch['s_prev'] = s_s[ch['d'], ch['p']]
            ch['ar_s'] = _dot_nt(ch['lhs'], ch['s_prev'].astype(BF16))
        for ch in chains:
            u_rhs = ch['ar_s'][0:L2] + ch['mv']
            u_st = _dot(ch['t'].astype(BF16), u_rhs.astype(BF16)).astype(BF16)
            ch['uv'] = jnp.concatenate([u_st, ch['v_st']], axis=0)
        for ch in chains:
            y_st = ch['ar_s'][L2:2 * L2] + _dot(ch['t_rbk'], ch['uv'])
            ch['y'] = y_st[0:L] + y_st[L:L2]
            s_s[ch['d'], ch['p']] = ch['s_prev'] * ch['gam'] + _dot_tn(ch['uv'], ch['bk'])
        for d in range(2):
            y_dir = jnp.concatenate([ch['y'] for ch in chains if ch['d'] == d], axis=1)
            yd_s[d, pl.ds(c0s[d], L), :] = y_dir.astype(yd_s.dtype)

    steps = 2

    def body(i, carry):
        starts = []
        for j in range(steps):
            step = i * steps + j
            starts.append((pl.multiple_of(step * L, L), pl.multiple_of((n_ch - 1 - step) * L, L)))
        prepared = prep([(c0s[d], d) for c0s in starts for d in range(2)])
        groups = [(prepared[2 * j] + prepared[2 * j + 1], starts[j]) for j in range(steps)]
        state_free([ch for chains, _ in groups for ch in chains])
        for chains, c0s in groups:
            state_step(chains, c0s)
        return carry

    lax.fori_loop(0, n_ch // steps, body, 0)

    def epilogue(i, carry):
        c0 = pl.multiple_of(i * L, L)
        ones_bd = bd_ref[...]
        ys = yd_s[0, pl.ds(c0, L), :].astype(F32) + yd_s[1, pl.ds(c0, L), :].astype(F32)
        mu = _head_sum(ys, ones_bd) * (1.0 / RW_HEAD)
        yc = ys - mu
        var = _head_sum(yc * yc, ones_bd) * (1.0 / RW_HEAD)
        yn = yc * lax.rsqrt(var + RW_LN_EPS) * lnw_ref[...] + lnb_ref[...]
        bonus = bon_s[0, pl.ds(c0, L), :].astype(F32) + bon_s[1, pl.ds(c0, L), :].astype(F32)
        out = (yn + bonus) * g_s[pl.ds(c0, L), :].astype(F32)
        y_ref[pl.ds(c0, L), :] = out.astype(y_ref.dtype)
        return carry

    lax.fori_loop(0, n_ch, epilogue, 0, unroll=min(EPILOGUE_UNROLL, n_ch))
    for d in range(2):
        for p in range(RW_PAIRS):
            s_pair = s_s[d, p]
            sfin_ref[d, 2 * p] = s_pair[0:RW_HEAD, 0:RW_HEAD]
            sfin_ref[d, 2 * p + 1] = s_pair[RW_HEAD:, RW_HEAD:]


def _rwkv_mixer(z, s0, p, n_batch, t_len):
    L = SCAN_CHUNK
    c2 = lambda b: (0, 0)
    c3 = lambda b: (0, 0, 0)
    st_spec = pl.BlockSpec((None, 2, RW_HEADS, RW_HEAD, RW_HEAD), lambda b: (b, 0, 0, 0, 0))
    zero_init = s0 is None
    z_block_bytes = t_len * RW_ZCOLS * z.dtype.itemsize
    return pl.pallas_call(
        functools.partial(_rwkv_kernel, t_len=t_len, zero_init=zero_init),
        grid=(n_batch,),
        in_specs=[pl.BlockSpec((t_len, RW_ZCOLS), lambda b: (b, 0),
                               **(dict(pipeline_mode=pl.Buffered(1)) if z_block_bytes > Z_DOUBLE_BUFFER_MAX else {}))]
        + ([] if zero_init else [st_spec]) + [
            pl.BlockSpec((2, RW_ZCOLS), c2),
            pl.BlockSpec((1, D_RW), c2),
            pl.BlockSpec((1, D_RW), c2),
            pl.BlockSpec((1, D_RW), c2),
            pl.BlockSpec((RW_G_LORA, D_RW), c2),
            pl.BlockSpec((2, D_RW), c2),
            pl.BlockSpec((2, 2, LANES, D_RW), lambda b: (0, 0, 0, 0)),
            pl.BlockSpec((2, D_RW), c2),
            pl.BlockSpec((2, LANES, D_RW), c3),
            pl.BlockSpec((1, D_RW), c2),
            pl.BlockSpec((1, D_RW), c2),
            pl.BlockSpec((2, L, L), c3),
            pl.BlockSpec((LANES, LANES), c2),
        ],
        out_specs=[pl.BlockSpec((t_len, D_RW), lambda b: (b, 0)), st_spec],
        out_shape=[
            jax.ShapeDtypeStruct((n_batch * t_len, D_RW), Y_DTYPE),
            jax.ShapeDtypeStruct((n_batch, 2, RW_HEADS, RW_HEAD, RW_HEAD), F32),
        ],
        scratch_shapes=[
            pltpu.VMEM((2, L + 2 * HALO, RW_ZCOLS), F32),
            pltpu.VMEM((2, RW_PAIRS, LANES, LANES), F32),
            pltpu.VMEM((2, t_len, D_RW), BF16),
            pltpu.VMEM((2, t_len, D_RW), BF16),
            pltpu.VMEM((t_len, D_RW), BF16),
            pltpu.VMEM((t_len, RW_ZCOLS), BF16),
            pltpu.VMEM((t_len, D_RW), BF16),
        ],
        compiler_params=_ARB1,
    )(z, *(() if zero_init else (s0,)), p['mu'], p['kk'], p['ka'], p['rk'], p['g2'], p['w0'], p['w2'], p['a0'],
      p['a2'], p['ln_w'], p['ln_b'], p['tri'], p['ones_bd'])


def _mlstm_kernel(*refs, t_len, zero_init):
    L = ML_SCAN
    n_ch = t_len // L
    z_ref = refs[0]
    (cw_ref, cb_ref, gb_ref, nw_ref, tri_ref, esel_ref, shift_ref, ones_ref,
     y_ref, cfin_ref, nfin_ref, mfin_ref, c_s, n_s, m_s, hd_s, qk_s, gx_s, gr_s) = refs[1 if zero_init else 4:]
    if zero_init:
        c_s[...] = jnp.zeros(c_s.shape, F32)
        n_s[...] = jnp.zeros(n_s.shape, F32)
        m_s[...] = jnp.zeros(m_s.shape, F32)
    else:
        c0_ref, n0_ref, m0_ref = refs[1:4]
        c_s[...] = c0_ref[...]
        m_s[...] = m0_ref[...]
        for d in range(2):
            for h in range(ML_HEADS):
                n_s[d, h] = jnp.broadcast_to(n0_ref[d, h:h + 1, :], (ML_DK, ML_DK)).T
    q_scale = ML_DK ** -0.5
    g_lo = 4 * D_ML
    n_real = 4 * ML_HEADS

    def conv_chunk(i, carry):
        c0 = pl.multiple_of(i * L, L)
        gts = z_ref[pl.ds(c0, L), g_lo:g_lo + LANES].astype(F32) + gb_ref[...]
        x16 = gts.T[0:n_real, :]
        lf_hi, lf_lo = _split2(-_softplus(-x16))
        lane = lax.broadcasted_iota(jnp.int32, (8, L), 1)
        stats = []
        for d in range(2):
            b16 = _dot_nt(lf_hi, tri_ref[d]) + _dot_nt(lf_lo, tri_ref[d])
            x8 = x16[8 * d:8 * d + 8, :]
            b8 = b16[8 * d:8 * d + 8, :]
            g8 = pltpu.roll(x8, ML_HEADS, axis=0) - b8
            stats.append([g8, b8, g8])
        step = 1
        while step < L:
            for d in range(2):
                cm8 = stats[d][2]
                if d == 0:
                    sh = jnp.where(lane >= step, pltpu.roll(cm8, step, axis=1), -jnp.inf)
                else:
                    sh = jnp.where(lane < L - step, pltpu.roll(cm8, L - step, axis=1), -jnp.inf)
                stats[d][2] = jnp.maximum(cm8, sh)
            step *= 2

        ext = _load_ext(z_ref, c0, L, t_len, 0, 2 * D_ML, raw=True)
        taps = _dot(shift_ref[...], ext)
        qk = cb_ref[...] + ext[HALO:HALO + L].astype(F32) * cw_ref[CONV_LEFT:CONV_LEFT + 1, :]
        for n, j in enumerate(jj for jj in range(CONV_W) if jj != CONV_LEFT):
            qk = qk + taps[n * L:(n + 1) * L] * cw_ref[j:j + 1, :]
        qk = _silu(qk)
        qk_s[pl.ds(c0, L), 0:D_ML] = (qk[:, :D_ML] * q_scale).astype(BF16)
        qk_s[pl.ds(c0, L), D_ML:2 * D_ML] = qk[:, D_ML:].astype(BF16)

        for d, (g8, b8, cm8) in enumerate(stats):
            x32 = jnp.concatenate([g8, b8, cm8, jnp.zeros((8, L), F32)], axis=0)
            hi = x32.astype(BF16)
            r1 = x32 - hi.astype(F32)
            mid = r1.astype(BF16)
            lo = (r1 - mid.astype(F32)).astype(BF16)
            gx_s[d, i] = jnp.concatenate([hi, mid, lo], axis=0)
            gr_s[d, i] = g8
        return carry

    lax.fori_loop(0, n_ch, conv_chunk, 0, unroll=min(ML_UNROLL, n_ch))

    def prep(ci, d):
        c0 = pl.multiple_of(ci * L, L)
        q = qk_s[pl.ds(c0, L), 0:D_ML]
        k = qk_s[pl.ds(c0, L), D_ML:2 * D_ML]
        v = z_ref[pl.ds(c0, L), 2 * D_ML:3 * D_ML]
        g8 = gr_s[d, ci]
        cb = _dot_tn(gx_s[d, ci], esel_ref[...])
        ri = lax.broadcasted_iota(jnp.int32, (L, L), 0)
        ci = lax.broadcasted_iota(jnp.int32, (L, L), 1)
        causal = (ci <= ri) if d == 0 else (ci >= ri)
        last = L - 1 if d == 0 else 0
        ones = ones_ref[...]
        chains = []
        for h in range(ML_HEADS):
            cols = slice(h * ML_DK, (h + 1) * ML_DK)
            chains.append(dict(
                d=d, h=h, last=last, causal=causal,
                g_row=g8[ML_HEADS + h:ML_HEADS + h + 1, :],
                g_b=cb[:, cols], b_b=cb[:, D_ML + h * ML_DK:D_ML + (h + 1) * ML_DK],
                cm_b=cb[:, 2 * D_ML + h * ML_DK:2 * D_ML + (h + 1) * ML_DK],
                qb=q[:, cols], kb=k[:, cols],
                v=v[:, cols].astype(F32), v_ones=jnp.concatenate([v[:, cols].astype(BF16), ones], axis=1)))
        return chains

    def body(i, carry):
        c0s = (pl.multiple_of(i * L, L), pl.multiple_of((n_ch - 1 - i) * L, L))
        chains = prep(i, 0) + prep(n_ch - 1 - i, 1)
        for ch in chains:
            d, h = ch['d'], ch['h']
            ch['qk'] = _dot_nt(ch['qb'], ch['kb'])
            ch['c_prev'] = c_s[d, h]
            ch['n_prev'] = n_s[d, h]
            ch['m_prev'] = m_s[d, h:h + 1, :]
            ch['qcn'] = _dot(ch['qb'], jnp.concatenate([ch['c_prev'].astype(BF16),
                                                        ch['n_prev'].astype(BF16)], axis=1))
        for ch in chains:
            mu = jnp.maximum(ch['m_prev'], ch['cm_b'])
            ch['mu'] = mu
            ch['s'] = ch['qk'] * jnp.exp(jnp.where(ch['causal'], ch['g_row'] - mu, -jnp.inf))
            ch['w_inter'] = jnp.exp(ch['m_prev'] - mu)
        for ch in chains:
            last = ch['last']
            ch['sv'] = _dot(ch['s'].astype(BF16), ch['v_ones'])
            mu_last = ch['mu'][last:last + 1, :]
            ch['m_new'] = ch['b_b'][last:last + 1, :] + mu_last
            ch['dec'] = jnp.exp(ch['m_prev'] - mu_last)
            w_s = jnp.exp(ch['g_b'] - mu_last)
            ch['upd'] = _dot_tn(ch['kb'], jnp.concatenate([(w_s * ch['v']).astype(BF16),
                                                           w_s.astype(BF16)], axis=1))
        for ch in chains:
            d, h = ch['d'], ch['h']
            num = ch['sv'][:, :ML_DK] + ch['w_inter'] * ch['qcn'][:, :ML_DK]
            den = ch['sv'][:, ML_DK:] + ch['w_inter'] * ch['qcn'][:, ML_DK:]
            ch['out'] = num / jnp.maximum(jnp.abs(den), jnp.exp(-(ch['b_b'] + ch['mu'])))
            c_s[d, h] = ch['dec'] * ch['c_prev'] + ch['upd'][:, :ML_DK]
            n_s[d, h] = ch['dec'] * ch['n_prev'] + ch['upd'][:, ML_DK:]
            m_s[d, h:h + 1, :] = ch['m_new']
        for d in range(2):
            hd_s[d, pl.ds(c0s[d], L), :] = jnp.concatenate([ch['out'] for ch in chains if ch['d'] == d], axis=1)
        return carry

    lax.fori_loop(0, n_ch, body, 0, unroll=min(ML_UNROLL, n_ch))

    def epilogue(i, carry):
        c0 = pl.multiple_of(i * L, L)
        ones = ones_ref[...]
        hs = hd_s[0, pl.ds(c0, L), :] + hd_s[1, pl.ds(c0, L), :]
        o = z_ref[pl.ds(c0, L), 3 * D_ML:4 * D_ML].astype(F32)
        mu = _head_sum(hs, ones) * (1.0 / ML_DK)
        xc = hs - mu
        var = _head_sum(xc * xc, ones) * (1.0 / ML_DK)
        y = xc * lax.rsqrt(var + NORM_EPS) * nw_ref[...] * _sigmoid(o)
        y_ref[pl.ds(c0, L), :] = y.astype(y_ref.dtype)
        return carry

    lax.fori_loop(0, n_ch, epilogue, 0, unroll=min(EPILOGUE_UNROLL, n_ch))
    cfin_ref[...] = c_s[...]
    mfin_ref[...] = m_s[...]
    for d in range(2):
        for h in range(ML_HEADS):
            nfin_ref[d, h:h + 1, :] = n_s[d, h].T[0:1, :]


def _mlstm_mixer(z, states, p, n_batch, t_len):
    assert z.dtype == BF16
    L = ML_SCAN
    c2 = lambda b: (0, 0)
    c3 = lambda b: (0, 0, 0)
    c_spec = pl.BlockSpec((None, 2, ML_HEADS, ML_DK, ML_DK), lambda b: (b, 0, 0, 0, 0))
    m_spec = pl.BlockSpec((None, 2, ML_HEADS, ML_DK), lambda b: (b, 0, 0, 0))
    zero_init = states is None
    state_specs = [] if zero_init else [c_spec, m_spec, m_spec]
    return pl.pallas_call(
        functools.partial(_mlstm_kernel, t_len=t_len, zero_init=zero_init),
        grid=(n_batch,),
        in_specs=[pl.BlockSpec((t_len, ML_ZCOLS), lambda b: (b, 0))] + state_specs + [
            pl.BlockSpec((CONV_W, 2 * D_ML), c2),
            pl.BlockSpec((1, 2 * D_ML), c2),
            pl.BlockSpec((1, LANES), c2),
            pl.BlockSpec((1, D_ML), c2),
            pl.BlockSpec((2, L, L), c3),
            pl.BlockSpec(p['esel'].shape, c2),
            pl.BlockSpec(p['shift'].shape, c2),
            pl.BlockSpec((L, LANES), c2),
        ],
        out_specs=[pl.BlockSpec((t_len, D_ML), lambda b: (b, 0)), c_spec, m_spec, m_spec],
        out_shape=[
            jax.ShapeDtypeStruct((n_batch * t_len, D_ML), Y_DTYPE),
            jax.ShapeDtypeStruct((n_batch, 2, ML_HEADS, ML_DK, ML_DK), F32),
            jax.ShapeDtypeStruct((n_batch, 2, ML_HEADS, ML_DK), F32),
            jax.ShapeDtypeStruct((n_batch, 2, ML_HEADS, ML_DK), F32),
        ],
        scratch_shapes=[
            pltpu.VMEM((2, ML_HEADS, ML_DK, ML_DK), F32),
            pltpu.VMEM((2, ML_HEADS, ML_DK, ML_DK), F32),
            pltpu.VMEM((2, ML_HEADS, ML_DK), F32),
            pltpu.VMEM((2, t_len, D_ML), F32),
            pltpu.VMEM((t_len, 2 * D_ML), BF16),
            pltpu.VMEM((2, t_len // L, 96, L), BF16),
            pltpu.VMEM((2, t_len // L, 8, L), F32),
        ],
        compiler_params=_ARB1,
    )(z, *(() if zero_init else states), p['conv_w'], p['conv_b'], p['gate_b'], p['norm_w'], p['tri'], p['esel'],
      p['shift'], p['ones'])


def _conv_shift(n):
    taps = jnp.array([j for j in range(CONV_W) if j != CONV_LEFT])
    t = jnp.arange((CONV_W - 1) * n)
    src = HALO + t % n + taps[t // n] - CONV_LEFT
    return (src[:, None] == jnp.arange(n + 2 * HALO)[None, :]).astype(BF16)


def _mlstm_consts():
    L = ML_SCAN
    lower = jnp.tril(jnp.ones((L, L), F32))
    tri = jnp.stack([lower, lower.T]).astype(BF16)
    rows = jnp.arange(32)
    blk = jnp.arange(3 * D_ML) // ML_DK
    src_row = 8 * (blk // ML_HEADS) + ML_HEADS + blk % ML_HEADS
    esel = (rows[:, None] == src_row[None, :]).astype(BF16)
    esel = jnp.concatenate([esel, esel, esel], axis=0)
    return dict(tri=tri, esel=esel, shift=_conv_shift(L), ones=jnp.ones((L, LANES), BF16))


def _block_diag(w):
    n, c, d = w.shape
    return jnp.einsum('ncd,nm->ncmd', w, jnp.eye(n, dtype=w.dtype)).reshape(n * c, n * d)


def _scan_tri():
    L = SCAN_CHUNK
    lower = jnp.tril(jnp.ones((L, L), F32))
    return jnp.stack([lower, lower.T]).astype(BF16)


def _prep_even(ev_w_in, pool_w, pool_scale, rg_conv_w, rg_conv_b, rg_wa, rg_ba, rg_wx, rg_bx, rg_lam, ev_w_out, j):
    wg = jnp.concatenate([_block_diag(rg_wa[j, 0]), _block_diag(rg_wx[j, 0]),
                          _block_diag(rg_wa[j, 1]), _block_diag(rg_wx[j, 1])], axis=1)
    bg = jnp.concatenate([rg_ba[j, 0], rg_bx[j, 0], rg_ba[j, 1], rg_bx[j, 1]]).reshape(1, 4 * D_RG)
    return dict(
        w_in=[ev_w_in[j].astype(BF16)],
        w_out=ev_w_out[j].astype(BF16),
        pool_w=pool_w[j].astype(BF16),
        pool_scale=pool_scale[j].reshape(1, D_POOL),
        conv_w=rg_conv_w[j],
        conv_b=rg_conv_b[j].reshape(1, D_RG),
        wg=wg.astype(BF16),
        bg=bg,
        lam=rg_lam[j],
    )


def _pad_cols(w, cuts, widths):
    pieces = jnp.split(w, cuts, axis=-1)
    out = []
    for piece, width in zip(pieces, widths):
        extra = width - piece.shape[-1]
        if extra:
            piece = jnp.pad(piece, [(0, 0)] * (piece.ndim - 1) + [(0, extra)])
        out.append(piece)
    return jnp.concatenate(out, axis=-1)


def _prep_odd(od_w_in, rw_mu, rw_w0, rw_w2, rw_a0, rw_a2, rw_kk, rw_ka, rw_rk, rw_g2, rw_ln_w, rw_ln_b,
              ml_conv_w, ml_conv_b, ml_bi, ml_bf, ml_norm_w, od_w_out, j):
    rw_cols = 3 * D_RW + RW_G_LORA + RW_W_LORA + RW_A_LORA
    base = 3 * D_RW + RW_G_LORA
    cuts = [base, base + RW_W_LORA]
    widths = [base, LANES, LANES]
    w_rw = _pad_cols(od_w_in[j][:, :rw_cols], cuts, widths)
    w_ml = _pad_cols(od_w_in[j][:, rw_cols:], [4 * D_ML], [4 * D_ML, LANES])
    pad_rows = lambda w: jnp.pad(w, ((0, 0), (0, LANES - w.shape[1]), (0, 0)))
    gate_b = jnp.concatenate([ml_bi[j, 0], ml_bf[j, 0], ml_bi[j, 1], ml_bf[j, 1]])
    gate_b = jnp.pad(gate_b, (0, LANES - gate_b.shape[0])).reshape(1, LANES)
    hm = jnp.arange(LANES) // RW_HEAD
    rw = dict(
        mu=_pad_cols(rw_mu[j], cuts, widths),
        kk=rw_kk[j].reshape(1, D_RW), ka=rw_ka[j].reshape(1, D_RW), rk=rw_rk[j].reshape(1, D_RW),
        g2=rw_g2[j].astype(BF16),
        w0=rw_w0[j], w2=jnp.stack(_split2(pad_rows(rw_w2[j])), axis=1),
        a0=rw_a0[j], a2=pad_rows(rw_a2[j]).astype(BF16),
        ln_w=rw_ln_w[j].reshape(1, D_RW), ln_b=rw_ln_b[j].reshape(1, D_RW),
        tri=_scan_tri(),
        ones_bd=(hm[:, None] == hm[None, :]).astype(BF16),
    )
    ml = dict(
        conv_w=ml_conv_w[j], conv_b=ml_conv_b[j].reshape(1, 2 * D_ML),
        gate_b=gate_b, norm_w=ml_norm_w[j].reshape(1, D_ML), **_mlstm_consts(),
    )
    return dict(w_in=[w_rw.astype(BF16), w_ml.astype(BF16)], w_out=od_w_out[j].astype(BF16), rw=rw, ml=ml)


def _trunk(x, mod, rows_per_mod, mod_base, grid, states, params, n_batch, t_len):
    ev, od, dense = params
    rg0, rw0, ml0 = states
    x = x.reshape(n_batch * t_len, D_MODEL)
    spec = (rows_per_mod, mod_base)

    (z,) = _in_proj(x, mod[0], *spec, dense['norm1_w'][0], ev['w_in'])
    y_ev, rg_f = _even_mixer(z, rg0, ev, n_batch, t_len, grid)
    x = _out_mlp([y_ev], x, mod[0], *spec, dense['norm2_w'][0], ev['w_out'],
                 dense['mlp_w1'], dense['mlp_w2'], 0, dense['final_norm_w'], False)

    z_rw, z_ml = _in_proj(x, mod[1], *spec, dense['norm1_w'][1], od['w_in'])
    y_rw, s_f = _rwkv_mixer(z_rw, rw0, od['rw'], n_batch, t_len)
    y_ml, c_f, n_f, m_f = _mlstm_mixer(z_ml, ml0, od['ml'], n_batch, t_len)
    y = _out_mlp([y_rw, y_ml], x, mod[1], *spec, dense['norm2_w'][1], od['w_out'],
                 dense['mlp_w1'], dense['mlp_w2'], 1, dense['final_norm_w'], True)
    return y.reshape(n_batch, t_len, D_MODEL), (rg_f, s_f, c_f, n_f, m_f)


def kernel(x_prompt, x_sample, state_rglru, state_rwkv, state_mlstm_C, state_mlstm_n, state_mlstm_m,
           c, c_ctx, norm1_w, norm2_w, w_mod, b_mod, mlp_w1, mlp_w2, final_norm_w,
           ev_w_in, pool_w, pool_scale, rg_conv_w, rg_conv_b, rg_wa, rg_ba, rg_wx, rg_bx, rg_lam, ev_w_out,
           od_w_in, rw_mu, rw_w0, rw_w2, rw_a0, rw_a2, rw_kk, rw_ka, rw_rk, rw_g2, rw_ln_w, rw_ln_b,
           ml_conv_w, ml_conv_b, ml_bi, ml_bf, ml_norm_w, od_w_out):
    bp, tp, _ = x_prompt.shape
    bs, ts, _ = x_sample.shape

    assert bs + 1 <= N_COND_ROWS and ts % IN_ROW_BLOCK == 0 and (bp * tp) % IN_ROW_BLOCK == 0
    assert IN_ROW_BLOCK % ROW_BLOCK == 0
    cond = jnp.concatenate([c_ctx[None, :], c, jnp.zeros((N_COND_ROWS - 1 - bs, D_MODEL), F32)], axis=0)
    mod = _modulation(cond, w_mod, b_mod)

    ev = _prep_even(ev_w_in, pool_w, pool_scale, rg_conv_w, rg_conv_b, rg_wa, rg_ba, rg_wx, rg_bx, rg_lam,
                    ev_w_out, 0)
    od = _prep_odd(od_w_in, rw_mu, rw_w0, rw_w2, rw_a0, rw_a2, rw_kk, rw_ka, rw_rk, rw_g2, rw_ln_w, rw_ln_b,
                   ml_conv_w, ml_conv_b, ml_bi, ml_bf, ml_norm_w, od_w_out, 0)
    dense = dict(norm1_w=norm1_w, norm2_w=norm2_w, mlp_w1=mlp_w1.astype(BF16), mlp_w2=mlp_w2.astype(BF16),
                 final_norm_w=final_norm_w)
    params = (ev, od, dense)

    zero_states = (jnp.zeros((bp, 2, D_RG), F32), None, None)
    y_prompt, (rg_f, s_f, c_f, n_f, m_f) = _trunk(
        x_prompt, mod, bp * tp, 0, False, zero_states, params, bp, tp)

    sample_states = (
        state_rglru[:, 0],
        state_rwkv[:, 0],
        (state_mlstm_C[:, 0], state_mlstm_n[:, 0],
         jnp.broadcast_to(state_mlstm_m[:, 0][..., None], (bs, 2, ML_HEADS, ML_DK))),
    )
    y_sample, _ = _trunk(x_sample, mod, ts, 1, True, sample_states, params, bs, ts)

    return (y_prompt, y_sample,
            rg_f[:, None],
            s_f[:, None],
            c_f[:, None],
            n_f[:, None],
            m_f[:, None, :, :, 0])
```
